```python
import math
import jax, jax.numpy as jnp
from jax import lax
import numpy as np

D_MODEL = 1024
BATCH = 4
SEQ = 8192
DEPTH = 2

GRID_W = 64
CTX_LEN = 256
CHUNK = 64
SHORT_CONV = 3
GLA_HEADS = 4
GLA_DK = 128
GLA_DV = 256
GLA_GATE_RANK = 16
GLA_GATE_TAU = 16.0
SSD_HEADS = 16
SSD_HEADDIM = 64
SSD_STATE = 128
SSD_GROUPS = 2
DT_MIN = 0.001
DT_MAX = 0.1
CONF_CH = D_MODEL
CONF_KERNEL = 31
GDN_HEADS = 8
GDN_DK = 128
GDN_DV = 128
N_EXPERTS = 16
EC_CAPACITY = 2
EXPERT_FF = D_MODEL

GLA_QK = GLA_HEADS * GLA_DK
GLA_V = GLA_HEADS * GLA_DV
SSD_INNER = SSD_HEADS * SSD_HEADDIM
SSD_BC = SSD_GROUPS * SSD_STATE
GDN_QK = GDN_HEADS * GDN_DK
GDN_V = GDN_HEADS * GDN_DV
AB_SPLITS = (GLA_QK, GLA_QK, GLA_V, GLA_V, 2 * GLA_GATE_RANK, SSD_INNER, SSD_INNER, SSD_BC, SSD_BC, 2 * SSD_HEADS)
IN_AB = 2 * GLA_QK + 2 * GLA_V + 2 * GLA_GATE_RANK + 2 * SSD_INNER + 2 * SSD_BC + 2 * SSD_HEADS
AB_MIX = GLA_V + SSD_INNER
CD_SPLITS = (CONF_CH, CONF_CH, GDN_QK, GDN_QK, GDN_V, GDN_V, 2 * GDN_HEADS, 2 * GDN_HEADS)
IN_CD = 2 * CONF_CH + 2 * GDN_QK + 2 * GDN_V + 4 * GDN_HEADS
CD_MIX = CONF_CH + GDN_V

kernel_name = "hybrid_gla_ssd_conformer_gdn_ec_moe_prefix"

F32 = jnp.float32


def split_cols(a, sizes):
    return jnp.split(a, np.cumsum(sizes)[:-1].tolist(), axis=-1)


def rmsnorm(x, g, eps=1e-6):
    xf = x.astype(F32)
    y = xf * lax.rsqrt(jnp.mean(jnp.square(xf), axis=-1, keepdims=True) + eps)
    return (y * g.astype(F32)).astype(x.dtype)


def head_rmsnorm(x, g):
    return rmsnorm(x, g.reshape(x.shape[-2:]))


def layernorm(x, g, b, eps=1e-5):
    xf = x.astype(F32)
    mu = jnp.mean(xf, axis=-1, keepdims=True)
    var = jnp.mean(jnp.square(xf - mu), axis=-1, keepdims=True)
    return ((xf - mu) * lax.rsqrt(var + eps) * g.astype(F32) + b.astype(F32)).astype(x.dtype)


def l2norm(x, eps=1e-6):
    return x * lax.rsqrt(jnp.sum(jnp.square(x), axis=-1, keepdims=True) + eps)


def modulate(h, shift, scale):
    return h * (1 + scale) + shift


def dwconv(x, w, b=None):
    k, ch = w.shape
    pad = (k - 1) // 2
    y = lax.conv_general_dilated(x, w[:, None, :].astype(x.dtype), (1,), [(pad, pad)],
                                 dimension_numbers=('NWC', 'WIO', 'NWC'), feature_group_count=ch)
    return y if b is None else y + b.astype(x.dtype)


def gla_scan(q, k, v, logg, s0, with_out):
    bsz, nh, t, dk = q.shape
    dv = v.shape[-1]
    nc = t // CHUNK
    chunks = lambda z: z.reshape(bsz, nh, nc, CHUNK, z.shape[-1])
    q, k, v, logg = chunks(q), chunks(k), chunks(v), chunks(logg)
    gc = jnp.cumsum(logg, axis=3)
    glast = gc[:, :, :, -1]
    kd = k * jnp.exp(glast[:, :, :, None] - gc)
    front = lambda z: jnp.moveaxis(z, 2, 0)

    def advance(s, kd_c, v_c, gl_c):
        return jnp.exp(gl_c)[..., None] * s + jnp.einsum('bhjd,bhje->bhde', kd_c, v_c)

    xs = (front(kd), front(v), front(glast))
    if not with_out:
        s_fin, _ = lax.scan(lambda s, xc: (advance(s, *xc), None), s0, xs)
        return None, s_fin
    tril = jnp.tril(jnp.ones((CHUNK, CHUNK), dtype=bool))
    qg = q * jnp.exp(gc)
    att = jnp.where(tril, jnp.einsum('bhcid,bhcjd->bhcij', qg, k * jnp.exp(-gc)), 0.0)
    o_intra = jnp.einsum('bhcij,bhcje->bhcie', att, v)

    def step(s, xc):
        kd_c, v_c, gl_c, qg_c = xc
        return advance(s, kd_c, v_c, gl_c), jnp.einsum('bhid,bhde->bhie', qg_c, s)

    s_fin, o_inter = lax.scan(step, s0, xs + (front(qg),))
    o = o_intra + jnp.moveaxis(o_inter, 0, 2)
    return o.reshape(bsz, nh, t, dv), s_fin


def ssd_scan(v, loga, bm, cm, s0, with_out):
    bsz, t, nh, p = v.shape
    ng, n = bm.shape[2:]
    r = nh // ng
    nc = t // CHUNK
    v = v.reshape(bsz, nc, CHUNK, ng, r, p)
    acum = jnp.cumsum(loga.reshape(bsz, nc, CHUNK, ng, r), axis=2)
    bm = bm.reshape(bsz, nc, CHUNK, ng, n)
    cm = cm.reshape(bsz, nc, CHUNK, ng, n)
    alast = acum[:, :, -1]
    wgt = jnp.exp(alast[:, :, None] - acum)
    front = lambda z: jnp.moveaxis(z, 1, 0)

    def advance(s, bm_c, wgt_c, v_c, al_c):
        return (jnp.exp(al_c)[..., None, None] * s
                + jnp.einsum('bjgn,bjgr,bjgrp->bgrnp', bm_c, wgt_c, v_c))

    xs = (front(bm), front(wgt), front(v), front(alast))
    if not with_out:
        s_fin, _ = lax.scan(lambda s, xc: (advance(s, *xc), None), s0, xs)
        return None, s_fin
    tril = jnp.tril(jnp.ones((CHUNK, CHUNK), dtype=bool))[:, :, None, None]
    seg = jnp.exp(jnp.where(tril, acum[:, :, :, None] - acum[:, :, None], -jnp.inf))
    cb = jnp.einsum('bcign,bcjgn->bcijg', cm, bm)
    y_intra = jnp.einsum('bcijgr,bcjgrp->bcigrp', seg * cb[..., None], v)
    cdec = jnp.exp(acum)

    def step(s, xc):
        bm_c, wgt_c, v_c, al_c, cm_c, cd_c = xc
        y = jnp.einsum('bign,bgrnp->bigrp', cm_c, s) * cd_c[..., None]
        return advance(s, bm_c, wgt_c, v_c, al_c), y

    s_fin, y_inter = lax.scan(step, s0, xs + (front(cm), front(cdec)))
    y = y_intra + jnp.moveaxis(y_inter, 0, 1)
    return y.reshape(bsz, t, nh, p), s_fin


def gdn_scan(q, k, v, beta, logg, s0, with_out):
    bsz, nh, t, dk = q.shape
    dv = v.shape[-1]
    nc = t // CHUNK
    chunks = lambda z: z.reshape(bsz, nh, nc, CHUNK, *z.shape[3:])
    q, k, v, beta, logg = (chunks(z) for z in (q, k, v, beta, logg))
    gc = jnp.cumsum(logg, axis=-1)
    glast = gc[..., -1]
    tril = jnp.tril(jnp.ones((CHUNK, CHUNK), dtype=bool))
    strict = jnp.tril(jnp.ones((CHUNK, CHUNK), dtype=bool), k=-1)
    decay = jnp.exp(jnp.where(tril, gc[..., :, None] - gc[..., None, :], -jnp.inf))
    kb = k * beta[..., None]
    m = jnp.eye(CHUNK, dtype=k.dtype) + jnp.where(strict, jnp.einsum('bhcid,bhcjd->bhcij', kb, k) * decay, 0.0)
    rhs = jnp.concatenate([v * beta[..., None], kb * jnp.exp(gc)[..., None]], axis=-1)
    sol = lax.linalg.triangular_solve(m, rhs, left_side=True, lower=True, unit_diagonal=True)
    u, w = sol[..., :dv], sol[..., dv:]
    kd = k * jnp.exp(glast[..., None] - gc)[..., None]
    dec = jnp.exp(glast)
    front = lambda z: jnp.moveaxis(z, 2, 0)

    def advance(s, w_c, u_c, kd_c, dec_c):
        vn = u_c - jnp.einsum('bhid,bhde->bhie', w_c, s)
        return vn, dec_c[..., None, None] * s + jnp.einsum('bhjd,bhje->bhde', kd_c, vn)

    xs = (front(w), front(u), front(kd), front(dec))
    if not with_out:
        s_fin, _ = lax.scan(lambda s, xc: (advance(s, *xc)[1], None), s0, xs)
        return None, s_fin
    aqk = jnp.einsum('bhcid,bhcjd->bhcij', q, k) * decay
    qg = q * jnp.exp(gc)[..., None]

    def step(s, xc):
        w_c, u_c, kd_c, dec_c, aqk_c, qg_c = xc
        vn, s_new = advance(s, w_c, u_c, kd_c, dec_c)
        o = jnp.einsum('bhid,bhde->bhie', qg_c, s) + jnp.einsum('bhij,bhje->bhie', aqk_c, vn)
        return s_new, o

    s_fin, o = lax.scan(step, s0, xs + (front(aqk), front(qg)))
    return jnp.moveaxis(o, 0, 2).reshape(bsz, nh, t, dv), s_fin


def bidir_scan(scan_fn, args_f, args_b, t_axis, init_f, init_b, with_out):
    flip = lambda a: jnp.flip(a, axis=t_axis)
    o_f, s_f = scan_fn(*args_f, init_f, with_out)
    o_b, s_b = scan_fn(*[flip(a) for a in args_b], init_b, with_out)
    o = o_f + flip(o_b) if with_out else None
    return o, s_f, s_b


def ab_stream(h, w_in, b_in, gla_w_gate2, gla_b_gate2, ssd_conv_w, ssd_conv_b, ssd_dt_bias, ssd_a_log):
    bsz, t, _ = h.shape
    q, k, v, r, glr, z, xs, bm, cm, dt = split_cols(h @ w_in + b_in, AB_SPLITS)
    heads = lambda a, nh: a.astype(F32).reshape(bsz, t, nh, -1).transpose(0, 2, 1, 3)
    q = heads(q, GLA_HEADS) * GLA_DK ** -0.5
    k = heads(k, GLA_HEADS)
    v = heads(v, GLA_HEADS)
    gz = (jnp.einsum('btsr,srk->sbtk', glr.astype(F32).reshape(bsz, t, 2, GLA_GATE_RANK), gla_w_gate2.astype(F32))
          + gla_b_gate2.astype(F32)[:, None, None])
    logg = (jax.nn.log_sigmoid(gz) / GLA_GATE_TAU).reshape(2, bsz, t, GLA_HEADS, GLA_DK).transpose(0, 1, 3, 2, 4)
    xbc = jax.nn.silu(dwconv(jnp.concatenate([xs, bm, cm], axis=-1), ssd_conv_w, ssd_conv_b)).astype(F32)
    xs, bm, cm = split_cols(xbc, (SSD_INNER, SSD_BC, SSD_BC))
    xs = xs.reshape(bsz, t, SSD_HEADS, SSD_HEADDIM)
    bm = bm.reshape(bsz, t, SSD_GROUPS, SSD_STATE)
    cm = cm.reshape(bsz, t, SSD_GROUPS, SSD_STATE)
    dt = jax.nn.softplus(dt.astype(F32).reshape(bsz, t, 2, SSD_HEADS) + ssd_dt_bias.astype(F32))
    loga = -dt * jnp.exp(ssd_a_log.astype(F32))
    gla_f = (q, k, v, logg[0])
    gla_b = (q, k, v, logg[1])
    ssd_f = (xs * dt[:, :, 0, :, None], loga[:, :, 0], bm, cm)
    ssd_b = (xs * dt[:, :, 1, :, None], loga[:, :, 1], bm, cm)
    return gla_f, gla_b, ssd_f, ssd_b, (r, z, xs)


def ab_merge(o_gla, y_ssd, extras, w_out, b_out, gla_norm_g, ssd_d, ssd_norm_g, dtype):
    r, z, xs = extras
    bsz, t = r.shape[:2]
    o = head_rmsnorm(jnp.swapaxes(o_gla, 1, 2), gla_norm_g) * jax.nn.silu(r.astype(F32)).reshape(bsz, t, GLA_HEADS, GLA_DV)
    y = (y_ssd + ssd_d.astype(F32)[:, None] * xs).reshape(bsz, t, SSD_INNER) * jax.nn.silu(z.astype(F32))
    y = rmsnorm(y.reshape(bsz, t, SSD_GROUPS, -1), ssd_norm_g.reshape(SSD_GROUPS, -1))
    mixed = jnp.concatenate([o.reshape(bsz, t, GLA_V), y.reshape(bsz, t, SSD_INNER)], axis=-1).astype(dtype)
    return mixed @ w_out + b_out


def mixer_ab(h, hc, w_in, b_in, w_out, b_out, gla_w_gate2, gla_b_gate2, gla_norm_g,
             ssd_conv_w, ssd_conv_b, ssd_dt_bias, ssd_a_log, ssd_d, ssd_norm_g, need_ctx):
    stream = lambda s: ab_stream(s, w_in, b_in, gla_w_gate2, gla_b_gate2, ssd_conv_w, ssd_conv_b, ssd_dt_bias, ssd_a_log)
    cg_f, cg_b, cs_f, cs_b, c_ext = stream(hc)
    lg_f, lg_b, ls_f, ls_b, l_ext = stream(h)
    bsz = h.shape[0]
    zg = jnp.zeros((bsz, GLA_HEADS, GLA_DK, GLA_DV), F32)
    zs = jnp.zeros((bsz, SSD_GROUPS, SSD_HEADS // SSD_GROUPS, SSD_STATE, SSD_HEADDIM), F32)
    og_c, sg_f, sg_b = bidir_scan(gla_scan, cg_f, cg_b, 2, zg, zg, need_ctx)
    os_c, ss_f, ss_b = bidir_scan(ssd_scan, cs_f, cs_b, 1, zs, zs, need_ctx)
    og_l, _, _ = bidir_scan(gla_scan, lg_f, lg_b, 2, sg_f, sg_b, True)
    os_l, _, _ = bidir_scan(ssd_scan, ls_f, ls_b, 1, ss_f, ss_b, True)
    merge = lambda og, osd, ext: ab_merge(og, osd, ext, w_out, b_out, gla_norm_g, ssd_d, ssd_norm_g, h.dtype)
    out_l = merge(og_l, os_l, l_ext)
    out_c = merge(og_c, os_c, c_ext) if need_ctx else None
    return out_l, out_c


def conformer_conv(u, rows, dw_w, dw_b, ln_g, ln_b):
    bsz, t, ch = u.shape
    y = dwconv(u.reshape(bsz * rows, t // rows, ch), dw_w, dw_b).reshape(bsz, t, ch)
    return jax.nn.silu(layernorm(y, ln_g, ln_b))


def cd_stream(h, w_in, b_in, gdn_conv_w, gdn_a_log, gdn_dt_bias):
    bsz, t, _ = h.shape
    ga, gb, q, k, v, og, a_raw, b_raw = split_cols(h @ w_in + b_in, CD_SPLITS)
    glu = ga * jax.nn.sigmoid(gb)
    qkv = jax.nn.silu(dwconv(jnp.concatenate([q, k, v], axis=-1), gdn_conv_w)).astype(F32)
    q, k, v = split_cols(qkv, (GDN_QK, GDN_QK, GDN_V))
    heads = lambda a: a.reshape(bsz, t, GDN_HEADS, -1).transpose(0, 2, 1, 3)
    q = l2norm(heads(q)) * GDN_DK ** -0.5
    k = l2norm(heads(k))
    v = heads(v)
    beta = jax.nn.sigmoid(b_raw.astype(F32).reshape(bsz, t, 2, GDN_HEADS)).transpose(2, 0, 3, 1)
    logg = (-jnp.exp(gdn_a_log.astype(F32))
            * jax.nn.softplus(a_raw.astype(F32).reshape(bsz, t, 2, GDN_HEADS) + gdn_dt_bias.astype(F32))).transpose(2, 0, 3, 1)
    return (q, k, v, beta[0], logg[0]), (q, k, v, beta[1], logg[1]), glu, og


def mixer_cd(h, hc, rows, w_in, b_in, w_out, b_out, conf_dw_w, conf_dw_b, conf_ln_g, conf_ln_b,
             gdn_conv_w, gdn_a_log, gdn_dt_bias, gdn_norm_g, need_ctx):
    c_f, c_b, c_glu, c_og = cd_stream(hc, w_in, b_in, gdn_conv_w, gdn_a_log, gdn_dt_bias)
    l_f, l_b, l_glu, l_og = cd_stream(h, w_in, b_in, gdn_conv_w, gdn_a_log, gdn_dt_bias)
    bsz = h.shape[0]
    z0 = jnp.zeros((bsz, GDN_HEADS, GDN_DK, GDN_DV), F32)
    od_c, s_f, s_b = bidir_scan(gdn_scan, c_f, c_b, 2, z0, z0, need_ctx)
    od_l, _, _ = bidir_scan(gdn_scan, l_f, l_b, 2, s_f, s_b, True)

    def merge(od, glu, og, n_rows):
        t = glu.shape[1]
        conv = conformer_conv(glu, n_rows, conf_dw_w, conf_dw_b, conf_ln_g, conf_ln_b)
        o = head_rmsnorm(jnp.swapaxes(od, 1, 2), gdn_norm_g) * jax.nn.silu(og.astype(F32)).reshape(bsz, t, GDN_HEADS, GDN_DV)
        mixed = jnp.concatenate([conv, o.reshape(bsz, t, GDN_V).astype(conv.dtype)], axis=-1)
        return mixed @ w_out + b_out

    out_l = merge(od_l, l_glu, l_og, rows)
    out_c = merge(od_c, c_glu, c_og, 1) if need_ctx else None
    return out_l, out_c


def expert_choice_ffn(h, router_w, w1, w3, w2):
    bsz, n, d = h.shape
    cap = n * EC_CAPACITY // N_EXPERTS
    aff = jax.nn.softmax((h @ router_w).astype(F32), axis=-1)
    gates, idx = lax.top_k(jnp.swapaxes(aff, 1, 2), cap)
    idx = idx.reshape(bsz, N_EXPERTS * cap)
    xe = jnp.take_along_axis(h, idx[..., None], axis=1).reshape(bsz, N_EXPERTS, cap, d)
    hid = jax.nn.silu(jnp.einsum('becd,edf->becf', xe, w1)) * jnp.einsum('becd,edf->becf', xe, w3)
    ye = jnp.einsum('becf,efd->becd', hid, w2) * gates[..., None].astype(h.dtype)
    return jnp.zeros_like(h).at[jnp.arange(bsz)[:, None], idx].add(ye.reshape(bsz, N_EXPERTS * cap, d))


def setup_inputs(seed: int = 0) -> dict:
    key = jax.random.key(seed)
    ks = iter(jax.random.split(key, 48))

    def nrm(shape, scale):
        return jax.random.normal(next(ks), shape, jnp.float32) * scale

    def gain(shape):
        return 1.0 + nrm(shape, 0.1)

    def dt_bias(shape):
        u = jax.random.uniform(next(ks), shape, jnp.float32)
        dt = jnp.exp(u * (math.log(DT_MAX) - math.log(DT_MIN)) + math.log(DT_MIN))
        return dt + jnp.log(-jnp.expm1(-dt))

    def a_log(shape):
        return jnp.log(jax.random.uniform(next(ks), shape, jnp.float32, 1.0, 16.0))

    d = D_MODEL
    ne, no = (DEPTH + 1) // 2, DEPTH // 2
    return {
        "x": nrm((BATCH, SEQ, d), 1.0),
        "c": nrm((BATCH, d), 1.0),
        "ctx": nrm((BATCH, CTX_LEN, d), 1.0),
        "c_ctx": nrm((d,), 1.0),
        "mod_w": nrm((DEPTH, d, 6 * d), 0.5 * d ** -0.5),
        "mod_b": nrm((DEPTH, 6 * d), 0.02),
        "norm1_g": gain((DEPTH, d)),
        "norm2_g": gain((DEPTH, d)),
        "ab_w_in": nrm((ne, d, IN_AB), d ** -0.5),
        "ab_b_in": nrm((ne, IN_AB), 0.02),
        "ab_w_out": nrm((ne, AB_MIX, d), AB_MIX ** -0.5),
        "ab_b_out": nrm((ne, d), 0.02),
        "gla_w_gate2": nrm((ne, 2, GLA_GATE_RANK, GLA_QK), GLA_GATE_RANK ** -0.5),
        "gla_b_gate2": nrm((ne, 2, GLA_QK), 0.1),
        "gla_norm_g": gain((ne, GLA_V)),
        "ssd_conv_w": nrm((ne, SHORT_CONV, SSD_INNER + 2 * SSD_BC), SHORT_CONV ** -0.5),
        "ssd_conv_b": nrm((ne, SSD_INNER + 2 * SSD_BC), 0.02),
        "ssd_dt_bias": dt_bias((ne, 2, SSD_HEADS)),
        "ssd_a_log": a_log((ne, 2, SSD_HEADS)),
        "ssd_d": gain((ne, SSD_HEADS)),
        "ssd_norm_g": gain((ne, SSD_INNER)),
        "cd_w_in": nrm((no, d, IN_CD), d ** -0.5),
        "cd_b_in": nrm((no, IN_CD), 0.02),
        "cd_w_out": nrm((no, CD_MIX, d), CD_MIX ** -0.5),
        "cd_b_out": nrm((no, d), 0.02),
        "conf_dw_w": nrm((no, CONF_KERNEL, CONF_CH), CONF_KERNEL ** -0.5),
        "conf_dw_b": nrm((no, CONF_CH), 0.02),
        "conf_ln_g": gain((no, CONF_CH)),
        "conf_ln_b": nrm((no, CONF_CH), 0.02),
        "gdn_conv_w": nrm((no, SHORT_CONV, 2 * GDN_QK + GDN_V), SHORT_CONV ** -0.5),
        "gdn_a_log": a_log((no, 2, GDN_HEADS)),
        "gdn_dt_bias": dt_bias((no, 2, GDN_HEADS)),
        "gdn_norm_g": gain((no, GDN_V)),
        "moe_router": nrm((DEPTH, d, N_EXPERTS), d ** -0.5),
        "moe_w1": nrm((DEPTH, N_EXPERTS, d, EXPERT_FF), d ** -0.5),
        "moe_w3": nrm((DEPTH, N_EXPERTS, d, EXPERT_FF), d ** -0.5),
        "moe_w2": nrm((DEPTH, N_EXPERTS, EXPERT_FF, d), EXPERT_FF ** -0.5),
        "final_norm_g": gain((d,)),
    }


def reference(x, c, ctx, c_ctx, mod_w, mod_b, norm1_g, norm2_g,
              ab_w_in, ab_b_in, ab_w_out, ab_b_out, gla_w_gate2, gla_b_gate2, gla_norm_g,
              ssd_conv_w, ssd_conv_b, ssd_dt_bias, ssd_a_log, ssd_d, ssd_norm_g,
              cd_w_in, cd_b_in, cd_w_out, cd_b_out, conf_dw_w, conf_dw_b, conf_ln_g, conf_ln_b,
              gdn_conv_w, gdn_a_log, gdn_dt_bias, gdn_norm_g,
              moe_router, moe_w1, moe_w3, moe_w2, final_norm_g):
    rows = x.shape[1] // GRID_W
    xc = ctx
    for i in range(DEPTH):
        last = i == DEPTH - 1
        j = i // 2
        sh1, sc1, g1, sh2, sc2, g2 = (m[:, None, :] for m in
                                      jnp.split(jax.nn.silu(c) @ mod_w[i] + mod_b[i], 6, axis=-1))
        sh1c, sc1c, g1c, sh2c, sc2c, g2c = jnp.split(jax.nn.silu(c_ctx) @ mod_w[i] + mod_b[i], 6, axis=-1)
        h = modulate(rmsnorm(x, norm1_g[i]), sh1, sc1)
        hc = modulate(rmsnorm(xc, norm1_g[i]), sh1c, sc1c)
        if i % 2 == 0:
            m, mc = mixer_ab(h, hc, ab_w_in[j], ab_b_in[j], ab_w_out[j], ab_b_out[j],
                             gla_w_gate2[j], gla_b_gate2[j], gla_norm_g[j],
                             ssd_conv_w[j], ssd_conv_b[j], ssd_dt_bias[j], ssd_a_log[j], ssd_d[j], ssd_norm_g[j],
                             not last)
        else:
            m, mc = mixer_cd(h, hc, rows, cd_w_in[j], cd_b_in[j], cd_w_out[j], cd_b_out[j],
                             conf_dw_w[j], conf_dw_b[j], conf_ln_g[j], conf_ln_b[j],
                             gdn_conv_w[j], gdn_a_log[j], gdn_dt_bias[j], gdn_norm_g[j], not last)
        x = x + g1 * m
        h = modulate(rmsnorm(x, norm2_g[i]), sh2, sc2)
        x = x + g2 * expert_choice_ffn(h, moe_router[i], moe_w1[i], moe_w3[i], moe_w2[i])
        if not last:
            xc = xc + g1c * mc
            hc = modulate(rmsnorm(xc, norm2_g[i]), sh2c, sc2c)
            xc = xc + g2c * expert_choice_ffn(hc, moe_router[i], moe_w1[i], moe_w3[i], moe_w2[i])
    return rmsnorm(x, final_norm_g)
```

```python
import functools
import math

import jax
import jax.numpy as jnp
import numpy as np
from jax import lax
from jax.experimental import pallas as pl
from jax.experimental.pallas import tpu as pltpu

F32 = jnp.float32
BF16 = jnp.bfloat16

D_MODEL = 1024
GRID_W = 64
CHUNK = 64
GLA_HEADS, GLA_DK, GLA_DV, GLA_GATE_RANK, GLA_GATE_TAU = 4, 128, 256, 16, 16.0
SSD_HEADS, SSD_HEADDIM, SSD_STATE, SSD_GROUPS = 16, 64, 128, 2
CONF_CH, CONF_KERNEL = D_MODEL, 31
GDN_HEADS, GDN_DK, GDN_DV = 8, 128, 128
N_EXPERTS, EC_CAPACITY, EXPERT_FF = 16, 2, D_MODEL

GLA_QK = GLA_HEADS * GLA_DK
GLA_V = GLA_HEADS * GLA_DV
SSD_INNER = SSD_HEADS * SSD_HEADDIM
SSD_BC = SSD_GROUPS * SSD_STATE
GDN_QK = GDN_HEADS * GDN_DK
GDN_V = GDN_HEADS * GDN_DV
AB_SPLITS = (GLA_QK, GLA_QK, GLA_V, GLA_V, 2 * GLA_GATE_RANK, SSD_INNER, SSD_INNER, SSD_BC, SSD_BC, 2 * SSD_HEADS)
CD_SPLITS = (CONF_CH, CONF_CH, GDN_QK, GDN_QK, GDN_V, GDN_V, 2 * GDN_HEADS, 2 * GDN_HEADS)

LANE = 128
VMEM_LIMIT = 48 * 1024 * 1024


def _split_cols(a, sizes):
    return jnp.split(a, np.cumsum(sizes)[:-1].tolist(), axis=-1)


def _pad_cols(a, n):
    return jnp.pad(a, [(0, 0)] * (a.ndim - 1) + [(0, n - a.shape[-1])])


def _norm_proj_kernel(x_ref, g_ref, sh_ref, sc_ref, w_ref, b_ref, o_ref, h_ref):
    @pl.when(pl.program_id(2) == 0)
    def _():
        x = x_ref[0]
        ms = jnp.mean(x * x, axis=-1, keepdims=True)
        y = x * lax.rsqrt(ms + 1e-6) * g_ref[...]
        h_ref[...] = (y * (1.0 + sc_ref[0]) + sh_ref[0]).astype(BF16)

    o_ref[0] = jnp.dot(h_ref[...], w_ref[...], preferred_element_type=F32) + b_ref[...]


def norm_proj(x, g, shift, scale, w, b, tn):
    bsz, t, d = x.shape
    n = w.shape[1]
    tm = min(t, 512)
    assert t % tm == 0 and n % tn == 0
    return pl.pallas_call(
        _norm_proj_kernel,
        grid=(bsz, t // tm, n // tn),
        in_specs=[
            pl.BlockSpec((1, tm, d), lambda i, j, k: (i, j, 0)),
            pl.BlockSpec((1, d), lambda i, j, k: (0, 0)),
            pl.BlockSpec((1, 1, d), lambda i, j, k: (i, 0, 0)),
            pl.BlockSpec((1, 1, d), lambda i, j, k: (i, 0, 0)),
            pl.BlockSpec((d, tn), lambda i, j, k: (0, k)),
            pl.BlockSpec((1, tn), lambda i, j, k: (0, k)),
        ],
        out_specs=pl.BlockSpec((1, tm, tn), lambda i, j, k: (i, j, k)),
        out_shape=jax.ShapeDtypeStruct((bsz, t, n), F32),
        scratch_shapes=[pltpu.VMEM((tm, d), BF16)],
        compiler_params=pltpu.CompilerParams(
            dimension_semantics=("parallel", "parallel", "arbitrary"), vmem_limit_bytes=VMEM_LIMIT),
        name="norm_proj",
    )(x, g.reshape(1, d), shift.reshape(bsz, 1, d), scale.reshape(bsz, 1, d), w, b.reshape(1, n))


def _out_proj_kernel(m_ref, w_ref, b_ref, x_ref, gate_ref, o_ref):
    y = jnp.dot(m_ref[0].astype(BF16), w_ref[...], preferred_element_type=F32) + b_ref[...]
    o_ref[0] = x_ref[0] + gate_ref[0] * y


def out_proj_residual(mixed, w, b, x, gate):
    bsz, t, k = mixed.shape
    d = w.shape[1]
    tm = min(t, 512)
    return pl.pallas_call(
        _out_proj_kernel,
        grid=(bsz, t // tm),
        in_specs=[
            pl.BlockSpec((1, tm, k), lambda i, j: (i, j, 0)),
            pl.BlockSpec((k, d), lambda i, j: (0, 0)),
            pl.BlockSpec((1, d), lambda i, j: (0, 0)),
            pl.BlockSpec((1, tm, d), lambda i, j: (i, j, 0)),
            pl.BlockSpec((1, 1, d), lambda i, j: (i, 0, 0)),
        ],
        out_specs=pl.BlockSpec((1, tm, d), lambda i, j: (i, j, 0)),
        out_shape=jax.ShapeDtypeStruct((bsz, t, d), F32),
        compiler_params=pltpu.CompilerParams(
            dimension_semantics=("parallel", "parallel"), vmem_limit_bytes=VMEM_LIMIT),
        name="out_proj",
    )(mixed, w, b.reshape(1, d), x, gate.reshape(bsz, 1, d))


def _expert_ffn_kernel(x_ref, w1_ref, w3_ref, w2_ref, o_ref):
    x = x_ref[0, 0].astype(BF16)
    a = jnp.dot(x, w1_ref[0], preferred_element_type=F32)
    g = jnp.dot(x, w3_ref[0], preferred_element_type=F32)
    hid = (a * jax.nn.sigmoid(a) * g).astype(BF16)
    o_ref[0, 0] = jnp.dot(hid, w2_ref[0], preferred_element_type=F32)


def expert_ffn(xe, w1, w3, w2):
    bsz, ne, cap, d = xe.shape
    f = w1.shape[2]
    tm = min(cap, 512)
    return pl.pallas_call(
        _expert_ffn_kernel,
        grid=(ne, bsz, cap // tm),
        in_specs=[
            pl.BlockSpec((1, 1, tm, d), lambda e, i, j: (i, e, j, 0)),
            pl.BlockSpec((1, d, f), lambda e, i, j: (e, 0, 0)),
            pl.BlockSpec((1, d, f), lambda e, i, j: (e, 0, 0)),
            pl.BlockSpec((1, f, d), lambda e, i, j: (e, 0, 0)),
        ],
        out_specs=pl.BlockSpec((1, 1, tm, d), lambda e, i, j: (i, e, j, 0)),
        out_shape=jax.ShapeDtypeStruct((bsz, ne, cap, d), F32),
        compiler_params=pltpu.CompilerParams(
            dimension_semantics=("parallel", "parallel", "parallel"), vmem_limit_bytes=VMEM_LIMIT),
        name="expert_ffn",
    )(xe, w1, w3, w2)


def _rmsnorm_kernel(x_ref, g_ref, o_ref):
    x = x_ref[0]
    ms = jnp.mean(x * x, axis=-1, keepdims=True)
    o_ref[0] = x * lax.rsqrt(ms + 1e-6) * g_ref[...]


def rmsnorm_rows(x, g):
    bsz, t, d = x.shape
    tm = min(t, 1024)
    return pl.pallas_call(
        _rmsnorm_kernel,
        grid=(bsz, t // tm),
        in_specs=[pl.BlockSpec((1, tm, d), lambda i, j: (i, j, 0)), pl.BlockSpec((1, d), lambda i, j: (0, 0))],
        out_specs=pl.BlockSpec((1, tm, d), lambda i, j: (i, j, 0)),
        out_shape=jax.ShapeDtypeStruct((bsz, t, d), F32),
        compiler_params=pltpu.CompilerParams(dimension_semantics=("parallel", "parallel")),
        name="final_rmsnorm",
    )(x, g.reshape(1, d))


def _rmsnorm(x, g, eps=1e-6):
    return x * lax.rsqrt(jnp.mean(jnp.square(x), axis=-1, keepdims=True) + eps) * g


def _head_rmsnorm(x, g):
    return _rmsnorm(x, g.reshape(x.shape[-2:]))


def _layernorm(x, g, b, eps=1e-5):
    mu = jnp.mean(x, axis=-1, keepdims=True)
    var = jnp.mean(jnp.square(x - mu), axis=-1, keepdims=True)
    return (x - mu) * lax.rsqrt(var + eps) * g + b


def _l2norm(x, eps=1e-6):
    return x * lax.rsqrt(jnp.sum(jnp.square(x), axis=-1, keepdims=True) + eps)


def _dwconv(x, w, b=None):
    k, ch = w.shape
    pad = (k - 1) // 2
    y = lax.conv_general_dilated(x, w[:, None, :], (1,), [(pad, pad)],
                                 dimension_numbers=('NWC', 'WIO', 'NWC'), feature_group_count=ch)
    return y if b is None else y + b


def _gla_scan(q, k, v, logg, s0, with_out):
    bsz, nh, t, dk = q.shape
    dv = v.shape[-1]
    nc = t // CHUNK
    chunks = lambda z: z.reshape(bsz, nh, nc, CHUNK, z.shape[-1])
    q, k, v, logg = chunks(q), chunks(k), chunks(v), chunks(logg)
    gc = jnp.cumsum(logg, axis=3)
    glast = gc[:, :, :, -1]
    kd = k * jnp.exp(glast[:, :, :, None] - gc)
    front = lambda z: jnp.moveaxis(z, 2, 0)

    def advance(s, kd_c, v_c, gl_c):
        return jnp.exp(gl_c)[..., None] * s + jnp.einsum('bhjd,bhje->bhde', kd_c, v_c)

    xs = (front(kd), front(v), front(glast))
    if not with_out:
        s_fin, _ = lax.scan(lambda s, xc: (advance(s, *xc), None), s0, xs)
        return None, s_fin
    tril = jnp.tril(jnp.ones((CHUNK, CHUNK), dtype=bool))
    qg = q * jnp.exp(gc)
    att = jnp.where(tril, jnp.einsum('bhcid,bhcjd->bhcij', qg, k * jnp.exp(-gc)), 0.0)
    o_intra = jnp.einsum('bhcij,bhcje->bhcie', att, v)

    def step(s, xc):
        kd_c, v_c, gl_c, qg_c = xc
        return advance(s, kd_c, v_c, gl_c), jnp.einsum('bhid,bhde->bhie', qg_c, s)

    s_fin, o_inter = lax.scan(step, s0, xs + (front(qg),))
    o = o_intra + jnp.moveaxis(o_inter, 0, 2)
    return o.reshape(bsz, nh, t, dv), s_fin


def _ssd_scan(v, loga, bm, cm, s0, with_out):
    bsz, t, nh, p = v.shape
    ng, n = bm.shape[2:]
    r = nh // ng
    nc = t // CHUNK
    v = v.reshape(bsz, nc, CHUNK, ng, r, p)
    acum = jnp.cumsum(loga.reshape(bsz, nc, CHUNK, ng, r), axis=2)
    bm = bm.reshape(bsz, nc, CHUNK, ng, n)
    cm = cm.reshape(bsz, nc, CHUNK, ng, n)
    alast = acum[:, :, -1]
    wgt = jnp.exp(alast[:, :, None] - acum)
    front = lambda z: jnp.moveaxis(z, 1, 0)

    def advance(s, bm_c, wgt_c, v_c, al_c):
        return (jnp.exp(al_c)[..., None, None] * s
                + jnp.einsum('bjgn,bjgr,bjgrp->bgrnp', bm_c, wgt_c, v_c))

    xs = (front(bm), front(wgt), front(v), front(alast))
    if not with_out:
        s_fin, _ = lax.scan(lambda s, xc: (advance(s, *xc), None), s0, xs)
        return None, s_fin
    tril = jnp.tril(jnp.ones((CHUNK, CHUNK), dtype=bool))[:, :, None, None]
    seg = jnp.exp(jnp.where(tril, acum[:, :, :, None] - acum[:, :, None], -jnp.inf))
    cb = jnp.einsum('bcign,bcjgn->bcijg', cm, bm)
    y_intra = jnp.einsum('bcijgr,bcjgrp->bcigrp', seg * cb[..., None], v)
    cdec = jnp.exp(acum)

    def step(s, xc):
        bm_c, wgt_c, v_c, al_c, cm_c, cd_c = xc
        y = jnp.einsum('bign,bgrnp->bigrp', cm_c, s) * cd_c[..., None]
        return advance(s, bm_c, wgt_c, v_c, al_c), y

    s_fin, y_inter = lax.scan(step, s0, xs + (front(cm), front(cdec)))
    y = y_intra + jnp.moveaxis(y_inter, 0, 1)
    return y.reshape(bsz, t, nh, p), s_fin


def _gdn_scan(q, k, v, beta, logg, s0, with_out):
    bsz, nh, t, dk = q.shape
    dv = v.shape[-1]
    nc = t // CHUNK
    chunks = lambda z: z.reshape(bsz, nh, nc, CHUNK, *z.shape[3:])
    q, k, v, beta, logg = (chunks(z) for z in (q, k, v, beta, logg))
    gc = jnp.cumsum(logg, axis=-1)
    glast = gc[..., -1]
    tril = jnp.tril(jnp.ones((CHUNK, CHUNK), dtype=bool))
    strict = jnp.tril(jnp.ones((CHUNK, CHUNK), dtype=bool), k=-1)
    decay = jnp.exp(jnp.where(tril, gc[..., :, None] - gc[..., None, :], -jnp.inf))
    kb = k * beta[..., None]
    m = jnp.eye(CHUNK, dtype=k.dtype) + jnp.where(strict, jnp.einsum('bhcid,bhcjd->bhcij', kb, k) * decay, 0.0)
    rhs = jnp.concatenate([v * beta[..., None], kb * jnp.exp(gc)[..., None]], axis=-1)
    sol = lax.linalg.triangular_solve(m, rhs, left_side=True, lower=True, unit_diagonal=True)
    u, w = sol[..., :dv], sol[..., dv:]
    kd = k * jnp.exp(glast[..., None] - gc)[..., None]
    dec = jnp.exp(glast)
    front = lambda z: jnp.moveaxis(z, 2, 0)

    def advance(s, w_c, u_c, kd_c, dec_c):
        vn = u_c - jnp.einsum('bhid,bhde->bhie', w_c, s)
        return vn, dec_c[..., None, None] * s + jnp.einsum('bhjd,bhje->bhde', kd_c, vn)

    xs = (front(w), front(u), front(kd), front(dec))
    if not with_out:
        s_fin, _ = lax.scan(lambda s, xc: (advance(s, *xc)[1], None), s0, xs)
        return None, s_fin
    aqk = jnp.einsum('bhcid,bhcjd->bhcij', q, k) * decay
    qg = q * jnp.exp(gc)[..., None]

    def step(s, xc):
        w_c, u_c, kd_c, dec_c, aqk_c, qg_c = xc
        vn, s_new = advance(s, w_c, u_c, kd_c, dec_c)
        o = jnp.einsum('bhid,bhde->bhie', qg_c, s) + jnp.einsum('bhij,bhje->bhie', aqk_c, vn)
        return s_new, o

    s_fin, o = lax.scan(step, s0, xs + (front(aqk), front(qg)))
    return jnp.moveaxis(o, 0, 2).reshape(bsz, nh, t, dv), s_fin


def _bidir_scan(scan_fn, args_f, args_b, t_axis, init_f, init_b, with_out):
    flip = lambda a: jnp.flip(a, axis=t_axis)
    o_f, s_f = scan_fn(*args_f, init_f, with_out)
    o_b, s_b = scan_fn(*[flip(a) for a in args_b], init_b, with_out)
    o = o_f + flip(o_b) if with_out else None
    return o, s_f, s_b


def _ab_stream(proj, gla_w_gate2, gla_b_gate2, ssd_conv_w, ssd_conv_b, ssd_dt_bias, ssd_a_log):
    bsz, t, _ = proj.shape
    q, k, v, r, glr, z, xs, bm, cm, dt = _split_cols(proj, AB_SPLITS)
    heads = lambda a, nh: a.reshape(bsz, t, nh, -1).transpose(0, 2, 1, 3)
    q = heads(q, GLA_HEADS) * GLA_DK ** -0.5
    k = heads(k, GLA_HEADS)
    v = heads(v, GLA_HEADS)
    gz = (jnp.einsum('btsr,srk->sbtk', glr.reshape(bsz, t, 2, GLA_GATE_RANK), gla_w_gate2)
          + gla_b_gate2[:, None, None])
    logg = (jax.nn.log_sigmoid(gz) / GLA_GATE_TAU).reshape(2, bsz, t, GLA_HEADS, GLA_DK).transpose(0, 1, 3, 2, 4)
    xbc = jax.nn.silu(_dwconv(jnp.concatenate([xs, bm, cm], axis=-1), ssd_conv_w, ssd_conv_b))
    xs, bm, cm = _split_cols(xbc, (SSD_INNER, SSD_BC, SSD_BC))
    xs = xs.reshape(bsz, t, SSD_HEADS, SSD_HEADDIM)
    bm = bm.reshape(bsz, t, SSD_GROUPS, SSD_STATE)
    cm = cm.reshape(bsz, t, SSD_GROUPS, SSD_STATE)
    dt = jax.nn.softplus(dt.reshape(bsz, t, 2, SSD_HEADS) + ssd_dt_bias)
    loga = -dt * jnp.exp(ssd_a_log)
    gla_f = (q, k, v, logg[0])
    gla_b = (q, k, v, logg[1])
    ssd_f = (xs * dt[:, :, 0, :, None], loga[:, :, 0], bm, cm)
    ssd_b = (xs * dt[:, :, 1, :, None], loga[:, :, 1], bm, cm)
    return gla_f, gla_b, ssd_f, ssd_b, (r, z, xs)


def _ab_mix(o_gla, y_ssd, extras, gla_norm_g, ssd_d, ssd_norm_g):
    r, z, xs = extras
    bsz, t = r.shape[:2]
    o = _head_rmsnorm(jnp.swapaxes(o_gla, 1, 2), gla_norm_g) * jax.nn.silu(r).reshape(bsz, t, GLA_HEADS, GLA_DV)
    y = (y_ssd + ssd_d[:, None] * xs).reshape(bsz, t, SSD_INNER) * jax.nn.silu(z)
    y = _rmsnorm(y.reshape(bsz, t, SSD_GROUPS, -1), ssd_norm_g.reshape(SSD_GROUPS, -1))
    return jnp.concatenate([o.reshape(bsz, t, GLA_V), y.reshape(bsz, t, SSD_INNER)], axis=-1)


def _mixer_ab(proj_l, proj_c, gla_w_gate2, gla_b_gate2, gla_norm_g,
              ssd_conv_w, ssd_conv_b, ssd_dt_bias, ssd_a_log, ssd_d, ssd_norm_g, need_ctx):
    stream = lambda s: _ab_stream(s, gla_w_gate2, gla_b_gate2, ssd_conv_w, ssd_conv_b, ssd_dt_bias, ssd_a_log)
    cg_f, cg_b, cs_f, cs_b, c_ext = stream(proj_c)
    lg_f, lg_b, ls_f, ls_b, l_ext = stream(proj_l)
    bsz = proj_l.shape[0]
    zg = jnp.zeros((bsz, GLA_HEADS, GLA_DK, GLA_DV), F32)
    zs = jnp.zeros((bsz, SSD_GROUPS, SSD_HEADS // SSD_GROUPS, SSD_STATE, SSD_HEADDIM), F32)
    og_c, sg_f, sg_b = _bidir_scan(_gla_scan, cg_f, cg_b, 2, zg, zg, need_ctx)
    os_c, ss_f, ss_b = _bidir_scan(_ssd_scan, cs_f, cs_b, 1, zs, zs, need_ctx)
    og_l, _, _ = _bidir_scan(_gla_scan, lg_f, lg_b, 2, sg_f, sg_b, True)
    os_l, _, _ = _bidir_scan(_ssd_scan, ls_f, ls_b, 1, ss_f, ss_b, True)
    mix = lambda og, osd, ext: _ab_mix(og, osd, ext, gla_norm_g, ssd_d, ssd_norm_g)
    return mix(og_l, os_l, l_ext), (mix(og_c, os_c, c_ext) if need_ctx else None)


def _conformer_conv(u, rows, dw_w, dw_b, ln_g, ln_b):
    bsz, t, ch = u.shape
    y = _dwconv(u.reshape(bsz * rows, t // rows, ch), dw_w, dw_b).reshape(bsz, t, ch)
    return jax.nn.silu(_layernorm(y, ln_g, ln_b))


def _cd_stream(proj, gdn_conv_w, gdn_a_log, gdn_dt_bias):
    bsz, t, _ = proj.shape
    ga, gb, q, k, v, og, a_raw, b_raw = _split_cols(proj, CD_SPLITS)
    glu = ga * jax.nn.sigmoid(gb)
    qkv = jax.nn.silu(_dwconv(jnp.concatenate([q, k, v], axis=-1), gdn_conv_w))
    q, k, v = _split_cols(qkv, (GDN_QK, GDN_QK, GDN_V))
    heads = lambda a: a.reshape(bsz, t, GDN_HEADS, -1).transpose(0, 2, 1, 3)
    q = _l2norm(heads(q)) * GDN_DK ** -0.5
    k = _l2norm(heads(k))
    v = heads(v)
    beta = jax.nn.sigmoid(b_raw.reshape(bsz, t, 2, GDN_HEADS)).transpose(2, 0, 3, 1)
    logg = (-jnp.exp(gdn_a_log)
            * jax.nn.softplus(a_raw.reshape(bsz, t, 2, GDN_HEADS) + gdn_dt_bias)).transpose(2, 0, 3, 1)
    return (q, k, v, beta[0], logg[0]), (q, k, v, beta[1], logg[1]), glu, og


def _mixer_cd(proj_l, proj_c, rows, conf_dw_w, conf_dw_b, conf_ln_g, conf_ln_b,
              gdn_conv_w, gdn_a_log, gdn_dt_bias, gdn_norm_g, need_ctx):
    c_f, c_b, c_glu, c_og = _cd_stream(proj_c, gdn_conv_w, gdn_a_log, gdn_dt_bias)
    l_f, l_b, l_glu, l_og = _cd_stream(proj_l, gdn_conv_w, gdn_a_log, gdn_dt_bias)
    bsz = proj_l.shape[0]
    z0 = jnp.zeros((bsz, GDN_HEADS, GDN_DK, GDN_DV), F32)
    od_c, s_f, s_b = _bidir_scan(_gdn_scan, c_f, c_b, 2, z0, z0, need_ctx)
    od_l, _, _ = _bidir_scan(_gdn_scan, l_f, l_b, 2, s_f, s_b, True)

    def mix(od, glu, og, n_rows):
        t = glu.shape[1]
        conv = _conformer_conv(glu, n_rows, conf_dw_w, conf_dw_b, conf_ln_g, conf_ln_b)
        o = _head_rmsnorm(jnp.swapaxes(od, 1, 2), gdn_norm_g) * jax.nn.silu(og).reshape(bsz, t, GDN_HEADS, GDN_DV)
        return jnp.concatenate([conv, o.reshape(bsz, t, GDN_V)], axis=-1)

    return mix(od_l, l_glu, l_og, rows), (mix(od_c, c_glu, c_og, 1) if need_ctx else None)


def _expert_choice_ffn(h, router_w, w1, w3, w2):
    bsz, n, d = h.shape
    cap = n * EC_CAPACITY // N_EXPERTS
    aff = jax.nn.softmax(h @ router_w, axis=-1)
    gates, idx = lax.top_k(jnp.swapaxes(aff, 1, 2), cap)
    idx = idx.reshape(bsz, N_EXPERTS * cap)
    xe = jnp.take_along_axis(h, idx[..., None], axis=1).reshape(bsz, N_EXPERTS, cap, d)
    ye = expert_ffn(xe, w1, w3, w2) * gates[..., None]
    return jnp.zeros_like(h).at[jnp.arange(bsz)[:, None], idx].add(ye.reshape(bsz, N_EXPERTS * cap, d))


def kernel(x, c, ctx, c_ctx, mod_w, mod_b, norm1_g, norm2_g, ab_w_in, ab_b_in, ab_w_out, ab_b_out, gla_w_gate2, gla_b_gate2, gla_norm_g, ssd_conv_w, ssd_conv_b, ssd_dt_bias, ssd_a_log, ssd_d, ssd_norm_g, cd_w_in, cd_b_in, cd_w_out, cd_b_out, conf_dw_w, conf_dw_b, conf_ln_g, conf_ln_b, gdn_conv_w, gdn_a_log, gdn_dt_bias, gdn_norm_g, moe_router, moe_w1, moe_w3, moe_w2, final_norm_g):
    bsz = x.shape[0]
    rows = x.shape[1] // GRID_W
    depth = mod_w.shape[0]
    xc = ctx
    for i in range(depth):
        last = i == depth - 1
        j = i // 2
        mod_l = jax.nn.silu(c) @ mod_w[i] + mod_b[i]
        mod_c = jnp.broadcast_to(jax.nn.silu(c_ctx) @ mod_w[i] + mod_b[i], mod_l.shape)
        sh1, sc1, g1, sh2, sc2, g2 = jnp.split(mod_l, 6, axis=-1)
        sh1c, sc1c, g1c, sh2c, sc2c, g2c = jnp.split(mod_c, 6, axis=-1)
        if i % 2 == 0:
            w_in, b_in, w_out, b_out = ab_w_in[j], ab_b_in[j], ab_w_out[j], ab_b_out[j]
        else:
            w_in, b_in, w_out, b_out = cd_w_in[j], cd_b_in[j], cd_w_out[j], cd_b_out[j]
        n_in = w_in.shape[1]
        n_pad = -(-n_in // (4 * LANE)) * (4 * LANE)
        w_in_p = _pad_cols(w_in, n_pad).astype(BF16)
        b_in_p = _pad_cols(b_in, n_pad)
        tn = n_pad // (n_pad // 1024) if n_pad % 1024 == 0 else 4 * LANE
        proj_l = norm_proj(x, norm1_g[i], sh1, sc1, w_in_p, b_in_p, tn)[..., :n_in]
        proj_c = norm_proj(xc, norm1_g[i], sh1c, sc1c, w_in_p, b_in_p, tn)[..., :n_in]
        if i % 2 == 0:
            m, mc = _mixer_ab(proj_l, proj_c, gla_w_gate2[j], gla_b_gate2[j], gla_norm_g[j],
                              ssd_conv_w[j], ssd_conv_b[j], ssd_dt_bias[j], ssd_a_log[j], ssd_d[j], ssd_norm_g[j],
                              not last)
        else:
            m, mc = _mixer_cd(proj_l, proj_c, rows, conf_dw_w[j], conf_dw_b[j], conf_ln_g[j], conf_ln_b[j],
                              gdn_conv_w[j], gdn_a_log[j], gdn_dt_bias[j], gdn_norm_g[j], not last)
        w_out_b = w_out.astype(BF16)
        x = out_proj_residual(m, w_out_b, b_out, x, g1)
        w1, w3, w2 = moe_w1[i].astype(BF16), moe_w3[i].astype(BF16), moe_w2[i].astype(BF16)
        h = _rmsnorm(x, norm2_g[i]) * (1 + sc2[:, None]) + sh2[:, None]
        x = x + g2[:, None] * _expert_choice_ffn(h, moe_router[i], w1, w3, w2)
        if not last:
            xc = out_proj_residual(mc, w_out_b, b_out, xc, g1c)
            hc = _rmsnorm(xc, norm2_g[i]) * (1 + sc2c[:, None]) + sh2c[:, None]
            xc = xc + g2c[:, None] * _expert_choice_ffn(hc, moe_router[i], w1, w3, w2)
    return rmsnorm_rows(x, final_norm_g)
```

```python
import functools

import jax
import jax.numpy as jnp
import numpy as np
from jax import lax
from jax.experimental import pallas as pl
from jax.experimental.pallas import tpu as pltpu

F32 = jnp.float32
BF16 = jnp.bfloat16

D_MODEL = 1024
GRID_W = 64
CHUNK = 64
GLA_HEADS, GLA_DK, GLA_DV, GLA_GATE_RANK, GLA_GATE_TAU = 4, 128, 256, 16, 16.0
SSD_HEADS, SSD_HEADDIM, SSD_STATE, SSD_GROUPS = 16, 64, 128, 2
CONF_CH, CONF_KERNEL = D_MODEL, 31
GDN_HEADS, GDN_DK, GDN_DV = 8, 128, 128
N_EXPERTS, EC_CAPACITY, EXPERT_FF = 16, 2, D_MODEL

GLA_QK = GLA_HEADS * GLA_DK
GLA_V = GLA_HEADS * GLA_DV
SSD_INNER = SSD_HEADS * SSD_HEADDIM
SSD_BC = SSD_GROUPS * SSD_STATE
SSD_HPG = SSD_HEADS // SSD_GROUPS
GDN_QK = GDN_HEADS * GDN_DK
GDN_V = GDN_HEADS * GDN_DV
CD_SPLITS = (CONF_CH, CONF_CH, GDN_QK, GDN_QK, GDN_V, GDN_V, 2 * GDN_HEADS, 2 * GDN_HEADS)

LANE = 128
TT = 256
CPT = TT // CHUNK
VMEM_LIMIT = 48 * 1024 * 1024

AB_Q, AB_K, AB_V, AB_R, AB_Z, AB_XS, AB_BM, AB_CM, AB_SMALL = 0, 512, 1024, 2048, 3072, 4096, 5120, 5376, 5632
AB_N = AB_SMALL + LANE
SM_DT = 2 * GLA_GATE_RANK
CD_N = 6400


def _split_cols(a, sizes):
    return jnp.split(a, np.cumsum(sizes)[:-1].tolist(), axis=-1)


def _pad_cols(a, n):
    return jnp.pad(a, [(0, 0)] * (a.ndim - 1) + [(0, n - a.shape[-1])])


def _dot(a, b):
    return jnp.dot(a, b, preferred_element_type=F32)


def _dot_nt(a, b):
    return lax.dot_general(a, b, (((1,), (1,)), ((), ())), preferred_element_type=F32)


def _split3(x):
    hi = x.astype(BF16)
    r = x - hi.astype(F32)
    mid = r.astype(BF16)
    lo = (r - mid.astype(F32)).astype(BF16)
    return hi, mid, lo


def _sel_dot(m, x):
    hi, mid, lo = _split3(x)
    return _dot(m, hi) + _dot(m, mid) + _dot(m, lo)


def _dot_sel(x, e):
    hi, mid, lo = _split3(x)
    return _dot(hi, e) + _dot(mid, e) + _dot(lo, e)


def _softplus(x):
    return jnp.maximum(x, 0.0) + jnp.log(1.0 + jnp.exp(-jnp.abs(x)))


def _silu(x):
    return x * jax.nn.sigmoid(x)


def _chunk_masks(is_fwd, n):
    r = lax.broadcasted_iota(jnp.int32, (n, n), 0)
    c = lax.broadcasted_iota(jnp.int32, (n, n), 1)
    same = lax.shift_right_logical(r, 6) == lax.shift_right_logical(c, 6)
    lo = jnp.where(is_fwd, c, r)
    hi = jnp.where(is_fwd, r, c)
    cum = jnp.logical_and(same, lo <= hi)
    return jnp.where(cum, 1.0, 0.0).astype(BF16), jnp.where(same, 1.0, 0.0).astype(BF16)


def _causal_mask(is_fwd, n):
    r = lax.broadcasted_iota(jnp.int32, (n, n), 0)
    c = lax.broadcasted_iota(jnp.int32, (n, n), 1)
    return jnp.where(is_fwd, c, r) <= jnp.where(is_fwd, r, c)


def _scan_tile(d, j, nt):
    return jnp.where(d == 0, j, jnp.where(j == 0, 0, nt - j))


def _norm_proj_kernel(x_ref, g_ref, sh_ref, sc_ref, w_ref, b_ref, o_ref, h_ref):
    @pl.when(pl.program_id(2) == 0)
    def _():
        x = x_ref[0]
        ms = jnp.mean(x * x, axis=-1, keepdims=True)
        y = x * lax.rsqrt(ms + 1e-6) * g_ref[...]
        h_ref[...] = (y * (1.0 + sc_ref[0, 0]) + sh_ref[0, 0]).astype(BF16)

    o_ref[0] = _dot(h_ref[...], w_ref[...]) + b_ref[...]


def norm_proj(x, g, shift, scale, w, b, tn):
    bsz, t, d = x.shape
    n = w.shape[1]
    seg = lambda i, j, k: (i, jnp.minimum(j, 1), 0, 0)
    return pl.pallas_call(
        _norm_proj_kernel,
        grid=(bsz, t // TT, n // tn),
        in_specs=[
            pl.BlockSpec((1, TT, d), lambda i, j, k: (i, j, 0)),
            pl.BlockSpec((1, d), lambda i, j, k: (0, 0)),
            pl.BlockSpec((1, 1, 1, d), seg),
            pl.BlockSpec((1, 1, 1, d), seg),
            pl.BlockSpec((d, tn), lambda i, j, k: (0, k)),
            pl.BlockSpec((1, tn), lambda i, j, k: (0, k)),
        ],
        out_specs=pl.BlockSpec((1, TT, tn), lambda i, j, k: (i, j, k)),
        out_shape=jax.ShapeDtypeStruct((bsz, t, n), F32),
        scratch_shapes=[pltpu.VMEM((TT, d), BF16)],
        compiler_params=pltpu.CompilerParams(
            dimension_semantics=("parallel", "parallel", "arbitrary"), vmem_limit_bytes=VMEM_LIMIT),
        name="norm_proj",
    )(x, g.reshape(1, d), shift, scale, w, b.reshape(1, n))


def _conv3_piece(x, lh, rh, w, b, left_ok, right_ok):
    n = x.shape[0]
    row = lax.broadcasted_iota(jnp.int32, x.shape, 0)
    prev_row = jnp.where(left_ok, lh[7:8, :], 0.0)
    next_row = jnp.where(right_ok, rh[0:1, :], 0.0)
    x_prev = jnp.where(row == 0, prev_row, pltpu.roll(x, 1, 0))
    x_next = jnp.where(row == n - 1, next_row, pltpu.roll(x, n - 1, 0))
    return _silu(w[0:1, :] * x_prev + w[1:2, :] * x + w[2:3, :] * x_next + b)


def _ab_prep_kernel(xs_ref, bm_ref, cm_ref, xsl_ref, bml_ref, cml_ref, xsr_ref, bmr_ref, cmr_ref,
                    w_ref, b_ref, o_ref):
    j = pl.program_id(1)
    nt = pl.num_programs(1)
    left_ok = j >= 2
    right_ok = jnp.logical_and(j >= 1, j < nt - 1)
    w = w_ref[...]
    b = b_ref[...]
    o_ref[0, :, 0:SSD_INNER] = _conv3_piece(xs_ref[0], xsl_ref[0], xsr_ref[0], w[:, 0:SSD_INNER],
                                            b[:, 0:SSD_INNER], left_ok, right_ok)
    c0, c1 = SSD_INNER, SSD_INNER + SSD_BC
    o_ref[0, :, c0:c1] = _conv3_piece(bm_ref[0], bml_ref[0], bmr_ref[0], w[:, c0:c1], b[:, c0:c1], left_ok, right_ok)
    c0, c1 = c1, c1 + SSD_BC
    o_ref[0, :, c0:c1] = _conv3_piece(cm_ref[0], cml_ref[0], cmr_ref[0], w[:, c0:c1], b[:, c0:c1], left_ok, right_ok)


def ab_prep(proj, conv_w, conv_b):
    bsz, t, _ = proj.shape
    nt = t // TT
    rb = TT // 8
    nrb = t // 8
    cw = SSD_INNER + 2 * SSD_BC

    def cur(width, col):
        return pl.BlockSpec((1, TT, width), lambda i, j: (i, j, col // width))

    def left(width, col):
        return pl.BlockSpec((1, 8, width), lambda i, j: (i, jnp.maximum(j * rb - 1, 0), col // width))

    def right(width, col):
        return pl.BlockSpec((1, 8, width), lambda i, j: (i, jnp.minimum((j + 1) * rb, nrb - 1), col // width))

    pieces = ((SSD_INNER, AB_XS), (SSD_BC, AB_BM), (SSD_BC, AB_CM))
    return pl.pallas_call(
        _ab_prep_kernel,
        grid=(bsz, nt),
        in_specs=[cur(*p) for p in pieces] + [left(*p) for p in pieces] + [right(*p) for p in pieces] + [
            pl.BlockSpec((3, cw), lambda i, j: (0, 0)),
            pl.BlockSpec((1, cw), lambda i, j: (0, 0)),
        ],
        out_specs=pl.BlockSpec((1, TT, cw), lambda i, j: (i, j, 0)),
        out_shape=jax.ShapeDtypeStruct((bsz, t, cw), F32),
        compiler_params=pltpu.CompilerParams(dimension_semantics=("parallel", "parallel")),
        name="ab_prep",
    )(*([proj] * 9), conv_w, conv_b.reshape(1, cw))


def _gla_kernel(q_ref, k_ref, v_ref, sm_ref, wg_ref, bg_ref, o_ref, qg_s, kn_s, kd_s, egl_s, st_s):
    d = pl.program_id(2)
    j = pl.program_id(3)
    is_fwd = d == 0

    @pl.when(j == 0)
    def _():
        st_s[...] = jnp.zeros_like(st_s)

    gz = _dot(sm_ref[0].astype(BF16), wg_ref[0, 0]) + bg_ref[0, 0]
    logg = (jnp.minimum(gz, 0.0) - jnp.log(1.0 + jnp.exp(-jnp.abs(gz)))) * (1.0 / GLA_GATE_TAU)
    m_cum, m_all = _chunk_masks(is_fwd, TT)
    gc = _sel_dot(m_cum, logg)
    gl = _sel_dot(m_all, logg)
    q = q_ref[0] * (GLA_DK ** -0.5)
    k = k_ref[0]
    qg_s[...] = (q * jnp.exp(gc)).astype(BF16)
    kn_s[...] = (k * jnp.exp(-gc)).astype(BF16)
    kd_s[...] = (k * jnp.exp(gl - gc)).astype(BF16)
    egl_s[...] = jnp.exp(gl)
    causal = _causal_mask(is_fwd, CHUNK)

    for ci in range(CPT):
        off = pl.multiple_of(jnp.where(is_fwd, ci, CPT - 1 - ci) * CHUNK, CHUNK)
        rows = pl.ds(off, CHUNK)
        qg = qg_s[rows, :]
        v = v_ref[0, rows, :]
        st = st_s[...]
        att = jnp.where(causal, _dot_nt(qg, kn_s[rows, :]), 0.0)
        o_ref[0, 0, rows, :] = _dot(att.astype(BF16), v.astype(BF16)) + _dot_nt(qg, st.astype(BF16))
        st_s[...] = st * egl_s[pl.ds(off, 1), :] + _dot(v.T.astype(BF16), kd_s[rows, :])


def gla_scan(proj, wg, bg):
    bsz, t, _ = proj.shape
    nt = t // TT
    tile = lambda i, h, d, j: _scan_tile(d, j, nt)
    return pl.pallas_call(
        _gla_kernel,
        grid=(bsz, GLA_HEADS, 2, nt),
        in_specs=[
            pl.BlockSpec((1, TT, GLA_DK), lambda i, h, d, j: (i, tile(i, h, d, j), AB_Q // GLA_DK + h)),
            pl.BlockSpec((1, TT, GLA_DK), lambda i, h, d, j: (i, tile(i, h, d, j), AB_K // GLA_DK + h)),
            pl.BlockSpec((1, TT, GLA_DV), lambda i, h, d, j: (i, tile(i, h, d, j), AB_V // GLA_DV + h)),
            pl.BlockSpec((1, TT, LANE), lambda i, h, d, j: (i, tile(i, h, d, j), AB_SMALL // LANE)),
            pl.BlockSpec((1, 1, LANE, GLA_DK), lambda i, h, d, j: (d, h, 0, 0)),
            pl.BlockSpec((1, 1, 1, GLA_DK), lambda i, h, d, j: (d, h, 0, 0)),
        ],
        out_specs=pl.BlockSpec((1, 1, TT, GLA_DV), lambda i, h, d, j: (d, i, tile(i, h, d, j), h)),
        out_shape=jax.ShapeDtypeStruct((2, bsz, t, GLA_V), F32),
        scratch_shapes=[pltpu.VMEM((TT, GLA_DK), BF16), pltpu.VMEM((TT, GLA_DK), BF16),
                        pltpu.VMEM((TT, GLA_DK), BF16), pltpu.VMEM((TT, GLA_DK), F32),
                        pltpu.VMEM((GLA_DV, GLA_DK), F32)],
        compiler_params=pltpu.CompilerParams(
            dimension_semantics=("parallel", "parallel", "parallel", "arbitrary")),
        name="gla_scan",
    )(proj, proj, proj, proj, wg, bg)


def _ssd_kernel(xs_ref, bm_ref, cm_ref, sm_ref, dtb_ref, nega_ref, e_ref, e8_ref, o_ref,
                v_s, vw_s, cdec_s, dec_s, a8_s, st_s):
    d = pl.program_id(2)
    j = pl.program_id(3)
    is_fwd = d == 0

    @pl.when(j == 0)
    def _():
        st_s[...] = jnp.zeros_like(st_s)

    dt = _softplus(sm_ref[0] + dtb_ref[0])
    la = dt * nega_ref[0]
    m_cum, m_all = _chunk_masks(is_fwd, TT)
    acum = _sel_dot(m_cum, la)
    atot = _sel_dot(m_all, la)
    e = e_ref[0, 0]
    acum_e = _dot_sel(acum, e)
    atot_e = _dot_sel(atot, e)
    v = xs_ref[0] * _dot_sel(dt, e)
    v_s[...] = v.astype(BF16)
    vw_s[...] = (v * jnp.exp(atot_e - acum_e)).astype(BF16)
    cdec_s[...] = jnp.exp(acum_e)
    dec_s[...] = jnp.exp(atot_e)
    a8_s[...] = _dot_sel(acum, e8_ref[0, 0])
    causal = _causal_mask(is_fwd, CHUNK)

    for ci in range(CPT):
        off = pl.multiple_of(jnp.where(is_fwd, ci, CPT - 1 - ci) * CHUNK, CHUNK)
        rows = pl.ds(off, CHUNK)
        bm = bm_ref[0, rows, :]
        cm = cm_ref[0, rows, :].astype(BF16)
        st = st_s[...]
        cb = _dot_nt(cm, bm.astype(BF16))
        y_inter = _dot(cm, st.astype(BF16)) * cdec_s[rows, :]
        a8 = a8_s[rows, :]
        a8t = a8.T
        v_c = v_s[rows, :]
        ys = []
        for hh in range(SSD_HPG):
            diff = a8[:, hh:hh + 1] - a8t[hh:hh + 1, :]
            seg = jnp.exp(jnp.where(causal, diff, -1e30))
            ys.append(_dot((seg * cb).astype(BF16), v_c[:, hh * SSD_HEADDIM:(hh + 1) * SSD_HEADDIM]))
        o_ref[0, 0, rows, :] = jnp.concatenate(ys, axis=1) + y_inter
        st_s[...] = st * dec_s[pl.ds(off, 1), :] + _dot(bm.T.astype(BF16), vw_s[rows, :])


def ssd_scan(xbc, proj, dtb, nega, e, e8):
    bsz, t, _ = xbc.shape
    nt = t // TT
    gw = SSD_HPG * SSD_HEADDIM
    tile = lambda d, j: _scan_tile(d, j, nt)
    return pl.pallas_call(
        _ssd_kernel,
        grid=(bsz, SSD_GROUPS, 2, nt),
        in_specs=[
            pl.BlockSpec((1, TT, gw), lambda i, g, d, j: (i, tile(d, j), g)),
            pl.BlockSpec((1, TT, SSD_STATE), lambda i, g, d, j: (i, tile(d, j), SSD_INNER // SSD_STATE + g)),
            pl.BlockSpec((1, TT, SSD_STATE), lambda i, g, d, j: (i, tile(d, j), (SSD_INNER + SSD_BC) // SSD_STATE + g)),
            pl.BlockSpec((1, TT, LANE), lambda i, g, d, j: (i, tile(d, j), AB_SMALL // LANE)),
            pl.BlockSpec((1, 1, LANE), lambda i, g, d, j: (d, 0, 0)),
            pl.BlockSpec((1, 1, LANE), lambda i, g, d, j: (d, 0, 0)),
            pl.BlockSpec((1, 1, LANE, gw), lambda i, g, d, j: (d, g, 0, 0)),
            pl.BlockSpec((1, 1, LANE, LANE), lambda i, g, d, j: (d, g, 0, 0)),
        ],
        out_specs=pl.BlockSpec((1, 1, TT, gw), lambda i, g, d, j: (d, i, tile(d, j), g)),
        out_shape=jax.ShapeDtypeStruct((2, bsz, t, SSD_INNER), F32),
        scratch_shapes=[pltpu.VMEM((TT, gw), BF16), pltpu.VMEM((TT, gw), BF16),
                        pltpu.VMEM((TT, gw), F32), pltpu.VMEM((TT, gw), F32),
                        pltpu.VMEM((TT, LANE), F32), pltpu.VMEM((SSD_STATE, gw), F32)],
        compiler_params=pltpu.CompilerParams(
            dimension_semantics=("parallel", "parallel", "parallel", "arbitrary")),
        name="ssd_scan",
    )(xbc, xbc, xbc, proj, dtb, nega, e, e8)


def _group_rmsnorm(x, width):
    parts = []
    for s in range(x.shape[1] // width):
        seg = x[:, s * width:(s + 1) * width]
        parts.append(seg * lax.rsqrt(jnp.mean(seg * seg, axis=-1, keepdims=True) + 1e-6))
    return jnp.concatenate(parts, axis=1)


def _ab_out_kernel(of_ref, ob_ref, yf_ref, yb_ref, r_ref, z_ref, xs_ref, gg_ref, dv_ref, sg_ref,
                   w_ref, b_ref, x_ref, gate_ref, o_ref):
    o = _group_rmsnorm(of_ref[0, 0] + ob_ref[0, 0], GLA_DV) * gg_ref[...] * _silu(r_ref[0])
    y = (yf_ref[0, 0] + yb_ref[0, 0] + dv_ref[...] * xs_ref[0]) * _silu(z_ref[0])
    y = _group_rmsnorm(y, SSD_INNER // SSD_GROUPS) * sg_ref[...]
    m = _dot(o.astype(BF16), w_ref[0:GLA_V, :]) + _dot(y.astype(BF16), w_ref[GLA_V:, :]) + b_ref[...]
    o_ref[0] = x_ref[0] + gate_ref[0, 0] * m


def ab_out(o_gla, y_ssd, proj, xbc, gla_g, d_vec, ssd_g, w, b, x, gate):
    bsz, t, d = x.shape
    seg = lambda i, j: (i, jnp.minimum(j, 1), 0, 0)
    row = lambda width: pl.BlockSpec((1, width), lambda i, j: (0, 0))
    return pl.pallas_call(
        _ab_out_kernel,
        grid=(bsz, t // TT),
        in_specs=[
            pl.BlockSpec((1, 1, TT, GLA_V), lambda i, j: (0, i, j, 0)),
            pl.BlockSpec((1, 1, TT, GLA_V), lambda i, j: (1, i, j, 0)),
            pl.BlockSpec((1, 1, TT, SSD_INNER), lambda i, j: (0, i, j, 0)),
            pl.BlockSpec((1, 1, TT, SSD_INNER), lambda i, j: (1, i, j, 0)),
            pl.BlockSpec((1, TT, GLA_V), lambda i, j: (i, j, AB_R // GLA_V)),
            pl.BlockSpec((1, TT, SSD_INNER), lambda i, j: (i, j, AB_Z // SSD_INNER)),
            pl.BlockSpec((1, TT, SSD_INNER), lambda i, j: (i, j, 0)),
            row(GLA_V), row(SSD_INNER), row(SSD_INNER),
            pl.BlockSpec((GLA_V + SSD_INNER, d), lambda i, j: (0, 0)),
            row(d),
            pl.BlockSpec((1, TT, d), lambda i, j: (i, j, 0)),
            pl.BlockSpec((1, 1, 1, d), seg),
        ],
        out_specs=pl.BlockSpec((1, TT, d), lambda i, j: (i, j, 0)),
        out_shape=jax.ShapeDtypeStruct((bsz, t, d), F32),
        compiler_params=pltpu.CompilerParams(
            dimension_semantics=("parallel", "parallel"), vmem_limit_bytes=VMEM_LIMIT),
        name="ab_out",
    )(o_gla, o_gla, y_ssd, y_ssd, proj, proj, xbc, gla_g.reshape(1, -1), d_vec.reshape(1, -1),
      ssd_g.reshape(1, -1), w, b.reshape(1, d), x, gate)


def _out_proj_kernel(m_ref, w_ref, b_ref, x_ref, gate_ref, o_ref):
    y = _dot(m_ref[0].astype(BF16), w_ref[...]) + b_ref[...]
    o_ref[0] = x_ref[0] + gate_ref[0] * y


def out_proj_residual(mixed, w, b, x, gate):
    bsz, t, k = mixed.shape
    d = w.shape[1]
    tm = min(t, 512)
    return pl.pallas_call(
        _out_proj_kernel,
        grid=(bsz, t // tm),
        in_specs=[
            pl.BlockSpec((1, tm, k), lambda i, j: (i, j, 0)),
            pl.BlockSpec((k, d), lambda i, j: (0, 0)),
            pl.BlockSpec((1, d), lambda i, j: (0, 0)),
            pl.BlockSpec((1, tm, d), lambda i, j: (i, j, 0)),
            pl.BlockSpec((1, 1, d), lambda i, j: (i, 0, 0)),
        ],
        out_specs=pl.BlockSpec((1, tm, d), lambda i, j: (i, j, 0)),
        out_shape=jax.ShapeDtypeStruct((bsz, t, d), F32),
        compiler_params=pltpu.CompilerParams(
            dimension_semantics=("parallel", "parallel"), vmem_limit_bytes=VMEM_LIMIT),
        name="out_proj",
    )(mixed, w, b.reshape(1, d), x, gate.reshape(bsz, 1, d))


def _expert_ffn_kernel(x_ref, w1_ref, w3_ref, w2_ref, o_ref):
    x = x_ref[0, 0].astype(BF16)
    a = _dot(x, w1_ref[0])
    g = _dot(x, w3_ref[0])
    o_ref[0, 0] = _dot((_silu(a) * g).astype(BF16), w2_ref[0])


def expert_ffn(xe, w1, w3, w2):
    bsz, ne, cap, d = xe.shape
    f = w1.shape[2]
    tm = min(cap, 512)
    return pl.pallas_call(
        _expert_ffn_kernel,
        grid=(ne, bsz, cap // tm),
        in_specs=[
            pl.BlockSpec((1, 1, tm, d), lambda e, i, j: (i, e, j, 0)),
            pl.BlockSpec((1, d, f), lambda e, i, j: (e, 0, 0)),
            pl.BlockSpec((1, d, f), lambda e, i, j: (e, 0, 0)),
            pl.BlockSpec((1, f, d), lambda e, i, j: (e, 0, 0)),
        ],
        out_specs=pl.BlockSpec((1, 1, tm, d), lambda e, i, j: (i, e, j, 0)),
        out_shape=jax.ShapeDtypeStruct((bsz, ne, cap, d), F32),
        compiler_params=pltpu.CompilerParams(
            dimension_semantics=("parallel", "parallel", "parallel"), vmem_limit_bytes=VMEM_LIMIT),
        name="expert_ffn",
    )(xe, w1, w3, w2)


def _rmsnorm_kernel(x_ref, g_ref, o_ref):
    x = x_ref[0]
    ms = jnp.mean(x * x, axis=-1, keepdims=True)
    o_ref[0] = x * lax.rsqrt(ms + 1e-6) * g_ref[...]


def rmsnorm_rows(x, g):
    bsz, t, d = x.shape
    tm = min(t, 1024)
    return pl.pallas_call(
        _rmsnorm_kernel,
        grid=(bsz, t // tm),
        in_specs=[pl.BlockSpec((1, tm, d), lambda i, j: (i, j, 0)), pl.BlockSpec((1, d), lambda i, j: (0, 0))],
        out_specs=pl.BlockSpec((1, tm, d), lambda i, j: (i, j, 0)),
        out_shape=jax.ShapeDtypeStruct((bsz, t, d), F32),
        compiler_params=pltpu.CompilerParams(dimension_semantics=("parallel", "parallel")),
        name="final_rmsnorm",
    )(x, g.reshape(1, d))


def _ab_in_layout(w_in, b_in):
    q, k, v, r, glr, z, xs, bm, cm, dt = _split_cols(
        jnp.concatenate([w_in, b_in[None]], axis=0),
        (GLA_QK, GLA_QK, GLA_V, GLA_V, 2 * GLA_GATE_RANK, SSD_INNER, SSD_INNER, SSD_BC, SSD_BC, 2 * SSD_HEADS))
    wb = _pad_cols(jnp.concatenate([q, k, v, r, z, xs, bm, cm, glr, dt], axis=1), AB_N)
    return wb[:-1].astype(BF16), wb[-1]


def _gla_gate_params(w_gate2, b_gate2):
    wg = jnp.zeros((2, GLA_HEADS, LANE, GLA_DK), F32)
    for d in range(2):
        blk = w_gate2[d].reshape(GLA_GATE_RANK, GLA_HEADS, GLA_DK).transpose(1, 0, 2)
        wg = wg.at[d, :, d * GLA_GATE_RANK:(d + 1) * GLA_GATE_RANK, :].set(blk)
    return wg.astype(BF16), b_gate2.reshape(2, GLA_HEADS, 1, GLA_DK)


def _ssd_params(dt_bias, a_log):
    gw = SSD_HPG * SSD_HEADDIM
    dtb = jnp.zeros((2, 1, LANE), F32)
    nega = jnp.zeros((2, 1, LANE), F32)
    e = np.zeros((2, SSD_GROUPS, LANE, gw), np.float32)
    e8 = np.zeros((2, SSD_GROUPS, LANE, LANE), np.float32)
    for d in range(2):
        c0 = SM_DT + d * SSD_HEADS
        dtb = dtb.at[d, 0, c0:c0 + SSD_HEADS].set(dt_bias[d])
        nega = nega.at[d, 0, c0:c0 + SSD_HEADS].set(-jnp.exp(a_log[d]))
        for g in range(SSD_GROUPS):
            for hh in range(SSD_HPG):
                c = c0 + g * SSD_HPG + hh
                e[d, g, c, hh * SSD_HEADDIM:(hh + 1) * SSD_HEADDIM] = 1.0
                e8[d, g, c, hh] = 1.0
    return dtb, nega, jnp.asarray(e, BF16), jnp.asarray(e8, BF16)


def _rmsnorm(x, g, eps=1e-6):
    return x * lax.rsqrt(jnp.mean(jnp.square(x), axis=-1, keepdims=True) + eps) * g


def _head_rmsnorm(x, g):
    return _rmsnorm(x, g.reshape(x.shape[-2:]))


def _layernorm(x, g, b, eps=1e-5):
    mu = jnp.mean(x, axis=-1, keepdims=True)
    var = jnp.mean(jnp.square(x - mu), axis=-1, keepdims=True)
    return (x - mu) * lax.rsqrt(var + eps) * g + b


def _l2norm(x, eps=1e-6):
    return x * lax.rsqrt(jnp.sum(jnp.square(x), axis=-1, keepdims=True) + eps)


def _dwconv(x, w, b=None):
    k, ch = w.shape
    pad = (k - 1) // 2
    y = lax.conv_general_dilated(x, w[:, None, :], (1,), [(pad, pad)],
                                 dimension_numbers=('NWC', 'WIO', 'NWC'), feature_group_count=ch)
    return y if b is None else y + b


def _gdn_scan(q, k, v, beta, logg, s0, with_out):
    bsz, nh, t, dk = q.shape
    dv = v.shape[-1]
    nc = t // CHUNK
    chunks = lambda z: z.reshape(bsz, nh, nc, CHUNK, *z.shape[3:])
    q, k, v, beta, logg = (chunks(z) for z in (q, k, v, beta, logg))
    gc = jnp.cumsum(logg, axis=-1)
    glast = gc[..., -1]
    tril = jnp.tril(jnp.ones((CHUNK, CHUNK), dtype=bool))
    strict = jnp.tril(jnp.ones((CHUNK, CHUNK), dtype=bool), k=-1)
    decay = jnp.exp(jnp.where(tril, gc[..., :, None] - gc[..., None, :], -jnp.inf))
    kb = k * beta[..., None]
    m = jnp.eye(CHUNK, dtype=k.dtype) + jnp.where(strict, jnp.einsum('bhcid,bhcjd->bhcij', kb, k) * decay, 0.0)
    rhs = jnp.concatenate([v * beta[..., None], kb * jnp.exp(gc)[..., None]], axis=-1)
    sol = lax.linalg.triangular_solve(m, rhs, left_side=True, lower=True, unit_diagonal=True)
    u, w = sol[..., :dv], sol[..., dv:]
    kd = k * jnp.exp(glast[..., None] - gc)[..., None]
    dec = jnp.exp(glast)
    front = lambda z: jnp.moveaxis(z, 2, 0)

    def advance(s, w_c, u_c, kd_c, dec_c):
        vn = u_c - jnp.einsum('bhid,bhde->bhie', w_c, s)
        return vn, dec_c[..., None, None] * s + jnp.einsum('bhjd,bhje->bhde', kd_c, vn)

    xs = (front(w), front(u), front(kd), front(dec))
    if not with_out:
        s_fin, _ = lax.scan(lambda s, xc: (advance(s, *xc)[1], None), s0, xs)
        return None, s_fin
    aqk = jnp.einsum('bhcid,bhcjd->bhcij', q, k) * decay
    qg = q * jnp.exp(gc)[..., None]

    def step(s, xc):
        w_c, u_c, kd_c, dec_c, aqk_c, qg_c = xc
        vn, s_new = advance(s, w_c, u_c, kd_c, dec_c)
        o = jnp.einsum('bhid,bhde->bhie', qg_c, s) + jnp.einsum('bhij,bhje->bhie', aqk_c, vn)
        return s_new, o

    s_fin, o = lax.scan(step, s0, xs + (front(aqk), front(qg)))
    return jnp.moveaxis(o, 0, 2).reshape(bsz, nh, t, dv), s_fin


def _bidir_scan(scan_fn, args_f, args_b, t_axis, init_f, init_b, with_out):
    flip = lambda a: jnp.flip(a, axis=t_axis)
    o_f, s_f = scan_fn(*args_f, init_f, with_out)
    o_b, s_b = scan_fn(*[flip(a) for a in args_b], init_b, with_out)
    o = o_f + flip(o_b) if with_out else None
    return o, s_f, s_b


def _conformer_conv(u, rows, dw_w, dw_b, ln_g, ln_b):
    bsz, t, ch = u.shape
    y = _dwconv(u.reshape(bsz * rows, t // rows, ch), dw_w, dw_b).reshape(bsz, t, ch)
    return jax.nn.silu(_layernorm(y, ln_g, ln_b))


def _cd_stream(proj, gdn_conv_w, gdn_a_log, gdn_dt_bias):
    bsz, t, _ = proj.shape
    ga, gb, q, k, v, og, a_raw, b_raw = _split_cols(proj, CD_SPLITS)
    glu = ga * jax.nn.sigmoid(gb)
    qkv = jax.nn.silu(_dwconv(jnp.concatenate([q, k, v], axis=-1), gdn_conv_w))
    q, k, v = _split_cols(qkv, (GDN_QK, GDN_QK, GDN_V))
    heads = lambda a: a.reshape(bsz, t, GDN_HEADS, -1).transpose(0, 2, 1, 3)
    q = _l2norm(heads(q)) * GDN_DK ** -0.5
    k = _l2norm(heads(k))
    v = heads(v)
    beta = jax.nn.sigmoid(b_raw.reshape(bsz, t, 2, GDN_HEADS)).transpose(2, 0, 3, 1)
    logg = (-jnp.exp(gdn_a_log)
            * jax.nn.softplus(a_raw.reshape(bsz, t, 2, GDN_HEADS) + gdn_dt_bias)).transpose(2, 0, 3, 1)
    return (q, k, v, beta[0], logg[0]), (q, k, v, beta[1], logg[1]), glu, og


def _mixer_cd(proj_l, proj_c, rows, conf_dw_w, conf_dw_b, conf_ln_g, conf_ln_b,
              gdn_conv_w, gdn_a_log, gdn_dt_bias, gdn_norm_g, need_ctx):
    c_f, c_b, c_glu, c_og = _cd_stream(proj_c, gdn_conv_w, gdn_a_log, gdn_dt_bias)
    l_f, l_b, l_glu, l_og = _cd_stream(proj_l, gdn_conv_w, gdn_a_log, gdn_dt_bias)
    bsz = proj_l.shape[0]
    z0 = jnp.zeros((bsz, GDN_HEADS, GDN_DK, GDN_DV), F32)
    od_c, s_f, s_b = _bidir_scan(_gdn_scan, c_f, c_b, 2, z0, z0, need_ctx)
    od_l, _, _ = _bidir_scan(_gdn_scan, l_f, l_b, 2, s_f, s_b, True)

    def mix(od, glu, og, n_rows):
        t = glu.shape[1]
        conv = _conformer_conv(glu, n_rows, conf_dw_w, conf_dw_b, conf_ln_g, conf_ln_b)
        o = _head_rmsnorm(jnp.swapaxes(od, 1, 2), gdn_norm_g) * jax.nn.silu(og).reshape(bsz, t, GDN_HEADS, GDN_DV)
        return jnp.concatenate([conv, o.reshape(bsz, t, GDN_V)], axis=-1)

    return mix(od_l, l_glu, l_og, rows), (mix(od_c, c_glu, c_og, 1) if need_ctx else None)


def _expert_choice_ffn(h, router_w, w1, w3, w2):
    bsz, n, d = h.shape
    cap = n * EC_CAPACITY // N_EXPERTS
    aff = jax.nn.softmax(h @ router_w, axis=-1)
    gates, idx = lax.top_k(jnp.swapaxes(aff, 1, 2), cap)
    idx = idx.reshape(bsz, N_EXPERTS * cap)
    xe = jnp.take_along_axis(h, idx[..., None], axis=1).reshape(bsz, N_EXPERTS, cap, d)
    ye = expert_ffn(xe, w1, w3, w2) * gates[..., None]
    return jnp.zeros_like(h).at[jnp.arange(bsz)[:, None], idx].add(ye.reshape(bsz, N_EXPERTS * cap, d))


def layer_mixer(i, j, xa, mods, p, last):
    sh1, sc1, g1 = (mods[:, :, s] for s in range(3))
    if i % 2 == 0:
        w_in, b_in = _ab_in_layout(p["ab_w_in"][j], p["ab_b_in"][j])
        proj = norm_proj(xa, p["norm1_g"][i], sh1, sc1, w_in, b_in, 1920)
        xbc = ab_prep(proj, p["ssd_conv_w"][j], p["ssd_conv_b"][j])
        wg, bg = _gla_gate_params(p["gla_w_gate2"][j], p["gla_b_gate2"][j])
        o_gla = gla_scan(proj, wg, bg)
        y_ssd = ssd_scan(xbc, proj, *_ssd_params(p["ssd_dt_bias"][j], p["ssd_a_log"][j]))
        return ab_out(o_gla, y_ssd, proj, xbc, p["gla_norm_g"][j], jnp.repeat(p["ssd_d"][j], SSD_HEADDIM),
                      p["ssd_norm_g"][j], p["ab_w_out"][j].astype(BF16), p["ab_b_out"][j], xa, g1)
    rows = (xa.shape[1] - TT) // GRID_W
    w_in = _pad_cols(p["cd_w_in"][j], CD_N).astype(BF16)
    b_in = _pad_cols(p["cd_b_in"][j], CD_N)
    n_in = p["cd_w_in"][j].shape[1]
    proj = norm_proj(xa, p["norm1_g"][i], sh1, sc1, w_in, b_in, 1280)
    m, mc = _mixer_cd(proj[:, TT:, :n_in], proj[:, :TT, :n_in], rows,
                      p["conf_dw_w"][j], p["conf_dw_b"][j], p["conf_ln_g"][j], p["conf_ln_b"][j],
                      p["gdn_conv_w"][j], p["gdn_a_log"][j], p["gdn_dt_bias"][j], p["gdn_norm_g"][j], not last)
    w_out_b = p["cd_w_out"][j].astype(BF16)
    x_l = out_proj_residual(m, w_out_b, p["cd_b_out"][j], xa[:, TT:], g1[:, 1, 0])
    x_c = out_proj_residual(mc, w_out_b, p["cd_b_out"][j], xa[:, :TT], g1[:, 0, 0]) if not last else xa[:, :TT]
    return jnp.concatenate([x_c, x_l], axis=1)


def kernel(x, c, ctx, c_ctx, mod_w, mod_b, norm1_g, norm2_g, ab_w_in, ab_b_in, ab_w_out, ab_b_out, gla_w_gate2, gla_b_gate2, gla_norm_g, ssd_conv_w, ssd_conv_b, ssd_dt_bias, ssd_a_log, ssd_d, ssd_norm_g, cd_w_in, cd_b_in, cd_w_out, cd_b_out, conf_dw_w, conf_dw_b, conf_ln_g, conf_ln_b, gdn_conv_w, gdn_a_log, gdn_dt_bias, gdn_norm_g, moe_router, moe_w1, moe_w3, moe_w2, final_norm_g):
    p = dict(locals())
    bsz, seq, d = x.shape
    assert ctx.shape[1] == TT and seq % TT == 0
    depth = mod_w.shape[0]
    xa = jnp.concatenate([ctx, x], axis=1)
    for i in range(depth):
        last = i == depth - 1
        mod_l = jax.nn.silu(c) @ mod_w[i] + mod_b[i]
        mod_c = jnp.broadcast_to(jax.nn.silu(c_ctx) @ mod_w[i] + mod_b[i], mod_l.shape)
        mods = jnp.stack([mod_c, mod_l], axis=1).reshape(bsz, 2, 6, 1, d)
        sh1, sc1, g1, sh2, sc2, g2 = (mods[:, :, s] for s in range(6))
        xa = layer_mixer(i, i // 2, xa, mods, p, last)
        w1, w3, w2 = moe_w1[i].astype(BF16), moe_w3[i].astype(BF16), moe_w2[i].astype(BF16)
        x_c, x_l = xa[:, :TT], xa[:, TT:]
        h = _rmsnorm(x_l, norm2_g[i]) * (1 + sc2[:, 1]) + sh2[:, 1]
        x_l = x_l + g2[:, 1] * _expert_choice_ffn(h, moe_router[i], w1, w3, w2)
        if not last:
            hc = _rmsnorm(x_c, norm2_g[i]) * (1 + sc2[:, 0]) + sh2[:, 0]
            x_c = x_c + g2[:, 0] * _expert_choice_ffn(hc, moe_router[i], w1, w3, w2)
        xa = jnp.concatenate([x_c, x_l], axis=1)
    return rmsnorm_rows(xa[:, TT:], final_norm_g)
```

```python
import functools

import jax
import jax.numpy as jnp
import numpy as np
from jax import lax
from jax.experimental import pallas as pl
from jax.experimental.pallas import tpu as pltpu

F32 = jnp.float32
BF16 = jnp.bfloat16

D_MODEL = 1024
GRID_W = 64
CHUNK = 64
GLA_HEADS, GLA_DK, GLA_DV, GLA_GATE_RANK, GLA_GATE_TAU = 4, 128, 256, 16, 16.0
SSD_HEADS, SSD_HEADDIM, SSD_STATE, SSD_GROUPS = 16, 64, 128, 2
CONF_CH, CONF_KERNEL = D_MODEL, 31
GDN_HEADS, GDN_DK, GDN_DV = 8, 128, 128
N_EXPERTS, EC_CAPACITY, EXPERT_FF = 16, 2, D_MODEL

GLA_QK = GLA_HEADS * GLA_DK
GLA_V = GLA_HEADS * GLA_DV
SSD_INNER = SSD_HEADS * SSD_HEADDIM
SSD_BC = SSD_GROUPS * SSD_STATE
SSD_HPG = SSD_HEADS // SSD_GROUPS
GDN_QK = GDN_HEADS * GDN_DK
GDN_V = GDN_HEADS * GDN_DV
CD_SPLITS = (CONF_CH, CONF_CH, GDN_QK, GDN_QK, GDN_V, GDN_V, 2 * GDN_HEADS, 2 * GDN_HEADS)

LANE = 128
TT = 256
CPT = TT // CHUNK
VMEM_LIMIT = 48 * 1024 * 1024

AB_Q, AB_K, AB_V, AB_R, AB_Z, AB_XS, AB_BM, AB_CM, AB_SMALL = 0, 512, 1024, 2048, 3072, 4096, 5120, 5376, 5632
AB_N = AB_SMALL + LANE
SM_DT = 2 * GLA_GATE_RANK
CD_N = 6400


def _split_cols(a, sizes):
    return jnp.split(a, np.cumsum(sizes)[:-1].tolist(), axis=-1)


def _pad_cols(a, n):
    return jnp.pad(a, [(0, 0)] * (a.ndim - 1) + [(0, n - a.shape[-1])])


def _dot(a, b):
    return jnp.dot(a, b, preferred_element_type=F32)


def _dot_nt(a, b):
    return lax.dot_general(a, b, (((1,), (1,)), ((), ())), preferred_element_type=F32)


def _split3(x):
    hi = x.astype(BF16)
    r = x - hi.astype(F32)
    mid = r.astype(BF16)
    lo = (r - mid.astype(F32)).astype(BF16)
    return hi, mid, lo


def _sel_dot(m, x):
    hi, mid, lo = _split3(x)
    return _dot(m, hi) + _dot(m, mid) + _dot(m, lo)


def _dot_sel(x, e):
    hi, mid, lo = _split3(x)
    return _dot(hi, e) + _dot(mid, e) + _dot(lo, e)


def _softplus(x):
    return jnp.maximum(x, 0.0) + jnp.log(1.0 + jnp.exp(-jnp.abs(x)))


def _silu(x):
    return x * jax.nn.sigmoid(x)


def _chunk_masks(is_fwd, n):
    r = lax.broadcasted_iota(jnp.int32, (n, n), 0)
    c = lax.broadcasted_iota(jnp.int32, (n, n), 1)
    same = lax.shift_right_logical(r, 6) == lax.shift_right_logical(c, 6)
    lo = jnp.where(is_fwd, c, r)
    hi = jnp.where(is_fwd, r, c)
    cum = jnp.logical_and(same, lo <= hi)
    return jnp.where(cum, 1.0, 0.0).astype(BF16), jnp.where(same, 1.0, 0.0).astype(BF16)


def _causal_mask(is_fwd, n):
    r = lax.broadcasted_iota(jnp.int32, (n, n), 0)
    c = lax.broadcasted_iota(jnp.int32, (n, n), 1)
    return jnp.where(is_fwd, c, r) <= jnp.where(is_fwd, r, c)


def _scan_tile(d, j, nt):
    return jnp.where(d == 0, j, jnp.where(j == 0, 0, nt - j))


def _norm_proj_kernel(x_ref, g_ref, sh_ref, sc_ref, w_ref, b_ref, o_ref, h_ref):
    @pl.when(pl.program_id(2) == 0)
    def _():
        x = x_ref[0]
        ms = jnp.mean(x * x, axis=-1, keepdims=True)
        y = x * lax.rsqrt(ms + 1e-6) * g_ref[...]
        h_ref[...] = (y * (1.0 + sc_ref[0, 0]) + sh_ref[0, 0]).astype(BF16)

    o_ref[0] = _dot(h_ref[...], w_ref[...]) + b_ref[...]


def norm_proj(x, g, shift, scale, w, b, tn):
    bsz, t, d = x.shape
    n = w.shape[1]
    seg = lambda i, j, k: (i, jnp.minimum(j, 1), 0, 0)
    return pl.pallas_call(
        _norm_proj_kernel,
        grid=(bsz, t // TT, n // tn),
        in_specs=[
            pl.BlockSpec((1, TT, d), lambda i, j, k: (i, j, 0)),
            pl.BlockSpec((1, d), lambda i, j, k: (0, 0)),
            pl.BlockSpec((1, 1, 1, d), seg),
            pl.BlockSpec((1, 1, 1, d), seg),
            pl.BlockSpec((d, tn), lambda i, j, k: (0, k)),
            pl.BlockSpec((1, tn), lambda i, j, k: (0, k)),
        ],
        out_specs=pl.BlockSpec((1, TT, tn), lambda i, j, k: (i, j, k)),
        out_shape=jax.ShapeDtypeStruct((bsz, t, n), F32),
        scratch_shapes=[pltpu.VMEM((TT, d), BF16)],
        compiler_params=pltpu.CompilerParams(
            dimension_semantics=("parallel", "parallel", "arbitrary"), vmem_limit_bytes=VMEM_LIMIT),
        name="norm_proj",
    )(x, g.reshape(1, d), shift, scale, w, b.reshape(1, n))


def _conv3_piece(x, lh, rh, w, b, left_ok, right_ok):
    n = x.shape[0]
    row = lax.broadcasted_iota(jnp.int32, x.shape, 0)
    prev_row = jnp.where(left_ok, lh[7:8, :], 0.0)
    next_row = jnp.where(right_ok, rh[0:1, :], 0.0)
    x_prev = jnp.where(row == 0, prev_row, pltpu.roll(x, 1, 0))
    x_next = jnp.where(row == n - 1, next_row, pltpu.roll(x, n - 1, 0))
    return _silu(w[0:1, :] * x_prev + w[1:2, :] * x + w[2:3, :] * x_next + b)


def _ab_prep_kernel(xs_ref, bm_ref, cm_ref, xsl_ref, bml_ref, cml_ref, xsr_ref, bmr_ref, cmr_ref,
                    w_ref, b_ref, o_ref):
    j = pl.program_id(1)
    nt = pl.num_programs(1)
    left_ok = j >= 2
    right_ok = jnp.logical_and(j >= 1, j < nt - 1)
    w = w_ref[...]
    b = b_ref[...]
    o_ref[0, :, 0:SSD_INNER] = _conv3_piece(xs_ref[0], xsl_ref[0], xsr_ref[0], w[:, 0:SSD_INNER],
                                            b[:, 0:SSD_INNER], left_ok, right_ok)
    c0, c1 = SSD_INNER, SSD_INNER + SSD_BC
    o_ref[0, :, c0:c1] = _conv3_piece(bm_ref[0], bml_ref[0], bmr_ref[0], w[:, c0:c1], b[:, c0:c1], left_ok, right_ok)
    c0, c1 = c1, c1 + SSD_BC
    o_ref[0, :, c0:c1] = _conv3_piece(cm_ref[0], cml_ref[0], cmr_ref[0], w[:, c0:c1], b[:, c0:c1], left_ok, right_ok)


def ab_prep(proj, conv_w, conv_b):
    bsz, t, _ = proj.shape
    nt = t // TT
    rb = TT // 8
    nrb = t // 8
    cw = SSD_INNER + 2 * SSD_BC

    def cur(width, col):
        return pl.BlockSpec((1, TT, width), lambda i, j: (i, j, col // width))

    def left(width, col):
        return pl.BlockSpec((1, 8, width), lambda i, j: (i, jnp.maximum(j * rb - 1, 0), col // width))

    def right(width, col):
        return pl.BlockSpec((1, 8, width), lambda i, j: (i, jnp.minimum((j + 1) * rb, nrb - 1), col // width))

    pieces = ((SSD_INNER, AB_XS), (SSD_BC, AB_BM), (SSD_BC, AB_CM))
    return pl.pallas_call(
        _ab_prep_kernel,
        grid=(bsz, nt),
        in_specs=[cur(*p) for p in pieces] + [left(*p) for p in pieces] + [right(*p) for p in pieces] + [
            pl.BlockSpec((3, cw), lambda i, j: (0, 0)),
            pl.BlockSpec((1, cw), lambda i, j: (0, 0)),
        ],
        out_specs=pl.BlockSpec((1, TT, cw), lambda i, j: (i, j, 0)),
        out_shape=jax.ShapeDtypeStruct((bsz, t, cw), F32),
        compiler_params=pltpu.CompilerParams(dimension_semantics=("parallel", "parallel")),
        name="ab_prep",
    )(*([proj] * 9), conv_w, conv_b.reshape(1, cw))


def _gla_kernel(q_ref, k_ref, v_ref, sm_ref, wg_ref, bg_ref, o_ref, qg_s, kn_s, kd_s, egl_s, st_s):
    d = pl.program_id(2)
    j = pl.program_id(3)
    is_fwd = d == 0

    @pl.when(j == 0)
    def _():
        st_s[...] = jnp.zeros_like(st_s)

    gz = _dot(sm_ref[0].astype(BF16), wg_ref[0, 0]) + bg_ref[0, 0]
    logg = (jnp.minimum(gz, 0.0) - jnp.log(1.0 + jnp.exp(-jnp.abs(gz)))) * (1.0 / GLA_GATE_TAU)
    m_cum, m_all = _chunk_masks(is_fwd, TT)
    gc = _sel_dot(m_cum, logg)
    gl = _sel_dot(m_all, logg)
    q = q_ref[0] * (GLA_DK ** -0.5)
    k = k_ref[0]
    qg_s[...] = (q * jnp.exp(gc)).astype(BF16)
    kn_s[...] = (k * jnp.exp(-gc)).astype(BF16)
    kd_s[...] = (k * jnp.exp(gl - gc)).astype(BF16)
    egl_s[...] = jnp.exp(gl)
    causal = _causal_mask(is_fwd, CHUNK)

    for ci in range(CPT):
        off = pl.multiple_of(jnp.where(is_fwd, ci, CPT - 1 - ci) * CHUNK, CHUNK)
        rows = pl.ds(off, CHUNK)
        qg = qg_s[rows, :]
        v = v_ref[0, rows, :]
        st = st_s[...]
        att = jnp.where(causal, _dot_nt(qg, kn_s[rows, :]), 0.0)
        o_ref[0, 0, rows, :] = _dot(att.astype(BF16), v.astype(BF16)) + _dot_nt(qg, st.astype(BF16))
        st_s[...] = st * egl_s[pl.ds(off, 1), :] + _dot(v.T.astype(BF16), kd_s[rows, :])


def gla_scan(proj, wg, bg):
    bsz, t, _ = proj.shape
    nt = t // TT
    tile = lambda i, h, d, j: _scan_tile(d, j, nt)
    return pl.pallas_call(
        _gla_kernel,
        grid=(bsz, GLA_HEADS, 2, nt),
        in_specs=[
            pl.BlockSpec((1, TT, GLA_DK), lambda i, h, d, j: (i, tile(i, h, d, j), AB_Q // GLA_DK + h)),
            pl.BlockSpec((1, TT, GLA_DK), lambda i, h, d, j: (i, tile(i, h, d, j), AB_K // GLA_DK + h)),
            pl.BlockSpec((1, TT, GLA_DV), lambda i, h, d, j: (i, tile(i, h, d, j), AB_V // GLA_DV + h)),
            pl.BlockSpec((1, TT, LANE), lambda i, h, d, j: (i, tile(i, h, d, j), AB_SMALL // LANE)),
            pl.BlockSpec((1, 1, LANE, GLA_DK), lambda i, h, d, j: (d, h, 0, 0)),
            pl.BlockSpec((1, 1, 1, GLA_DK), lambda i, h, d, j: (d, h, 0, 0)),
        ],
        out_specs=pl.BlockSpec((1, 1, TT, GLA_DV), lambda i, h, d, j: (d, i, tile(i, h, d, j), h)),
        out_shape=jax.ShapeDtypeStruct((2, bsz, t, GLA_V), F32),
        scratch_shapes=[pltpu.VMEM((TT, GLA_DK), BF16), pltpu.VMEM((TT, GLA_DK), BF16),
                        pltpu.VMEM((TT, GLA_DK), BF16), pltpu.VMEM((TT, GLA_DK), F32),
                        pltpu.VMEM((GLA_DV, GLA_DK), F32)],
        compiler_params=pltpu.CompilerParams(
            dimension_semantics=("parallel", "parallel", "parallel", "arbitrary")),
        name="gla_scan",
    )(proj, proj, proj, proj, wg, bg)


def _ssd_kernel(xs_ref, bm_ref, cm_ref, sm_ref, dtb_ref, nega_ref, e_ref, e8_ref, o_ref,
                v_s, vw_s, cdec_s, dec_s, a8_s, st_s):
    d = pl.program_id(2)
    j = pl.program_id(3)
    is_fwd = d == 0

    @pl.when(j == 0)
    def _():
        st_s[...] = jnp.zeros_like(st_s)

    dt = _softplus(sm_ref[0] + dtb_ref[0])
    la = dt * nega_ref[0]
    m_cum, m_all = _chunk_masks(is_fwd, TT)
    acum = _sel_dot(m_cum, la)
    atot = _sel_dot(m_all, la)
    e = e_ref[0, 0]
    acum_e = _dot_sel(acum, e)
    atot_e = _dot_sel(atot, e)
    v = xs_ref[0] * _dot_sel(dt, e)
    v_s[...] = v.astype(BF16)
    vw_s[...] = (v * jnp.exp(atot_e - acum_e)).astype(BF16)
    cdec_s[...] = jnp.exp(acum_e)
    dec_s[...] = jnp.exp(atot_e)
    a8_s[...] = _dot_sel(acum, e8_ref[0, 0])
    causal = _causal_mask(is_fwd, CHUNK)

    for ci in range(CPT):
        off = pl.multiple_of(jnp.where(is_fwd, ci, CPT - 1 - ci) * CHUNK, CHUNK)
        rows = pl.ds(off, CHUNK)
        bm = bm_ref[0, rows, :]
        cm = cm_ref[0, rows, :].astype(BF16)
        st = st_s[...]
        cb = _dot_nt(cm, bm.astype(BF16))
        y_inter = _dot(cm, st.astype(BF16)) * cdec_s[rows, :]
        a8 = a8_s[rows, :]
        a8t = a8.T
        v_c = v_s[rows, :]
        ys = []
        for hh in range(SSD_HPG):
            diff = a8[:, hh:hh + 1] - a8t[hh:hh + 1, :]
            seg = jnp.exp(jnp.where(causal, diff, -1e30))
            ys.append(_dot((seg * cb).astype(BF16), v_c[:, hh * SSD_HEADDIM:(hh + 1) * SSD_HEADDIM]))
        o_ref[0, 0, rows, :] = jnp.concatenate(ys, axis=1) + y_inter
        st_s[...] = st * dec_s[pl.ds(off, 1), :] + _dot(bm.T.astype(BF16), vw_s[rows, :])


def ssd_scan(xbc, proj, dtb, nega, e, e8):
    bsz, t, _ = xbc.shape
    nt = t // TT
    gw = SSD_HPG * SSD_HEADDIM
    tile = lambda d, j: _scan_tile(d, j, nt)
    return pl.pallas_call(
        _ssd_kernel,
        grid=(bsz, SSD_GROUPS, 2, nt),
        in_specs=[
            pl.BlockSpec((1, TT, gw), lambda i, g, d, j: (i, tile(d, j), g)),
            pl.BlockSpec((1, TT, SSD_STATE), lambda i, g, d, j: (i, tile(d, j), SSD_INNER // SSD_STATE + g)),
            pl.BlockSpec((1, TT, SSD_STATE), lambda i, g, d, j: (i, tile(d, j), (SSD_INNER + SSD_BC) // SSD_STATE + g)),
            pl.BlockSpec((1, TT, LANE), lambda i, g, d, j: (i, tile(d, j), AB_SMALL // LANE)),
            pl.BlockSpec((1, 1, LANE), lambda i, g, d, j: (d, 0, 0)),
            pl.BlockSpec((1, 1, LANE), lambda i, g, d, j: (d, 0, 0)),
            pl.BlockSpec((1, 1, LANE, gw), lambda i, g, d, j: (d, g, 0, 0)),
            pl.BlockSpec((1, 1, LANE, LANE), lambda i, g, d, j: (d, g, 0, 0)),
        ],
        out_specs=pl.BlockSpec((1, 1, TT, gw), lambda i, g, d, j: (d, i, tile(d, j), g)),
        out_shape=jax.ShapeDtypeStruct((2, bsz, t, SSD_INNER), F32),
        scratch_shapes=[pltpu.VMEM((TT, gw), BF16), pltpu.VMEM((TT, gw), BF16),
                        pltpu.VMEM((TT, gw), F32), pltpu.VMEM((TT, gw), F32),
                        pltpu.VMEM((TT, LANE), F32), pltpu.VMEM((SSD_STATE, gw), F32)],
        compiler_params=pltpu.CompilerParams(
            dimension_semantics=("parallel", "parallel", "parallel", "arbitrary")),
        name="ssd_scan",
    )(xbc, xbc, xbc, proj, dtb, nega, e, e8)


def _group_rmsnorm(x, width):
    parts = []
    for s in range(x.shape[1] // width):
        seg = x[:, s * width:(s + 1) * width]
        parts.append(seg * lax.rsqrt(jnp.mean(seg * seg, axis=-1, keepdims=True) + 1e-6))
    return jnp.concatenate(parts, axis=1)


def _ab_out_kernel(of_ref, ob_ref, yf_ref, yb_ref, r_ref, z_ref, xs_ref, gg_ref, dv_ref, sg_ref,
                   w_ref, b_ref, x_ref, gate_ref, o_ref):
    o = _group_rmsnorm(of_ref[0, 0] + ob_ref[0, 0], GLA_DV) * gg_ref[...] * _silu(r_ref[0])
    y = (yf_ref[0, 0] + yb_ref[0, 0] + dv_ref[...] * xs_ref[0]) * _silu(z_ref[0])
    y = _group_rmsnorm(y, SSD_INNER // SSD_GROUPS) * sg_ref[...]
    m = _dot(o.astype(BF16), w_ref[0:GLA_V, :]) + _dot(y.astype(BF16), w_ref[GLA_V:, :]) + b_ref[...]
    o_ref[0] = x_ref[0] + gate_ref[0, 0] * m


def ab_out(o_gla, y_ssd, proj, xbc, gla_g, d_vec, ssd_g, w, b, x, gate):
    bsz, t, d = x.shape
    seg = lambda i, j: (i, jnp.minimum(j, 1), 0, 0)
    row = lambda width: pl.BlockSpec((1, width), lambda i, j: (0, 0))
    return pl.pallas_call(
        _ab_out_kernel,
        grid=(bsz, t // TT),
        in_specs=[
            pl.BlockSpec((1, 1, TT, GLA_V), lambda i, j: (0, i, j, 0)),
            pl.BlockSpec((1, 1, TT, GLA_V), lambda i, j: (1, i, j, 0)),
            pl.BlockSpec((1, 1, TT, SSD_INNER), lambda i, j: (0, i, j, 0)),
            pl.BlockSpec((1, 1, TT, SSD_INNER), lambda i, j: (1, i, j, 0)),
            pl.BlockSpec((1, TT, GLA_V), lambda i, j: (i, j, AB_R // GLA_V)),
            pl.BlockSpec((1, TT, SSD_INNER), lambda i, j: (i, j, AB_Z // SSD_INNER)),
            pl.BlockSpec((1, TT, SSD_INNER), lambda i, j: (i, j, 0)),
            row(GLA_V), row(SSD_INNER), row(SSD_INNER),
            pl.BlockSpec((GLA_V + SSD_INNER, d), lambda i, j: (0, 0)),
            row(d),
            pl.BlockSpec((1, TT, d), lambda i, j: (i, j, 0)),
            pl.BlockSpec((1, 1, 1, d), seg),
        ],
        out_specs=pl.BlockSpec((1, TT, d), lambda i, j: (i, j, 0)),
        out_shape=jax.ShapeDtypeStruct((bsz, t, d), F32),
        compiler_params=pltpu.CompilerParams(
            dimension_semantics=("parallel", "parallel"), vmem_limit_bytes=VMEM_LIMIT),
        name="ab_out",
    )(o_gla, o_gla, y_ssd, y_ssd, proj, proj, xbc, gla_g.reshape(1, -1), d_vec.reshape(1, -1),
      ssd_g.reshape(1, -1), w, b.reshape(1, d), x, gate)


def _cd_prep_kernel(q_ref, k_ref, v_ref, ql_ref, kl_ref, vl_ref, qr_ref, kr_ref, vr_ref, w_ref, o_ref):
    j = pl.program_id(1)
    nt = pl.num_programs(1)
    left_ok = j >= 2
    right_ok = jnp.logical_and(j >= 1, j < nt - 1)
    w = w_ref[...]
    srcs = ((q_ref, ql_ref, qr_ref, GDN_DK ** -0.5), (k_ref, kl_ref, kr_ref, 1.0), (v_ref, vl_ref, vr_ref, None))
    for s, (c_ref, l_ref, r_ref, scale) in enumerate(srcs):
        c0 = s * GDN_QK
        y = _conv3_piece(c_ref[0], l_ref[0], r_ref[0], w[:, c0:c0 + GDN_QK], 0.0, left_ok, right_ok)
        if scale is None:
            o_ref[0, :, c0:c0 + GDN_QK] = y
            continue
        for h in range(GDN_HEADS):
            seg = y[:, h * GDN_DK:(h + 1) * GDN_DK]
            inv = lax.rsqrt(jnp.sum(seg * seg, axis=-1, keepdims=True) + 1e-6) * scale
            o_ref[0, :, c0 + h * GDN_DK:c0 + (h + 1) * GDN_DK] = seg * inv


def cd_prep(proj, conv_w):
    bsz, t, _ = proj.shape
    nt = t // TT
    rb = TT // 8
    nrb = t // 8
    width = GDN_QK
    cols = (2, 3, 4)

    cur = lambda cb: pl.BlockSpec((1, TT, width), lambda i, j: (i, j, cb))
    left = lambda cb: pl.BlockSpec((1, 8, width), lambda i, j: (i, jnp.maximum(j * rb - 1, 0), cb))
    right = lambda cb: pl.BlockSpec((1, 8, width), lambda i, j: (i, jnp.minimum((j + 1) * rb, nrb - 1), cb))
    return pl.pallas_call(
        _cd_prep_kernel,
        grid=(bsz, nt),
        in_specs=[cur(cb) for cb in cols] + [left(cb) for cb in cols] + [right(cb) for cb in cols] + [
            pl.BlockSpec((3, 3 * width), lambda i, j: (0, 0))],
        out_specs=pl.BlockSpec((1, TT, 3 * width), lambda i, j: (i, j, 0)),
        out_shape=jax.ShapeDtypeStruct((bsz, t, 3 * width), F32),
        compiler_params=pltpu.CompilerParams(dimension_semantics=("parallel", "parallel")),
        name="cd_prep",
    )(*([proj] * 9), conv_w)


GDN_HB = 4


def _mm2(a, b):
    return _dot(a.astype(BF16), b.astype(BF16))


def _unit_tri_inverse(mats, b16, b32, eye):
    each = lambda f, *ls: [f(*xs) for xs in zip(*ls)]
    d16 = each(lambda a: jnp.where(b16, a, 0.0), mats)
    d2 = each(lambda x: _mm2(x, x), d16)
    d4 = each(lambda x: _mm2(x, x), d2)
    d8 = each(lambda x: _mm2(x, x), d4)
    t = each(lambda x: eye - x, d16)
    for p in (d2, d4, d8):
        t = each(lambda x, y: x + _mm2(x, y), t, p)
    off32 = jnp.logical_and(b32, jnp.logical_not(b16))
    for sel in (off32, jnp.logical_not(b32)):
        a_off = each(lambda a: jnp.where(sel, a, 0.0), mats)
        inner = each(_mm2, a_off, t)
        t = each(lambda x, y: x - _mm2(x, y), t, inner)
    return t


def _gdn_kernel(q_ref, k_ref, v_ref, sm_ref, dtb_ref, nega_ref, ea_ref, eb_ref, o_ref,
                u_s, w_s, kd_s, qg_s, aqk_s, dec_s, st_s):
    d = pl.program_id(2)
    j = pl.program_id(3)
    is_fwd = d == 0

    @pl.when(j == 0)
    def _():
        st_s[...] = jnp.zeros_like(st_s)

    sm = sm_ref[0]
    m_cum, m_all = _chunk_masks(is_fwd, TT)
    la = _softplus(sm + dtb_ref[...]) * nega_ref[...]
    gc_sm = _sel_dot(m_cum, la)
    gl_sm = _sel_dot(m_all, la)
    be_sm = jax.nn.sigmoid(sm)

    r = lax.broadcasted_iota(jnp.int32, (CHUNK, CHUNK), 0)
    c = lax.broadcasted_iota(jnp.int32, (CHUNK, CHUNK), 1)
    causal = _causal_mask(is_fwd, CHUNK)
    strict = jnp.logical_and(causal, r != c)
    b16 = lax.shift_right_logical(r, 4) == lax.shift_right_logical(c, 4)
    b32 = lax.shift_right_logical(r, 5) == lax.shift_right_logical(c, 5)
    eye = jnp.where(r == c, 1.0, 0.0)
    chunk_rows = [slice(ci * CHUNK, (ci + 1) * CHUNK) for ci in range(CPT)]

    amats, rhss = [], []
    for hh in range(GDN_HB):
        cols = slice(hh * GDN_DK, (hh + 1) * GDN_DK)
        gc = _dot_sel(gc_sm, ea_ref[0, hh])
        gl = _dot_sel(gl_sm, ea_ref[0, hh])
        beta_e = _dot_sel(be_sm, eb_ref[0, hh])
        q = q_ref[0, :, cols]
        k = k_ref[0, :, cols]
        egc = jnp.exp(gc)
        kb = k * beta_e
        qg_s[:, cols] = (q * egc).astype(BF16)
        kd_s[:, cols] = k * jnp.exp(gl - gc)
        dec_s[:, cols] = jnp.exp(gl)
        rhs = jnp.concatenate([v_ref[0, :, cols] * beta_e, kb * egc], axis=1)
        for ci, rows in enumerate(chunk_rows):
            gcc = gc[rows, :]
            dmat = jnp.exp(jnp.where(causal, gcc[:, 0:CHUNK] - gcc.T[0:CHUNK, :], -1e30))
            kc = k[rows].astype(BF16)
            amats.append(jnp.where(strict, _dot_nt(kb[rows].astype(BF16), kc) * dmat, 0.0))
            rhss.append(rhs[rows])
            aqk_s[rows, hh * CHUNK:(hh + 1) * CHUNK] = (_dot_nt(q[rows].astype(BF16), kc) * dmat).astype(BF16)
    tinv = _unit_tri_inverse(amats, b16, b32, eye)
    for n, (t, rhs_c) in enumerate(zip(tinv, rhss)):
        hh, rows = n // CPT, chunk_rows[n % CPT]
        cols = slice(hh * GDN_DK, (hh + 1) * GDN_DK)
        sol = _mm2(t, rhs_c)
        u_s[rows, cols] = sol[:, 0:GDN_DV]
        w_s[rows, cols] = sol[:, GDN_DV:].astype(BF16)

    for ci in range(CPT):
        off = pl.multiple_of(jnp.where(is_fwd, ci, CPT - 1 - ci) * CHUNK, CHUNK)
        rows = pl.ds(off, CHUNK)
        for hh in range(GDN_HB):
            cols = slice(hh * GDN_DK, (hh + 1) * GDN_DK)
            st = st_s[hh]
            stb = st.astype(BF16)
            vn = u_s[rows, cols] - _dot(w_s[rows, cols], stb)
            vnb = vn.astype(BF16)
            o_ref[0, 0, rows, cols] = (_dot(qg_s[rows, cols], stb)
                                       + _dot(aqk_s[rows, hh * CHUNK:(hh + 1) * CHUNK], vnb))
            st_s[hh] = st * dec_s[pl.ds(off, 1), cols] + _dot(kd_s[rows, cols].T.astype(BF16), vnb)


def gdn_scan(qkv, proj, dtb, nega, ea, eb):
    bsz, t, _ = qkv.shape
    nt = t // TT
    tile = lambda d, j: _scan_tile(d, j, nt)
    ng = GDN_HEADS // GDN_HB
    wb = GDN_HB * GDN_DK
    return pl.pallas_call(
        _gdn_kernel,
        grid=(bsz, ng, 2, nt),
        in_specs=[
            pl.BlockSpec((1, TT, wb), lambda i, h, d, j: (i, tile(d, j), h)),
            pl.BlockSpec((1, TT, wb), lambda i, h, d, j: (i, tile(d, j), ng + h)),
            pl.BlockSpec((1, TT, wb), lambda i, h, d, j: (i, tile(d, j), 2 * ng + h)),
            pl.BlockSpec((1, TT, LANE), lambda i, h, d, j: (i, tile(d, j), 6 * D_MODEL // LANE)),
            pl.BlockSpec((1, LANE), lambda i, h, d, j: (0, 0)),
            pl.BlockSpec((1, LANE), lambda i, h, d, j: (0, 0)),
            pl.BlockSpec((1, GDN_HB, LANE, LANE), lambda i, h, d, j: (d, h, 0, 0)),
            pl.BlockSpec((1, GDN_HB, LANE, LANE), lambda i, h, d, j: (d, h, 0, 0)),
        ],
        out_specs=pl.BlockSpec((1, 1, TT, wb), lambda i, h, d, j: (d, i, tile(d, j), h)),
        out_shape=jax.ShapeDtypeStruct((2, bsz, t, GDN_V), F32),
        scratch_shapes=[pltpu.VMEM((TT, wb), F32), pltpu.VMEM((TT, wb), BF16),
                        pltpu.VMEM((TT, wb), F32), pltpu.VMEM((TT, wb), BF16),
                        pltpu.VMEM((TT, GDN_HB * CHUNK), BF16), pltpu.VMEM((TT, wb), F32),
                        pltpu.VMEM((GDN_HB, GDN_DK, GDN_DV), F32)],
        compiler_params=pltpu.CompilerParams(
            dimension_semantics=("parallel", "parallel", "parallel", "arbitrary")),
        name="gdn_scan",
    )(qkv, qkv, qkv, proj, dtb, nega, ea, eb)


CONF_PAD = 16


def _cd_out_kernel(ga_ref, gb_ref, og_ref, of_ref, ob_ref, cw_ref, cb_ref, lg_ref, lb_ref, ng_ref,
                   w_ref, b_ref, x_ref, gate_ref, o_ref, pad_s):
    j = pl.program_id(1)
    seglen = jnp.where(j == 0, TT, GRID_W)
    half = (CONF_KERNEL - 1) // 2
    pad_s[0:CONF_PAD, :] = jnp.zeros((CONF_PAD, CONF_CH), F32)
    pad_s[CONF_PAD + TT:, :] = jnp.zeros((CONF_PAD, CONF_CH), F32)
    pad_s[CONF_PAD:CONF_PAD + TT, :] = ga_ref[0] * jax.nn.sigmoid(gb_ref[0])
    pos = jnp.bitwise_and(lax.broadcasted_iota(jnp.int32, (TT, 1), 0), seglen - 1)
    acc = jnp.zeros((TT, CONF_CH), F32) + cb_ref[...]
    for kk in range(CONF_KERNEL):
        s = kk - half
        inside = jnp.logical_and(pos + s >= 0, pos + s < seglen)
        acc = acc + pad_s[CONF_PAD + s:CONF_PAD + s + TT, :] * jnp.where(inside, 1.0, 0.0) * cw_ref[kk:kk + 1, :]
    mu = jnp.mean(acc, axis=-1, keepdims=True)
    cen = acc - mu
    var = jnp.mean(cen * cen, axis=-1, keepdims=True)
    conv = _silu(cen * lax.rsqrt(var + 1e-5) * lg_ref[...] + lb_ref[...])
    o = _group_rmsnorm(of_ref[0, 0] + ob_ref[0, 0], GDN_DV) * ng_ref[...] * _silu(og_ref[0])
    m = _dot(conv.astype(BF16), w_ref[0:CONF_CH, :]) + _dot(o.astype(BF16), w_ref[CONF_CH:, :]) + b_ref[...]
    o_ref[0] = x_ref[0] + gate_ref[0, 0] * m


def cd_out(o_gdn, proj, conv_w, conv_b, ln_g, ln_b, norm_g, w, b, x, gate):
    bsz, t, d = x.shape
    seg = lambda i, j: (i, jnp.minimum(j, 1), 0, 0)
    row = lambda width: pl.BlockSpec((1, width), lambda i, j: (0, 0))
    return pl.pallas_call(
        _cd_out_kernel,
        grid=(bsz, t // TT),
        in_specs=[
            pl.BlockSpec((1, TT, CONF_CH), lambda i, j: (i, j, 0)),
            pl.BlockSpec((1, TT, CONF_CH), lambda i, j: (i, j, 1)),
            pl.BlockSpec((1, TT, GDN_V), lambda i, j: (i, j, 5)),
            pl.BlockSpec((1, 1, TT, GDN_V), lambda i, j: (0, i, j, 0)),
            pl.BlockSpec((1, 1, TT, GDN_V), lambda i, j: (1, i, j, 0)),
            pl.BlockSpec((CONF_KERNEL, CONF_CH), lambda i, j: (0, 0)),
            row(CONF_CH), row(CONF_CH), row(CONF_CH), row(GDN_V),
            pl.BlockSpec((CONF_CH + GDN_V, d), lambda i, j: (0, 0)),
            row(d),
            pl.BlockSpec((1, TT, d), lambda i, j: (i, j, 0)),
            pl.BlockSpec((1, 1, 1, d), seg),
        ],
        out_specs=pl.BlockSpec((1, TT, d), lambda i, j: (i, j, 0)),
        out_shape=jax.ShapeDtypeStruct((bsz, t, d), F32),
        scratch_shapes=[pltpu.VMEM((TT + 2 * CONF_PAD, CONF_CH), F32)],
        compiler_params=pltpu.CompilerParams(
            dimension_semantics=("parallel", "parallel"), vmem_limit_bytes=VMEM_LIMIT),
        name="cd_out",
    )(proj, proj, proj, o_gdn, o_gdn, conv_w, conv_b.reshape(1, -1), ln_g.reshape(1, -1), ln_b.reshape(1, -1),
      norm_g.reshape(1, -1), w, b.reshape(1, d), x, gate)


def _out_proj_kernel(m_ref, w_ref, b_ref, x_ref, gate_ref, o_ref):
    y = _dot(m_ref[0].astype(BF16), w_ref[...]) + b_ref[...]
    o_ref[0] = x_ref[0] + gate_ref[0] * y


def out_proj_residual(mixed, w, b, x, gate):
    bsz, t, k = mixed.shape
    d = w.shape[1]
    tm = min(t, 512)
    return pl.pallas_call(
        _out_proj_kernel,
        grid=(bsz, t // tm),
        in_specs=[
            pl.BlockSpec((1, tm, k), lambda i, j: (i, j, 0)),
            pl.BlockSpec((k, d), lambda i, j: (0, 0)),
            pl.BlockSpec((1, d), lambda i, j: (0, 0)),
            pl.BlockSpec((1, tm, d), lambda i, j: (i, j, 0)),
            pl.BlockSpec((1, 1, d), lambda i, j: (i, 0, 0)),
        ],
        out_specs=pl.BlockSpec((1, tm, d), lambda i, j: (i, j, 0)),
        out_shape=jax.ShapeDtypeStruct((bsz, t, d), F32),
        compiler_params=pltpu.CompilerParams(
            dimension_semantics=("parallel", "parallel"), vmem_limit_bytes=VMEM_LIMIT),
        name="out_proj",
    )(mixed, w, b.reshape(1, d), x, gate.reshape(bsz, 1, d))


def _expert_ffn_kernel(x_ref, w1_ref, w3_ref, w2_ref, o_ref):
    x = x_ref[0, 0].astype(BF16)
    a = _dot(x, w1_ref[0])
    g = _dot(x, w3_ref[0])
    o_ref[0, 0] = _dot((_silu(a) * g).astype(BF16), w2_ref[0])


def expert_ffn(xe, w1, w3, w2):
    bsz, ne, cap, d = xe.shape
    f = w1.shape[2]
    tm = min(cap, 512)
    return pl.pallas_call(
        _expert_ffn_kernel,
        grid=(ne, bsz, cap // tm),
        in_specs=[
            pl.BlockSpec((1, 1, tm, d), lambda e, i, j: (i, e, j, 0)),
            pl.BlockSpec((1, d, f), lambda e, i, j: (e, 0, 0)),
            pl.BlockSpec((1, d, f), lambda e, i, j: (e, 0, 0)),
            pl.BlockSpec((1, f, d), lambda e, i, j: (e, 0, 0)),
        ],
        out_specs=pl.BlockSpec((1, 1, tm, d), lambda e, i, j: (i, e, j, 0)),
        out_shape=jax.ShapeDtypeStruct((bsz, ne, cap, d), F32),
        compiler_params=pltpu.CompilerParams(
            dimension_semantics=("parallel", "parallel", "parallel"), vmem_limit_bytes=VMEM_LIMIT),
        name="expert_ffn",
    )(xe, w1, w3, w2)


def _rmsnorm_kernel(x_ref, g_ref, o_ref):
    x = x_ref[0]
    ms = jnp.mean(x * x, axis=-1, keepdims=True)
    o_ref[0] = x * lax.rsqrt(ms + 1e-6) * g_ref[...]


def rmsnorm_rows(x, g):
    bsz, t, d = x.shape
    tm = min(t, 1024)
    return pl.pallas_call(
        _rmsnorm_kernel,
        grid=(bsz, t // tm),
        in_specs=[pl.BlockSpec((1, tm, d), lambda i, j: (i, j, 0)), pl.BlockSpec((1, d), lambda i, j: (0, 0))],
        out_specs=pl.BlockSpec((1, tm, d), lambda i, j: (i, j, 0)),
        out_shape=jax.ShapeDtypeStruct((bsz, t, d), F32),
        compiler_params=pltpu.CompilerParams(dimension_semantics=("parallel", "parallel")),
        name="final_rmsnorm",
    )(x, g.reshape(1, d))


def _ab_in_layout(w_in, b_in):
    q, k, v, r, glr, z, xs, bm, cm, dt = _split_cols(
        jnp.concatenate([w_in, b_in[None]], axis=0),
        (GLA_QK, GLA_QK, GLA_V, GLA_V, 2 * GLA_GATE_RANK, SSD_INNER, SSD_INNER, SSD_BC, SSD_BC, 2 * SSD_HEADS))
    wb = _pad_cols(jnp.concatenate([q, k, v, r, z, xs, bm, cm, glr, dt], axis=1), AB_N)
    return wb[:-1].astype(BF16), wb[-1]


def _gla_gate_params(w_gate2, b_gate2):
    wg = jnp.zeros((2, GLA_HEADS, LANE, GLA_DK), F32)
    for d in range(2):
        blk = w_gate2[d].reshape(GLA_GATE_RANK, GLA_HEADS, GLA_DK).transpose(1, 0, 2)
        wg = wg.at[d, :, d * GLA_GATE_RANK:(d + 1) * GLA_GATE_RANK, :].set(blk)
    return wg.astype(BF16), b_gate2.reshape(2, GLA_HEADS, 1, GLA_DK)


def _ssd_params(dt_bias, a_log):
    gw = SSD_HPG * SSD_HEADDIM
    dtb = jnp.zeros((2, 1, LANE), F32)
    nega = jnp.zeros((2, 1, LANE), F32)
    e = np.zeros((2, SSD_GROUPS, LANE, gw), np.float32)
    e8 = np.zeros((2, SSD_GROUPS, LANE, LANE), np.float32)
    for d in range(2):
        c0 = SM_DT + d * SSD_HEADS
        dtb = dtb.at[d, 0, c0:c0 + SSD_HEADS].set(dt_bias[d])
        nega = nega.at[d, 0, c0:c0 + SSD_HEADS].set(-jnp.exp(a_log[d]))
        for g in range(SSD_GROUPS):
            for hh in range(SSD_HPG):
                c = c0 + g * SSD_HPG + hh
                e[d, g, c, hh * SSD_HEADDIM:(hh + 1) * SSD_HEADDIM] = 1.0
                e8[d, g, c, hh] = 1.0
    return dtb, nega, jnp.asarray(e, BF16), jnp.asarray(e8, BF16)


def _gdn_params(dt_bias, a_log):
    n = 2 * GDN_HEADS
    dtb = jnp.zeros((1, LANE), F32).at[0, 0:n].set(dt_bias.reshape(n))
    nega = jnp.zeros((1, LANE), F32).at[0, 0:n].set(-jnp.exp(a_log.reshape(n)))
    ea = np.zeros((2, GDN_HEADS, LANE, LANE), np.float32)
    eb = np.zeros((2, GDN_HEADS, LANE, LANE), np.float32)
    for d in range(2):
        for h in range(GDN_HEADS):
            ea[d, h, d * GDN_HEADS + h, :] = 1.0
            eb[d, h, n + d * GDN_HEADS + h, :] = 1.0
    return dtb, nega, jnp.asarray(ea, BF16), jnp.asarray(eb, BF16)


def _rmsnorm(x, g, eps=1e-6):
    return x * lax.rsqrt(jnp.mean(jnp.square(x), axis=-1, keepdims=True) + eps) * g


def _head_rmsnorm(x, g):
    return _rmsnorm(x, g.reshape(x.shape[-2:]))


def _layernorm(x, g, b, eps=1e-5):
    mu = jnp.mean(x, axis=-1, keepdims=True)
    var = jnp.mean(jnp.square(x - mu), axis=-1, keepdims=True)
    return (x - mu) * lax.rsqrt(var + eps) * g + b


def _l2norm(x, eps=1e-6):
    return x * lax.rsqrt(jnp.sum(jnp.square(x), axis=-1, keepdims=True) + eps)


def _dwconv(x, w, b=None):
    k, ch = w.shape
    pad = (k - 1) // 2
    y = lax.conv_general_dilated(x, w[:, None, :], (1,), [(pad, pad)],
                                 dimension_numbers=('NWC', 'WIO', 'NWC'), feature_group_count=ch)
    return y if b is None else y + b


def _gdn_scan(q, k, v, beta, logg, s0, with_out):
    bsz, nh, t, dk = q.shape
    dv = v.shape[-1]
    nc = t // CHUNK
    chunks = lambda z: z.reshape(bsz, nh, nc, CHUNK, *z.shape[3:])
    q, k, v, beta, logg = (chunks(z) for z in (q, k, v, beta, logg))
    gc = jnp.cumsum(logg, axis=-1)
    glast = gc[..., -1]
    tril = jnp.tril(jnp.ones((CHUNK, CHUNK), dtype=bool))
    strict = jnp.tril(jnp.ones((CHUNK, CHUNK), dtype=bool), k=-1)
    decay = jnp.exp(jnp.where(tril, gc[..., :, None] - gc[..., None, :], -jnp.inf))
    kb = k * beta[..., None]
    m = jnp.eye(CHUNK, dtype=k.dtype) + jnp.where(strict, jnp.einsum('bhcid,bhcjd->bhcij', kb, k) * decay, 0.0)
    rhs = jnp.concatenate([v * beta[..., None], kb * jnp.exp(gc)[..., None]], axis=-1)
    sol = lax.linalg.triangular_solve(m, rhs, left_side=True, lower=True, unit_diagonal=True)
    u, w = sol[..., :dv], sol[..., dv:]
    kd = k * jnp.exp(glast[..., None] - gc)[..., None]
    dec = jnp.exp(glast)
    front = lambda z: jnp.moveaxis(z, 2, 0)

    def advance(s, w_c, u_c, kd_c, dec_c):
        vn = u_c - jnp.einsum('bhid,bhde->bhie', w_c, s)
        return vn, dec_c[..., None, None] * s + jnp.einsum('bhjd,bhje->bhde', kd_c, vn)

    xs = (front(w), front(u), front(kd), front(dec))
    if not with_out:
        s_fin, _ = lax.scan(lambda s, xc: (advance(s, *xc)[1], None), s0, xs)
        return None, s_fin
    aqk = jnp.einsum('bhcid,bhcjd->bhcij', q, k) * decay
    qg = q * jnp.exp(gc)[..., None]

    def step(s, xc):
        w_c, u_c, kd_c, dec_c, aqk_c, qg_c = xc
        vn, s_new = advance(s, w_c, u_c, kd_c, dec_c)
        o = jnp.einsum('bhid,bhde->bhie', qg_c, s) + jnp.einsum('bhij,bhje->bhie', aqk_c, vn)
        return s_new, o

    s_fin, o = lax.scan(step, s0, xs + (front(aqk), front(qg)))
    return jnp.moveaxis(o, 0, 2).reshape(bsz, nh, t, dv), s_fin


def _bidir_scan(scan_fn, args_f, args_b, t_axis, init_f, init_b, with_out):
    flip = lambda a: jnp.flip(a, axis=t_axis)
    o_f, s_f = scan_fn(*args_f, init_f, with_out)
    o_b, s_b = scan_fn(*[flip(a) for a in args_b], init_b, with_out)
    o = o_f + flip(o_b) if with_out else None
    return o, s_f, s_b


def _conformer_conv(u, rows, dw_w, dw_b, ln_g, ln_b):
    bsz, t, ch = u.shape
    y = _dwconv(u.reshape(bsz * rows, t // rows, ch), dw_w, dw_b).reshape(bsz, t, ch)
    return jax.nn.silu(_layernorm(y, ln_g, ln_b))


def _cd_stream(proj, gdn_conv_w, gdn_a_log, gdn_dt_bias):
    bsz, t, _ = proj.shape
    ga, gb, q, k, v, og, a_raw, b_raw = _split_cols(proj, CD_SPLITS)
    glu = ga * jax.nn.sigmoid(gb)
    qkv = jax.nn.silu(_dwconv(jnp.concatenate([q, k, v], axis=-1), gdn_conv_w))
    q, k, v = _split_cols(qkv, (GDN_QK, GDN_QK, GDN_V))
    heads = lambda a: a.reshape(bsz, t, GDN_HEADS, -1).transpose(0, 2, 1, 3)
    q = _l2norm(heads(q)) * GDN_DK ** -0.5
    k = _l2norm(heads(k))
    v = heads(v)
    beta = jax.nn.sigmoid(b_raw.reshape(bsz, t, 2, GDN_HEADS)).transpose(2, 0, 3, 1)
    logg = (-jnp.exp(gdn_a_log)
            * jax.nn.softplus(a_raw.reshape(bsz, t, 2, GDN_HEADS) + gdn_dt_bias)).transpose(2, 0, 3, 1)
    return (q, k, v, beta[0], logg[0]), (q, k, v, beta[1], logg[1]), glu, og


def _mixer_cd(proj_l, proj_c, rows, conf_dw_w, conf_dw_b, conf_ln_g, conf_ln_b,
              gdn_conv_w, gdn_a_log, gdn_dt_bias, gdn_norm_g, need_ctx):
    c_f, c_b, c_glu, c_og = _cd_stream(proj_c, gdn_conv_w, gdn_a_log, gdn_dt_bias)
    l_f, l_b, l_glu, l_og = _cd_stream(proj_l, gdn_conv_w, gdn_a_log, gdn_dt_bias)
    bsz = proj_l.shape[0]
    z0 = jnp.zeros((bsz, GDN_HEADS, GDN_DK, GDN_DV), F32)
    od_c, s_f, s_b = _bidir_scan(_gdn_scan, c_f, c_b, 2, z0, z0, need_ctx)
    od_l, _, _ = _bidir_scan(_gdn_scan, l_f, l_b, 2, s_f, s_b, True)

    def mix(od, glu, og, n_rows):
        t = glu.shape[1]
        conv = _conformer_conv(glu, n_rows, conf_dw_w, conf_dw_b, conf_ln_g, conf_ln_b)
        o = _head_rmsnorm(jnp.swapaxes(od, 1, 2), gdn_norm_g) * jax.nn.silu(og).reshape(bsz, t, GDN_HEADS, GDN_DV)
        return jnp.concatenate([conv, o.reshape(bsz, t, GDN_V)], axis=-1)

    return mix(od_l, l_glu, l_og, rows), (mix(od_c, c_glu, c_og, 1) if need_ctx else None)


def _expert_choice_ffn(h, router_w, w1, w3, w2):
    bsz, n, d = h.shape
    cap = n * EC_CAPACITY // N_EXPERTS
    aff = jax.nn.softmax(h @ router_w, axis=-1)
    gates, idx = lax.top_k(jnp.swapaxes(aff, 1, 2), cap)
    idx = idx.reshape(bsz, N_EXPERTS * cap)
    xe = jnp.take_along_axis(h, idx[..., None], axis=1).reshape(bsz, N_EXPERTS, cap, d)
    ye = expert_ffn(xe, w1, w3, w2) * gates[..., None]
    return jnp.zeros_like(h).at[jnp.arange(bsz)[:, None], idx].add(ye.reshape(bsz, N_EXPERTS * cap, d))


def layer_mixer(i, j, xa, mods, p, last):
    sh1, sc1, g1 = (mods[:, :, s] for s in range(3))
    if i % 2 == 0:
        w_in, b_in = _ab_in_layout(p["ab_w_in"][j], p["ab_b_in"][j])
        proj = norm_proj(xa, p["norm1_g"][i], sh1, sc1, w_in, b_in, 1920)
        xbc = ab_prep(proj, p["ssd_conv_w"][j], p["ssd_conv_b"][j])
        wg, bg = _gla_gate_params(p["gla_w_gate2"][j], p["gla_b_gate2"][j])
        o_gla = gla_scan(proj, wg, bg)
        y_ssd = ssd_scan(xbc, proj, *_ssd_params(p["ssd_dt_bias"][j], p["ssd_a_log"][j]))
        return ab_out(o_gla, y_ssd, proj, xbc, p["gla_norm_g"][j], jnp.repeat(p["ssd_d"][j], SSD_HEADDIM),
                      p["ssd_norm_g"][j], p["ab_w_out"][j].astype(BF16), p["ab_b_out"][j], xa, g1)
    w_in = _pad_cols(p["cd_w_in"][j], CD_N).astype(BF16)
    b_in = _pad_cols(p["cd_b_in"][j], CD_N)
    proj = norm_proj(xa, p["norm1_g"][i], sh1, sc1, w_in, b_in, 1280)
    qkv = cd_prep(proj, p["gdn_conv_w"][j])
    o_gdn = gdn_scan(qkv, proj, *_gdn_params(p["gdn_dt_bias"][j], p["gdn_a_log"][j]))
    return cd_out(o_gdn, proj, p["conf_dw_w"][j], p["conf_dw_b"][j], p["conf_ln_g"][j], p["conf_ln_b"][j],
                  p["gdn_norm_g"][j], p["cd_w_out"][j].astype(BF16), p["cd_b_out"][j], xa, g1)


def kernel(x, c, ctx, c_ctx, mod_w, mod_b, norm1_g, norm2_g, ab_w_in, ab_b_in, ab_w_out, ab_b_out, gla_w_gate2, gla_b_gate2, gla_norm_g, ssd_conv_w, ssd_conv_b, ssd_dt_bias, ssd_a_log, ssd_d, ssd_norm_g, cd_w_in, cd_b_in, cd_w_out, cd_b_out, conf_dw_w, conf_dw_b, conf_ln_g, conf_ln_b, gdn_conv_w, gdn_a_log, gdn_dt_bias, gdn_norm_g, moe_router, moe_w1, moe_w3, moe_w2, final_norm_g):
    p = dict(locals())
    bsz, seq, d = x.shape
    assert ctx.shape[1] == TT and seq % TT == 0
    depth = mod_w.shape[0]
    xa = jnp.concatenate([ctx, x], axis=1)
    for i in range(depth):
        last = i == depth - 1
        mod_l = jax.nn.silu(c) @ mod_w[i] + mod_b[i]
        mod_c = jnp.broadcast_to(jax.nn.silu(c_ctx) @ mod_w[i] + mod_b[i], mod_l.shape)
        mods = jnp.stack([mod_c, mod_l], axis=1).reshape(bsz, 2, 6, 1, d)
        sh1, sc1, g1, sh2, sc2, g2 = (mods[:, :, s] for s in range(6))
        xa = layer_mixer(i, i // 2, xa, mods, p, last)
        w1, w3, w2 = moe_w1[i].astype(BF16), moe_w3[i].astype(BF16), moe_w2[i].astype(BF16)
        x_c, x_l = xa[:, :TT], xa[:, TT:]
        h = _rmsnorm(x_l, norm2_g[i]) * (1 + sc2[:, 1]) + sh2[:, 1]
        x_l = x_l + g2[:, 1] * _expert_choice_ffn(h, moe_router[i], w1, w3, w2)
        if not last:
            hc = _rmsnorm(x_c, norm2_g[i]) * (1 + sc2[:, 0]) + sh2[:, 0]
            x_c = x_c + g2[:, 0] * _expert_choice_ffn(hc, moe_router[i], w1, w3, w2)
        xa = jnp.concatenate([x_c, x_l], axis=1)
    return rmsnorm_rows(xa[:, TT:], final_norm_g)
```

```python
import functools

import jax
import jax.numpy as jnp
import numpy as np
from jax import lax
from jax.experimental import pallas as pl
from jax.experimental.pallas import tpu as pltpu

F32 = jnp.float32
BF16 = jnp.bfloat16

D_MODEL = 1024
GRID_W = 64
CHUNK = 64
GLA_HEADS, GLA_DK, GLA_DV, GLA_GATE_RANK, GLA_GATE_TAU = 4, 128, 256, 16, 16.0
SSD_HEADS, SSD_HEADDIM, SSD_STATE, SSD_GROUPS = 16, 64, 128, 2
CONF_CH, CONF_KERNEL = D_MODEL, 31
GDN_HEADS, GDN_DK, GDN_DV = 8, 128, 128
N_EXPERTS, EC_CAPACITY, EXPERT_FF = 16, 2, D_MODEL

GLA_QK = GLA_HEADS * GLA_DK
GLA_V = GLA_HEADS * GLA_DV
SSD_INNER = SSD_HEADS * SSD_HEADDIM
SSD_BC = SSD_GROUPS * SSD_STATE
SSD_HPG = SSD_HEADS // SSD_GROUPS
GDN_QK = GDN_HEADS * GDN_DK
GDN_V = GDN_HEADS * GDN_DV
CD_SPLITS = (CONF_CH, CONF_CH, GDN_QK, GDN_QK, GDN_V, GDN_V, 2 * GDN_HEADS, 2 * GDN_HEADS)

LANE = 128
TT = 256
CPT = TT // CHUNK
VMEM_LIMIT = 48 * 1024 * 1024

AB_Q, AB_K, AB_V, AB_R, AB_Z, AB_XS, AB_BM, AB_CM, AB_SMALL = 0, 512, 1024, 2048, 3072, 4096, 5120, 5376, 5632
AB_N = AB_SMALL + LANE
SM_DT = 2 * GLA_GATE_RANK
CD_N = 6400


def _split_cols(a, sizes):
    return jnp.split(a, np.cumsum(sizes)[:-1].tolist(), axis=-1)


def _pad_cols(a, n):
    return jnp.pad(a, [(0, 0)] * (a.ndim - 1) + [(0, n - a.shape[-1])])


def _dot(a, b):
    return jnp.dot(a, b, preferred_element_type=F32)


def _dot_nt(a, b):
    return lax.dot_general(a, b, (((1,), (1,)), ((), ())), preferred_element_type=F32)


def _split3(x):
    hi = x.astype(BF16)
    r = x - hi.astype(F32)
    mid = r.astype(BF16)
    lo = (r - mid.astype(F32)).astype(BF16)
    return hi, mid, lo


def _sel_dot(m, x):
    hi, mid, lo = _split3(x)
    return _dot(m, hi) + _dot(m, mid) + _dot(m, lo)


def _dot_sel(x, e):
    hi, mid, lo = _split3(x)
    return _dot(hi, e) + _dot(mid, e) + _dot(lo, e)


def _softplus(x):
    return jnp.maximum(x, 0.0) + jnp.log(1.0 + jnp.exp(-jnp.abs(x)))


def _silu(x):
    return x * jax.nn.sigmoid(x)


def _chunk_masks(is_fwd, n):
    r = lax.broadcasted_iota(jnp.int32, (n, n), 0)
    c = lax.broadcasted_iota(jnp.int32, (n, n), 1)
    same = lax.shift_right_logical(r, 6) == lax.shift_right_logical(c, 6)
    lo = jnp.where(is_fwd, c, r)
    hi = jnp.where(is_fwd, r, c)
    cum = jnp.logical_and(same, lo <= hi)
    return jnp.where(cum, 1.0, 0.0).astype(BF16), jnp.where(same, 1.0, 0.0).astype(BF16)


def _causal_mask(is_fwd, n):
    r = lax.broadcasted_iota(jnp.int32, (n, n), 0)
    c = lax.broadcasted_iota(jnp.int32, (n, n), 1)
    return jnp.where(is_fwd, c, r) <= jnp.where(is_fwd, r, c)


def _scan_tile(d, j, nt):
    return jnp.where(d == 0, j, jnp.where(j == 0, 0, nt - j))


def _norm_proj_kernel(x_ref, g_ref, sh_ref, sc_ref, w_ref, b_ref, o_ref, h_ref):
    @pl.when(pl.program_id(2) == 0)
    def _():
        x = x_ref[0]
        ms = jnp.mean(x * x, axis=-1, keepdims=True)
        y = x * lax.rsqrt(ms + 1e-6) * g_ref[...]
        h_ref[...] = (y * (1.0 + sc_ref[0, 0]) + sh_ref[0, 0]).astype(BF16)

    o_ref[0] = _dot(h_ref[...], w_ref[...]) + b_ref[...]


def norm_proj(x, g, shift, scale, w, b, tn):
    bsz, t, d = x.shape
    n = w.shape[1]
    seg = lambda i, j, k: (i, jnp.minimum(j, 1), 0, 0)
    return pl.pallas_call(
        _norm_proj_kernel,
        grid=(bsz, t // TT, n // tn),
        in_specs=[
            pl.BlockSpec((1, TT, d), lambda i, j, k: (i, j, 0)),
            pl.BlockSpec((1, d), lambda i, j, k: (0, 0)),
            pl.BlockSpec((1, 1, 1, d), seg),
            pl.BlockSpec((1, 1, 1, d), seg),
            pl.BlockSpec((d, tn), lambda i, j, k: (0, k)),
            pl.BlockSpec((1, tn), lambda i, j, k: (0, k)),
        ],
        out_specs=pl.BlockSpec((1, TT, tn), lambda i, j, k: (i, j, k)),
        out_shape=jax.ShapeDtypeStruct((bsz, t, n), F32),
        scratch_shapes=[pltpu.VMEM((TT, d), BF16)],
        compiler_params=pltpu.CompilerParams(
            dimension_semantics=("parallel", "parallel", "arbitrary"), vmem_limit_bytes=VMEM_LIMIT),
        name="norm_proj",
    )(x, g.reshape(1, d), shift, scale, w, b.reshape(1, n))


def _conv3_piece(x, lh, rh, w, b, left_ok, right_ok):
    n = x.shape[0]
    row = lax.broadcasted_iota(jnp.int32, x.shape, 0)
    prev_row = jnp.where(left_ok, lh[7:8, :], 0.0)
    next_row = jnp.where(right_ok, rh[0:1, :], 0.0)
    x_prev = jnp.where(row == 0, prev_row, pltpu.roll(x, 1, 0))
    x_next = jnp.where(row == n - 1, next_row, pltpu.roll(x, n - 1, 0))
    return _silu(w[0:1, :] * x_prev + w[1:2, :] * x + w[2:3, :] * x_next + b)


def _ab_prep_kernel(xs_ref, bm_ref, cm_ref, xsl_ref, bml_ref, cml_ref, xsr_ref, bmr_ref, cmr_ref,
                    w_ref, b_ref, o_ref):
    j = pl.program_id(1)
    nt = pl.num_programs(1)
    left_ok = j >= 2
    right_ok = jnp.logical_and(j >= 1, j < nt - 1)
    w = w_ref[...]
    b = b_ref[...]
    o_ref[0, :, 0:SSD_INNER] = _conv3_piece(xs_ref[0], xsl_ref[0], xsr_ref[0], w[:, 0:SSD_INNER],
                                            b[:, 0:SSD_INNER], left_ok, right_ok)
    c0, c1 = SSD_INNER, SSD_INNER + SSD_BC
    o_ref[0, :, c0:c1] = _conv3_piece(bm_ref[0], bml_ref[0], bmr_ref[0], w[:, c0:c1], b[:, c0:c1], left_ok, right_ok)
    c0, c1 = c1, c1 + SSD_BC
    o_ref[0, :, c0:c1] = _conv3_piece(cm_ref[0], cml_ref[0], cmr_ref[0], w[:, c0:c1], b[:, c0:c1], left_ok, right_ok)


def ab_prep(proj, conv_w, conv_b):
    bsz, t, _ = proj.shape
    nt = t // TT
    rb = TT // 8
    nrb = t // 8
    cw = SSD_INNER + 2 * SSD_BC

    def cur(width, col):
        return pl.BlockSpec((1, TT, width), lambda i, j: (i, j, col // width))

    def left(width, col):
        return pl.BlockSpec((1, 8, width), lambda i, j: (i, jnp.maximum(j * rb - 1, 0), col // width))

    def right(width, col):
        return pl.BlockSpec((1, 8, width), lambda i, j: (i, jnp.minimum((j + 1) * rb, nrb - 1), col // width))

    pieces = ((SSD_INNER, AB_XS), (SSD_BC, AB_BM), (SSD_BC, AB_CM))
    return pl.pallas_call(
        _ab_prep_kernel,
        grid=(bsz, nt),
        in_specs=[cur(*p) for p in pieces] + [left(*p) for p in pieces] + [right(*p) for p in pieces] + [
            pl.BlockSpec((3, cw), lambda i, j: (0, 0)),
            pl.BlockSpec((1, cw), lambda i, j: (0, 0)),
        ],
        out_specs=pl.BlockSpec((1, TT, cw), lambda i, j: (i, j, 0)),
        out_shape=jax.ShapeDtypeStruct((bsz, t, cw), F32),
        compiler_params=pltpu.CompilerParams(dimension_semantics=("parallel", "parallel")),
        name="ab_prep",
    )(*([proj] * 9), conv_w, conv_b.reshape(1, cw))


def _gla_kernel(q_ref, k_ref, v_ref, sm_ref, wg_ref, bg_ref, o_ref, qg_s, kn_s, kd_s, egl_s, st_s):
    d = pl.program_id(2)
    j = pl.program_id(3)
    is_fwd = d == 0

    @pl.when(j == 0)
    def _():
        st_s[...] = jnp.zeros_like(st_s)

    gz = _dot(sm_ref[0].astype(BF16), wg_ref[0, 0]) + bg_ref[0, 0]
    logg = (jnp.minimum(gz, 0.0) - jnp.log(1.0 + jnp.exp(-jnp.abs(gz)))) * (1.0 / GLA_GATE_TAU)
    m_cum, m_all = _chunk_masks(is_fwd, TT)
    gc = _sel_dot(m_cum, logg)
    gl = _sel_dot(m_all, logg)
    q = q_ref[0] * (GLA_DK ** -0.5)
    k = k_ref[0]
    qg_s[...] = (q * jnp.exp(gc)).astype(BF16)
    kn_s[...] = (k * jnp.exp(-gc)).astype(BF16)
    kd_s[...] = (k * jnp.exp(gl - gc)).astype(BF16)
    egl_s[...] = jnp.exp(gl)
    causal = _causal_mask(is_fwd, CHUNK)

    for ci in range(CPT):
        off = pl.multiple_of(jnp.where(is_fwd, ci, CPT - 1 - ci) * CHUNK, CHUNK)
        rows = pl.ds(off, CHUNK)
        qg = qg_s[rows, :]
        v = v_ref[0, rows, :]
        st = st_s[...]
        att = jnp.where(causal, _dot_nt(qg, kn_s[rows, :]), 0.0)
        o_ref[0, 0, rows, :] = _dot(att.astype(BF16), v.astype(BF16)) + _dot_nt(qg, st.astype(BF16))
        st_s[...] = st * egl_s[pl.ds(off, 1), :] + _dot(v.T.astype(BF16), kd_s[rows, :])


def gla_scan(proj, wg, bg):
    bsz, t, _ = proj.shape
    nt = t // TT
    tile = lambda i, h, d, j: _scan_tile(d, j, nt)
    return pl.pallas_call(
        _gla_kernel,
        grid=(bsz, GLA_HEADS, 2, nt),
        in_specs=[
            pl.BlockSpec((1, TT, GLA_DK), lambda i, h, d, j: (i, tile(i, h, d, j), AB_Q // GLA_DK + h)),
            pl.BlockSpec((1, TT, GLA_DK), lambda i, h, d, j: (i, tile(i, h, d, j), AB_K // GLA_DK + h)),
            pl.BlockSpec((1, TT, GLA_DV), lambda i, h, d, j: (i, tile(i, h, d, j), AB_V // GLA_DV + h)),
            pl.BlockSpec((1, TT, LANE), lambda i, h, d, j: (i, tile(i, h, d, j), AB_SMALL // LANE)),
            pl.BlockSpec((1, 1, LANE, GLA_DK), lambda i, h, d, j: (d, h, 0, 0)),
            pl.BlockSpec((1, 1, 1, GLA_DK), lambda i, h, d, j: (d, h, 0, 0)),
        ],
        out_specs=pl.BlockSpec((1, 1, TT, GLA_DV), lambda i, h, d, j: (d, i, tile(i, h, d, j), h)),
        out_shape=jax.ShapeDtypeStruct((2, bsz, t, GLA_V), F32),
        scratch_shapes=[pltpu.VMEM((TT, GLA_DK), BF16), pltpu.VMEM((TT, GLA_DK), BF16),
                        pltpu.VMEM((TT, GLA_DK), BF16), pltpu.VMEM((TT, GLA_DK), F32),
                        pltpu.VMEM((GLA_DV, GLA_DK), F32)],
        compiler_params=pltpu.CompilerParams(
            dimension_semantics=("parallel", "parallel", "parallel", "arbitrary")),
        name="gla_scan",
    )(proj, proj, proj, proj, wg, bg)


def _ssd_kernel(xs_ref, bm_ref, cm_ref, sm_ref, dtb_ref, nega_ref, e_ref, e8_ref, o_ref,
                v_s, vw_s, cdec_s, dec_s, a8_s, st_s):
    d = pl.program_id(2)
    j = pl.program_id(3)
    is_fwd = d == 0

    @pl.when(j == 0)
    def _():
        st_s[...] = jnp.zeros_like(st_s)

    dt = _softplus(sm_ref[0] + dtb_ref[0])
    la = dt * nega_ref[0]
    m_cum, m_all = _chunk_masks(is_fwd, TT)
    acum = _sel_dot(m_cum, la)
    atot = _sel_dot(m_all, la)
    e = e_ref[0, 0]
    acum_e = _dot_sel(acum, e)
    atot_e = _dot_sel(atot, e)
    v = xs_ref[0] * _dot_sel(dt, e)
    v_s[...] = v.astype(BF16)
    vw_s[...] = (v * jnp.exp(atot_e - acum_e)).astype(BF16)
    cdec_s[...] = jnp.exp(acum_e)
    dec_s[...] = jnp.exp(atot_e)
    a8_s[...] = _dot_sel(acum, e8_ref[0, 0])
    causal = _causal_mask(is_fwd, CHUNK)

    for ci in range(CPT):
        off = pl.multiple_of(jnp.where(is_fwd, ci, CPT - 1 - ci) * CHUNK, CHUNK)
        rows = pl.ds(off, CHUNK)
        bm = bm_ref[0, rows, :]
        cm = cm_ref[0, rows, :].astype(BF16)
        st = st_s[...]
        cb = _dot_nt(cm, bm.astype(BF16))
        y_inter = _dot(cm, st.astype(BF16)) * cdec_s[rows, :]
        a8 = a8_s[rows, :]
        a8t = a8.T
        v_c = v_s[rows, :]
        ys = []
        for hh in range(SSD_HPG):
            diff = a8[:, hh:hh + 1] - a8t[hh:hh + 1, :]
            seg = jnp.exp(jnp.where(causal, diff, -1e30))
            ys.append(_dot((seg * cb).astype(BF16), v_c[:, hh * SSD_HEADDIM:(hh + 1) * SSD_HEADDIM]))
        o_ref[0, 0, rows, :] = jnp.concatenate(ys, axis=1) + y_inter
        st_s[...] = st * dec_s[pl.ds(off, 1), :] + _dot(bm.T.astype(BF16), vw_s[rows, :])


def ssd_scan(xbc, proj, dtb, nega, e, e8):
    bsz, t, _ = xbc.shape
    nt = t // TT
    gw = SSD_HPG * SSD_HEADDIM
    tile = lambda d, j: _scan_tile(d, j, nt)
    return pl.pallas_call(
        _ssd_kernel,
        grid=(bsz, SSD_GROUPS, 2, nt),
        in_specs=[
            pl.BlockSpec((1, TT, gw), lambda i, g, d, j: (i, tile(d, j), g)),
            pl.BlockSpec((1, TT, SSD_STATE), lambda i, g, d, j: (i, tile(d, j), SSD_INNER // SSD_STATE + g)),
            pl.BlockSpec((1, TT, SSD_STATE), lambda i, g, d, j: (i, tile(d, j), (SSD_INNER + SSD_BC) // SSD_STATE + g)),
            pl.BlockSpec((1, TT, LANE), lambda i, g, d, j: (i, tile(d, j), AB_SMALL // LANE)),
            pl.BlockSpec((1, 1, LANE), lambda i, g, d, j: (d, 0, 0)),
            pl.BlockSpec((1, 1, LANE), lambda i, g, d, j: (d, 0, 0)),
            pl.BlockSpec((1, 1, LANE, gw), lambda i, g, d, j: (d, g, 0, 0)),
            pl.BlockSpec((1, 1, LANE, LANE), lambda i, g, d, j: (d, g, 0, 0)),
        ],
        out_specs=pl.BlockSpec((1, 1, TT, gw), lambda i, g, d, j: (d, i, tile(d, j), g)),
        out_shape=jax.ShapeDtypeStruct((2, bsz, t, SSD_INNER), F32),
        scratch_shapes=[pltpu.VMEM((TT, gw), BF16), pltpu.VMEM((TT, gw), BF16),
                        pltpu.VMEM((TT, gw), F32), pltpu.VMEM((TT, gw), F32),
                        pltpu.VMEM((TT, LANE), F32), pltpu.VMEM((SSD_STATE, gw), F32)],
        compiler_params=pltpu.CompilerParams(
            dimension_semantics=("parallel", "parallel", "parallel", "arbitrary")),
        name="ssd_scan",
    )(xbc, xbc, xbc, proj, dtb, nega, e, e8)


def _group_rmsnorm(x, width):
    parts = []
    for s in range(x.shape[1] // width):
        seg = x[:, s * width:(s + 1) * width]
        parts.append(seg * lax.rsqrt(jnp.mean(seg * seg, axis=-1, keepdims=True) + 1e-6))
    return jnp.concatenate(parts, axis=1)


def _ab_out_kernel(of_ref, ob_ref, yf_ref, yb_ref, r_ref, z_ref, xs_ref, gg_ref, dv_ref, sg_ref,
                   w_ref, b_ref, x_ref, gate_ref, o_ref):
    o = _group_rmsnorm(of_ref[0, 0] + ob_ref[0, 0], GLA_DV) * gg_ref[...] * _silu(r_ref[0])
    y = (yf_ref[0, 0] + yb_ref[0, 0] + dv_ref[...] * xs_ref[0]) * _silu(z_ref[0])
    y = _group_rmsnorm(y, SSD_INNER // SSD_GROUPS) * sg_ref[...]
    m = _dot(o.astype(BF16), w_ref[0:GLA_V, :]) + _dot(y.astype(BF16), w_ref[GLA_V:, :]) + b_ref[...]
    o_ref[0] = x_ref[0] + gate_ref[0, 0] * m


def ab_out(o_gla, y_ssd, proj, xbc, gla_g, d_vec, ssd_g, w, b, x, gate):
    bsz, t, d = x.shape
    seg = lambda i, j: (i, jnp.minimum(j, 1), 0, 0)
    row = lambda width: pl.BlockSpec((1, width), lambda i, j: (0, 0))
    return pl.pallas_call(
        _ab_out_kernel,
        grid=(bsz, t // TT),
        in_specs=[
            pl.BlockSpec((1, 1, TT, GLA_V), lambda i, j: (0, i, j, 0)),
            pl.BlockSpec((1, 1, TT, GLA_V), lambda i, j: (1, i, j, 0)),
            pl.BlockSpec((1, 1, TT, SSD_INNER), lambda i, j: (0, i, j, 0)),
            pl.BlockSpec((1, 1, TT, SSD_INNER), lambda i, j: (1, i, j, 0)),
            pl.BlockSpec((1, TT, GLA_V), lambda i, j: (i, j, AB_R // GLA_V)),
            pl.BlockSpec((1, TT, SSD_INNER), lambda i, j: (i, j, AB_Z // SSD_INNER)),
            pl.BlockSpec((1, TT, SSD_INNER), lambda i, j: (i, j, 0)),
            row(GLA_V), row(SSD_INNER), row(SSD_INNER),
            pl.BlockSpec((GLA_V + SSD_INNER, d), lambda i, j: (0, 0)),
            row(d),
            pl.BlockSpec((1, TT, d), lambda i, j: (i, j, 0)),
            pl.BlockSpec((1, 1, 1, d), seg),
        ],
        out_specs=pl.BlockSpec((1, TT, d), lambda i, j: (i, j, 0)),
        out_shape=jax.ShapeDtypeStruct((bsz, t, d), F32),
        compiler_params=pltpu.CompilerParams(
            dimension_semantics=("parallel", "parallel"), vmem_limit_bytes=VMEM_LIMIT),
        name="ab_out",
    )(o_gla, o_gla, y_ssd, y_ssd, proj, proj, xbc, gla_g.reshape(1, -1), d_vec.reshape(1, -1),
      ssd_g.reshape(1, -1), w, b.reshape(1, d), x, gate)


def _cd_prep_kernel(q_ref, k_ref, v_ref, ql_ref, kl_ref, vl_ref, qr_ref, kr_ref, vr_ref, w_ref, o_ref):
    j = pl.program_id(1)
    nt = pl.num_programs(1)
    left_ok = j >= 2
    right_ok = jnp.logical_and(j >= 1, j < nt - 1)
    w = w_ref[...]
    srcs = ((q_ref, ql_ref, qr_ref, GDN_DK ** -0.5), (k_ref, kl_ref, kr_ref, 1.0), (v_ref, vl_ref, vr_ref, None))
    for s, (c_ref, l_ref, r_ref, scale) in enumerate(srcs):
        c0 = s * GDN_QK
        y = _conv3_piece(c_ref[0], l_ref[0], r_ref[0], w[:, c0:c0 + GDN_QK], 0.0, left_ok, right_ok)
        if scale is None:
            o_ref[0, :, c0:c0 + GDN_QK] = y
            continue
        for h in range(GDN_HEADS):
            seg = y[:, h * GDN_DK:(h + 1) * GDN_DK]
            inv = lax.rsqrt(jnp.sum(seg * seg, axis=-1, keepdims=True) + 1e-6) * scale
            o_ref[0, :, c0 + h * GDN_DK:c0 + (h + 1) * GDN_DK] = seg * inv


def cd_prep(proj, conv_w):
    bsz, t, _ = proj.shape
    nt = t // TT
    rb = TT // 8
    nrb = t // 8
    width = GDN_QK
    cols = (2, 3, 4)

    cur = lambda cb: pl.BlockSpec((1, TT, width), lambda i, j: (i, j, cb))
    left = lambda cb: pl.BlockSpec((1, 8, width), lambda i, j: (i, jnp.maximum(j * rb - 1, 0), cb))
    right = lambda cb: pl.BlockSpec((1, 8, width), lambda i, j: (i, jnp.minimum((j + 1) * rb, nrb - 1), cb))
    return pl.pallas_call(
        _cd_prep_kernel,
        grid=(bsz, nt),
        in_specs=[cur(cb) for cb in cols] + [left(cb) for cb in cols] + [right(cb) for cb in cols] + [
            pl.BlockSpec((3, 3 * width), lambda i, j: (0, 0))],
        out_specs=pl.BlockSpec((1, TT, 3 * width), lambda i, j: (i, j, 0)),
        out_shape=jax.ShapeDtypeStruct((bsz, t, 3 * width), F32),
        compiler_params=pltpu.CompilerParams(dimension_semantics=("parallel", "parallel")),
        name="cd_prep",
    )(*([proj] * 9), conv_w)


GDN_HB = 4


def _mm2(a, b):
    return _dot(a.astype(BF16), b.astype(BF16))


def _unit_tri_inverse(mats, b16, b32, eye):
    each = lambda f, *ls: [f(*xs) for xs in zip(*ls)]
    d16 = each(lambda a: jnp.where(b16, a, 0.0), mats)
    d2 = each(lambda x: _mm2(x, x), d16)
    d4 = each(lambda x: _mm2(x, x), d2)
    d8 = each(lambda x: _mm2(x, x), d4)
    t = each(lambda x: eye - x, d16)
    for p in (d2, d4, d8):
        t = each(lambda x, y: x + _mm2(x, y), t, p)
    off32 = jnp.logical_and(b32, jnp.logical_not(b16))
    for sel in (off32, jnp.logical_not(b32)):
        a_off = each(lambda a: jnp.where(sel, a, 0.0), mats)
        inner = each(_mm2, a_off, t)
        t = each(lambda x, y: x - _mm2(x, y), t, inner)
    return t


def _gdn_kernel(q_ref, k_ref, v_ref, sm_ref, dtb_ref, nega_ref, ea_ref, eb_ref, o_ref,
                u_s, w_s, kd_s, qg_s, aqk_s, dec_s, st_s):
    d = pl.program_id(2)
    j = pl.program_id(3)
    is_fwd = d == 0

    @pl.when(j == 0)
    def _():
        st_s[...] = jnp.zeros_like(st_s)

    sm = sm_ref[0]
    m_cum, m_all = _chunk_masks(is_fwd, TT)
    la = _softplus(sm + dtb_ref[...]) * nega_ref[...]
    gc_sm = _sel_dot(m_cum, la)
    gl_sm = _sel_dot(m_all, la)
    be_sm = jax.nn.sigmoid(sm)

    r = lax.broadcasted_iota(jnp.int32, (CHUNK, CHUNK), 0)
    c = lax.broadcasted_iota(jnp.int32, (CHUNK, CHUNK), 1)
    causal = _causal_mask(is_fwd, CHUNK)
    strict = jnp.logical_and(causal, r != c)
    b16 = lax.shift_right_logical(r, 4) == lax.shift_right_logical(c, 4)
    b32 = lax.shift_right_logical(r, 5) == lax.shift_right_logical(c, 5)
    eye = jnp.where(r == c, 1.0, 0.0)
    chunk_rows = [slice(ci * CHUNK, (ci + 1) * CHUNK) for ci in range(CPT)]

    amats, rhss = [], []
    for hh in range(GDN_HB):
        cols = slice(hh * GDN_DK, (hh + 1) * GDN_DK)
        gc = _dot_sel(gc_sm, ea_ref[0, hh])
        gl = _dot_sel(gl_sm, ea_ref[0, hh])
        beta_e = _dot_sel(be_sm, eb_ref[0, hh])
        q = q_ref[0, :, cols]
        k = k_ref[0, :, cols]
        egc = jnp.exp(gc)
        kb = k * beta_e
        qg_s[:, cols] = (q * egc).astype(BF16)
        kd_s[:, cols] = k * jnp.exp(gl - gc)
        dec_s[:, cols] = jnp.exp(gl)
        rhs = jnp.concatenate([v_ref[0, :, cols] * beta_e, kb * egc], axis=1)
        for ci, rows in enumerate(chunk_rows):
            gcc = gc[rows, :]
            dmat = jnp.exp(jnp.where(causal, gcc[:, 0:CHUNK] - gcc.T[0:CHUNK, :], -1e30))
            kc = k[rows].astype(BF16)
            amats.append(jnp.where(strict, _dot_nt(kb[rows].astype(BF16), kc) * dmat, 0.0))
            rhss.append(rhs[rows])
            aqk_s[rows, hh * CHUNK:(hh + 1) * CHUNK] = (_dot_nt(q[rows].astype(BF16), kc) * dmat).astype(BF16)
    tinv = _unit_tri_inverse(amats, b16, b32, eye)
    for n, (t, rhs_c) in enumerate(zip(tinv, rhss)):
        hh, rows = n // CPT, chunk_rows[n % CPT]
        cols = slice(hh * GDN_DK, (hh + 1) * GDN_DK)
        sol = _mm2(t, rhs_c)
        u_s[rows, cols] = sol[:, 0:GDN_DV]
        w_s[rows, cols] = sol[:, GDN_DV:].astype(BF16)

    for ci in range(CPT):
        off = pl.multiple_of(jnp.where(is_fwd, ci, CPT - 1 - ci) * CHUNK, CHUNK)
        rows = pl.ds(off, CHUNK)
        for hh in range(GDN_HB):
            cols = slice(hh * GDN_DK, (hh + 1) * GDN_DK)
            st = st_s[hh]
            stb = st.astype(BF16)
            vn = u_s[rows, cols] - _dot(w_s[rows, cols], stb)
            vnb = vn.astype(BF16)
            o_ref[0, 0, rows, cols] = (_dot(qg_s[rows, cols], stb)
                                       + _dot(aqk_s[rows, hh * CHUNK:(hh + 1) * CHUNK], vnb))
            st_s[hh] = st * dec_s[pl.ds(off, 1), cols] + _dot(kd_s[rows, cols].T.astype(BF16), vnb)


def gdn_scan(qkv, proj, dtb, nega, ea, eb):
    bsz, t, _ = qkv.shape
    nt = t // TT
    tile = lambda d, j: _scan_tile(d, j, nt)
    ng = GDN_HEADS // GDN_HB
    wb = GDN_HB * GDN_DK
    return pl.pallas_call(
        _gdn_kernel,
        grid=(bsz, ng, 2, nt),
        in_specs=[
            pl.BlockSpec((1, TT, wb), lambda i, h, d, j: (i, tile(d, j), h)),
            pl.BlockSpec((1, TT, wb), lambda i, h, d, j: (i, tile(d, j), ng + h)),
            pl.BlockSpec((1, TT, wb), lambda i, h, d, j: (i, tile(d, j), 2 * ng + h)),
            pl.BlockSpec((1, TT, LANE), lambda i, h, d, j: (i, tile(d, j), 6 * D_MODEL // LANE)),
            pl.BlockSpec((1, LANE), lambda i, h, d, j: (0, 0)),
            pl.BlockSpec((1, LANE), lambda i, h, d, j: (0, 0)),
            pl.BlockSpec((1, GDN_HB, LANE, LANE), lambda i, h, d, j: (d, h, 0, 0)),
            pl.BlockSpec((1, GDN_HB, LANE, LANE), lambda i, h, d, j: (d, h, 0, 0)),
        ],
        out_specs=pl.BlockSpec((1, 1, TT, wb), lambda i, h, d, j: (d, i, tile(d, j), h)),
        out_shape=jax.ShapeDtypeStruct((2, bsz, t, GDN_V), F32),
        scratch_shapes=[pltpu.VMEM((TT, wb), F32), pltpu.VMEM((TT, wb), BF16),
                        pltpu.VMEM((TT, wb), F32), pltpu.VMEM((TT, wb), BF16),
                        pltpu.VMEM((TT, GDN_HB * CHUNK), BF16), pltpu.VMEM((TT, wb), F32),
                        pltpu.VMEM((GDN_HB, GDN_DK, GDN_DV), F32)],
        compiler_params=pltpu.CompilerParams(
            dimension_semantics=("parallel", "parallel", "parallel", "arbitrary")),
        name="gdn_scan",
    )(qkv, qkv, qkv, proj, dtb, nega, ea, eb)


CONF_PAD = 16


def _cd_out_kernel(ga_ref, gb_ref, og_ref, of_ref, ob_ref, cw_ref, cb_ref, lg_ref, lb_ref, ng_ref,
                   w_ref, b_ref, x_ref, gate_ref, o_ref, pad_s):
    j = pl.program_id(1)
    seglen = jnp.where(j == 0, TT, GRID_W)
    half = (CONF_KERNEL - 1) // 2
    pad_s[0:CONF_PAD, :] = jnp.zeros((CONF_PAD, CONF_CH), F32)
    pad_s[CONF_PAD + TT:, :] = jnp.zeros((CONF_PAD, CONF_CH), F32)
    pad_s[CONF_PAD:CONF_PAD + TT, :] = ga_ref[0] * jax.nn.sigmoid(gb_ref[0])
    pos = jnp.bitwise_and(lax.broadcasted_iota(jnp.int32, (TT, 1), 0), seglen - 1)
    acc = jnp.zeros((TT, CONF_CH), F32) + cb_ref[...]
    for kk in range(CONF_KERNEL):
        s = kk - half
        inside = jnp.logical_and(pos + s >= 0, pos + s < seglen)
        acc = acc + pad_s[CONF_PAD + s:CONF_PAD + s + TT, :] * jnp.where(inside, 1.0, 0.0) * cw_ref[kk:kk + 1, :]
    mu = jnp.mean(acc, axis=-1, keepdims=True)
    cen = acc - mu
    var = jnp.mean(cen * cen, axis=-1, keepdims=True)
    conv = _silu(cen * lax.rsqrt(var + 1e-5) * lg_ref[...] + lb_ref[...])
    o = _group_rmsnorm(of_ref[0, 0] + ob_ref[0, 0], GDN_DV) * ng_ref[...] * _silu(og_ref[0])
    m = _dot(conv.astype(BF16), w_ref[0:CONF_CH, :]) + _dot(o.astype(BF16), w_ref[CONF_CH:, :]) + b_ref[...]
    o_ref[0] = x_ref[0] + gate_ref[0, 0] * m


def cd_out(o_gdn, proj, conv_w, conv_b, ln_g, ln_b, norm_g, w, b, x, gate):
    bsz, t, d = x.shape
    seg = lambda i, j: (i, jnp.minimum(j, 1), 0, 0)
    row = lambda width: pl.BlockSpec((1, width), lambda i, j: (0, 0))
    return pl.pallas_call(
        _cd_out_kernel,
        grid=(bsz, t // TT),
        in_specs=[
            pl.BlockSpec((1, TT, CONF_CH), lambda i, j: (i, j, 0)),
            pl.BlockSpec((1, TT, CONF_CH), lambda i, j: (i, j, 1)),
            pl.BlockSpec((1, TT, GDN_V), lambda i, j: (i, j, 5)),
            pl.BlockSpec((1, 1, TT, GDN_V), lambda i, j: (0, i, j, 0)),
            pl.BlockSpec((1, 1, TT, GDN_V), lambda i, j: (1, i, j, 0)),
            pl.BlockSpec((CONF_KERNEL, CONF_CH), lambda i, j: (0, 0)),
            row(CONF_CH), row(CONF_CH), row(CONF_CH), row(GDN_V),
            pl.BlockSpec((CONF_CH + GDN_V, d), lambda i, j: (0, 0)),
            row(d),
            pl.BlockSpec((1, TT, d), lambda i, j: (i, j, 0)),
            pl.BlockSpec((1, 1, 1, d), seg),
        ],
        out_specs=pl.BlockSpec((1, TT, d), lambda i, j: (i, j, 0)),
        out_shape=jax.ShapeDtypeStruct((bsz, t, d), F32),
        scratch_shapes=[pltpu.VMEM((TT + 2 * CONF_PAD, CONF_CH), F32)],
        compiler_params=pltpu.CompilerParams(
            dimension_semantics=("parallel", "parallel"), vmem_limit_bytes=VMEM_LIMIT),
        name="cd_out",
    )(proj, proj, proj, o_gdn, o_gdn, conv_w, conv_b.reshape(1, -1), ln_g.reshape(1, -1), ln_b.reshape(1, -1),
      norm_g.reshape(1, -1), w, b.reshape(1, d), x, gate)


def _out_proj_kernel(m_ref, w_ref, b_ref, x_ref, gate_ref, o_ref):
    y = _dot(m_ref[0].astype(BF16), w_ref[...]) + b_ref[...]
    o_ref[0] = x_ref[0] + gate_ref[0] * y


def out_proj_residual(mixed, w, b, x, gate):
    bsz, t, k = mixed.shape
    d = w.shape[1]
    tm = min(t, 512)
    return pl.pallas_call(
        _out_proj_kernel,
        grid=(bsz, t // tm),
        in_specs=[
            pl.BlockSpec((1, tm, k), lambda i, j: (i, j, 0)),
            pl.BlockSpec((k, d), lambda i, j: (0, 0)),
            pl.BlockSpec((1, d), lambda i, j: (0, 0)),
            pl.BlockSpec((1, tm, d), lambda i, j: (i, j, 0)),
            pl.BlockSpec((1, 1, d), lambda i, j: (i, 0, 0)),
        ],
        out_specs=pl.BlockSpec((1, tm, d), lambda i, j: (i, j, 0)),
        out_shape=jax.ShapeDtypeStruct((bsz, t, d), F32),
        compiler_params=pltpu.CompilerParams(
            dimension_semantics=("parallel", "parallel"), vmem_limit_bytes=VMEM_LIMIT),
        name="out_proj",
    )(mixed, w, b.reshape(1, d), x, gate.reshape(bsz, 1, d))


def _router_kernel(x_ref, g_ref, sh_ref, sc_ref, rw_ref, h_ref, aff_ref):
    x = x_ref[0]
    ms = jnp.mean(x * x, axis=-1, keepdims=True)
    h = (x * lax.rsqrt(ms + 1e-6) * g_ref[...] * (1.0 + sc_ref[0, 0]) + sh_ref[0, 0]).astype(BF16)
    h_ref[0] = h
    logits = _dot(h, rw_ref[...])
    lane = lax.broadcasted_iota(jnp.int32, logits.shape, 1)
    logits = jnp.where(lane < N_EXPERTS, logits, -1e30)
    e = jnp.exp(logits - jnp.max(logits, axis=-1, keepdims=True))
    aff = e / jnp.sum(e, axis=-1, keepdims=True)
    aff_ref[0] = aff.T[0:N_EXPERTS, :]


def moe_router(xa, g, shift, scale, rw, seg, tile0, ntiles):
    bsz, _, d = xa.shape
    n = ntiles * TT
    return pl.pallas_call(
        _router_kernel,
        grid=(bsz, ntiles),
        in_specs=[
            pl.BlockSpec((1, TT, d), lambda i, j: (i, j + tile0, 0)),
            pl.BlockSpec((1, d), lambda i, j: (0, 0)),
            pl.BlockSpec((1, 1, 1, d), lambda i, j: (i, seg, 0, 0)),
            pl.BlockSpec((1, 1, 1, d), lambda i, j: (i, seg, 0, 0)),
            pl.BlockSpec((d, LANE), lambda i, j: (0, 0)),
        ],
        out_specs=[pl.BlockSpec((1, TT, d), lambda i, j: (i, j, 0)),
                   pl.BlockSpec((1, N_EXPERTS, TT), lambda i, j: (i, 0, j))],
        out_shape=[jax.ShapeDtypeStruct((bsz, n, d), BF16), jax.ShapeDtypeStruct((bsz, N_EXPERTS, n), F32)],
        compiler_params=pltpu.CompilerParams(dimension_semantics=("parallel", "parallel")),
        name="moe_router",
    )(xa, g.reshape(1, d), shift, scale, rw)


def _lane_block_prefix(x, u_strict):
    nblk = x.shape[1] // LANE
    run = jnp.zeros((x.shape[0], 1), F32)
    outs = []
    for cblk in range(nblk):
        xc = x[:, cblk * LANE:(cblk + 1) * LANE]
        outs.append(_dot(xc.astype(BF16), u_strict) + run)
        run = run + jnp.sum(xc, axis=-1, keepdims=True)
    return jnp.concatenate(outs, axis=1), run


def _select_kernel(aff_ref, slot_ref, *, cap):
    aff = aff_ref[0]
    bits = pltpu.bitcast(aff, jnp.int32)
    capf = jnp.float32(cap)

    def step(i, thr):
        cand = jnp.bitwise_or(thr, lax.shift_left(jnp.int32(1), 30 - i))
        cnt = jnp.sum(jnp.where(bits >= cand, 1.0, 0.0), axis=-1, keepdims=True)
        return jnp.where(cnt >= capf, cand, thr)

    thr = lax.fori_loop(0, 31, step, jnp.zeros((aff.shape[0], 1), jnp.int32))
    gt = jnp.where(bits > thr, 1.0, 0.0)
    eq = jnp.where(bits == thr, 1.0, 0.0)
    r = lax.broadcasted_iota(jnp.int32, (LANE, LANE), 0)
    c = lax.broadcasted_iota(jnp.int32, (LANE, LANE), 1)
    u_strict = jnp.where(r < c, 1.0, 0.0).astype(BF16)
    need = capf - jnp.sum(gt, axis=-1, keepdims=True)
    eq_rank, _ = _lane_block_prefix(eq, u_strict)
    sel = jnp.maximum(gt, jnp.where(eq_rank < need, eq, 0.0))
    slot, _ = _lane_block_prefix(sel, u_strict)
    slot_ref[0] = jnp.where(sel > 0.0, slot.astype(jnp.int32), -1)


def moe_select(aff, cap):
    bsz, ne, n = aff.shape
    return pl.pallas_call(
        functools.partial(_select_kernel, cap=cap),
        grid=(bsz,),
        in_specs=[pl.BlockSpec((1, ne, n), lambda i: (i, 0, 0))],
        out_specs=pl.BlockSpec((1, ne, n), lambda i: (i, 0, 0)),
        out_shape=jax.ShapeDtypeStruct((bsz, ne, n), jnp.int32),
        compiler_params=pltpu.CompilerParams(dimension_semantics=("parallel",)),
        name="moe_select",
    )(aff)


def _slot_index_kernel(slot_ref, idx_ref, *, cap):
    slot = slot_ref[0]
    n = slot.shape[1]
    srow = lax.broadcasted_iota(jnp.int32, (cap, LANE), 0)
    lane = lax.broadcasted_iota(jnp.int32, (cap, LANE), 1)
    acc = jnp.zeros((cap, LANE), jnp.int32)
    for cblk in range(n // LANE):
        s_c = slot[:, cblk * LANE:(cblk + 1) * LANE]
        acc = acc + jnp.where(srow == s_c, lane + (cblk * LANE + 1), 0)
    ones = jnp.ones((8, LANE), BF16)
    hi = _dot_nt(ones, lax.shift_right_logical(acc, 7).astype(F32).astype(BF16))
    lo = _dot_nt(ones, jnp.bitwise_and(acc, LANE - 1).astype(F32).astype(BF16))
    idx_ref[0] = (hi[0:1, :] * float(LANE) + lo[0:1, :]).astype(jnp.int32) - 1


def moe_slot_index(slot, cap):
    bsz, ne, n = slot.shape
    idx = pl.pallas_call(
        functools.partial(_slot_index_kernel, cap=cap),
        grid=(bsz * ne,),
        in_specs=[pl.BlockSpec((1, 1, n), lambda i: (i, 0, 0))],
        out_specs=pl.BlockSpec((1, 1, cap), lambda i: (i, 0, 0)),
        out_shape=jax.ShapeDtypeStruct((bsz * ne, 1, cap), jnp.int32),
        compiler_params=pltpu.CompilerParams(dimension_semantics=("parallel",)),
        name="moe_slot_index",
    )(slot.reshape(bsz * ne, 1, n))
    return idx.reshape(bsz, ne, cap)


WIN_ALIGN = 16
WIN_FAST = 128


def _combine_kernel(ws_ref, slot_ref, aff_ref, *rest, win):
    ye_refs, (x_ref, gate_ref, o_ref) = rest[:N_EXPERTS], rest[N_EXPERTS:]
    b = pl.program_id(0)
    j = pl.program_id(1)
    srow = lax.broadcasted_iota(jnp.int32, (win, TT), 0)
    acc = jnp.zeros((TT, D_MODEL), F32)
    for e in range(N_EXPERTS):
        sel = jnp.where(srow + ws_ref[b, e, j] == slot_ref[0, e:e + 1, :], aff_ref[0, e:e + 1, :], 0.0).T
        hi = sel.astype(BF16)
        lo = (sel - hi.astype(F32)).astype(BF16)
        ye = ye_refs[e][...]
        acc = acc + _dot(hi, ye) + _dot(lo, ye)
    o_ref[0] = x_ref[0] + gate_ref[0, 0] * acc


def moe_combine(ws, slot, aff, ye, xa, gate, seg, tile0, win):
    bsz, ne, n = slot.shape
    nt = n // TT
    d = xa.shape[2]

    def ye_spec(e):
        return pl.BlockSpec((pl.Squeezed(), pl.Squeezed(), pl.Element(win), pl.Element(d)),
                            lambda i, j, ws_ref: (i, e, pl.multiple_of(ws_ref[i, e, j], WIN_ALIGN), 0))

    return pl.pallas_call(
        functools.partial(_combine_kernel, win=win),
        grid_spec=pltpu.PrefetchScalarGridSpec(
            num_scalar_prefetch=1,
            grid=(bsz, nt),
            in_specs=[pl.BlockSpec((1, ne, TT), lambda i, j, ws_ref: (i, 0, j)),
                      pl.BlockSpec((1, ne, TT), lambda i, j, ws_ref: (i, 0, j))]
            + [ye_spec(e) for e in range(ne)]
            + [pl.BlockSpec((1, TT, d), lambda i, j, ws_ref: (i, j + tile0, 0)),
               pl.BlockSpec((1, 1, 1, d), lambda i, j, ws_ref: (i, seg, 0, 0))],
            out_specs=pl.BlockSpec((1, TT, d), lambda i, j, ws_ref: (i, j, 0)),
        ),
        out_shape=jax.ShapeDtypeStruct((bsz, n, d), F32),
        compiler_params=pltpu.CompilerParams(
            dimension_semantics=("parallel", "parallel"), vmem_limit_bytes=VMEM_LIMIT),
        name="moe_combine",
    )(ws, slot, aff, *([ye] * ne), xa, gate)


def _expert_ffn_kernel(x_ref, w1_ref, w3_ref, w2_ref, o_ref):
    x = x_ref[0, 0].astype(BF16)
    a = _dot(x, w1_ref[0])
    g = _dot(x, w3_ref[0])
    o_ref[0, 0] = _dot((_silu(a) * g).astype(BF16), w2_ref[0]).astype(BF16)


def expert_ffn(xe, w1, w3, w2):
    bsz, ne, cap, d = xe.shape
    f = w1.shape[2]
    tm = min(cap, 512)
    return pl.pallas_call(
        _expert_ffn_kernel,
        grid=(ne, bsz, cap // tm),
        in_specs=[
            pl.BlockSpec((1, 1, tm, d), lambda e, i, j: (i, e, j, 0)),
            pl.BlockSpec((1, d, f), lambda e, i, j: (e, 0, 0)),
            pl.BlockSpec((1, d, f), lambda e, i, j: (e, 0, 0)),
            pl.BlockSpec((1, f, d), lambda e, i, j: (e, 0, 0)),
        ],
        out_specs=pl.BlockSpec((1, 1, tm, d), lambda e, i, j: (i, e, j, 0)),
        out_shape=jax.ShapeDtypeStruct((bsz, ne, cap, d), BF16),
        compiler_params=pltpu.CompilerParams(
            dimension_semantics=("parallel", "parallel", "parallel"), vmem_limit_bytes=VMEM_LIMIT),
        name="expert_ffn",
    )(xe, w1, w3, w2)


def _rmsnorm_kernel(x_ref, g_ref, o_ref):
    x = x_ref[0]
    ms = jnp.mean(x * x, axis=-1, keepdims=True)
    o_ref[0] = x * lax.rsqrt(ms + 1e-6) * g_ref[...]


def rmsnorm_rows(x, g):
    bsz, t, d = x.shape
    tm = min(t, 1024)
    return pl.pallas_call(
        _rmsnorm_kernel,
        grid=(bsz, t // tm),
        in_specs=[pl.BlockSpec((1, tm, d), lambda i, j: (i, j, 0)), pl.BlockSpec((1, d), lambda i, j: (0, 0))],
        out_specs=pl.BlockSpec((1, tm, d), lambda i, j: (i, j, 0)),
        out_shape=jax.ShapeDtypeStruct((bsz, t, d), F32),
        compiler_params=pltpu.CompilerParams(dimension_semantics=("parallel", "parallel")),
        name="final_rmsnorm",
    )(x, g.reshape(1, d))


def _ab_in_layout(w_in, b_in):
    q, k, v, r, glr, z, xs, bm, cm, dt = _split_cols(
        jnp.concatenate([w_in, b_in[None]], axis=0),
        (GLA_QK, GLA_QK, GLA_V, GLA_V, 2 * GLA_GATE_RANK, SSD_INNER, SSD_INNER, SSD_BC, SSD_BC, 2 * SSD_HEADS))
    wb = _pad_cols(jnp.concatenate([q, k, v, r, z, xs, bm, cm, glr, dt], axis=1), AB_N)
    return wb[:-1].astype(BF16), wb[-1]


def _gla_gate_params(w_gate2, b_gate2):
    wg = jnp.zeros((2, GLA_HEADS, LANE, GLA_DK), F32)
    for d in range(2):
        blk = w_gate2[d].reshape(GLA_GATE_RANK, GLA_HEADS, GLA_DK).transpose(1, 0, 2)
        wg = wg.at[d, :, d * GLA_GATE_RANK:(d + 1) * GLA_GATE_RANK, :].set(blk)
    return wg.astype(BF16), b_gate2.reshape(2, GLA_HEADS, 1, GLA_DK)


def _ssd_params(dt_bias, a_log):
    gw = SSD_HPG * SSD_HEADDIM
    dtb = jnp.zeros((2, 1, LANE), F32)
    nega = jnp.zeros((2, 1, LANE), F32)
    e = np.zeros((2, SSD_GROUPS, LANE, gw), np.float32)
    e8 = np.zeros((2, SSD_GROUPS, LANE, LANE), np.float32)
    for d in range(2):
        c0 = SM_DT + d * SSD_HEADS
        dtb = dtb.at[d, 0, c0:c0 + SSD_HEADS].set(dt_bias[d])
        nega = nega.at[d, 0, c0:c0 + SSD_HEADS].set(-jnp.exp(a_log[d]))
        for g in range(SSD_GROUPS):
            for hh in range(SSD_HPG):
                c = c0 + g * SSD_HPG + hh
                e[d, g, c, hh * SSD_HEADDIM:(hh + 1) * SSD_HEADDIM] = 1.0
                e8[d, g, c, hh] = 1.0
    return dtb, nega, jnp.asarray(e, BF16), jnp.asarray(e8, BF16)


def _gdn_params(dt_bias, a_log):
    n = 2 * GDN_HEADS
    dtb = jnp.zeros((1, LANE), F32).at[0, 0:n].set(dt_bias.reshape(n))
    nega = jnp.zeros((1, LANE), F32).at[0, 0:n].set(-jnp.exp(a_log.reshape(n)))
    ea = np.zeros((2, GDN_HEADS, LANE, LANE), np.float32)
    eb = np.zeros((2, GDN_HEADS, LANE, LANE), np.float32)
    for d in range(2):
        for h in range(GDN_HEADS):
            ea[d, h, d * GDN_HEADS + h, :] = 1.0
            eb[d, h, n + d * GDN_HEADS + h, :] = 1.0
    return dtb, nega, jnp.asarray(ea, BF16), jnp.asarray(eb, BF16)


def _rmsnorm(x, g, eps=1e-6):
    return x * lax.rsqrt(jnp.mean(jnp.square(x), axis=-1, keepdims=True) + eps) * g


def _head_rmsnorm(x, g):
    return _rmsnorm(x, g.reshape(x.shape[-2:]))


def _layernorm(x, g, b, eps=1e-5):
    mu = jnp.mean(x, axis=-1, keepdims=True)
    var = jnp.mean(jnp.square(x - mu), axis=-1, keepdims=True)
    return (x - mu) * lax.rsqrt(var + eps) * g + b


def _l2norm(x, eps=1e-6):
    return x * lax.rsqrt(jnp.sum(jnp.square(x), axis=-1, keepdims=True) + eps)


def _dwconv(x, w, b=None):
    k, ch = w.shape
    pad = (k - 1) // 2
    y = lax.conv_general_dilated(x, w[:, None, :], (1,), [(pad, pad)],
                                 dimension_numbers=('NWC', 'WIO', 'NWC'), feature_group_count=ch)
    return y if b is None else y + b


def _gdn_scan(q, k, v, beta, logg, s0, with_out):
    bsz, nh, t, dk = q.shape
    dv = v.shape[-1]
    nc = t // CHUNK
    chunks = lambda z: z.reshape(bsz, nh, nc, CHUNK, *z.shape[3:])
    q, k, v, beta, logg = (chunks(z) for z in (q, k, v, beta, logg))
    gc = jnp.cumsum(logg, axis=-1)
    glast = gc[..., -1]
    tril = jnp.tril(jnp.ones((CHUNK, CHUNK), dtype=bool))
    strict = jnp.tril(jnp.ones((CHUNK, CHUNK), dtype=bool), k=-1)
    decay = jnp.exp(jnp.where(tril, gc[..., :, None] - gc[..., None, :], -jnp.inf))
    kb = k * beta[..., None]
    m = jnp.eye(CHUNK, dtype=k.dtype) + jnp.where(strict, jnp.einsum('bhcid,bhcjd->bhcij', kb, k) * decay, 0.0)
    rhs = jnp.concatenate([v * beta[..., None], kb * jnp.exp(gc)[..., None]], axis=-1)
    sol = lax.linalg.triangular_solve(m, rhs, left_side=True, lower=True, unit_diagonal=True)
    u, w = sol[..., :dv], sol[..., dv:]
    kd = k * jnp.exp(glast[..., None] - gc)[..., None]
    dec = jnp.exp(glast)
    front = lambda z: jnp.moveaxis(z, 2, 0)

    def advance(s, w_c, u_c, kd_c, dec_c):
        vn = u_c - jnp.einsum('bhid,bhde->bhie', w_c, s)
        return vn, dec_c[..., None, None] * s + jnp.einsum('bhjd,bhje->bhde', kd_c, vn)

    xs = (front(w), front(u), front(kd), front(dec))
    if not with_out:
        s_fin, _ = lax.scan(lambda s, xc: (advance(s, *xc)[1], None), s0, xs)
        return None, s_fin
    aqk = jnp.einsum('bhcid,bhcjd->bhcij', q, k) * decay
    qg = q * jnp.exp(gc)[..., None]

    def step(s, xc):
        w_c, u_c, kd_c, dec_c, aqk_c, qg_c = xc
        vn, s_new = advance(s, w_c, u_c, kd_c, dec_c)
        o = jnp.einsum('bhid,bhde->bhie', qg_c, s) + jnp.einsum('bhij,bhje->bhie', aqk_c, vn)
        return s_new, o

    s_fin, o = lax.scan(step, s0, xs + (front(aqk), front(qg)))
    return jnp.moveaxis(o, 0, 2).reshape(bsz, nh, t, dv), s_fin


def _bidir_scan(scan_fn, args_f, args_b, t_axis, init_f, init_b, with_out):
    flip = lambda a: jnp.flip(a, axis=t_axis)
    o_f, s_f = scan_fn(*args_f, init_f, with_out)
    o_b, s_b = scan_fn(*[flip(a) for a in args_b], init_b, with_out)
    o = o_f + flip(o_b) if with_out else None
    return o, s_f, s_b


def _conformer_conv(u, rows, dw_w, dw_b, ln_g, ln_b):
    bsz, t, ch = u.shape
    y = _dwconv(u.reshape(bsz * rows, t // rows, ch), dw_w, dw_b).reshape(bsz, t, ch)
    return jax.nn.silu(_layernorm(y, ln_g, ln_b))


def _cd_stream(proj, gdn_conv_w, gdn_a_log, gdn_dt_bias):
    bsz, t, _ = proj.shape
    ga, gb, q, k, v, og, a_raw, b_raw = _split_cols(proj, CD_SPLITS)
    glu = ga * jax.nn.sigmoid(gb)
    qkv = jax.nn.silu(_dwconv(jnp.concatenate([q, k, v], axis=-1), gdn_conv_w))
    q, k, v = _split_cols(qkv, (GDN_QK, GDN_QK, GDN_V))
    heads = lambda a: a.reshape(bsz, t, GDN_HEADS, -1).transpose(0, 2, 1, 3)
    q = _l2norm(heads(q)) * GDN_DK ** -0.5
    k = _l2norm(heads(k))
    v = heads(v)
    beta = jax.nn.sigmoid(b_raw.reshape(bsz, t, 2, GDN_HEADS)).transpose(2, 0, 3, 1)
    logg = (-jnp.exp(gdn_a_log)
            * jax.nn.softplus(a_raw.reshape(bsz, t, 2, GDN_HEADS) + gdn_dt_bias)).transpose(2, 0, 3, 1)
    return (q, k, v, beta[0], logg[0]), (q, k, v, beta[1], logg[1]), glu, og


def _mixer_cd(proj_l, proj_c, rows, conf_dw_w, conf_dw_b, conf_ln_g, conf_ln_b,
              gdn_conv_w, gdn_a_log, gdn_dt_bias, gdn_norm_g, need_ctx):
    c_f, c_b, c_glu, c_og = _cd_stream(proj_c, gdn_conv_w, gdn_a_log, gdn_dt_bias)
    l_f, l_b, l_glu, l_og = _cd_stream(proj_l, gdn_conv_w, gdn_a_log, gdn_dt_bias)
    bsz = proj_l.shape[0]
    z0 = jnp.zeros((bsz, GDN_HEADS, GDN_DK, GDN_DV), F32)
    od_c, s_f, s_b = _bidir_scan(_gdn_scan, c_f, c_b, 2, z0, z0, need_ctx)
    od_l, _, _ = _bidir_scan(_gdn_scan, l_f, l_b, 2, s_f, s_b, True)

    def mix(od, glu, og, n_rows):
        t = glu.shape[1]
        conv = _conformer_conv(glu, n_rows, conf_dw_w, conf_dw_b, conf_ln_g, conf_ln_b)
        o = _head_rmsnorm(jnp.swapaxes(od, 1, 2), gdn_norm_g) * jax.nn.silu(og).reshape(bsz, t, GDN_HEADS, GDN_DV)
        return jnp.concatenate([conv, o.reshape(bsz, t, GDN_V)], axis=-1)

    return mix(od_l, l_glu, l_og, rows), (mix(od_c, c_glu, c_og, 1) if need_ctx else None)


def moe_segment(i, xa, mods, p, w1, w3, w2, seg, tile0, ntiles):
    bsz, _, d = xa.shape
    n = ntiles * TT
    cap = n * EC_CAPACITY // N_EXPERTS
    rw = _pad_cols(p["moe_router"][i], LANE).astype(BF16)
    h, aff = moe_router(xa, p["norm2_g"][i], mods[:, :, 3], mods[:, :, 4], rw, seg, tile0, ntiles)
    slot = moe_select(aff, cap)
    idx = moe_slot_index(slot, cap).reshape(bsz, N_EXPERTS * cap)
    xe = jnp.take_along_axis(h, idx[..., None], axis=1).reshape(bsz, N_EXPERTS, cap, d)
    ye = expert_ffn(xe, w1, w3, w2)
    counts = jnp.sum((slot >= 0).reshape(bsz, N_EXPERTS, ntiles, TT), axis=-1, dtype=jnp.int32)
    starts = jnp.cumsum(counts, axis=-1) - counts
    aligned = starts // WIN_ALIGN * WIN_ALIGN

    def run(win):
        ws = jnp.minimum(aligned, cap - win)
        return moe_combine(ws, slot, aff, ye, xa, mods[:, :, 5], seg, tile0, win)

    win_fast, win_full = min(cap, WIN_FAST), min(cap, TT + WIN_ALIGN)
    if win_fast == win_full:
        return run(win_full)
    overflow = jnp.any(starts + counts - jnp.minimum(aligned, cap - win_fast) > win_fast)
    return lax.cond(overflow, lambda: run(win_full), lambda: run(win_fast))


def layer_mixer(i, j, xa, mods, p, last):
    sh1, sc1, g1 = (mods[:, :, s] for s in range(3))
    if i % 2 == 0:
        w_in, b_in = _ab_in_layout(p["ab_w_in"][j], p["ab_b_in"][j])
        proj = norm_proj(xa, p["norm1_g"][i], sh1, sc1, w_in, b_in, 1920)
        xbc = ab_prep(proj, p["ssd_conv_w"][j], p["ssd_conv_b"][j])
        wg, bg = _gla_gate_params(p["gla_w_gate2"][j], p["gla_b_gate2"][j])
        o_gla = gla_scan(proj, wg, bg)
        y_ssd = ssd_scan(xbc, proj, *_ssd_params(p["ssd_dt_bias"][j], p["ssd_a_log"][j]))
        return ab_out(o_gla, y_ssd, proj, xbc, p["gla_norm_g"][j], jnp.repeat(p["ssd_d"][j], SSD_HEADDIM),
                      p["ssd_norm_g"][j], p["ab_w_out"][j].astype(BF16), p["ab_b_out"][j], xa, g1)
    w_in = _pad_cols(p["cd_w_in"][j], CD_N).astype(BF16)
    b_in = _pad_cols(p["cd_b_in"][j], CD_N)
    proj = norm_proj(xa, p["norm1_g"][i], sh1, sc1, w_in, b_in, 1280)
    qkv = cd_prep(proj, p["gdn_conv_w"][j])
    o_gdn = gdn_scan(qkv, proj, *_gdn_params(p["gdn_dt_bias"][j], p["gdn_a_log"][j]))
    return cd_out(o_gdn, proj, p["conf_dw_w"][j], p["conf_dw_b"][j], p["conf_ln_g"][j], p["conf_ln_b"][j],
                  p["gdn_norm_g"][j], p["cd_w_out"][j].astype(BF16), p["cd_b_out"][j], xa, g1)


def kernel(x, c, ctx, c_ctx, mod_w, mod_b, norm1_g, norm2_g, ab_w_in, ab_b_in, ab_w_out, ab_b_out, gla_w_gate2, gla_b_gate2, gla_norm_g, ssd_conv_w, ssd_conv_b, ssd_dt_bias, ssd_a_log, ssd_d, ssd_norm_g, cd_w_in, cd_b_in, cd_w_out, cd_b_out, conf_dw_w, conf_dw_b, conf_ln_g, conf_ln_b, gdn_conv_w, gdn_a_log, gdn_dt_bias, gdn_norm_g, moe_router, moe_w1, moe_w3, moe_w2, final_norm_g):
    p = dict(locals())
    bsz, seq, d = x.shape
    assert ctx.shape[1] == TT and seq % TT == 0
    depth = mod_w.shape[0]
    xa = jnp.concatenate([ctx, x], axis=1)
    for i in range(depth):
        last = i == depth - 1
        mod_l = jax.nn.silu(c) @ mod_w[i] + mod_b[i]
        mod_c = jnp.broadcast_to(jax.nn.silu(c_ctx) @ mod_w[i] + mod_b[i], mod_l.shape)
        mods = jnp.stack([mod_c, mod_l], axis=1).reshape(bsz, 2, 6, 1, d)
        sh1, sc1, g1, sh2, sc2, g2 = (mods[:, :, s] for s in range(6))
        xa = layer_mixer(i, i // 2, xa, mods, p, last)
        w1, w3, w2 = moe_w1[i].astype(BF16), moe_w3[i].astype(BF16), moe_w2[i].astype(BF16)
        x_c, x_l = xa[:, :TT], xa[:, TT:]
        x_l = moe_segment(i, xa, mods, p, w1, w3, w2, 1, 1, seq // TT)
        if not last:
            x_c = moe_segment(i, xa, mods, p, w1, w3, w2, 0, 0, 1)
        xa = jnp.concatenate([x_c, x_l], axis=1)
    return rmsnorm_rows(xa[:, TT:], final_norm_g)
```

```python
import functools

import jax
import jax.numpy as jnp
import numpy as np
from jax import lax
from jax.experimental import pallas as pl
from jax.experimental.pallas import tpu as pltpu

F32 = jnp.float32
BF16 = jnp.bfloat16

D_MODEL = 1024
GRID_W = 64
CHUNK = 64
GLA_HEADS, GLA_DK, GLA_DV, GLA_GATE_RANK, GLA_GATE_TAU = 4, 128, 256, 16, 16.0
SSD_HEADS, SSD_HEADDIM, SSD_STATE, SSD_GROUPS = 16, 64, 128, 2
CONF_CH, CONF_KERNEL = D_MODEL, 31
GDN_HEADS, GDN_DK, GDN_DV = 8, 128, 128
N_EXPERTS, EC_CAPACITY, EXPERT_FF = 16, 2, D_MODEL

GLA_QK = GLA_HEADS * GLA_DK
GLA_V = GLA_HEADS * GLA_DV
SSD_INNER = SSD_HEADS * SSD_HEADDIM
SSD_BC = SSD_GROUPS * SSD_STATE
SSD_HPG = SSD_HEADS // SSD_GROUPS
GDN_QK = GDN_HEADS * GDN_DK
GDN_V = GDN_HEADS * GDN_DV
CD_SPLITS = (CONF_CH, CONF_CH, GDN_QK, GDN_QK, GDN_V, GDN_V, 2 * GDN_HEADS, 2 * GDN_HEADS)

LANE = 128
TT = 256
CPT = TT // CHUNK
VMEM_LIMIT = 48 * 1024 * 1024

AB_Q, AB_K, AB_V, AB_R, AB_Z, AB_XS, AB_BM, AB_CM, AB_SMALL = 0, 512, 1024, 2048, 3072, 4096, 5120, 5376, 5632
AB_N = AB_SMALL + LANE
SM_DT = 2 * GLA_GATE_RANK
CD_N = 6400


def _split_cols(a, sizes):
    return jnp.split(a, np.cumsum(sizes)[:-1].tolist(), axis=-1)


def _pad_cols(a, n):
    return jnp.pad(a, [(0, 0)] * (a.ndim - 1) + [(0, n - a.shape[-1])])


def _dot(a, b):
    return jnp.dot(a, b, preferred_element_type=F32)


def _dot_nt(a, b):
    return lax.dot_general(a, b, (((1,), (1,)), ((), ())), preferred_element_type=F32)


def _split3(x):
    hi = x.astype(BF16)
    r = x - hi.astype(F32)
    mid = r.astype(BF16)
    lo = (r - mid.astype(F32)).astype(BF16)
    return hi, mid, lo


def _sel_dot(m, x):
    hi, mid, lo = _split3(x)
    return _dot(m, hi) + _dot(m, mid) + _dot(m, lo)


def _dot_sel(x, e):
    hi, mid, lo = _split3(x)
    return _dot(hi, e) + _dot(mid, e) + _dot(lo, e)


def _softplus(x):
    return jnp.maximum(x, 0.0) + jnp.log(1.0 + jnp.exp(-jnp.abs(x)))


def _silu(x):
    return x * jax.nn.sigmoid(x)


def _chunk_masks(is_fwd, n):
    r = lax.broadcasted_iota(jnp.int32, (n, n), 0)
    c = lax.broadcasted_iota(jnp.int32, (n, n), 1)
    same = lax.shift_right_logical(r, 6) == lax.shift_right_logical(c, 6)
    lo = jnp.where(is_fwd, c, r)
    hi = jnp.where(is_fwd, r, c)
    cum = jnp.logical_and(same, lo <= hi)
    return jnp.where(cum, 1.0, 0.0).astype(BF16), jnp.where(same, 1.0, 0.0).astype(BF16)


def _causal_mask(is_fwd, n):
    r = lax.broadcasted_iota(jnp.int32, (n, n), 0)
    c = lax.broadcasted_iota(jnp.int32, (n, n), 1)
    return jnp.where(is_fwd, c, r) <= jnp.where(is_fwd, r, c)


def _scan_tile(d, j, nt):
    return jnp.where(d == 0, j, jnp.where(j == 0, 0, nt - j))


def _norm_proj_kernel(x_ref, g_ref, sh_ref, sc_ref, w_ref, b_ref, o_ref):
    x = x_ref[0]
    ms = jnp.mean(x * x, axis=-1, keepdims=True)
    h = (x * lax.rsqrt(ms + 1e-6) * g_ref[...] * (1.0 + sc_ref[0, 0]) + sh_ref[0, 0]).astype(BF16)
    o_ref[0] = _dot(h, w_ref[...]) + b_ref[...]


def norm_proj(x, g, shift, scale, w, b, tn):
    bsz, t, d = x.shape
    n = w.shape[1]
    seg = lambda k, i, j: (i, jnp.minimum(j, 1), 0, 0)
    return pl.pallas_call(
        _norm_proj_kernel,
        grid=(n // tn, bsz, t // TT),
        in_specs=[
            pl.BlockSpec((1, TT, d), lambda k, i, j: (i, j, 0)),
            pl.BlockSpec((1, d), lambda k, i, j: (0, 0)),
            pl.BlockSpec((1, 1, 1, d), seg),
            pl.BlockSpec((1, 1, 1, d), seg),
            pl.BlockSpec((d, tn), lambda k, i, j: (0, k)),
            pl.BlockSpec((1, tn), lambda k, i, j: (0, k)),
        ],
        out_specs=pl.BlockSpec((1, TT, tn), lambda k, i, j: (i, j, k)),
        out_shape=jax.ShapeDtypeStruct((bsz, t, n), F32),
        compiler_params=pltpu.CompilerParams(
            dimension_semantics=("parallel", "parallel", "parallel"), vmem_limit_bytes=VMEM_LIMIT),
        name="norm_proj",
    )(x, g.reshape(1, d), shift, scale, w, b.reshape(1, n))


def _conv3_piece(x, lh, rh, w, b, left_ok, right_ok):
    n = x.shape[0]
    row = lax.broadcasted_iota(jnp.int32, x.shape, 0)
    prev_row = jnp.where(left_ok, lh[7:8, :], 0.0)
    next_row = jnp.where(right_ok, rh[0:1, :], 0.0)
    x_prev = jnp.where(row == 0, prev_row, pltpu.roll(x, 1, 0))
    x_next = jnp.where(row == n - 1, next_row, pltpu.roll(x, n - 1, 0))
    return _silu(w[0:1, :] * x_prev + w[1:2, :] * x + w[2:3, :] * x_next + b)


def _ab_prep_kernel(xs_ref, bm_ref, cm_ref, xsl_ref, bml_ref, cml_ref, xsr_ref, bmr_ref, cmr_ref,
                    w_ref, b_ref, o_ref):
    j = pl.program_id(1)
    nt = pl.num_programs(1)
    left_ok = j >= 2
    right_ok = jnp.logical_and(j >= 1, j < nt - 1)
    w = w_ref[...]
    b = b_ref[...]
    o_ref[0, :, 0:SSD_INNER] = _conv3_piece(xs_ref[0], xsl_ref[0], xsr_ref[0], w[:, 0:SSD_INNER],
                                            b[:, 0:SSD_INNER], left_ok, right_ok)
    c0, c1 = SSD_INNER, SSD_INNER + SSD_BC
    o_ref[0, :, c0:c1] = _conv3_piece(bm_ref[0], bml_ref[0], bmr_ref[0], w[:, c0:c1], b[:, c0:c1], left_ok, right_ok)
    c0, c1 = c1, c1 + SSD_BC
    o_ref[0, :, c0:c1] = _conv3_piece(cm_ref[0], cml_ref[0], cmr_ref[0], w[:, c0:c1], b[:, c0:c1], left_ok, right_ok)


def ab_prep(proj, conv_w, conv_b):
    bsz, t, _ = proj.shape
    nt = t // TT
    rb = TT // 8
    nrb = t // 8
    cw = SSD_INNER + 2 * SSD_BC

    def cur(width, col):
        return pl.BlockSpec((1, TT, width), lambda i, j: (i, j, col // width))

    def left(width, col):
        return pl.BlockSpec((1, 8, width), lambda i, j: (i, jnp.maximum(j * rb - 1, 0), col // width))

    def right(width, col):
        return pl.BlockSpec((1, 8, width), lambda i, j: (i, jnp.minimum((j + 1) * rb, nrb - 1), col // width))

    pieces = ((SSD_INNER, AB_XS), (SSD_BC, AB_BM), (SSD_BC, AB_CM))
    return pl.pallas_call(
        _ab_prep_kernel,
        grid=(bsz, nt),
        in_specs=[cur(*p) for p in pieces] + [left(*p) for p in pieces] + [right(*p) for p in pieces] + [
            pl.BlockSpec((3, cw), lambda i, j: (0, 0)),
            pl.BlockSpec((1, cw), lambda i, j: (0, 0)),
        ],
        out_specs=pl.BlockSpec((1, TT, cw), lambda i, j: (i, j, 0)),
        out_shape=jax.ShapeDtypeStruct((bsz, t, cw), F32),
        compiler_params=pltpu.CompilerParams(dimension_semantics=("parallel", "parallel")),
        name="ab_prep",
    )(*([proj] * 9), conv_w, conv_b.reshape(1, cw))


def _gla_kernel(q_ref, k_ref, v_ref, sm_ref, wg_ref, bg_ref, o_ref, qg_s, kn_s, kd_s, egl_s, st_s):
    d = pl.program_id(1)
    j = pl.program_id(2)
    is_fwd = d == 0

    @pl.when(j == 0)
    def _():
        st_s[...] = jnp.zeros_like(st_s)

    gz = _dot(sm_ref[0].astype(BF16), wg_ref[0]) + bg_ref[0]
    logg = (jnp.minimum(gz, 0.0) - jnp.log(1.0 + jnp.exp(-jnp.abs(gz)))) * (1.0 / GLA_GATE_TAU)
    m_cum, m_all = _chunk_masks(is_fwd, TT)
    gc = _sel_dot(m_cum, logg)
    gl = _sel_dot(m_all, logg)
    q = q_ref[0] * (GLA_DK ** -0.5)
    k = k_ref[0]
    qg_s[...] = (q * jnp.exp(gc)).astype(BF16)
    kn_s[...] = (k * jnp.exp(-gc)).astype(BF16)
    kd_s[...] = (k * jnp.exp(gl - gc)).astype(BF16)
    egl_s[...] = jnp.exp(gl)
    causal = _causal_mask(is_fwd, CHUNK)

    for ci in range(CPT):
        off = pl.multiple_of(jnp.where(is_fwd, ci, CPT - 1 - ci) * CHUNK, CHUNK)
        rows = pl.ds(off, CHUNK)
        for h in range(GLA_HEADS):
            kc = slice(h * GLA_DK, (h + 1) * GLA_DK)
            vc = slice(h * GLA_DV, (h + 1) * GLA_DV)
            qg = qg_s[rows, kc]
            v = v_ref[0, rows, vc]
            st = st_s[h]
            att = jnp.where(causal, _dot_nt(qg, kn_s[rows, kc]), 0.0)
            o_ref[0, 0, rows, vc] = _dot(att.astype(BF16), v.astype(BF16)) + _dot_nt(qg, st.astype(BF16))
            st_s[h] = st * egl_s[pl.ds(off, 1), kc] + _dot(v.T.astype(BF16), kd_s[rows, kc])


def gla_scan(proj, wg, bg):
    bsz, t, _ = proj.shape
    nt = t // TT
    tile = lambda d, j: _scan_tile(d, j, nt)
    return pl.pallas_call(
        _gla_kernel,
        grid=(bsz, 2, nt),
        in_specs=[
            pl.BlockSpec((1, TT, GLA_QK), lambda i, d, j: (i, tile(d, j), AB_Q // GLA_QK)),
            pl.BlockSpec((1, TT, GLA_QK), lambda i, d, j: (i, tile(d, j), AB_K // GLA_QK)),
            pl.BlockSpec((1, TT, GLA_V), lambda i, d, j: (i, tile(d, j), AB_V // GLA_V)),
            pl.BlockSpec((1, TT, LANE), lambda i, d, j: (i, tile(d, j), AB_SMALL // LANE)),
            pl.BlockSpec((1, LANE, GLA_QK), lambda i, d, j: (d, 0, 0)),
            pl.BlockSpec((1, 1, GLA_QK), lambda i, d, j: (d, 0, 0)),
        ],
        out_specs=pl.BlockSpec((1, 1, TT, GLA_V), lambda i, d, j: (d, i, tile(d, j), 0)),
        out_shape=jax.ShapeDtypeStruct((2, bsz, t, GLA_V), F32),
        scratch_shapes=[pltpu.VMEM((TT, GLA_QK), BF16), pltpu.VMEM((TT, GLA_QK), BF16),
                        pltpu.VMEM((TT, GLA_QK), BF16), pltpu.VMEM((TT, GLA_QK), F32),
                        pltpu.VMEM((GLA_HEADS, GLA_DV, GLA_DK), F32)],
        compiler_params=pltpu.CompilerParams(
            dimension_semantics=("parallel", "parallel", "arbitrary")),
        name="gla_scan",
    )(proj, proj, proj, proj, wg, bg)


def _ssd_kernel(xs_ref, bm_ref, cm_ref, sm_ref, dtb_ref, nega_ref, e_ref, e8_ref, o_ref,
                v_s, vw_s, cdec_s, dec_s, a8_s, st_s):
    d = pl.program_id(2)
    j = pl.program_id(3)
    is_fwd = d == 0

    @pl.when(j == 0)
    def _():
        st_s[...] = jnp.zeros_like(st_s)

    dt = _softplus(sm_ref[0] + dtb_ref[0])
    la = dt * nega_ref[0]
    m_cum, m_all = _chunk_masks(is_fwd, TT)
    acum = _sel_dot(m_cum, la)
    atot = _sel_dot(m_all, la)
    e = e_ref[0, 0]
    acum_e = _dot_sel(acum, e)
    atot_e = _dot_sel(atot, e)
    v = xs_ref[0] * _dot_sel(dt, e)
    v_s[...] = v.astype(BF16)
    vw_s[...] = (v * jnp.exp(atot_e - acum_e)).astype(BF16)
    cdec_s[...] = jnp.exp(acum_e)
    dec_s[...] = jnp.exp(atot_e)
    a8_s[...] = _dot_sel(acum, e8_ref[0, 0])
    causal = _causal_mask(is_fwd, CHUNK)

    for ci in range(CPT):
        off = pl.multiple_of(jnp.where(is_fwd, ci, CPT - 1 - ci) * CHUNK, CHUNK)
        rows = pl.ds(off, CHUNK)
        bm = bm_ref[0, rows, :]
        cm = cm_ref[0, rows, :].astype(BF16)
        st = st_s[...]
        cb = _dot_nt(cm, bm.astype(BF16))
        y_inter = _dot(cm, st.astype(BF16)) * cdec_s[rows, :]
        a8 = a8_s[rows, :]
        a8t = a8.T
        v_c = v_s[rows, :]
        ys = []
        for hh in range(SSD_HPG):
            diff = a8[:, hh:hh + 1] - a8t[hh:hh + 1, :]
            seg = jnp.exp(jnp.where(causal, diff, -1e30))
            ys.append(_dot((seg * cb).astype(BF16), v_c[:, hh * SSD_HEADDIM:(hh + 1) * SSD_HEADDIM]))
        o_ref[0, 0, rows, :] = jnp.concatenate(ys, axis=1) + y_inter
        st_s[...] = st * dec_s[pl.ds(off, 1), :] + _dot(bm.T.astype(BF16), vw_s[rows, :])


def ssd_scan(xbc, proj, dtb, nega, e, e8):
    bsz, t, _ = xbc.shape
    nt = t // TT
    gw = SSD_HPG * SSD_HEADDIM
    tile = lambda d, j: _scan_tile(d, j, nt)
    return pl.pallas_call(
        _ssd_kernel,
        grid=(bsz, SSD_GROUPS, 2, nt),
        in_specs=[
            pl.BlockSpec((1, TT, gw), lambda i, g, d, j: (i, tile(d, j), g)),
            pl.BlockSpec((1, TT, SSD_STATE), lambda i, g, d, j: (i, tile(d, j), SSD_INNER // SSD_STATE + g)),
            pl.BlockSpec((1, TT, SSD_STATE), lambda i, g, d, j: (i, tile(d, j), (SSD_INNER + SSD_BC) // SSD_STATE + g)),
            pl.BlockSpec((1, TT, LANE), lambda i, g, d, j: (i, tile(d, j), AB_SMALL // LANE)),
            pl.BlockSpec((1, 1, LANE), lambda i, g, d, j: (d, 0, 0)),
            pl.BlockSpec((1, 1, LANE), lambda i, g, d, j: (d, 0, 0)),
            pl.BlockSpec((1, 1, LANE, gw), lambda i, g, d, j: (d, g, 0, 0)),
            pl.BlockSpec((1, 1, LANE, LANE), lambda i, g, d, j: (d, g, 0, 0)),
        ],
        out_specs=pl.BlockSpec((1, 1, TT, gw), lambda i, g, d, j: (d, i, tile(d, j), g)),
        out_shape=jax.ShapeDtypeStruct((2, bsz, t, SSD_INNER), F32),
        scratch_shapes=[pltpu.VMEM((TT, gw), BF16), pltpu.VMEM((TT, gw), BF16),
                        pltpu.VMEM((TT, gw), F32), pltpu.VMEM((TT, gw), F32),
                        pltpu.VMEM((TT, LANE), F32), pltpu.VMEM((SSD_STATE, gw), F32)],
        compiler_params=pltpu.CompilerParams(
            dimension_semantics=("parallel", "parallel", "parallel", "arbitrary")),
        name="ssd_scan",
    )(xbc, xbc, xbc, proj, dtb, nega, e, e8)


def _group_rmsnorm(x, width):
    parts = []
    for s in range(x.shape[1] // width):
        seg = x[:, s * width:(s + 1) * width]
        parts.append(seg * lax.rsqrt(jnp.mean(seg * seg, axis=-1, keepdims=True) + 1e-6))
    return jnp.concatenate(parts, axis=1)


def _ab_out_kernel(of_ref, ob_ref, yf_ref, yb_ref, r_ref, z_ref, xs_ref, gg_ref, dv_ref, sg_ref,
                   w_ref, b_ref, x_ref, gate_ref, o_ref):
    o = _group_rmsnorm(of_ref[0, 0] + ob_ref[0, 0], GLA_DV) * gg_ref[...] * _silu(r_ref[0])
    y = (yf_ref[0, 0] + yb_ref[0, 0] + dv_ref[...] * xs_ref[0]) * _silu(z_ref[0])
    y = _group_rmsnorm(y, SSD_INNER // SSD_GROUPS) * sg_ref[...]
    m = _dot(o.astype(BF16), w_ref[0:GLA_V, :]) + _dot(y.astype(BF16), w_ref[GLA_V:, :]) + b_ref[...]
    o_ref[0] = x_ref[0] + gate_ref[0, 0] * m


def ab_out(o_gla, y_ssd, proj, xbc, gla_g, d_vec, ssd_g, w, b, x, gate):
    bsz, t, d = x.shape
    seg = lambda i, j: (i, jnp.minimum(j, 1), 0, 0)
    row = lambda width: pl.BlockSpec((1, width), lambda i, j: (0, 0))
    return pl.pallas_call(
        _ab_out_kernel,
        grid=(bsz, t // TT),
        in_specs=[
            pl.BlockSpec((1, 1, TT, GLA_V), lambda i, j: (0, i, j, 0)),
            pl.BlockSpec((1, 1, TT, GLA_V), lambda i, j: (1, i, j, 0)),
            pl.BlockSpec((1, 1, TT, SSD_INNER), lambda i, j: (0, i, j, 0)),
            pl.BlockSpec((1, 1, TT, SSD_INNER), lambda i, j: (1, i, j, 0)),
            pl.BlockSpec((1, TT, GLA_V), lambda i, j: (i, j, AB_R // GLA_V)),
            pl.BlockSpec((1, TT, SSD_INNER), lambda i, j: (i, j, AB_Z // SSD_INNER)),
            pl.BlockSpec((1, TT, SSD_INNER), lambda i, j: (i, j, 0)),
            row(GLA_V), row(SSD_INNER), row(SSD_INNER),
            pl.BlockSpec((GLA_V + SSD_INNER, d), lambda i, j: (0, 0)),
            row(d),
            pl.BlockSpec((1, TT, d), lambda i, j: (i, j, 0)),
            pl.BlockSpec((1, 1, 1, d), seg),
        ],
        out_specs=pl.BlockSpec((1, TT, d), lambda i, j: (i, j, 0)),
        out_shape=jax.ShapeDtypeStruct((bsz, t, d), F32),
        compiler_params=pltpu.CompilerParams(
            dimension_semantics=("parallel", "parallel"), vmem_limit_bytes=VMEM_LIMIT),
        name="ab_out",
    )(o_gla, o_gla, y_ssd, y_ssd, proj, proj, xbc, gla_g.reshape(1, -1), d_vec.reshape(1, -1),
      ssd_g.reshape(1, -1), w, b.reshape(1, d), x, gate)


def _cd_prep_kernel(q_ref, k_ref, v_ref, ql_ref, kl_ref, vl_ref, qr_ref, kr_ref, vr_ref, w_ref, o_ref):
    j = pl.program_id(1)
    nt = pl.num_programs(1)
    left_ok = j >= 2
    right_ok = jnp.logical_and(j >= 1, j < nt - 1)
    w = w_ref[...]
    srcs = ((q_ref, ql_ref, qr_ref, GDN_DK ** -0.5), (k_ref, kl_ref, kr_ref, 1.0), (v_ref, vl_ref, vr_ref, None))
    for s, (c_ref, l_ref, r_ref, scale) in enumerate(srcs):
        c0 = s * GDN_QK
        y = _conv3_piece(c_ref[0], l_ref[0], r_ref[0], w[:, c0:c0 + GDN_QK], 0.0, left_ok, right_ok)
        if scale is None:
            o_ref[0, :, c0:c0 + GDN_QK] = y
            continue
        for h in range(GDN_HEADS):
            seg = y[:, h * GDN_DK:(h + 1) * GDN_DK]
            inv = lax.rsqrt(jnp.sum(seg * seg, axis=-1, keepdims=True) + 1e-6) * scale
            o_ref[0, :, c0 + h * GDN_DK:c0 + (h + 1) * GDN_DK] = seg * inv


def cd_prep(proj, conv_w):
    bsz, t, _ = proj.shape
    nt = t // TT
    rb = TT // 8
    nrb = t // 8
    width = GDN_QK
    cols = (2, 3, 4)

    cur = lambda cb: pl.BlockSpec((1, TT, width), lambda i, j: (i, j, cb))
    left = lambda cb: pl.BlockSpec((1, 8, width), lambda i, j: (i, jnp.maximum(j * rb - 1, 0), cb))
    right = lambda cb: pl.BlockSpec((1, 8, width), lambda i, j: (i, jnp.minimum((j + 1) * rb, nrb - 1), cb))
    return pl.pallas_call(
        _cd_prep_kernel,
        grid=(bsz, nt),
        in_specs=[cur(cb) for cb in cols] + [left(cb) for cb in cols] + [right(cb) for cb in cols] + [
            pl.BlockSpec((3, 3 * width), lambda i, j: (0, 0))],
        out_specs=pl.BlockSpec((1, TT, 3 * width), lambda i, j: (i, j, 0)),
        out_shape=jax.ShapeDtypeStruct((bsz, t, 3 * width), F32),
        compiler_params=pltpu.CompilerParams(dimension_semantics=("parallel", "parallel")),
        name="cd_prep",
    )(*([proj] * 9), conv_w)


GDN_HB = 8


def _mm2(a, b):
    return _dot(a.astype(BF16), b.astype(BF16))


def _unit_tri_inverse(mats, b16, b32, eye):
    each = lambda f, *ls: [f(*xs) for xs in zip(*ls)]
    d16 = each(lambda a: jnp.where(b16, a, 0.0), mats)
    d2 = each(lambda x: _mm2(x, x), d16)
    d4 = each(lambda x: _mm2(x, x), d2)
    d8 = each(lambda x: _mm2(x, x), d4)
    t = each(lambda x: eye - x, d16)
    for p in (d2, d4, d8):
        t = each(lambda x, y: x + _mm2(x, y), t, p)
    off32 = jnp.logical_and(b32, jnp.logical_not(b16))
    for sel in (off32, jnp.logical_not(b32)):
        a_off = each(lambda a: jnp.where(sel, a, 0.0), mats)
        inner = each(_mm2, a_off, t)
        t = each(lambda x, y: x - _mm2(x, y), t, inner)
    return t


def _gdn_kernel(q_ref, k_ref, v_ref, sm_ref, dtb_ref, nega_ref, o_ref,
                u_s, w_s, kd_s, qg_s, aqk_s, dec_s, st_s):
    d = pl.program_id(2)
    j = pl.program_id(3)
    is_fwd = d == 0

    @pl.when(j == 0)
    def _():
        st_s[...] = jnp.zeros_like(st_s)

    sm = sm_ref[0]
    m_cum, m_all = _chunk_masks(is_fwd, TT)
    first = d * GDN_HEADS + pl.program_id(1) * GDN_HB
    rot = jnp.where(first == 0, 0, LANE - first)
    la = pltpu.roll(_softplus(sm + dtb_ref[...]) * nega_ref[...], rot, 1)
    be_sm = pltpu.roll(jax.nn.sigmoid(sm), rot, 1)
    gc_sm = _sel_dot(m_cum, la)
    gl_sm = _sel_dot(m_all, la)
    lane_bcast = lambda a, col: jnp.broadcast_to(a[:, col:col + 1], (TT, GDN_DK))

    r = lax.broadcasted_iota(jnp.int32, (CHUNK, CHUNK), 0)
    c = lax.broadcasted_iota(jnp.int32, (CHUNK, CHUNK), 1)
    causal = _causal_mask(is_fwd, CHUNK)
    strict = jnp.logical_and(causal, r != c)
    b16 = lax.shift_right_logical(r, 4) == lax.shift_right_logical(c, 4)
    b32 = lax.shift_right_logical(r, 5) == lax.shift_right_logical(c, 5)
    eye = jnp.where(r == c, 1.0, 0.0)
    chunk_rows = [slice(ci * CHUNK, (ci + 1) * CHUNK) for ci in range(CPT)]

    amats, rhss = [], []
    for hh in range(GDN_HB):
        cols = slice(hh * GDN_DK, (hh + 1) * GDN_DK)
        gc = lane_bcast(gc_sm, hh)
        gl = lane_bcast(gl_sm, hh)
        beta_e = lane_bcast(be_sm, 2 * GDN_HEADS + hh)
        q = q_ref[0, :, cols]
        k = k_ref[0, :, cols]
        egc = jnp.exp(gc)
        kb = k * beta_e
        qg_s[:, cols] = (q * egc).astype(BF16)
        kd_s[:, cols] = k * jnp.exp(gl - gc)
        dec_s[:, cols] = jnp.exp(gl)
        rhs = jnp.concatenate([v_ref[0, :, cols] * beta_e, kb * egc], axis=1)
        for ci, rows in enumerate(chunk_rows):
            gcc = gc[rows, :]
            dmat = jnp.exp(jnp.where(causal, gcc[:, 0:CHUNK] - gcc.T[0:CHUNK, :], -1e30))
            kc = k[rows].astype(BF16)
            amats.append(jnp.where(strict, _dot_nt(kb[rows].astype(BF16), kc) * dmat, 0.0))
            rhss.append(rhs[rows])
            aqk_s[rows, hh * CHUNK:(hh + 1) * CHUNK] = (_dot_nt(q[rows].astype(BF16), kc) * dmat).astype(BF16)
    tinv = _unit_tri_inverse(amats, b16, b32, eye)
    for n, (t, rhs_c) in enumerate(zip(tinv, rhss)):
        hh, rows = n // CPT, chunk_rows[n % CPT]
        cols = slice(hh * GDN_DK, (hh + 1) * GDN_DK)
        sol = _mm2(t, rhs_c)
        u_s[rows, cols] = sol[:, 0:GDN_DV]
        w_s[rows, cols] = sol[:, GDN_DV:].astype(BF16)

    for ci in range(CPT):
        off = pl.multiple_of(jnp.where(is_fwd, ci, CPT - 1 - ci) * CHUNK, CHUNK)
        rows = pl.ds(off, CHUNK)
        for hh in range(GDN_HB):
            cols = slice(hh * GDN_DK, (hh + 1) * GDN_DK)
            st = st_s[hh]
            stb = st.astype(BF16)
            vn = u_s[rows, cols] - _dot(w_s[rows, cols], stb)
            vnb = vn.astype(BF16)
            o_ref[0, 0, rows, cols] = (_dot(qg_s[rows, cols], stb)
                                       + _dot(aqk_s[rows, hh * CHUNK:(hh + 1) * CHUNK], vnb))
            st_s[hh] = st * dec_s[pl.ds(off, 1), cols] + _dot(kd_s[rows, cols].T.astype(BF16), vnb)


def gdn_scan(qkv, proj, dtb, nega):
    bsz, t, _ = qkv.shape
    nt = t // TT
    tile = lambda d, j: _scan_tile(d, j, nt)
    ng = GDN_HEADS // GDN_HB
    wb = GDN_HB * GDN_DK
    return pl.pallas_call(
        _gdn_kernel,
        grid=(bsz, ng, 2, nt),
        in_specs=[
            pl.BlockSpec((1, TT, wb), lambda i, h, d, j: (i, tile(d, j), h)),
            pl.BlockSpec((1, TT, wb), lambda i, h, d, j: (i, tile(d, j), ng + h)),
            pl.BlockSpec((1, TT, wb), lambda i, h, d, j: (i, tile(d, j), 2 * ng + h)),
            pl.BlockSpec((1, TT, LANE), lambda i, h, d, j: (i, tile(d, j), 6 * D_MODEL // LANE)),
            pl.BlockSpec((1, LANE), lambda i, h, d, j: (0, 0)),
            pl.BlockSpec((1, LANE), lambda i, h, d, j: (0, 0)),
        ],
        out_specs=pl.BlockSpec((1, 1, TT, wb), lambda i, h, d, j: (d, i, tile(d, j), h)),
        out_shape=jax.ShapeDtypeStruct((2, bsz, t, GDN_V), F32),
        scratch_shapes=[pltpu.VMEM((TT, wb), F32), pltpu.VMEM((TT, wb), BF16),
                        pltpu.VMEM((TT, wb), F32), pltpu.VMEM((TT, wb), BF16),
                        pltpu.VMEM((TT, GDN_HB * CHUNK), BF16), pltpu.VMEM((TT, wb), F32),
                        pltpu.VMEM((GDN_HB, GDN_DK, GDN_DV), F32)],
        compiler_params=pltpu.CompilerParams(
            dimension_semantics=("parallel", "parallel", "parallel", "arbitrary")),
        name="gdn_scan",
    )(qkv, qkv, qkv, proj, dtb, nega)


CONF_PAD = 16


def _cd_out_kernel(ga_ref, gb_ref, og_ref, of_ref, ob_ref, cw_ref, cb_ref, lg_ref, lb_ref, ng_ref,
                   w_ref, b_ref, x_ref, gate_ref, o_ref, pad_s, conv_s):
    j = pl.program_id(1)
    half = (CONF_KERNEL - 1) // 2
    glu = ga_ref[0] * jax.nn.sigmoid(gb_ref[0])
    zeros = jnp.zeros((CONF_PAD, CONF_CH), F32)

    def conv_segments(seglen):
        stride = seglen + 2 * CONF_PAD
        for g in range(TT // seglen):
            base = g * stride
            pad_s[base:base + CONF_PAD, :] = zeros
            pad_s[base + CONF_PAD:base + CONF_PAD + seglen, :] = glu[g * seglen:(g + 1) * seglen]
            pad_s[base + CONF_PAD + seglen:base + stride, :] = zeros
        for g in range(TT // seglen):
            lo = g * stride + CONF_PAD - half
            acc = jnp.zeros((seglen, CONF_CH), F32) + cb_ref[...]
            for kk in range(CONF_KERNEL):
                acc = acc + pad_s[lo + kk:lo + kk + seglen, :] * cw_ref[kk:kk + 1, :]
            conv_s[g * seglen:(g + 1) * seglen, :] = acc

    @pl.when(j == 0)
    def _():
        conv_segments(TT)

    @pl.when(j > 0)
    def _():
        conv_segments(GRID_W)

    acc = conv_s[...]
    mu = jnp.mean(acc, axis=-1, keepdims=True)
    cen = acc - mu
    var = jnp.mean(cen * cen, axis=-1, keepdims=True)
    conv = _silu(cen * lax.rsqrt(var + 1e-5) * lg_ref[...] + lb_ref[...])
    o = _group_rmsnorm(of_ref[0, 0] + ob_ref[0, 0], GDN_DV) * ng_ref[...] * _silu(og_ref[0])
    m = _dot(conv.astype(BF16), w_ref[0:CONF_CH, :]) + _dot(o.astype(BF16), w_ref[CONF_CH:, :]) + b_ref[...]
    o_ref[0] = x_ref[0] + gate_ref[0, 0] * m


def cd_out(o_gdn, proj, conv_w, conv_b, ln_g, ln_b, norm_g, w, b, x, gate):
    bsz, t, d = x.shape
    seg = lambda i, j: (i, jnp.minimum(j, 1), 0, 0)
    row = lambda width: pl.BlockSpec((1, width), lambda i, j: (0, 0))
    return pl.pallas_call(
        _cd_out_kernel,
        grid=(bsz, t // TT),
        in_specs=[
            pl.BlockSpec((1, TT, CONF_CH), lambda i, j: (i, j, 0)),
            pl.BlockSpec((1, TT, CONF_CH), lambda i, j: (i, j, 1)),
            pl.BlockSpec((1, TT, GDN_V), lambda i, j: (i, j, 5)),
            pl.BlockSpec((1, 1, TT, GDN_V), lambda i, j: (0, i, j, 0)),
            pl.BlockSpec((1, 1, TT, GDN_V), lambda i, j: (1, i, j, 0)),
            pl.BlockSpec((CONF_KERNEL, CONF_CH), lambda i, j: (0, 0)),
            row(CONF_CH), row(CONF_CH), row(CONF_CH), row(GDN_V),
            pl.BlockSpec((CONF_CH + GDN_V, d), lambda i, j: (0, 0)),
            row(d),
            pl.BlockSpec((1, TT, d), lambda i, j: (i, j, 0)),
            pl.BlockSpec((1, 1, 1, d), seg),
        ],
        out_specs=pl.BlockSpec((1, TT, d), lambda i, j: (i, j, 0)),
        out_shape=jax.ShapeDtypeStruct((bsz, t, d), F32),
        scratch_shapes=[pltpu.VMEM(((TT // GRID_W) * (GRID_W + 2 * CONF_PAD), CONF_CH), F32),
                        pltpu.VMEM((TT, CONF_CH), F32)],
        compiler_params=pltpu.CompilerParams(
            dimension_semantics=("parallel", "parallel"), vmem_limit_bytes=VMEM_LIMIT),
        name="cd_out",
    )(proj, proj, proj, o_gdn, o_gdn, conv_w, conv_b.reshape(1, -1), ln_g.reshape(1, -1), ln_b.reshape(1, -1),
      norm_g.reshape(1, -1), w, b.reshape(1, d), x, gate)


def _out_proj_kernel(m_ref, w_ref, b_ref, x_ref, gate_ref, o_ref):
    y = _dot(m_ref[0].astype(BF16), w_ref[...]) + b_ref[...]
    o_ref[0] = x_ref[0] + gate_ref[0] * y


def out_proj_residual(mixed, w, b, x, gate):
    bsz, t, k = mixed.shape
    d = w.shape[1]
    tm = min(t, 512)
    return pl.pallas_call(
        _out_proj_kernel,
        grid=(bsz, t // tm),
        in_specs=[
            pl.BlockSpec((1, tm, k), lambda i, j: (i, j, 0)),
            pl.BlockSpec((k, d), lambda i, j: (0, 0)),
            pl.BlockSpec((1, d), lambda i, j: (0, 0)),
            pl.BlockSpec((1, tm, d), lambda i, j: (i, j, 0)),
            pl.BlockSpec((1, 1, d), lambda i, j: (i, 0, 0)),
        ],
        out_specs=pl.BlockSpec((1, tm, d), lambda i, j: (i, j, 0)),
        out_shape=jax.ShapeDtypeStruct((bsz, t, d), F32),
        compiler_params=pltpu.CompilerParams(
            dimension_semantics=("parallel", "parallel"), vmem_limit_bytes=VMEM_LIMIT),
        name="out_proj",
    )(mixed, w, b.reshape(1, d), x, gate.reshape(bsz, 1, d))


def _router_kernel(x_ref, g_ref, sh_ref, sc_ref, rw_ref, h_ref, aff_ref):
    x = x_ref[0]
    ms = jnp.mean(x * x, axis=-1, keepdims=True)
    h = (x * lax.rsqrt(ms + 1e-6) * g_ref[...] * (1.0 + sc_ref[0, 0]) + sh_ref[0, 0]).astype(BF16)
    h_ref[0] = h
    logits = _dot(h, rw_ref[...])
    lane = lax.broadcasted_iota(jnp.int32, logits.shape, 1)
    logits = jnp.where(lane < N_EXPERTS, logits, -1e30)
    e = jnp.exp(logits - jnp.max(logits, axis=-1, keepdims=True))
    aff = e / jnp.sum(e, axis=-1, keepdims=True)
    aff_ref[0] = aff.T[0:N_EXPERTS, :]


def moe_router(xa, g, shift, scale, rw, seg, tile0, ntiles):
    bsz, _, d = xa.shape
    n = ntiles * TT
    return pl.pallas_call(
        _router_kernel,
        grid=(bsz, ntiles),
        in_specs=[
            pl.BlockSpec((1, TT, d), lambda i, j: (i, j + tile0, 0)),
            pl.BlockSpec((1, d), lambda i, j: (0, 0)),
            pl.BlockSpec((1, 1, 1, d), lambda i, j: (i, seg, 0, 0)),
            pl.BlockSpec((1, 1, 1, d), lambda i, j: (i, seg, 0, 0)),
            pl.BlockSpec((d, LANE), lambda i, j: (0, 0)),
        ],
        out_specs=[pl.BlockSpec((1, TT, d), lambda i, j: (i, j, 0)),
                   pl.BlockSpec((1, N_EXPERTS, TT), lambda i, j: (i, 0, j))],
        out_shape=[jax.ShapeDtypeStruct((bsz, n, d), BF16), jax.ShapeDtypeStruct((bsz, N_EXPERTS, n), F32)],
        compiler_params=pltpu.CompilerParams(dimension_semantics=("parallel", "parallel")),
        name="moe_router",
    )(xa, g.reshape(1, d), shift, scale, rw)


def _lane_block_prefix(x, u_strict):
    nblk = x.shape[1] // LANE
    run = jnp.zeros((x.shape[0], 1), F32)
    outs = []
    for cblk in range(nblk):
        xc = x[:, cblk * LANE:(cblk + 1) * LANE]
        outs.append(_dot(xc.astype(BF16), u_strict) + run)
        run = run + jnp.sum(xc, axis=-1, keepdims=True)
    return jnp.concatenate(outs, axis=1), run


def _select_kernel(aff_ref, slot_ref, *, cap):
    aff = aff_ref[0]
    bits = pltpu.bitcast(aff, jnp.int32)
    capf = jnp.float32(cap)

    def step(i, thr):
        cand = jnp.bitwise_or(thr, lax.shift_left(jnp.int32(1), 30 - i))
        cnt = jnp.sum(jnp.where(bits >= cand, 1.0, 0.0), axis=-1, keepdims=True)
        return jnp.where(cnt >= capf, cand, thr)

    thr = lax.fori_loop(0, 31, step, jnp.zeros((aff.shape[0], 1), jnp.int32))
    gt = jnp.where(bits > thr, 1.0, 0.0)
    eq = jnp.where(bits == thr, 1.0, 0.0)
    r = lax.broadcasted_iota(jnp.int32, (LANE, LANE), 0)
    c = lax.broadcasted_iota(jnp.int32, (LANE, LANE), 1)
    u_strict = jnp.where(r < c, 1.0, 0.0).astype(BF16)
    need = capf - jnp.sum(gt, axis=-1, keepdims=True)
    eq_rank, _ = _lane_block_prefix(eq, u_strict)
    sel = jnp.maximum(gt, jnp.where(eq_rank < need, eq, 0.0))
    slot, _ = _lane_block_prefix(sel, u_strict)
    slot_ref[0] = jnp.where(sel > 0.0, slot.astype(jnp.int32), -1)


def moe_select(aff, cap):
    bsz, ne, n = aff.shape
    return pl.pallas_call(
        functools.partial(_select_kernel, cap=cap),
        grid=(bsz,),
        in_specs=[pl.BlockSpec((1, ne, n), lambda i: (i, 0, 0))],
        out_specs=pl.BlockSpec((1, ne, n), lambda i: (i, 0, 0)),
        out_shape=jax.ShapeDtypeStruct((bsz, ne, n), jnp.int32),
        compiler_params=pltpu.CompilerParams(dimension_semantics=("parallel",)),
        name="moe_select",
    )(aff)


def _slot_index_kernel(slot_ref, idx_ref, *, cap):
    slot = slot_ref[0]
    n = slot.shape[1]
    srow = lax.broadcasted_iota(jnp.int32, (cap, LANE), 0)
    lane = lax.broadcasted_iota(jnp.int32, (cap, LANE), 1)
    acc = jnp.zeros((cap, LANE), jnp.int32)
    for cblk in range(n // LANE):
        s_c = slot[:, cblk * LANE:(cblk + 1) * LANE]
        acc = acc + jnp.where(srow == s_c, lane + (cblk * LANE + 1), 0)
    ones = jnp.ones((8, LANE), BF16)
    hi = _dot_nt(ones, lax.shift_right_logical(acc, 7).astype(F32).astype(BF16))
    lo = _dot_nt(ones, jnp.bitwise_and(acc, LANE - 1).astype(F32).astype(BF16))
    idx_ref[0] = (hi[0:1, :] * float(LANE) + lo[0:1, :]).astype(jnp.int32) - 1


def moe_slot_index(slot, cap):
    bsz, ne, n = slot.shape
    idx = pl.pallas_call(
        functools.partial(_slot_index_kernel, cap=cap),
        grid=(bsz * ne,),
        in_specs=[pl.BlockSpec((1, 1, n), lambda i: (i, 0, 0))],
        out_specs=pl.BlockSpec((1, 1, cap), lambda i: (i, 0, 0)),
        out_shape=jax.ShapeDtypeStruct((bsz * ne, 1, cap), jnp.int32),
        compiler_params=pltpu.CompilerParams(dimension_semantics=("parallel",)),
        name="moe_slot_index",
    )(slot.reshape(bsz * ne, 1, n))
    return idx.reshape(bsz, ne, cap)


WIN_ALIGN = 16
WIN_FAST = 128


def _combine_kernel(ws_ref, slot_ref, aff_ref, *rest, win):
    ye_refs, (x_ref, gate_ref, o_ref) = rest[:N_EXPERTS], rest[N_EXPERTS:]
    b = pl.program_id(0)
    j = pl.program_id(1)
    srow = lax.broadcasted_iota(jnp.int32, (win, TT), 0)
    his, los = [], []
    for e in range(N_EXPERTS):
        sel = jnp.where(srow + ws_ref[b, e, j] == slot_ref[0, e:e + 1, :], aff_ref[0, e:e + 1, :], 0.0).T
        hi = sel.astype(BF16)
        his.append(hi)
        los.append((sel - hi.astype(F32)).astype(BF16))
    ye = jnp.concatenate([r[...] for r in ye_refs], axis=0)
    acc = _dot(jnp.concatenate(his, axis=1), ye) + _dot(jnp.concatenate(los, axis=1), ye)
    o_ref[0] = x_ref[0] + gate_ref[0, 0] * acc


def moe_combine(ws, slot, aff, ye, xa, gate, seg, tile0, win):
    bsz, ne, n = slot.shape
    nt = n // TT
    d = xa.shape[2]

    def ye_spec(e):
        return pl.BlockSpec((pl.Squeezed(), pl.Squeezed(), pl.Element(win), pl.Element(d)),
                            lambda i, j, ws_ref: (i, e, pl.multiple_of(ws_ref[i, e, j], WIN_ALIGN), 0))

    return pl.pallas_call(
        functools.partial(_combine_kernel, win=win),
        grid_spec=pltpu.PrefetchScalarGridSpec(
            num_scalar_prefetch=1,
            grid=(bsz, nt),
            in_specs=[pl.BlockSpec((1, ne, TT), lambda i, j, ws_ref: (i, 0, j)),
                      pl.BlockSpec((1, ne, TT), lambda i, j, ws_ref: (i, 0, j))]
            + [ye_spec(e) for e in range(ne)]
            + [pl.BlockSpec((1, TT, d), lambda i, j, ws_ref: (i, j + tile0, 0)),
               pl.BlockSpec((1, 1, 1, d), lambda i, j, ws_ref: (i, seg, 0, 0))],
            out_specs=pl.BlockSpec((1, TT, d), lambda i, j, ws_ref: (i, j, 0)),
        ),
        out_shape=jax.ShapeDtypeStruct((bsz, n, d), F32),
        compiler_params=pltpu.CompilerParams(
            dimension_semantics=("parallel", "parallel"), vmem_limit_bytes=VMEM_LIMIT),
        name="moe_combine",
    )(ws, slot, aff, *([ye] * ne), xa, gate)


def _expert_ffn_kernel(x_ref, w1_ref, w3_ref, w2_ref, o_ref):
    x = x_ref[0, 0].astype(BF16)
    a = _dot(x, w1_ref[0])
    g = _dot(x, w3_ref[0])
    o_ref[0, 0] = _dot((_silu(a) * g).astype(BF16), w2_ref[0]).astype(BF16)


def expert_ffn(xe, w1, w3, w2):
    bsz, ne, cap, d = xe.shape
    f = w1.shape[2]
    tm = min(cap, 512)
    return pl.pallas_call(
        _expert_ffn_kernel,
        grid=(ne, bsz, cap // tm),
        in_specs=[
            pl.BlockSpec((1, 1, tm, d), lambda e, i, j: (i, e, j, 0)),
            pl.BlockSpec((1, d, f), lambda e, i, j: (e, 0, 0)),
            pl.BlockSpec((1, d, f), lambda e, i, j: (e, 0, 0)),
            pl.BlockSpec((1, f, d), lambda e, i, j: (e, 0, 0)),
        ],
        out_specs=pl.BlockSpec((1, 1, tm, d), lambda e, i, j: (i, e, j, 0)),
        out_shape=jax.ShapeDtypeStruct((bsz, ne, cap, d), BF16),
        compiler_params=pltpu.CompilerParams(
            dimension_semantics=("parallel", "parallel", "parallel"), vmem_limit_bytes=VMEM_LIMIT),
        name="expert_ffn",
    )(xe, w1, w3, w2)


def _rmsnorm_kernel(x_ref, g_ref, o_ref):
    x = x_ref[0]
    ms = jnp.mean(x * x, axis=-1, keepdims=True)
    o_ref[0] = x * lax.rsqrt(ms + 1e-6) * g_ref[...]


def rmsnorm_rows(x, g):
    bsz, t, d = x.shape
    tm = min(t, 1024)
    return pl.pallas_call(
        _rmsnorm_kernel,
        grid=(bsz, t // tm),
        in_specs=[pl.BlockSpec((1, tm, d), lambda i, j: (i, j, 0)), pl.BlockSpec((1, d), lambda i, j: (0, 0))],
        out_specs=pl.BlockSpec((1, tm, d), lambda i, j: (i, j, 0)),
        out_shape=jax.ShapeDtypeStruct((bsz, t, d), F32),
        compiler_params=pltpu.CompilerParams(dimension_semantics=("parallel", "parallel")),
        name="final_rmsnorm",
    )(x, g.reshape(1, d))


def _ab_in_layout(w_in, b_in):
    q, k, v, r, glr, z, xs, bm, cm, dt = _split_cols(
        jnp.concatenate([w_in, b_in[None]], axis=0),
        (GLA_QK, GLA_QK, GLA_V, GLA_V, 2 * GLA_GATE_RANK, SSD_INNER, SSD_INNER, SSD_BC, SSD_BC, 2 * SSD_HEADS))
    wb = _pad_cols(jnp.concatenate([q, k, v, r, z, xs, bm, cm, glr, dt], axis=1), AB_N)
    return wb[:-1].astype(BF16), wb[-1]


def _gla_gate_params(w_gate2, b_gate2):
    wg = jnp.zeros((2, LANE, GLA_QK), F32)
    for d in range(2):
        wg = wg.at[d, d * GLA_GATE_RANK:(d + 1) * GLA_GATE_RANK, :].set(w_gate2[d])
    return wg.astype(BF16), b_gate2.reshape(2, 1, GLA_QK)


def _ssd_params(dt_bias, a_log):
    gw = SSD_HPG * SSD_HEADDIM
    dtb = jnp.zeros((2, 1, LANE), F32)
    nega = jnp.zeros((2, 1, LANE), F32)
    e = np.zeros((2, SSD_GROUPS, LANE, gw), np.float32)
    e8 = np.zeros((2, SSD_GROUPS, LANE, LANE), np.float32)
    for d in range(2):
        c0 = SM_DT + d * SSD_HEADS
        dtb = dtb.at[d, 0, c0:c0 + SSD_HEADS].set(dt_bias[d])
        nega = nega.at[d, 0, c0:c0 + SSD_HEADS].set(-jnp.exp(a_log[d]))
        for g in range(SSD_GROUPS):
            for hh in range(SSD_HPG):
                c = c0 + g * SSD_HPG + hh
                e[d, g, c, hh * SSD_HEADDIM:(hh + 1) * SSD_HEADDIM] = 1.0
                e8[d, g, c, hh] = 1.0
    return dtb, nega, jnp.asarray(e, BF16), jnp.asarray(e8, BF16)


def _gdn_params(dt_bias, a_log):
    n = 2 * GDN_HEADS
    dtb = jnp.zeros((1, LANE), F32).at[0, 0:n].set(dt_bias.reshape(n))
    nega = jnp.zeros((1, LANE), F32).at[0, 0:n].set(-jnp.exp(a_log.reshape(n)))
    return dtb, nega


def _rmsnorm(x, g, eps=1e-6):
    return x * lax.rsqrt(jnp.mean(jnp.square(x), axis=-1, keepdims=True) + eps) * g


def _head_rmsnorm(x, g):
    return _rmsnorm(x, g.reshape(x.shape[-2:]))


def _layernorm(x, g, b, eps=1e-5):
    mu = jnp.mean(x, axis=-1, keepdims=True)
    var = jnp.mean(jnp.square(x - mu), axis=-1, keepdims=True)
    return (x - mu) * lax.rsqrt(var + eps) * g + b


def _l2norm(x, eps=1e-6):
    return x * lax.rsqrt(jnp.sum(jnp.square(x), axis=-1, keepdims=True) + eps)


def _dwconv(x, w, b=None):
    k, ch = w.shape
    pad = (k - 1) // 2
    y = lax.conv_general_dilated(x, w[:, None, :], (1,), [(pad, pad)],
                                 dimension_numbers=('NWC', 'WIO', 'NWC'), feature_group_count=ch)
    return y if b is None else y + b


def _gdn_scan(q, k, v, beta, logg, s0, with_out):
    bsz, nh, t, dk = q.shape
    dv = v.shape[-1]
    nc = t // CHUNK
    chunks = lambda z: z.reshape(bsz, nh, nc, CHUNK, *z.shape[3:])
    q, k, v, beta, logg = (chunks(z) for z in (q, k, v, beta, logg))
    gc = jnp.cumsum(logg, axis=-1)
    glast = gc[..., -1]
    tril = jnp.tril(jnp.ones((CHUNK, CHUNK), dtype=bool))
    strict = jnp.tril(jnp.ones((CHUNK, CHUNK), dtype=bool), k=-1)
    decay = jnp.exp(jnp.where(tril, gc[..., :, None] - gc[..., None, :], -jnp.inf))
    kb = k * beta[..., None]
    m = jnp.eye(CHUNK, dtype=k.dtype) + jnp.where(strict, jnp.einsum('bhcid,bhcjd->bhcij', kb, k) * decay, 0.0)
    rhs = jnp.concatenate([v * beta[..., None], kb * jnp.exp(gc)[..., None]], axis=-1)
    sol = lax.linalg.triangular_solve(m, rhs, left_side=True, lower=True, unit_diagonal=True)
    u, w = sol[..., :dv], sol[..., dv:]
    kd = k * jnp.exp(glast[..., None] - gc)[..., None]
    dec = jnp.exp(glast)
    front = lambda z: jnp.moveaxis(z, 2, 0)

    def advance(s, w_c, u_c, kd_c, dec_c):
        vn = u_c - jnp.einsum('bhid,bhde->bhie', w_c, s)
        return vn, dec_c[..., None, None] * s + jnp.einsum('bhjd,bhje->bhde', kd_c, vn)

    xs = (front(w), front(u), front(kd), front(dec))
    if not with_out:
        s_fin, _ = lax.scan(lambda s, xc: (advance(s, *xc)[1], None), s0, xs)
        return None, s_fin
    aqk = jnp.einsum('bhcid,bhcjd->bhcij', q, k) * decay
    qg = q * jnp.exp(gc)[..., None]

    def step(s, xc):
        w_c, u_c, kd_c, dec_c, aqk_c, qg_c = xc
        vn, s_new = advance(s, w_c, u_c, kd_c, dec_c)
        o = jnp.einsum('bhid,bhde->bhie', qg_c, s) + jnp.einsum('bhij,bhje->bhie', aqk_c, vn)
        return s_new, o

    s_fin, o = lax.scan(step, s0, xs + (front(aqk), front(qg)))
    return jnp.moveaxis(o, 0, 2).reshape(bsz, nh, t, dv), s_fin


def _bidir_scan(scan_fn, args_f, args_b, t_axis, init_f, init_b, with_out):
    flip = lambda a: jnp.flip(a, axis=t_axis)
    o_f, s_f = scan_fn(*args_f, init_f, with_out)
    o_b, s_b = scan_fn(*[flip(a) for a in args_b], init_b, with_out)
    o = o_f + flip(o_b) if with_out else None
    return o, s_f, s_b


def _conformer_conv(u, rows, dw_w, dw_b, ln_g, ln_b):
    bsz, t, ch = u.shape
    y = _dwconv(u.reshape(bsz * rows, t // rows, ch), dw_w, dw_b).reshape(bsz, t, ch)
    return jax.nn.silu(_layernorm(y, ln_g, ln_b))


def _cd_stream(proj, gdn_conv_w, gdn_a_log, gdn_dt_bias):
    bsz, t, _ = proj.shape
    ga, gb, q, k, v, og, a_raw, b_raw = _split_cols(proj, CD_SPLITS)
    glu = ga * jax.nn.sigmoid(gb)
    qkv = jax.nn.silu(_dwconv(jnp.concatenate([q, k, v], axis=-1), gdn_conv_w))
    q, k, v = _split_cols(qkv, (GDN_QK, GDN_QK, GDN_V))
    heads = lambda a: a.reshape(bsz, t, GDN_HEADS, -1).transpose(0, 2, 1, 3)
    q = _l2norm(heads(q)) * GDN_DK ** -0.5
    k = _l2norm(heads(k))
    v = heads(v)
    beta = jax.nn.sigmoid(b_raw.reshape(bsz, t, 2, GDN_HEADS)).transpose(2, 0, 3, 1)
    logg = (-jnp.exp(gdn_a_log)
            * jax.nn.softplus(a_raw.reshape(bsz, t, 2, GDN_HEADS) + gdn_dt_bias)).transpose(2, 0, 3, 1)
    return (q, k, v, beta[0], logg[0]), (q, k, v, beta[1], logg[1]), glu, og


def _mixer_cd(proj_l, proj_c, rows, conf_dw_w, conf_dw_b, conf_ln_g, conf_ln_b,
              gdn_conv_w, gdn_a_log, gdn_dt_bias, gdn_norm_g, need_ctx):
    c_f, c_b, c_glu, c_og = _cd_stream(proj_c, gdn_conv_w, gdn_a_log, gdn_dt_bias)
    l_f, l_b, l_glu, l_og = _cd_stream(proj_l, gdn_conv_w, gdn_a_log, gdn_dt_bias)
    bsz = proj_l.shape[0]
    z0 = jnp.zeros((bsz, GDN_HEADS, GDN_DK, GDN_DV), F32)
    od_c, s_f, s_b = _bidir_scan(_gdn_scan, c_f, c_b, 2, z0, z0, need_ctx)
    od_l, _, _ = _bidir_scan(_gdn_scan, l_f, l_b, 2, s_f, s_b, True)

    def mix(od, glu, og, n_rows):
        t = glu.shape[1]
        conv = _conformer_conv(glu, n_rows, conf_dw_w, conf_dw_b, conf_ln_g, conf_ln_b)
        o = _head_rmsnorm(jnp.swapaxes(od, 1, 2), gdn_norm_g) * jax.nn.silu(og).reshape(bsz, t, GDN_HEADS, GDN_DV)
        return jnp.concatenate([conv, o.reshape(bsz, t, GDN_V)], axis=-1)

    return mix(od_l, l_glu, l_og, rows), (mix(od_c, c_glu, c_og, 1) if need_ctx else None)


def moe_segment(i, xa, mods, p, w1, w3, w2, seg, tile0, ntiles):
    bsz, _, d = xa.shape
    n = ntiles * TT
    cap = n * EC_CAPACITY // N_EXPERTS
    rw = _pad_cols(p["moe_router"][i], LANE).astype(BF16)
    h, aff = moe_router(xa, p["norm2_g"][i], mods[:, :, 3], mods[:, :, 4], rw, seg, tile0, ntiles)
    slot = moe_select(aff, cap)
    idx = moe_slot_index(slot, cap).reshape(bsz, N_EXPERTS * cap)
    xe = jnp.take_along_axis(h, idx[..., None], axis=1).reshape(bsz, N_EXPERTS, cap, d)
    ye = expert_ffn(xe, w1, w3, w2)
    counts = jnp.sum((slot >= 0).reshape(bsz, N_EXPERTS, ntiles, TT), axis=-1, dtype=jnp.int32)
    starts = jnp.cumsum(counts, axis=-1) - counts
    aligned = starts // WIN_ALIGN * WIN_ALIGN

    def run(win):
        ws = jnp.minimum(aligned, cap - win)
        return moe_combine(ws, slot, aff, ye, xa, mods[:, :, 5], seg, tile0, win)

    win_fast, win_full = min(cap, WIN_FAST), min(cap, TT + WIN_ALIGN)
    if win_fast == win_full:
        return run(win_full)
    overflow = jnp.any(starts + counts - jnp.minimum(aligned, cap - win_fast) > win_fast)
    return lax.cond(overflow, lambda: run(win_full), lambda: run(win_fast))


def layer_mixer(i, j, xa, mods, p, last):
    sh1, sc1, g1 = (mods[:, :, s] for s in range(3))
    if i % 2 == 0:
        w_in, b_in = _ab_in_layout(p["ab_w_in"][j], p["ab_b_in"][j])
        proj = norm_proj(xa, p["norm1_g"][i], sh1, sc1, w_in, b_in, 1920)
        xbc = ab_prep(proj, p["ssd_conv_w"][j], p["ssd_conv_b"][j])
        wg, bg = _gla_gate_params(p["gla_w_gate2"][j], p["gla_b_gate2"][j])
        o_gla = gla_scan(proj, wg, bg)
        y_ssd = ssd_scan(xbc, proj, *_ssd_params(p["ssd_dt_bias"][j], p["ssd_a_log"][j]))
        return ab_out(o_gla, y_ssd, proj, xbc, p["gla_norm_g"][j], jnp.repeat(p["ssd_d"][j], SSD_HEADDIM),
                      p["ssd_norm_g"][j], p["ab_w_out"][j].astype(BF16), p["ab_b_out"][j], xa, g1)
    w_in = _pad_cols(p["cd_w_in"][j], CD_N).astype(BF16)
    b_in = _pad_cols(p["cd_b_in"][j], CD_N)
    proj = norm_proj(xa, p["norm1_g"][i], sh1, sc1, w_in, b_in, 1280)
    qkv = cd_prep(proj, p["gdn_conv_w"][j])
    o_gdn = gdn_scan(qkv, proj, *_gdn_params(p["gdn_dt_bias"][j], p["gdn_a_log"][j]))
    return cd_out(o_gdn, proj, p["conf_dw_w"][j], p["conf_dw_b"][j], p["conf_ln_g"][j], p["conf_ln_b"][j],
                  p["gdn_norm_g"][j], p["cd_w_out"][j].astype(BF16), p["cd_b_out"][j], xa, g1)


def kernel(x, c, ctx, c_ctx, mod_w, mod_b, norm1_g, norm2_g, ab_w_in, ab_b_in, ab_w_out, ab_b_out, gla_w_gate2, gla_b_gate2, gla_norm_g, ssd_conv_w, ssd_conv_b, ssd_dt_bias, ssd_a_log, ssd_d, ssd_norm_g, cd_w_in, cd_b_in, cd_w_out, cd_b_out, conf_dw_w, conf_dw_b, conf_ln_g, conf_ln_b, gdn_conv_w, gdn_a_log, gdn_dt_bias, gdn_norm_g, moe_router, moe_w1, moe_w3, moe_w2, final_norm_g):
    p = dict(locals())
    bsz, seq, d = x.shape
    assert ctx.shape[1] == TT and seq % TT == 0
    depth = mod_w.shape[0]
    xa = jnp.concatenate([ctx, x], axis=1)
    for i in range(depth):
        last = i == depth - 1
        mod_l = jax.nn.silu(c) @ mod_w[i] + mod_b[i]
        mod_c = jnp.broadcast_to(jax.nn.silu(c_ctx) @ mod_w[i] + mod_b[i], mod_l.shape)
        mods = jnp.stack([mod_c, mod_l], axis=1).reshape(bsz, 2, 6, 1, d)
        sh1, sc1, g1, sh2, sc2, g2 = (mods[:, :, s] for s in range(6))
        xa = layer_mixer(i, i // 2, xa, mods, p, last)
        w1, w3, w2 = moe_w1[i].astype(BF16), moe_w3[i].astype(BF16), moe_w2[i].astype(BF16)
        x_c, x_l = xa[:, :TT], xa[:, TT:]
        x_l = moe_segment(i, xa, mods, p, w1, w3, w2, 1, 1, seq // TT)
        if not last:
            x_c = moe_segment(i, xa, mods, p, w1, w3, w2, 0, 0, 1)
        xa = jnp.concatenate([x_c, x_l], axis=1)
    return rmsnorm_rows(xa[:, TT:], final_norm_g)
```

```python
import functools

import jax
import jax.numpy as jnp
import numpy as np
from jax import lax
from jax.experimental import pallas as pl
from jax.experimental.pallas import tpu as pltpu

F32 = jnp.float32
BF16 = jnp.bfloat16

D_MODEL = 1024
GRID_W = 64
CHUNK = 64
GLA_HEADS, GLA_DK, GLA_DV, GLA_GATE_RANK, GLA_GATE_TAU = 4, 128, 256, 16, 16.0
SSD_HEADS, SSD_HEADDIM, SSD_STATE, SSD_GROUPS = 16, 64, 128, 2
CONF_CH, CONF_KERNEL = D_MODEL, 31
GDN_HEADS, GDN_DK, GDN_DV = 8, 128, 128
N_EXPERTS, EC_CAPACITY, EXPERT_FF = 16, 2, D_MODEL

GLA_QK = GLA_HEADS * GLA_DK
GLA_V = GLA_HEADS * GLA_DV
SSD_INNER = SSD_HEADS * SSD_HEADDIM
SSD_BC = SSD_GROUPS * SSD_STATE
SSD_HPG = SSD_HEADS // SSD_GROUPS
GDN_QK = GDN_HEADS * GDN_DK
GDN_V = GDN_HEADS * GDN_DV
CD_SPLITS = (CONF_CH, CONF_CH, GDN_QK, GDN_QK, GDN_V, GDN_V, 2 * GDN_HEADS, 2 * GDN_HEADS)

LANE = 128
TT = 256
CPT = TT // CHUNK
VMEM_LIMIT = 48 * 1024 * 1024

AB_Q, AB_K, AB_V, AB_R, AB_Z, AB_XS, AB_BM, AB_CM, AB_SMALL = 0, 512, 1024, 2048, 3072, 4096, 5120, 5376, 5632
AB_N = AB_SMALL + LANE
SM_DT = 2 * GLA_GATE_RANK
CD_N = 6400


def _split_cols(a, sizes):
    return jnp.split(a, np.cumsum(sizes)[:-1].tolist(), axis=-1)


def _pad_cols(a, n):
    return jnp.pad(a, [(0, 0)] * (a.ndim - 1) + [(0, n - a.shape[-1])])


def _dot(a, b):
    return jnp.dot(a, b, preferred_element_type=F32)


def _dot_nt(a, b):
    return lax.dot_general(a, b, (((1,), (1,)), ((), ())), preferred_element_type=F32)


def _split3(x):
    hi = x.astype(BF16)
    r = x - hi.astype(F32)
    mid = r.astype(BF16)
    lo = (r - mid.astype(F32)).astype(BF16)
    return hi, mid, lo


def _sel_dot(m, x):
    hi, mid, lo = _split3(x)
    return _dot(m, hi) + _dot(m, mid) + _dot(m, lo)


def _dot_sel(x, e):
    hi, mid, lo = _split3(x)
    return _dot(hi, e) + _dot(mid, e) + _dot(lo, e)


def _softplus(x):
    return jnp.maximum(x, 0.0) + jnp.log(1.0 + jnp.exp(-jnp.abs(x)))


def _silu(x):
    return x * jax.nn.sigmoid(x)


def _chunk_masks(is_fwd, n):
    r = lax.broadcasted_iota(jnp.int32, (n, n), 0)
    c = lax.broadcasted_iota(jnp.int32, (n, n), 1)
    same = lax.shift_right_logical(r, 6) == lax.shift_right_logical(c, 6)
    lo = jnp.where(is_fwd, c, r)
    hi = jnp.where(is_fwd, r, c)
    cum = jnp.logical_and(same, lo <= hi)
    return jnp.where(cum, 1.0, 0.0).astype(BF16), jnp.where(same, 1.0, 0.0).astype(BF16)


def _causal_mask(is_fwd, n):
    r = lax.broadcasted_iota(jnp.int32, (n, n), 0)
    c = lax.broadcasted_iota(jnp.int32, (n, n), 1)
    return jnp.where(is_fwd, c, r) <= jnp.where(is_fwd, r, c)


def _scan_tile(d, j, nt):
    return jnp.where(d == 0, j, jnp.where(j == 0, 0, nt - j))


NP_TM = 3 * TT


def _norm_proj_kernel(x_ref, g_ref, sh_ref, sc_ref, w_ref, b_ref, o_ref):
    x = x_ref[0]
    row = lax.broadcasted_iota(jnp.int32, (NP_TM, 1), 0) + pl.program_id(2) * NP_TM
    is_ctx = row < TT
    scale = jnp.where(is_ctx, sc_ref[0, 0], sc_ref[0, 1])
    shift = jnp.where(is_ctx, sh_ref[0, 0], sh_ref[0, 1])
    ms = jnp.mean(x * x, axis=-1, keepdims=True)
    h = (x * lax.rsqrt(ms + 1e-6) * g_ref[...] * (1.0 + scale) + shift).astype(BF16)
    o_ref[0] = _dot(h, w_ref[...]) + b_ref[...]


def norm_proj(x, g, shift, scale, w, b, tn):
    bsz, t, d = x.shape
    n = w.shape[1]
    assert t % NP_TM == 0
    return pl.pallas_call(
        _norm_proj_kernel,
        grid=(n // tn, bsz, t // NP_TM),
        in_specs=[
            pl.BlockSpec((1, NP_TM, d), lambda k, i, j: (i, j, 0)),
            pl.BlockSpec((1, d), lambda k, i, j: (0, 0)),
            pl.BlockSpec((1, 2, 1, d), lambda k, i, j: (i, 0, 0, 0)),
            pl.BlockSpec((1, 2, 1, d), lambda k, i, j: (i, 0, 0, 0)),
            pl.BlockSpec((d, tn), lambda k, i, j: (0, k)),
            pl.BlockSpec((1, tn), lambda k, i, j: (0, k)),
        ],
        out_specs=pl.BlockSpec((1, NP_TM, tn), lambda k, i, j: (i, j, k)),
        out_shape=jax.ShapeDtypeStruct((bsz, t, n), F32),
        compiler_params=pltpu.CompilerParams(
            dimension_semantics=("parallel", "parallel", "parallel"), vmem_limit_bytes=VMEM_LIMIT),
        name="norm_proj",
    )(x, g.reshape(1, d), shift, scale, w, b.reshape(1, n))


def _conv3_piece(x, lh, rh, w, b, left_ok, right_ok):
    n = x.shape[0]
    row = lax.broadcasted_iota(jnp.int32, x.shape, 0)
    prev_row = jnp.where(left_ok, lh[7:8, :], 0.0)
    next_row = jnp.where(right_ok, rh[0:1, :], 0.0)
    x_prev = jnp.where(row == 0, prev_row, pltpu.roll(x, 1, 0))
    x_next = jnp.where(row == n - 1, next_row, pltpu.roll(x, n - 1, 0))
    return _silu(w[0:1, :] * x_prev + w[1:2, :] * x + w[2:3, :] * x_next + b)


def _ab_prep_kernel(xs_ref, bm_ref, cm_ref, xsl_ref, bml_ref, cml_ref, xsr_ref, bmr_ref, cmr_ref,
                    w_ref, b_ref, o_ref):
    j = pl.program_id(1)
    nt = pl.num_programs(1)
    left_ok = j >= 2
    right_ok = jnp.logical_and(j >= 1, j < nt - 1)
    w = w_ref[...]
    b = b_ref[...]
    o_ref[0, :, 0:SSD_INNER] = _conv3_piece(xs_ref[0], xsl_ref[0], xsr_ref[0], w[:, 0:SSD_INNER],
                                            b[:, 0:SSD_INNER], left_ok, right_ok)
    c0, c1 = SSD_INNER, SSD_INNER + SSD_BC
    o_ref[0, :, c0:c1] = _conv3_piece(bm_ref[0], bml_ref[0], bmr_ref[0], w[:, c0:c1], b[:, c0:c1], left_ok, right_ok)
    c0, c1 = c1, c1 + SSD_BC
    o_ref[0, :, c0:c1] = _conv3_piece(cm_ref[0], cml_ref[0], cmr_ref[0], w[:, c0:c1], b[:, c0:c1], left_ok, right_ok)


def ab_prep(proj, conv_w, conv_b):
    bsz, t, _ = proj.shape
    nt = t // TT
    rb = TT // 8
    nrb = t // 8
    cw = SSD_INNER + 2 * SSD_BC

    def cur(width, col):
        return pl.BlockSpec((1, TT, width), lambda i, j: (i, j, col // width))

    def left(width, col):
        return pl.BlockSpec((1, 8, width), lambda i, j: (i, jnp.maximum(j * rb - 1, 0), col // width))

    def right(width, col):
        return pl.BlockSpec((1, 8, width), lambda i, j: (i, jnp.minimum((j + 1) * rb, nrb - 1), col // width))

    pieces = ((SSD_INNER, AB_XS), (SSD_BC, AB_BM), (SSD_BC, AB_CM))
    return pl.pallas_call(
        _ab_prep_kernel,
        grid=(bsz, nt),
        in_specs=[cur(*p) for p in pieces] + [left(*p) for p in pieces] + [right(*p) for p in pieces] + [
            pl.BlockSpec((3, cw), lambda i, j: (0, 0)),
            pl.BlockSpec((1, cw), lambda i, j: (0, 0)),
        ],
        out_specs=pl.BlockSpec((1, TT, cw), lambda i, j: (i, j, 0)),
        out_shape=jax.ShapeDtypeStruct((bsz, t, cw), F32),
        compiler_params=pltpu.CompilerParams(dimension_semantics=("parallel", "parallel")),
        name="ab_prep",
    )(*([proj] * 9), conv_w, conv_b.reshape(1, cw))


def _gla_kernel(q_ref, k_ref, v_ref, sm_ref, wg_ref, bg_ref, o_ref, qg_s, kn_s, kd_s, egl_s, st_s):
    d = pl.program_id(1)
    j = pl.program_id(2)
    is_fwd = d == 0

    @pl.when(j == 0)
    def _():
        st_s[...] = jnp.zeros_like(st_s)

    gz = _dot(sm_ref[0].astype(BF16), wg_ref[0]) + bg_ref[0]
    logg = (jnp.minimum(gz, 0.0) - jnp.log(1.0 + jnp.exp(-jnp.abs(gz)))) * (1.0 / GLA_GATE_TAU)
    m_cum, m_all = _chunk_masks(is_fwd, TT)
    gc = _sel_dot(m_cum, logg)
    gl = _sel_dot(m_all, logg)
    q = q_ref[0] * (GLA_DK ** -0.5)
    k = k_ref[0]
    qg_s[...] = (q * jnp.exp(gc)).astype(BF16)
    kn_s[...] = (k * jnp.exp(-gc)).astype(BF16)
    kd_s[...] = (k * jnp.exp(gl - gc)).astype(BF16)
    egl_s[...] = jnp.exp(gl)
    causal = _causal_mask(is_fwd, CHUNK)

    for ci in range(CPT):
        off = pl.multiple_of(jnp.where(is_fwd, ci, CPT - 1 - ci) * CHUNK, CHUNK)
        rows = pl.ds(off, CHUNK)
        for h in range(GLA_HEADS):
            kc = slice(h * GLA_DK, (h + 1) * GLA_DK)
            vc = slice(h * GLA_DV, (h + 1) * GLA_DV)
            qg = qg_s[rows, kc]
            v = v_ref[0, rows, vc]
            st = st_s[h]
            att = jnp.where(causal, _dot_nt(qg, kn_s[rows, kc]), 0.0)
            o_ref[0, 0, rows, vc] = _dot(att.astype(BF16), v.astype(BF16)) + _dot_nt(qg, st.astype(BF16))
            st_s[h] = st * egl_s[pl.ds(off, 1), kc] + _dot(v.T.astype(BF16), kd_s[rows, kc])


def gla_scan(proj, wg, bg):
    bsz, t, _ = proj.shape
    nt = t // TT
    tile = lambda d, j: _scan_tile(d, j, nt)
    return pl.pallas_call(
        _gla_kernel,
        grid=(bsz, 2, nt),
        in_specs=[
            pl.BlockSpec((1, TT, GLA_QK), lambda i, d, j: (i, tile(d, j), AB_Q // GLA_QK)),
            pl.BlockSpec((1, TT, GLA_QK), lambda i, d, j: (i, tile(d, j), AB_K // GLA_QK)),
            pl.BlockSpec((1, TT, GLA_V), lambda i, d, j: (i, tile(d, j), AB_V // GLA_V)),
            pl.BlockSpec((1, TT, LANE), lambda i, d, j: (i, tile(d, j), AB_SMALL // LANE)),
            pl.BlockSpec((1, LANE, GLA_QK), lambda i, d, j: (d, 0, 0)),
            pl.BlockSpec((1, 1, GLA_QK), lambda i, d, j: (d, 0, 0)),
        ],
        out_specs=pl.BlockSpec((1, 1, TT, GLA_V), lambda i, d, j: (d, i, tile(d, j), 0)),
        out_shape=jax.ShapeDtypeStruct((2, bsz, t, GLA_V), F32),
        scratch_shapes=[pltpu.VMEM((TT, GLA_QK), BF16), pltpu.VMEM((TT, GLA_QK), BF16),
                        pltpu.VMEM((TT, GLA_QK), BF16), pltpu.VMEM((TT, GLA_QK), F32),
                        pltpu.VMEM((GLA_HEADS, GLA_DV, GLA_DK), F32)],
        compiler_params=pltpu.CompilerParams(
            dimension_semantics=("parallel", "parallel", "arbitrary")),
        name="gla_scan",
    )(proj, proj, proj, proj, wg, bg)


def _dot_sel2(x, e):
    hi = x.astype(BF16)
    return _dot(hi, e) + _dot((x - hi.astype(F32)).astype(BF16), e)


def _ssd_kernel(xs_ref, bm_ref, cm_ref, sm_ref, dtb_ref, nega_ref, e_ref, o_ref,
                v_s, vw_s, cdec_s, dec_s, ah_s, st_s):
    d = pl.program_id(1)
    j = pl.program_id(2)
    is_fwd = d == 0
    gw = SSD_HPG * SSD_HEADDIM

    @pl.when(j == 0)
    def _():
        st_s[...] = jnp.zeros_like(st_s)

    dt = _softplus(sm_ref[0] + dtb_ref[0])
    la = dt * nega_ref[0]
    m_cum, m_all = _chunk_masks(is_fwd, TT)
    acum = _sel_dot(m_cum, la)
    atot = _sel_dot(m_all, la)
    e = e_ref[0]
    v = xs_ref[0] * _dot_sel2(dt, e)
    v_s[...] = v.astype(BF16)
    vw_s[...] = (v * _dot_sel2(jnp.exp(atot - acum), e)).astype(BF16)
    cdec_s[...] = _dot_sel2(jnp.exp(acum), e)
    dec_s[...] = _dot_sel2(jnp.exp(atot), e)
    ah_s[...] = pltpu.roll(acum, LANE - SM_DT - d * SSD_HEADS, 1)
    causal = _causal_mask(is_fwd, CHUNK)

    for ci in range(CPT):
        off = pl.multiple_of(jnp.where(is_fwd, ci, CPT - 1 - ci) * CHUNK, CHUNK)
        rows = pl.ds(off, CHUNK)
        ah = ah_s[rows, :]
        aht = ah.T
        for g in range(SSD_GROUPS):
            gc = slice(g * gw, (g + 1) * gw)
            nc = slice(g * SSD_STATE, (g + 1) * SSD_STATE)
            bm = bm_ref[0, rows, nc]
            cm = cm_ref[0, rows, nc].astype(BF16)
            st = st_s[g]
            cb = _dot_nt(cm, bm.astype(BF16))
            y_inter = _dot(cm, st.astype(BF16)) * cdec_s[rows, gc]
            v_c = v_s[rows, gc]
            ys = []
            for hh in range(SSD_HPG):
                h = g * SSD_HPG + hh
                seg = jnp.exp(jnp.where(causal, ah[:, h:h + 1] - aht[h:h + 1, :], -1e30))
                ys.append(_dot((seg * cb).astype(BF16), v_c[:, hh * SSD_HEADDIM:(hh + 1) * SSD_HEADDIM]))
            o_ref[0, 0, rows, gc] = jnp.concatenate(ys, axis=1) + y_inter
            st_s[g] = st * dec_s[pl.ds(off, 1), gc] + _dot(bm.T.astype(BF16), vw_s[rows, gc])


def ssd_scan(xbc, proj, dtb, nega, e):
    bsz, t, _ = xbc.shape
    nt = t // TT
    gw = SSD_HPG * SSD_HEADDIM
    tile = lambda d, j: _scan_tile(d, j, nt)
    return pl.pallas_call(
        _ssd_kernel,
        grid=(bsz, 2, nt),
        in_specs=[
            pl.BlockSpec((1, TT, SSD_INNER), lambda i, d, j: (i, tile(d, j), 0)),
            pl.BlockSpec((1, TT, SSD_BC), lambda i, d, j: (i, tile(d, j), SSD_INNER // SSD_BC)),
            pl.BlockSpec((1, TT, SSD_BC), lambda i, d, j: (i, tile(d, j), SSD_INNER // SSD_BC + 1)),
            pl.BlockSpec((1, TT, LANE), lambda i, d, j: (i, tile(d, j), AB_SMALL // LANE)),
            pl.BlockSpec((1, 1, LANE), lambda i, d, j: (d, 0, 0)),
            pl.BlockSpec((1, 1, LANE), lambda i, d, j: (d, 0, 0)),
            pl.BlockSpec((1, LANE, SSD_INNER), lambda i, d, j: (d, 0, 0)),
        ],
        out_specs=pl.BlockSpec((1, 1, TT, SSD_INNER), lambda i, d, j: (d, i, tile(d, j), 0)),
        out_shape=jax.ShapeDtypeStruct((2, bsz, t, SSD_INNER), F32),
        scratch_shapes=[pltpu.VMEM((TT, SSD_INNER), BF16), pltpu.VMEM((TT, SSD_INNER), BF16),
                        pltpu.VMEM((TT, SSD_INNER), F32), pltpu.VMEM((TT, SSD_INNER), F32),
                        pltpu.VMEM((TT, LANE), F32), pltpu.VMEM((SSD_GROUPS, SSD_STATE, gw), F32)],
        compiler_params=pltpu.CompilerParams(
            dimension_semantics=("parallel", "parallel", "arbitrary")),
        name="ssd_scan",
    )(xbc, xbc, xbc, proj, dtb, nega, e)


def _group_rmsnorm(x, width):
    parts = []
    for s in range(x.shape[1] // width):
        seg = x[:, s * width:(s + 1) * width]
        parts.append(seg * lax.rsqrt(jnp.mean(seg * seg, axis=-1, keepdims=True) + 1e-6))
    return jnp.concatenate(parts, axis=1)


def _ab_out_kernel(of_ref, ob_ref, yf_ref, yb_ref, r_ref, z_ref, xs_ref, gg_ref, dv_ref, sg_ref,
                   w_ref, b_ref, x_ref, gate_ref, o_ref):
    o = _group_rmsnorm(of_ref[0, 0] + ob_ref[0, 0], GLA_DV) * gg_ref[...] * _silu(r_ref[0])
    y = (yf_ref[0, 0] + yb_ref[0, 0] + dv_ref[...] * xs_ref[0]) * _silu(z_ref[0])
    y = _group_rmsnorm(y, SSD_INNER // SSD_GROUPS) * sg_ref[...]
    m = _dot(o.astype(BF16), w_ref[0:GLA_V, :]) + _dot(y.astype(BF16), w_ref[GLA_V:, :]) + b_ref[...]
    o_ref[0] = x_ref[0] + gate_ref[0, 0] * m


def ab_out(o_gla, y_ssd, proj, xbc, gla_g, d_vec, ssd_g, w, b, x, gate):
    bsz, t, d = x.shape
    seg = lambda i, j: (i, jnp.minimum(j, 1), 0, 0)
    row = lambda width: pl.BlockSpec((1, width), lambda i, j: (0, 0))
    return pl.pallas_call(
        _ab_out_kernel,
        grid=(bsz, t // TT),
        in_specs=[
            pl.BlockSpec((1, 1, TT, GLA_V), lambda i, j: (0, i, j, 0)),
            pl.BlockSpec((1, 1, TT, GLA_V), lambda i, j: (1, i, j, 0)),
            pl.BlockSpec((1, 1, TT, SSD_INNER), lambda i, j: (0, i, j, 0)),
            pl.BlockSpec((1, 1, TT, SSD_INNER), lambda i, j: (1, i, j, 0)),
            pl.BlockSpec((1, TT, GLA_V), lambda i, j: (i, j, AB_R // GLA_V)),
            pl.BlockSpec((1, TT, SSD_INNER), lambda i, j: (i, j, AB_Z // SSD_INNER)),
            pl.BlockSpec((1, TT, SSD_INNER), lambda i, j: (i, j, 0)),
            row(GLA_V), row(SSD_INNER), row(SSD_INNER),
            pl.BlockSpec((GLA_V + SSD_INNER, d), lambda i, j: (0, 0)),
            row(d),
            pl.BlockSpec((1, TT, d), lambda i, j: (i, j, 0)),
            pl.BlockSpec((1, 1, 1, d), seg),
        ],
        out_specs=pl.BlockSpec((1, TT, d), lambda i, j: (i, j, 0)),
        out_shape=jax.ShapeDtypeStruct((bsz, t, d), F32),
        compiler_params=pltpu.CompilerParams(
            dimension_semantics=("parallel", "parallel"), vmem_limit_bytes=VMEM_LIMIT),
        name="ab_out",
    )(o_gla, o_gla, y_ssd, y_ssd, proj, proj, xbc, gla_g.reshape(1, -1), d_vec.reshape(1, -1),
      ssd_g.reshape(1, -1), w, b.reshape(1, d), x, gate)


def _cd_prep_kernel(q_ref, k_ref, v_ref, ql_ref, kl_ref, vl_ref, qr_ref, kr_ref, vr_ref, w_ref, o_ref):
    j = pl.program_id(1)
    nt = pl.num_programs(1)
    left_ok = j >= 2
    right_ok = jnp.logical_and(j >= 1, j < nt - 1)
    w = w_ref[...]
    srcs = ((q_ref, ql_ref, qr_ref, GDN_DK ** -0.5), (k_ref, kl_ref, kr_ref, 1.0), (v_ref, vl_ref, vr_ref, None))
    for s, (c_ref, l_ref, r_ref, scale) in enumerate(srcs):
        c0 = s * GDN_QK
        y = _conv3_piece(c_ref[0], l_ref[0], r_ref[0], w[:, c0:c0 + GDN_QK], 0.0, left_ok, right_ok)
        if scale is None:
            o_ref[0, :, c0:c0 + GDN_QK] = y
            continue
        for h in range(GDN_HEADS):
            seg = y[:, h * GDN_DK:(h + 1) * GDN_DK]
            inv = lax.rsqrt(jnp.sum(seg * seg, axis=-1, keepdims=True) + 1e-6) * scale
            o_ref[0, :, c0 + h * GDN_DK:c0 + (h + 1) * GDN_DK] = seg * inv


def cd_prep(proj, conv_w):
    bsz, t, _ = proj.shape
    nt = t // TT
    rb = TT // 8
    nrb = t // 8
    width = GDN_QK
    cols = (2, 3, 4)

    cur = lambda cb: pl.BlockSpec((1, TT, width), lambda i, j: (i, j, cb))
    left = lambda cb: pl.BlockSpec((1, 8, width), lambda i, j: (i, jnp.maximum(j * rb - 1, 0), cb))
    right = lambda cb: pl.BlockSpec((1, 8, width), lambda i, j: (i, jnp.minimum((j + 1) * rb, nrb - 1), cb))
    return pl.pallas_call(
        _cd_prep_kernel,
        grid=(bsz, nt),
        in_specs=[cur(cb) for cb in cols] + [left(cb) for cb in cols] + [right(cb) for cb in cols] + [
            pl.BlockSpec((3, 3 * width), lambda i, j: (0, 0))],
        out_specs=pl.BlockSpec((1, TT, 3 * width), lambda i, j: (i, j, 0)),
        out_shape=jax.ShapeDtypeStruct((bsz, t, 3 * width), F32),
        compiler_params=pltpu.CompilerParams(dimension_semantics=("parallel", "parallel")),
        name="cd_prep",
    )(*([proj] * 9), conv_w)


GDN_HB = 8


def _mm2(a, b):
    return _dot(a.astype(BF16), b.astype(BF16))


def _unit_tri_inverse(mats, b16, b32, eye):
    each = lambda f, *ls: [f(*xs) for xs in zip(*ls)]
    d16 = each(lambda a: jnp.where(b16, a, 0.0), mats)
    d2 = each(lambda x: _mm2(x, x), d16)
    d4 = each(lambda x: _mm2(x, x), d2)
    d8 = each(lambda x: _mm2(x, x), d4)
    t = each(lambda x: eye - x, d16)
    for p in (d2, d4, d8):
        t = each(lambda x, y: x + _mm2(x, y), t, p)
    off32 = jnp.logical_and(b32, jnp.logical_not(b16))
    for sel in (off32, jnp.logical_not(b32)):
        a_off = each(lambda a: jnp.where(sel, a, 0.0), mats)
        inner = each(_mm2, a_off, t)
        t = each(lambda x, y: x - _mm2(x, y), t, inner)
    return t


def _gdn_kernel(q_ref, k_ref, v_ref, sm_ref, dtb_ref, nega_ref, o_ref,
                u_s, w_s, kd_s, qg_s, aqk_s, dec_s, st_s):
    d = pl.program_id(2)
    j = pl.program_id(3)
    is_fwd = d == 0

    @pl.when(j == 0)
    def _():
        st_s[...] = jnp.zeros_like(st_s)

    sm = sm_ref[0]
    m_cum, m_all = _chunk_masks(is_fwd, TT)
    first = d * GDN_HEADS + pl.program_id(1) * GDN_HB
    rot = jnp.where(first == 0, 0, LANE - first)
    la = pltpu.roll(_softplus(sm + dtb_ref[...]) * nega_ref[...], rot, 1)
    be_sm = pltpu.roll(jax.nn.sigmoid(sm), rot, 1)
    gc_sm = _sel_dot(m_cum, la)
    gl_sm = _sel_dot(m_all, la)
    lane_bcast = lambda a, col: jnp.broadcast_to(a[:, col:col + 1], (TT, GDN_DK))

    r = lax.broadcasted_iota(jnp.int32, (CHUNK, CHUNK), 0)
    c = lax.broadcasted_iota(jnp.int32, (CHUNK, CHUNK), 1)
    causal = _causal_mask(is_fwd, CHUNK)
    strict = jnp.logical_and(causal, r != c)
    b16 = lax.shift_right_logical(r, 4) == lax.shift_right_logical(c, 4)
    b32 = lax.shift_right_logical(r, 5) == lax.shift_right_logical(c, 5)
    eye = jnp.where(r == c, 1.0, 0.0)
    chunk_rows = [slice(ci * CHUNK, (ci + 1) * CHUNK) for ci in range(CPT)]

    amats, rhss = [], []
    for hh in range(GDN_HB):
        cols = slice(hh * GDN_DK, (hh + 1) * GDN_DK)
        gc = lane_bcast(gc_sm, hh)
        gl = lane_bcast(gl_sm, hh)
        beta_e = lane_bcast(be_sm, 2 * GDN_HEADS + hh)
        q = q_ref[0, :, cols]
        k = k_ref[0, :, cols]
        egc = jnp.exp(gc)
        kb = k * beta_e
        qg_s[:, cols] = (q * egc).astype(BF16)
        kd_s[:, cols] = k * jnp.exp(gl - gc)
        dec_s[:, cols] = jnp.exp(gl)
        rhs = jnp.concatenate([v_ref[0, :, cols] * beta_e, kb * egc], axis=1)
        for ci, rows in enumerate(chunk_rows):
            gcc = gc[rows, :]
            dmat = jnp.exp(jnp.where(causal, gcc[:, 0:CHUNK] - gcc.T[0:CHUNK, :], -1e30))
            kc = k[rows].astype(BF16)
            amats.append(jnp.where(strict, _dot_nt(kb[rows].astype(BF16), kc) * dmat, 0.0))
            rhss.append(rhs[rows])
            aqk_s[rows, hh * CHUNK:(hh + 1) * CHUNK] = (_dot_nt(q[rows].astype(BF16), kc) * dmat).astype(BF16)
    tinv = _unit_tri_inverse(amats, b16, b32, eye)
    for n, (t, rhs_c) in enumerate(zip(tinv, rhss)):
        hh, rows = n // CPT, chunk_rows[n % CPT]
        cols = slice(hh * GDN_DK, (hh + 1) * GDN_DK)
        sol = _mm2(t, rhs_c)
        u_s[rows, cols] = sol[:, 0:GDN_DV]
        w_s[rows, cols] = sol[:, GDN_DV:].astype(BF16)

    for ci in range(CPT):
        off = pl.multiple_of(jnp.where(is_fwd, ci, CPT - 1 - ci) * CHUNK, CHUNK)
        rows = pl.ds(off, CHUNK)
        for hh in range(GDN_HB):
            cols = slice(hh * GDN_DK, (hh + 1) * GDN_DK)
            st = st_s[hh]
            stb = st.astype(BF16)
            vn = u_s[rows, cols] - _dot(w_s[rows, cols], stb)
            vnb = vn.astype(BF16)
            o_ref[0, 0, rows, cols] = (_dot(qg_s[rows, cols], stb)
                                       + _dot(aqk_s[rows, hh * CHUNK:(hh + 1) * CHUNK], vnb))
            st_s[hh] = st * dec_s[pl.ds(off, 1), cols] + _dot(kd_s[rows, cols].T.astype(BF16), vnb)


def gdn_scan(qkv, proj, dtb, nega):
    bsz, t, _ = qkv.shape
    nt = t // TT
    tile = lambda d, j: _scan_tile(d, j, nt)
    ng = GDN_HEADS // GDN_HB
    wb = GDN_HB * GDN_DK
    return pl.pallas_call(
        _gdn_kernel,
        grid=(bsz, ng, 2, nt),
        in_specs=[
            pl.BlockSpec((1, TT, wb), lambda i, h, d, j: (i, tile(d, j), h)),
            pl.BlockSpec((1, TT, wb), lambda i, h, d, j: (i, tile(d, j), ng + h)),
            pl.BlockSpec((1, TT, wb), lambda i, h, d, j: (i, tile(d, j), 2 * ng + h)),
            pl.BlockSpec((1, TT, LANE), lambda i, h, d, j: (i, tile(d, j), 6 * D_MODEL // LANE)),
            pl.BlockSpec((1, LANE), lambda i, h, d, j: (0, 0)),
            pl.BlockSpec((1, LANE), lambda i, h, d, j: (0, 0)),
        ],
        out_specs=pl.BlockSpec((1, 1, TT, wb), lambda i, h, d, j: (d, i, tile(d, j), h)),
        out_shape=jax.ShapeDtypeStruct((2, bsz, t, GDN_V), F32),
        scratch_shapes=[pltpu.VMEM((TT, wb), F32), pltpu.VMEM((TT, wb), BF16),
                        pltpu.VMEM((TT, wb), F32), pltpu.VMEM((TT, wb), BF16),
                        pltpu.VMEM((TT, GDN_HB * CHUNK), BF16), pltpu.VMEM((TT, wb), F32),
                        pltpu.VMEM((GDN_HB, GDN_DK, GDN_DV), F32)],
        compiler_params=pltpu.CompilerParams(
            dimension_semantics=("parallel", "parallel", "parallel", "arbitrary")),
        name="gdn_scan",
    )(qkv, qkv, qkv, proj, dtb, nega)


CONF_PAD = 16


def _cd_out_kernel(ga_ref, gb_ref, og_ref, of_ref, ob_ref, cw_ref, cb_ref, lg_ref, lb_ref, ng_ref,
                   w_ref, b_ref, x_ref, gate_ref, o_ref, pad_s, conv_s):
    j = pl.program_id(1)
    half = (CONF_KERNEL - 1) // 2
    glu = ga_ref[0] * jax.nn.sigmoid(gb_ref[0])
    zeros = jnp.zeros((CONF_PAD, CONF_CH), F32)

    def conv_segments(seglen):
        stride = seglen + 2 * CONF_PAD
        for g in range(TT // seglen):
            base = g * stride
            pad_s[base:base + CONF_PAD, :] = zeros
            pad_s[base + CONF_PAD:base + CONF_PAD + seglen, :] = glu[g * seglen:(g + 1) * seglen]
            pad_s[base + CONF_PAD + seglen:base + stride, :] = zeros
        for g in range(TT // seglen):
            lo = g * stride + CONF_PAD - half
            acc = jnp.zeros((seglen, CONF_CH), F32) + cb_ref[...]
            for kk in range(CONF_KERNEL):
                acc = acc + pad_s[lo + kk:lo + kk + seglen, :] * cw_ref[kk:kk + 1, :]
            conv_s[g * seglen:(g + 1) * seglen, :] = acc

    @pl.when(j == 0)
    def _():
        conv_segments(TT)

    @pl.when(j > 0)
    def _():
        conv_segments(GRID_W)

    acc = conv_s[...]
    mu = jnp.mean(acc, axis=-1, keepdims=True)
    cen = acc - mu
    var = jnp.mean(cen * cen, axis=-1, keepdims=True)
    conv = _silu(cen * lax.rsqrt(var + 1e-5) * lg_ref[...] + lb_ref[...])
    o = _group_rmsnorm(of_ref[0, 0] + ob_ref[0, 0], GDN_DV) * ng_ref[...] * _silu(og_ref[0])
    m = _dot(conv.astype(BF16), w_ref[0:CONF_CH, :]) + _dot(o.astype(BF16), w_ref[CONF_CH:, :]) + b_ref[...]
    o_ref[0] = x_ref[0] + gate_ref[0, 0] * m


def cd_out(o_gdn, proj, conv_w, conv_b, ln_g, ln_b, norm_g, w, b, x, gate):
    bsz, t, d = x.shape
    seg = lambda i, j: (i, jnp.minimum(j, 1), 0, 0)
    row = lambda width: pl.BlockSpec((1, width), lambda i, j: (0, 0))
    return pl.pallas_call(
        _cd_out_kernel,
        grid=(bsz, t // TT),
        in_specs=[
            pl.BlockSpec((1, TT, CONF_CH), lambda i, j: (i, j, 0)),
            pl.BlockSpec((1, TT, CONF_CH), lambda i, j: (i, j, 1)),
            pl.BlockSpec((1, TT, GDN_V), lambda i, j: (i, j, 5)),
            pl.BlockSpec((1, 1, TT, GDN_V), lambda i, j: (0, i, j, 0)),
            pl.BlockSpec((1, 1, TT, GDN_V), lambda i, j: (1, i, j, 0)),
            pl.BlockSpec((CONF_KERNEL, CONF_CH), lambda i, j: (0, 0)),
            row(CONF_CH), row(CONF_CH), row(CONF_CH), row(GDN_V),
            pl.BlockSpec((CONF_CH + GDN_V, d), lambda i, j: (0, 0)),
            row(d),
            pl.BlockSpec((1, TT, d), lambda i, j: (i, j, 0)),
            pl.BlockSpec((1, 1, 1, d), seg),
        ],
        out_specs=pl.BlockSpec((1, TT, d), lambda i, j: (i, j, 0)),
        out_shape=jax.ShapeDtypeStruct((bsz, t, d), F32),
        scratch_shapes=[pltpu.VMEM(((TT // GRID_W) * (GRID_W + 2 * CONF_PAD), CONF_CH), F32),
                        pltpu.VMEM((TT, CONF_CH), F32)],
        compiler_params=pltpu.CompilerParams(
            dimension_semantics=("parallel", "parallel"), vmem_limit_bytes=VMEM_LIMIT),
        name="cd_out",
    )(proj, proj, proj, o_gdn, o_gdn, conv_w, conv_b.reshape(1, -1), ln_g.reshape(1, -1), ln_b.reshape(1, -1),
      norm_g.reshape(1, -1), w, b.reshape(1, d), x, gate)


def _out_proj_kernel(m_ref, w_ref, b_ref, x_ref, gate_ref, o_ref):
    y = _dot(m_ref[0].astype(BF16), w_ref[...]) + b_ref[...]
    o_ref[0] = x_ref[0] + gate_ref[0] * y


def out_proj_residual(mixed, w, b, x, gate):
    bsz, t, k = mixed.shape
    d = w.shape[1]
    tm = min(t, 512)
    return pl.pallas_call(
        _out_proj_kernel,
        grid=(bsz, t // tm),
        in_specs=[
            pl.BlockSpec((1, tm, k), lambda i, j: (i, j, 0)),
            pl.BlockSpec((k, d), lambda i, j: (0, 0)),
            pl.BlockSpec((1, d), lambda i, j: (0, 0)),
            pl.BlockSpec((1, tm, d), lambda i, j: (i, j, 0)),
            pl.BlockSpec((1, 1, d), lambda i, j: (i, 0, 0)),
        ],
        out_specs=pl.BlockSpec((1, tm, d), lambda i, j: (i, j, 0)),
        out_shape=jax.ShapeDtypeStruct((bsz, t, d), F32),
        compiler_params=pltpu.CompilerParams(
            dimension_semantics=("parallel", "parallel"), vmem_limit_bytes=VMEM_LIMIT),
        name="out_proj",
    )(mixed, w, b.reshape(1, d), x, gate.reshape(bsz, 1, d))


def _router_kernel(x_ref, g_ref, sh_ref, sc_ref, rw_ref, h_ref, aff_ref):
    x = x_ref[0]
    ms = jnp.mean(x * x, axis=-1, keepdims=True)
    h = (x * lax.rsqrt(ms + 1e-6) * g_ref[...] * (1.0 + sc_ref[0, 0]) + sh_ref[0, 0]).astype(BF16)
    h_ref[0] = h
    logits = _dot(h, rw_ref[...])
    lane = lax.broadcasted_iota(jnp.int32, logits.shape, 1)
    logits = jnp.where(lane < N_EXPERTS, logits, -1e30)
    e = jnp.exp(logits - jnp.max(logits, axis=-1, keepdims=True))
    aff = e / jnp.sum(e, axis=-1, keepdims=True)
    aff_ref[0] = aff.T[0:N_EXPERTS, :]


def moe_router(xa, g, shift, scale, rw, seg, tile0, ntiles):
    bsz, _, d = xa.shape
    n = ntiles * TT
    return pl.pallas_call(
        _router_kernel,
        grid=(bsz, ntiles),
        in_specs=[
            pl.BlockSpec((1, TT, d), lambda i, j: (i, j + tile0, 0)),
            pl.BlockSpec((1, d), lambda i, j: (0, 0)),
            pl.BlockSpec((1, 1, 1, d), lambda i, j: (i, seg, 0, 0)),
            pl.BlockSpec((1, 1, 1, d), lambda i, j: (i, seg, 0, 0)),
            pl.BlockSpec((d, LANE), lambda i, j: (0, 0)),
        ],
        out_specs=[pl.BlockSpec((1, TT, d), lambda i, j: (i, j, 0)),
                   pl.BlockSpec((1, N_EXPERTS, TT), lambda i, j: (i, 0, j))],
        out_shape=[jax.ShapeDtypeStruct((bsz, n, d), BF16), jax.ShapeDtypeStruct((bsz, N_EXPERTS, n), F32)],
        compiler_params=pltpu.CompilerParams(dimension_semantics=("parallel", "parallel")),
        name="moe_router",
    )(xa, g.reshape(1, d), shift, scale, rw)


def _lane_block_prefix(x, u_strict):
    nblk = x.shape[1] // LANE
    run = jnp.zeros((x.shape[0], 1), F32)
    outs = []
    for cblk in range(nblk):
        xc = x[:, cblk * LANE:(cblk + 1) * LANE]
        outs.append(_dot(xc.astype(BF16), u_strict) + run)
        run = run + jnp.sum(xc, axis=-1, keepdims=True)
    return jnp.concatenate(outs, axis=1), run


def _select_kernel(aff_ref, slot_ref, *, cap):
    aff = aff_ref[0]
    bits = pltpu.bitcast(aff, jnp.int32)
    capf = jnp.float32(cap)

    def step(i, thr):
        cand = jnp.bitwise_or(thr, lax.shift_left(jnp.int32(1), 30 - i))
        cnt = jnp.sum(jnp.where(bits >= cand, 1.0, 0.0), axis=-1, keepdims=True)
        return jnp.where(cnt >= capf, cand, thr)

    thr = lax.fori_loop(0, 31, step, jnp.zeros((aff.shape[0], 1), jnp.int32))
    gt = jnp.where(bits > thr, 1.0, 0.0)
    eq = jnp.where(bits == thr, 1.0, 0.0)
    r = lax.broadcasted_iota(jnp.int32, (LANE, LANE), 0)
    c = lax.broadcasted_iota(jnp.int32, (LANE, LANE), 1)
    u_strict = jnp.where(r < c, 1.0, 0.0).astype(BF16)
    need = capf - jnp.sum(gt, axis=-1, keepdims=True)
    eq_rank, _ = _lane_block_prefix(eq, u_strict)
    sel = jnp.maximum(gt, jnp.where(eq_rank < need, eq, 0.0))
    slot, _ = _lane_block_prefix(sel, u_strict)
    slot_ref[0] = jnp.where(sel > 0.0, slot.astype(jnp.int32), -1)


def moe_select(aff, cap):
    bsz, ne, n = aff.shape
    return pl.pallas_call(
        functools.partial(_select_kernel, cap=cap),
        grid=(bsz,),
        in_specs=[pl.BlockSpec((1, ne, n), lambda i: (i, 0, 0))],
        out_specs=pl.BlockSpec((1, ne, n), lambda i: (i, 0, 0)),
        out_shape=jax.ShapeDtypeStruct((bsz, ne, n), jnp.int32),
        compiler_params=pltpu.CompilerParams(dimension_semantics=("parallel",)),
        name="moe_select",
    )(aff)


def _slot_index_kernel(slot_ref, idx_ref, *, cap):
    slot = slot_ref[0]
    n = slot.shape[1]
    srow = lax.broadcasted_iota(jnp.int32, (cap, LANE), 0)
    lane = lax.broadcasted_iota(jnp.int32, (cap, LANE), 1)
    acc = jnp.zeros((cap, LANE), jnp.int32)
    for cblk in range(n // LANE):
        s_c = slot[:, cblk * LANE:(cblk + 1) * LANE]
        acc = acc + jnp.where(srow == s_c, lane + (cblk * LANE + 1), 0)
    ones = jnp.ones((8, LANE), BF16)
    hi = _dot_nt(ones, lax.shift_right_logical(acc, 7).astype(F32).astype(BF16))
    lo = _dot_nt(ones, jnp.bitwise_and(acc, LANE - 1).astype(F32).astype(BF16))
    idx_ref[0] = (hi[0:1, :] * float(LANE) + lo[0:1, :]).astype(jnp.int32) - 1


def moe_slot_index(slot, cap):
    bsz, ne, n = slot.shape
    idx = pl.pallas_call(
        functools.partial(_slot_index_kernel, cap=cap),
        grid=(bsz * ne,),
        in_specs=[pl.BlockSpec((1, 1, n), lambda i: (i, 0, 0))],
        out_specs=pl.BlockSpec((1, 1, cap), lambda i: (i, 0, 0)),
        out_shape=jax.ShapeDtypeStruct((bsz * ne, 1, cap), jnp.int32),
        compiler_params=pltpu.CompilerParams(dimension_semantics=("parallel",)),
        name="moe_slot_index",
    )(slot.reshape(bsz * ne, 1, n))
    return idx.reshape(bsz, ne, cap)


WIN_ALIGN = 16
WIN_FAST = 128


def _combine_kernel(ws_ref, slot_ref, aff_ref, *rest, win):
    ye_refs, (x_ref, gate_ref, o_ref) = rest[:N_EXPERTS], rest[N_EXPERTS:]
    b = pl.program_id(0)
    j = pl.program_id(1)
    srow = lax.broadcasted_iota(jnp.int32, (win, TT), 0)
    his, los = [], []
    for e in range(N_EXPERTS):
        sel = jnp.where(srow + ws_ref[b, e, j] == slot_ref[0, e:e + 1, :], aff_ref[0, e:e + 1, :], 0.0).T
        hi = sel.astype(BF16)
        his.append(hi)
        los.append((sel - hi.astype(F32)).astype(BF16))
    ye = jnp.concatenate([r[...] for r in ye_refs], axis=0)
    acc = _dot(jnp.concatenate(his, axis=1), ye) + _dot(jnp.concatenate(los, axis=1), ye)
    o_ref[0] = x_ref[0] + gate_ref[0, 0] * acc


def moe_combine(ws, slot, aff, ye, xa, gate, seg, tile0, win):
    bsz, ne, n = slot.shape
    nt = n // TT
    d = xa.shape[2]

    def ye_spec(e):
        return pl.BlockSpec((pl.Squeezed(), pl.Squeezed(), pl.Element(win), pl.Element(d)),
                            lambda i, j, ws_ref: (i, e, pl.multiple_of(ws_ref[i, e, j], WIN_ALIGN), 0))

    return pl.pallas_call(
        functools.partial(_combine_kernel, win=win),
        grid_spec=pltpu.PrefetchScalarGridSpec(
            num_scalar_prefetch=1,
            grid=(bsz, nt),
            in_specs=[pl.BlockSpec((1, ne, TT), lambda i, j, ws_ref: (i, 0, j)),
                      pl.BlockSpec((1, ne, TT), lambda i, j, ws_ref: (i, 0, j))]
            + [ye_spec(e) for e in range(ne)]
            + [pl.BlockSpec((1, TT, d), lambda i, j, ws_ref: (i, j + tile0, 0)),
               pl.BlockSpec((1, 1, 1, d), lambda i, j, ws_ref: (i, seg, 0, 0))],
            out_specs=pl.BlockSpec((1, TT, d), lambda i, j, ws_ref: (i, j + tile0, 0)),
        ),
        out_shape=jax.ShapeDtypeStruct(xa.shape, F32),
        input_output_aliases={3 + ne: 0},
        compiler_params=pltpu.CompilerParams(
            dimension_semantics=("parallel", "parallel"), vmem_limit_bytes=VMEM_LIMIT),
        name="moe_combine",
    )(ws, slot, aff, *([ye] * ne), xa, gate)


def _expert_ffn_kernel(x_ref, w1_ref, w3_ref, w2_ref, o_ref):
    x = x_ref[0, 0].astype(BF16)
    a = _dot(x, w1_ref[0])
    g = _dot(x, w3_ref[0])
    o_ref[0, 0] = _dot((_silu(a) * g).astype(BF16), w2_ref[0]).astype(BF16)


def expert_ffn(xe, w1, w3, w2):
    bsz, ne, cap, d = xe.shape
    f = w1.shape[2]
    tm = min(cap, 512)
    return pl.pallas_call(
        _expert_ffn_kernel,
        grid=(ne, bsz, cap // tm),
        in_specs=[
            pl.BlockSpec((1, 1, tm, d), lambda e, i, j: (i, e, j, 0)),
            pl.BlockSpec((1, d, f), lambda e, i, j: (e, 0, 0)),
            pl.BlockSpec((1, d, f), lambda e, i, j: (e, 0, 0)),
            pl.BlockSpec((1, f, d), lambda e, i, j: (e, 0, 0)),
        ],
        out_specs=pl.BlockSpec((1, 1, tm, d), lambda e, i, j: (i, e, j, 0)),
        out_shape=jax.ShapeDtypeStruct((bsz, ne, cap, d), BF16),
        compiler_params=pltpu.CompilerParams(
            dimension_semantics=("parallel", "parallel", "parallel"), vmem_limit_bytes=VMEM_LIMIT),
        name="expert_ffn",
    )(xe, w1, w3, w2)


def _rmsnorm_kernel(x_ref, g_ref, o_ref):
    x = x_ref[0]
    ms = jnp.mean(x * x, axis=-1, keepdims=True)
    o_ref[0] = x * lax.rsqrt(ms + 1e-6) * g_ref[...]


def rmsnorm_rows(x, g, tile0):
    bsz, t, d = x.shape
    nt = t // TT - tile0
    return pl.pallas_call(
        _rmsnorm_kernel,
        grid=(bsz, nt),
        in_specs=[pl.BlockSpec((1, TT, d), lambda i, j: (i, j + tile0, 0)), pl.BlockSpec((1, d), lambda i, j: (0, 0))],
        out_specs=pl.BlockSpec((1, TT, d), lambda i, j: (i, j, 0)),
        out_shape=jax.ShapeDtypeStruct((bsz, nt * TT, d), F32),
        compiler_params=pltpu.CompilerParams(dimension_semantics=("parallel", "parallel")),
        name="final_rmsnorm",
    )(x, g.reshape(1, d))


def _ab_in_layout(w_in, b_in):
    q, k, v, r, glr, z, xs, bm, cm, dt = _split_cols(
        jnp.concatenate([w_in, b_in[None]], axis=0),
        (GLA_QK, GLA_QK, GLA_V, GLA_V, 2 * GLA_GATE_RANK, SSD_INNER, SSD_INNER, SSD_BC, SSD_BC, 2 * SSD_HEADS))
    wb = _pad_cols(jnp.concatenate([q, k, v, r, z, xs, bm, cm, glr, dt], axis=1), AB_N)
    return wb[:-1].astype(BF16), wb[-1]


def _gla_gate_params(w_gate2, b_gate2):
    wg = jnp.zeros((2, LANE, GLA_QK), F32)
    for d in range(2):
        wg = wg.at[d, d * GLA_GATE_RANK:(d + 1) * GLA_GATE_RANK, :].set(w_gate2[d])
    return wg.astype(BF16), b_gate2.reshape(2, 1, GLA_QK)


def _ssd_params(dt_bias, a_log):
    dtb = jnp.zeros((2, 1, LANE), F32)
    nega = jnp.zeros((2, 1, LANE), F32)
    e = np.zeros((2, LANE, SSD_INNER), np.float32)
    for d in range(2):
        c0 = SM_DT + d * SSD_HEADS
        dtb = dtb.at[d, 0, c0:c0 + SSD_HEADS].set(dt_bias[d])
        nega = nega.at[d, 0, c0:c0 + SSD_HEADS].set(-jnp.exp(a_log[d]))
        for h in range(SSD_HEADS):
            e[d, c0 + h, h * SSD_HEADDIM:(h + 1) * SSD_HEADDIM] = 1.0
    return dtb, nega, jnp.asarray(e, BF16)


def _gdn_params(dt_bias, a_log):
    n = 2 * GDN_HEADS
    dtb = jnp.zeros((1, LANE), F32).at[0, 0:n].set(dt_bias.reshape(n))
    nega = jnp.zeros((1, LANE), F32).at[0, 0:n].set(-jnp.exp(a_log.reshape(n)))
    return dtb, nega


def _rmsnorm(x, g, eps=1e-6):
    return x * lax.rsqrt(jnp.mean(jnp.square(x), axis=-1, keepdims=True) + eps) * g


def _head_rmsnorm(x, g):
    return _rmsnorm(x, g.reshape(x.shape[-2:]))


def _layernorm(x, g, b, eps=1e-5):
    mu = jnp.mean(x, axis=-1, keepdims=True)
    var = jnp.mean(jnp.square(x - mu), axis=-1, keepdims=True)
    return (x - mu) * lax.rsqrt(var + eps) * g + b


def _l2norm(x, eps=1e-6):
    return x * lax.rsqrt(jnp.sum(jnp.square(x), axis=-1, keepdims=True) + eps)


def _dwconv(x, w, b=None):
    k, ch = w.shape
    pad = (k - 1) // 2
    y = lax.conv_general_dilated(x, w[:, None, :], (1,), [(pad, pad)],
                                 dimension_numbers=('NWC', 'WIO', 'NWC'), feature_group_count=ch)
    return y if b is None else y + b


def _gdn_scan(q, k, v, beta, logg, s0, with_out):
    bsz, nh, t, dk = q.shape
    dv = v.shape[-1]
    nc = t // CHUNK
    chunks = lambda z: z.reshape(bsz, nh, nc, CHUNK, *z.shape[3:])
    q, k, v, beta, logg = (chunks(z) for z in (q, k, v, beta, logg))
    gc = jnp.cumsum(logg, axis=-1)
    glast = gc[..., -1]
    tril = jnp.tril(jnp.ones((CHUNK, CHUNK), dtype=bool))
    strict = jnp.tril(jnp.ones((CHUNK, CHUNK), dtype=bool), k=-1)
    decay = jnp.exp(jnp.where(tril, gc[..., :, None] - gc[..., None, :], -jnp.inf))
    kb = k * beta[..., None]
    m = jnp.eye(CHUNK, dtype=k.dtype) + jnp.where(strict, jnp.einsum('bhcid,bhcjd->bhcij', kb, k) * decay, 0.0)
    rhs = jnp.concatenate([v * beta[..., None], kb * jnp.exp(gc)[..., None]], axis=-1)
    sol = lax.linalg.triangular_solve(m, rhs, left_side=True, lower=True, unit_diagonal=True)
    u, w = sol[..., :dv], sol[..., dv:]
    kd = k * jnp.exp(glast[..., None] - gc)[..., None]
    dec = jnp.exp(glast)
    front = lambda z: jnp.moveaxis(z, 2, 0)

    def advance(s, w_c, u_c, kd_c, dec_c):
        vn = u_c - jnp.einsum('bhid,bhde->bhie', w_c, s)
        return vn, dec_c[..., None, None] * s + jnp.einsum('bhjd,bhje->bhde', kd_c, vn)

    xs = (front(w), front(u), front(kd), front(dec))
    if not with_out:
        s_fin, _ = lax.scan(lambda s, xc: (advance(s, *xc)[1], None), s0, xs)
        return None, s_fin
    aqk = jnp.einsum('bhcid,bhcjd->bhcij', q, k) * decay
    qg = q * jnp.exp(gc)[..., None]

    def step(s, xc):
        w_c, u_c, kd_c, dec_c, aqk_c, qg_c = xc
        vn, s_new = advance(s, w_c, u_c, kd_c, dec_c)
        o = jnp.einsum('bhid,bhde->bhie', qg_c, s) + jnp.einsum('bhij,bhje->bhie', aqk_c, vn)
        return s_new, o

    s_fin, o = lax.scan(step, s0, xs + (front(aqk), front(qg)))
    return jnp.moveaxis(o, 0, 2).reshape(bsz, nh, t, dv), s_fin


def _bidir_scan(scan_fn, args_f, args_b, t_axis, init_f, init_b, with_out):
    flip = lambda a: jnp.flip(a, axis=t_axis)
    o_f, s_f = scan_fn(*args_f, init_f, with_out)
    o_b, s_b = scan_fn(*[flip(a) for a in args_b], init_b, with_out)
    o = o_f + flip(o_b) if with_out else None
    return o, s_f, s_b


def _conformer_conv(u, rows, dw_w, dw_b, ln_g, ln_b):
    bsz, t, ch = u.shape
    y = _dwconv(u.reshape(bsz * rows, t // rows, ch), dw_w, dw_b).reshape(bsz, t, ch)
    return jax.nn.silu(_layernorm(y, ln_g, ln_b))


def _cd_stream(proj, gdn_conv_w, gdn_a_log, gdn_dt_bias):
    bsz, t, _ = proj.shape
    ga, gb, q, k, v, og, a_raw, b_raw = _split_cols(proj, CD_SPLITS)
    glu = ga * jax.nn.sigmoid(gb)
    qkv = jax.nn.silu(_dwconv(jnp.concatenate([q, k, v], axis=-1), gdn_conv_w))
    q, k, v = _split_cols(qkv, (GDN_QK, GDN_QK, GDN_V))
    heads = lambda a: a.reshape(bsz, t, GDN_HEADS, -1).transpose(0, 2, 1, 3)
    q = _l2norm(heads(q)) * GDN_DK ** -0.5
    k = _l2norm(heads(k))
    v = heads(v)
    beta = jax.nn.sigmoid(b_raw.reshape(bsz, t, 2, GDN_HEADS)).transpose(2, 0, 3, 1)
    logg = (-jnp.exp(gdn_a_log)
            * jax.nn.softplus(a_raw.reshape(bsz, t, 2, GDN_HEADS) + gdn_dt_bias)).transpose(2, 0, 3, 1)
    return (q, k, v, beta[0], logg[0]), (q, k, v, beta[1], logg[1]), glu, og


def _mixer_cd(proj_l, proj_c, rows, conf_dw_w, conf_dw_b, conf_ln_g, conf_ln_b,
              gdn_conv_w, gdn_a_log, gdn_dt_bias, gdn_norm_g, need_ctx):
    c_f, c_b, c_glu, c_og = _cd_stream(proj_c, gdn_conv_w, gdn_a_log, gdn_dt_bias)
    l_f, l_b, l_glu, l_og = _cd_stream(proj_l, gdn_conv_w, gdn_a_log, gdn_dt_bias)
    bsz = proj_l.shape[0]
    z0 = jnp.zeros((bsz, GDN_HEADS, GDN_DK, GDN_DV), F32)
    od_c, s_f, s_b = _bidir_scan(_gdn_scan, c_f, c_b, 2, z0, z0, need_ctx)
    od_l, _, _ = _bidir_scan(_gdn_scan, l_f, l_b, 2, s_f, s_b, True)

    def mix(od, glu, og, n_rows):
        t = glu.shape[1]
        conv = _conformer_conv(glu, n_rows, conf_dw_w, conf_dw_b, conf_ln_g, conf_ln_b)
        o = _head_rmsnorm(jnp.swapaxes(od, 1, 2), gdn_norm_g) * jax.nn.silu(og).reshape(bsz, t, GDN_HEADS, GDN_DV)
        return jnp.concatenate([conv, o.reshape(bsz, t, GDN_V)], axis=-1)

    return mix(od_l, l_glu, l_og, rows), (mix(od_c, c_glu, c_og, 1) if need_ctx else None)


def moe_segment(i, xa, mods, p, w1, w3, w2, seg, tile0, ntiles):
    bsz, _, d = xa.shape
    n = ntiles * TT
    cap = n * EC_CAPACITY // N_EXPERTS
    rw = _pad_cols(p["moe_router"][i], LANE).astype(BF16)
    h, aff = moe_router(xa, p["norm2_g"][i], mods[:, :, 3], mods[:, :, 4], rw, seg, tile0, ntiles)
    slot = moe_select(aff, cap)
    idx = moe_slot_index(slot, cap).reshape(bsz, N_EXPERTS * cap)
    xe = jnp.take_along_axis(h, idx[..., None], axis=1).reshape(bsz, N_EXPERTS, cap, d)
    ye = expert_ffn(xe, w1, w3, w2)
    counts = jnp.sum((slot >= 0).reshape(bsz, N_EXPERTS, ntiles, TT), axis=-1, dtype=jnp.int32)
    starts = jnp.cumsum(counts, axis=-1) - counts
    aligned = starts // WIN_ALIGN * WIN_ALIGN

    def run(win):
        ws = jnp.minimum(aligned, cap - win)
        return moe_combine(ws, slot, aff, ye, xa, mods[:, :, 5], seg, tile0, win)

    win_fast, win_full = min(cap, WIN_FAST), min(cap, TT + WIN_ALIGN)
    if win_fast == win_full:
        return run(win_full)
    overflow = jnp.any(starts + counts - jnp.minimum(aligned, cap - win_fast) > win_fast)
    return lax.cond(overflow, lambda: run(win_full), lambda: run(win_fast))


def layer_mixer(i, j, xa, mods, p, last):
    sh1, sc1, g1 = (mods[:, :, s] for s in range(3))
    if i % 2 == 0:
        w_in, b_in = _ab_in_layout(p["ab_w_in"][j], p["ab_b_in"][j])
        proj = norm_proj(xa, p["norm1_g"][i], sh1, sc1, w_in, b_in, 1920)
        xbc = ab_prep(proj, p["ssd_conv_w"][j], p["ssd_conv_b"][j])
        wg, bg = _gla_gate_params(p["gla_w_gate2"][j], p["gla_b_gate2"][j])
        o_gla = gla_scan(proj, wg, bg)
        y_ssd = ssd_scan(xbc, proj, *_ssd_params(p["ssd_dt_bias"][j], p["ssd_a_log"][j]))
        return ab_out(o_gla, y_ssd, proj, xbc, p["gla_norm_g"][j], jnp.repeat(p["ssd_d"][j], SSD_HEADDIM),
                      p["ssd_norm_g"][j], p["ab_w_out"][j].astype(BF16), p["ab_b_out"][j], xa, g1)
    w_in = _pad_cols(p["cd_w_in"][j], CD_N).astype(BF16)
    b_in = _pad_cols(p["cd_b_in"][j], CD_N)
    proj = norm_proj(xa, p["norm1_g"][i], sh1, sc1, w_in, b_in, 1280)
    qkv = cd_prep(proj, p["gdn_conv_w"][j])
    o_gdn = gdn_scan(qkv, proj, *_gdn_params(p["gdn_dt_bias"][j], p["gdn_a_log"][j]))
    return cd_out(o_gdn, proj, p["conf_dw_w"][j], p["conf_dw_b"][j], p["conf_ln_g"][j], p["conf_ln_b"][j],
                  p["gdn_norm_g"][j], p["cd_w_out"][j].astype(BF16), p["cd_b_out"][j], xa, g1)


def kernel(x, c, ctx, c_ctx, mod_w, mod_b, norm1_g, norm2_g, ab_w_in, ab_b_in, ab_w_out, ab_b_out, gla_w_gate2, gla_b_gate2, gla_norm_g, ssd_conv_w, ssd_conv_b, ssd_dt_bias, ssd_a_log, ssd_d, ssd_norm_g, cd_w_in, cd_b_in, cd_w_out, cd_b_out, conf_dw_w, conf_dw_b, conf_ln_g, conf_ln_b, gdn_conv_w, gdn_a_log, gdn_dt_bias, gdn_norm_g, moe_router, moe_w1, moe_w3, moe_w2, final_norm_g):
    p = dict(locals())
    bsz, seq, d = x.shape
    assert ctx.shape[1] == TT and seq % TT == 0
    depth = mod_w.shape[0]
    xa = jnp.concatenate([ctx, x], axis=1)
    for i in range(depth):
        last = i == depth - 1
        mod_l = jax.nn.silu(c) @ mod_w[i] + mod_b[i]
        mod_c = jnp.broadcast_to(jax.nn.silu(c_ctx) @ mod_w[i] + mod_b[i], mod_l.shape)
        mods = jnp.stack([mod_c, mod_l], axis=1).reshape(bsz, 2, 6, 1, d)
        sh1, sc1, g1, sh2, sc2, g2 = (mods[:, :, s] for s in range(6))
        xa = layer_mixer(i, i // 2, xa, mods, p, last)
        w1, w3, w2 = moe_w1[i].astype(BF16), moe_w3[i].astype(BF16), moe_w2[i].astype(BF16)
        xa = moe_segment(i, xa, mods, p, w1, w3, w2, 1, 1, seq // TT)
        if not last:
            xa = moe_segment(i, xa, mods, p, w1, w3, w2, 0, 0, 1)
    return rmsnorm_rows(xa, final_norm_g, 1)
```

```python
import functools

import jax
import jax.numpy as jnp
import numpy as np
from jax import lax
from jax.experimental import pallas as pl
from jax.experimental.pallas import tpu as pltpu

F32 = jnp.float32
BF16 = jnp.bfloat16

D_MODEL = 1024
GRID_W = 64
CHUNK = 64
GLA_HEADS, GLA_DK, GLA_DV, GLA_GATE_RANK, GLA_GATE_TAU = 4, 128, 256, 16, 16.0
SSD_HEADS, SSD_HEADDIM, SSD_STATE, SSD_GROUPS = 16, 64, 128, 2
CONF_CH, CONF_KERNEL = D_MODEL, 31
GDN_HEADS, GDN_DK, GDN_DV = 8, 128, 128
N_EXPERTS, EC_CAPACITY, EXPERT_FF = 16, 2, D_MODEL

GLA_QK = GLA_HEADS * GLA_DK
GLA_V = GLA_HEADS * GLA_DV
SSD_INNER = SSD_HEADS * SSD_HEADDIM
SSD_BC = SSD_GROUPS * SSD_STATE
SSD_HPG = SSD_HEADS // SSD_GROUPS
GDN_QK = GDN_HEADS * GDN_DK
GDN_V = GDN_HEADS * GDN_DV
CD_SPLITS = (CONF_CH, CONF_CH, GDN_QK, GDN_QK, GDN_V, GDN_V, 2 * GDN_HEADS, 2 * GDN_HEADS)

LANE = 128
SUBLANES = 8
TT = 256
CPT = TT // CHUNK
VMEM_LIMIT = 48 * 1024 * 1024

AB_Q, AB_K, AB_V, AB_R, AB_Z, AB_XS, AB_BM, AB_CM, AB_SMALL = 0, 512, 1024, 2048, 3072, 4096, 5120, 5376, 5632
AB_N = AB_SMALL + LANE
SM_DT = 2 * GLA_GATE_RANK
CD_N = 6400


def _split_cols(a, sizes):
    return jnp.split(a, np.cumsum(sizes)[:-1].tolist(), axis=-1)


def _pad_cols(a, n):
    return jnp.pad(a, [(0, 0)] * (a.ndim - 1) + [(0, n - a.shape[-1])])


def _dot(a, b):
    return jnp.dot(a, b, preferred_element_type=F32)


def _dot_nt(a, b):
    return lax.dot_general(a, b, (((1,), (1,)), ((), ())), preferred_element_type=F32)


def _split3(x):
    hi = x.astype(BF16)
    r = x - hi.astype(F32)
    mid = r.astype(BF16)
    lo = (r - mid.astype(F32)).astype(BF16)
    return hi, mid, lo


def _sel_dot(m, x):
    hi, mid, lo = _split3(x)
    return _dot(m, hi) + _dot(m, mid) + _dot(m, lo)


def _dot_sel(x, e):
    hi, mid, lo = _split3(x)
    return _dot(hi, e) + _dot(mid, e) + _dot(lo, e)


def _softplus(x):
    return jnp.maximum(x, 0.0) + jnp.log(1.0 + jnp.exp(-jnp.abs(x)))


def _silu(x):
    return x * jax.nn.sigmoid(x)


def _chunk_masks(is_fwd, n):
    r = lax.broadcasted_iota(jnp.int32, (n, n), 0)
    c = lax.broadcasted_iota(jnp.int32, (n, n), 1)
    same = lax.shift_right_logical(r, 6) == lax.shift_right_logical(c, 6)
    lo = jnp.where(is_fwd, c, r)
    hi = jnp.where(is_fwd, r, c)
    cum = jnp.logical_and(same, lo <= hi)
    return jnp.where(cum, 1.0, 0.0).astype(BF16), jnp.where(same, 1.0, 0.0).astype(BF16)


def _causal_mask(is_fwd, n):
    r = lax.broadcasted_iota(jnp.int32, (n, n), 0)
    c = lax.broadcasted_iota(jnp.int32, (n, n), 1)
    return jnp.where(is_fwd, c, r) <= jnp.where(is_fwd, r, c)


def _scan_tile(d, j, nt):
    return jnp.where(d == 0, j, jnp.where(j == 0, 0, nt - j))


NP_TM = 3 * TT


def _norm_proj_kernel(x_ref, g_ref, sh_ref, sc_ref, w_ref, b_ref, o_ref):
    x = x_ref[0]
    row = lax.broadcasted_iota(jnp.int32, (NP_TM, 1), 0) + pl.program_id(2) * NP_TM
    is_ctx = row < TT
    scale = jnp.where(is_ctx, sc_ref[0, 0], sc_ref[0, 1])
    shift = jnp.where(is_ctx, sh_ref[0, 0], sh_ref[0, 1])
    ms = jnp.mean(x * x, axis=-1, keepdims=True)
    h = (x * lax.rsqrt(ms + 1e-6) * g_ref[...] * (1.0 + scale) + shift).astype(BF16)
    o_ref[0] = _dot(h, w_ref[...]) + b_ref[...]


def norm_proj(x, g, shift, scale, w, b, tn):
    bsz, t, d = x.shape
    n = w.shape[1]
    assert t % NP_TM == 0
    return pl.pallas_call(
        _norm_proj_kernel,
        grid=(n // tn, bsz, t // NP_TM),
        in_specs=[
            pl.BlockSpec((1, NP_TM, d), lambda k, i, j: (i, j, 0)),
            pl.BlockSpec((1, d), lambda k, i, j: (0, 0)),
            pl.BlockSpec((1, 2, 1, d), lambda k, i, j: (i, 0, 0, 0)),
            pl.BlockSpec((1, 2, 1, d), lambda k, i, j: (i, 0, 0, 0)),
            pl.BlockSpec((d, tn), lambda k, i, j: (0, k)),
            pl.BlockSpec((1, tn), lambda k, i, j: (0, k)),
        ],
        out_specs=pl.BlockSpec((1, NP_TM, tn), lambda k, i, j: (i, j, k)),
        out_shape=jax.ShapeDtypeStruct((bsz, t, n), F32),
        compiler_params=pltpu.CompilerParams(
            dimension_semantics=("parallel", "parallel", "parallel"), vmem_limit_bytes=VMEM_LIMIT),
        name="norm_proj",
    )(x, g.reshape(1, d), shift, scale, w, b.reshape(1, n))


def _conv3_piece(x, lh, rh, w, b, left_ok, right_ok):
    n = x.shape[0]
    row = lax.broadcasted_iota(jnp.int32, x.shape, 0)
    prev_row = jnp.where(left_ok, lh[7:8, :], 0.0)
    next_row = jnp.where(right_ok, rh[0:1, :], 0.0)
    x_prev = jnp.where(row == 0, prev_row, pltpu.roll(x, 1, 0))
    x_next = jnp.where(row == n - 1, next_row, pltpu.roll(x, n - 1, 0))
    return _silu(w[0:1, :] * x_prev + w[1:2, :] * x + w[2:3, :] * x_next + b)


def _ab_prep_kernel(xs_ref, bm_ref, cm_ref, xsl_ref, bml_ref, cml_ref, xsr_ref, bmr_ref, cmr_ref,
                    w_ref, b_ref, o_ref):
    j = pl.program_id(1)
    nt = pl.num_programs(1)
    left_ok = j >= 2
    right_ok = jnp.logical_and(j >= 1, j < nt - 1)
    w = w_ref[...]
    b = b_ref[...]
    o_ref[0, :, 0:SSD_INNER] = _conv3_piece(xs_ref[0], xsl_ref[0], xsr_ref[0], w[:, 0:SSD_INNER],
                                            b[:, 0:SSD_INNER], left_ok, right_ok)
    c0, c1 = SSD_INNER, SSD_INNER + SSD_BC
    o_ref[0, :, c0:c1] = _conv3_piece(bm_ref[0], bml_ref[0], bmr_ref[0], w[:, c0:c1], b[:, c0:c1], left_ok, right_ok)
    c0, c1 = c1, c1 + SSD_BC
    o_ref[0, :, c0:c1] = _conv3_piece(cm_ref[0], cml_ref[0], cmr_ref[0], w[:, c0:c1], b[:, c0:c1], left_ok, right_ok)


def ab_prep(proj, conv_w, conv_b):
    bsz, t, _ = proj.shape
    nt = t // TT
    rb = TT // 8
    nrb = t // 8
    cw = SSD_INNER + 2 * SSD_BC

    def cur(width, col):
        return pl.BlockSpec((1, TT, width), lambda i, j: (i, j, col // width))

    def left(width, col):
        return pl.BlockSpec((1, 8, width), lambda i, j: (i, jnp.maximum(j * rb - 1, 0), col // width))

    def right(width, col):
        return pl.BlockSpec((1, 8, width), lambda i, j: (i, jnp.minimum((j + 1) * rb, nrb - 1), col // width))

    pieces = ((SSD_INNER, AB_XS), (SSD_BC, AB_BM), (SSD_BC, AB_CM))
    return pl.pallas_call(
        _ab_prep_kernel,
        grid=(bsz, nt),
        in_specs=[cur(*p) for p in pieces] + [left(*p) for p in pieces] + [right(*p) for p in pieces] + [
            pl.BlockSpec((3, cw), lambda i, j: (0, 0)),
            pl.BlockSpec((1, cw), lambda i, j: (0, 0)),
        ],
        out_specs=pl.BlockSpec((1, TT, cw), lambda i, j: (i, j, 0)),
        out_shape=jax.ShapeDtypeStruct((bsz, t, cw), F32),
        compiler_params=pltpu.CompilerParams(dimension_semantics=("parallel", "parallel")),
        name="ab_prep",
    )(*([proj] * 9), conv_w, conv_b.reshape(1, cw))


def _gla_kernel(q_ref, k_ref, v_ref, sm_ref, wg_ref, bg_ref, o_ref, qg_s, kn_s, kd_s, egl_s, st_s):
    d = pl.program_id(1)
    j = pl.program_id(2)
    is_fwd = d == 0

    @pl.when(j == 0)
    def _():
        st_s[...] = jnp.zeros_like(st_s)

    gz = _dot(sm_ref[0].astype(BF16), wg_ref[0]) + bg_ref[0]
    logg = (jnp.minimum(gz, 0.0) - jnp.log(1.0 + jnp.exp(-jnp.abs(gz)))) * (1.0 / GLA_GATE_TAU)
    m_cum, m_all = _chunk_masks(is_fwd, TT)
    gc = _sel_dot(m_cum, logg)
    gl = _sel_dot(m_all, logg)
    q = q_ref[0] * (GLA_DK ** -0.5)
    k = k_ref[0]
    qg_s[...] = (q * jnp.exp(gc)).astype(BF16)
    kn_s[...] = (k * jnp.exp(-gc)).astype(BF16)
    kd_s[...] = (k * jnp.exp(gl - gc)).astype(BF16)
    egl_s[...] = jnp.exp(gl)
    causal = _causal_mask(is_fwd, CHUNK)

    for ci in range(CPT):
        off = pl.multiple_of(jnp.where(is_fwd, ci, CPT - 1 - ci) * CHUNK, CHUNK)
        rows = pl.ds(off, CHUNK)
        for h in range(GLA_HEADS):
            kc = slice(h * GLA_DK, (h + 1) * GLA_DK)
            vc = slice(h * GLA_DV, (h + 1) * GLA_DV)
            qg = qg_s[rows, kc]
            v = v_ref[0, rows, vc]
            st = st_s[h]
            att = jnp.where(causal, _dot_nt(qg, kn_s[rows, kc]), 0.0)
            o_ref[0, 0, rows, vc] = _dot(att.astype(BF16), v.astype(BF16)) + _dot_nt(qg, st.astype(BF16))
            st_s[h] = st * egl_s[pl.ds(off, 1), kc] + _dot(v.T.astype(BF16), kd_s[rows, kc])


def gla_scan(proj, wg, bg):
    bsz, t, _ = proj.shape
    nt = t // TT
    tile = lambda d, j: _scan_tile(d, j, nt)
    return pl.pallas_call(
        _gla_kernel,
        grid=(bsz, 2, nt),
        in_specs=[
            pl.BlockSpec((1, TT, GLA_QK), lambda i, d, j: (i, tile(d, j), AB_Q // GLA_QK)),
            pl.BlockSpec((1, TT, GLA_QK), lambda i, d, j: (i, tile(d, j), AB_K // GLA_QK)),
            pl.BlockSpec((1, TT, GLA_V), lambda i, d, j: (i, tile(d, j), AB_V // GLA_V)),
            pl.BlockSpec((1, TT, LANE), lambda i, d, j: (i, tile(d, j), AB_SMALL // LANE)),
            pl.BlockSpec((1, LANE, GLA_QK), lambda i, d, j: (d, 0, 0)),
            pl.BlockSpec((1, 1, GLA_QK), lambda i, d, j: (d, 0, 0)),
        ],
        out_specs=pl.BlockSpec((1, 1, TT, GLA_V), lambda i, d, j: (d, i, tile(d, j), 0)),
        out_shape=jax.ShapeDtypeStruct((2, bsz, t, GLA_V), F32),
        scratch_shapes=[pltpu.VMEM((TT, GLA_QK), BF16), pltpu.VMEM((TT, GLA_QK), BF16),
                        pltpu.VMEM((TT, GLA_QK), BF16), pltpu.VMEM((TT, GLA_QK), F32),
                        pltpu.VMEM((GLA_HEADS, GLA_DV, GLA_DK), F32)],
        compiler_params=pltpu.CompilerParams(
            dimension_semantics=("parallel", "parallel", "arbitrary")),
        name="gla_scan",
    )(proj, proj, proj, proj, wg, bg)


def _dot_sel2(x, e):
    hi = x.astype(BF16)
    return _dot(hi, e) + _dot((x - hi.astype(F32)).astype(BF16), e)


def _ssd_kernel(xs_ref, bm_ref, cm_ref, sm_ref, dtb_ref, nega_ref, e_ref, o_ref,
                v_s, vw_s, cdec_s, dec_s, ah_s, st_s):
    d = pl.program_id(1)
    j = pl.program_id(2)
    is_fwd = d == 0
    gw = SSD_HPG * SSD_HEADDIM

    @pl.when(j == 0)
    def _():
        st_s[...] = jnp.zeros_like(st_s)

    dt = _softplus(sm_ref[0] + dtb_ref[0])
    la = dt * nega_ref[0]
    m_cum, m_all = _chunk_masks(is_fwd, TT)
    acum = _sel_dot(m_cum, la)
    atot = _sel_dot(m_all, la)
    e = e_ref[0]
    v = xs_ref[0] * _dot_sel2(dt, e)
    v_s[...] = v.astype(BF16)
    vw_s[...] = (v * _dot_sel2(jnp.exp(atot - acum), e)).astype(BF16)
    cdec_s[...] = _dot_sel2(jnp.exp(acum), e)
    dec_s[...] = _dot_sel2(jnp.exp(atot), e)
    ah_s[...] = pltpu.roll(acum, LANE - SM_DT - d * SSD_HEADS, 1)
    causal = _causal_mask(is_fwd, CHUNK)

    for ci in range(CPT):
        off = pl.multiple_of(jnp.where(is_fwd, ci, CPT - 1 - ci) * CHUNK, CHUNK)
        rows = pl.ds(off, CHUNK)
        ah = ah_s[rows, :]
        aht = ah.T
        for g in range(SSD_GROUPS):
            gc = slice(g * gw, (g + 1) * gw)
            nc = slice(g * SSD_STATE, (g + 1) * SSD_STATE)
            bm = bm_ref[0, rows, nc]
            cm = cm_ref[0, rows, nc].astype(BF16)
            st = st_s[g]
            cb = _dot_nt(cm, bm.astype(BF16))
            y_inter = _dot(cm, st.astype(BF16)) * cdec_s[rows, gc]
            v_c = v_s[rows, gc]
            ys = []
            for hh in range(SSD_HPG):
                h = g * SSD_HPG + hh
                seg = jnp.exp(jnp.where(causal, ah[:, h:h + 1] - aht[h:h + 1, :], -1e30))
                ys.append(_dot((seg * cb).astype(BF16), v_c[:, hh * SSD_HEADDIM:(hh + 1) * SSD_HEADDIM]))
            o_ref[0, 0, rows, gc] = jnp.concatenate(ys, axis=1) + y_inter
            st_s[g] = st * dec_s[pl.ds(off, 1), gc] + _dot(bm.T.astype(BF16), vw_s[rows, gc])


def ssd_scan(xbc, proj, dtb, nega, e):
    bsz, t, _ = xbc.shape
    nt = t // TT
    gw = SSD_HPG * SSD_HEADDIM
    tile = lambda d, j: _scan_tile(d, j, nt)
    return pl.pallas_call(
        _ssd_kernel,
        grid=(bsz, 2, nt),
        in_specs=[
            pl.BlockSpec((1, TT, SSD_INNER), lambda i, d, j: (i, tile(d, j), 0)),
            pl.BlockSpec((1, TT, SSD_BC), lambda i, d, j: (i, tile(d, j), SSD_INNER // SSD_BC)),
            pl.BlockSpec((1, TT, SSD_BC), lambda i, d, j: (i, tile(d, j), SSD_INNER // SSD_BC + 1)),
            pl.BlockSpec((1, TT, LANE), lambda i, d, j: (i, tile(d, j), AB_SMALL // LANE)),
            pl.BlockSpec((1, 1, LANE), lambda i, d, j: (d, 0, 0)),
            pl.BlockSpec((1, 1, LANE), lambda i, d, j: (d, 0, 0)),
            pl.BlockSpec((1, LANE, SSD_INNER), lambda i, d, j: (d, 0, 0)),
        ],
        out_specs=pl.BlockSpec((1, 1, TT, SSD_INNER), lambda i, d, j: (d, i, tile(d, j), 0)),
        out_shape=jax.ShapeDtypeStruct((2, bsz, t, SSD_INNER), F32),
        scratch_shapes=[pltpu.VMEM((TT, SSD_INNER), BF16), pltpu.VMEM((TT, SSD_INNER), BF16),
                        pltpu.VMEM((TT, SSD_INNER), F32), pltpu.VMEM((TT, SSD_INNER), F32),
                        pltpu.VMEM((TT, LANE), F32), pltpu.VMEM((SSD_GROUPS, SSD_STATE, gw), F32)],
        compiler_params=pltpu.CompilerParams(
            dimension_semantics=("parallel", "parallel", "arbitrary")),
        name="ssd_scan",
    )(xbc, xbc, xbc, proj, dtb, nega, e)


def _group_rmsnorm(x, width):
    parts = []
    for s in range(x.shape[1] // width):
        seg = x[:, s * width:(s + 1) * width]
        parts.append(seg * lax.rsqrt(jnp.mean(seg * seg, axis=-1, keepdims=True) + 1e-6))
    return jnp.concatenate(parts, axis=1)


def _ab_out_kernel(of_ref, ob_ref, yf_ref, yb_ref, r_ref, z_ref, xs_ref, gg_ref, dv_ref, sg_ref,
                   w_ref, b_ref, x_ref, gate_ref, o_ref):
    o = _group_rmsnorm(of_ref[0, 0] + ob_ref[0, 0], GLA_DV) * gg_ref[...] * _silu(r_ref[0])
    y = (yf_ref[0, 0] + yb_ref[0, 0] + dv_ref[...] * xs_ref[0]) * _silu(z_ref[0])
    y = _group_rmsnorm(y, SSD_INNER // SSD_GROUPS) * sg_ref[...]
    m = _dot(o.astype(BF16), w_ref[0:GLA_V, :]) + _dot(y.astype(BF16), w_ref[GLA_V:, :]) + b_ref[...]
    o_ref[0] = x_ref[0] + gate_ref[0, 0] * m


def ab_out(o_gla, y_ssd, proj, xbc, gla_g, d_vec, ssd_g, w, b, x, gate):
    bsz, t, d = x.shape
    seg = lambda i, j: (i, jnp.minimum(j, 1), 0, 0)
    row = lambda width: pl.BlockSpec((1, width), lambda i, j: (0, 0))
    return pl.pallas_call(
        _ab_out_kernel,
        grid=(bsz, t // TT),
        in_specs=[
            pl.BlockSpec((1, 1, TT, GLA_V), lambda i, j: (0, i, j, 0)),
            pl.BlockSpec((1, 1, TT, GLA_V), lambda i, j: (1, i, j, 0)),
            pl.BlockSpec((1, 1, TT, SSD_INNER), lambda i, j: (0, i, j, 0)),
            pl.BlockSpec((1, 1, TT, SSD_INNER), lambda i, j: (1, i, j, 0)),
            pl.BlockSpec((1, TT, GLA_V), lambda i, j: (i, j, AB_R // GLA_V)),
            pl.BlockSpec((1, TT, SSD_INNER), lambda i, j: (i, j, AB_Z // SSD_INNER)),
            pl.BlockSpec((1, TT, SSD_INNER), lambda i, j: (i, j, 0)),
            row(GLA_V), row(SSD_INNER), row(SSD_INNER),
            pl.BlockSpec((GLA_V + SSD_INNER, d), lambda i, j: (0, 0)),
            row(d),
            pl.BlockSpec((1, TT, d), lambda i, j: (i, j, 0)),
            pl.BlockSpec((1, 1, 1, d), seg),
        ],
        out_specs=pl.BlockSpec((1, TT, d), lambda i, j: (i, j, 0)),
        out_shape=jax.ShapeDtypeStruct((bsz, t, d), F32),
        compiler_params=pltpu.CompilerParams(
            dimension_semantics=("parallel", "parallel"), vmem_limit_bytes=VMEM_LIMIT),
        name="ab_out",
    )(o_gla, o_gla, y_ssd, y_ssd, proj, proj, xbc, gla_g.reshape(1, -1), d_vec.reshape(1, -1),
      ssd_g.reshape(1, -1), w, b.reshape(1, d), x, gate)


def _cd_prep_kernel(q_ref, k_ref, v_ref, ql_ref, kl_ref, vl_ref, qr_ref, kr_ref, vr_ref, w_ref, o_ref):
    j = pl.program_id(1)
    nt = pl.num_programs(1)
    left_ok = j >= 2
    right_ok = jnp.logical_and(j >= 1, j < nt - 1)
    w = w_ref[...]
    srcs = ((q_ref, ql_ref, qr_ref, GDN_DK ** -0.5), (k_ref, kl_ref, kr_ref, 1.0), (v_ref, vl_ref, vr_ref, None))
    for s, (c_ref, l_ref, r_ref, scale) in enumerate(srcs):
        c0 = s * GDN_QK
        y = _conv3_piece(c_ref[0], l_ref[0], r_ref[0], w[:, c0:c0 + GDN_QK], 0.0, left_ok, right_ok)
        if scale is None:
            o_ref[0, :, c0:c0 + GDN_QK] = y
            continue
        for h in range(GDN_HEADS):
            seg = y[:, h * GDN_DK:(h + 1) * GDN_DK]
            inv = lax.rsqrt(jnp.sum(seg * seg, axis=-1, keepdims=True) + 1e-6) * scale
            o_ref[0, :, c0 + h * GDN_DK:c0 + (h + 1) * GDN_DK] = seg * inv


def cd_prep(proj, conv_w):
    bsz, t, _ = proj.shape
    nt = t // TT
    rb = TT // 8
    nrb = t // 8
    width = GDN_QK
    cols = (2, 3, 4)

    cur = lambda cb: pl.BlockSpec((1, TT, width), lambda i, j: (i, j, cb))
    left = lambda cb: pl.BlockSpec((1, 8, width), lambda i, j: (i, jnp.maximum(j * rb - 1, 0), cb))
    right = lambda cb: pl.BlockSpec((1, 8, width), lambda i, j: (i, jnp.minimum((j + 1) * rb, nrb - 1), cb))
    return pl.pallas_call(
        _cd_prep_kernel,
        grid=(bsz, nt),
        in_specs=[cur(cb) for cb in cols] + [left(cb) for cb in cols] + [right(cb) for cb in cols] + [
            pl.BlockSpec((3, 3 * width), lambda i, j: (0, 0))],
        out_specs=pl.BlockSpec((1, TT, 3 * width), lambda i, j: (i, j, 0)),
        out_shape=jax.ShapeDtypeStruct((bsz, t, 3 * width), F32),
        compiler_params=pltpu.CompilerParams(dimension_semantics=("parallel", "parallel")),
        name="cd_prep",
    )(*([proj] * 9), conv_w)


GDN_HB = 8


def _mm2(a, b):
    return _dot(a.astype(BF16), b.astype(BF16))


def _unit_tri_inverse(mats, b16, b32, eye):
    each = lambda f, *ls: [f(*xs) for xs in zip(*ls)]
    d16 = each(lambda a: jnp.where(b16, a, 0.0), mats)
    d2 = each(lambda x: _mm2(x, x), d16)
    d4 = each(lambda x: _mm2(x, x), d2)
    d8 = each(lambda x: _mm2(x, x), d4)
    t = each(lambda x: eye - x, d16)
    for p in (d2, d4, d8):
        t = each(lambda x, y: x + _mm2(x, y), t, p)
    off32 = jnp.logical_and(b32, jnp.logical_not(b16))
    for sel in (off32, jnp.logical_not(b32)):
        a_off = each(lambda a: jnp.where(sel, a, 0.0), mats)
        inner = each(_mm2, a_off, t)
        t = each(lambda x, y: x - _mm2(x, y), t, inner)
    return t


def _gdn_kernel(q_ref, k_ref, v_ref, sm_ref, dtb_ref, nega_ref, o_ref,
                u_s, w_s, kd_s, qg_s, aqk_s, dec_s, st_s):
    d = pl.program_id(2)
    j = pl.program_id(3)
    is_fwd = d == 0

    @pl.when(j == 0)
    def _():
        st_s[...] = jnp.zeros_like(st_s)

    sm = sm_ref[0]
    m_cum, m_all = _chunk_masks(is_fwd, TT)
    first = d * GDN_HEADS + pl.program_id(1) * GDN_HB
    rot = jnp.where(first == 0, 0, LANE - first)
    la = pltpu.roll(_softplus(sm + dtb_ref[...]) * nega_ref[...], rot, 1)
    be_sm = pltpu.roll(jax.nn.sigmoid(sm), rot, 1)
    gc_sm = _sel_dot(m_cum, la)
    gl_sm = _sel_dot(m_all, la)
    lane_bcast = lambda a, col: jnp.broadcast_to(a[:, col:col + 1], (TT, GDN_DK))

    r = lax.broadcasted_iota(jnp.int32, (CHUNK, CHUNK), 0)
    c = lax.broadcasted_iota(jnp.int32, (CHUNK, CHUNK), 1)
    causal = _causal_mask(is_fwd, CHUNK)
    strict = jnp.logical_and(causal, r != c)
    b16 = lax.shift_right_logical(r, 4) == lax.shift_right_logical(c, 4)
    b32 = lax.shift_right_logical(r, 5) == lax.shift_right_logical(c, 5)
    eye = jnp.where(r == c, 1.0, 0.0)
    chunk_rows = [slice(ci * CHUNK, (ci + 1) * CHUNK) for ci in range(CPT)]

    amats, rhss = [], []
    for hh in range(GDN_HB):
        cols = slice(hh * GDN_DK, (hh + 1) * GDN_DK)
        gc = lane_bcast(gc_sm, hh)
        gl = lane_bcast(gl_sm, hh)
        beta_e = lane_bcast(be_sm, 2 * GDN_HEADS + hh)
        q = q_ref[0, :, cols]
        k = k_ref[0, :, cols]
        egc = jnp.exp(gc)
        kb = k * beta_e
        qg_s[:, cols] = (q * egc).astype(BF16)
        kd_s[:, cols] = k * jnp.exp(gl - gc)
        dec_s[:, cols] = jnp.exp(gl)
        rhs = jnp.concatenate([v_ref[0, :, cols] * beta_e, kb * egc], axis=1)
        for ci, rows in enumerate(chunk_rows):
            gcc = gc[rows, :]
            dmat = jnp.exp(jnp.where(causal, gcc[:, 0:CHUNK] - gcc.T[0:CHUNK, :], -1e30))
            kc = k[rows].astype(BF16)
            amats.append(jnp.where(strict, _dot_nt(kb[rows].astype(BF16), kc) * dmat, 0.0))
            rhss.append(rhs[rows])
            aqk_s[rows, hh * CHUNK:(hh + 1) * CHUNK] = (_dot_nt(q[rows].astype(BF16), kc) * dmat).astype(BF16)
    tinv = _unit_tri_inverse(amats, b16, b32, eye)
    for n, (t, rhs_c) in enumerate(zip(tinv, rhss)):
        hh, rows = n // CPT, chunk_rows[n % CPT]
        cols = slice(hh * GDN_DK, (hh + 1) * GDN_DK)
        sol = _mm2(t, rhs_c)
        u_s[rows, cols] = sol[:, 0:GDN_DV]
        w_s[rows, cols] = sol[:, GDN_DV:].astype(BF16)

    for ci in range(CPT):
        off = pl.multiple_of(jnp.where(is_fwd, ci, CPT - 1 - ci) * CHUNK, CHUNK)
        rows = pl.ds(off, CHUNK)
        for hh in range(GDN_HB):
            cols = slice(hh * GDN_DK, (hh + 1) * GDN_DK)
            st = st_s[hh]
            stb = st.astype(BF16)
            vn = u_s[rows, cols] - _dot(w_s[rows, cols], stb)
            vnb = vn.astype(BF16)
            o_ref[0, 0, rows, cols] = (_dot(qg_s[rows, cols], stb)
                                       + _dot(aqk_s[rows, hh * CHUNK:(hh + 1) * CHUNK], vnb))
            st_s[hh] = st * dec_s[pl.ds(off, 1), cols] + _dot(kd_s[rows, cols].T.astype(BF16), vnb)


def gdn_scan(qkv, proj, dtb, nega):
    bsz, t, _ = qkv.shape
    nt = t // TT
    tile = lambda d, j: _scan_tile(d, j, nt)
    ng = GDN_HEADS // GDN_HB
    wb = GDN_HB * GDN_DK
    return pl.pallas_call(
        _gdn_kernel,
        grid=(bsz, ng, 2, nt),
        in_specs=[
            pl.BlockSpec((1, TT, wb), lambda i, h, d, j: (i, tile(d, j), h)),
            pl.BlockSpec((1, TT, wb), lambda i, h, d, j: (i, tile(d, j), ng + h)),
            pl.BlockSpec((1, TT, wb), lambda i, h, d, j: (i, tile(d, j), 2 * ng + h)),
            pl.BlockSpec((1, TT, LANE), lambda i, h, d, j: (i, tile(d, j), 6 * D_MODEL // LANE)),
            pl.BlockSpec((1, LANE), lambda i, h, d, j: (0, 0)),
            pl.BlockSpec((1, LANE), lambda i, h, d, j: (0, 0)),
        ],
        out_specs=pl.BlockSpec((1, 1, TT, wb), lambda i, h, d, j: (d, i, tile(d, j), h)),
        out_shape=jax.ShapeDtypeStruct((2, bsz, t, GDN_V), F32),
        scratch_shapes=[pltpu.VMEM((TT, wb), F32), pltpu.VMEM((TT, wb), BF16),
                        pltpu.VMEM((TT, wb), F32), pltpu.VMEM((TT, wb), BF16),
                        pltpu.VMEM((TT, GDN_HB * CHUNK), BF16), pltpu.VMEM((TT, wb), F32),
                        pltpu.VMEM((GDN_HB, GDN_DK, GDN_DV), F32)],
        compiler_params=pltpu.CompilerParams(
            dimension_semantics=("parallel", "parallel", "parallel", "arbitrary")),
        name="gdn_scan",
    )(qkv, qkv, qkv, proj, dtb, nega)


CONF_PAD = 16


def _cd_out_kernel(ga_ref, gb_ref, og_ref, of_ref, ob_ref, cw_ref, cb_ref, lg_ref, lb_ref, ng_ref,
                   w_ref, b_ref, x_ref, gate_ref, o_ref, pad_s, shift_s, conv_s):
    j = pl.program_id(1)
    half = (CONF_KERNEL - 1) // 2
    glu = ga_ref[0] * jax.nn.sigmoid(gb_ref[0])
    zeros = jnp.zeros((CONF_PAD, CONF_CH), F32)

    def conv_segments(seglen):
        stride = seglen + 2 * CONF_PAD
        for g in range(TT // seglen):
            base = g * stride
            pad_s[base:base + CONF_PAD, :] = zeros
            pad_s[base + CONF_PAD:base + CONF_PAD + seglen, :] = glu[g * seglen:(g + 1) * seglen]
            pad_s[base + CONF_PAD + seglen:base + stride, :] = zeros
        nseg = TT // seglen
        used = nseg * stride
        conv_s[...] = jnp.zeros((TT, CONF_CH), F32) + cb_ref[...]
        for phase in range(SUBLANES):
            taps = [kk for kk in range(CONF_KERNEL) if (CONF_PAD - half + kk) % SUBLANES == phase]
            if phase:
                shift_s[0:used - SUBLANES, :] = pad_s[phase:used - SUBLANES + phase, :]
            src = shift_s if phase else pad_s
            for g in range(nseg):
                acc = conv_s[g * seglen:(g + 1) * seglen, :]
                for kk in taps:
                    lo = g * stride + CONF_PAD - half + kk - phase
                    acc = acc + src[lo:lo + seglen, :] * cw_ref[kk:kk + 1, :]
                conv_s[g * seglen:(g + 1) * seglen, :] = acc

    @pl.when(j == 0)
    def _():
        conv_segments(TT)

    @pl.when(j > 0)
    def _():
        conv_segments(GRID_W)

    acc = conv_s[...]
    mu = jnp.mean(acc, axis=-1, keepdims=True)
    cen = acc - mu
    var = jnp.mean(cen * cen, axis=-1, keepdims=True)
    conv = _silu(cen * lax.rsqrt(var + 1e-5) * lg_ref[...] + lb_ref[...])
    o = _group_rmsnorm(of_ref[0, 0] + ob_ref[0, 0], GDN_DV) * ng_ref[...] * _silu(og_ref[0])
    m = _dot(conv.astype(BF16), w_ref[0:CONF_CH, :]) + _dot(o.astype(BF16), w_ref[CONF_CH:, :]) + b_ref[...]
    o_ref[0] = x_ref[0] + gate_ref[0, 0] * m


def cd_out(o_gdn, proj, conv_w, conv_b, ln_g, ln_b, norm_g, w, b, x, gate):
    bsz, t, d = x.shape
    seg = lambda i, j: (i, jnp.minimum(j, 1), 0, 0)
    row = lambda width: pl.BlockSpec((1, width), lambda i, j: (0, 0))
    return pl.pallas_call(
        _cd_out_kernel,
        grid=(bsz, t // TT),
        in_specs=[
            pl.BlockSpec((1, TT, CONF_CH), lambda i, j: (i, j, 0)),
            pl.BlockSpec((1, TT, CONF_CH), lambda i, j: (i, j, 1)),
            pl.BlockSpec((1, TT, GDN_V), lambda i, j: (i, j, 5)),
            pl.BlockSpec((1, 1, TT, GDN_V), lambda i, j: (0, i, j, 0)),
            pl.BlockSpec((1, 1, TT, GDN_V), lambda i, j: (1, i, j, 0)),
            pl.BlockSpec((CONF_KERNEL, CONF_CH), lambda i, j: (0, 0)),
            row(CONF_CH), row(CONF_CH), row(CONF_CH), row(GDN_V),
            pl.BlockSpec((CONF_CH + GDN_V, d), lambda i, j: (0, 0)),
            row(d),
            pl.BlockSpec((1, TT, d), lambda i, j: (i, j, 0)),
            pl.BlockSpec((1, 1, 1, d), seg),
        ],
        out_specs=pl.BlockSpec((1, TT, d), lambda i, j: (i, j, 0)),
        out_shape=jax.ShapeDtypeStruct((bsz, t, d), F32),
        scratch_shapes=[pltpu.VMEM(((TT // GRID_W) * (GRID_W + 2 * CONF_PAD), CONF_CH), F32),
                        pltpu.VMEM(((TT // GRID_W) * (GRID_W + 2 * CONF_PAD), CONF_CH), F32),
                        pltpu.VMEM((TT, CONF_CH), F32)],
        compiler_params=pltpu.CompilerParams(
            dimension_semantics=("parallel", "parallel"), vmem_limit_bytes=VMEM_LIMIT),
        name="cd_out",
    )(proj, proj, proj, o_gdn, o_gdn, conv_w, conv_b.reshape(1, -1), ln_g.reshape(1, -1), ln_b.reshape(1, -1),
      norm_g.reshape(1, -1), w, b.reshape(1, d), x, gate)


def _out_proj_kernel(m_ref, w_ref, b_ref, x_ref, gate_ref, o_ref):
    y = _dot(m_ref[0].astype(BF16), w_ref[...]) + b_ref[...]
    o_ref[0] = x_ref[0] + gate_ref[0] * y


def out_proj_residual(mixed, w, b, x, gate):
    bsz, t, k = mixed.shape
    d = w.shape[1]
    tm = min(t, 512)
    return pl.pallas_call(
        _out_proj_kernel,
        grid=(bsz, t // tm),
        in_specs=[
            pl.BlockSpec((1, tm, k), lambda i, j: (i, j, 0)),
            pl.BlockSpec((k, d), lambda i, j: (0, 0)),
            pl.BlockSpec((1, d), lambda i, j: (0, 0)),
            pl.BlockSpec((1, tm, d), lambda i, j: (i, j, 0)),
            pl.BlockSpec((1, 1, d), lambda i, j: (i, 0, 0)),
        ],
        out_specs=pl.BlockSpec((1, tm, d), lambda i, j: (i, j, 0)),
        out_shape=jax.ShapeDtypeStruct((bsz, t, d), F32),
        compiler_params=pltpu.CompilerParams(
            dimension_semantics=("parallel", "parallel"), vmem_limit_bytes=VMEM_LIMIT),
        name="out_proj",
    )(mixed, w, b.reshape(1, d), x, gate.reshape(bsz, 1, d))


def _router_kernel(x_ref, g_ref, sh_ref, sc_ref, rw_ref, h_ref, aff_ref):
    x = x_ref[0]
    ms = jnp.mean(x * x, axis=-1, keepdims=True)
    h = (x * lax.rsqrt(ms + 1e-6) * g_ref[...] * (1.0 + sc_ref[0, 0]) + sh_ref[0, 0]).astype(BF16)
    h_ref[0] = h
    logits = _dot(h, rw_ref[...])
    lane = lax.broadcasted_iota(jnp.int32, logits.shape, 1)
    logits = jnp.where(lane < N_EXPERTS, logits, -1e30)
    e = jnp.exp(logits - jnp.max(logits, axis=-1, keepdims=True))
    aff = e / jnp.sum(e, axis=-1, keepdims=True)
    aff_ref[0] = aff.T[0:N_EXPERTS, :]


def moe_router(xa, g, shift, scale, rw, seg, tile0, ntiles):
    bsz, _, d = xa.shape
    n = ntiles * TT
    return pl.pallas_call(
        _router_kernel,
        grid=(bsz, ntiles),
        in_specs=[
            pl.BlockSpec((1, TT, d), lambda i, j: (i, j + tile0, 0)),
            pl.BlockSpec((1, d), lambda i, j: (0, 0)),
            pl.BlockSpec((1, 1, 1, d), lambda i, j: (i, seg, 0, 0)),
            pl.BlockSpec((1, 1, 1, d), lambda i, j: (i, seg, 0, 0)),
            pl.BlockSpec((d, LANE), lambda i, j: (0, 0)),
        ],
        out_specs=[pl.BlockSpec((1, TT, d), lambda i, j: (i, j, 0)),
                   pl.BlockSpec((1, N_EXPERTS, TT), lambda i, j: (i, 0, j))],
        out_shape=[jax.ShapeDtypeStruct((bsz, n, d), BF16), jax.ShapeDtypeStruct((bsz, N_EXPERTS, n), F32)],
        compiler_params=pltpu.CompilerParams(dimension_semantics=("parallel", "parallel")),
        name="moe_router",
    )(xa, g.reshape(1, d), shift, scale, rw)


def _lane_block_prefix(x, u_strict):
    nblk = x.shape[1] // LANE
    run = jnp.zeros((x.shape[0], 1), F32)
    outs = []
    for cblk in range(nblk):
        xc = x[:, cblk * LANE:(cblk + 1) * LANE]
        outs.append(_dot(xc.astype(BF16), u_strict) + run)
        run = run + jnp.sum(xc, axis=-1, keepdims=True)
    return jnp.concatenate(outs, axis=1), run


def _select_kernel(aff_ref, slot_ref, *, cap):
    aff = aff_ref[0]
    bits = pltpu.bitcast(aff, jnp.int32)
    capf = jnp.float32(cap)

    def step(i, thr):
        cand = jnp.bitwise_or(thr, lax.shift_left(jnp.int32(1), 30 - i))
        cnt = jnp.sum(jnp.where(bits >= cand, 1.0, 0.0), axis=-1, keepdims=True)
        return jnp.where(cnt >= capf, cand, thr)

    thr = lax.fori_loop(0, 31, step, jnp.zeros((aff.shape[0], 1), jnp.int32))
    gt = jnp.where(bits > thr, 1.0, 0.0)
    eq = jnp.where(bits == thr, 1.0, 0.0)
    r = lax.broadcasted_iota(jnp.int32, (LANE, LANE), 0)
    c = lax.broadcasted_iota(jnp.int32, (LANE, LANE), 1)
    u_strict = jnp.where(r < c, 1.0, 0.0).astype(BF16)
    need = capf - jnp.sum(gt, axis=-1, keepdims=True)
    eq_rank, _ = _lane_block_prefix(eq, u_strict)
    sel = jnp.maximum(gt, jnp.where(eq_rank < need, eq, 0.0))
    slot, _ = _lane_block_prefix(sel, u_strict)
    slot_ref[0] = jnp.where(sel > 0.0, slot.astype(jnp.int32), -1)


def moe_select(aff, cap):
    bsz, ne, n = aff.shape
    return pl.pallas_call(
        functools.partial(_select_kernel, cap=cap),
        grid=(bsz,),
        in_specs=[pl.BlockSpec((1, ne, n), lambda i: (i, 0, 0))],
        out_specs=pl.BlockSpec((1, ne, n), lambda i: (i, 0, 0)),
        out_shape=jax.ShapeDtypeStruct((bsz, ne, n), jnp.int32),
        compiler_params=pltpu.CompilerParams(dimension_semantics=("parallel",)),
        name="moe_select",
    )(aff)


def _slot_index_kernel(slot_ref, idx_ref, *, cap):
    slot = slot_ref[0]
    n = slot.shape[1]
    srow = lax.broadcasted_iota(jnp.int32, (cap, LANE), 0)
    lane = lax.broadcasted_iota(jnp.int32, (cap, LANE), 1)
    acc = jnp.zeros((cap, LANE), jnp.int32)
    for cblk in range(n // LANE):
        s_c = slot[:, cblk * LANE:(cblk + 1) * LANE]
        acc = acc + jnp.where(srow == s_c, lane + (cblk * LANE + 1), 0)
    ones = jnp.ones((8, LANE), BF16)
    hi = _dot_nt(ones, lax.shift_right_logical(acc, 7).astype(F32).astype(BF16))
    lo = _dot_nt(ones, jnp.bitwise_and(acc, LANE - 1).astype(F32).astype(BF16))
    idx_ref[0] = (hi[0:1, :] * float(LANE) + lo[0:1, :]).astype(jnp.int32) - 1


SLOT_WIN = 2 * LANE


def _slot_index_win_kernel(base_ref, slot_ref, idx_ref, acc_s, *, cap):
    i = pl.program_id(0)
    n = slot_ref.shape[2]
    acc_s[...] = jnp.zeros(acc_s.shape, jnp.int32)
    srow = lax.broadcasted_iota(jnp.int32, (SLOT_WIN, LANE), 0)
    lane = lax.broadcasted_iota(jnp.int32, (SLOT_WIN, LANE), 1)
    for cblk in range(n // LANE):
        base = pl.multiple_of(base_ref[i, cblk], LANE)
        s_c = slot_ref[0, :, cblk * LANE:(cblk + 1) * LANE]
        rows = pl.ds(base, SLOT_WIN)
        acc_s[rows, :] = acc_s[rows, :] + jnp.where(srow + base == s_c, lane + (cblk * LANE + 1), 0)
    acc = acc_s[0:cap, :]
    ones = jnp.ones((8, LANE), BF16)
    hi = _dot_nt(ones, lax.shift_right_logical(acc, 7).astype(F32).astype(BF16))
    lo = _dot_nt(ones, jnp.bitwise_and(acc, LANE - 1).astype(F32).astype(BF16))
    idx_ref[0] = (hi[0:1, :] * float(LANE) + lo[0:1, :]).astype(jnp.int32) - 1


def moe_slot_index_windowed(slot, base, cap):
    bsz, ne, n = slot.shape
    idx = pl.pallas_call(
        functools.partial(_slot_index_win_kernel, cap=cap),
        grid_spec=pltpu.PrefetchScalarGridSpec(
            num_scalar_prefetch=1,
            grid=(bsz * ne,),
            in_specs=[pl.BlockSpec((1, 1, n), lambda i, base_ref: (i, 0, 0))],
            out_specs=pl.BlockSpec((1, 1, cap), lambda i, base_ref: (i, 0, 0)),
            scratch_shapes=[pltpu.VMEM((cap + LANE, LANE), jnp.int32)],
        ),
        out_shape=jax.ShapeDtypeStruct((bsz * ne, 1, cap), jnp.int32),
        compiler_params=pltpu.CompilerParams(dimension_semantics=("arbitrary",)),
        name="moe_slot_index",
    )(base, slot.reshape(bsz * ne, 1, n))
    return idx.reshape(bsz, ne, cap)


def moe_slot_index(slot, cap):
    bsz, ne, n = slot.shape
    idx = pl.pallas_call(
        functools.partial(_slot_index_kernel, cap=cap),
        grid=(bsz * ne,),
        in_specs=[pl.BlockSpec((1, 1, n), lambda i: (i, 0, 0))],
        out_specs=pl.BlockSpec((1, 1, cap), lambda i: (i, 0, 0)),
        out_shape=jax.ShapeDtypeStruct((bsz * ne, 1, cap), jnp.int32),
        compiler_params=pltpu.CompilerParams(dimension_semantics=("parallel",)),
        name="moe_slot_index",
    )(slot.reshape(bsz * ne, 1, n))
    return idx.reshape(bsz, ne, cap)


WIN_ALIGN = 16
WIN_FAST = 128


def _combine_kernel(ws_ref, slot_ref, aff_ref, *rest, win, final):
    ye_refs, (x_ref, gate_ref), o_ref = rest[:N_EXPERTS], rest[N_EXPERTS:N_EXPERTS + 2], rest[-1]
    b = pl.program_id(0)
    j = pl.program_id(1)
    srow = lax.broadcasted_iota(jnp.int32, (win, TT), 0)
    his, los = [], []
    for e in range(N_EXPERTS):
        sel = jnp.where(srow + ws_ref[b, e, j] == slot_ref[0, e:e + 1, :], aff_ref[0, e:e + 1, :], 0.0).T
        hi = sel.astype(BF16)
        his.append(hi)
        los.append((sel - hi.astype(F32)).astype(BF16))
    ye = jnp.concatenate([r[...] for r in ye_refs], axis=0)
    acc = _dot(jnp.concatenate(his, axis=1), ye) + _dot(jnp.concatenate(los, axis=1), ye)
    y = x_ref[0] + gate_ref[0, 0] * acc
    if final:
        g_ref = rest[N_EXPERTS + 2]
        y = y * lax.rsqrt(jnp.mean(y * y, axis=-1, keepdims=True) + 1e-6) * g_ref[...]
    o_ref[0] = y


def moe_combine(ws, slot, aff, ye, xa, gate, seg, tile0, win, final_g=None):
    bsz, ne, n = slot.shape
    nt = n // TT
    d = xa.shape[2]

    def ye_spec(e):
        return pl.BlockSpec((pl.Squeezed(), pl.Squeezed(), pl.Element(win), pl.Element(d)),
                            lambda i, j, ws_ref: (i, e, pl.multiple_of(ws_ref[i, e, j], WIN_ALIGN), 0))

    in_specs = ([pl.BlockSpec((1, ne, TT), lambda i, j, ws_ref: (i, 0, j)),
                 pl.BlockSpec((1, ne, TT), lambda i, j, ws_ref: (i, 0, j))]
                + [ye_spec(e) for e in range(ne)]
                + [pl.BlockSpec((1, TT, d), lambda i, j, ws_ref: (i, j + tile0, 0)),
                   pl.BlockSpec((1, 1, 1, d), lambda i, j, ws_ref: (i, seg, 0, 0))])
    args = (ws, slot, aff, *([ye] * ne), xa, gate)
    if final_g is None:
        out_idx, out_shape, aliases = (lambda i, j, ws_ref: (i, j + tile0, 0)), xa.shape, {3 + ne: 0}
    else:
        out_idx, out_shape, aliases = (lambda i, j, ws_ref: (i, j, 0)), (bsz, n, d), {}
        in_specs.append(pl.BlockSpec((1, d), lambda i, j, ws_ref: (0, 0)))
        args += (final_g.reshape(1, d),)
    return pl.pallas_call(
        functools.partial(_combine_kernel, win=win, final=final_g is not None),
        grid_spec=pltpu.PrefetchScalarGridSpec(
            num_scalar_prefetch=1,
            grid=(bsz, nt),
            in_specs=in_specs,
            out_specs=pl.BlockSpec((1, TT, d), out_idx),
        ),
        out_shape=jax.ShapeDtypeStruct(out_shape, F32),
        input_output_aliases=aliases,
        compiler_params=pltpu.CompilerParams(
            dimension_semantics=("parallel", "parallel"), vmem_limit_bytes=VMEM_LIMIT),
        name="moe_combine",
    )(*args)


def _expert_ffn_kernel(x_ref, w1_ref, w3_ref, w2_ref, o_ref):
    x = x_ref[0, 0].astype(BF16)
    a = _dot(x, w1_ref[0])
    g = _dot(x, w3_ref[0])
    o_ref[0, 0] = _dot((_silu(a) * g).astype(BF16), w2_ref[0]).astype(BF16)


def expert_ffn(xe, w1, w3, w2):
    bsz, ne, cap, d = xe.shape
    f = w1.shape[2]
    tm = min(cap, 512)
    return pl.pallas_call(
        _expert_ffn_kernel,
        grid=(ne, bsz, cap // tm),
        in_specs=[
            pl.BlockSpec((1, 1, tm, d), lambda e, i, j: (i, e, j, 0)),
            pl.BlockSpec((1, d, f), lambda e, i, j: (e, 0, 0)),
            pl.BlockSpec((1, d, f), lambda e, i, j: (e, 0, 0)),
            pl.BlockSpec((1, f, d), lambda e, i, j: (e, 0, 0)),
        ],
        out_specs=pl.BlockSpec((1, 1, tm, d), lambda e, i, j: (i, e, j, 0)),
        out_shape=jax.ShapeDtypeStruct((bsz, ne, cap, d), BF16),
        compiler_params=pltpu.CompilerParams(
            dimension_semantics=("parallel", "parallel", "parallel"), vmem_limit_bytes=VMEM_LIMIT),
        name="expert_ffn",
    )(xe, w1, w3, w2)


def _rmsnorm_kernel(x_ref, g_ref, o_ref):
    x = x_ref[0]
    ms = jnp.mean(x * x, axis=-1, keepdims=True)
    o_ref[0] = x * lax.rsqrt(ms + 1e-6) * g_ref[...]


def rmsnorm_rows(x, g, tile0):
    bsz, t, d = x.shape
    nt = t // TT - tile0
    return pl.pallas_call(
        _rmsnorm_kernel,
        grid=(bsz, nt),
        in_specs=[pl.BlockSpec((1, TT, d), lambda i, j: (i, j + tile0, 0)), pl.BlockSpec((1, d), lambda i, j: (0, 0))],
        out_specs=pl.BlockSpec((1, TT, d), lambda i, j: (i, j, 0)),
        out_shape=jax.ShapeDtypeStruct((bsz, nt * TT, d), F32),
        compiler_params=pltpu.CompilerParams(dimension_semantics=("parallel", "parallel")),
        name="final_rmsnorm",
    )(x, g.reshape(1, d))


def _ab_in_layout(w_in, b_in):
    q, k, v, r, glr, z, xs, bm, cm, dt = _split_cols(
        jnp.concatenate([w_in, b_in[None]], axis=0),
        (GLA_QK, GLA_QK, GLA_V, GLA_V, 2 * GLA_GATE_RANK, SSD_INNER, SSD_INNER, SSD_BC, SSD_BC, 2 * SSD_HEADS))
    wb = _pad_cols(jnp.concatenate([q, k, v, r, z, xs, bm, cm, glr, dt], axis=1), AB_N)
    return wb[:-1].astype(BF16), wb[-1]


def _gla_gate_params(w_gate2, b_gate2):
    wg = jnp.zeros((2, LANE, GLA_QK), F32)
    for d in range(2):
        wg = wg.at[d, d * GLA_GATE_RANK:(d + 1) * GLA_GATE_RANK, :].set(w_gate2[d])
    return wg.astype(BF16), b_gate2.reshape(2, 1, GLA_QK)


def _ssd_params(dt_bias, a_log):
    dtb = jnp.zeros((2, 1, LANE), F32)
    nega = jnp.zeros((2, 1, LANE), F32)
    e = np.zeros((2, LANE, SSD_INNER), np.float32)
    for d in range(2):
        c0 = SM_DT + d * SSD_HEADS
        dtb = dtb.at[d, 0, c0:c0 + SSD_HEADS].set(dt_bias[d])
        nega = nega.at[d, 0, c0:c0 + SSD_HEADS].set(-jnp.exp(a_log[d]))
        for h in range(SSD_HEADS):
            e[d, c0 + h, h * SSD_HEADDIM:(h + 1) * SSD_HEADDIM] = 1.0
    return dtb, nega, jnp.asarray(e, BF16)


def _gdn_params(dt_bias, a_log):
    n = 2 * GDN_HEADS
    dtb = jnp.zeros((1, LANE), F32).at[0, 0:n].set(dt_bias.reshape(n))
    nega = jnp.zeros((1, LANE), F32).at[0, 0:n].set(-jnp.exp(a_log.reshape(n)))
    return dtb, nega


def _rmsnorm(x, g, eps=1e-6):
    return x * lax.rsqrt(jnp.mean(jnp.square(x), axis=-1, keepdims=True) + eps) * g


def _head_rmsnorm(x, g):
    return _rmsnorm(x, g.reshape(x.shape[-2:]))


def _layernorm(x, g, b, eps=1e-5):
    mu = jnp.mean(x, axis=-1, keepdims=True)
    var = jnp.mean(jnp.square(x - mu), axis=-1, keepdims=True)
    return (x - mu) * lax.rsqrt(var + eps) * g + b


def _l2norm(x, eps=1e-6):
    return x * lax.rsqrt(jnp.sum(jnp.square(x), axis=-1, keepdims=True) + eps)


def _dwconv(x, w, b=None):
    k, ch = w.shape
    pad = (k - 1) // 2
    y = lax.conv_general_dilated(x, w[:, None, :], (1,), [(pad, pad)],
                                 dimension_numbers=('NWC', 'WIO', 'NWC'), feature_group_count=ch)
    return y if b is None else y + b


def _gdn_scan(q, k, v, beta, logg, s0, with_out):
    bsz, nh, t, dk = q.shape
    dv = v.shape[-1]
    nc = t // CHUNK
    chunks = lambda z: z.reshape(bsz, nh, nc, CHUNK, *z.shape[3:])
    q, k, v, beta, logg = (chunks(z) for z in (q, k, v, beta, logg))
    gc = jnp.cumsum(logg, axis=-1)
    glast = gc[..., -1]
    tril = jnp.tril(jnp.ones((CHUNK, CHUNK), dtype=bool))
    strict = jnp.tril(jnp.ones((CHUNK, CHUNK), dtype=bool), k=-1)
    decay = jnp.exp(jnp.where(tril, gc[..., :, None] - gc[..., None, :], -jnp.inf))
    kb = k * beta[..., None]
    m = jnp.eye(CHUNK, dtype=k.dtype) + jnp.where(strict, jnp.einsum('bhcid,bhcjd->bhcij', kb, k) * decay, 0.0)
    rhs = jnp.concatenate([v * beta[..., None], kb * jnp.exp(gc)[..., None]], axis=-1)
    sol = lax.linalg.triangular_solve(m, rhs, left_side=True, lower=True, unit_diagonal=True)
    u, w = sol[..., :dv], sol[..., dv:]
    kd = k * jnp.exp(glast[..., None] - gc)[..., None]
    dec = jnp.exp(glast)
    front = lambda z: jnp.moveaxis(z, 2, 0)

    def advance(s, w_c, u_c, kd_c, dec_c):
        vn = u_c - jnp.einsum('bhid,bhde->bhie', w_c, s)
        return vn, dec_c[..., None, None] * s + jnp.einsum('bhjd,bhje->bhde', kd_c, vn)

    xs = (front(w), front(u), front(kd), front(dec))
    if not with_out:
        s_fin, _ = lax.scan(lambda s, xc: (advance(s, *xc)[1], None), s0, xs)
        return None, s_fin
    aqk = jnp.einsum('bhcid,bhcjd->bhcij', q, k) * decay
    qg = q * jnp.exp(gc)[..., None]

    def step(s, xc):
        w_c, u_c, kd_c, dec_c, aqk_c, qg_c = xc
        vn, s_new = advance(s, w_c, u_c, kd_c, dec_c)
        o = jnp.einsum('bhid,bhde->bhie', qg_c, s) + jnp.einsum('bhij,bhje->bhie', aqk_c, vn)
        return s_new, o

    s_fin, o = lax.scan(step, s0, xs + (front(aqk), front(qg)))
    return jnp.moveaxis(o, 0, 2).reshape(bsz, nh, t, dv), s_fin


def _bidir_scan(scan_fn, args_f, args_b, t_axis, init_f, init_b, with_out):
    flip = lambda a: jnp.flip(a, axis=t_axis)
    o_f, s_f = scan_fn(*args_f, init_f, with_out)
    o_b, s_b = scan_fn(*[flip(a) for a in args_b], init_b, with_out)
    o = o_f + flip(o_b) if with_out else None
    return o, s_f, s_b


def _conformer_conv(u, rows, dw_w, dw_b, ln_g, ln_b):
    bsz, t, ch = u.shape
    y = _dwconv(u.reshape(bsz * rows, t // rows, ch), dw_w, dw_b).reshape(bsz, t, ch)
    return jax.nn.silu(_layernorm(y, ln_g, ln_b))


def _cd_stream(proj, gdn_conv_w, gdn_a_log, gdn_dt_bias):
    bsz, t, _ = proj.shape
    ga, gb, q, k, v, og, a_raw, b_raw = _split_cols(proj, CD_SPLITS)
    glu = ga * jax.nn.sigmoid(gb)
    qkv = jax.nn.silu(_dwconv(jnp.concatenate([q, k, v], axis=-1), gdn_conv_w))
    q, k, v = _split_cols(qkv, (GDN_QK, GDN_QK, GDN_V))
    heads = lambda a: a.reshape(bsz, t, GDN_HEADS, -1).transpose(0, 2, 1, 3)
    q = _l2norm(heads(q)) * GDN_DK ** -0.5
    k = _l2norm(heads(k))
    v = heads(v)
    beta = jax.nn.sigmoid(b_raw.reshape(bsz, t, 2, GDN_HEADS)).transpose(2, 0, 3, 1)
    logg = (-jnp.exp(gdn_a_log)
            * jax.nn.softplus(a_raw.reshape(bsz, t, 2, GDN_HEADS) + gdn_dt_bias)).transpose(2, 0, 3, 1)
    return (q, k, v, beta[0], logg[0]), (q, k, v, beta[1], logg[1]), glu, og


def _mixer_cd(proj_l, proj_c, rows, conf_dw_w, conf_dw_b, conf_ln_g, conf_ln_b,
              gdn_conv_w, gdn_a_log, gdn_dt_bias, gdn_norm_g, need_ctx):
    c_f, c_b, c_glu, c_og = _cd_stream(proj_c, gdn_conv_w, gdn_a_log, gdn_dt_bias)
    l_f, l_b, l_glu, l_og = _cd_stream(proj_l, gdn_conv_w, gdn_a_log, gdn_dt_bias)
    bsz = proj_l.shape[0]
    z0 = jnp.zeros((bsz, GDN_HEADS, GDN_DK, GDN_DV), F32)
    od_c, s_f, s_b = _bidir_scan(_gdn_scan, c_f, c_b, 2, z0, z0, need_ctx)
    od_l, _, _ = _bidir_scan(_gdn_scan, l_f, l_b, 2, s_f, s_b, True)

    def mix(od, glu, og, n_rows):
        t = glu.shape[1]
        conv = _conformer_conv(glu, n_rows, conf_dw_w, conf_dw_b, conf_ln_g, conf_ln_b)
        o = _head_rmsnorm(jnp.swapaxes(od, 1, 2), gdn_norm_g) * jax.nn.silu(og).reshape(bsz, t, GDN_HEADS, GDN_DV)
        return jnp.concatenate([conv, o.reshape(bsz, t, GDN_V)], axis=-1)

    return mix(od_l, l_glu, l_og, rows), (mix(od_c, c_glu, c_og, 1) if need_ctx else None)


def moe_segment(i, xa, mods, p, w1, w3, w2, seg, tile0, ntiles, final_g=None):
    bsz, _, d = xa.shape
    n = ntiles * TT
    cap = n * EC_CAPACITY // N_EXPERTS
    rw = _pad_cols(p["moe_router"][i], LANE).astype(BF16)
    h, aff = moe_router(xa, p["norm2_g"][i], mods[:, :, 3], mods[:, :, 4], rw, seg, tile0, ntiles)
    slot = moe_select(aff, cap)
    cnt128 = jnp.sum((slot >= 0).reshape(bsz, N_EXPERTS, n // LANE, LANE), axis=-1, dtype=jnp.int32)
    start128 = jnp.cumsum(cnt128, axis=-1) - cnt128
    if cap >= SLOT_WIN:
        base = jnp.minimum(start128 // LANE * LANE, cap - LANE).reshape(bsz * N_EXPERTS, n // LANE)
        idx = moe_slot_index_windowed(slot, base, cap)
    else:
        idx = moe_slot_index(slot, cap)
    idx = idx.reshape(bsz, N_EXPERTS * cap)
    xe = jnp.take_along_axis(h, idx[..., None], axis=1).reshape(bsz, N_EXPERTS, cap, d)
    ye = expert_ffn(xe, w1, w3, w2)
    per_tile = TT // LANE
    counts = jnp.sum(cnt128.reshape(bsz, N_EXPERTS, ntiles, per_tile), axis=-1)
    starts = start128[:, :, ::per_tile]
    aligned = starts // WIN_ALIGN * WIN_ALIGN

    def run(win):
        ws = jnp.minimum(aligned, cap - win)
        return moe_combine(ws, slot, aff, ye, xa, mods[:, :, 5], seg, tile0, win, final_g)

    win_fast, win_full = min(cap, WIN_FAST), min(cap, TT + WIN_ALIGN)
    if win_fast == win_full:
        return run(win_full)
    overflow = jnp.any(starts + counts - jnp.minimum(aligned, cap - win_fast) > win_fast)
    return lax.cond(overflow, lambda: run(win_full), lambda: run(win_fast))


def layer_mixer(i, j, xa, mods, p, last):
    sh1, sc1, g1 = (mods[:, :, s] for s in range(3))
    if i % 2 == 0:
        w_in, b_in = _ab_in_layout(p["ab_w_in"][j], p["ab_b_in"][j])
        proj = norm_proj(xa, p["norm1_g"][i], sh1, sc1, w_in, b_in, 1920)
        xbc = ab_prep(proj, p["ssd_conv_w"][j], p["ssd_conv_b"][j])
        wg, bg = _gla_gate_params(p["gla_w_gate2"][j], p["gla_b_gate2"][j])
        o_gla = gla_scan(proj, wg, bg)
        y_ssd = ssd_scan(xbc, proj, *_ssd_params(p["ssd_dt_bias"][j], p["ssd_a_log"][j]))
        return ab_out(o_gla, y_ssd, proj, xbc, p["gla_norm_g"][j], jnp.repeat(p["ssd_d"][j], SSD_HEADDIM),
                      p["ssd_norm_g"][j], p["ab_w_out"][j].astype(BF16), p["ab_b_out"][j], xa, g1)
    w_in = _pad_cols(p["cd_w_in"][j], CD_N).astype(BF16)
    b_in = _pad_cols(p["cd_b_in"][j], CD_N)
    proj = norm_proj(xa, p["norm1_g"][i], sh1, sc1, w_in, b_in, 1280)
    qkv = cd_prep(proj, p["gdn_conv_w"][j])
    o_gdn = gdn_scan(qkv, proj, *_gdn_params(p["gdn_dt_bias"][j], p["gdn_a_log"][j]))
    return cd_out(o_gdn, proj, p["conf_dw_w"][j], p["conf_dw_b"][j], p["conf_ln_g"][j], p["conf_ln_b"][j],
                  p["gdn_norm_g"][j], p["cd_w_out"][j].astype(BF16), p["cd_b_out"][j], xa, g1)


def kernel(x, c, ctx, c_ctx, mod_w, mod_b, norm1_g, norm2_g, ab_w_in, ab_b_in, ab_w_out, ab_b_out, gla_w_gate2, gla_b_gate2, gla_norm_g, ssd_conv_w, ssd_conv_b, ssd_dt_bias, ssd_a_log, ssd_d, ssd_norm_g, cd_w_in, cd_b_in, cd_w_out, cd_b_out, conf_dw_w, conf_dw_b, conf_ln_g, conf_ln_b, gdn_conv_w, gdn_a_log, gdn_dt_bias, gdn_norm_g, moe_router, moe_w1, moe_w3, moe_w2, final_norm_g):
    p = dict(locals())
    bsz, seq, d = x.shape
    assert ctx.shape[1] == TT and seq % TT == 0
    depth = mod_w.shape[0]
    xa = jnp.concatenate([ctx, x], axis=1)
    for i in range(depth):
        last = i == depth - 1
        mod_l = jax.nn.silu(c) @ mod_w[i] + mod_b[i]
        mod_c = jnp.broadcast_to(jax.nn.silu(c_ctx) @ mod_w[i] + mod_b[i], mod_l.shape)
        mods = jnp.stack([mod_c, mod_l], axis=1).reshape(bsz, 2, 6, 1, d)
        sh1, sc1, g1, sh2, sc2, g2 = (mods[:, :, s] for s in range(6))
        xa = layer_mixer(i, i // 2, xa, mods, p, last)
        w1, w3, w2 = moe_w1[i].astype(BF16), moe_w3[i].astype(BF16), moe_w2[i].astype(BF16)
        if last:
            return moe_segment(i, xa, mods, p, w1, w3, w2, 1, 1, seq // TT, final_norm_g)
        xa = moe_segment(i, xa, mods, p, w1, w3, w2, 1, 1, seq // TT)
        xa = moe_segment(i, xa, mods, p, w1, w3, w2, 0, 0, 1)
```

```python
import functools

import jax
import jax.numpy as jnp
import numpy as np
from jax import lax
from jax.experimental import pallas as pl
from jax.experimental.pallas import tpu as pltpu

F32 = jnp.float32
BF16 = jnp.bfloat16

D_MODEL = 1024
GRID_W = 64
CHUNK = 64
GLA_HEADS, GLA_DK, GLA_DV, GLA_GATE_RANK, GLA_GATE_TAU = 4, 128, 256, 16, 16.0
SSD_HEADS, SSD_HEADDIM, SSD_STATE, SSD_GROUPS = 16, 64, 128, 2
CONF_CH, CONF_KERNEL = D_MODEL, 31
GDN_HEADS, GDN_DK, GDN_DV = 8, 128, 128
N_EXPERTS, EC_CAPACITY, EXPERT_FF = 16, 2, D_MODEL

GLA_QK = GLA_HEADS * GLA_DK
GLA_V = GLA_HEADS * GLA_DV
SSD_INNER = SSD_HEADS * SSD_HEADDIM
SSD_BC = SSD_GROUPS * SSD_STATE
SSD_HPG = SSD_HEADS // SSD_GROUPS
GDN_QK = GDN_HEADS * GDN_DK
GDN_V = GDN_HEADS * GDN_DV
CD_SPLITS = (CONF_CH, CONF_CH, GDN_QK, GDN_QK, GDN_V, GDN_V, 2 * GDN_HEADS, 2 * GDN_HEADS)

LANE = 128
SUBLANES = 8
TT = 256
CPT = TT // CHUNK
VMEM_LIMIT = 48 * 1024 * 1024

AB_Q, AB_K, AB_V, AB_R, AB_Z, AB_XS, AB_BM, AB_CM, AB_SMALL = 0, 512, 1024, 2048, 3072, 4096, 5120, 5376, 5632
AB_N = AB_SMALL + LANE
SM_DT = 2 * GLA_GATE_RANK
CD_N = 6400


def _split_cols(a, sizes):
    return jnp.split(a, np.cumsum(sizes)[:-1].tolist(), axis=-1)


def _pad_cols(a, n):
    return jnp.pad(a, [(0, 0)] * (a.ndim - 1) + [(0, n - a.shape[-1])])


def _dot(a, b):
    return jnp.dot(a, b, preferred_element_type=F32)


def _dot_nt(a, b):
    return lax.dot_general(a, b, (((1,), (1,)), ((), ())), preferred_element_type=F32)


def _split3(x):
    hi = x.astype(BF16)
    r = x - hi.astype(F32)
    mid = r.astype(BF16)
    lo = (r - mid.astype(F32)).astype(BF16)
    return hi, mid, lo


def _sel_dot(m, x):
    hi, mid, lo = _split3(x)
    return _dot(m, hi) + _dot(m, mid) + _dot(m, lo)


def _dot_sel(x, e):
    hi, mid, lo = _split3(x)
    return _dot(hi, e) + _dot(mid, e) + _dot(lo, e)


def _softplus(x):
    return jnp.maximum(x, 0.0) + jnp.log(1.0 + jnp.exp(-jnp.abs(x)))


def _silu(x):
    return x * jax.nn.sigmoid(x)


def _chunk_masks(is_fwd, n):
    r = lax.broadcasted_iota(jnp.int32, (n, n), 0)
    c = lax.broadcasted_iota(jnp.int32, (n, n), 1)
    same = lax.shift_right_logical(r, 6) == lax.shift_right_logical(c, 6)
    lo = jnp.where(is_fwd, c, r)
    hi = jnp.where(is_fwd, r, c)
    cum = jnp.logical_and(same, lo <= hi)
    return jnp.where(cum, 1.0, 0.0).astype(BF16), jnp.where(same, 1.0, 0.0).astype(BF16)


def _causal_mask(is_fwd, n):
    r = lax.broadcasted_iota(jnp.int32, (n, n), 0)
    c = lax.broadcasted_iota(jnp.int32, (n, n), 1)
    return jnp.where(is_fwd, c, r) <= jnp.where(is_fwd, r, c)


def _scan_tile(d, j, nt):
    return jnp.where(d == 0, j, jnp.where(j == 0, 0, nt - j))


NP_TM = 3 * TT


def _norm_proj_kernel(x_ref, g_ref, sh_ref, sc_ref, w_ref, b_ref, o_ref):
    x = x_ref[0]
    row = lax.broadcasted_iota(jnp.int32, (NP_TM, 1), 0) + pl.program_id(2) * NP_TM
    is_ctx = row < TT
    scale = jnp.where(is_ctx, sc_ref[0, 0], sc_ref[0, 1])
    shift = jnp.where(is_ctx, sh_ref[0, 0], sh_ref[0, 1])
    ms = jnp.mean(x * x, axis=-1, keepdims=True)
    h = (x * lax.rsqrt(ms + 1e-6) * g_ref[...] * (1.0 + scale) + shift).astype(BF16)
    o_ref[0] = _dot(h, w_ref[...]) + b_ref[...]


def norm_proj(x, g, shift, scale, w, b, tn):
    bsz, t, d = x.shape
    n = w.shape[1]
    assert t % NP_TM == 0
    return pl.pallas_call(
        _norm_proj_kernel,
        grid=(n // tn, bsz, t // NP_TM),
        in_specs=[
            pl.BlockSpec((1, NP_TM, d), lambda k, i, j: (i, j, 0)),
            pl.BlockSpec((1, d), lambda k, i, j: (0, 0)),
            pl.BlockSpec((1, 2, 1, d), lambda k, i, j: (i, 0, 0, 0)),
            pl.BlockSpec((1, 2, 1, d), lambda k, i, j: (i, 0, 0, 0)),
            pl.BlockSpec((d, tn), lambda k, i, j: (0, k)),
            pl.BlockSpec((1, tn), lambda k, i, j: (0, k)),
        ],
        out_specs=pl.BlockSpec((1, NP_TM, tn), lambda k, i, j: (i, j, k)),
        out_shape=jax.ShapeDtypeStruct((bsz, t, n), F32),
        compiler_params=pltpu.CompilerParams(
            dimension_semantics=("parallel", "parallel", "parallel"), vmem_limit_bytes=VMEM_LIMIT),
        name="norm_proj",
    )(x, g.reshape(1, d), shift, scale, w, b.reshape(1, n))


def _conv3_piece(x, lh, rh, w, b, left_ok, right_ok):
    n = x.shape[0]
    row = lax.broadcasted_iota(jnp.int32, x.shape, 0)
    prev_row = jnp.where(left_ok, lh[7:8, :], 0.0)
    next_row = jnp.where(right_ok, rh[0:1, :], 0.0)
    x_prev = jnp.where(row == 0, prev_row, pltpu.roll(x, 1, 0))
    x_next = jnp.where(row == n - 1, next_row, pltpu.roll(x, n - 1, 0))
    return _silu(w[0:1, :] * x_prev + w[1:2, :] * x + w[2:3, :] * x_next + b)


def _ab_prep_kernel(xs_ref, bm_ref, cm_ref, xsl_ref, bml_ref, cml_ref, xsr_ref, bmr_ref, cmr_ref,
                    w_ref, b_ref, o_ref):
    j = pl.program_id(1)
    nt = pl.num_programs(1)
    left_ok = j >= 2
    right_ok = jnp.logical_and(j >= 1, j < nt - 1)
    w = w_ref[...]
    b = b_ref[...]
    o_ref[0, :, 0:SSD_INNER] = _conv3_piece(xs_ref[0], xsl_ref[0], xsr_ref[0], w[:, 0:SSD_INNER],
                                            b[:, 0:SSD_INNER], left_ok, right_ok)
    c0, c1 = SSD_INNER, SSD_INNER + SSD_BC
    o_ref[0, :, c0:c1] = _conv3_piece(bm_ref[0], bml_ref[0], bmr_ref[0], w[:, c0:c1], b[:, c0:c1], left_ok, right_ok)
    c0, c1 = c1, c1 + SSD_BC
    o_ref[0, :, c0:c1] = _conv3_piece(cm_ref[0], cml_ref[0], cmr_ref[0], w[:, c0:c1], b[:, c0:c1], left_ok, right_ok)


def ab_prep(proj, conv_w, conv_b):
    bsz, t, _ = proj.shape
    nt = t // TT
    rb = TT // 8
    nrb = t // 8
    cw = SSD_INNER + 2 * SSD_BC

    def cur(width, col):
        return pl.BlockSpec((1, TT, width), lambda i, j: (i, j, col // width))

    def left(width, col):
        return pl.BlockSpec((1, 8, width), lambda i, j: (i, jnp.maximum(j * rb - 1, 0), col // width))

    def right(width, col):
        return pl.BlockSpec((1, 8, width), lambda i, j: (i, jnp.minimum((j + 1) * rb, nrb - 1), col // width))

    pieces = ((SSD_INNER, AB_XS), (SSD_BC, AB_BM), (SSD_BC, AB_CM))
    return pl.pallas_call(
        _ab_prep_kernel,
        grid=(bsz, nt),
        in_specs=[cur(*p) for p in pieces] + [left(*p) for p in pieces] + [right(*p) for p in pieces] + [
            pl.BlockSpec((3, cw), lambda i, j: (0, 0)),
            pl.BlockSpec((1, cw), lambda i, j: (0, 0)),
        ],
        out_specs=pl.BlockSpec((1, TT, cw), lambda i, j: (i, j, 0)),
        out_shape=jax.ShapeDtypeStruct((bsz, t, cw), F32),
        compiler_params=pltpu.CompilerParams(dimension_semantics=("parallel", "parallel")),
        name="ab_prep",
    )(*([proj] * 9), conv_w, conv_b.reshape(1, cw))


def _gla_kernel(q_ref, k_ref, v_ref, sm_ref, wg_ref, bg_ref, o_ref, qg_s, egl_s, oi_s, u_s, st_s):
    d = pl.program_id(1)
    j = pl.program_id(2)
    is_fwd = d == 0

    @pl.when(j == 0)
    def _():
        st_s[...] = jnp.zeros_like(st_s)

    gz = _dot(sm_ref[0].astype(BF16), wg_ref[0]) + bg_ref[0]
    logg = (jnp.minimum(gz, 0.0) - jnp.log(1.0 + jnp.exp(-jnp.abs(gz)))) * (1.0 / GLA_GATE_TAU)
    m_cum, m_all = _chunk_masks(is_fwd, TT)
    gc = _sel_dot(m_cum, logg)
    gl = _sel_dot(m_all, logg)
    q = q_ref[0] * (GLA_DK ** -0.5)
    k = k_ref[0]
    qg = (q * jnp.exp(gc)).astype(BF16)
    kn = (k * jnp.exp(-gc)).astype(BF16)
    kd = (k * jnp.exp(gl - gc)).astype(BF16)
    qg_s[...] = qg
    egl_s[...] = jnp.exp(gl)
    causal = _causal_mask(is_fwd, CHUNK)

    pairs = [(h, ci) for h in range(GLA_HEADS) for ci in range(CPT)]
    rows_of = lambda ci: slice(ci * CHUNK, (ci + 1) * CHUNK)
    kcols = lambda h: slice(h * GLA_DK, (h + 1) * GLA_DK)
    vcols = lambda h: slice(h * GLA_DV, (h + 1) * GLA_DV)
    vs = [v_ref[0, rows_of(ci), vcols(h)] for h, ci in pairs]
    atts = [jnp.where(causal, _dot_nt(qg[rows_of(ci), kcols(h)], kn[rows_of(ci), kcols(h)]), 0.0).astype(BF16)
            for h, ci in pairs]
    for n, (h, ci) in enumerate(pairs):
        oi_s[rows_of(ci), vcols(h)] = _dot(atts[n], vs[n].astype(BF16))
        u_s[n] = _dot(vs[n].T.astype(BF16), kd[rows_of(ci), kcols(h)])

    for ci in range(CPT):
        cidx = jnp.where(is_fwd, ci, CPT - 1 - ci)
        off = pl.multiple_of(cidx * CHUNK, CHUNK)
        rows = pl.ds(off, CHUNK)
        for h in range(GLA_HEADS):
            st = st_s[h]
            o_ref[0, 0, rows, vcols(h)] = oi_s[rows, vcols(h)] + _dot_nt(qg_s[rows, kcols(h)], st.astype(BF16))
            st_s[h] = st * egl_s[pl.ds(off, 1), kcols(h)] + u_s[h * CPT + cidx]


def gla_scan(proj, wg, bg):
    bsz, t, _ = proj.shape
    nt = t // TT
    tile = lambda d, j: _scan_tile(d, j, nt)
    return pl.pallas_call(
        _gla_kernel,
        grid=(bsz, 2, nt),
        in_specs=[
            pl.BlockSpec((1, TT, GLA_QK), lambda i, d, j: (i, tile(d, j), AB_Q // GLA_QK)),
            pl.BlockSpec((1, TT, GLA_QK), lambda i, d, j: (i, tile(d, j), AB_K // GLA_QK)),
            pl.BlockSpec((1, TT, GLA_V), lambda i, d, j: (i, tile(d, j), AB_V // GLA_V)),
            pl.BlockSpec((1, TT, LANE), lambda i, d, j: (i, tile(d, j), AB_SMALL // LANE)),
            pl.BlockSpec((1, LANE, GLA_QK), lambda i, d, j: (d, 0, 0)),
            pl.BlockSpec((1, 1, GLA_QK), lambda i, d, j: (d, 0, 0)),
        ],
        out_specs=pl.BlockSpec((1, 1, TT, GLA_V), lambda i, d, j: (d, i, tile(d, j), 0)),
        out_shape=jax.ShapeDtypeStruct((2, bsz, t, GLA_V), F32),
        scratch_shapes=[pltpu.VMEM((TT, GLA_QK), BF16), pltpu.VMEM((TT, GLA_QK), F32),
                        pltpu.VMEM((TT, GLA_V), F32), pltpu.VMEM((GLA_HEADS * CPT, GLA_DV, GLA_DK), F32),
                        pltpu.VMEM((GLA_HEADS, GLA_DV, GLA_DK), F32)],
        compiler_params=pltpu.CompilerParams(
            dimension_semantics=("parallel", "parallel", "arbitrary")),
        name="gla_scan",
    )(proj, proj, proj, proj, wg, bg)


def _dot_sel2(x, e):
    hi = x.astype(BF16)
    return _dot(hi, e) + _dot((x - hi.astype(F32)).astype(BF16), e)


def _ssd_kernel(xs_ref, bm_ref, cm_ref, sm_ref, dtb_ref, nega_ref, e_ref, o_ref,
                v_s, vw_s, cdec_s, dec_s, ah_s, st_s):
    d = pl.program_id(1)
    j = pl.program_id(2)
    is_fwd = d == 0
    gw = SSD_HPG * SSD_HEADDIM

    @pl.when(j == 0)
    def _():
        st_s[...] = jnp.zeros_like(st_s)

    dt = _softplus(sm_ref[0] + dtb_ref[0])
    la = dt * nega_ref[0]
    m_cum, m_all = _chunk_masks(is_fwd, TT)
    acum = _sel_dot(m_cum, la)
    atot = _sel_dot(m_all, la)
    e = e_ref[0]
    v = xs_ref[0] * _dot_sel2(dt, e)
    v_s[...] = v.astype(BF16)
    vw_s[...] = (v * _dot_sel2(jnp.exp(atot - acum), e)).astype(BF16)
    cdec_s[...] = _dot_sel2(jnp.exp(acum), e)
    dec_s[...] = _dot_sel2(jnp.exp(atot), e)
    ah_s[...] = pltpu.roll(acum, LANE - SM_DT - d * SSD_HEADS, 1)
    causal = _causal_mask(is_fwd, CHUNK)

    for ci in range(CPT):
        off = pl.multiple_of(jnp.where(is_fwd, ci, CPT - 1 - ci) * CHUNK, CHUNK)
        rows = pl.ds(off, CHUNK)
        ah = ah_s[rows, :]
        aht = ah.T
        for g in range(SSD_GROUPS):
            gc = slice(g * gw, (g + 1) * gw)
            nc = slice(g * SSD_STATE, (g + 1) * SSD_STATE)
            bm = bm_ref[0, rows, nc]
            cm = cm_ref[0, rows, nc].astype(BF16)
            st = st_s[g]
            cb = _dot_nt(cm, bm.astype(BF16))
            y_inter = _dot(cm, st.astype(BF16)) * cdec_s[rows, gc]
            v_c = v_s[rows, gc]
            ys = []
            for hh in range(SSD_HPG):
                h = g * SSD_HPG + hh
                seg = jnp.exp(jnp.where(causal, ah[:, h:h + 1] - aht[h:h + 1, :], -1e30))
                ys.append(_dot((seg * cb).astype(BF16), v_c[:, hh * SSD_HEADDIM:(hh + 1) * SSD_HEADDIM]))
            o_ref[0, 0, rows, gc] = jnp.concatenate(ys, axis=1) + y_inter
            st_s[g] = st * dec_s[pl.ds(off, 1), gc] + _dot(bm.T.astype(BF16), vw_s[rows, gc])


def ssd_scan(xbc, proj, dtb, nega, e):
    bsz, t, _ = xbc.shape
    nt = t // TT
    gw = SSD_HPG * SSD_HEADDIM
    tile = lambda d, j: _scan_tile(d, j, nt)
    return pl.pallas_call(
        _ssd_kernel,
        grid=(bsz, 2, nt),
        in_specs=[
            pl.BlockSpec((1, TT, SSD_INNER), lambda i, d, j: (i, tile(d, j), 0)),
            pl.BlockSpec((1, TT, SSD_BC), lambda i, d, j: (i, tile(d, j), SSD_INNER // SSD_BC)),
            pl.BlockSpec((1, TT, SSD_BC), lambda i, d, j: (i, tile(d, j), SSD_INNER // SSD_BC + 1)),
            pl.BlockSpec((1, TT, LANE), lambda i, d, j: (i, tile(d, j), AB_SMALL // LANE)),
            pl.BlockSpec((1, 1, LANE), lambda i, d, j: (d, 0, 0)),
            pl.BlockSpec((1, 1, LANE), lambda i, d, j: (d, 0, 0)),
            pl.BlockSpec((1, LANE, SSD_INNER), lambda i, d, j: (d, 0, 0)),
        ],
        out_specs=pl.BlockSpec((1, 1, TT, SSD_INNER), lambda i, d, j: (d, i, tile(d, j), 0)),
        out_shape=jax.ShapeDtypeStruct((2, bsz, t, SSD_INNER), F32),
        scratch_shapes=[pltpu.VMEM((TT, SSD_INNER), BF16), pltpu.VMEM((TT, SSD_INNER), BF16),
                        pltpu.VMEM((TT, SSD_INNER), F32), pltpu.VMEM((TT, SSD_INNER), F32),
                        pltpu.VMEM((TT, LANE), F32), pltpu.VMEM((SSD_GROUPS, SSD_STATE, gw), F32)],
        compiler_params=pltpu.CompilerParams(
            dimension_semantics=("parallel", "parallel", "arbitrary")),
        name="ssd_scan",
    )(xbc, xbc, xbc, proj, dtb, nega, e)


def _group_rmsnorm(x, width):
    parts = []
    for s in range(x.shape[1] // width):
        seg = x[:, s * width:(s + 1) * width]
        parts.append(seg * lax.rsqrt(jnp.mean(seg * seg, axis=-1, keepdims=True) + 1e-6))
    return jnp.concatenate(parts, axis=1)


def _ab_out_kernel(of_ref, ob_ref, yf_ref, yb_ref, r_ref, z_ref, xs_ref, gg_ref, dv_ref, sg_ref,
                   w_ref, b_ref, x_ref, gate_ref, o_ref):
    o = _group_rmsnorm(of_ref[0, 0] + ob_ref[0, 0], GLA_DV) * gg_ref[...] * _silu(r_ref[0])
    y = (yf_ref[0, 0] + yb_ref[0, 0] + dv_ref[...] * xs_ref[0]) * _silu(z_ref[0])
    y = _group_rmsnorm(y, SSD_INNER // SSD_GROUPS) * sg_ref[...]
    m = _dot(o.astype(BF16), w_ref[0:GLA_V, :]) + _dot(y.astype(BF16), w_ref[GLA_V:, :]) + b_ref[...]
    o_ref[0] = x_ref[0] + gate_ref[0, 0] * m


def ab_out(o_gla, y_ssd, proj, xbc, gla_g, d_vec, ssd_g, w, b, x, gate):
    bsz, t, d = x.shape
    seg = lambda i, j: (i, jnp.minimum(j, 1), 0, 0)
    row = lambda width: pl.BlockSpec((1, width), lambda i, j: (0, 0))
    return pl.pallas_call(
        _ab_out_kernel,
        grid=(bsz, t // TT),
        in_specs=[
            pl.BlockSpec((1, 1, TT, GLA_V), lambda i, j: (0, i, j, 0)),
            pl.BlockSpec((1, 1, TT, GLA_V), lambda i, j: (1, i, j, 0)),
            pl.BlockSpec((1, 1, TT, SSD_INNER), lambda i, j: (0, i, j, 0)),
            pl.BlockSpec((1, 1, TT, SSD_INNER), lambda i, j: (1, i, j, 0)),
            pl.BlockSpec((1, TT, GLA_V), lambda i, j: (i, j, AB_R // GLA_V)),
            pl.BlockSpec((1, TT, SSD_INNER), lambda i, j: (i, j, AB_Z // SSD_INNER)),
            pl.BlockSpec((1, TT, SSD_INNER), lambda i, j: (i, j, 0)),
            row(GLA_V), row(SSD_INNER), row(SSD_INNER),
            pl.BlockSpec((GLA_V + SSD_INNER, d), lambda i, j: (0, 0)),
            row(d),
            pl.BlockSpec((1, TT, d), lambda i, j: (i, j, 0)),
            pl.BlockSpec((1, 1, 1, d), seg),
        ],
        out_specs=pl.BlockSpec((1, TT, d), lambda i, j: (i, j, 0)),
        out_shape=jax.ShapeDtypeStruct((bsz, t, d), F32),
        compiler_params=pltpu.CompilerParams(
            dimension_semantics=("parallel", "parallel"), vmem_limit_bytes=VMEM_LIMIT),
        name="ab_out",
    )(o_gla, o_gla, y_ssd, y_ssd, proj, proj, xbc, gla_g.reshape(1, -1), d_vec.reshape(1, -1),
      ssd_g.reshape(1, -1), w, b.reshape(1, d), x, gate)


def _cd_prep_kernel(q_ref, k_ref, v_ref, ql_ref, kl_ref, vl_ref, qr_ref, kr_ref, vr_ref, w_ref, o_ref):
    j = pl.program_id(1)
    nt = pl.num_programs(1)
    left_ok = j >= 2
    right_ok = jnp.logical_and(j >= 1, j < nt - 1)
    w = w_ref[...]
    srcs = ((q_ref, ql_ref, qr_ref, GDN_DK ** -0.5), (k_ref, kl_ref, kr_ref, 1.0), (v_ref, vl_ref, vr_ref, None))
    for s, (c_ref, l_ref, r_ref, scale) in enumerate(srcs):
        c0 = s * GDN_QK
        y = _conv3_piece(c_ref[0], l_ref[0], r_ref[0], w[:, c0:c0 + GDN_QK], 0.0, left_ok, right_ok)
        if scale is None:
            o_ref[0, :, c0:c0 + GDN_QK] = y
            continue
        for h in range(GDN_HEADS):
            seg = y[:, h * GDN_DK:(h + 1) * GDN_DK]
            inv = lax.rsqrt(jnp.sum(seg * seg, axis=-1, keepdims=True) + 1e-6) * scale
            o_ref[0, :, c0 + h * GDN_DK:c0 + (h + 1) * GDN_DK] = seg * inv


def cd_prep(proj, conv_w):
    bsz, t, _ = proj.shape
    nt = t // TT
    rb = TT // 8
    nrb = t // 8
    width = GDN_QK
    cols = (2, 3, 4)

    cur = lambda cb: pl.BlockSpec((1, TT, width), lambda i, j: (i, j, cb))
    left = lambda cb: pl.BlockSpec((1, 8, width), lambda i, j: (i, jnp.maximum(j * rb - 1, 0), cb))
    right = lambda cb: pl.BlockSpec((1, 8, width), lambda i, j: (i, jnp.minimum((j + 1) * rb, nrb - 1), cb))
    return pl.pallas_call(
        _cd_prep_kernel,
        grid=(bsz, nt),
        in_specs=[cur(cb) for cb in cols] + [left(cb) for cb in cols] + [right(cb) for cb in cols] + [
            pl.BlockSpec((3, 3 * width), lambda i, j: (0, 0))],
        out_specs=pl.BlockSpec((1, TT, 3 * width), lambda i, j: (i, j, 0)),
        out_shape=jax.ShapeDtypeStruct((bsz, t, 3 * width), F32),
        compiler_params=pltpu.CompilerParams(dimension_semantics=("parallel", "parallel")),
        name="cd_prep",
    )(*([proj] * 9), conv_w)


GDN_HB = 8


def _mm2(a, b):
    return _dot(a.astype(BF16), b.astype(BF16))


def _unit_tri_inverse(mats, b16, b32, eye):
    each = lambda f, *ls: [f(*xs) for xs in zip(*ls)]
    d16 = each(lambda a: jnp.where(b16, a, 0.0), mats)
    d2 = each(lambda x: _mm2(x, x), d16)
    d4 = each(lambda x: _mm2(x, x), d2)
    d8 = each(lambda x: _mm2(x, x), d4)
    t = each(lambda x: eye - x, d16)
    for p in (d2, d4, d8):
        t = each(lambda x, y: x + _mm2(x, y), t, p)
    off32 = jnp.logical_and(b32, jnp.logical_not(b16))
    for sel in (off32, jnp.logical_not(b32)):
        a_off = each(lambda a: jnp.where(sel, a, 0.0), mats)
        inner = each(_mm2, a_off, t)
        t = each(lambda x, y: x - _mm2(x, y), t, inner)
    return t


def _gdn_kernel(q_ref, k_ref, v_ref, sm_ref, dtb_ref, nega_ref, o_ref,
                dec_s, n_s, p_s, oc_s, qp_s, st_s):
    d = pl.program_id(2)
    j = pl.program_id(3)
    is_fwd = d == 0

    @pl.when(j == 0)
    def _():
        st_s[...] = jnp.zeros_like(st_s)

    sm = sm_ref[0]
    m_cum, m_all = _chunk_masks(is_fwd, TT)
    first = d * GDN_HEADS + pl.program_id(1) * GDN_HB
    rot = jnp.where(first == 0, 0, LANE - first)
    la = pltpu.roll(_softplus(sm + dtb_ref[...]) * nega_ref[...], rot, 1)
    be_sm = pltpu.roll(jax.nn.sigmoid(sm), rot, 1)
    gc_sm = _sel_dot(m_cum, la)
    gl_sm = _sel_dot(m_all, la)
    lane_bcast = lambda a, col: jnp.broadcast_to(a[:, col:col + 1], (TT, GDN_DK))

    r = lax.broadcasted_iota(jnp.int32, (CHUNK, CHUNK), 0)
    c = lax.broadcasted_iota(jnp.int32, (CHUNK, CHUNK), 1)
    causal = _causal_mask(is_fwd, CHUNK)
    strict = jnp.logical_and(causal, r != c)
    b16 = lax.shift_right_logical(r, 4) == lax.shift_right_logical(c, 4)
    b32 = lax.shift_right_logical(r, 5) == lax.shift_right_logical(c, 5)
    eye = jnp.where(r == c, 1.0, 0.0)
    chunk_rows = [slice(ci * CHUNK, (ci + 1) * CHUNK) for ci in range(CPT)]

    amats, rhss, aqks, kdts, qgs = [], [], [], [], []
    for hh in range(GDN_HB):
        cols = slice(hh * GDN_DK, (hh + 1) * GDN_DK)
        gc = lane_bcast(gc_sm, hh)
        gl = lane_bcast(gl_sm, hh)
        beta_e = lane_bcast(be_sm, 2 * GDN_HEADS + hh)
        q = q_ref[0, :, cols]
        k = k_ref[0, :, cols]
        egc = jnp.exp(gc)
        kb = k * beta_e
        qg = q * egc
        kd = k * jnp.exp(gl - gc)
        dec_s[:, cols] = jnp.exp(gl)
        rhs = jnp.concatenate([v_ref[0, :, cols] * beta_e, kb * egc], axis=1)
        for rows in chunk_rows:
            gcc = gc[rows, :]
            dmat = jnp.exp(jnp.where(causal, gcc[:, 0:CHUNK] - gcc.T[0:CHUNK, :], -1e30))
            kc = k[rows].astype(BF16)
            amats.append(jnp.where(strict, _dot_nt(kb[rows].astype(BF16), kc) * dmat, 0.0))
            rhss.append(rhs[rows])
            aqks.append((_dot_nt(q[rows].astype(BF16), kc) * dmat).astype(BF16))
            kdts.append(kd[rows].T.astype(BF16))
            qgs.append(qg[rows])
    tinv = _unit_tri_inverse(amats, b16, b32, eye)
    sols = [_mm2(t, rhs_c).astype(BF16) for t, rhs_c in zip(tinv, rhss)]
    for n, sol in enumerate(sols):
        hh, ci = n // CPT, n % CPT
        rows, cols = chunk_rows[ci], slice(hh * GDN_DK, (hh + 1) * GDN_DK)
        ks = _dot(kdts[n], sol)
        qs = _dot(aqks[n], sol)
        n_s[n] = ks[:, 0:GDN_DV]
        p_s[n] = ks[:, GDN_DV:].astype(BF16)
        oc_s[rows, cols] = qs[:, 0:GDN_DV]
        qp_s[rows, cols] = (qgs[n] - qs[:, GDN_DV:]).astype(BF16)

    for ci in range(CPT):
        cidx = jnp.where(is_fwd, ci, CPT - 1 - ci)
        off = pl.multiple_of(cidx * CHUNK, CHUNK)
        rows = pl.ds(off, CHUNK)
        for hh in range(GDN_HB):
            cols = slice(hh * GDN_DK, (hh + 1) * GDN_DK)
            st = st_s[hh]
            stb = st.astype(BF16)
            o_ref[0, 0, rows, cols] = _dot(qp_s[rows, cols], stb) + oc_s[rows, cols]
            st_s[hh] = st * dec_s[pl.ds(off, 1), cols] - _dot(p_s[hh * CPT + cidx], stb) + n_s[hh * CPT + cidx]


def gdn_scan(qkv, proj, dtb, nega):
    bsz, t, _ = qkv.shape
    nt = t // TT
    tile = lambda d, j: _scan_tile(d, j, nt)
    ng = GDN_HEADS // GDN_HB
    wb = GDN_HB * GDN_DK
    return pl.pallas_call(
        _gdn_kernel,
        grid=(bsz, ng, 2, nt),
        in_specs=[
            pl.BlockSpec((1, TT, wb), lambda i, h, d, j: (i, tile(d, j), h)),
            pl.BlockSpec((1, TT, wb), lambda i, h, d, j: (i, tile(d, j), ng + h)),
            pl.BlockSpec((1, TT, wb), lambda i, h, d, j: (i, tile(d, j), 2 * ng + h)),
            pl.BlockSpec((1, TT, LANE), lambda i, h, d, j: (i, tile(d, j), 6 * D_MODEL // LANE)),
            pl.BlockSpec((1, LANE), lambda i, h, d, j: (0, 0)),
            pl.BlockSpec((1, LANE), lambda i, h, d, j: (0, 0)),
        ],
        out_specs=pl.BlockSpec((1, 1, TT, wb), lambda i, h, d, j: (d, i, tile(d, j), h)),
        out_shape=jax.ShapeDtypeStruct((2, bsz, t, GDN_V), F32),
        scratch_shapes=[pltpu.VMEM((TT, wb), F32),
                        pltpu.VMEM((GDN_HB * CPT, GDN_DK, GDN_DV), F32),
                        pltpu.VMEM((GDN_HB * CPT, GDN_DK, GDN_DK), BF16),
                        pltpu.VMEM((TT, wb), F32), pltpu.VMEM((TT, wb), BF16),
                        pltpu.VMEM((GDN_HB, GDN_DK, GDN_DV), F32)],
        compiler_params=pltpu.CompilerParams(
            dimension_semantics=("parallel", "parallel", "parallel", "arbitrary")),
        name="gdn_scan",
    )(qkv, qkv, qkv, proj, dtb, nega)


CONF_PAD = 16


def _cd_out_kernel(ga_ref, gb_ref, og_ref, of_ref, ob_ref, cw_ref, cb_ref, lg_ref, lb_ref, ng_ref,
                   w_ref, b_ref, x_ref, gate_ref, o_ref, pad_s, shift_s, conv_s):
    j = pl.program_id(1)
    half = (CONF_KERNEL - 1) // 2
    glu = ga_ref[0] * jax.nn.sigmoid(gb_ref[0])
    zeros = jnp.zeros((CONF_PAD, CONF_CH), F32)

    def conv_segments(seglen):
        stride = seglen + 2 * CONF_PAD
        for g in range(TT // seglen):
            base = g * stride
            pad_s[base:base + CONF_PAD, :] = zeros
            pad_s[base + CONF_PAD:base + CONF_PAD + seglen, :] = glu[g * seglen:(g + 1) * seglen]
            pad_s[base + CONF_PAD + seglen:base + stride, :] = zeros
        nseg = TT // seglen
        used = nseg * stride
        conv_s[...] = jnp.zeros((TT, CONF_CH), F32) + cb_ref[...]
        for phase in range(SUBLANES):
            taps = [kk for kk in range(CONF_KERNEL) if (CONF_PAD - half + kk) % SUBLANES == phase]
            if phase:
                shift_s[0:used - SUBLANES, :] = pad_s[phase:used - SUBLANES + phase, :]
            src = shift_s if phase else pad_s
            for g in range(nseg):
                acc = conv_s[g * seglen:(g + 1) * seglen, :]
                for kk in taps:
                    lo = g * stride + CONF_PAD - half + kk - phase
                    acc = acc + src[lo:lo + seglen, :] * cw_ref[kk:kk + 1, :]
                conv_s[g * seglen:(g + 1) * seglen, :] = acc

    @pl.when(j == 0)
    def _():
        conv_segments(TT)

    @pl.when(j > 0)
    def _():
        conv_segments(GRID_W)

    acc = conv_s[...]
    mu = jnp.mean(acc, axis=-1, keepdims=True)
    cen = acc - mu
    var = jnp.mean(cen * cen, axis=-1, keepdims=True)
    conv = _silu(cen * lax.rsqrt(var + 1e-5) * lg_ref[...] + lb_ref[...])
    o = _group_rmsnorm(of_ref[0, 0] + ob_ref[0, 0], GDN_DV) * ng_ref[...] * _silu(og_ref[0])
    m = _dot(conv.astype(BF16), w_ref[0:CONF_CH, :]) + _dot(o.astype(BF16), w_ref[CONF_CH:, :]) + b_ref[...]
    o_ref[0] = x_ref[0] + gate_ref[0, 0] * m


def cd_out(o_gdn, proj, conv_w, conv_b, ln_g, ln_b, norm_g, w, b, x, gate):
    bsz, t, d = x.shape
    seg = lambda i, j: (i, jnp.minimum(j, 1), 0, 0)
    row = lambda width: pl.BlockSpec((1, width), lambda i, j: (0, 0))
    return pl.pallas_call(
        _cd_out_kernel,
        grid=(bsz, t // TT),
        in_specs=[
            pl.BlockSpec((1, TT, CONF_CH), lambda i, j: (i, j, 0)),
            pl.BlockSpec((1, TT, CONF_CH), lambda i, j: (i, j, 1)),
            pl.BlockSpec((1, TT, GDN_V), lambda i, j: (i, j, 5)),
            pl.BlockSpec((1, 1, TT, GDN_V), lambda i, j: (0, i, j, 0)),
            pl.BlockSpec((1, 1, TT, GDN_V), lambda i, j: (1, i, j, 0)),
            pl.BlockSpec((CONF_KERNEL, CONF_CH), lambda i, j: (0, 0)),
            row(CONF_CH), row(CONF_CH), row(CONF_CH), row(GDN_V),
            pl.BlockSpec((CONF_CH + GDN_V, d), lambda i, j: (0, 0)),
            row(d),
            pl.BlockSpec((1, TT, d), lambda i, j: (i, j, 0)),
            pl.BlockSpec((1, 1, 1, d), seg),
        ],
        out_specs=pl.BlockSpec((1, TT, d), lambda i, j: (i, j, 0)),
        out_shape=jax.ShapeDtypeStruct((bsz, t, d), F32),
        scratch_shapes=[pltpu.VMEM(((TT // GRID_W) * (GRID_W + 2 * CONF_PAD), CONF_CH), F32),
                        pltpu.VMEM(((TT // GRID_W) * (GRID_W + 2 * CONF_PAD), CONF_CH), F32),
                        pltpu.VMEM((TT, CONF_CH), F32)],
        compiler_params=pltpu.CompilerParams(
            dimension_semantics=("parallel", "parallel"), vmem_limit_bytes=VMEM_LIMIT),
        name="cd_out",
    )(proj, proj, proj, o_gdn, o_gdn, conv_w, conv_b.reshape(1, -1), ln_g.reshape(1, -1), ln_b.reshape(1, -1),
      norm_g.reshape(1, -1), w, b.reshape(1, d), x, gate)


def _out_proj_kernel(m_ref, w_ref, b_ref, x_ref, gate_ref, o_ref):
    y = _dot(m_ref[0].astype(BF16), w_ref[...]) + b_ref[...]
    o_ref[0] = x_ref[0] + gate_ref[0] * y


def out_proj_residual(mixed, w, b, x, gate):
    bsz, t, k = mixed.shape
    d = w.shape[1]
    tm = min(t, 512)
    return pl.pallas_call(
        _out_proj_kernel,
        grid=(bsz, t // tm),
        in_specs=[
            pl.BlockSpec((1, tm, k), lambda i, j: (i, j, 0)),
            pl.BlockSpec((k, d), lambda i, j: (0, 0)),
            pl.BlockSpec((1, d), lambda i, j: (0, 0)),
            pl.BlockSpec((1, tm, d), lambda i, j: (i, j, 0)),
            pl.BlockSpec((1, 1, d), lambda i, j: (i, 0, 0)),
        ],
        out_specs=pl.BlockSpec((1, tm, d), lambda i, j: (i, j, 0)),
        out_shape=jax.ShapeDtypeStruct((bsz, t, d), F32),
        compiler_params=pltpu.CompilerParams(
            dimension_semantics=("parallel", "parallel"), vmem_limit_bytes=VMEM_LIMIT),
        name="out_proj",
    )(mixed, w, b.reshape(1, d), x, gate.reshape(bsz, 1, d))


def _router_kernel(x_ref, g_ref, sh_ref, sc_ref, rw_ref, h_ref, aff_ref):
    x = x_ref[0]
    ms = jnp.mean(x * x, axis=-1, keepdims=True)
    h = (x * lax.rsqrt(ms + 1e-6) * g_ref[...] * (1.0 + sc_ref[0, 0]) + sh_ref[0, 0]).astype(BF16)
    h_ref[0] = h
    logits = _dot(h, rw_ref[...])
    lane = lax.broadcasted_iota(jnp.int32, logits.shape, 1)
    logits = jnp.where(lane < N_EXPERTS, logits, -1e30)
    e = jnp.exp(logits - jnp.max(logits, axis=-1, keepdims=True))
    aff = e / jnp.sum(e, axis=-1, keepdims=True)
    aff_ref[0] = aff.T[0:N_EXPERTS, :]


def moe_router(xa, g, shift, scale, rw, seg, tile0, ntiles):
    bsz, _, d = xa.shape
    n = ntiles * TT
    return pl.pallas_call(
        _router_kernel,
        grid=(bsz, ntiles),
        in_specs=[
            pl.BlockSpec((1, TT, d), lambda i, j: (i, j + tile0, 0)),
            pl.BlockSpec((1, d), lambda i, j: (0, 0)),
            pl.BlockSpec((1, 1, 1, d), lambda i, j: (i, seg, 0, 0)),
            pl.BlockSpec((1, 1, 1, d), lambda i, j: (i, seg, 0, 0)),
            pl.BlockSpec((d, LANE), lambda i, j: (0, 0)),
        ],
        out_specs=[pl.BlockSpec((1, TT, d), lambda i, j: (i, j, 0)),
                   pl.BlockSpec((1, N_EXPERTS, TT), lambda i, j: (i, 0, j))],
        out_shape=[jax.ShapeDtypeStruct((bsz, n, d), BF16), jax.ShapeDtypeStruct((bsz, N_EXPERTS, n), F32)],
        compiler_params=pltpu.CompilerParams(dimension_semantics=("parallel", "parallel")),
        name="moe_router",
    )(xa, g.reshape(1, d), shift, scale, rw)


def _lane_block_prefix(x, u_strict):
    nblk = x.shape[1] // LANE
    run = jnp.zeros((x.shape[0], 1), F32)
    outs = []
    for cblk in range(nblk):
        xc = x[:, cblk * LANE:(cblk + 1) * LANE]
        outs.append(_dot(xc.astype(BF16), u_strict) + run)
        run = run + jnp.sum(xc, axis=-1, keepdims=True)
    return jnp.concatenate(outs, axis=1), run


def _select_kernel(aff_ref, slot_ref, *, cap):
    aff = aff_ref[0]
    bits = pltpu.bitcast(aff, jnp.int32)
    capf = jnp.float32(cap)

    def step(i, thr):
        cand = jnp.bitwise_or(thr, lax.shift_left(jnp.int32(1), 30 - i))
        cnt = jnp.sum(jnp.where(bits >= cand, 1.0, 0.0), axis=-1, keepdims=True)
        return jnp.where(cnt >= capf, cand, thr)

    thr = lax.fori_loop(0, 31, step, jnp.zeros((aff.shape[0], 1), jnp.int32))
    gt = jnp.where(bits > thr, 1.0, 0.0)
    eq = jnp.where(bits == thr, 1.0, 0.0)
    r = lax.broadcasted_iota(jnp.int32, (LANE, LANE), 0)
    c = lax.broadcasted_iota(jnp.int32, (LANE, LANE), 1)
    u_strict = jnp.where(r < c, 1.0, 0.0).astype(BF16)
    need = capf - jnp.sum(gt, axis=-1, keepdims=True)
    eq_rank, _ = _lane_block_prefix(eq, u_strict)
    sel = jnp.maximum(gt, jnp.where(eq_rank < need, eq, 0.0))
    slot, _ = _lane_block_prefix(sel, u_strict)
    slot_ref[0] = jnp.where(sel > 0.0, slot.astype(jnp.int32), -1)


def moe_select(aff, cap):
    bsz, ne, n = aff.shape
    return pl.pallas_call(
        functools.partial(_select_kernel, cap=cap),
        grid=(bsz,),
        in_specs=[pl.BlockSpec((1, ne, n), lambda i: (i, 0, 0))],
        out_specs=pl.BlockSpec((1, ne, n), lambda i: (i, 0, 0)),
        out_shape=jax.ShapeDtypeStruct((bsz, ne, n), jnp.int32),
        compiler_params=pltpu.CompilerParams(dimension_semantics=("parallel",)),
        name="moe_select",
    )(aff)


def _slot_index_kernel(slot_ref, idx_ref, *, cap):
    slot = slot_ref[0]
    n = slot.shape[1]
    srow = lax.broadcasted_iota(jnp.int32, (cap, LANE), 0)
    lane = lax.broadcasted_iota(jnp.int32, (cap, LANE), 1)
    acc = jnp.zeros((cap, LANE), jnp.int32)
    for cblk in range(n // LANE):
        s_c = slot[:, cblk * LANE:(cblk + 1) * LANE]
        acc = acc + jnp.where(srow == s_c, lane + (cblk * LANE + 1), 0)
    ones = jnp.ones((8, LANE), BF16)
    hi = _dot_nt(ones, lax.shift_right_logical(acc, 7).astype(F32).astype(BF16))
    lo = _dot_nt(ones, jnp.bitwise_and(acc, LANE - 1).astype(F32).astype(BF16))
    idx_ref[0] = (hi[0:1, :] * float(LANE) + lo[0:1, :]).astype(jnp.int32) - 1


SLOT_WIN = 2 * LANE


def _slot_index_win_kernel(base_ref, slot_ref, idx_ref, acc_s, *, cap):
    i = pl.program_id(0)
    n = slot_ref.shape[2]
    acc_s[...] = jnp.zeros(acc_s.shape, jnp.int32)
    srow = lax.broadcasted_iota(jnp.int32, (SLOT_WIN, LANE), 0)
    lane = lax.broadcasted_iota(jnp.int32, (SLOT_WIN, LANE), 1)
    for cblk in range(n // LANE):
        base = pl.multiple_of(base_ref[i, cblk], LANE)
        s_c = slot_ref[0, :, cblk * LANE:(cblk + 1) * LANE]
        rows = pl.ds(base, SLOT_WIN)
        acc_s[rows, :] = acc_s[rows, :] + jnp.where(srow + base == s_c, lane + (cblk * LANE + 1), 0)
    acc = acc_s[0:cap, :]
    ones = jnp.ones((8, LANE), BF16)
    hi = _dot_nt(ones, lax.shift_right_logical(acc, 7).astype(F32).astype(BF16))
    lo = _dot_nt(ones, jnp.bitwise_and(acc, LANE - 1).astype(F32).astype(BF16))
    idx_ref[0] = (hi[0:1, :] * float(LANE) + lo[0:1, :]).astype(jnp.int32) - 1


def moe_slot_index_windowed(slot, base, cap):
    bsz, ne, n = slot.shape
    idx = pl.pallas_call(
        functools.partial(_slot_index_win_kernel, cap=cap),
        grid_spec=pltpu.PrefetchScalarGridSpec(
            num_scalar_prefetch=1,
            grid=(bsz * ne,),
            in_specs=[pl.BlockSpec((1, 1, n), lambda i, base_ref: (i, 0, 0))],
            out_specs=pl.BlockSpec((1, 1, cap), lambda i, base_ref: (i, 0, 0)),
            scratch_shapes=[pltpu.VMEM((cap + LANE, LANE), jnp.int32)],
        ),
        out_shape=jax.ShapeDtypeStruct((bsz * ne, 1, cap), jnp.int32),
        compiler_params=pltpu.CompilerParams(dimension_semantics=("arbitrary",)),
        name="moe_slot_index",
    )(base, slot.reshape(bsz * ne, 1, n))
    return idx.reshape(bsz, ne, cap)


def moe_slot_index(slot, cap):
    bsz, ne, n = slot.shape
    idx = pl.pallas_call(
        functools.partial(_slot_index_kernel, cap=cap),
        grid=(bsz * ne,),
        in_specs=[pl.BlockSpec((1, 1, n), lambda i: (i, 0, 0))],
        out_specs=pl.BlockSpec((1, 1, cap), lambda i: (i, 0, 0)),
        out_shape=jax.ShapeDtypeStruct((bsz * ne, 1, cap), jnp.int32),
        compiler_params=pltpu.CompilerParams(dimension_semantics=("parallel",)),
        name="moe_slot_index",
    )(slot.reshape(bsz * ne, 1, n))
    return idx.reshape(bsz, ne, cap)


WIN_ALIGN = 16
WIN_FAST = 128


def _combine_kernel(ws_ref, slot_ref, aff_ref, *rest, win, final):
    ye_refs, (x_ref, gate_ref), o_ref = rest[:N_EXPERTS], rest[N_EXPERTS:N_EXPERTS + 2], rest[-1]
    b = pl.program_id(0)
    j = pl.program_id(1)
    srow = lax.broadcasted_iota(jnp.int32, (win, TT), 0)
    his, los = [], []
    for e in range(N_EXPERTS):
        sel = jnp.where(srow + ws_ref[b, e, j] == slot_ref[0, e:e + 1, :], aff_ref[0, e:e + 1, :], 0.0).T
        hi = sel.astype(BF16)
        his.append(hi)
        los.append((sel - hi.astype(F32)).astype(BF16))
    ye = jnp.concatenate([r[...] for r in ye_refs], axis=0)
    acc = _dot(jnp.concatenate(his, axis=1), ye) + _dot(jnp.concatenate(los, axis=1), ye)
    y = x_ref[0] + gate_ref[0, 0] * acc
    if final:
        g_ref = rest[N_EXPERTS + 2]
        y = y * lax.rsqrt(jnp.mean(y * y, axis=-1, keepdims=True) + 1e-6) * g_ref[...]
    o_ref[0] = y


def moe_combine(ws, slot, aff, ye, xa, gate, seg, tile0, win, final_g=None):
    bsz, ne, n = slot.shape
    nt = n // TT
    d = xa.shape[2]

    def ye_spec(e):
        return pl.BlockSpec((pl.Squeezed(), pl.Squeezed(), pl.Element(win), pl.Element(d)),
                            lambda i, j, ws_ref: (i, e, pl.multiple_of(ws_ref[i, e, j], WIN_ALIGN), 0))

    in_specs = ([pl.BlockSpec((1, ne, TT), lambda i, j, ws_ref: (i, 0, j)),
                 pl.BlockSpec((1, ne, TT), lambda i, j, ws_ref: (i, 0, j))]
                + [ye_spec(e) for e in range(ne)]
                + [pl.BlockSpec((1, TT, d), lambda i, j, ws_ref: (i, j + tile0, 0)),
                   pl.BlockSpec((1, 1, 1, d), lambda i, j, ws_ref: (i, seg, 0, 0))])
    args = (ws, slot, aff, *([ye] * ne), xa, gate)
    if final_g is None:
        out_idx, out_shape, aliases = (lambda i, j, ws_ref: (i, j + tile0, 0)), xa.shape, {3 + ne: 0}
    else:
        out_idx, out_shape, aliases = (lambda i, j, ws_ref: (i, j, 0)), (bsz, n, d), {}
        in_specs.append(pl.BlockSpec((1, d), lambda i, j, ws_ref: (0, 0)))
        args += (final_g.reshape(1, d),)
    return pl.pallas_call(
        functools.partial(_combine_kernel, win=win, final=final_g is not None),
        grid_spec=pltpu.PrefetchScalarGridSpec(
            num_scalar_prefetch=1,
            grid=(bsz, nt),
            in_specs=in_specs,
            out_specs=pl.BlockSpec((1, TT, d), out_idx),
        ),
        out_shape=jax.ShapeDtypeStruct(out_shape, F32),
        input_output_aliases=aliases,
        compiler_params=pltpu.CompilerParams(
            dimension_semantics=("parallel", "parallel"), vmem_limit_bytes=VMEM_LIMIT),
        name="moe_combine",
    )(*args)


def _expert_ffn_kernel(x_ref, w1_ref, w3_ref, w2_ref, o_ref):
    x = x_ref[0, 0].astype(BF16)
    a = _dot(x, w1_ref[0])
    g = _dot(x, w3_ref[0])
    o_ref[0, 0] = _dot((_silu(a) * g).astype(BF16), w2_ref[0]).astype(BF16)


def expert_ffn(xe, w1, w3, w2):
    bsz, ne, cap, d = xe.shape
    f = w1.shape[2]
    tm = min(cap, 512)
    return pl.pallas_call(
        _expert_ffn_kernel,
        grid=(ne, bsz, cap // tm),
        in_specs=[
            pl.BlockSpec((1, 1, tm, d), lambda e, i, j: (i, e, j, 0)),
            pl.BlockSpec((1, d, f), lambda e, i, j: (e, 0, 0)),
            pl.BlockSpec((1, d, f), lambda e, i, j: (e, 0, 0)),
            pl.BlockSpec((1, f, d), lambda e, i, j: (e, 0, 0)),
        ],
        out_specs=pl.BlockSpec((1, 1, tm, d), lambda e, i, j: (i, e, j, 0)),
        out_shape=jax.ShapeDtypeStruct((bsz, ne, cap, d), BF16),
        compiler_params=pltpu.CompilerParams(
            dimension_semantics=("parallel", "parallel", "parallel"), vmem_limit_bytes=VMEM_LIMIT),
        name="expert_ffn",
    )(xe, w1, w3, w2)


def _rmsnorm_kernel(x_ref, g_ref, o_ref):
    x = x_ref[0]
    ms = jnp.mean(x * x, axis=-1, keepdims=True)
    o_ref[0] = x * lax.rsqrt(ms + 1e-6) * g_ref[...]


def rmsnorm_rows(x, g, tile0):
    bsz, t, d = x.shape
    nt = t // TT - tile0
    return pl.pallas_call(
        _rmsnorm_kernel,
        grid=(bsz, nt),
        in_specs=[pl.BlockSpec((1, TT, d), lambda i, j: (i, j + tile0, 0)), pl.BlockSpec((1, d), lambda i, j: (0, 0))],
        out_specs=pl.BlockSpec((1, TT, d), lambda i, j: (i, j, 0)),
        out_shape=jax.ShapeDtypeStruct((bsz, nt * TT, d), F32),
        compiler_params=pltpu.CompilerParams(dimension_semantics=("parallel", "parallel")),
        name="final_rmsnorm",
    )(x, g.reshape(1, d))


def _ab_in_layout(w_in, b_in):
    q, k, v, r, glr, z, xs, bm, cm, dt = _split_cols(
        jnp.concatenate([w_in, b_in[None]], axis=0),
        (GLA_QK, GLA_QK, GLA_V, GLA_V, 2 * GLA_GATE_RANK, SSD_INNER, SSD_INNER, SSD_BC, SSD_BC, 2 * SSD_HEADS))
    wb = _pad_cols(jnp.concatenate([q, k, v, r, z, xs, bm, cm, glr, dt], axis=1), AB_N)
    return wb[:-1].astype(BF16), wb[-1]


def _gla_gate_params(w_gate2, b_gate2):
    wg = jnp.zeros((2, LANE, GLA_QK), F32)
    for d in range(2):
        wg = wg.at[d, d * GLA_GATE_RANK:(d + 1) * GLA_GATE_RANK, :].set(w_gate2[d])
    return wg.astype(BF16), b_gate2.reshape(2, 1, GLA_QK)


def _ssd_params(dt_bias, a_log):
    dtb = jnp.zeros((2, 1, LANE), F32)
    nega = jnp.zeros((2, 1, LANE), F32)
    e = np.zeros((2, LANE, SSD_INNER), np.float32)
    for d in range(2):
        c0 = SM_DT + d * SSD_HEADS
        dtb = dtb.at[d, 0, c0:c0 + SSD_HEADS].set(dt_bias[d])
        nega = nega.at[d, 0, c0:c0 + SSD_HEADS].set(-jnp.exp(a_log[d]))
        for h in range(SSD_HEADS):
            e[d, c0 + h, h * SSD_HEADDIM:(h + 1) * SSD_HEADDIM] = 1.0
    return dtb, nega, jnp.asarray(e, BF16)


def _gdn_params(dt_bias, a_log):
    n = 2 * GDN_HEADS
    dtb = jnp.zeros((1, LANE), F32).at[0, 0:n].set(dt_bias.reshape(n))
    nega = jnp.zeros((1, LANE), F32).at[0, 0:n].set(-jnp.exp(a_log.reshape(n)))
    return dtb, nega


def _rmsnorm(x, g, eps=1e-6):
    return x * lax.rsqrt(jnp.mean(jnp.square(x), axis=-1, keepdims=True) + eps) * g


def _head_rmsnorm(x, g):
    return _rmsnorm(x, g.reshape(x.shape[-2:]))


def _layernorm(x, g, b, eps=1e-5):
    mu = jnp.mean(x, axis=-1, keepdims=True)
    var = jnp.mean(jnp.square(x - mu), axis=-1, keepdims=True)
    return (x - mu) * lax.rsqrt(var + eps) * g + b


def _l2norm(x, eps=1e-6):
    return x * lax.rsqrt(jnp.sum(jnp.square(x), axis=-1, keepdims=True) + eps)


def _dwconv(x, w, b=None):
    k, ch = w.shape
    pad = (k - 1) // 2
    y = lax.conv_general_dilated(x, w[:, None, :], (1,), [(pad, pad)],
                                 dimension_numbers=('NWC', 'WIO', 'NWC'), feature_group_count=ch)
    return y if b is None else y + b


def _gdn_scan(q, k, v, beta, logg, s0, with_out):
    bsz, nh, t, dk = q.shape
    dv = v.shape[-1]
    nc = t // CHUNK
    chunks = lambda z: z.reshape(bsz, nh, nc, CHUNK, *z.shape[3:])
    q, k, v, beta, logg = (chunks(z) for z in (q, k, v, beta, logg))
    gc = jnp.cumsum(logg, axis=-1)
    glast = gc[..., -1]
    tril = jnp.tril(jnp.ones((CHUNK, CHUNK), dtype=bool))
    strict = jnp.tril(jnp.ones((CHUNK, CHUNK), dtype=bool), k=-1)
    decay = jnp.exp(jnp.where(tril, gc[..., :, None] - gc[..., None, :], -jnp.inf))
    kb = k * beta[..., None]
    m = jnp.eye(CHUNK, dtype=k.dtype) + jnp.where(strict, jnp.einsum('bhcid,bhcjd->bhcij', kb, k) * decay, 0.0)
    rhs = jnp.concatenate([v * beta[..., None], kb * jnp.exp(gc)[..., None]], axis=-1)
    sol = lax.linalg.triangular_solve(m, rhs, left_side=True, lower=True, unit_diagonal=True)
    u, w = sol[..., :dv], sol[..., dv:]
    kd = k * jnp.exp(glast[..., None] - gc)[..., None]
    dec = jnp.exp(glast)
    front = lambda z: jnp.moveaxis(z, 2, 0)

    def advance(s, w_c, u_c, kd_c, dec_c):
        vn = u_c - jnp.einsum('bhid,bhde->bhie', w_c, s)
        return vn, dec_c[..., None, None] * s + jnp.einsum('bhjd,bhje->bhde', kd_c, vn)

    xs = (front(w), front(u), front(kd), front(dec))
    if not with_out:
        s_fin, _ = lax.scan(lambda s, xc: (advance(s, *xc)[1], None), s0, xs)
        return None, s_fin
    aqk = jnp.einsum('bhcid,bhcjd->bhcij', q, k) * decay
    qg = q * jnp.exp(gc)[..., None]

    def step(s, xc):
        w_c, u_c, kd_c, dec_c, aqk_c, qg_c = xc
        vn, s_new = advance(s, w_c, u_c, kd_c, dec_c)
        o = jnp.einsum('bhid,bhde->bhie', qg_c, s) + jnp.einsum('bhij,bhje->bhie', aqk_c, vn)
        return s_new, o

    s_fin, o = lax.scan(step, s0, xs + (front(aqk), front(qg)))
    return jnp.moveaxis(o, 0, 2).reshape(bsz, nh, t, dv), s_fin


def _bidir_scan(scan_fn, args_f, args_b, t_axis, init_f, init_b, with_out):
    flip = lambda a: jnp.flip(a, axis=t_axis)
    o_f, s_f = scan_fn(*args_f, init_f, with_out)
    o_b, s_b = scan_fn(*[flip(a) for a in args_b], init_b, with_out)
    o = o_f + flip(o_b) if with_out else None
    return o, s_f, s_b


def _conformer_conv(u, rows, dw_w, dw_b, ln_g, ln_b):
    bsz, t, ch = u.shape
    y = _dwconv(u.reshape(bsz * rows, t // rows, ch), dw_w, dw_b).reshape(bsz, t, ch)
    return jax.nn.silu(_layernorm(y, ln_g, ln_b))


def _cd_stream(proj, gdn_conv_w, gdn_a_log, gdn_dt_bias):
    bsz, t, _ = proj.shape
    ga, gb, q, k, v, og, a_raw, b_raw = _split_cols(proj, CD_SPLITS)
    glu = ga * jax.nn.sigmoid(gb)
    qkv = jax.nn.silu(_dwconv(jnp.concatenate([q, k, v], axis=-1), gdn_conv_w))
    q, k, v = _split_cols(qkv, (GDN_QK, GDN_QK, GDN_V))
    heads = lambda a: a.reshape(bsz, t, GDN_HEADS, -1).transpose(0, 2, 1, 3)
    q = _l2norm(heads(q)) * GDN_DK ** -0.5
    k = _l2norm(heads(k))
    v = heads(v)
    beta = jax.nn.sigmoid(b_raw.reshape(bsz, t, 2, GDN_HEADS)).transpose(2, 0, 3, 1)
    logg = (-jnp.exp(gdn_a_log)
            * jax.nn.softplus(a_raw.reshape(bsz, t, 2, GDN_HEADS) + gdn_dt_bias)).transpose(2, 0, 3, 1)
    return (q, k, v, beta[0], logg[0]), (q, k, v, beta[1], logg[1]), glu, og


def _mixer_cd(proj_l, proj_c, rows, conf_dw_w, conf_dw_b, conf_ln_g, conf_ln_b,
              gdn_conv_w, gdn_a_log, gdn_dt_bias, gdn_norm_g, need_ctx):
    c_f, c_b, c_glu, c_og = _cd_stream(proj_c, gdn_conv_w, gdn_a_log, gdn_dt_bias)
    l_f, l_b, l_glu, l_og = _cd_stream(proj_l, gdn_conv_w, gdn_a_log, gdn_dt_bias)
    bsz = proj_l.shape[0]
    z0 = jnp.zeros((bsz, GDN_HEADS, GDN_DK, GDN_DV), F32)
    od_c, s_f, s_b = _bidir_scan(_gdn_scan, c_f, c_b, 2, z0, z0, need_ctx)
    od_l, _, _ = _bidir_scan(_gdn_scan, l_f, l_b, 2, s_f, s_b, True)

    def mix(od, glu, og, n_rows):
        t = glu.shape[1]
        conv = _conformer_conv(glu, n_rows, conf_dw_w, conf_dw_b, conf_ln_g, conf_ln_b)
        o = _head_rmsnorm(jnp.swapaxes(od, 1, 2), gdn_norm_g) * jax.nn.silu(og).reshape(bsz, t, GDN_HEADS, GDN_DV)
        return jnp.concatenate([conv, o.reshape(bsz, t, GDN_V)], axis=-1)

    return mix(od_l, l_glu, l_og, rows), (mix(od_c, c_glu, c_og, 1) if need_ctx else None)


def moe_segment(i, xa, mods, p, w1, w3, w2, seg, tile0, ntiles, final_g=None):
    bsz, _, d = xa.shape
    n = ntiles * TT
    cap = n * EC_CAPACITY // N_EXPERTS
    rw = _pad_cols(p["moe_router"][i], LANE).astype(BF16)
    h, aff = moe_router(xa, p["norm2_g"][i], mods[:, :, 3], mods[:, :, 4], rw, seg, tile0, ntiles)
    slot = moe_select(aff, cap)
    cnt128 = jnp.sum((slot >= 0).reshape(bsz, N_EXPERTS, n // LANE, LANE), axis=-1, dtype=jnp.int32)
    start128 = jnp.cumsum(cnt128, axis=-1) - cnt128
    if cap >= SLOT_WIN:
        base = jnp.minimum(start128 // LANE * LANE, cap - LANE).reshape(bsz * N_EXPERTS, n // LANE)
        idx = moe_slot_index_windowed(slot, base, cap)
    else:
        idx = moe_slot_index(slot, cap)
    idx = idx.reshape(bsz, N_EXPERTS * cap)
    xe = jnp.take_along_axis(h, idx[..., None], axis=1).reshape(bsz, N_EXPERTS, cap, d)
    ye = expert_ffn(xe, w1, w3, w2)
    per_tile = TT // LANE
    counts = jnp.sum(cnt128.reshape(bsz, N_EXPERTS, ntiles, per_tile), axis=-1)
    starts = start128[:, :, ::per_tile]
    aligned = starts // WIN_ALIGN * WIN_ALIGN

    def run(win):
        ws = jnp.minimum(aligned, cap - win)
        return moe_combine(ws, slot, aff, ye, xa, mods[:, :, 5], seg, tile0, win, final_g)

    win_fast, win_full = min(cap, WIN_FAST), min(cap, TT + WIN_ALIGN)
    if win_fast == win_full:
        return run(win_full)
    overflow = jnp.any(starts + counts - jnp.minimum(aligned, cap - win_fast) > win_fast)
    return lax.cond(overflow, lambda: run(win_full), lambda: run(win_fast))


def layer_mixer(i, j, xa, mods, p, last):
    sh1, sc1, g1 = (mods[:, :, s] for s in range(3))
    if i % 2 == 0:
        w_in, b_in = _ab_in_layout(p["ab_w_in"][j], p["ab_b_in"][j])
        proj = norm_proj(xa, p["norm1_g"][i], sh1, sc1, w_in, b_in, 1920)
        xbc = ab_prep(proj, p["ssd_conv_w"][j], p["ssd_conv_b"][j])
        wg, bg = _gla_gate_params(p["gla_w_gate2"][j], p["gla_b_gate2"][j])
        o_gla = gla_scan(proj, wg, bg)
        y_ssd = ssd_scan(xbc, proj, *_ssd_params(p["ssd_dt_bias"][j], p["ssd_a_log"][j]))
        return ab_out(o_gla, y_ssd, proj, xbc, p["gla_norm_g"][j], jnp.repeat(p["ssd_d"][j], SSD_HEADDIM),
                      p["ssd_norm_g"][j], p["ab_w_out"][j].astype(BF16), p["ab_b_out"][j], xa, g1)
    w_in = _pad_cols(p["cd_w_in"][j], CD_N).astype(BF16)
    b_in = _pad_cols(p["cd_b_in"][j], CD_N)
    proj = norm_proj(xa, p["norm1_g"][i], sh1, sc1, w_in, b_in, 1280)
    qkv = cd_prep(proj, p["gdn_conv_w"][j])
    o_gdn = gdn_scan(qkv, proj, *_gdn_params(p["gdn_dt_bias"][j], p["gdn_a_log"][j]))
    return cd_out(o_gdn, proj, p["conf_dw_w"][j], p["conf_dw_b"][j], p["conf_ln_g"][j], p["conf_ln_b"][j],
                  p["gdn_norm_g"][j], p["cd_w_out"][j].astype(BF16), p["cd_b_out"][j], xa, g1)


def kernel(x, c, ctx, c_ctx, mod_w, mod_b, norm1_g, norm2_g, ab_w_in, ab_b_in, ab_w_out, ab_b_out, gla_w_gate2, gla_b_gate2, gla_norm_g, ssd_conv_w, ssd_conv_b, ssd_dt_bias, ssd_a_log, ssd_d, ssd_norm_g, cd_w_in, cd_b_in, cd_w_out, cd_b_out, conf_dw_w, conf_dw_b, conf_ln_g, conf_ln_b, gdn_conv_w, gdn_a_log, gdn_dt_bias, gdn_norm_g, moe_router, moe_w1, moe_w3, moe_w2, final_norm_g):
    p = dict(locals())
    bsz, seq, d = x.shape
    assert ctx.shape[1] == TT and seq % TT == 0
    depth = mod_w.shape[0]
    xa = jnp.concatenate([ctx, x], axis=1)
    for i in range(depth):
        last = i == depth - 1
        mod_l = jax.nn.silu(c) @ mod_w[i] + mod_b[i]
        mod_c = jnp.broadcast_to(jax.nn.silu(c_ctx) @ mod_w[i] + mod_b[i], mod_l.shape)
        mods = jnp.stack([mod_c, mod_l], axis=1).reshape(bsz, 2, 6, 1, d)
        sh1, sc1, g1, sh2, sc2, g2 = (mods[:, :, s] for s in range(6))
        xa = layer_mixer(i, i // 2, xa, mods, p, last)
        w1, w3, w2 = moe_w1[i].astype(BF16), moe_w3[i].astype(BF16), moe_w2[i].astype(BF16)
        if last:
            return moe_segment(i, xa, mods, p, w1, w3, w2, 1, 1, seq // TT, final_norm_g)
        xa = moe_segment(i, xa, mods, p, w1, w3, w2, 1, 1, seq // TT)
        xa = moe_segment(i, xa, mods, p, w1, w3, w2, 0, 0, 1)
```

```python
import functools

import jax
import jax.numpy as jnp
import numpy as np
from jax import lax
from jax.experimental import pallas as pl
from jax.experimental.pallas import tpu as pltpu

F32 = jnp.float32
BF16 = jnp.bfloat16

D_MODEL = 1024
GRID_W = 64
CHUNK = 64
GLA_HEADS, GLA_DK, GLA_DV, GLA_GATE_RANK, GLA_GATE_TAU = 4, 128, 256, 16, 16.0
SSD_HEADS, SSD_HEADDIM, SSD_STATE, SSD_GROUPS = 16, 64, 128, 2
CONF_CH, CONF_KERNEL = D_MODEL, 31
GDN_HEADS, GDN_DK, GDN_DV = 8, 128, 128
N_EXPERTS, EC_CAPACITY, EXPERT_FF = 16, 2, D_MODEL

GLA_QK = GLA_HEADS * GLA_DK
GLA_V = GLA_HEADS * GLA_DV
SSD_INNER = SSD_HEADS * SSD_HEADDIM
SSD_BC = SSD_GROUPS * SSD_STATE
SSD_HPG = SSD_HEADS // SSD_GROUPS
GDN_QK = GDN_HEADS * GDN_DK
GDN_V = GDN_HEADS * GDN_DV
CD_SPLITS = (CONF_CH, CONF_CH, GDN_QK, GDN_QK, GDN_V, GDN_V, 2 * GDN_HEADS, 2 * GDN_HEADS)

LANE = 128
SUBLANES = 8
TT = 256
CPT = TT // CHUNK
VMEM_LIMIT = 48 * 1024 * 1024

AB_Q, AB_K, AB_V, AB_R, AB_Z, AB_XS, AB_BM, AB_CM, AB_SMALL = 0, 512, 1024, 2048, 3072, 4096, 5120, 5376, 5632
AB_N = AB_SMALL + LANE
SM_DT = 2 * GLA_GATE_RANK
CD_N = 6400


def _split_cols(a, sizes):
    return jnp.split(a, np.cumsum(sizes)[:-1].tolist(), axis=-1)


def _pad_cols(a, n):
    return jnp.pad(a, [(0, 0)] * (a.ndim - 1) + [(0, n - a.shape[-1])])


def _dot(a, b):
    return jnp.dot(a, b, preferred_element_type=F32)


def _dot_nt(a, b):
    return lax.dot_general(a, b, (((1,), (1,)), ((), ())), preferred_element_type=F32)


def _split3(x):
    hi = x.astype(BF16)
    r = x - hi.astype(F32)
    mid = r.astype(BF16)
    lo = (r - mid.astype(F32)).astype(BF16)
    return hi, mid, lo


def _sel_dot(m, x):
    hi, mid, lo = _split3(x)
    return _dot(m, hi) + _dot(m, mid) + _dot(m, lo)


def _dot_sel(x, e):
    hi, mid, lo = _split3(x)
    return _dot(hi, e) + _dot(mid, e) + _dot(lo, e)


def _softplus(x):
    return jnp.maximum(x, 0.0) + jnp.log(1.0 + jnp.exp(-jnp.abs(x)))


def _silu(x):
    return x * jax.nn.sigmoid(x)


def _chunk_masks(is_fwd, n):
    r = lax.broadcasted_iota(jnp.int32, (n, n), 0)
    c = lax.broadcasted_iota(jnp.int32, (n, n), 1)
    same = lax.shift_right_logical(r, 6) == lax.shift_right_logical(c, 6)
    lo = jnp.where(is_fwd, c, r)
    hi = jnp.where(is_fwd, r, c)
    cum = jnp.logical_and(same, lo <= hi)
    return jnp.where(cum, 1.0, 0.0).astype(BF16), jnp.where(same, 1.0, 0.0).astype(BF16)


def _causal_mask(is_fwd, n):
    r = lax.broadcasted_iota(jnp.int32, (n, n), 0)
    c = lax.broadcasted_iota(jnp.int32, (n, n), 1)
    return jnp.where(is_fwd, c, r) <= jnp.where(is_fwd, r, c)


def _scan_tile(d, j, nt):
    return jnp.where(d == 0, j, jnp.where(j == 0, 0, nt - j))


def _mod_proj_kernel(c_ref, w_ref, b_ref, o_ref):
    o_ref[...] = _dot(_silu(c_ref[...]).astype(BF16), w_ref[...]) + b_ref[...]


def mod_proj(cond, w, b):
    r, d = cond.shape
    n = w.shape[1]
    tn = 6 * LANE * 2
    return pl.pallas_call(
        _mod_proj_kernel,
        grid=(n // tn,),
        in_specs=[pl.BlockSpec((r, d), lambda k: (0, 0)),
                  pl.BlockSpec((d, tn), lambda k: (0, k)),
                  pl.BlockSpec((1, tn), lambda k: (0, k))],
        out_specs=pl.BlockSpec((r, tn), lambda k: (0, k)),
        out_shape=jax.ShapeDtypeStruct((r, n), F32),
        compiler_params=pltpu.CompilerParams(dimension_semantics=("parallel",)),
        name="mod_proj",
    )(cond, w, b.reshape(1, n))


NP_TM = 3 * TT


def _norm_proj_kernel(x_ref, g_ref, sh_ref, sc_ref, w_ref, b_ref, o_ref):
    x = x_ref[0]
    row = lax.broadcasted_iota(jnp.int32, (NP_TM, 1), 0) + pl.program_id(2) * NP_TM
    is_ctx = row < TT
    scale = jnp.where(is_ctx, sc_ref[0, 0], sc_ref[0, 1])
    shift = jnp.where(is_ctx, sh_ref[0, 0], sh_ref[0, 1])
    ms = jnp.mean(x * x, axis=-1, keepdims=True)
    h = (x * lax.rsqrt(ms + 1e-6) * g_ref[...] * (1.0 + scale) + shift).astype(BF16)
    o_ref[0] = _dot(h, w_ref[...]) + b_ref[...]


def norm_proj(x, g, shift, scale, w, b, tn):
    bsz, t, d = x.shape
    n = w.shape[1]
    assert t % NP_TM == 0
    return pl.pallas_call(
        _norm_proj_kernel,
        grid=(n // tn, bsz, t // NP_TM),
        in_specs=[
            pl.BlockSpec((1, NP_TM, d), lambda k, i, j: (i, j, 0)),
            pl.BlockSpec((1, d), lambda k, i, j: (0, 0)),
            pl.BlockSpec((1, 2, 1, d), lambda k, i, j: (i, 0, 0, 0)),
            pl.BlockSpec((1, 2, 1, d), lambda k, i, j: (i, 0, 0, 0)),
            pl.BlockSpec((d, tn), lambda k, i, j: (0, k)),
            pl.BlockSpec((1, tn), lambda k, i, j: (0, k)),
        ],
        out_specs=pl.BlockSpec((1, NP_TM, tn), lambda k, i, j: (i, j, k)),
        out_shape=jax.ShapeDtypeStruct((bsz, t, n), F32),
        compiler_params=pltpu.CompilerParams(
            dimension_semantics=("parallel", "parallel", "parallel"), vmem_limit_bytes=VMEM_LIMIT),
        name="norm_proj",
    )(x, g.reshape(1, d), shift, scale, w, b.reshape(1, n))


def _conv3_piece(x, lh, rh, w, b, left_ok, right_ok):
    n = x.shape[0]
    row = lax.broadcasted_iota(jnp.int32, x.shape, 0)
    prev_row = jnp.where(left_ok, lh[7:8, :], 0.0)
    next_row = jnp.where(right_ok, rh[0:1, :], 0.0)
    x_prev = jnp.where(row == 0, prev_row, pltpu.roll(x, 1, 0))
    x_next = jnp.where(row == n - 1, next_row, pltpu.roll(x, n - 1, 0))
    return _silu(w[0:1, :] * x_prev + w[1:2, :] * x + w[2:3, :] * x_next + b)


def _ab_prep_kernel(xs_ref, bm_ref, cm_ref, xsl_ref, bml_ref, cml_ref, xsr_ref, bmr_ref, cmr_ref,
                    w_ref, b_ref, o_ref):
    j = pl.program_id(1)
    nt = pl.num_programs(1)
    left_ok = j >= 2
    right_ok = jnp.logical_and(j >= 1, j < nt - 1)
    w = w_ref[...]
    b = b_ref[...]
    o_ref[0, :, 0:SSD_INNER] = _conv3_piece(xs_ref[0], xsl_ref[0], xsr_ref[0], w[:, 0:SSD_INNER],
                                            b[:, 0:SSD_INNER], left_ok, right_ok)
    c0, c1 = SSD_INNER, SSD_INNER + SSD_BC
    o_ref[0, :, c0:c1] = _conv3_piece(bm_ref[0], bml_ref[0], bmr_ref[0], w[:, c0:c1], b[:, c0:c1], left_ok, right_ok)
    c0, c1 = c1, c1 + SSD_BC
    o_ref[0, :, c0:c1] = _conv3_piece(cm_ref[0], cml_ref[0], cmr_ref[0], w[:, c0:c1], b[:, c0:c1], left_ok, right_ok)


def ab_prep(proj, conv_w, conv_b):
    bsz, t, _ = proj.shape
    nt = t // TT
    rb = TT // 8
    nrb = t // 8
    cw = SSD_INNER + 2 * SSD_BC

    def cur(width, col):
        return pl.BlockSpec((1, TT, width), lambda i, j: (i, j, col // width))

    def left(width, col):
        return pl.BlockSpec((1, 8, width), lambda i, j: (i, jnp.maximum(j * rb - 1, 0), col // width))

    def right(width, col):
        return pl.BlockSpec((1, 8, width), lambda i, j: (i, jnp.minimum((j + 1) * rb, nrb - 1), col // width))

    pieces = ((SSD_INNER, AB_XS), (SSD_BC, AB_BM), (SSD_BC, AB_CM))
    return pl.pallas_call(
        _ab_prep_kernel,
        grid=(bsz, nt),
        in_specs=[cur(*p) for p in pieces] + [left(*p) for p in pieces] + [right(*p) for p in pieces] + [
            pl.BlockSpec((3, cw), lambda i, j: (0, 0)),
            pl.BlockSpec((1, cw), lambda i, j: (0, 0)),
        ],
        out_specs=pl.BlockSpec((1, TT, cw), lambda i, j: (i, j, 0)),
        out_shape=jax.ShapeDtypeStruct((bsz, t, cw), F32),
        compiler_params=pltpu.CompilerParams(dimension_semantics=("parallel", "parallel")),
        name="ab_prep",
    )(*([proj] * 9), conv_w, conv_b.reshape(1, cw))


def _gla_kernel(q_ref, k_ref, v_ref, sm_ref, wg_ref, bg_ref, o_ref, qg_s, egl_s, oi_s, u_s, st_s):
    d = pl.program_id(1)
    j = pl.program_id(2)
    is_fwd = d == 0

    @pl.when(j == 0)
    def _():
        st_s[...] = jnp.zeros_like(st_s)

    gz = _dot(sm_ref[0].astype(BF16), wg_ref[0]) + bg_ref[0]
    logg = (jnp.minimum(gz, 0.0) - jnp.log(1.0 + jnp.exp(-jnp.abs(gz)))) * (1.0 / GLA_GATE_TAU)
    m_cum, m_all = _chunk_masks(is_fwd, TT)
    gc = _sel_dot(m_cum, logg)
    gl = _sel_dot(m_all, logg)
    q = q_ref[0] * (GLA_DK ** -0.5)
    k = k_ref[0]
    qg = (q * jnp.exp(gc)).astype(BF16)
    kn = (k * jnp.exp(-gc)).astype(BF16)
    kd = (k * jnp.exp(gl - gc)).astype(BF16)
    qg_s[...] = qg
    egl_s[...] = jnp.exp(gl)
    causal = _causal_mask(is_fwd, CHUNK)

    pairs = [(h, ci) for h in range(GLA_HEADS) for ci in range(CPT)]
    rows_of = lambda ci: slice(ci * CHUNK, (ci + 1) * CHUNK)
    kcols = lambda h: slice(h * GLA_DK, (h + 1) * GLA_DK)
    vcols = lambda h: slice(h * GLA_DV, (h + 1) * GLA_DV)
    vs = [v_ref[0, rows_of(ci), vcols(h)] for h, ci in pairs]
    atts = [jnp.where(causal, _dot_nt(qg[rows_of(ci), kcols(h)], kn[rows_of(ci), kcols(h)]), 0.0).astype(BF16)
            for h, ci in pairs]
    for n, (h, ci) in enumerate(pairs):
        oi_s[rows_of(ci), vcols(h)] = _dot(atts[n], vs[n].astype(BF16))
        u_s[n] = _dot(vs[n].T.astype(BF16), kd[rows_of(ci), kcols(h)])

    for ci in range(CPT):
        cidx = jnp.where(is_fwd, ci, CPT - 1 - ci)
        off = pl.multiple_of(cidx * CHUNK, CHUNK)
        rows = pl.ds(off, CHUNK)
        for h in range(GLA_HEADS):
            st = st_s[h]
            o_ref[0, 0, rows, vcols(h)] = (oi_s[rows, vcols(h)]
                                           + _dot_nt(qg_s[rows, kcols(h)], st.astype(BF16))).astype(BF16)
            st_s[h] = st * egl_s[pl.ds(off, 1), kcols(h)] + u_s[h * CPT + cidx]


def gla_scan(proj, wg, bg):
    bsz, t, _ = proj.shape
    nt = t // TT
    tile = lambda d, j: _scan_tile(d, j, nt)
    return pl.pallas_call(
        _gla_kernel,
        grid=(bsz, 2, nt),
        in_specs=[
            pl.BlockSpec((1, TT, GLA_QK), lambda i, d, j: (i, tile(d, j), AB_Q // GLA_QK)),
            pl.BlockSpec((1, TT, GLA_QK), lambda i, d, j: (i, tile(d, j), AB_K // GLA_QK)),
            pl.BlockSpec((1, TT, GLA_V), lambda i, d, j: (i, tile(d, j), AB_V // GLA_V)),
            pl.BlockSpec((1, TT, LANE), lambda i, d, j: (i, tile(d, j), AB_SMALL // LANE)),
            pl.BlockSpec((1, LANE, GLA_QK), lambda i, d, j: (d, 0, 0)),
            pl.BlockSpec((1, 1, GLA_QK), lambda i, d, j: (d, 0, 0)),
        ],
        out_specs=pl.BlockSpec((1, 1, TT, GLA_V), lambda i, d, j: (d, i, tile(d, j), 0)),
        out_shape=jax.ShapeDtypeStruct((2, bsz, t, GLA_V), BF16),
        scratch_shapes=[pltpu.VMEM((TT, GLA_QK), BF16), pltpu.VMEM((TT, GLA_QK), F32),
                        pltpu.VMEM((TT, GLA_V), F32), pltpu.VMEM((GLA_HEADS * CPT, GLA_DV, GLA_DK), F32),
                        pltpu.VMEM((GLA_HEADS, GLA_DV, GLA_DK), F32)],
        compiler_params=pltpu.CompilerParams(
            dimension_semantics=("parallel", "parallel", "arbitrary")),
        name="gla_scan",
    )(proj, proj, proj, proj, wg, bg)


def _dot_sel2(x, e):
    hi = x.astype(BF16)
    return _dot(hi, e) + _dot((x - hi.astype(F32)).astype(BF16), e)


def _ssd_kernel(xs_ref, bm_ref, cm_ref, sm_ref, dtb_ref, nega_ref, e_ref, o_ref,
                v_s, vw_s, cdec_s, dec_s, ah_s, st_s):
    d = pl.program_id(1)
    j = pl.program_id(2)
    is_fwd = d == 0
    gw = SSD_HPG * SSD_HEADDIM

    @pl.when(j == 0)
    def _():
        st_s[...] = jnp.zeros_like(st_s)

    dt = _softplus(sm_ref[0] + dtb_ref[0])
    la = dt * nega_ref[0]
    m_cum, m_all = _chunk_masks(is_fwd, TT)
    acum = _sel_dot(m_cum, la)
    atot = _sel_dot(m_all, la)
    e = e_ref[0]
    v = xs_ref[0] * _dot_sel2(dt, e)
    v_s[...] = v.astype(BF16)
    vw_s[...] = (v * _dot_sel2(jnp.exp(atot - acum), e)).astype(BF16)
    cdec_s[...] = _dot_sel2(jnp.exp(acum), e)
    dec_s[...] = _dot_sel2(jnp.exp(atot), e)
    ah_s[...] = pltpu.roll(acum, LANE - SM_DT - d * SSD_HEADS, 1)
    causal = _causal_mask(is_fwd, CHUNK)

    for ci in range(CPT):
        off = pl.multiple_of(jnp.where(is_fwd, ci, CPT - 1 - ci) * CHUNK, CHUNK)
        rows = pl.ds(off, CHUNK)
        ah = ah_s[rows, :]
        aht = ah.T
        for g in range(SSD_GROUPS):
            gc = slice(g * gw, (g + 1) * gw)
            nc = slice(g * SSD_STATE, (g + 1) * SSD_STATE)
            bm = bm_ref[0, rows, nc]
            cm = cm_ref[0, rows, nc].astype(BF16)
            st = st_s[g]
            cb = _dot_nt(cm, bm.astype(BF16))
            y_inter = _dot(cm, st.astype(BF16)) * cdec_s[rows, gc]
            v_c = v_s[rows, gc]
            ys = []
            for hh in range(SSD_HPG):
                h = g * SSD_HPG + hh
                seg = jnp.exp(jnp.where(causal, ah[:, h:h + 1] - aht[h:h + 1, :], -1e30))
                ys.append(_dot((seg * cb).astype(BF16), v_c[:, hh * SSD_HEADDIM:(hh + 1) * SSD_HEADDIM]))
            o_ref[0, 0, rows, gc] = (jnp.concatenate(ys, axis=1) + y_inter).astype(BF16)
            st_s[g] = st * dec_s[pl.ds(off, 1), gc] + _dot(bm.T.astype(BF16), vw_s[rows, gc])


def ssd_scan(xbc, proj, dtb, nega, e):
    bsz, t, _ = xbc.shape
    nt = t // TT
    gw = SSD_HPG * SSD_HEADDIM
    tile = lambda d, j: _scan_tile(d, j, nt)
    return pl.pallas_call(
        _ssd_kernel,
        grid=(bsz, 2, nt),
        in_specs=[
            pl.BlockSpec((1, TT, SSD_INNER), lambda i, d, j: (i, tile(d, j), 0)),
            pl.BlockSpec((1, TT, SSD_BC), lambda i, d, j: (i, tile(d, j), SSD_INNER // SSD_BC)),
            pl.BlockSpec((1, TT, SSD_BC), lambda i, d, j: (i, tile(d, j), SSD_INNER // SSD_BC + 1)),
            pl.BlockSpec((1, TT, LANE), lambda i, d, j: (i, tile(d, j), AB_SMALL // LANE)),
            pl.BlockSpec((1, 1, LANE), lambda i, d, j: (d, 0, 0)),
            pl.BlockSpec((1, 1, LANE), lambda i, d, j: (d, 0, 0)),
            pl.BlockSpec((1, LANE, SSD_INNER), lambda i, d, j: (d, 0, 0)),
        ],
        out_specs=pl.BlockSpec((1, 1, TT, SSD_INNER), lambda i, d, j: (d, i, tile(d, j), 0)),
        out_shape=jax.ShapeDtypeStruct((2, bsz, t, SSD_INNER), BF16),
        scratch_shapes=[pltpu.VMEM((TT, SSD_INNER), BF16), pltpu.VMEM((TT, SSD_INNER), BF16),
                        pltpu.VMEM((TT, SSD_INNER), F32), pltpu.VMEM((TT, SSD_INNER), F32),
                        pltpu.VMEM((TT, LANE), F32), pltpu.VMEM((SSD_GROUPS, SSD_STATE, gw), F32)],
        compiler_params=pltpu.CompilerParams(
            dimension_semantics=("parallel", "parallel", "arbitrary")),
        name="ssd_scan",
    )(xbc, xbc, xbc, proj, dtb, nega, e)


def _group_rmsnorm(x, width):
    parts = []
    for s in range(x.shape[1] // width):
        seg = x[:, s * width:(s + 1) * width]
        parts.append(seg * lax.rsqrt(jnp.mean(seg * seg, axis=-1, keepdims=True) + 1e-6))
    return jnp.concatenate(parts, axis=1)


def _ab_out_kernel(of_ref, ob_ref, yf_ref, yb_ref, r_ref, z_ref, xs_ref, gg_ref, dv_ref, sg_ref,
                   w_ref, b_ref, x_ref, gate_ref, o_ref):
    o = _group_rmsnorm(of_ref[0, 0].astype(F32) + ob_ref[0, 0].astype(F32), GLA_DV) * gg_ref[...] * _silu(r_ref[0])
    y = (yf_ref[0, 0].astype(F32) + yb_ref[0, 0].astype(F32) + dv_ref[...] * xs_ref[0]) * _silu(z_ref[0])
    y = _group_rmsnorm(y, SSD_INNER // SSD_GROUPS) * sg_ref[...]
    m = _dot(o.astype(BF16), w_ref[0:GLA_V, :]) + _dot(y.astype(BF16), w_ref[GLA_V:, :]) + b_ref[...]
    o_ref[0] = x_ref[0] + gate_ref[0, 0] * m


def ab_out(o_gla, y_ssd, proj, xbc, gla_g, d_vec, ssd_g, w, b, x, gate):
    bsz, t, d = x.shape
    seg = lambda i, j: (i, jnp.minimum(j, 1), 0, 0)
    row = lambda width: pl.BlockSpec((1, width), lambda i, j: (0, 0))
    return pl.pallas_call(
        _ab_out_kernel,
        grid=(bsz, t // TT),
        in_specs=[
            pl.BlockSpec((1, 1, TT, GLA_V), lambda i, j: (0, i, j, 0)),
            pl.BlockSpec((1, 1, TT, GLA_V), lambda i, j: (1, i, j, 0)),
            pl.BlockSpec((1, 1, TT, SSD_INNER), lambda i, j: (0, i, j, 0)),
            pl.BlockSpec((1, 1, TT, SSD_INNER), lambda i, j: (1, i, j, 0)),
            pl.BlockSpec((1, TT, GLA_V), lambda i, j: (i, j, AB_R // GLA_V)),
            pl.BlockSpec((1, TT, SSD_INNER), lambda i, j: (i, j, AB_Z // SSD_INNER)),
            pl.BlockSpec((1, TT, SSD_INNER), lambda i, j: (i, j, 0)),
            row(GLA_V), row(SSD_INNER), row(SSD_INNER),
            pl.BlockSpec((GLA_V + SSD_INNER, d), lambda i, j: (0, 0)),
            row(d),
            pl.BlockSpec((1, TT, d), lambda i, j: (i, j, 0)),
            pl.BlockSpec((1, 1, 1, d), seg),
        ],
        out_specs=pl.BlockSpec((1, TT, d), lambda i, j: (i, j, 0)),
        out_shape=jax.ShapeDtypeStruct((bsz, t, d), F32),
        compiler_params=pltpu.CompilerParams(
            dimension_semantics=("parallel", "parallel"), vmem_limit_bytes=VMEM_LIMIT),
        name="ab_out",
    )(o_gla, o_gla, y_ssd, y_ssd, proj, proj, xbc, gla_g.reshape(1, -1), d_vec.reshape(1, -1),
      ssd_g.reshape(1, -1), w, b.reshape(1, d), x, gate)


def _cd_prep_kernel(q_ref, k_ref, v_ref, ql_ref, kl_ref, vl_ref, qr_ref, kr_ref, vr_ref, w_ref, o_ref):
    j = pl.program_id(1)
    nt = pl.num_programs(1)
    left_ok = j >= 2
    right_ok = jnp.logical_and(j >= 1, j < nt - 1)
    w = w_ref[...]
    srcs = ((q_ref, ql_ref, qr_ref, GDN_DK ** -0.5), (k_ref, kl_ref, kr_ref, 1.0), (v_ref, vl_ref, vr_ref, None))
    for s, (c_ref, l_ref, r_ref, scale) in enumerate(srcs):
        c0 = s * GDN_QK
        y = _conv3_piece(c_ref[0], l_ref[0], r_ref[0], w[:, c0:c0 + GDN_QK], 0.0, left_ok, right_ok)
        if scale is None:
            o_ref[0, :, c0:c0 + GDN_QK] = y
            continue
        for h in range(GDN_HEADS):
            seg = y[:, h * GDN_DK:(h + 1) * GDN_DK]
            inv = lax.rsqrt(jnp.sum(seg * seg, axis=-1, keepdims=True) + 1e-6) * scale
            o_ref[0, :, c0 + h * GDN_DK:c0 + (h + 1) * GDN_DK] = seg * inv


def cd_prep(proj, conv_w):
    bsz, t, _ = proj.shape
    nt = t // TT
    rb = TT // 8
    nrb = t // 8
    width = GDN_QK
    cols = (2, 3, 4)

    cur = lambda cb: pl.BlockSpec((1, TT, width), lambda i, j: (i, j, cb))
    left = lambda cb: pl.BlockSpec((1, 8, width), lambda i, j: (i, jnp.maximum(j * rb - 1, 0), cb))
    right = lambda cb: pl.BlockSpec((1, 8, width), lambda i, j: (i, jnp.minimum((j + 1) * rb, nrb - 1), cb))
    return pl.pallas_call(
        _cd_prep_kernel,
        grid=(bsz, nt),
        in_specs=[cur(cb) for cb in cols] + [left(cb) for cb in cols] + [right(cb) for cb in cols] + [
            pl.BlockSpec((3, 3 * width), lambda i, j: (0, 0))],
        out_specs=pl.BlockSpec((1, TT, 3 * width), lambda i, j: (i, j, 0)),
        out_shape=jax.ShapeDtypeStruct((bsz, t, 3 * width), F32),
        compiler_params=pltpu.CompilerParams(dimension_semantics=("parallel", "parallel")),
        name="cd_prep",
    )(*([proj] * 9), conv_w)


GDN_HB = 8


def _mm2(a, b):
    return _dot(a.astype(BF16), b.astype(BF16))


def _unit_tri_inverse(mats, b16, b32, eye):
    each = lambda f, *ls: [f(*xs) for xs in zip(*ls)]
    d16 = each(lambda a: jnp.where(b16, a, 0.0), mats)
    d2 = each(lambda x: _mm2(x, x), d16)
    d4 = each(lambda x: _mm2(x, x), d2)
    d8 = each(lambda x: _mm2(x, x), d4)
    t = each(lambda x: eye - x, d16)
    for p in (d2, d4, d8):
        t = each(lambda x, y: x + _mm2(x, y), t, p)
    off32 = jnp.logical_and(b32, jnp.logical_not(b16))
    for sel in (off32, jnp.logical_not(b32)):
        a_off = each(lambda a: jnp.where(sel, a, 0.0), mats)
        inner = each(_mm2, a_off, t)
        t = each(lambda x, y: x - _mm2(x, y), t, inner)
    return t


def _gdn_kernel(q_ref, k_ref, v_ref, sm_ref, dtb_ref, nega_ref, o_ref,
                dec_s, n_s, p_s, oc_s, qp_s, st_s):
    d = pl.program_id(2)
    j = pl.program_id(3)
    is_fwd = d == 0

    @pl.when(j == 0)
    def _():
        st_s[...] = jnp.zeros_like(st_s)

    sm = sm_ref[0]
    m_cum, m_all = _chunk_masks(is_fwd, TT)
    first = d * GDN_HEADS + pl.program_id(1) * GDN_HB
    rot = jnp.where(first == 0, 0, LANE - first)
    la = pltpu.roll(_softplus(sm + dtb_ref[...]) * nega_ref[...], rot, 1)
    be_sm = pltpu.roll(jax.nn.sigmoid(sm), rot, 1)
    gc_sm = _sel_dot(m_cum, la)
    gl_sm = _sel_dot(m_all, la)
    lane_bcast = lambda a, col: jnp.broadcast_to(a[:, col:col + 1], (TT, GDN_DK))

    r = lax.broadcasted_iota(jnp.int32, (CHUNK, CHUNK), 0)
    c = lax.broadcasted_iota(jnp.int32, (CHUNK, CHUNK), 1)
    causal = _causal_mask(is_fwd, CHUNK)
    strict = jnp.logical_and(causal, r != c)
    b16 = lax.shift_right_logical(r, 4) == lax.shift_right_logical(c, 4)
    b32 = lax.shift_right_logical(r, 5) == lax.shift_right_logical(c, 5)
    eye = jnp.where(r == c, 1.0, 0.0)
    chunk_rows = [slice(ci * CHUNK, (ci + 1) * CHUNK) for ci in range(CPT)]

    amats, rhss, aqks, kdts, qgs = [], [], [], [], []
    for hh in range(GDN_HB):
        cols = slice(hh * GDN_DK, (hh + 1) * GDN_DK)
        gc = lane_bcast(gc_sm, hh)
        gl = lane_bcast(gl_sm, hh)
        beta_e = lane_bcast(be_sm, 2 * GDN_HEADS + hh)
        q = q_ref[0, :, cols]
        k = k_ref[0, :, cols]
        egc = jnp.exp(gc)
        kb = k * beta_e
        qg = q * egc
        kd = k * jnp.exp(gl - gc)
        dec_s[:, cols] = jnp.exp(gl)
        rhs = jnp.concatenate([v_ref[0, :, cols] * beta_e, kb * egc], axis=1)
        for rows in chunk_rows:
            gcc = gc[rows, :]
            dmat = jnp.exp(jnp.where(causal, gcc[:, 0:CHUNK] - gcc.T[0:CHUNK, :], -1e30))
            kc = k[rows].astype(BF16)
            amats.append(jnp.where(strict, _dot_nt(kb[rows].astype(BF16), kc) * dmat, 0.0))
            rhss.append(rhs[rows])
            aqks.append((_dot_nt(q[rows].astype(BF16), kc) * dmat).astype(BF16))
            kdts.append(kd[rows].T.astype(BF16))
            qgs.append(qg[rows])
    tinv = _unit_tri_inverse(amats, b16, b32, eye)
    sols = [_mm2(t, rhs_c).astype(BF16) for t, rhs_c in zip(tinv, rhss)]
    for n, sol in enumerate(sols):
        hh, ci = n // CPT, n % CPT
        rows, cols = chunk_rows[ci], slice(hh * GDN_DK, (hh + 1) * GDN_DK)
        ks = _dot(kdts[n], sol)
        qs = _dot(aqks[n], sol)
        n_s[n] = ks[:, 0:GDN_DV]
        p_s[n] = ks[:, GDN_DV:].astype(BF16)
        oc_s[rows, cols] = qs[:, 0:GDN_DV]
        qp_s[rows, cols] = (qgs[n] - qs[:, GDN_DV:]).astype(BF16)

    for ci in range(CPT):
        cidx = jnp.where(is_fwd, ci, CPT - 1 - ci)
        off = pl.multiple_of(cidx * CHUNK, CHUNK)
        rows = pl.ds(off, CHUNK)
        for hh in range(GDN_HB):
            cols = slice(hh * GDN_DK, (hh + 1) * GDN_DK)
            st = st_s[hh]
            stb = st.astype(BF16)
            o_ref[0, 0, rows, cols] = (_dot(qp_s[rows, cols], stb) + oc_s[rows, cols]).astype(BF16)
            st_s[hh] = st * dec_s[pl.ds(off, 1), cols] - _dot(p_s[hh * CPT + cidx], stb) + n_s[hh * CPT + cidx]


def gdn_scan(qkv, proj, dtb, nega):
    bsz, t, _ = qkv.shape
    nt = t // TT
    tile = lambda d, j: _scan_tile(d, j, nt)
    ng = GDN_HEADS // GDN_HB
    wb = GDN_HB * GDN_DK
    return pl.pallas_call(
        _gdn_kernel,
        grid=(bsz, ng, 2, nt),
        in_specs=[
            pl.BlockSpec((1, TT, wb), lambda i, h, d, j: (i, tile(d, j), h)),
            pl.BlockSpec((1, TT, wb), lambda i, h, d, j: (i, tile(d, j), ng + h)),
            pl.BlockSpec((1, TT, wb), lambda i, h, d, j: (i, tile(d, j), 2 * ng + h)),
            pl.BlockSpec((1, TT, LANE), lambda i, h, d, j: (i, tile(d, j), 6 * D_MODEL // LANE)),
            pl.BlockSpec((1, LANE), lambda i, h, d, j: (0, 0)),
            pl.BlockSpec((1, LANE), lambda i, h, d, j: (0, 0)),
        ],
        out_specs=pl.BlockSpec((1, 1, TT, wb), lambda i, h, d, j: (d, i, tile(d, j), h)),
        out_shape=jax.ShapeDtypeStruct((2, bsz, t, GDN_V), BF16),
        scratch_shapes=[pltpu.VMEM((TT, wb), F32),
                        pltpu.VMEM((GDN_HB * CPT, GDN_DK, GDN_DV), F32),
                        pltpu.VMEM((GDN_HB * CPT, GDN_DK, GDN_DK), BF16),
                        pltpu.VMEM((TT, wb), F32), pltpu.VMEM((TT, wb), BF16),
                        pltpu.VMEM((GDN_HB, GDN_DK, GDN_DV), F32)],
        compiler_params=pltpu.CompilerParams(
            dimension_semantics=("parallel", "parallel", "parallel", "arbitrary")),
        name="gdn_scan",
    )(qkv, qkv, qkv, proj, dtb, nega)


CONF_PAD = 16


def _cd_out_kernel(ga_ref, gb_ref, og_ref, of_ref, ob_ref, cw_ref, cb_ref, lg_ref, lb_ref, ng_ref,
                   w_ref, b_ref, x_ref, gate_ref, o_ref, pad_s, shift_s, conv_s):
    j = pl.program_id(1)
    half = (CONF_KERNEL - 1) // 2
    glu = ga_ref[0] * jax.nn.sigmoid(gb_ref[0])
    zeros = jnp.zeros((CONF_PAD, CONF_CH), F32)

    def conv_segments(seglen):
        stride = seglen + 2 * CONF_PAD
        for g in range(TT // seglen):
            base = g * stride
            pad_s[base:base + CONF_PAD, :] = zeros
            pad_s[base + CONF_PAD:base + CONF_PAD + seglen, :] = glu[g * seglen:(g + 1) * seglen]
            pad_s[base + CONF_PAD + seglen:base + stride, :] = zeros
        nseg = TT // seglen
        used = nseg * stride
        conv_s[...] = jnp.zeros((TT, CONF_CH), F32) + cb_ref[...]
        for phase in range(SUBLANES):
            taps = [kk for kk in range(CONF_KERNEL) if (CONF_PAD - half + kk) % SUBLANES == phase]
            if phase:
                shift_s[0:used - SUBLANES, :] = pad_s[phase:used - SUBLANES + phase, :]
            src = shift_s if phase else pad_s
            for g in range(nseg):
                acc = conv_s[g * seglen:(g + 1) * seglen, :]
                for kk in taps:
                    lo = g * stride + CONF_PAD - half + kk - phase
                    acc = acc + src[lo:lo + seglen, :] * cw_ref[kk:kk + 1, :]
                conv_s[g * seglen:(g + 1) * seglen, :] = acc

    @pl.when(j == 0)
    def _():
        conv_segments(TT)

    @pl.when(j > 0)
    def _():
        conv_segments(GRID_W)

    acc = conv_s[...]
    mu = jnp.mean(acc, axis=-1, keepdims=True)
    cen = acc - mu
    var = jnp.mean(cen * cen, axis=-1, keepdims=True)
    conv = _silu(cen * lax.rsqrt(var + 1e-5) * lg_ref[...] + lb_ref[...])
    o = _group_rmsnorm(of_ref[0, 0].astype(F32) + ob_ref[0, 0].astype(F32), GDN_DV) * ng_ref[...] * _silu(og_ref[0])
    m = _dot(conv.astype(BF16), w_ref[0:CONF_CH, :]) + _dot(o.astype(BF16), w_ref[CONF_CH:, :]) + b_ref[...]
    o_ref[0] = x_ref[0] + gate_ref[0, 0] * m


def cd_out(o_gdn, proj, conv_w, conv_b, ln_g, ln_b, norm_g, w, b, x, gate):
    bsz, t, d = x.shape
    seg = lambda i, j: (i, jnp.minimum(j, 1), 0, 0)
    row = lambda width: pl.BlockSpec((1, width), lambda i, j: (0, 0))
    return pl.pallas_call(
        _cd_out_kernel,
        grid=(bsz, t // TT),
        in_specs=[
            pl.BlockSpec((1, TT, CONF_CH), lambda i, j: (i, j, 0)),
            pl.BlockSpec((1, TT, CONF_CH), lambda i, j: (i, j, 1)),
            pl.BlockSpec((1, TT, GDN_V), lambda i, j: (i, j, 5)),
            pl.BlockSpec((1, 1, TT, GDN_V), lambda i, j: (0, i, j, 0)),
            pl.BlockSpec((1, 1, TT, GDN_V), lambda i, j: (1, i, j, 0)),
            pl.BlockSpec((CONF_KERNEL, CONF_CH), lambda i, j: (0, 0)),
            row(CONF_CH), row(CONF_CH), row(CONF_CH), row(GDN_V),
            pl.BlockSpec((CONF_CH + GDN_V, d), lambda i, j: (0, 0)),
            row(d),
            pl.BlockSpec((1, TT, d), lambda i, j: (i, j, 0)),
            pl.BlockSpec((1, 1, 1, d), seg),
        ],
        out_specs=pl.BlockSpec((1, TT, d), lambda i, j: (i, j, 0)),
        out_shape=jax.ShapeDtypeStruct((bsz, t, d), F32),
        scratch_shapes=[pltpu.VMEM(((TT // GRID_W) * (GRID_W + 2 * CONF_PAD), CONF_CH), F32),
                        pltpu.VMEM(((TT // GRID_W) * (GRID_W + 2 * CONF_PAD), CONF_CH), F32),
                        pltpu.VMEM((TT, CONF_CH), F32)],
        compiler_params=pltpu.CompilerParams(
            dimension_semantics=("parallel", "parallel"), vmem_limit_bytes=VMEM_LIMIT),
        name="cd_out",
    )(proj, proj, proj, o_gdn, o_gdn, conv_w, conv_b.reshape(1, -1), ln_g.reshape(1, -1), ln_b.reshape(1, -1),
      norm_g.reshape(1, -1), w, b.reshape(1, d), x, gate)


def _out_proj_kernel(m_ref, w_ref, b_ref, x_ref, gate_ref, o_ref):
    y = _dot(m_ref[0].astype(BF16), w_ref[...]) + b_ref[...]
    o_ref[0] = x_ref[0] + gate_ref[0] * y


def out_proj_residual(mixed, w, b, x, gate):
    bsz, t, k = mixed.shape
    d = w.shape[1]
    tm = min(t, 512)
    return pl.pallas_call(
        _out_proj_kernel,
        grid=(bsz, t // tm),
        in_specs=[
            pl.BlockSpec((1, tm, k), lambda i, j: (i, j, 0)),
            pl.BlockSpec((k, d), lambda i, j: (0, 0)),
            pl.BlockSpec((1, d), lambda i, j: (0, 0)),
            pl.BlockSpec((1, tm, d), lambda i, j: (i, j, 0)),
            pl.BlockSpec((1, 1, d), lambda i, j: (i, 0, 0)),
        ],
        out_specs=pl.BlockSpec((1, tm, d), lambda i, j: (i, j, 0)),
        out_shape=jax.ShapeDtypeStruct((bsz, t, d), F32),
        compiler_params=pltpu.CompilerParams(
            dimension_semantics=("parallel", "parallel"), vmem_limit_bytes=VMEM_LIMIT),
        name="out_proj",
    )(mixed, w, b.reshape(1, d), x, gate.reshape(bsz, 1, d))


def _router_kernel(x_ref, g_ref, sh_ref, sc_ref, rw_ref, h_ref, aff_ref):
    x = x_ref[0]
    ms = jnp.mean(x * x, axis=-1, keepdims=True)
    h = (x * lax.rsqrt(ms + 1e-6) * g_ref[...] * (1.0 + sc_ref[0, 0]) + sh_ref[0, 0]).astype(BF16)
    h_ref[0] = h
    logits = _dot(h, rw_ref[...])
    lane = lax.broadcasted_iota(jnp.int32, logits.shape, 1)
    logits = jnp.where(lane < N_EXPERTS, logits, -1e30)
    e = jnp.exp(logits - jnp.max(logits, axis=-1, keepdims=True))
    aff = e / jnp.sum(e, axis=-1, keepdims=True)
    aff_ref[0] = aff.T[0:N_EXPERTS, :]


def moe_router(xa, g, shift, scale, rw, seg, tile0, ntiles):
    bsz, _, d = xa.shape
    n = ntiles * TT
    return pl.pallas_call(
        _router_kernel,
        grid=(bsz, ntiles),
        in_specs=[
            pl.BlockSpec((1, TT, d), lambda i, j: (i, j + tile0, 0)),
            pl.BlockSpec((1, d), lambda i, j: (0, 0)),
            pl.BlockSpec((1, 1, 1, d), lambda i, j: (i, seg, 0, 0)),
            pl.BlockSpec((1, 1, 1, d), lambda i, j: (i, seg, 0, 0)),
            pl.BlockSpec((d, LANE), lambda i, j: (0, 0)),
        ],
        out_specs=[pl.BlockSpec((1, TT, d), lambda i, j: (i, j, 0)),
                   pl.BlockSpec((1, N_EXPERTS, TT), lambda i, j: (i, 0, j))],
        out_shape=[jax.ShapeDtypeStruct((bsz, n, d), BF16), jax.ShapeDtypeStruct((bsz, N_EXPERTS, n), F32)],
        compiler_params=pltpu.CompilerParams(dimension_semantics=("parallel", "parallel")),
        name="moe_router",
    )(xa, g.reshape(1, d), shift, scale, rw)


def _lane_block_prefix(x, u_strict):
    nblk = x.shape[1] // LANE
    run = jnp.zeros((x.shape[0], 1), F32)
    outs = []
    for cblk in range(nblk):
        xc = x[:, cblk * LANE:(cblk + 1) * LANE]
        outs.append(_dot(xc.astype(BF16), u_strict) + run)
        run = run + jnp.sum(xc, axis=-1, keepdims=True)
    return jnp.concatenate(outs, axis=1), run


def _select_kernel(aff_ref, slot_ref, *, cap):
    aff = aff_ref[0]
    bits = pltpu.bitcast(aff, jnp.int32)
    capf = jnp.float32(cap)

    def step(i, thr):
        cand = jnp.bitwise_or(thr, lax.shift_left(jnp.int32(1), 30 - i))
        cnt = jnp.sum(jnp.where(bits >= cand, 1.0, 0.0), axis=-1, keepdims=True)
        return jnp.where(cnt >= capf, cand, thr)

    thr = lax.fori_loop(0, 31, step, jnp.zeros((aff.shape[0], 1), jnp.int32))
    gt = jnp.where(bits > thr, 1.0, 0.0)
    eq = jnp.where(bits == thr, 1.0, 0.0)
    r = lax.broadcasted_iota(jnp.int32, (LANE, LANE), 0)
    c = lax.broadcasted_iota(jnp.int32, (LANE, LANE), 1)
    u_strict = jnp.where(r < c, 1.0, 0.0).astype(BF16)
    need = capf - jnp.sum(gt, axis=-1, keepdims=True)
    eq_rank, _ = _lane_block_prefix(eq, u_strict)
    sel = jnp.maximum(gt, jnp.where(eq_rank < need, eq, 0.0))
    slot, _ = _lane_block_prefix(sel, u_strict)
    slot_ref[0] = jnp.where(sel > 0.0, slot.astype(jnp.int32), -1)


def moe_select(aff, cap):
    bsz, ne, n = aff.shape
    return pl.pallas_call(
        functools.partial(_select_kernel, cap=cap),
        grid=(bsz,),
        in_specs=[pl.BlockSpec((1, ne, n), lambda i: (i, 0, 0))],
        out_specs=pl.BlockSpec((1, ne, n), lambda i: (i, 0, 0)),
        out_shape=jax.ShapeDtypeStruct((bsz, ne, n), jnp.int32),
        compiler_params=pltpu.CompilerParams(dimension_semantics=("parallel",)),
        name="moe_select",
    )(aff)


def _slot_index_kernel(slot_ref, idx_ref, *, cap):
    slot = slot_ref[0]
    n = slot.shape[1]
    srow = lax.broadcasted_iota(jnp.int32, (cap, LANE), 0)
    lane = lax.broadcasted_iota(jnp.int32, (cap, LANE), 1)
    acc = jnp.zeros((cap, LANE), jnp.int32)
    for cblk in range(n // LANE):
        s_c = slot[:, cblk * LANE:(cblk + 1) * LANE]
        acc = acc + jnp.where(srow == s_c, lane + (cblk * LANE + 1), 0)
    ones = jnp.ones((8, LANE), BF16)
    hi = _dot_nt(ones, lax.shift_right_logical(acc, 7).astype(F32).astype(BF16))
    lo = _dot_nt(ones, jnp.bitwise_and(acc, LANE - 1).astype(F32).astype(BF16))
    idx_ref[0] = (hi[0:1, :] * float(LANE) + lo[0:1, :]).astype(jnp.int32) - 1


SLOT_WIN = 2 * LANE


def _slot_index_win_kernel(base_ref, slot_ref, idx_ref, acc_s, *, cap):
    i = pl.program_id(0)
    n = slot_ref.shape[2]
    acc_s[...] = jnp.zeros(acc_s.shape, jnp.int32)
    srow = lax.broadcasted_iota(jnp.int32, (SLOT_WIN, LANE), 0)
    lane = lax.broadcasted_iota(jnp.int32, (SLOT_WIN, LANE), 1)
    for cblk in range(n // LANE):
        base = pl.multiple_of(base_ref[i, cblk], LANE)
        s_c = slot_ref[0, :, cblk * LANE:(cblk + 1) * LANE]
        rows = pl.ds(base, SLOT_WIN)
        acc_s[rows, :] = acc_s[rows, :] + jnp.where(srow + base == s_c, lane + (cblk * LANE + 1), 0)
    acc = acc_s[0:cap, :]
    ones = jnp.ones((8, LANE), BF16)
    hi = _dot_nt(ones, lax.shift_right_logical(acc, 7).astype(F32).astype(BF16))
    lo = _dot_nt(ones, jnp.bitwise_and(acc, LANE - 1).astype(F32).astype(BF16))
    idx_ref[0] = (hi[0:1, :] * float(LANE) + lo[0:1, :]).astype(jnp.int32) - 1


def moe_slot_index_windowed(slot, base, cap):
    bsz, ne, n = slot.shape
    idx = pl.pallas_call(
        functools.partial(_slot_index_win_kernel, cap=cap),
        grid_spec=pltpu.PrefetchScalarGridSpec(
            num_scalar_prefetch=1,
            grid=(bsz * ne,),
            in_specs=[pl.BlockSpec((1, 1, n), lambda i, base_ref: (i, 0, 0))],
            out_specs=pl.BlockSpec((1, 1, cap), lambda i, base_ref: (i, 0, 0)),
            scratch_shapes=[pltpu.VMEM((cap + LANE, LANE), jnp.int32)],
        ),
        out_shape=jax.ShapeDtypeStruct((bsz * ne, 1, cap), jnp.int32),
        compiler_params=pltpu.CompilerParams(dimension_semantics=("arbitrary",)),
        name="moe_slot_index",
    )(base, slot.reshape(bsz * ne, 1, n))
    return idx.reshape(bsz, ne, cap)


def moe_slot_index(slot, cap):
    bsz, ne, n = slot.shape
    idx = pl.pallas_call(
        functools.partial(_slot_index_kernel, cap=cap),
        grid=(bsz * ne,),
        in_specs=[pl.BlockSpec((1, 1, n), lambda i: (i, 0, 0))],
        out_specs=pl.BlockSpec((1, 1, cap), lambda i: (i, 0, 0)),
        out_shape=jax.ShapeDtypeStruct((bsz * ne, 1, cap), jnp.int32),
        compiler_params=pltpu.CompilerParams(dimension_semantics=("parallel",)),
        name="moe_slot_index",
    )(slot.reshape(bsz * ne, 1, n))
    return idx.reshape(bsz, ne, cap)


WIN_ALIGN = 16
WIN_FAST = 128


def _combine_kernel(ws_ref, slot_ref, aff_ref, *rest, win, final):
    ye_refs, (x_ref, gate_ref), o_ref = rest[:N_EXPERTS], rest[N_EXPERTS:N_EXPERTS + 2], rest[-1]
    b = pl.program_id(0)
    j = pl.program_id(1)
    srow = lax.broadcasted_iota(jnp.int32, (win, TT), 0)
    his, los = [], []
    for e in range(N_EXPERTS):
        sel = jnp.where(srow + ws_ref[b, e, j] == slot_ref[0, e:e + 1, :], aff_ref[0, e:e + 1, :], 0.0).T
        hi = sel.astype(BF16)
        his.append(hi)
        los.append((sel - hi.astype(F32)).astype(BF16))
    ye = jnp.concatenate([r[...] for r in ye_refs], axis=0)
    acc = _dot(jnp.concatenate(his, axis=1), ye) + _dot(jnp.concatenate(los, axis=1), ye)
    y = x_ref[0] + gate_ref[0, 0] * acc
    if final:
        g_ref = rest[N_EXPERTS + 2]
        y = y * lax.rsqrt(jnp.mean(y * y, axis=-1, keepdims=True) + 1e-6) * g_ref[...]
    o_ref[0] = y


def moe_combine(ws, slot, aff, ye, xa, gate, seg, tile0, win, final_g=None):
    bsz, ne, n = slot.shape
    nt = n // TT
    d = xa.shape[2]

    def ye_spec(e):
        return pl.BlockSpec((pl.Squeezed(), pl.Squeezed(), pl.Element(win), pl.Element(d)),
                            lambda i, j, ws_ref: (i, e, pl.multiple_of(ws_ref[i, e, j], WIN_ALIGN), 0))

    in_specs = ([pl.BlockSpec((1, ne, TT), lambda i, j, ws_ref: (i, 0, j)),
                 pl.BlockSpec((1, ne, TT), lambda i, j, ws_ref: (i, 0, j))]
                + [ye_spec(e) for e in range(ne)]
                + [pl.BlockSpec((1, TT, d), lambda i, j, ws_ref: (i, j + tile0, 0)),
                   pl.BlockSpec((1, 1, 1, d), lambda i, j, ws_ref: (i, seg, 0, 0))])
    args = (ws, slot, aff, *([ye] * ne), xa, gate)
    if final_g is None:
        out_idx, out_shape, aliases = (lambda i, j, ws_ref: (i, j + tile0, 0)), xa.shape, {3 + ne: 0}
    else:
        out_idx, out_shape, aliases = (lambda i, j, ws_ref: (i, j, 0)), (bsz, n, d), {}
        in_specs.append(pl.BlockSpec((1, d), lambda i, j, ws_ref: (0, 0)))
        args += (final_g.reshape(1, d),)
    return pl.pallas_call(
        functools.partial(_combine_kernel, win=win, final=final_g is not None),
        grid_spec=pltpu.PrefetchScalarGridSpec(
            num_scalar_prefetch=1,
            grid=(bsz, nt),
            in_specs=in_specs,
            out_specs=pl.BlockSpec((1, TT, d), out_idx),
        ),
        out_shape=jax.ShapeDtypeStruct(out_shape, F32),
        input_output_aliases=aliases,
        compiler_params=pltpu.CompilerParams(
            dimension_semantics=("parallel", "parallel"), vmem_limit_bytes=VMEM_LIMIT),
        name="moe_combine",
    )(*args)


def _expert_ffn_kernel(x_ref, w1_ref, w3_ref, w2_ref, o_ref):
    x = x_ref[0, 0]
    a = _dot(x, w1_ref[0, 0])
    g = _dot(x, w3_ref[0, 0])
    o_ref[0, 0] = _dot((_silu(a) * g).astype(BF16), w2_ref[0, 0]).astype(BF16)


def expert_ffn(xe, w1, w3, w2, layer):
    bsz, ne, cap, d = xe.shape
    f = w1.shape[3]
    tm = min(cap, 1024)
    return pl.pallas_call(
        _expert_ffn_kernel,
        grid=(ne, bsz, cap // tm),
        in_specs=[
            pl.BlockSpec((1, 1, tm, d), lambda e, i, j: (i, e, j, 0)),
            pl.BlockSpec((1, 1, d, f), lambda e, i, j: (layer, e, 0, 0)),
            pl.BlockSpec((1, 1, d, f), lambda e, i, j: (layer, e, 0, 0)),
            pl.BlockSpec((1, 1, f, d), lambda e, i, j: (layer, e, 0, 0)),
        ],
        out_specs=pl.BlockSpec((1, 1, tm, d), lambda e, i, j: (i, e, j, 0)),
        out_shape=jax.ShapeDtypeStruct((bsz, ne, cap, d), BF16),
        compiler_params=pltpu.CompilerParams(
            dimension_semantics=("parallel", "parallel", "parallel"), vmem_limit_bytes=VMEM_LIMIT),
        name="expert_ffn",
    )(xe, w1, w3, w2)


def _rmsnorm_kernel(x_ref, g_ref, o_ref):
    x = x_ref[0]
    ms = jnp.mean(x * x, axis=-1, keepdims=True)
    o_ref[0] = x * lax.rsqrt(ms + 1e-6) * g_ref[...]


def rmsnorm_rows(x, g, tile0):
    bsz, t, d = x.shape
    nt = t // TT - tile0
    return pl.pallas_call(
        _rmsnorm_kernel,
        grid=(bsz, nt),
        in_specs=[pl.BlockSpec((1, TT, d), lambda i, j: (i, j + tile0, 0)), pl.BlockSpec((1, d), lambda i, j: (0, 0))],
        out_specs=pl.BlockSpec((1, TT, d), lambda i, j: (i, j, 0)),
        out_shape=jax.ShapeDtypeStruct((bsz, nt * TT, d), F32),
        compiler_params=pltpu.CompilerParams(dimension_semantics=("parallel", "parallel")),
        name="final_rmsnorm",
    )(x, g.reshape(1, d))


def _ab_in_layout(w_in, b_in):
    q, k, v, r, glr, z, xs, bm, cm, dt = _split_cols(
        jnp.concatenate([w_in, b_in[None]], axis=0),
        (GLA_QK, GLA_QK, GLA_V, GLA_V, 2 * GLA_GATE_RANK, SSD_INNER, SSD_INNER, SSD_BC, SSD_BC, 2 * SSD_HEADS))
    wb = _pad_cols(jnp.concatenate([q, k, v, r, z, xs, bm, cm, glr, dt], axis=1), AB_N)
    return wb[:-1].astype(BF16), wb[-1]


def _gla_gate_params(w_gate2, b_gate2):
    wg = jnp.zeros((2, LANE, GLA_QK), F32)
    for d in range(2):
        wg = wg.at[d, d * GLA_GATE_RANK:(d + 1) * GLA_GATE_RANK, :].set(w_gate2[d])
    return wg.astype(BF16), b_gate2.reshape(2, 1, GLA_QK)


def _ssd_params(dt_bias, a_log):
    dtb = jnp.zeros((2, 1, LANE), F32)
    nega = jnp.zeros((2, 1, LANE), F32)
    e = np.zeros((2, LANE, SSD_INNER), np.float32)
    for d in range(2):
        c0 = SM_DT + d * SSD_HEADS
        dtb = dtb.at[d, 0, c0:c0 + SSD_HEADS].set(dt_bias[d])
        nega = nega.at[d, 0, c0:c0 + SSD_HEADS].set(-jnp.exp(a_log[d]))
        for h in range(SSD_HEADS):
            e[d, c0 + h, h * SSD_HEADDIM:(h + 1) * SSD_HEADDIM] = 1.0
    return dtb, nega, jnp.asarray(e, BF16)


def _gdn_params(dt_bias, a_log):
    n = 2 * GDN_HEADS
    dtb = jnp.zeros((1, LANE), F32).at[0, 0:n].set(dt_bias.reshape(n))
    nega = jnp.zeros((1, LANE), F32).at[0, 0:n].set(-jnp.exp(a_log.reshape(n)))
    return dtb, nega


def _rmsnorm(x, g, eps=1e-6):
    return x * lax.rsqrt(jnp.mean(jnp.square(x), axis=-1, keepdims=True) + eps) * g


def _head_rmsnorm(x, g):
    return _rmsnorm(x, g.reshape(x.shape[-2:]))


def _layernorm(x, g, b, eps=1e-5):
    mu = jnp.mean(x, axis=-1, keepdims=True)
    var = jnp.mean(jnp.square(x - mu), axis=-1, keepdims=True)
    return (x - mu) * lax.rsqrt(var + eps) * g + b


def _l2norm(x, eps=1e-6):
    return x * lax.rsqrt(jnp.sum(jnp.square(x), axis=-1, keepdims=True) + eps)


def _dwconv(x, w, b=None):
    k, ch = w.shape
    pad = (k - 1) // 2
    y = lax.conv_general_dilated(x, w[:, None, :], (1,), [(pad, pad)],
                                 dimension_numbers=('NWC', 'WIO', 'NWC'), feature_group_count=ch)
    return y if b is None else y + b


def _gdn_scan(q, k, v, beta, logg, s0, with_out):
    bsz, nh, t, dk = q.shape
    dv = v.shape[-1]
    nc = t // CHUNK
    chunks = lambda z: z.reshape(bsz, nh, nc, CHUNK, *z.shape[3:])
    q, k, v, beta, logg = (chunks(z) for z in (q, k, v, beta, logg))
    gc = jnp.cumsum(logg, axis=-1)
    glast = gc[..., -1]
    tril = jnp.tril(jnp.ones((CHUNK, CHUNK), dtype=bool))
    strict = jnp.tril(jnp.ones((CHUNK, CHUNK), dtype=bool), k=-1)
    decay = jnp.exp(jnp.where(tril, gc[..., :, None] - gc[..., None, :], -jnp.inf))
    kb = k * beta[..., None]
    m = jnp.eye(CHUNK, dtype=k.dtype) + jnp.where(strict, jnp.einsum('bhcid,bhcjd->bhcij', kb, k) * decay, 0.0)
    rhs = jnp.concatenate([v * beta[..., None], kb * jnp.exp(gc)[..., None]], axis=-1)
    sol = lax.linalg.triangular_solve(m, rhs, left_side=True, lower=True, unit_diagonal=True)
    u, w = sol[..., :dv], sol[..., dv:]
    kd = k * jnp.exp(glast[..., None] - gc)[..., None]
    dec = jnp.exp(glast)
    front = lambda z: jnp.moveaxis(z, 2, 0)

    def advance(s, w_c, u_c, kd_c, dec_c):
        vn = u_c - jnp.einsum('bhid,bhde->bhie', w_c, s)
        return vn, dec_c[..., None, None] * s + jnp.einsum('bhjd,bhje->bhde', kd_c, vn)

    xs = (front(w), front(u), front(kd), front(dec))
    if not with_out:
        s_fin, _ = lax.scan(lambda s, xc: (advance(s, *xc)[1], None), s0, xs)
        return None, s_fin
    aqk = jnp.einsum('bhcid,bhcjd->bhcij', q, k) * decay
    qg = q * jnp.exp(gc)[..., None]

    def step(s, xc):
        w_c, u_c, kd_c, dec_c, aqk_c, qg_c = xc
        vn, s_new = advance(s, w_c, u_c, kd_c, dec_c)
        o = jnp.einsum('bhid,bhde->bhie', qg_c, s) + jnp.einsum('bhij,bhje->bhie', aqk_c, vn)
        return s_new, o

    s_fin, o = lax.scan(step, s0, xs + (front(aqk), front(qg)))
    return jnp.moveaxis(o, 0, 2).reshape(bsz, nh, t, dv), s_fin


def _bidir_scan(scan_fn, args_f, args_b, t_axis, init_f, init_b, with_out):
    flip = lambda a: jnp.flip(a, axis=t_axis)
    o_f, s_f = scan_fn(*args_f, init_f, with_out)
    o_b, s_b = scan_fn(*[flip(a) for a in args_b], init_b, with_out)
    o = o_f + flip(o_b) if with_out else None
    return o, s_f, s_b


def _conformer_conv(u, rows, dw_w, dw_b, ln_g, ln_b):
    bsz, t, ch = u.shape
    y = _dwconv(u.reshape(bsz * rows, t // rows, ch), dw_w, dw_b).reshape(bsz, t, ch)
    return jax.nn.silu(_layernorm(y, ln_g, ln_b))


def _cd_stream(proj, gdn_conv_w, gdn_a_log, gdn_dt_bias):
    bsz, t, _ = proj.shape
    ga, gb, q, k, v, og, a_raw, b_raw = _split_cols(proj, CD_SPLITS)
    glu = ga * jax.nn.sigmoid(gb)
    qkv = jax.nn.silu(_dwconv(jnp.concatenate([q, k, v], axis=-1), gdn_conv_w))
    q, k, v = _split_cols(qkv, (GDN_QK, GDN_QK, GDN_V))
    heads = lambda a: a.reshape(bsz, t, GDN_HEADS, -1).transpose(0, 2, 1, 3)
    q = _l2norm(heads(q)) * GDN_DK ** -0.5
    k = _l2norm(heads(k))
    v = heads(v)
    beta = jax.nn.sigmoid(b_raw.reshape(bsz, t, 2, GDN_HEADS)).transpose(2, 0, 3, 1)
    logg = (-jnp.exp(gdn_a_log)
            * jax.nn.softplus(a_raw.reshape(bsz, t, 2, GDN_HEADS) + gdn_dt_bias)).transpose(2, 0, 3, 1)
    return (q, k, v, beta[0], logg[0]), (q, k, v, beta[1], logg[1]), glu, og


def _mixer_cd(proj_l, proj_c, rows, conf_dw_w, conf_dw_b, conf_ln_g, conf_ln_b,
              gdn_conv_w, gdn_a_log, gdn_dt_bias, gdn_norm_g, need_ctx):
    c_f, c_b, c_glu, c_og = _cd_stream(proj_c, gdn_conv_w, gdn_a_log, gdn_dt_bias)
    l_f, l_b, l_glu, l_og = _cd_stream(proj_l, gdn_conv_w, gdn_a_log, gdn_dt_bias)
    bsz = proj_l.shape[0]
    z0 = jnp.zeros((bsz, GDN_HEADS, GDN_DK, GDN_DV), F32)
    od_c, s_f, s_b = _bidir_scan(_gdn_scan, c_f, c_b, 2, z0, z0, need_ctx)
    od_l, _, _ = _bidir_scan(_gdn_scan, l_f, l_b, 2, s_f, s_b, True)

    def mix(od, glu, og, n_rows):
        t = glu.shape[1]
        conv = _conformer_conv(glu, n_rows, conf_dw_w, conf_dw_b, conf_ln_g, conf_ln_b)
        o = _head_rmsnorm(jnp.swapaxes(od, 1, 2), gdn_norm_g) * jax.nn.silu(og).reshape(bsz, t, GDN_HEADS, GDN_DV)
        return jnp.concatenate([conv, o.reshape(bsz, t, GDN_V)], axis=-1)

    return mix(od_l, l_glu, l_og, rows), (mix(od_c, c_glu, c_og, 1) if need_ctx else None)


def moe_segment(i, xa, mods, p, w1, w3, w2, seg, tile0, ntiles, final_g=None):
    bsz, _, d = xa.shape
    n = ntiles * TT
    cap = n * EC_CAPACITY // N_EXPERTS
    rw = _pad_cols(p["moe_router"][i], LANE).astype(BF16)
    h, aff = moe_router(xa, p["norm2_g"][i], mods[:, :, 3], mods[:, :, 4], rw, seg, tile0, ntiles)
    slot = moe_select(aff, cap)
    cnt128 = jnp.sum((slot >= 0).reshape(bsz, N_EXPERTS, n // LANE, LANE), axis=-1, dtype=jnp.int32)
    start128 = jnp.cumsum(cnt128, axis=-1) - cnt128
    if cap >= SLOT_WIN:
        base = jnp.minimum(start128 // LANE * LANE, cap - LANE).reshape(bsz * N_EXPERTS, n // LANE)
        idx = moe_slot_index_windowed(slot, base, cap)
    else:
        idx = moe_slot_index(slot, cap)
    idx = idx.reshape(bsz, N_EXPERTS * cap)
    xe = jnp.take_along_axis(h, idx[..., None], axis=1, mode="promise_in_bounds").reshape(bsz, N_EXPERTS, cap, d)
    ye = expert_ffn(xe, w1, w3, w2, i)
    per_tile = TT // LANE
    counts = jnp.sum(cnt128.reshape(bsz, N_EXPERTS, ntiles, per_tile), axis=-1)
    starts = start128[:, :, ::per_tile]
    aligned = starts // WIN_ALIGN * WIN_ALIGN

    def run(win):
        ws = jnp.minimum(aligned, cap - win)
        return moe_combine(ws, slot, aff, ye, xa, mods[:, :, 5], seg, tile0, win, final_g)

    win_fast, win_full = min(cap, WIN_FAST), min(cap, TT + WIN_ALIGN)
    if win_fast == win_full:
        return run(win_full)
    overflow = jnp.any(starts + counts - jnp.minimum(aligned, cap - win_fast) > win_fast)
    return lax.cond(overflow, lambda: run(win_full), lambda: run(win_fast))


def layer_mixer(i, j, xa, mods, p, last):
    sh1, sc1, g1 = (mods[:, :, s] for s in range(3))
    if i % 2 == 0:
        w_in, b_in = _ab_in_layout(p["ab_w_in"][j], p["ab_b_in"][j])
        proj = norm_proj(xa, p["norm1_g"][i], sh1, sc1, w_in, b_in, 1920)
        xbc = ab_prep(proj, p["ssd_conv_w"][j], p["ssd_conv_b"][j])
        wg, bg = _gla_gate_params(p["gla_w_gate2"][j], p["gla_b_gate2"][j])
        o_gla = gla_scan(proj, wg, bg)
        y_ssd = ssd_scan(xbc, proj, *_ssd_params(p["ssd_dt_bias"][j], p["ssd_a_log"][j]))
        return ab_out(o_gla, y_ssd, proj, xbc, p["gla_norm_g"][j], jnp.repeat(p["ssd_d"][j], SSD_HEADDIM),
                      p["ssd_norm_g"][j], p["ab_w_out"][j].astype(BF16), p["ab_b_out"][j], xa, g1)
    w_in = _pad_cols(p["cd_w_in"][j], CD_N).astype(BF16)
    b_in = _pad_cols(p["cd_b_in"][j], CD_N)
    proj = norm_proj(xa, p["norm1_g"][i], sh1, sc1, w_in, b_in, 1280)
    qkv = cd_prep(proj, p["gdn_conv_w"][j])
    o_gdn = gdn_scan(qkv, proj, *_gdn_params(p["gdn_dt_bias"][j], p["gdn_a_log"][j]))
    return cd_out(o_gdn, proj, p["conf_dw_w"][j], p["conf_dw_b"][j], p["conf_ln_g"][j], p["conf_ln_b"][j],
                  p["gdn_norm_g"][j], p["cd_w_out"][j].astype(BF16), p["cd_b_out"][j], xa, g1)


def kernel(x, c, ctx, c_ctx, mod_w, mod_b, norm1_g, norm2_g, ab_w_in, ab_b_in, ab_w_out, ab_b_out, gla_w_gate2, gla_b_gate2, gla_norm_g, ssd_conv_w, ssd_conv_b, ssd_dt_bias, ssd_a_log, ssd_d, ssd_norm_g, cd_w_in, cd_b_in, cd_w_out, cd_b_out, conf_dw_w, conf_dw_b, conf_ln_g, conf_ln_b, gdn_conv_w, gdn_a_log, gdn_dt_bias, gdn_norm_g, moe_router, moe_w1, moe_w3, moe_w2, final_norm_g):
    p = dict(locals())
    bsz, seq, d = x.shape
    assert ctx.shape[1] == TT and seq % TT == 0
    depth = mod_w.shape[0]
    xa = jnp.concatenate([ctx, x], axis=1)
    w1, w3, w2 = moe_w1.astype(BF16), moe_w3.astype(BF16), moe_w2.astype(BF16)
    cond = jnp.concatenate([c, c_ctx[None]], axis=0)
    for i in range(depth):
        last = i == depth - 1
        mod = mod_proj(cond, mod_w[i].astype(BF16), mod_b[i])
        mods = jnp.stack([jnp.broadcast_to(mod[bsz], (bsz, 6 * d)), mod[:bsz]], axis=1).reshape(bsz, 2, 6, 1, d)
        xa = layer_mixer(i, i // 2, xa, mods, p, last)
        if last:
            return moe_segment(i, xa, mods, p, w1, w3, w2, 1, 1, seq // TT, final_norm_g)
        xa = moe_segment(i, xa, mods, p, w1, w3, w2, 1, 1, seq // TT)
        xa = moe_segment(i, xa, mods, p, w1, w3, w2, 0, 0, 1)
```

```python
import functools

import jax
import jax.numpy as jnp
import numpy as np
from jax import lax
from jax.experimental import pallas as pl
from jax.experimental.pallas import tpu as pltpu

F32 = jnp.float32
BF16 = jnp.bfloat16

D_MODEL = 1024
GRID_W = 64
CHUNK = 64
GLA_HEADS, GLA_DK, GLA_DV, GLA_GATE_RANK, GLA_GATE_TAU = 4, 128, 256, 16, 16.0
SSD_HEADS, SSD_HEADDIM, SSD_STATE, SSD_GROUPS = 16, 64, 128, 2
CONF_CH, CONF_KERNEL = D_MODEL, 31
GDN_HEADS, GDN_DK, GDN_DV = 8, 128, 128
N_EXPERTS, EC_CAPACITY, EXPERT_FF = 16, 2, D_MODEL

GLA_QK = GLA_HEADS * GLA_DK
GLA_V = GLA_HEADS * GLA_DV
SSD_INNER = SSD_HEADS * SSD_HEADDIM
SSD_BC = SSD_GROUPS * SSD_STATE
SSD_HPG = SSD_HEADS // SSD_GROUPS
GDN_QK = GDN_HEADS * GDN_DK
GDN_V = GDN_HEADS * GDN_DV
CD_SPLITS = (CONF_CH, CONF_CH, GDN_QK, GDN_QK, GDN_V, GDN_V, 2 * GDN_HEADS, 2 * GDN_HEADS)

LANE = 128
SUBLANES = 8
TT = 256
CPT = TT // CHUNK
VMEM_LIMIT = 48 * 1024 * 1024

AB_Q, AB_K, AB_V, AB_R, AB_Z, AB_XS, AB_BM, AB_CM, AB_WIDE = 0, 512, 1024, 2048, 3072, 4096, 5120, 5376, 5632
SM_DT = 2 * GLA_GATE_RANK
CD_WIDE = 6 * D_MODEL


def _split_cols(a, sizes):
    return jnp.split(a, np.cumsum(sizes)[:-1].tolist(), axis=-1)


def _pad_cols(a, n):
    return jnp.pad(a, [(0, 0)] * (a.ndim - 1) + [(0, n - a.shape[-1])])


def _dot(a, b):
    return jnp.dot(a, b, preferred_element_type=F32)


def _dot_nt(a, b):
    return lax.dot_general(a, b, (((1,), (1,)), ((), ())), preferred_element_type=F32)


def _split3(x):
    hi = x.astype(BF16)
    r = x - hi.astype(F32)
    mid = r.astype(BF16)
    lo = (r - mid.astype(F32)).astype(BF16)
    return hi, mid, lo


def _sel_dot(m, x):
    hi, mid, lo = _split3(x)
    return _dot(m, hi) + _dot(m, mid) + _dot(m, lo)


def _dot_sel(x, e):
    hi, mid, lo = _split3(x)
    return _dot(hi, e) + _dot(mid, e) + _dot(lo, e)


def _softplus(x):
    return jnp.maximum(x, 0.0) + jnp.log(1.0 + jnp.exp(-jnp.abs(x)))


def _silu(x):
    return x * jax.nn.sigmoid(x)


def _chunk_masks(is_fwd, n):
    r = lax.broadcasted_iota(jnp.int32, (n, n), 0)
    c = lax.broadcasted_iota(jnp.int32, (n, n), 1)
    same = lax.shift_right_logical(r, 6) == lax.shift_right_logical(c, 6)
    lo = jnp.where(is_fwd, c, r)
    hi = jnp.where(is_fwd, r, c)
    cum = jnp.logical_and(same, lo <= hi)
    return jnp.where(cum, 1.0, 0.0).astype(BF16), jnp.where(same, 1.0, 0.0).astype(BF16)


def _causal_mask(is_fwd, n):
    r = lax.broadcasted_iota(jnp.int32, (n, n), 0)
    c = lax.broadcasted_iota(jnp.int32, (n, n), 1)
    return jnp.where(is_fwd, c, r) <= jnp.where(is_fwd, r, c)


def _scan_tile(d, j, nt):
    return jnp.where(d == 0, j, jnp.where(j == 0, 0, nt - j))


def _mod_proj_kernel(c_ref, w_ref, b_ref, o_ref):
    o_ref[...] = _dot(_silu(c_ref[...]).astype(BF16), w_ref[...]) + b_ref[...]


def mod_proj(cond, w, b):
    r, d = cond.shape
    n = w.shape[1]
    tn = 6 * LANE * 2
    return pl.pallas_call(
        _mod_proj_kernel,
        grid=(n // tn,),
        in_specs=[pl.BlockSpec((r, d), lambda k: (0, 0)),
                  pl.BlockSpec((d, tn), lambda k: (0, k)),
                  pl.BlockSpec((1, tn), lambda k: (0, k))],
        out_specs=pl.BlockSpec((r, tn), lambda k: (0, k)),
        out_shape=jax.ShapeDtypeStruct((r, n), F32),
        compiler_params=pltpu.CompilerParams(dimension_semantics=("parallel",)),
        name="mod_proj",
    )(cond, w, b.reshape(1, n))


NP_TM = 3 * TT


def _norm_proj_kernel(x_ref, g_ref, sh_ref, sc_ref, w_ref, b_ref, ws_ref, bs_ref, o_ref, os_ref):
    x = x_ref[0]
    row = lax.broadcasted_iota(jnp.int32, (NP_TM, 1), 0) + pl.program_id(2) * NP_TM
    is_ctx = row < TT
    scale = jnp.where(is_ctx, sc_ref[0, 0], sc_ref[0, 1])
    shift = jnp.where(is_ctx, sh_ref[0, 0], sh_ref[0, 1])
    ms = jnp.mean(x * x, axis=-1, keepdims=True)
    h = (x * lax.rsqrt(ms + 1e-6) * g_ref[...] * (1.0 + scale) + shift).astype(BF16)
    o_ref[0] = (_dot(h, w_ref[...]) + b_ref[...]).astype(BF16)
    os_ref[0, 0] = _dot(h, ws_ref[...]) + bs_ref[...]


def norm_proj(x, g, shift, scale, w, b, w_small, b_small, tn):
    bsz, t, d = x.shape
    n = w.shape[1]
    assert t % NP_TM == 0 and n % tn == 0
    wide, narrow = pl.pallas_call(
        _norm_proj_kernel,
        grid=(n // tn, bsz, t // NP_TM),
        in_specs=[
            pl.BlockSpec((1, NP_TM, d), lambda k, i, j: (i, j, 0)),
            pl.BlockSpec((1, d), lambda k, i, j: (0, 0)),
            pl.BlockSpec((1, 2, 1, d), lambda k, i, j: (i, 0, 0, 0)),
            pl.BlockSpec((1, 2, 1, d), lambda k, i, j: (i, 0, 0, 0)),
            pl.BlockSpec((d, tn), lambda k, i, j: (0, k)),
            pl.BlockSpec((1, tn), lambda k, i, j: (0, k)),
            pl.BlockSpec((d, LANE), lambda k, i, j: (0, 0)),
            pl.BlockSpec((1, LANE), lambda k, i, j: (0, 0)),
        ],
        out_specs=[pl.BlockSpec((1, NP_TM, tn), lambda k, i, j: (i, j, k)),
                   pl.BlockSpec((1, 1, NP_TM, LANE), lambda k, i, j: (k, i, j, 0))],
        out_shape=[jax.ShapeDtypeStruct((bsz, t, n), BF16),
                   jax.ShapeDtypeStruct((n // tn, bsz, t, LANE), F32)],
        compiler_params=pltpu.CompilerParams(
            dimension_semantics=("parallel", "parallel", "parallel"), vmem_limit_bytes=VMEM_LIMIT),
        name="norm_proj",
    )(x, g.reshape(1, d), shift, scale, w, b.reshape(1, n), w_small, b_small.reshape(1, LANE))
    return wide, narrow[0]


HALO = 16


def _conv3_piece(c_ref, l_ref, r_ref, w, b, left_ok, right_ok):
    x = c_ref[0].astype(F32)
    n = x.shape[0]
    row = lax.broadcasted_iota(jnp.int32, x.shape, 0)
    prev_row = jnp.where(left_ok, l_ref[0, HALO - 1:HALO, :].astype(F32), 0.0)
    next_row = jnp.where(right_ok, r_ref[0, 0:1, :].astype(F32), 0.0)
    x_prev = jnp.where(row == 0, prev_row, pltpu.roll(x, 1, 0))
    x_next = jnp.where(row == n - 1, next_row, pltpu.roll(x, n - 1, 0))
    return _silu(w[0:1, :] * x_prev + w[1:2, :] * x + w[2:3, :] * x_next + b)


def _ab_prep_kernel(xs_ref, bm_ref, cm_ref, xsl_ref, bml_ref, cml_ref, xsr_ref, bmr_ref, cmr_ref,
                    w_ref, b_ref, o_ref):
    j = pl.program_id(1)
    nt = pl.num_programs(1)
    left_ok = j >= 2
    right_ok = jnp.logical_and(j >= 1, j < nt - 1)
    w = w_ref[...]
    b = b_ref[...]
    o_ref[0, :, 0:SSD_INNER] = _conv3_piece(xs_ref, xsl_ref, xsr_ref, w[:, 0:SSD_INNER],
                                            b[:, 0:SSD_INNER], left_ok, right_ok).astype(BF16)
    c0, c1 = SSD_INNER, SSD_INNER + SSD_BC
    o_ref[0, :, c0:c1] = _conv3_piece(bm_ref, bml_ref, bmr_ref, w[:, c0:c1], b[:, c0:c1],
                                      left_ok, right_ok).astype(BF16)
    c0, c1 = c1, c1 + SSD_BC
    o_ref[0, :, c0:c1] = _conv3_piece(cm_ref, cml_ref, cmr_ref, w[:, c0:c1], b[:, c0:c1],
                                      left_ok, right_ok).astype(BF16)


def ab_prep(proj, conv_w, conv_b):
    bsz, t, _ = proj.shape
    nt = t // TT
    rb = TT // HALO
    nrb = t // HALO
    cw = SSD_INNER + 2 * SSD_BC

    def cur(width, col):
        return pl.BlockSpec((1, TT, width), lambda i, j: (i, j, col // width))

    def left(width, col):
        return pl.BlockSpec((1, HALO, width), lambda i, j: (i, jnp.maximum(j * rb - 1, 0), col // width))

    def right(width, col):
        return pl.BlockSpec((1, HALO, width), lambda i, j: (i, jnp.minimum((j + 1) * rb, nrb - 1), col // width))

    pieces = ((SSD_INNER, AB_XS), (SSD_BC, AB_BM), (SSD_BC, AB_CM))
    return pl.pallas_call(
        _ab_prep_kernel,
        grid=(bsz, nt),
        in_specs=[cur(*p) for p in pieces] + [left(*p) for p in pieces] + [right(*p) for p in pieces] + [
            pl.BlockSpec((3, cw), lambda i, j: (0, 0)),
            pl.BlockSpec((1, cw), lambda i, j: (0, 0)),
        ],
        out_specs=pl.BlockSpec((1, TT, cw), lambda i, j: (i, j, 0)),
        out_shape=jax.ShapeDtypeStruct((bsz, t, cw), BF16),
        compiler_params=pltpu.CompilerParams(dimension_semantics=("parallel", "parallel")),
        name="ab_prep",
    )(*([proj] * 9), conv_w, conv_b.reshape(1, cw))


def _gla_kernel(q_ref, k_ref, v_ref, sm_ref, wg_ref, bg_ref, o_ref, qg_s, egl_s, oi_s, u_s, st_s):
    d = pl.program_id(1)
    j = pl.program_id(2)
    is_fwd = d == 0

    @pl.when(j == 0)
    def _():
        st_s[...] = jnp.zeros_like(st_s)

    gz = _dot(sm_ref[0].astype(BF16), wg_ref[0]) + bg_ref[0]
    logg = (jnp.minimum(gz, 0.0) - jnp.log(1.0 + jnp.exp(-jnp.abs(gz)))) * (1.0 / GLA_GATE_TAU)
    m_cum, m_all = _chunk_masks(is_fwd, TT)
    gc = _sel_dot(m_cum, logg)
    gl = _sel_dot(m_all, logg)
    q = q_ref[0].astype(F32) * (GLA_DK ** -0.5)
    k = k_ref[0].astype(F32)
    qg = (q * jnp.exp(gc)).astype(BF16)
    kn = (k * jnp.exp(-gc)).astype(BF16)
    kd = (k * jnp.exp(gl - gc)).astype(BF16)
    qg_s[...] = qg
    egl_s[...] = jnp.exp(gl)
    causal = _causal_mask(is_fwd, CHUNK)

    pairs = [(h, ci) for h in range(GLA_HEADS) for ci in range(CPT)]
    rows_of = lambda ci: slice(ci * CHUNK, (ci + 1) * CHUNK)
    kcols = lambda h: slice(h * GLA_DK, (h + 1) * GLA_DK)
    vcols = lambda h: slice(h * GLA_DV, (h + 1) * GLA_DV)
    vs = [v_ref[0, rows_of(ci), vcols(h)] for h, ci in pairs]
    atts = [jnp.where(causal, _dot_nt(qg[rows_of(ci), kcols(h)], kn[rows_of(ci), kcols(h)]), 0.0).astype(BF16)
            for h, ci in pairs]
    for n, (h, ci) in enumerate(pairs):
        oi_s[rows_of(ci), vcols(h)] = _dot(atts[n], vs[n])
        u_s[n] = _dot(vs[n].astype(F32).T.astype(BF16), kd[rows_of(ci), kcols(h)])

    for ci in range(CPT):
        cidx = jnp.where(is_fwd, ci, CPT - 1 - ci)
        off = pl.multiple_of(cidx * CHUNK, CHUNK)
        rows = pl.ds(off, CHUNK)
        for h in range(GLA_HEADS):
            st = st_s[h]
            o_ref[0, 0, rows, vcols(h)] = (oi_s[rows, vcols(h)]
                                           + _dot_nt(qg_s[rows, kcols(h)], st.astype(BF16))).astype(BF16)
            st_s[h] = st * egl_s[pl.ds(off, 1), kcols(h)] + u_s[h * CPT + cidx]


def gla_scan(proj, small, wg, bg):
    bsz, t, _ = proj.shape
    nt = t // TT
    tile = lambda d, j: _scan_tile(d, j, nt)
    return pl.pallas_call(
        _gla_kernel,
        grid=(bsz, 2, nt),
        in_specs=[
            pl.BlockSpec((1, TT, GLA_QK), lambda i, d, j: (i, tile(d, j), AB_Q // GLA_QK)),
            pl.BlockSpec((1, TT, GLA_QK), lambda i, d, j: (i, tile(d, j), AB_K // GLA_QK)),
            pl.BlockSpec((1, TT, GLA_V), lambda i, d, j: (i, tile(d, j), AB_V // GLA_V)),
            pl.BlockSpec((1, TT, LANE), lambda i, d, j: (i, tile(d, j), 0)),
            pl.BlockSpec((1, LANE, GLA_QK), lambda i, d, j: (d, 0, 0)),
            pl.BlockSpec((1, 1, GLA_QK), lambda i, d, j: (d, 0, 0)),
        ],
        out_specs=pl.BlockSpec((1, 1, TT, GLA_V), lambda i, d, j: (d, i, tile(d, j), 0)),
        out_shape=jax.ShapeDtypeStruct((2, bsz, t, GLA_V), BF16),
        scratch_shapes=[pltpu.VMEM((TT, GLA_QK), BF16), pltpu.VMEM((TT, GLA_QK), F32),
                        pltpu.VMEM((TT, GLA_V), F32), pltpu.VMEM((GLA_HEADS * CPT, GLA_DV, GLA_DK), F32),
                        pltpu.VMEM((GLA_HEADS, GLA_DV, GLA_DK), F32)],
        compiler_params=pltpu.CompilerParams(
            dimension_semantics=("parallel", "parallel", "arbitrary")),
        name="gla_scan",
    )(proj, proj, proj, small, wg, bg)


def _dot_sel2(x, e):
    hi = x.astype(BF16)
    return _dot(hi, e) + _dot((x - hi.astype(F32)).astype(BF16), e)


def _ssd_kernel(xs_ref, bm_ref, cm_ref, sm_ref, dtb_ref, nega_ref, e_ref, o_ref,
                v_s, vw_s, cdec_s, dec_s, ah_s, st_s):
    d = pl.program_id(1)
    j = pl.program_id(2)
    is_fwd = d == 0
    gw = SSD_HPG * SSD_HEADDIM

    @pl.when(j == 0)
    def _():
        st_s[...] = jnp.zeros_like(st_s)

    dt = _softplus(sm_ref[0] + dtb_ref[0])
    la = dt * nega_ref[0]
    m_cum, m_all = _chunk_masks(is_fwd, TT)
    acum = _sel_dot(m_cum, la)
    atot = _sel_dot(m_all, la)
    e = e_ref[0]
    v = xs_ref[0].astype(F32) * _dot_sel2(dt, e)
    v_s[...] = v.astype(BF16)
    vw_s[...] = (v * _dot_sel2(jnp.exp(atot - acum), e)).astype(BF16)
    cdec_s[...] = _dot_sel2(jnp.exp(acum), e)
    dec_s[...] = _dot_sel2(jnp.exp(atot), e)
    ah_s[...] = pltpu.roll(acum, LANE - SM_DT - d * SSD_HEADS, 1)
    causal = _causal_mask(is_fwd, CHUNK)

    for ci in range(CPT):
        off = pl.multiple_of(jnp.where(is_fwd, ci, CPT - 1 - ci) * CHUNK, CHUNK)
        rows = pl.ds(off, CHUNK)
        ah = ah_s[rows, :]
        aht = ah.T
        for g in range(SSD_GROUPS):
            gc = slice(g * gw, (g + 1) * gw)
            nc = slice(g * SSD_STATE, (g + 1) * SSD_STATE)
            bm = bm_ref[0, rows, nc]
            cm = cm_ref[0, rows, nc]
            st = st_s[g]
            cb = _dot_nt(cm, bm)
            y_inter = _dot(cm, st.astype(BF16)) * cdec_s[rows, gc]
            v_c = v_s[rows, gc]
            ys = []
            for hh in range(SSD_HPG):
                h = g * SSD_HPG + hh
                seg = jnp.exp(jnp.where(causal, ah[:, h:h + 1] - aht[h:h + 1, :], -1e30))
                ys.append(_dot((seg * cb).astype(BF16), v_c[:, hh * SSD_HEADDIM:(hh + 1) * SSD_HEADDIM]))
            o_ref[0, 0, rows, gc] = (jnp.concatenate(ys, axis=1) + y_inter).astype(BF16)
            st_s[g] = st * dec_s[pl.ds(off, 1), gc] + _dot(bm.astype(F32).T.astype(BF16), vw_s[rows, gc])


def ssd_scan(xbc, small, dtb, nega, e):
    bsz, t, _ = xbc.shape
    nt = t // TT
    gw = SSD_HPG * SSD_HEADDIM
    tile = lambda d, j: _scan_tile(d, j, nt)
    return pl.pallas_call(
        _ssd_kernel,
        grid=(bsz, 2, nt),
        in_specs=[
            pl.BlockSpec((1, TT, SSD_INNER), lambda i, d, j: (i, tile(d, j), 0)),
            pl.BlockSpec((1, TT, SSD_BC), lambda i, d, j: (i, tile(d, j), SSD_INNER // SSD_BC)),
            pl.BlockSpec((1, TT, SSD_BC), lambda i, d, j: (i, tile(d, j), SSD_INNER // SSD_BC + 1)),
            pl.BlockSpec((1, TT, LANE), lambda i, d, j: (i, tile(d, j), 0)),
            pl.BlockSpec((1, 1, LANE), lambda i, d, j: (d, 0, 0)),
            pl.BlockSpec((1, 1, LANE), lambda i, d, j: (d, 0, 0)),
            pl.BlockSpec((1, LANE, SSD_INNER), lambda i, d, j: (d, 0, 0)),
        ],
        out_specs=pl.BlockSpec((1, 1, TT, SSD_INNER), lambda i, d, j: (d, i, tile(d, j), 0)),
        out_shape=jax.ShapeDtypeStruct((2, bsz, t, SSD_INNER), BF16),
        scratch_shapes=[pltpu.VMEM((TT, SSD_INNER), BF16), pltpu.VMEM((TT, SSD_INNER), BF16),
                        pltpu.VMEM((TT, SSD_INNER), F32), pltpu.VMEM((TT, SSD_INNER), F32),
                        pltpu.VMEM((TT, LANE), F32), pltpu.VMEM((SSD_GROUPS, SSD_STATE, gw), F32)],
        compiler_params=pltpu.CompilerParams(
            dimension_semantics=("parallel", "parallel", "arbitrary")),
        name="ssd_scan",
    )(xbc, xbc, xbc, small, dtb, nega, e)


def _group_rmsnorm(x, width):
    parts = []
    for s in range(x.shape[1] // width):
        seg = x[:, s * width:(s + 1) * width]
        parts.append(seg * lax.rsqrt(jnp.mean(seg * seg, axis=-1, keepdims=True) + 1e-6))
    return jnp.concatenate(parts, axis=1)


def _ab_out_kernel(of_ref, ob_ref, yf_ref, yb_ref, r_ref, z_ref, xs_ref, gg_ref, dv_ref, sg_ref,
                   w_ref, b_ref, x_ref, gate_ref, o_ref):
    f32 = lambda ref: ref[...].astype(F32)[0]
    o = _group_rmsnorm(f32(of_ref)[0] + f32(ob_ref)[0], GLA_DV) * gg_ref[...] * _silu(f32(r_ref))
    y = (f32(yf_ref)[0] + f32(yb_ref)[0] + dv_ref[...] * f32(xs_ref)) * _silu(f32(z_ref))
    y = _group_rmsnorm(y, SSD_INNER // SSD_GROUPS) * sg_ref[...]
    m = _dot(o.astype(BF16), w_ref[0:GLA_V, :]) + _dot(y.astype(BF16), w_ref[GLA_V:, :]) + b_ref[...]
    o_ref[0] = x_ref[0] + gate_ref[0, 0] * m


def ab_out(o_gla, y_ssd, proj, xbc, gla_g, d_vec, ssd_g, w, b, x, gate):
    bsz, t, d = x.shape
    seg = lambda i, j: (i, jnp.minimum(j, 1), 0, 0)
    row = lambda width: pl.BlockSpec((1, width), lambda i, j: (0, 0))
    return pl.pallas_call(
        _ab_out_kernel,
        grid=(bsz, t // TT),
        in_specs=[
            pl.BlockSpec((1, 1, TT, GLA_V), lambda i, j: (0, i, j, 0)),
            pl.BlockSpec((1, 1, TT, GLA_V), lambda i, j: (1, i, j, 0)),
            pl.BlockSpec((1, 1, TT, SSD_INNER), lambda i, j: (0, i, j, 0)),
            pl.BlockSpec((1, 1, TT, SSD_INNER), lambda i, j: (1, i, j, 0)),
            pl.BlockSpec((1, TT, GLA_V), lambda i, j: (i, j, AB_R // GLA_V)),
            pl.BlockSpec((1, TT, SSD_INNER), lambda i, j: (i, j, AB_Z // SSD_INNER)),
            pl.BlockSpec((1, TT, SSD_INNER), lambda i, j: (i, j, 0)),
            row(GLA_V), row(SSD_INNER), row(SSD_INNER),
            pl.BlockSpec((GLA_V + SSD_INNER, d), lambda i, j: (0, 0)),
            row(d),
            pl.BlockSpec((1, TT, d), lambda i, j: (i, j, 0)),
            pl.BlockSpec((1, 1, 1, d), seg),
        ],
        out_specs=pl.BlockSpec((1, TT, d), lambda i, j: (i, j, 0)),
        out_shape=jax.ShapeDtypeStruct((bsz, t, d), F32),
        compiler_params=pltpu.CompilerParams(
            dimension_semantics=("parallel", "parallel"), vmem_limit_bytes=VMEM_LIMIT),
        name="ab_out",
    )(o_gla, o_gla, y_ssd, y_ssd, proj, proj, xbc, gla_g.reshape(1, -1), d_vec.reshape(1, -1),
      ssd_g.reshape(1, -1), w, b.reshape(1, d), x, gate)


def _cd_prep_kernel(q_ref, k_ref, v_ref, ql_ref, kl_ref, vl_ref, qr_ref, kr_ref, vr_ref, w_ref, o_ref):
    j = pl.program_id(1)
    nt = pl.num_programs(1)
    left_ok = j >= 2
    right_ok = jnp.logical_and(j >= 1, j < nt - 1)
    w = w_ref[...]
    srcs = ((q_ref, ql_ref, qr_ref, GDN_DK ** -0.5), (k_ref, kl_ref, kr_ref, 1.0), (v_ref, vl_ref, vr_ref, None))
    for s, (c_ref, l_ref, r_ref, scale) in enumerate(srcs):
        c0 = s * GDN_QK
        y = _conv3_piece(c_ref, l_ref, r_ref, w[:, c0:c0 + GDN_QK], 0.0, left_ok, right_ok)
        if scale is None:
            o_ref[0, :, c0:c0 + GDN_QK] = y.astype(BF16)
            continue
        for h in range(GDN_HEADS):
            seg = y[:, h * GDN_DK:(h + 1) * GDN_DK]
            inv = lax.rsqrt(jnp.sum(seg * seg, axis=-1, keepdims=True) + 1e-6) * scale
            o_ref[0, :, c0 + h * GDN_DK:c0 + (h + 1) * GDN_DK] = (seg * inv).astype(BF16)


def cd_prep(proj, conv_w):
    bsz, t, _ = proj.shape
    nt = t // TT
    rb = TT // HALO
    nrb = t // HALO
    width = GDN_QK
    cols = (2, 3, 4)

    cur = lambda cb: pl.BlockSpec((1, TT, width), lambda i, j: (i, j, cb))
    left = lambda cb: pl.BlockSpec((1, HALO, width), lambda i, j: (i, jnp.maximum(j * rb - 1, 0), cb))
    right = lambda cb: pl.BlockSpec((1, HALO, width), lambda i, j: (i, jnp.minimum((j + 1) * rb, nrb - 1), cb))
    return pl.pallas_call(
        _cd_prep_kernel,
        grid=(bsz, nt),
        in_specs=[cur(cb) for cb in cols] + [left(cb) for cb in cols] + [right(cb) for cb in cols] + [
            pl.BlockSpec((3, 3 * width), lambda i, j: (0, 0))],
        out_specs=pl.BlockSpec((1, TT, 3 * width), lambda i, j: (i, j, 0)),
        out_shape=jax.ShapeDtypeStruct((bsz, t, 3 * width), BF16),
        compiler_params=pltpu.CompilerParams(dimension_semantics=("parallel", "parallel")),
        name="cd_prep",
    )(*([proj] * 9), conv_w)


GDN_HB = 8


def _mm2(a, b):
    return _dot(a.astype(BF16), b.astype(BF16))


def _unit_tri_inverse(mats, b16, b32, eye):
    each = lambda f, *ls: [f(*xs) for xs in zip(*ls)]
    d16 = each(lambda a: jnp.where(b16, a, 0.0), mats)
    d2 = each(lambda x: _mm2(x, x), d16)
    d4 = each(lambda x: _mm2(x, x), d2)
    d8 = each(lambda x: _mm2(x, x), d4)
    t = each(lambda x: eye - x, d16)
    for p in (d2, d4, d8):
        t = each(lambda x, y: x + _mm2(x, y), t, p)
    off32 = jnp.logical_and(b32, jnp.logical_not(b16))
    for sel in (off32, jnp.logical_not(b32)):
        a_off = each(lambda a: jnp.where(sel, a, 0.0), mats)
        inner = each(_mm2, a_off, t)
        t = each(lambda x, y: x - _mm2(x, y), t, inner)
    return t


def _gdn_kernel(q_ref, k_ref, v_ref, sm_ref, dtb_ref, nega_ref, o_ref,
                dec_s, n_s, p_s, oc_s, qp_s, st_s):
    d = pl.program_id(2)
    j = pl.program_id(3)
    is_fwd = d == 0

    @pl.when(j == 0)
    def _():
        st_s[...] = jnp.zeros_like(st_s)

    sm = sm_ref[0]
    m_cum, m_all = _chunk_masks(is_fwd, TT)
    first = d * GDN_HEADS + pl.program_id(1) * GDN_HB
    rot = jnp.where(first == 0, 0, LANE - first)
    la = pltpu.roll(_softplus(sm + dtb_ref[...]) * nega_ref[...], rot, 1)
    be_sm = pltpu.roll(jax.nn.sigmoid(sm), rot, 1)
    gc_sm = _sel_dot(m_cum, la)
    gl_sm = _sel_dot(m_all, la)
    lane_bcast = lambda a, col: jnp.broadcast_to(a[:, col:col + 1], (TT, GDN_DK))

    r = lax.broadcasted_iota(jnp.int32, (CHUNK, CHUNK), 0)
    c = lax.broadcasted_iota(jnp.int32, (CHUNK, CHUNK), 1)
    causal = _causal_mask(is_fwd, CHUNK)
    strict = jnp.logical_and(causal, r != c)
    b16 = lax.shift_right_logical(r, 4) == lax.shift_right_logical(c, 4)
    b32 = lax.shift_right_logical(r, 5) == lax.shift_right_logical(c, 5)
    eye = jnp.where(r == c, 1.0, 0.0)
    chunk_rows = [slice(ci * CHUNK, (ci + 1) * CHUNK) for ci in range(CPT)]

    amats, rhss, aqks, kdts, qgs = [], [], [], [], []
    for hh in range(GDN_HB):
        cols = slice(hh * GDN_DK, (hh + 1) * GDN_DK)
        gc = lane_bcast(gc_sm, hh)
        gl = lane_bcast(gl_sm, hh)
        beta_e = lane_bcast(be_sm, 2 * GDN_HEADS + hh)
        q = q_ref[0, :, cols].astype(F32)
        k = k_ref[0, :, cols].astype(F32)
        egc = jnp.exp(gc)
        kb = k * beta_e
        qg = q * egc
        kd = k * jnp.exp(gl - gc)
        dec_s[:, cols] = jnp.exp(gl)
        rhs = jnp.concatenate([v_ref[0, :, cols].astype(F32) * beta_e, kb * egc], axis=1)
        for rows in chunk_rows:
            gcc = gc[rows, :]
            dmat = jnp.exp(jnp.where(causal, gcc[:, 0:CHUNK] - gcc.T[0:CHUNK, :], -1e30))
            kc = k[rows].astype(BF16)
            amats.append(jnp.where(strict, _dot_nt(kb[rows].astype(BF16), kc) * dmat, 0.0))
            rhss.append(rhs[rows])
            aqks.append((_dot_nt(q[rows].astype(BF16), kc) * dmat).astype(BF16))
            kdts.append(kd[rows].T.astype(BF16))
            qgs.append(qg[rows])
    tinv = _unit_tri_inverse(amats, b16, b32, eye)
    sols = [_mm2(t, rhs_c).astype(BF16) for t, rhs_c in zip(tinv, rhss)]
    for n, sol in enumerate(sols):
        hh, ci = n // CPT, n % CPT
        rows, cols = chunk_rows[ci], slice(hh * GDN_DK, (hh + 1) * GDN_DK)
        ks = _dot(kdts[n], sol)
        qs = _dot(aqks[n], sol)
        n_s[n] = ks[:, 0:GDN_DV]
        p_s[n] = ks[:, GDN_DV:].astype(BF16)
        oc_s[rows, cols] = qs[:, 0:GDN_DV]
        qp_s[rows, cols] = (qgs[n] - qs[:, GDN_DV:]).astype(BF16)

    for ci in range(CPT):
        cidx = jnp.where(is_fwd, ci, CPT - 1 - ci)
        off = pl.multiple_of(cidx * CHUNK, CHUNK)
        rows = pl.ds(off, CHUNK)
        for hh in range(GDN_HB):
            cols = slice(hh * GDN_DK, (hh + 1) * GDN_DK)
            st = st_s[hh]
            stb = st.astype(BF16)
            o_ref[0, 0, rows, cols] = (_dot(qp_s[rows, cols], stb) + oc_s[rows, cols]).astype(BF16)
            st_s[hh] = st * dec_s[pl.ds(off, 1), cols] - _dot(p_s[hh * CPT + cidx], stb) + n_s[hh * CPT + cidx]


def gdn_scan(qkv, small, dtb, nega):
    bsz, t, _ = qkv.shape
    nt = t // TT
    tile = lambda d, j: _scan_tile(d, j, nt)
    ng = GDN_HEADS // GDN_HB
    wb = GDN_HB * GDN_DK
    return pl.pallas_call(
        _gdn_kernel,
        grid=(bsz, ng, 2, nt),
        in_specs=[
            pl.BlockSpec((1, TT, wb), lambda i, h, d, j: (i, tile(d, j), h)),
            pl.BlockSpec((1, TT, wb), lambda i, h, d, j: (i, tile(d, j), ng + h)),
            pl.BlockSpec((1, TT, wb), lambda i, h, d, j: (i, tile(d, j), 2 * ng + h)),
            pl.BlockSpec((1, TT, LANE), lambda i, h, d, j: (i, tile(d, j), 0)),
            pl.BlockSpec((1, LANE), lambda i, h, d, j: (0, 0)),
            pl.BlockSpec((1, LANE), lambda i, h, d, j: (0, 0)),
        ],
        out_specs=pl.BlockSpec((1, 1, TT, wb), lambda i, h, d, j: (d, i, tile(d, j), h)),
        out_shape=jax.ShapeDtypeStruct((2, bsz, t, GDN_V), BF16),
        scratch_shapes=[pltpu.VMEM((TT, wb), F32),
                        pltpu.VMEM((GDN_HB * CPT, GDN_DK, GDN_DV), F32),
                        pltpu.VMEM((GDN_HB * CPT, GDN_DK, GDN_DK), BF16),
                        pltpu.VMEM((TT, wb), F32), pltpu.VMEM((TT, wb), BF16),
                        pltpu.VMEM((GDN_HB, GDN_DK, GDN_DV), F32)],
        compiler_params=pltpu.CompilerParams(
            dimension_semantics=("parallel", "parallel", "parallel", "arbitrary")),
        name="gdn_scan",
    )(qkv, qkv, qkv, small, dtb, nega)


CONF_PAD = 16


def _cd_out_kernel(ga_ref, gb_ref, og_ref, of_ref, ob_ref, cw_ref, cb_ref, lg_ref, lb_ref, ng_ref,
                   w_ref, b_ref, x_ref, gate_ref, o_ref, pad_s, shift_s, conv_s):
    j = pl.program_id(1)
    half = (CONF_KERNEL - 1) // 2
    glu = ga_ref[0].astype(F32) * jax.nn.sigmoid(gb_ref[0].astype(F32))
    zeros = jnp.zeros((CONF_PAD, CONF_CH), F32)

    def conv_segments(seglen):
        stride = seglen + 2 * CONF_PAD
        for g in range(TT // seglen):
            base = g * stride
            pad_s[base:base + CONF_PAD, :] = zeros
            pad_s[base + CONF_PAD:base + CONF_PAD + seglen, :] = glu[g * seglen:(g + 1) * seglen]
            pad_s[base + CONF_PAD + seglen:base + stride, :] = zeros
        nseg = TT // seglen
        used = nseg * stride
        conv_s[...] = jnp.zeros((TT, CONF_CH), F32) + cb_ref[...]
        for phase in range(SUBLANES):
            taps = [kk for kk in range(CONF_KERNEL) if (CONF_PAD - half + kk) % SUBLANES == phase]
            if phase:
                shift_s[0:used - SUBLANES, :] = pad_s[phase:used - SUBLANES + phase, :]
            src = shift_s if phase else pad_s
            for g in range(nseg):
                acc = conv_s[g * seglen:(g + 1) * seglen, :]
                for kk in taps:
                    lo = g * stride + CONF_PAD - half + kk - phase
                    acc = acc + src[lo:lo + seglen, :] * cw_ref[kk:kk + 1, :]
                conv_s[g * seglen:(g + 1) * seglen, :] = acc

    @pl.when(j == 0)
    def _():
        conv_segments(TT)

    @pl.when(j > 0)
    def _():
        conv_segments(GRID_W)

    acc = conv_s[...]
    mu = jnp.mean(acc, axis=-1, keepdims=True)
    cen = acc - mu
    var = jnp.mean(cen * cen, axis=-1, keepdims=True)
    conv = _silu(cen * lax.rsqrt(var + 1e-5) * lg_ref[...] + lb_ref[...])
    o = (_group_rmsnorm(of_ref[0, 0].astype(F32) + ob_ref[0, 0].astype(F32), GDN_DV) * ng_ref[...]
         * _silu(og_ref[0].astype(F32)))
    m = _dot(conv.astype(BF16), w_ref[0:CONF_CH, :]) + _dot(o.astype(BF16), w_ref[CONF_CH:, :]) + b_ref[...]
    o_ref[0] = x_ref[0] + gate_ref[0, 0] * m


def cd_out(o_gdn, proj, conv_w, conv_b, ln_g, ln_b, norm_g, w, b, x, gate):
    bsz, t, d = x.shape
    seg = lambda i, j: (i, jnp.minimum(j, 1), 0, 0)
    row = lambda width: pl.BlockSpec((1, width), lambda i, j: (0, 0))
    return pl.pallas_call(
        _cd_out_kernel,
        grid=(bsz, t // TT),
        in_specs=[
            pl.BlockSpec((1, TT, CONF_CH), lambda i, j: (i, j, 0)),
            pl.BlockSpec((1, TT, CONF_CH), lambda i, j: (i, j, 1)),
            pl.BlockSpec((1, TT, GDN_V), lambda i, j: (i, j, 5)),
            pl.BlockSpec((1, 1, TT, GDN_V), lambda i, j: (0, i, j, 0)),
            pl.BlockSpec((1, 1, TT, GDN_V), lambda i, j: (1, i, j, 0)),
            pl.BlockSpec((CONF_KERNEL, CONF_CH), lambda i, j: (0, 0)),
            row(CONF_CH), row(CONF_CH), row(CONF_CH), row(GDN_V),
            pl.BlockSpec((CONF_CH + GDN_V, d), lambda i, j: (0, 0)),
            row(d),
            pl.BlockSpec((1, TT, d), lambda i, j: (i, j, 0)),
            pl.BlockSpec((1, 1, 1, d), seg),
        ],
        out_specs=pl.BlockSpec((1, TT, d), lambda i, j: (i, j, 0)),
        out_shape=jax.ShapeDtypeStruct((bsz, t, d), F32),
        scratch_shapes=[pltpu.VMEM(((TT // GRID_W) * (GRID_W + 2 * CONF_PAD), CONF_CH), F32),
                        pltpu.VMEM(((TT // GRID_W) * (GRID_W + 2 * CONF_PAD), CONF_CH), F32),
                        pltpu.VMEM((TT, CONF_CH), F32)],
        compiler_params=pltpu.CompilerParams(
            dimension_semantics=("parallel", "parallel"), vmem_limit_bytes=VMEM_LIMIT),
        name="cd_out",
    )(proj, proj, proj, o_gdn, o_gdn, conv_w, conv_b.reshape(1, -1), ln_g.reshape(1, -1), ln_b.reshape(1, -1),
      norm_g.reshape(1, -1), w, b.reshape(1, d), x, gate)


def _out_proj_kernel(m_ref, w_ref, b_ref, x_ref, gate_ref, o_ref):
    y = _dot(m_ref[0].astype(BF16), w_ref[...]) + b_ref[...]
    o_ref[0] = x_ref[0] + gate_ref[0] * y


def out_proj_residual(mixed, w, b, x, gate):
    bsz, t, k = mixed.shape
    d = w.shape[1]
    tm = min(t, 512)
    return pl.pallas_call(
        _out_proj_kernel,
        grid=(bsz, t // tm),
        in_specs=[
            pl.BlockSpec((1, tm, k), lambda i, j: (i, j, 0)),
            pl.BlockSpec((k, d), lambda i, j: (0, 0)),
            pl.BlockSpec((1, d), lambda i, j: (0, 0)),
            pl.BlockSpec((1, tm, d), lambda i, j: (i, j, 0)),
            pl.BlockSpec((1, 1, d), lambda i, j: (i, 0, 0)),
        ],
        out_specs=pl.BlockSpec((1, tm, d), lambda i, j: (i, j, 0)),
        out_shape=jax.ShapeDtypeStruct((bsz, t, d), F32),
        compiler_params=pltpu.CompilerParams(
            dimension_semantics=("parallel", "parallel"), vmem_limit_bytes=VMEM_LIMIT),
        name="out_proj",
    )(mixed, w, b.reshape(1, d), x, gate.reshape(bsz, 1, d))


def _router_kernel(x_ref, g_ref, sh_ref, sc_ref, rw_ref, h_ref, aff_ref):
    x = x_ref[0]
    ms = jnp.mean(x * x, axis=-1, keepdims=True)
    h = (x * lax.rsqrt(ms + 1e-6) * g_ref[...] * (1.0 + sc_ref[0, 0]) + sh_ref[0, 0]).astype(BF16)
    h_ref[0] = h
    logits = _dot(h, rw_ref[...])
    lane = lax.broadcasted_iota(jnp.int32, logits.shape, 1)
    logits = jnp.where(lane < N_EXPERTS, logits, -1e30)
    e = jnp.exp(logits - jnp.max(logits, axis=-1, keepdims=True))
    aff = e / jnp.sum(e, axis=-1, keepdims=True)
    aff_ref[0] = aff.T[0:N_EXPERTS, :]


def moe_router(xa, g, shift, scale, rw, seg, tile0, ntiles):
    bsz, _, d = xa.shape
    n = ntiles * TT
    return pl.pallas_call(
        _router_kernel,
        grid=(bsz, ntiles),
        in_specs=[
            pl.BlockSpec((1, TT, d), lambda i, j: (i, j + tile0, 0)),
            pl.BlockSpec((1, d), lambda i, j: (0, 0)),
            pl.BlockSpec((1, 1, 1, d), lambda i, j: (i, seg, 0, 0)),
            pl.BlockSpec((1, 1, 1, d), lambda i, j: (i, seg, 0, 0)),
            pl.BlockSpec((d, LANE), lambda i, j: (0, 0)),
        ],
        out_specs=[pl.BlockSpec((1, TT, d), lambda i, j: (i, j, 0)),
                   pl.BlockSpec((1, N_EXPERTS, TT), lambda i, j: (i, 0, j))],
        out_shape=[jax.ShapeDtypeStruct((bsz, n, d), BF16), jax.ShapeDtypeStruct((bsz, N_EXPERTS, n), F32)],
        compiler_params=pltpu.CompilerParams(dimension_semantics=("parallel", "parallel")),
        name="moe_router",
    )(xa, g.reshape(1, d), shift, scale, rw)


def _lane_block_prefix(x, u_strict):
    nblk = x.shape[1] // LANE
    run = jnp.zeros((x.shape[0], 1), F32)
    outs = []
    for cblk in range(nblk):
        xc = x[:, cblk * LANE:(cblk + 1) * LANE]
        outs.append(_dot(xc.astype(BF16), u_strict) + run)
        run = run + jnp.sum(xc, axis=-1, keepdims=True)
    return jnp.concatenate(outs, axis=1), run


def _select_kernel(aff_ref, slot_ref, *, cap):
    aff = aff_ref[0]
    bits = pltpu.bitcast(aff, jnp.int32)
    capf = jnp.float32(cap)

    def step(i, thr):
        cand = jnp.bitwise_or(thr, lax.shift_left(jnp.int32(1), 30 - i))
        cnt = jnp.sum(jnp.where(bits >= cand, 1.0, 0.0), axis=-1, keepdims=True)
        return jnp.where(cnt >= capf, cand, thr)

    thr = lax.fori_loop(0, 31, step, jnp.zeros((aff.shape[0], 1), jnp.int32))
    gt = jnp.where(bits > thr, 1.0, 0.0)
    eq = jnp.where(bits == thr, 1.0, 0.0)
    r = lax.broadcasted_iota(jnp.int32, (LANE, LANE), 0)
    c = lax.broadcasted_iota(jnp.int32, (LANE, LANE), 1)
    u_strict = jnp.where(r < c, 1.0, 0.0).astype(BF16)
    need = capf - jnp.sum(gt, axis=-1, keepdims=True)
    eq_rank, _ = _lane_block_prefix(eq, u_strict)
    sel = jnp.maximum(gt, jnp.where(eq_rank < need, eq, 0.0))
    slot, _ = _lane_block_prefix(sel, u_strict)
    slot_ref[0] = jnp.where(sel > 0.0, slot.astype(jnp.int32), -1)


def moe_select(aff, cap):
    bsz, ne, n = aff.shape
    return pl.pallas_call(
        functools.partial(_select_kernel, cap=cap),
        grid=(bsz,),
        in_specs=[pl.BlockSpec((1, ne, n), lambda i: (i, 0, 0))],
        out_specs=pl.BlockSpec((1, ne, n), lambda i: (i, 0, 0)),
        out_shape=jax.ShapeDtypeStruct((bsz, ne, n), jnp.int32),
        compiler_params=pltpu.CompilerParams(dimension_semantics=("parallel",)),
        name="moe_select",
    )(aff)


def _slot_index_kernel(slot_ref, idx_ref, *, cap):
    slot = slot_ref[0]
    n = slot.shape[1]
    srow = lax.broadcasted_iota(jnp.int32, (cap, LANE), 0)
    lane = lax.broadcasted_iota(jnp.int32, (cap, LANE), 1)
    acc = jnp.zeros((cap, LANE), jnp.int32)
    for cblk in range(n // LANE):
        s_c = slot[:, cblk * LANE:(cblk + 1) * LANE]
        acc = acc + jnp.where(srow == s_c, lane + (cblk * LANE + 1), 0)
    ones = jnp.ones((8, LANE), BF16)
    hi = _dot_nt(ones, lax.shift_right_logical(acc, 7).astype(F32).astype(BF16))
    lo = _dot_nt(ones, jnp.bitwise_and(acc, LANE - 1).astype(F32).astype(BF16))
    idx_ref[0] = (hi[0:1, :] * float(LANE) + lo[0:1, :]).astype(jnp.int32) - 1


SLOT_WIN = 2 * LANE


def _slot_index_win_kernel(base_ref, slot_ref, idx_ref, acc_s, *, cap):
    i = pl.program_id(0)
    n = slot_ref.shape[2]
    acc_s[...] = jnp.zeros(acc_s.shape, jnp.int32)
    srow = lax.broadcasted_iota(jnp.int32, (SLOT_WIN, LANE), 0)
    lane = lax.broadcasted_iota(jnp.int32, (SLOT_WIN, LANE), 1)
    for cblk in range(n // LANE):
        base = pl.multiple_of(base_ref[i, cblk], LANE)
        s_c = slot_ref[0, :, cblk * LANE:(cblk + 1) * LANE]
        rows = pl.ds(base, SLOT_WIN)
        acc_s[rows, :] = acc_s[rows, :] + jnp.where(srow + base == s_c, lane + (cblk * LANE + 1), 0)
    acc = acc_s[0:cap, :]
    ones = jnp.ones((8, LANE), BF16)
    hi = _dot_nt(ones, lax.shift_right_logical(acc, 7).astype(F32).astype(BF16))
    lo = _dot_nt(ones, jnp.bitwise_and(acc, LANE - 1).astype(F32).astype(BF16))
    idx_ref[0] = (hi[0:1, :] * float(LANE) + lo[0:1, :]).astype(jnp.int32) - 1


def moe_slot_index_windowed(slot, base, cap):
    bsz, ne, n = slot.shape
    idx = pl.pallas_call(
        functools.partial(_slot_index_win_kernel, cap=cap),
        grid_spec=pltpu.PrefetchScalarGridSpec(
            num_scalar_prefetch=1,
            grid=(bsz * ne,),
            in_specs=[pl.BlockSpec((1, 1, n), lambda i, base_ref: (i, 0, 0))],
            out_specs=pl.BlockSpec((1, 1, cap), lambda i, base_ref: (i, 0, 0)),
            scratch_shapes=[pltpu.VMEM((cap + LANE, LANE), jnp.int32)],
        ),
        out_shape=jax.ShapeDtypeStruct((bsz * ne, 1, cap), jnp.int32),
        compiler_params=pltpu.CompilerParams(dimension_semantics=("arbitrary",)),
        name="moe_slot_index",
    )(base, slot.reshape(bsz * ne, 1, n))
    return idx.reshape(bsz, ne, cap)


def moe_slot_index(slot, cap):
    bsz, ne, n = slot.shape
    idx = pl.pallas_call(
        functools.partial(_slot_index_kernel, cap=cap),
        grid=(bsz * ne,),
        in_specs=[pl.BlockSpec((1, 1, n), lambda i: (i, 0, 0))],
        out_specs=pl.BlockSpec((1, 1, cap), lambda i: (i, 0, 0)),
        out_shape=jax.ShapeDtypeStruct((bsz * ne, 1, cap), jnp.int32),
        compiler_params=pltpu.CompilerParams(dimension_semantics=("parallel",)),
        name="moe_slot_index",
    )(slot.reshape(bsz * ne, 1, n))
    return idx.reshape(bsz, ne, cap)


WIN_ALIGN = 16
WIN_FAST = 128


def _combine_kernel(ws_ref, slot_ref, aff_ref, *rest, win, final):
    ye_refs, (x_ref, gate_ref), o_ref = rest[:N_EXPERTS], rest[N_EXPERTS:N_EXPERTS + 2], rest[-1]
    b = pl.program_id(0)
    j = pl.program_id(1)
    srow = lax.broadcasted_iota(jnp.int32, (win, TT), 0)
    his, los = [], []
    for e in range(N_EXPERTS):
        sel = jnp.where(srow + ws_ref[b, e, j] == slot_ref[0, e:e + 1, :], aff_ref[0, e:e + 1, :], 0.0).T
        hi = sel.astype(BF16)
        his.append(hi)
        los.append((sel - hi.astype(F32)).astype(BF16))
    ye = jnp.concatenate([r[...] for r in ye_refs], axis=0)
    acc = _dot(jnp.concatenate(his, axis=1), ye) + _dot(jnp.concatenate(los, axis=1), ye)
    y = x_ref[0] + gate_ref[0, 0] * acc
    if final:
        g_ref = rest[N_EXPERTS + 2]
        y = y * lax.rsqrt(jnp.mean(y * y, axis=-1, keepdims=True) + 1e-6) * g_ref[...]
    o_ref[0] = y


def moe_combine(ws, slot, aff, ye, xa, gate, seg, tile0, win, final_g=None):
    bsz, ne, n = slot.shape
    nt = n // TT
    d = xa.shape[2]

    def ye_spec(e):
        return pl.BlockSpec((pl.Squeezed(), pl.Squeezed(), pl.Element(win), pl.Element(d)),
                            lambda i, j, ws_ref: (i, e, pl.multiple_of(ws_ref[i, e, j], WIN_ALIGN), 0))

    in_specs = ([pl.BlockSpec((1, ne, TT), lambda i, j, ws_ref: (i, 0, j)),
                 pl.BlockSpec((1, ne, TT), lambda i, j, ws_ref: (i, 0, j))]
                + [ye_spec(e) for e in range(ne)]
                + [pl.BlockSpec((1, TT, d), lambda i, j, ws_ref: (i, j + tile0, 0)),
                   pl.BlockSpec((1, 1, 1, d), lambda i, j, ws_ref: (i, seg, 0, 0))])
    args = (ws, slot, aff, *([ye] * ne), xa, gate)
    if final_g is None:
        out_idx, out_shape, aliases = (lambda i, j, ws_ref: (i, j + tile0, 0)), xa.shape, {3 + ne: 0}
    else:
        out_idx, out_shape, aliases = (lambda i, j, ws_ref: (i, j, 0)), (bsz, n, d), {}
        in_specs.append(pl.BlockSpec((1, d), lambda i, j, ws_ref: (0, 0)))
        args += (final_g.reshape(1, d),)
    return pl.pallas_call(
        functools.partial(_combine_kernel, win=win, final=final_g is not None),
        grid_spec=pltpu.PrefetchScalarGridSpec(
            num_scalar_prefetch=1,
            grid=(bsz, nt),
            in_specs=in_specs,
            out_specs=pl.BlockSpec((1, TT, d), out_idx),
        ),
        out_shape=jax.ShapeDtypeStruct(out_shape, F32),
        input_output_aliases=aliases,
        compiler_params=pltpu.CompilerParams(
            dimension_semantics=("parallel", "parallel"), vmem_limit_bytes=VMEM_LIMIT),
        name="moe_combine",
    )(*args)


def _expert_ffn_kernel(x_ref, w1_ref, w3_ref, w2_ref, o_ref):
    x = x_ref[0, 0]
    a = _dot(x, w1_ref[0, 0])
    g = _dot(x, w3_ref[0, 0])
    o_ref[0, 0] = _dot((_silu(a) * g).astype(BF16), w2_ref[0, 0]).astype(BF16)


def expert_ffn(xe, w1, w3, w2, layer):
    bsz, ne, cap, d = xe.shape
    f = w1.shape[3]
    tm = min(cap, 1024)
    return pl.pallas_call(
        _expert_ffn_kernel,
        grid=(ne, bsz, cap // tm),
        in_specs=[
            pl.BlockSpec((1, 1, tm, d), lambda e, i, j: (i, e, j, 0)),
            pl.BlockSpec((1, 1, d, f), lambda e, i, j: (layer, e, 0, 0)),
            pl.BlockSpec((1, 1, d, f), lambda e, i, j: (layer, e, 0, 0)),
            pl.BlockSpec((1, 1, f, d), lambda e, i, j: (layer, e, 0, 0)),
        ],
        out_specs=pl.BlockSpec((1, 1, tm, d), lambda e, i, j: (i, e, j, 0)),
        out_shape=jax.ShapeDtypeStruct((bsz, ne, cap, d), BF16),
        compiler_params=pltpu.CompilerParams(
            dimension_semantics=("parallel", "parallel", "parallel"), vmem_limit_bytes=VMEM_LIMIT),
        name="expert_ffn",
    )(xe, w1, w3, w2)


def _rmsnorm_kernel(x_ref, g_ref, o_ref):
    x = x_ref[0]
    ms = jnp.mean(x * x, axis=-1, keepdims=True)
    o_ref[0] = x * lax.rsqrt(ms + 1e-6) * g_ref[...]


def rmsnorm_rows(x, g, tile0):
    bsz, t, d = x.shape
    nt = t // TT - tile0
    return pl.pallas_call(
        _rmsnorm_kernel,
        grid=(bsz, nt),
        in_specs=[pl.BlockSpec((1, TT, d), lambda i, j: (i, j + tile0, 0)), pl.BlockSpec((1, d), lambda i, j: (0, 0))],
        out_specs=pl.BlockSpec((1, TT, d), lambda i, j: (i, j, 0)),
        out_shape=jax.ShapeDtypeStruct((bsz, nt * TT, d), F32),
        compiler_params=pltpu.CompilerParams(dimension_semantics=("parallel", "parallel")),
        name="final_rmsnorm",
    )(x, g.reshape(1, d))


def _ab_in_layout(w_in, b_in):
    q, k, v, r, glr, z, xs, bm, cm, dt = _split_cols(
        jnp.concatenate([w_in, b_in[None]], axis=0),
        (GLA_QK, GLA_QK, GLA_V, GLA_V, 2 * GLA_GATE_RANK, SSD_INNER, SSD_INNER, SSD_BC, SSD_BC, 2 * SSD_HEADS))
    wide = jnp.concatenate([q, k, v, r, z, xs, bm, cm], axis=1)
    narrow = _pad_cols(jnp.concatenate([glr, dt], axis=1), LANE)
    return wide[:-1].astype(BF16), wide[-1], narrow[:-1].astype(BF16), narrow[-1]


def _gla_gate_params(w_gate2, b_gate2):
    wg = jnp.zeros((2, LANE, GLA_QK), F32)
    for d in range(2):
        wg = wg.at[d, d * GLA_GATE_RANK:(d + 1) * GLA_GATE_RANK, :].set(w_gate2[d])
    return wg.astype(BF16), b_gate2.reshape(2, 1, GLA_QK)


def _ssd_params(dt_bias, a_log):
    dtb = jnp.zeros((2, 1, LANE), F32)
    nega = jnp.zeros((2, 1, LANE), F32)
    e = np.zeros((2, LANE, SSD_INNER), np.float32)
    for d in range(2):
        c0 = SM_DT + d * SSD_HEADS
        dtb = dtb.at[d, 0, c0:c0 + SSD_HEADS].set(dt_bias[d])
        nega = nega.at[d, 0, c0:c0 + SSD_HEADS].set(-jnp.exp(a_log[d]))
        for h in range(SSD_HEADS):
            e[d, c0 + h, h * SSD_HEADDIM:(h + 1) * SSD_HEADDIM] = 1.0
    return dtb, nega, jnp.asarray(e, BF16)


def _gdn_params(dt_bias, a_log):
    n = 2 * GDN_HEADS
    dtb = jnp.zeros((1, LANE), F32).at[0, 0:n].set(dt_bias.reshape(n))
    nega = jnp.zeros((1, LANE), F32).at[0, 0:n].set(-jnp.exp(a_log.reshape(n)))
    return dtb, nega


def _rmsnorm(x, g, eps=1e-6):
    return x * lax.rsqrt(jnp.mean(jnp.square(x), axis=-1, keepdims=True) + eps) * g


def _head_rmsnorm(x, g):
    return _rmsnorm(x, g.reshape(x.shape[-2:]))


def _layernorm(x, g, b, eps=1e-5):
    mu = jnp.mean(x, axis=-1, keepdims=True)
    var = jnp.mean(jnp.square(x - mu), axis=-1, keepdims=True)
    return (x - mu) * lax.rsqrt(var + eps) * g + b


def _l2norm(x, eps=1e-6):
    return x * lax.rsqrt(jnp.sum(jnp.square(x), axis=-1, keepdims=True) + eps)


def _dwconv(x, w, b=None):
    k, ch = w.shape
    pad = (k - 1) // 2
    y = lax.conv_general_dilated(x, w[:, None, :], (1,), [(pad, pad)],
                                 dimension_numbers=('NWC', 'WIO', 'NWC'), feature_group_count=ch)
    return y if b is None else y + b


def _gdn_scan(q, k, v, beta, logg, s0, with_out):
    bsz, nh, t, dk = q.shape
    dv = v.shape[-1]
    nc = t // CHUNK
    chunks = lambda z: z.reshape(bsz, nh, nc, CHUNK, *z.shape[3:])
    q, k, v, beta, logg = (chunks(z) for z in (q, k, v, beta, logg))
    gc = jnp.cumsum(logg, axis=-1)
    glast = gc[..., -1]
    tril = jnp.tril(jnp.ones((CHUNK, CHUNK), dtype=bool))
    strict = jnp.tril(jnp.ones((CHUNK, CHUNK), dtype=bool), k=-1)
    decay = jnp.exp(jnp.where(tril, gc[..., :, None] - gc[..., None, :], -jnp.inf))
    kb = k * beta[..., None]
    m = jnp.eye(CHUNK, dtype=k.dtype) + jnp.where(strict, jnp.einsum('bhcid,bhcjd->bhcij', kb, k) * decay, 0.0)
    rhs = jnp.concatenate([v * beta[..., None], kb * jnp.exp(gc)[..., None]], axis=-1)
    sol = lax.linalg.triangular_solve(m, rhs, left_side=True, lower=True, unit_diagonal=True)
    u, w = sol[..., :dv], sol[..., dv:]
    kd = k * jnp.exp(glast[..., None] - gc)[..., None]
    dec = jnp.exp(glast)
    front = lambda z: jnp.moveaxis(z, 2, 0)

    def advance(s, w_c, u_c, kd_c, dec_c):
        vn = u_c - jnp.einsum('bhid,bhde->bhie', w_c, s)
        return vn, dec_c[..., None, None] * s + jnp.einsum('bhjd,bhje->bhde', kd_c, vn)

    xs = (front(w), front(u), front(kd), front(dec))
    if not with_out:
        s_fin, _ = lax.scan(lambda s, xc: (advance(s, *xc)[1], None), s0, xs)
        return None, s_fin
    aqk = jnp.einsum('bhcid,bhcjd->bhcij', q, k) * decay
    qg = q * jnp.exp(gc)[..., None]

    def step(s, xc):
        w_c, u_c, kd_c, dec_c, aqk_c, qg_c = xc
        vn, s_new = advance(s, w_c, u_c, kd_c, dec_c)
        o = jnp.einsum('bhid,bhde->bhie', qg_c, s) + jnp.einsum('bhij,bhje->bhie', aqk_c, vn)
        return s_new, o

    s_fin, o = lax.scan(step, s0, xs + (front(aqk), front(qg)))
    return jnp.moveaxis(o, 0, 2).reshape(bsz, nh, t, dv), s_fin


def _bidir_scan(scan_fn, args_f, args_b, t_axis, init_f, init_b, with_out):
    flip = lambda a: jnp.flip(a, axis=t_axis)
    o_f, s_f = scan_fn(*args_f, init_f, with_out)
    o_b, s_b = scan_fn(*[flip(a) for a in args_b], init_b, with_out)
    o = o_f + flip(o_b) if with_out else None
    return o, s_f, s_b


def _conformer_conv(u, rows, dw_w, dw_b, ln_g, ln_b):
    bsz, t, ch = u.shape
    y = _dwconv(u.reshape(bsz * rows, t // rows, ch), dw_w, dw_b).reshape(bsz, t, ch)
    return jax.nn.silu(_layernorm(y, ln_g, ln_b))


def _cd_stream(proj, gdn_conv_w, gdn_a_log, gdn_dt_bias):
    bsz, t, _ = proj.shape
    ga, gb, q, k, v, og, a_raw, b_raw = _split_cols(proj, CD_SPLITS)
    glu = ga * jax.nn.sigmoid(gb)
    qkv = jax.nn.silu(_dwconv(jnp.concatenate([q, k, v], axis=-1), gdn_conv_w))
    q, k, v = _split_cols(qkv, (GDN_QK, GDN_QK, GDN_V))
    heads = lambda a: a.reshape(bsz, t, GDN_HEADS, -1).transpose(0, 2, 1, 3)
    q = _l2norm(heads(q)) * GDN_DK ** -0.5
    k = _l2norm(heads(k))
    v = heads(v)
    beta = jax.nn.sigmoid(b_raw.reshape(bsz, t, 2, GDN_HEADS)).transpose(2, 0, 3, 1)
    logg = (-jnp.exp(gdn_a_log)
            * jax.nn.softplus(a_raw.reshape(bsz, t, 2, GDN_HEADS) + gdn_dt_bias)).transpose(2, 0, 3, 1)
    return (q, k, v, beta[0], logg[0]), (q, k, v, beta[1], logg[1]), glu, og


def _mixer_cd(proj_l, proj_c, rows, conf_dw_w, conf_dw_b, conf_ln_g, conf_ln_b,
              gdn_conv_w, gdn_a_log, gdn_dt_bias, gdn_norm_g, need_ctx):
    c_f, c_b, c_glu, c_og = _cd_stream(proj_c, gdn_conv_w, gdn_a_log, gdn_dt_bias)
    l_f, l_b, l_glu, l_og = _cd_stream(proj_l, gdn_conv_w, gdn_a_log, gdn_dt_bias)
    bsz = proj_l.shape[0]
    z0 = jnp.zeros((bsz, GDN_HEADS, GDN_DK, GDN_DV), F32)
    od_c, s_f, s_b = _bidir_scan(_gdn_scan, c_f, c_b, 2, z0, z0, need_ctx)
    od_l, _, _ = _bidir_scan(_gdn_scan, l_f, l_b, 2, s_f, s_b, True)

    def mix(od, glu, og, n_rows):
        t = glu.shape[1]
        conv = _conformer_conv(glu, n_rows, conf_dw_w, conf_dw_b, conf_ln_g, conf_ln_b)
        o = _head_rmsnorm(jnp.swapaxes(od, 1, 2), gdn_norm_g) * jax.nn.silu(og).reshape(bsz, t, GDN_HEADS, GDN_DV)
        return jnp.concatenate([conv, o.reshape(bsz, t, GDN_V)], axis=-1)

    return mix(od_l, l_glu, l_og, rows), (mix(od_c, c_glu, c_og, 1) if need_ctx else None)


def moe_segment(i, xa, mods, p, w1, w3, w2, seg, tile0, ntiles, final_g=None):
    bsz, _, d = xa.shape
    n = ntiles * TT
    cap = n * EC_CAPACITY // N_EXPERTS
    rw = _pad_cols(p["moe_router"][i], LANE).astype(BF16)
    h, aff = moe_router(xa, p["norm2_g"][i], mods[:, :, 3], mods[:, :, 4], rw, seg, tile0, ntiles)
    slot = moe_select(aff, cap)
    cnt128 = jnp.sum((slot >= 0).reshape(bsz, N_EXPERTS, n // LANE, LANE), axis=-1, dtype=jnp.int32)
    start128 = jnp.cumsum(cnt128, axis=-1) - cnt128
    if cap >= SLOT_WIN:
        base = jnp.minimum(start128 // LANE * LANE, cap - LANE).reshape(bsz * N_EXPERTS, n // LANE)
        idx = moe_slot_index_windowed(slot, base, cap)
    else:
        idx = moe_slot_index(slot, cap)
    idx = idx.reshape(bsz, N_EXPERTS * cap)
    xe = jnp.take_along_axis(h, idx[..., None], axis=1, mode="promise_in_bounds").reshape(bsz, N_EXPERTS, cap, d)
    ye = expert_ffn(xe, w1, w3, w2, i)
    per_tile = TT // LANE
    counts = jnp.sum(cnt128.reshape(bsz, N_EXPERTS, ntiles, per_tile), axis=-1)
    starts = start128[:, :, ::per_tile]
    aligned = starts // WIN_ALIGN * WIN_ALIGN

    def run(win):
        ws = jnp.minimum(aligned, cap - win)
        return moe_combine(ws, slot, aff, ye, xa, mods[:, :, 5], seg, tile0, win, final_g)

    win_fast, win_full = min(cap, WIN_FAST), min(cap, TT + WIN_ALIGN)
    if win_fast == win_full:
        return run(win_full)
    overflow = jnp.any(starts + counts - jnp.minimum(aligned, cap - win_fast) > win_fast)
    return lax.cond(overflow, lambda: run(win_full), lambda: run(win_fast))


def layer_mixer(i, j, xa, mods, p, last):
    sh1, sc1, g1 = (mods[:, :, s] for s in range(3))
    if i % 2 == 0:
        proj, small = norm_proj(xa, p["norm1_g"][i], sh1, sc1, *_ab_in_layout(p["ab_w_in"][j], p["ab_b_in"][j]),
                                AB_WIDE // 2)
        xbc = ab_prep(proj, p["ssd_conv_w"][j], p["ssd_conv_b"][j])
        wg, bg = _gla_gate_params(p["gla_w_gate2"][j], p["gla_b_gate2"][j])
        o_gla = gla_scan(proj, small, wg, bg)
        y_ssd = ssd_scan(xbc, small, *_ssd_params(p["ssd_dt_bias"][j], p["ssd_a_log"][j]))
        return ab_out(o_gla, y_ssd, proj, xbc, p["gla_norm_g"][j], jnp.repeat(p["ssd_d"][j], SSD_HEADDIM),
                      p["ssd_norm_g"][j], p["ab_w_out"][j].astype(BF16), p["ab_b_out"][j], xa, g1)
    w_in, b_in = p["cd_w_in"][j], p["cd_b_in"][j]
    proj, small = norm_proj(xa, p["norm1_g"][i], sh1, sc1, w_in[:, :CD_WIDE].astype(BF16), b_in[:CD_WIDE],
                            _pad_cols(w_in[:, CD_WIDE:], LANE).astype(BF16), _pad_cols(b_in[CD_WIDE:], LANE),
                            CD_WIDE // 3)
    qkv = cd_prep(proj, p["gdn_conv_w"][j])
    o_gdn = gdn_scan(qkv, small, *_gdn_params(p["gdn_dt_bias"][j], p["gdn_a_log"][j]))
    return cd_out(o_gdn, proj, p["conf_dw_w"][j], p["conf_dw_b"][j], p["conf_ln_g"][j], p["conf_ln_b"][j],
                  p["gdn_norm_g"][j], p["cd_w_out"][j].astype(BF16), p["cd_b_out"][j], xa, g1)


def kernel(x, c, ctx, c_ctx, mod_w, mod_b, norm1_g, norm2_g, ab_w_in, ab_b_in, ab_w_out, ab_b_out, gla_w_gate2, gla_b_gate2, gla_norm_g, ssd_conv_w, ssd_conv_b, ssd_dt_bias, ssd_a_log, ssd_d, ssd_norm_g, cd_w_in, cd_b_in, cd_w_out, cd_b_out, conf_dw_w, conf_dw_b, conf_ln_g, conf_ln_b, gdn_conv_w, gdn_a_log, gdn_dt_bias, gdn_norm_g, moe_router, moe_w1, moe_w3, moe_w2, final_norm_g):
    p = dict(locals())
    bsz, seq, d = x.shape
    assert ctx.shape[1] == TT and seq % TT == 0
    depth = mod_w.shape[0]
    xa = jnp.concatenate([ctx, x], axis=1)
    w1, w3, w2 = moe_w1.astype(BF16), moe_w3.astype(BF16), moe_w2.astype(BF16)
    cond = jnp.concatenate([c, c_ctx[None]], axis=0)
    for i in range(depth):
        last = i == depth - 1
        mod = mod_proj(cond, mod_w[i].astype(BF16), mod_b[i])
        mods = jnp.stack([jnp.broadcast_to(mod[bsz], (bsz, 6 * d)), mod[:bsz]], axis=1).reshape(bsz, 2, 6, 1, d)
        xa = layer_mixer(i, i // 2, xa, mods, p, last)
        if last:
            return moe_segment(i, xa, mods, p, w1, w3, w2, 1, 1, seq // TT, final_norm_g)
        xa = moe_segment(i, xa, mods, p, w1, w3, w2, 1, 1, seq // TT)
        xa = moe_segment(i, xa, mods, p, w1, w3, w2, 0, 0, 1)
```

```python
import functools

import jax
import jax.numpy as jnp
import numpy as np
from jax import lax
from jax.experimental import pallas as pl
from jax.experimental.pallas import tpu as pltpu

F32 = jnp.float32
BF16 = jnp.bfloat16

D_MODEL = 1024
GRID_W = 64
CHUNK = 64
GLA_HEADS, GLA_DK, GLA_DV, GLA_GATE_RANK, GLA_GATE_TAU = 4, 128, 256, 16, 16.0
SSD_HEADS, SSD_HEADDIM, SSD_STATE, SSD_GROUPS = 16, 64, 128, 2
CONF_CH, CONF_KERNEL = D_MODEL, 31
GDN_HEADS, GDN_DK, GDN_DV = 8, 128, 128
N_EXPERTS, EC_CAPACITY, EXPERT_FF = 16, 2, D_MODEL

GLA_QK = GLA_HEADS * GLA_DK
GLA_V = GLA_HEADS * GLA_DV
SSD_INNER = SSD_HEADS * SSD_HEADDIM
SSD_BC = SSD_GROUPS * SSD_STATE
SSD_HPG = SSD_HEADS // SSD_GROUPS
GDN_QK = GDN_HEADS * GDN_DK
GDN_V = GDN_HEADS * GDN_DV
CD_SPLITS = (CONF_CH, CONF_CH, GDN_QK, GDN_QK, GDN_V, GDN_V, 2 * GDN_HEADS, 2 * GDN_HEADS)

LANE = 128
SUBLANES = 8
TT = 256
CPT = TT // CHUNK
VMEM_LIMIT = 48 * 1024 * 1024

AB_Q, AB_K, AB_V, AB_R, AB_Z, AB_XS, AB_BM, AB_CM, AB_WIDE = 0, 512, 1024, 2048, 3072, 4096, 5120, 5376, 5632
SM_DT = 2 * GLA_GATE_RANK
CD_WIDE = 6 * D_MODEL


def _split_cols(a, sizes):
    return jnp.split(a, np.cumsum(sizes)[:-1].tolist(), axis=-1)


def _pad_cols(a, n):
    return jnp.pad(a, [(0, 0)] * (a.ndim - 1) + [(0, n - a.shape[-1])])


def _dot(a, b):
    return jnp.dot(a, b, preferred_element_type=F32)


def _dot_nt(a, b):
    return lax.dot_general(a, b, (((1,), (1,)), ((), ())), preferred_element_type=F32)


def _split3(x):
    hi = x.astype(BF16)
    r = x - hi.astype(F32)
    mid = r.astype(BF16)
    lo = (r - mid.astype(F32)).astype(BF16)
    return hi, mid, lo


def _sel_dot(m, x):
    hi, mid, lo = _split3(x)
    return _dot(m, hi) + _dot(m, mid) + _dot(m, lo)


def _dot_sel(x, e):
    hi, mid, lo = _split3(x)
    return _dot(hi, e) + _dot(mid, e) + _dot(lo, e)


def _softplus(x):
    return jnp.maximum(x, 0.0) + jnp.log(1.0 + jnp.exp(-jnp.abs(x)))


def _silu(x):
    return x * jax.nn.sigmoid(x)


def _chunk_masks(is_fwd, n):
    r = lax.broadcasted_iota(jnp.int32, (n, n), 0)
    c = lax.broadcasted_iota(jnp.int32, (n, n), 1)
    same = lax.shift_right_logical(r, 6) == lax.shift_right_logical(c, 6)
    lo = jnp.where(is_fwd, c, r)
    hi = jnp.where(is_fwd, r, c)
    cum = jnp.logical_and(same, lo <= hi)
    return jnp.where(cum, 1.0, 0.0).astype(BF16), jnp.where(same, 1.0, 0.0).astype(BF16)


def _causal_mask(is_fwd, n):
    r = lax.broadcasted_iota(jnp.int32, (n, n), 0)
    c = lax.broadcasted_iota(jnp.int32, (n, n), 1)
    return jnp.where(is_fwd, c, r) <= jnp.where(is_fwd, r, c)


def _scan_tile(d, j, nt):
    return jnp.where(d == 0, j, jnp.where(j == 0, 0, nt - j))


def _mod_proj_kernel(c_ref, w_ref, b_ref, o_ref):
    o_ref[...] = _dot(_silu(c_ref[...]).astype(BF16), w_ref[...]) + b_ref[...]


def mod_proj(cond, w, b):
    r, d = cond.shape
    n = w.shape[1]
    tn = 6 * LANE * 2
    return pl.pallas_call(
        _mod_proj_kernel,
        grid=(n // tn,),
        in_specs=[pl.BlockSpec((r, d), lambda k: (0, 0)),
                  pl.BlockSpec((d, tn), lambda k: (0, k)),
                  pl.BlockSpec((1, tn), lambda k: (0, k))],
        out_specs=pl.BlockSpec((r, tn), lambda k: (0, k)),
        out_shape=jax.ShapeDtypeStruct((r, n), F32),
        compiler_params=pltpu.CompilerParams(dimension_semantics=("parallel",)),
        name="mod_proj",
    )(cond, w, b.reshape(1, n))


NP_TM = 3 * TT


def _norm_proj_kernel(x_ref, g_ref, sh_ref, sc_ref, w_ref, b_ref, ws_ref, bs_ref, o_ref, os_ref):
    x = x_ref[0]
    row = lax.broadcasted_iota(jnp.int32, (NP_TM, 1), 0) + pl.program_id(2) * NP_TM
    is_ctx = row < TT
    scale = jnp.where(is_ctx, sc_ref[0, 0], sc_ref[0, 1])
    shift = jnp.where(is_ctx, sh_ref[0, 0], sh_ref[0, 1])
    ms = jnp.mean(x * x, axis=-1, keepdims=True)
    h = (x * lax.rsqrt(ms + 1e-6) * g_ref[...] * (1.0 + scale) + shift).astype(BF16)
    o_ref[0] = (_dot(h, w_ref[...]) + b_ref[...]).astype(BF16)
    os_ref[0, 0] = _dot(h, ws_ref[...]) + bs_ref[...]


def norm_proj(x, g, shift, scale, w, b, w_small, b_small, tn):
    bsz, t, d = x.shape
    n = w.shape[1]
    assert t % NP_TM == 0 and n % tn == 0
    wide, narrow = pl.pallas_call(
        _norm_proj_kernel,
        grid=(n // tn, bsz, t // NP_TM),
        in_specs=[
            pl.BlockSpec((1, NP_TM, d), lambda k, i, j: (i, j, 0)),
            pl.BlockSpec((1, d), lambda k, i, j: (0, 0)),
            pl.BlockSpec((1, 2, 1, d), lambda k, i, j: (i, 0, 0, 0)),
            pl.BlockSpec((1, 2, 1, d), lambda k, i, j: (i, 0, 0, 0)),
            pl.BlockSpec((d, tn), lambda k, i, j: (0, k)),
            pl.BlockSpec((1, tn), lambda k, i, j: (0, k)),
            pl.BlockSpec((d, LANE), lambda k, i, j: (0, 0)),
            pl.BlockSpec((1, LANE), lambda k, i, j: (0, 0)),
        ],
        out_specs=[pl.BlockSpec((1, NP_TM, tn), lambda k, i, j: (i, j, k)),
                   pl.BlockSpec((1, 1, NP_TM, LANE), lambda k, i, j: (k, i, j, 0))],
        out_shape=[jax.ShapeDtypeStruct((bsz, t, n), BF16),
                   jax.ShapeDtypeStruct((n // tn, bsz, t, LANE), F32)],
        compiler_params=pltpu.CompilerParams(
            dimension_semantics=("parallel", "parallel", "parallel"), vmem_limit_bytes=VMEM_LIMIT),
        name="norm_proj",
    )(x, g.reshape(1, d), shift, scale, w, b.reshape(1, n), w_small, b_small.reshape(1, LANE))
    return wide, narrow[0]


HALO = 16


def _conv3_piece(c_ref, l_ref, r_ref, w, b, left_ok, right_ok):
    x = c_ref[0].astype(F32)
    n = x.shape[0]
    row = lax.broadcasted_iota(jnp.int32, x.shape, 0)
    prev_row = jnp.where(left_ok, l_ref[0, HALO - 1:HALO, :].astype(F32), 0.0)
    next_row = jnp.where(right_ok, r_ref[0, 0:1, :].astype(F32), 0.0)
    x_prev = jnp.where(row == 0, prev_row, pltpu.roll(x, 1, 0))
    x_next = jnp.where(row == n - 1, next_row, pltpu.roll(x, n - 1, 0))
    return _silu(w[0:1, :] * x_prev + w[1:2, :] * x + w[2:3, :] * x_next + b)


def _ab_prep_kernel(xs_ref, bm_ref, cm_ref, xsl_ref, bml_ref, cml_ref, xsr_ref, bmr_ref, cmr_ref,
                    w_ref, b_ref, o_ref):
    j = pl.program_id(1)
    nt = pl.num_programs(1)
    left_ok = j >= 2
    right_ok = jnp.logical_and(j >= 1, j < nt - 1)
    w = w_ref[...]
    b = b_ref[...]
    o_ref[0, :, 0:SSD_INNER] = _conv3_piece(xs_ref, xsl_ref, xsr_ref, w[:, 0:SSD_INNER],
                                            b[:, 0:SSD_INNER], left_ok, right_ok).astype(BF16)
    c0, c1 = SSD_INNER, SSD_INNER + SSD_BC
    o_ref[0, :, c0:c1] = _conv3_piece(bm_ref, bml_ref, bmr_ref, w[:, c0:c1], b[:, c0:c1],
                                      left_ok, right_ok).astype(BF16)
    c0, c1 = c1, c1 + SSD_BC
    o_ref[0, :, c0:c1] = _conv3_piece(cm_ref, cml_ref, cmr_ref, w[:, c0:c1], b[:, c0:c1],
                                      left_ok, right_ok).astype(BF16)


def ab_prep(proj, conv_w, conv_b):
    bsz, t, _ = proj.shape
    nt = t // TT
    rb = TT // HALO
    nrb = t // HALO
    cw = SSD_INNER + 2 * SSD_BC

    def cur(width, col):
        return pl.BlockSpec((1, TT, width), lambda i, j: (i, j, col // width))

    def left(width, col):
        return pl.BlockSpec((1, HALO, width), lambda i, j: (i, jnp.maximum(j * rb - 1, 0), col // width))

    def right(width, col):
        return pl.BlockSpec((1, HALO, width), lambda i, j: (i, jnp.minimum((j + 1) * rb, nrb - 1), col // width))

    pieces = ((SSD_INNER, AB_XS), (SSD_BC, AB_BM), (SSD_BC, AB_CM))
    return pl.pallas_call(
        _ab_prep_kernel,
        grid=(bsz, nt),
        in_specs=[cur(*p) for p in pieces] + [left(*p) for p in pieces] + [right(*p) for p in pieces] + [
            pl.BlockSpec((3, cw), lambda i, j: (0, 0)),
            pl.BlockSpec((1, cw), lambda i, j: (0, 0)),
        ],
        out_specs=pl.BlockSpec((1, TT, cw), lambda i, j: (i, j, 0)),
        out_shape=jax.ShapeDtypeStruct((bsz, t, cw), BF16),
        compiler_params=pltpu.CompilerParams(dimension_semantics=("parallel", "parallel")),
        name="ab_prep",
    )(*([proj] * 9), conv_w, conv_b.reshape(1, cw))


def _gla_kernel(q_ref, k_ref, v_ref, sm_ref, wg_ref, bg_ref, o_ref, qg_s, egl_s, oi_s, u_s, st_s):
    d = pl.program_id(1)
    j = pl.program_id(2)
    is_fwd = d == 0

    @pl.when(j == 0)
    def _():
        st_s[...] = jnp.zeros_like(st_s)

    gz = _dot(sm_ref[0].astype(BF16), wg_ref[0]) + bg_ref[0]
    logg = (jnp.minimum(gz, 0.0) - jnp.log(1.0 + jnp.exp(-jnp.abs(gz)))) * (1.0 / GLA_GATE_TAU)
    m_cum, m_all = _chunk_masks(is_fwd, TT)
    gc = _sel_dot(m_cum, logg)
    totals = [jnp.where(is_fwd, gc[(ci + 1) * CHUNK - 1:(ci + 1) * CHUNK, :], gc[ci * CHUNK:ci * CHUNK + 1, :])
              for ci in range(CPT)]
    gl = jnp.concatenate([jnp.broadcast_to(row, (CHUNK, GLA_QK)) for row in totals], axis=0)
    q = q_ref[0].astype(F32) * (GLA_DK ** -0.5)
    k = k_ref[0].astype(F32)
    qg = (q * jnp.exp(gc)).astype(BF16)
    kn = (k * jnp.exp(-gc)).astype(BF16)
    kd = (k * jnp.exp(gl - gc)).astype(BF16)
    qg_s[...] = qg
    egl_s[...] = jnp.exp(gl)
    causal = _causal_mask(is_fwd, CHUNK)

    pairs = [(h, ci) for h in range(GLA_HEADS) for ci in range(CPT)]
    rows_of = lambda ci: slice(ci * CHUNK, (ci + 1) * CHUNK)
    kcols = lambda h: slice(h * GLA_DK, (h + 1) * GLA_DK)
    vcols = lambda h: slice(h * GLA_DV, (h + 1) * GLA_DV)
    vs = [v_ref[0, rows_of(ci), vcols(h)] for h, ci in pairs]
    atts = [jnp.where(causal, _dot_nt(qg[rows_of(ci), kcols(h)], kn[rows_of(ci), kcols(h)]), 0.0).astype(BF16)
            for h, ci in pairs]
    for n, (h, ci) in enumerate(pairs):
        oi_s[rows_of(ci), vcols(h)] = _dot(atts[n], vs[n])
        u_s[n] = _dot(vs[n].astype(F32).T.astype(BF16), kd[rows_of(ci), kcols(h)])

    for ci in range(CPT):
        cidx = jnp.where(is_fwd, ci, CPT - 1 - ci)
        off = pl.multiple_of(cidx * CHUNK, CHUNK)
        rows = pl.ds(off, CHUNK)
        for h in range(GLA_HEADS):
            st = st_s[h]
            o_ref[0, 0, rows, vcols(h)] = (oi_s[rows, vcols(h)]
                                           + _dot_nt(qg_s[rows, kcols(h)], st.astype(BF16))).astype(BF16)
            st_s[h] = st * egl_s[pl.ds(off, 1), kcols(h)] + u_s[h * CPT + cidx]


def gla_scan(proj, small, wg, bg):
    bsz, t, _ = proj.shape
    nt = t // TT
    tile = lambda d, j: _scan_tile(d, j, nt)
    return pl.pallas_call(
        _gla_kernel,
        grid=(bsz, 2, nt),
        in_specs=[
            pl.BlockSpec((1, TT, GLA_QK), lambda i, d, j: (i, tile(d, j), AB_Q // GLA_QK)),
            pl.BlockSpec((1, TT, GLA_QK), lambda i, d, j: (i, tile(d, j), AB_K // GLA_QK)),
            pl.BlockSpec((1, TT, GLA_V), lambda i, d, j: (i, tile(d, j), AB_V // GLA_V)),
            pl.BlockSpec((1, TT, LANE), lambda i, d, j: (i, tile(d, j), 0)),
            pl.BlockSpec((1, LANE, GLA_QK), lambda i, d, j: (d, 0, 0)),
            pl.BlockSpec((1, 1, GLA_QK), lambda i, d, j: (d, 0, 0)),
        ],
        out_specs=pl.BlockSpec((1, 1, TT, GLA_V), lambda i, d, j: (d, i, tile(d, j), 0)),
        out_shape=jax.ShapeDtypeStruct((2, bsz, t, GLA_V), BF16),
        scratch_shapes=[pltpu.VMEM((TT, GLA_QK), BF16), pltpu.VMEM((TT, GLA_QK), F32),
                        pltpu.VMEM((TT, GLA_V), F32), pltpu.VMEM((GLA_HEADS * CPT, GLA_DV, GLA_DK), F32),
                        pltpu.VMEM((GLA_HEADS, GLA_DV, GLA_DK), F32)],
        compiler_params=pltpu.CompilerParams(
            dimension_semantics=("parallel", "parallel", "arbitrary")),
        name="gla_scan",
    )(proj, proj, proj, small, wg, bg)


def _dot_sel2(x, e):
    hi = x.astype(BF16)
    return _dot(hi, e) + _dot((x - hi.astype(F32)).astype(BF16), e)


def _ssd_kernel(xs_ref, bm_ref, cm_ref, sm_ref, dtb_ref, nega_ref, e_ref, o_ref,
                v_s, vw_s, cdec_s, dec_s, ah_s, st_s):
    d = pl.program_id(1)
    j = pl.program_id(2)
    is_fwd = d == 0
    gw = SSD_HPG * SSD_HEADDIM

    @pl.when(j == 0)
    def _():
        st_s[...] = jnp.zeros_like(st_s)

    dt = _softplus(sm_ref[0] + dtb_ref[0])
    la = dt * nega_ref[0]
    m_cum, m_all = _chunk_masks(is_fwd, TT)
    acum = _sel_dot(m_cum, la)
    atot = _sel_dot(m_all, la)
    e = e_ref[0]
    v = xs_ref[0].astype(F32) * _dot_sel2(dt, e)
    v_s[...] = v.astype(BF16)
    vw_s[...] = (v * _dot_sel2(jnp.exp(atot - acum), e)).astype(BF16)
    cdec_s[...] = _dot_sel2(jnp.exp(acum), e)
    etot = jnp.exp(atot)
    tot_rows = [etot[ci * CHUNK:ci * CHUNK + 1, :] for ci in range(CPT)]
    dec_s[...] = _dot_sel2(jnp.concatenate(tot_rows + [jnp.zeros((SUBLANES - CPT, LANE), F32)], axis=0), e)
    ah_s[...] = pltpu.roll(acum, LANE - SM_DT - d * SSD_HEADS, 1)
    causal = _causal_mask(is_fwd, CHUNK)

    for ci in range(CPT):
        cidx = jnp.where(is_fwd, ci, CPT - 1 - ci)
        off = pl.multiple_of(cidx * CHUNK, CHUNK)
        rows = pl.ds(off, CHUNK)
        ah = ah_s[rows, :]
        aht = ah.T
        for g in range(SSD_GROUPS):
            gc = slice(g * gw, (g + 1) * gw)
            nc = slice(g * SSD_STATE, (g + 1) * SSD_STATE)
            bm = bm_ref[0, rows, nc]
            cm = cm_ref[0, rows, nc]
            st = st_s[g]
            cb = _dot_nt(cm, bm)
            y_inter = _dot(cm, st.astype(BF16)) * cdec_s[rows, gc]
            v_c = v_s[rows, gc]
            ys = []
            for hh in range(SSD_HPG):
                h = g * SSD_HPG + hh
                seg = jnp.exp(jnp.where(causal, ah[:, h:h + 1] - aht[h:h + 1, :], -1e30))
                ys.append(_dot((seg * cb).astype(BF16), v_c[:, hh * SSD_HEADDIM:(hh + 1) * SSD_HEADDIM]))
            o_ref[0, 0, rows, gc] = (jnp.concatenate(ys, axis=1) + y_inter).astype(BF16)
            st_s[g] = st * dec_s[pl.ds(cidx, 1), gc] + _dot(bm.astype(F32).T.astype(BF16), vw_s[rows, gc])


def ssd_scan(xbc, small, dtb, nega, e):
    bsz, t, _ = xbc.shape
    nt = t // TT
    gw = SSD_HPG * SSD_HEADDIM
    tile = lambda d, j: _scan_tile(d, j, nt)
    return pl.pallas_call(
        _ssd_kernel,
        grid=(bsz, 2, nt),
        in_specs=[
            pl.BlockSpec((1, TT, SSD_INNER), lambda i, d, j: (i, tile(d, j), 0)),
            pl.BlockSpec((1, TT, SSD_BC), lambda i, d, j: (i, tile(d, j), SSD_INNER // SSD_BC)),
            pl.BlockSpec((1, TT, SSD_BC), lambda i, d, j: (i, tile(d, j), SSD_INNER // SSD_BC + 1)),
            pl.BlockSpec((1, TT, LANE), lambda i, d, j: (i, tile(d, j), 0)),
            pl.BlockSpec((1, 1, LANE), lambda i, d, j: (d, 0, 0)),
            pl.BlockSpec((1, 1, LANE), lambda i, d, j: (d, 0, 0)),
            pl.BlockSpec((1, LANE, SSD_INNER), lambda i, d, j: (d, 0, 0)),
        ],
        out_specs=pl.BlockSpec((1, 1, TT, SSD_INNER), lambda i, d, j: (d, i, tile(d, j), 0)),
        out_shape=jax.ShapeDtypeStruct((2, bsz, t, SSD_INNER), BF16),
        scratch_shapes=[pltpu.VMEM((TT, SSD_INNER), BF16), pltpu.VMEM((TT, SSD_INNER), BF16),
                        pltpu.VMEM((TT, SSD_INNER), F32), pltpu.VMEM((SUBLANES, SSD_INNER), F32),
                        pltpu.VMEM((TT, LANE), F32), pltpu.VMEM((SSD_GROUPS, SSD_STATE, gw), F32)],
        compiler_params=pltpu.CompilerParams(
            dimension_semantics=("parallel", "parallel", "arbitrary")),
        name="ssd_scan",
    )(xbc, xbc, xbc, small, dtb, nega, e)


def _group_rmsnorm(x, width):
    parts = []
    for s in range(x.shape[1] // width):
        seg = x[:, s * width:(s + 1) * width]
        parts.append(seg * lax.rsqrt(jnp.mean(seg * seg, axis=-1, keepdims=True) + 1e-6))
    return jnp.concatenate(parts, axis=1)


def _ab_out_kernel(of_ref, ob_ref, yf_ref, yb_ref, r_ref, z_ref, xs_ref, gg_ref, dv_ref, sg_ref,
                   w_ref, b_ref, x_ref, gate_ref, o_ref):
    f32 = lambda ref: ref[...].astype(F32)[0]
    o = _group_rmsnorm(f32(of_ref)[0] + f32(ob_ref)[0], GLA_DV) * gg_ref[...] * _silu(f32(r_ref))
    y = (f32(yf_ref)[0] + f32(yb_ref)[0] + dv_ref[...] * f32(xs_ref)) * _silu(f32(z_ref))
    y = _group_rmsnorm(y, SSD_INNER // SSD_GROUPS) * sg_ref[...]
    m = _dot(o.astype(BF16), w_ref[0:GLA_V, :]) + _dot(y.astype(BF16), w_ref[GLA_V:, :]) + b_ref[...]
    o_ref[0] = x_ref[0] + gate_ref[0, 0] * m


def ab_out(o_gla, y_ssd, proj, xbc, gla_g, d_vec, ssd_g, w, b, x, gate):
    bsz, t, d = x.shape
    seg = lambda i, j: (i, jnp.minimum(j, 1), 0, 0)
    row = lambda width: pl.BlockSpec((1, width), lambda i, j: (0, 0))
    return pl.pallas_call(
        _ab_out_kernel,
        grid=(bsz, t // TT),
        in_specs=[
            pl.BlockSpec((1, 1, TT, GLA_V), lambda i, j: (0, i, j, 0)),
            pl.BlockSpec((1, 1, TT, GLA_V), lambda i, j: (1, i, j, 0)),
            pl.BlockSpec((1, 1, TT, SSD_INNER), lambda i, j: (0, i, j, 0)),
            pl.BlockSpec((1, 1, TT, SSD_INNER), lambda i, j: (1, i, j, 0)),
            pl.BlockSpec((1, TT, GLA_V), lambda i, j: (i, j, AB_R // GLA_V)),
            pl.BlockSpec((1, TT, SSD_INNER), lambda i, j: (i, j, AB_Z // SSD_INNER)),
            pl.BlockSpec((1, TT, SSD_INNER), lambda i, j: (i, j, 0)),
            row(GLA_V), row(SSD_INNER), row(SSD_INNER),
            pl.BlockSpec((GLA_V + SSD_INNER, d), lambda i, j: (0, 0)),
            row(d),
            pl.BlockSpec((1, TT, d), lambda i, j: (i, j, 0)),
            pl.BlockSpec((1, 1, 1, d), seg),
        ],
        out_specs=pl.BlockSpec((1, TT, d), lambda i, j: (i, j, 0)),
        out_shape=jax.ShapeDtypeStruct((bsz, t, d), F32),
        compiler_params=pltpu.CompilerParams(
            dimension_semantics=("parallel", "parallel"), vmem_limit_bytes=VMEM_LIMIT),
        name="ab_out",
    )(o_gla, o_gla, y_ssd, y_ssd, proj, proj, xbc, gla_g.reshape(1, -1), d_vec.reshape(1, -1),
      ssd_g.reshape(1, -1), w, b.reshape(1, d), x, gate)


def _cd_prep_kernel(q_ref, k_ref, v_ref, ql_ref, kl_ref, vl_ref, qr_ref, kr_ref, vr_ref, w_ref, o_ref):
    j = pl.program_id(1)
    nt = pl.num_programs(1)
    left_ok = j >= 2
    right_ok = jnp.logical_and(j >= 1, j < nt - 1)
    w = w_ref[...]
    srcs = ((q_ref, ql_ref, qr_ref, GDN_DK ** -0.5), (k_ref, kl_ref, kr_ref, 1.0), (v_ref, vl_ref, vr_ref, None))
    for s, (c_ref, l_ref, r_ref, scale) in enumerate(srcs):
        c0 = s * GDN_QK
        y = _conv3_piece(c_ref, l_ref, r_ref, w[:, c0:c0 + GDN_QK], 0.0, left_ok, right_ok)
        if scale is None:
            o_ref[0, :, c0:c0 + GDN_QK] = y.astype(BF16)
            continue
        for h in range(GDN_HEADS):
            seg = y[:, h * GDN_DK:(h + 1) * GDN_DK]
            inv = lax.rsqrt(jnp.sum(seg * seg, axis=-1, keepdims=True) + 1e-6) * scale
            o_ref[0, :, c0 + h * GDN_DK:c0 + (h + 1) * GDN_DK] = (seg * inv).astype(BF16)


def cd_prep(proj, conv_w):
    bsz, t, _ = proj.shape
    nt = t // TT
    rb = TT // HALO
    nrb = t // HALO
    width = GDN_QK
    cols = (2, 3, 4)

    cur = lambda cb: pl.BlockSpec((1, TT, width), lambda i, j: (i, j, cb))
    left = lambda cb: pl.BlockSpec((1, HALO, width), lambda i, j: (i, jnp.maximum(j * rb - 1, 0), cb))
    right = lambda cb: pl.BlockSpec((1, HALO, width), lambda i, j: (i, jnp.minimum((j + 1) * rb, nrb - 1), cb))
    return pl.pallas_call(
        _cd_prep_kernel,
        grid=(bsz, nt),
        in_specs=[cur(cb) for cb in cols] + [left(cb) for cb in cols] + [right(cb) for cb in cols] + [
            pl.BlockSpec((3, 3 * width), lambda i, j: (0, 0))],
        out_specs=pl.BlockSpec((1, TT, 3 * width), lambda i, j: (i, j, 0)),
        out_shape=jax.ShapeDtypeStruct((bsz, t, 3 * width), BF16),
        compiler_params=pltpu.CompilerParams(dimension_semantics=("parallel", "parallel")),
        name="cd_prep",
    )(*([proj] * 9), conv_w)


GDN_HB = 8


def _mm2(a, b):
    return _dot(a.astype(BF16), b.astype(BF16))


def _unit_tri_inverse(mats, b16, b32, eye):
    each = lambda f, *ls: [f(*xs) for xs in zip(*ls)]
    d16 = each(lambda a: jnp.where(b16, a, 0.0), mats)
    d2 = each(lambda x: _mm2(x, x), d16)
    d4 = each(lambda x: _mm2(x, x), d2)
    d8 = each(lambda x: _mm2(x, x), d4)
    t = each(lambda x: eye - x, d16)
    for p in (d2, d4, d8):
        t = each(lambda x, y: x + _mm2(x, y), t, p)
    off32 = jnp.logical_and(b32, jnp.logical_not(b16))
    for sel in (off32, jnp.logical_not(b32)):
        a_off = each(lambda a: jnp.where(sel, a, 0.0), mats)
        inner = each(_mm2, a_off, t)
        t = each(lambda x, y: x - _mm2(x, y), t, inner)
    return t


def _gdn_kernel(q_ref, k_ref, v_ref, sm_ref, dtb_ref, nega_ref, o_ref,
                dec_s, n_s, p_s, oc_s, qp_s, st_s):
    d = pl.program_id(2)
    j = pl.program_id(3)
    is_fwd = d == 0

    @pl.when(j == 0)
    def _():
        st_s[...] = jnp.zeros_like(st_s)

    sm = sm_ref[0]
    m_cum, m_all = _chunk_masks(is_fwd, TT)
    first = d * GDN_HEADS + pl.program_id(1) * GDN_HB
    rot = jnp.where(first == 0, 0, LANE - first)
    la = pltpu.roll(_softplus(sm + dtb_ref[...]) * nega_ref[...], rot, 1)
    be_sm = pltpu.roll(jax.nn.sigmoid(sm), rot, 1)
    gc_sm = _sel_dot(m_cum, la)
    gl_sm = _sel_dot(m_all, la)
    lane_bcast = lambda a, col: jnp.broadcast_to(a[:, col:col + 1], (TT, GDN_DK))

    r = lax.broadcasted_iota(jnp.int32, (CHUNK, CHUNK), 0)
    c = lax.broadcasted_iota(jnp.int32, (CHUNK, CHUNK), 1)
    causal = _causal_mask(is_fwd, CHUNK)
    strict = jnp.logical_and(causal, r != c)
    b16 = lax.shift_right_logical(r, 4) == lax.shift_right_logical(c, 4)
    b32 = lax.shift_right_logical(r, 5) == lax.shift_right_logical(c, 5)
    eye = jnp.where(r == c, 1.0, 0.0)
    chunk_rows = [slice(ci * CHUNK, (ci + 1) * CHUNK) for ci in range(CPT)]

    amats, rhss, aqks, kdts, qgs = [], [], [], [], []
    for hh in range(GDN_HB):
        cols = slice(hh * GDN_DK, (hh + 1) * GDN_DK)
        gc = lane_bcast(gc_sm, hh)
        gl = lane_bcast(gl_sm, hh)
        beta_e = lane_bcast(be_sm, 2 * GDN_HEADS + hh)
        q = q_ref[0, :, cols].astype(F32)
        k = k_ref[0, :, cols].astype(F32)
        egc = jnp.exp(gc)
        kb = k * beta_e
        qg = q * egc
        kd = k * jnp.exp(gl - gc)
        dec_s[:, cols] = jnp.exp(gl)
        rhs = jnp.concatenate([v_ref[0, :, cols].astype(F32) * beta_e, kb * egc], axis=1)
        for rows in chunk_rows:
            gcc = gc[rows, :]
            dmat = jnp.exp(jnp.where(causal, gcc[:, 0:CHUNK] - gcc.T[0:CHUNK, :], -1e30))
            kc = k[rows].astype(BF16)
            amats.append(jnp.where(strict, _dot_nt(kb[rows].astype(BF16), kc) * dmat, 0.0))
            rhss.append(rhs[rows])
            aqks.append((_dot_nt(q[rows].astype(BF16), kc) * dmat).astype(BF16))
            kdts.append(kd[rows].T.astype(BF16))
            qgs.append(qg[rows])
    tinv = _unit_tri_inverse(amats, b16, b32, eye)
    sols = [_mm2(t, rhs_c).astype(BF16) for t, rhs_c in zip(tinv, rhss)]
    for n, sol in enumerate(sols):
        hh, ci = n // CPT, n % CPT
        rows, cols = chunk_rows[ci], slice(hh * GDN_DK, (hh + 1) * GDN_DK)
        ks = _dot(kdts[n], sol)
        qs = _dot(aqks[n], sol)
        n_s[n] = ks[:, 0:GDN_DV]
        p_s[n] = ks[:, GDN_DV:].astype(BF16)
        oc_s[rows, cols] = qs[:, 0:GDN_DV]
        qp_s[rows, cols] = (qgs[n] - qs[:, GDN_DV:]).astype(BF16)

    for ci in range(CPT):
        cidx = jnp.where(is_fwd, ci, CPT - 1 - ci)
        off = pl.multiple_of(cidx * CHUNK, CHUNK)
        rows = pl.ds(off, CHUNK)
        for hh in range(GDN_HB):
            cols = slice(hh * GDN_DK, (hh + 1) * GDN_DK)
            st = st_s[hh]
            stb = st.astype(BF16)
            o_ref[0, 0, rows, cols] = (_dot(qp_s[rows, cols], stb) + oc_s[rows, cols]).astype(BF16)
            st_s[hh] = st * dec_s[pl.ds(off, 1), cols] - _dot(p_s[hh * CPT + cidx], stb) + n_s[hh * CPT + cidx]


def gdn_scan(qkv, small, dtb, nega):
    bsz, t, _ = qkv.shape
    nt = t // TT
    tile = lambda d, j: _scan_tile(d, j, nt)
    ng = GDN_HEADS // GDN_HB
    wb = GDN_HB * GDN_DK
    return pl.pallas_call(
        _gdn_kernel,
        grid=(bsz, ng, 2, nt),
        in_specs=[
            pl.BlockSpec((1, TT, wb), lambda i, h, d, j: (i, tile(d, j), h)),
            pl.BlockSpec((1, TT, wb), lambda i, h, d, j: (i, tile(d, j), ng + h)),
            pl.BlockSpec((1, TT, wb), lambda i, h, d, j: (i, tile(d, j), 2 * ng + h)),
            pl.BlockSpec((1, TT, LANE), lambda i, h, d, j: (i, tile(d, j), 0)),
            pl.BlockSpec((1, LANE), lambda i, h, d, j: (0, 0)),
            pl.BlockSpec((1, LANE), lambda i, h, d, j: (0, 0)),
        ],
        out_specs=pl.BlockSpec((1, 1, TT, wb), lambda i, h, d, j: (d, i, tile(d, j), h)),
        out_shape=jax.ShapeDtypeStruct((2, bsz, t, GDN_V), BF16),
        scratch_shapes=[pltpu.VMEM((TT, wb), F32),
                        pltpu.VMEM((GDN_HB * CPT, GDN_DK, GDN_DV), F32),
                        pltpu.VMEM((GDN_HB * CPT, GDN_DK, GDN_DK), BF16),
                        pltpu.VMEM((TT, wb), F32), pltpu.VMEM((TT, wb), BF16),
                        pltpu.VMEM((GDN_HB, GDN_DK, GDN_DV), F32)],
        compiler_params=pltpu.CompilerParams(
            dimension_semantics=("parallel", "parallel", "parallel", "arbitrary")),
        name="gdn_scan",
    )(qkv, qkv, qkv, small, dtb, nega)


CONF_PAD = 16


def _cd_out_kernel(ga_ref, gb_ref, og_ref, of_ref, ob_ref, cw_ref, cb_ref, lg_ref, lb_ref, ng_ref,
                   w_ref, b_ref, x_ref, gate_ref, o_ref, pad_s, shift_s, conv_s):
    j = pl.program_id(1)
    half = (CONF_KERNEL - 1) // 2
    glu = ga_ref[0].astype(F32) * jax.nn.sigmoid(gb_ref[0].astype(F32))
    zeros = jnp.zeros((CONF_PAD, CONF_CH), F32)

    def conv_segments(seglen):
        stride = seglen + 2 * CONF_PAD
        for g in range(TT // seglen):
            base = g * stride
            pad_s[base:base + CONF_PAD, :] = zeros
            pad_s[base + CONF_PAD:base + CONF_PAD + seglen, :] = glu[g * seglen:(g + 1) * seglen]
            pad_s[base + CONF_PAD + seglen:base + stride, :] = zeros
        nseg = TT // seglen
        used = nseg * stride
        conv_s[...] = jnp.zeros((TT, CONF_CH), F32) + cb_ref[...]
        for phase in range(SUBLANES):
            taps = [kk for kk in range(CONF_KERNEL) if (CONF_PAD - half + kk) % SUBLANES == phase]
            if phase:
                shift_s[0:used - SUBLANES, :] = pad_s[phase:used - SUBLANES + phase, :]
            src = shift_s if phase else pad_s
            for g in range(nseg):
                acc = conv_s[g * seglen:(g + 1) * seglen, :]
                for kk in taps:
                    lo = g * stride + CONF_PAD - half + kk - phase
                    acc = acc + src[lo:lo + seglen, :] * cw_ref[kk:kk + 1, :]
                conv_s[g * seglen:(g + 1) * seglen, :] = acc

    @pl.when(j == 0)
    def _():
        conv_segments(TT)

    @pl.when(j > 0)
    def _():
        conv_segments(GRID_W)

    acc = conv_s[...]
    mu = jnp.mean(acc, axis=-1, keepdims=True)
    cen = acc - mu
    var = jnp.mean(cen * cen, axis=-1, keepdims=True)
    conv = _silu(cen * lax.rsqrt(var + 1e-5) * lg_ref[...] + lb_ref[...])
    o = (_group_rmsnorm(of_ref[0, 0].astype(F32) + ob_ref[0, 0].astype(F32), GDN_DV) * ng_ref[...]
         * _silu(og_ref[0].astype(F32)))
    m = _dot(conv.astype(BF16), w_ref[0:CONF_CH, :]) + _dot(o.astype(BF16), w_ref[CONF_CH:, :]) + b_ref[...]
    o_ref[0] = x_ref[0] + gate_ref[0, 0] * m


def cd_out(o_gdn, proj, conv_w, conv_b, ln_g, ln_b, norm_g, w, b, x, gate):
    bsz, t, d = x.shape
    seg = lambda i, j: (i, jnp.minimum(j, 1), 0, 0)
    row = lambda width: pl.BlockSpec((1, width), lambda i, j: (0, 0))
    return pl.pallas_call(
        _cd_out_kernel,
        grid=(bsz, t // TT),
        in_specs=[
            pl.BlockSpec((1, TT, CONF_CH), lambda i, j: (i, j, 0)),
            pl.BlockSpec((1, TT, CONF_CH), lambda i, j: (i, j, 1)),
            pl.BlockSpec((1, TT, GDN_V), lambda i, j: (i, j, 5)),
            pl.BlockSpec((1, 1, TT, GDN_V), lambda i, j: (0, i, j, 0)),
            pl.BlockSpec((1, 1, TT, GDN_V), lambda i, j: (1, i, j, 0)),
            pl.BlockSpec((CONF_KERNEL, CONF_CH), lambda i, j: (0, 0)),
            row(CONF_CH), row(CONF_CH), row(CONF_CH), row(GDN_V),
            pl.BlockSpec((CONF_CH + GDN_V, d), lambda i, j: (0, 0)),
            row(d),
            pl.BlockSpec((1, TT, d), lambda i, j: (i, j, 0)),
            pl.BlockSpec((1, 1, 1, d), seg),
        ],
        out_specs=pl.BlockSpec((1, TT, d), lambda i, j: (i, j, 0)),
        out_shape=jax.ShapeDtypeStruct((bsz, t, d), F32),
        scratch_shapes=[pltpu.VMEM(((TT // GRID_W) * (GRID_W + 2 * CONF_PAD), CONF_CH), F32),
                        pltpu.VMEM(((TT // GRID_W) * (GRID_W + 2 * CONF_PAD), CONF_CH), F32),
                        pltpu.VMEM((TT, CONF_CH), F32)],
        compiler_params=pltpu.CompilerParams(
            dimension_semantics=("parallel", "parallel"), vmem_limit_bytes=VMEM_LIMIT),
        name="cd_out",
    )(proj, proj, proj, o_gdn, o_gdn, conv_w, conv_b.reshape(1, -1), ln_g.reshape(1, -1), ln_b.reshape(1, -1),
      norm_g.reshape(1, -1), w, b.reshape(1, d), x, gate)


def _out_proj_kernel(m_ref, w_ref, b_ref, x_ref, gate_ref, o_ref):
    y = _dot(m_ref[0].astype(BF16), w_ref[...]) + b_ref[...]
    o_ref[0] = x_ref[0] + gate_ref[0] * y


def out_proj_residual(mixed, w, b, x, gate):
    bsz, t, k = mixed.shape
    d = w.shape[1]
    tm = min(t, 512)
    return pl.pallas_call(
        _out_proj_kernel,
        grid=(bsz, t // tm),
        in_specs=[
            pl.BlockSpec((1, tm, k), lambda i, j: (i, j, 0)),
            pl.BlockSpec((k, d), lambda i, j: (0, 0)),
            pl.BlockSpec((1, d), lambda i, j: (0, 0)),
            pl.BlockSpec((1, tm, d), lambda i, j: (i, j, 0)),
            pl.BlockSpec((1, 1, d), lambda i, j: (i, 0, 0)),
        ],
        out_specs=pl.BlockSpec((1, tm, d), lambda i, j: (i, j, 0)),
        out_shape=jax.ShapeDtypeStruct((bsz, t, d), F32),
        compiler_params=pltpu.CompilerParams(
            dimension_semantics=("parallel", "parallel"), vmem_limit_bytes=VMEM_LIMIT),
        name="out_proj",
    )(mixed, w, b.reshape(1, d), x, gate.reshape(bsz, 1, d))


def _router_kernel(x_ref, g_ref, sh_ref, sc_ref, rw_ref, h_ref, aff_ref):
    x = x_ref[0]
    ms = jnp.mean(x * x, axis=-1, keepdims=True)
    h = (x * lax.rsqrt(ms + 1e-6) * g_ref[...] * (1.0 + sc_ref[0, 0]) + sh_ref[0, 0]).astype(BF16)
    h_ref[0] = h
    logits = _dot(h, rw_ref[...])
    lane = lax.broadcasted_iota(jnp.int32, logits.shape, 1)
    logits = jnp.where(lane < N_EXPERTS, logits, -1e30)
    e = jnp.exp(logits - jnp.max(logits, axis=-1, keepdims=True))
    aff = e / jnp.sum(e, axis=-1, keepdims=True)
    aff_ref[0] = aff.T[0:N_EXPERTS, :]


def moe_router(xa, g, shift, scale, rw, seg, tile0, ntiles):
    bsz, _, d = xa.shape
    n = ntiles * TT
    return pl.pallas_call(
        _router_kernel,
        grid=(bsz, ntiles),
        in_specs=[
            pl.BlockSpec((1, TT, d), lambda i, j: (i, j + tile0, 0)),
            pl.BlockSpec((1, d), lambda i, j: (0, 0)),
            pl.BlockSpec((1, 1, 1, d), lambda i, j: (i, seg, 0, 0)),
            pl.BlockSpec((1, 1, 1, d), lambda i, j: (i, seg, 0, 0)),
            pl.BlockSpec((d, LANE), lambda i, j: (0, 0)),
        ],
        out_specs=[pl.BlockSpec((1, TT, d), lambda i, j: (i, j, 0)),
                   pl.BlockSpec((1, N_EXPERTS, TT), lambda i, j: (i, 0, j))],
        out_shape=[jax.ShapeDtypeStruct((bsz, n, d), BF16), jax.ShapeDtypeStruct((bsz, N_EXPERTS, n), F32)],
        compiler_params=pltpu.CompilerParams(dimension_semantics=("parallel", "parallel")),
        name="moe_router",
    )(xa, g.reshape(1, d), shift, scale, rw)


def _lane_block_prefix(x, u_strict):
    nblk = x.shape[1] // LANE
    run = jnp.zeros((x.shape[0], 1), F32)
    outs = []
    for cblk in range(nblk):
        xc = x[:, cblk * LANE:(cblk + 1) * LANE]
        outs.append(_dot(xc.astype(BF16), u_strict) + run)
        run = run + jnp.sum(xc, axis=-1, keepdims=True)
    return jnp.concatenate(outs, axis=1), run


def _select_kernel(aff_ref, slot_ref, *, cap):
    aff = aff_ref[0]
    bits = pltpu.bitcast(aff, jnp.int32)
    capf = jnp.float32(cap)

    def step(i, thr):
        cand = jnp.bitwise_or(thr, lax.shift_left(jnp.int32(1), 30 - i))
        cnt = jnp.sum(jnp.where(bits >= cand, 1.0, 0.0), axis=-1, keepdims=True)
        return jnp.where(cnt >= capf, cand, thr)

    thr = lax.fori_loop(0, 31, step, jnp.zeros((aff.shape[0], 1), jnp.int32))
    gt = jnp.where(bits > thr, 1.0, 0.0)
    eq = jnp.where(bits == thr, 1.0, 0.0)
    r = lax.broadcasted_iota(jnp.int32, (LANE, LANE), 0)
    c = lax.broadcasted_iota(jnp.int32, (LANE, LANE), 1)
    u_strict = jnp.where(r < c, 1.0, 0.0).astype(BF16)
    need = capf - jnp.sum(gt, axis=-1, keepdims=True)
    eq_rank, _ = _lane_block_prefix(eq, u_strict)
    sel = jnp.maximum(gt, jnp.where(eq_rank < need, eq, 0.0))
    slot, _ = _lane_block_prefix(sel, u_strict)
    slot_ref[0] = jnp.where(sel > 0.0, slot.astype(jnp.int32), -1)


def moe_select(aff, cap):
    bsz, ne, n = aff.shape
    return pl.pallas_call(
        functools.partial(_select_kernel, cap=cap),
        grid=(bsz,),
        in_specs=[pl.BlockSpec((1, ne, n), lambda i: (i, 0, 0))],
        out_specs=pl.BlockSpec((1, ne, n), lambda i: (i, 0, 0)),
        out_shape=jax.ShapeDtypeStruct((bsz, ne, n), jnp.int32),
        compiler_params=pltpu.CompilerParams(dimension_semantics=("parallel",)),
        name="moe_select",
    )(aff)


def _slot_index_kernel(slot_ref, idx_ref, *, cap):
    slot = slot_ref[0]
    n = slot.shape[1]
    srow = lax.broadcasted_iota(jnp.int32, (cap, LANE), 0)
    lane = lax.broadcasted_iota(jnp.int32, (cap, LANE), 1)
    acc = jnp.zeros((cap, LANE), jnp.int32)
    for cblk in range(n // LANE):
        s_c = slot[:, cblk * LANE:(cblk + 1) * LANE]
        acc = acc + jnp.where(srow == s_c, lane + (cblk * LANE + 1), 0)
    ones = jnp.ones((8, LANE), BF16)
    hi = _dot_nt(ones, lax.shift_right_logical(acc, 7).astype(F32).astype(BF16))
    lo = _dot_nt(ones, jnp.bitwise_and(acc, LANE - 1).astype(F32).astype(BF16))
    idx_ref[0] = (hi[0:1, :] * float(LANE) + lo[0:1, :]).astype(jnp.int32) - 1


SLOT_WIN = 2 * LANE


def _slot_index_win_kernel(base_ref, slot_ref, idx_ref, acc_s, *, cap):
    i = pl.program_id(0)
    n = slot_ref.shape[2]
    acc_s[...] = jnp.zeros(acc_s.shape, jnp.int32)
    srow = lax.broadcasted_iota(jnp.int32, (SLOT_WIN, LANE), 0)
    lane = lax.broadcasted_iota(jnp.int32, (SLOT_WIN, LANE), 1)
    for cblk in range(n // LANE):
        base = pl.multiple_of(base_ref[i, cblk], LANE)
        s_c = slot_ref[0, :, cblk * LANE:(cblk + 1) * LANE]
        rows = pl.ds(base, SLOT_WIN)
        acc_s[rows, :] = acc_s[rows, :] + jnp.where(srow + base == s_c, lane + (cblk * LANE + 1), 0)
    acc = acc_s[0:cap, :]
    ones = jnp.ones((8, LANE), BF16)
    hi = _dot_nt(ones, lax.shift_right_logical(acc, 7).astype(F32).astype(BF16))
    lo = _dot_nt(ones, jnp.bitwise_and(acc, LANE - 1).astype(F32).astype(BF16))
    idx_ref[0] = (hi[0:1, :] * float(LANE) + lo[0:1, :]).astype(jnp.int32) - 1


def moe_slot_index_windowed(slot, base, cap):
    bsz, ne, n = slot.shape
    idx = pl.pallas_call(
        functools.partial(_slot_index_win_kernel, cap=cap),
        grid_spec=pltpu.PrefetchScalarGridSpec(
            num_scalar_prefetch=1,
            grid=(bsz * ne,),
            in_specs=[pl.BlockSpec((1, 1, n), lambda i, base_ref: (i, 0, 0))],
            out_specs=pl.BlockSpec((1, 1, cap), lambda i, base_ref: (i, 0, 0)),
            scratch_shapes=[pltpu.VMEM((cap + LANE, LANE), jnp.int32)],
        ),
        out_shape=jax.ShapeDtypeStruct((bsz * ne, 1, cap), jnp.int32),
        compiler_params=pltpu.CompilerParams(dimension_semantics=("arbitrary",)),
        name="moe_slot_index",
    )(base, slot.reshape(bsz * ne, 1, n))
    return idx.reshape(bsz, ne, cap)


def moe_slot_index(slot, cap):
    bsz, ne, n = slot.shape
    idx = pl.pallas_call(
        functools.partial(_slot_index_kernel, cap=cap),
        grid=(bsz * ne,),
        in_specs=[pl.BlockSpec((1, 1, n), lambda i: (i, 0, 0))],
        out_specs=pl.BlockSpec((1, 1, cap), lambda i: (i, 0, 0)),
        out_shape=jax.ShapeDtypeStruct((bsz * ne, 1, cap), jnp.int32),
        compiler_params=pltpu.CompilerParams(dimension_semantics=("parallel",)),
        name="moe_slot_index",
    )(slot.reshape(bsz * ne, 1, n))
    return idx.reshape(bsz, ne, cap)


WIN_ALIGN = 16
WIN_FAST = 128


def _combine_kernel(ws_ref, slot_ref, aff_ref, *rest, win, final):
    ye_refs, (x_ref, gate_ref), o_ref = rest[:N_EXPERTS], rest[N_EXPERTS:N_EXPERTS + 2], rest[-1]
    b = pl.program_id(0)
    j = pl.program_id(1)
    srow = lax.broadcasted_iota(jnp.int32, (win, TT), 0)
    his, los = [], []
    for e in range(N_EXPERTS):
        sel = jnp.where(srow + ws_ref[b, e, j] == slot_ref[0, e:e + 1, :], aff_ref[0, e:e + 1, :], 0.0).T
        hi = sel.astype(BF16)
        his.append(hi)
        los.append((sel - hi.astype(F32)).astype(BF16))
    ye = jnp.concatenate([r[...] for r in ye_refs], axis=0)
    acc = _dot(jnp.concatenate(his, axis=1), ye) + _dot(jnp.concatenate(los, axis=1), ye)
    y = x_ref[0] + gate_ref[0, 0] * acc
    if final:
        g_ref = rest[N_EXPERTS + 2]
        y = y * lax.rsqrt(jnp.mean(y * y, axis=-1, keepdims=True) + 1e-6) * g_ref[...]
    o_ref[0] = y


def moe_combine(ws, slot, aff, ye, xa, gate, seg, tile0, win, final_g=None):
    bsz, ne, n = slot.shape
    nt = n // TT
    d = xa.shape[2]

    def ye_spec(e):
        return pl.BlockSpec((pl.Squeezed(), pl.Squeezed(), pl.Element(win), pl.Element(d)),
                            lambda i, j, ws_ref: (i, e, pl.multiple_of(ws_ref[i, e, j], WIN_ALIGN), 0))

    in_specs = ([pl.BlockSpec((1, ne, TT), lambda i, j, ws_ref: (i, 0, j)),
                 pl.BlockSpec((1, ne, TT), lambda i, j, ws_ref: (i, 0, j))]
                + [ye_spec(e) for e in range(ne)]
                + [pl.BlockSpec((1, TT, d), lambda i, j, ws_ref: (i, j + tile0, 0)),
                   pl.BlockSpec((1, 1, 1, d), lambda i, j, ws_ref: (i, seg, 0, 0))])
    args = (ws, slot, aff, *([ye] * ne), xa, gate)
    if final_g is None:
        out_idx, out_shape, aliases = (lambda i, j, ws_ref: (i, j + tile0, 0)), xa.shape, {3 + ne: 0}
    else:
        out_idx, out_shape, aliases = (lambda i, j, ws_ref: (i, j, 0)), (bsz, n, d), {}
        in_specs.append(pl.BlockSpec((1, d), lambda i, j, ws_ref: (0, 0)))
        args += (final_g.reshape(1, d),)
    return pl.pallas_call(
        functools.partial(_combine_kernel, win=win, final=final_g is not None),
        grid_spec=pltpu.PrefetchScalarGridSpec(
            num_scalar_prefetch=1,
            grid=(bsz, nt),
            in_specs=in_specs,
            out_specs=pl.BlockSpec((1, TT, d), out_idx),
        ),
        out_shape=jax.ShapeDtypeStruct(out_shape, F32),
        input_output_aliases=aliases,
        compiler_params=pltpu.CompilerParams(
            dimension_semantics=("parallel", "parallel"), vmem_limit_bytes=VMEM_LIMIT),
        name="moe_combine",
    )(*args)


def _expert_ffn_kernel(x_ref, w1_ref, w3_ref, w2_ref, o_ref, w1_s, w3_s, w2_s):
    @pl.when(jnp.logical_and(pl.program_id(1) == 0, pl.program_id(2) == 0))
    def _():
        w1_s[...] = w1_ref[0, 0].astype(BF16)
        w3_s[...] = w3_ref[0, 0].astype(BF16)
        w2_s[...] = w2_ref[0, 0].astype(BF16)

    x = x_ref[0, 0]
    a = _dot(x, w1_s[...])
    g = _dot(x, w3_s[...])
    o_ref[0, 0] = _dot((_silu(a) * g).astype(BF16), w2_s[...]).astype(BF16)


def expert_ffn(xe, w1, w3, w2, layer):
    bsz, ne, cap, d = xe.shape
    f = w1.shape[3]
    tm = min(cap, 1024)
    once = pl.Buffered(1)
    return pl.pallas_call(
        _expert_ffn_kernel,
        grid=(ne, bsz, cap // tm),
        in_specs=[
            pl.BlockSpec((1, 1, tm, d), lambda e, i, j: (i, e, j, 0)),
            pl.BlockSpec((1, 1, d, f), lambda e, i, j: (layer, e, 0, 0), pipeline_mode=once),
            pl.BlockSpec((1, 1, d, f), lambda e, i, j: (layer, e, 0, 0), pipeline_mode=once),
            pl.BlockSpec((1, 1, f, d), lambda e, i, j: (layer, e, 0, 0), pipeline_mode=once),
        ],
        out_specs=pl.BlockSpec((1, 1, tm, d), lambda e, i, j: (i, e, j, 0)),
        out_shape=jax.ShapeDtypeStruct((bsz, ne, cap, d), BF16),
        scratch_shapes=[pltpu.VMEM((d, f), BF16), pltpu.VMEM((d, f), BF16), pltpu.VMEM((f, d), BF16)],
        compiler_params=pltpu.CompilerParams(
            dimension_semantics=("arbitrary", "arbitrary", "arbitrary"), vmem_limit_bytes=VMEM_LIMIT),
        name="expert_ffn",
    )(xe, w1, w3, w2)


def _rmsnorm_kernel(x_ref, g_ref, o_ref):
    x = x_ref[0]
    ms = jnp.mean(x * x, axis=-1, keepdims=True)
    o_ref[0] = x * lax.rsqrt(ms + 1e-6) * g_ref[...]


def rmsnorm_rows(x, g, tile0):
    bsz, t, d = x.shape
    nt = t // TT - tile0
    return pl.pallas_call(
        _rmsnorm_kernel,
        grid=(bsz, nt),
        in_specs=[pl.BlockSpec((1, TT, d), lambda i, j: (i, j + tile0, 0)), pl.BlockSpec((1, d), lambda i, j: (0, 0))],
        out_specs=pl.BlockSpec((1, TT, d), lambda i, j: (i, j, 0)),
        out_shape=jax.ShapeDtypeStruct((bsz, nt * TT, d), F32),
        compiler_params=pltpu.CompilerParams(dimension_semantics=("parallel", "parallel")),
        name="final_rmsnorm",
    )(x, g.reshape(1, d))


def _ab_in_layout(w_in, b_in):
    q, k, v, r, glr, z, xs, bm, cm, dt = _split_cols(
        jnp.concatenate([w_in, b_in[None]], axis=0),
        (GLA_QK, GLA_QK, GLA_V, GLA_V, 2 * GLA_GATE_RANK, SSD_INNER, SSD_INNER, SSD_BC, SSD_BC, 2 * SSD_HEADS))
    wide = jnp.concatenate([q, k, v, r, z, xs, bm, cm], axis=1)
    narrow = _pad_cols(jnp.concatenate([glr, dt], axis=1), LANE)
    return wide[:-1].astype(BF16), wide[-1], narrow[:-1].astype(BF16), narrow[-1]


def _gla_gate_params(w_gate2, b_gate2):
    wg = jnp.zeros((2, LANE, GLA_QK), F32)
    for d in range(2):
        wg = wg.at[d, d * GLA_GATE_RANK:(d + 1) * GLA_GATE_RANK, :].set(w_gate2[d])
    return wg.astype(BF16), b_gate2.reshape(2, 1, GLA_QK)


def _ssd_params(dt_bias, a_log):
    dtb = jnp.zeros((2, 1, LANE), F32)
    nega = jnp.zeros((2, 1, LANE), F32)
    e = np.zeros((2, LANE, SSD_INNER), np.float32)
    for d in range(2):
        c0 = SM_DT + d * SSD_HEADS
        dtb = dtb.at[d, 0, c0:c0 + SSD_HEADS].set(dt_bias[d])
        nega = nega.at[d, 0, c0:c0 + SSD_HEADS].set(-jnp.exp(a_log[d]))
        for h in range(SSD_HEADS):
            e[d, c0 + h, h * SSD_HEADDIM:(h + 1) * SSD_HEADDIM] = 1.0
    return dtb, nega, jnp.asarray(e, BF16)


def _gdn_params(dt_bias, a_log):
    n = 2 * GDN_HEADS
    dtb = jnp.zeros((1, LANE), F32).at[0, 0:n].set(dt_bias.reshape(n))
    nega = jnp.zeros((1, LANE), F32).at[0, 0:n].set(-jnp.exp(a_log.reshape(n)))
    return dtb, nega


def _rmsnorm(x, g, eps=1e-6):
    return x * lax.rsqrt(jnp.mean(jnp.square(x), axis=-1, keepdims=True) + eps) * g


def _head_rmsnorm(x, g):
    return _rmsnorm(x, g.reshape(x.shape[-2:]))


def _layernorm(x, g, b, eps=1e-5):
    mu = jnp.mean(x, axis=-1, keepdims=True)
    var = jnp.mean(jnp.square(x - mu), axis=-1, keepdims=True)
    return (x - mu) * lax.rsqrt(var + eps) * g + b


def _l2norm(x, eps=1e-6):
    return x * lax.rsqrt(jnp.sum(jnp.square(x), axis=-1, keepdims=True) + eps)


def _dwconv(x, w, b=None):
    k, ch = w.shape
    pad = (k - 1) // 2
    y = lax.conv_general_dilated(x, w[:, None, :], (1,), [(pad, pad)],
                                 dimension_numbers=('NWC', 'WIO', 'NWC'), feature_group_count=ch)
    return y if b is None else y + b


def _gdn_scan(q, k, v, beta, logg, s0, with_out):
    bsz, nh, t, dk = q.shape
    dv = v.shape[-1]
    nc = t // CHUNK
    chunks = lambda z: z.reshape(bsz, nh, nc, CHUNK, *z.shape[3:])
    q, k, v, beta, logg = (chunks(z) for z in (q, k, v, beta, logg))
    gc = jnp.cumsum(logg, axis=-1)
    glast = gc[..., -1]
    tril = jnp.tril(jnp.ones((CHUNK, CHUNK), dtype=bool))
    strict = jnp.tril(jnp.ones((CHUNK, CHUNK), dtype=bool), k=-1)
    decay = jnp.exp(jnp.where(tril, gc[..., :, None] - gc[..., None, :], -jnp.inf))
    kb = k * beta[..., None]
    m = jnp.eye(CHUNK, dtype=k.dtype) + jnp.where(strict, jnp.einsum('bhcid,bhcjd->bhcij', kb, k) * decay, 0.0)
    rhs = jnp.concatenate([v * beta[..., None], kb * jnp.exp(gc)[..., None]], axis=-1)
    sol = lax.linalg.triangular_solve(m, rhs, left_side=True, lower=True, unit_diagonal=True)
    u, w = sol[..., :dv], sol[..., dv:]
    kd = k * jnp.exp(glast[..., None] - gc)[..., None]
    dec = jnp.exp(glast)
    front = lambda z: jnp.moveaxis(z, 2, 0)

    def advance(s, w_c, u_c, kd_c, dec_c):
        vn = u_c - jnp.einsum('bhid,bhde->bhie', w_c, s)
        return vn, dec_c[..., None, None] * s + jnp.einsum('bhjd,bhje->bhde', kd_c, vn)

    xs = (front(w), front(u), front(kd), front(dec))
    if not with_out:
        s_fin, _ = lax.scan(lambda s, xc: (advance(s, *xc)[1], None), s0, xs)
        return None, s_fin
    aqk = jnp.einsum('bhcid,bhcjd->bhcij', q, k) * decay
    qg = q * jnp.exp(gc)[..., None]

    def step(s, xc):
        w_c, u_c, kd_c, dec_c, aqk_c, qg_c = xc
        vn, s_new = advance(s, w_c, u_c, kd_c, dec_c)
        o = jnp.einsum('bhid,bhde->bhie', qg_c, s) + jnp.einsum('bhij,bhje->bhie', aqk_c, vn)
        return s_new, o

    s_fin, o = lax.scan(step, s0, xs + (front(aqk), front(qg)))
    return jnp.moveaxis(o, 0, 2).reshape(bsz, nh, t, dv), s_fin


def _bidir_scan(scan_fn, args_f, args_b, t_axis, init_f, init_b, with_out):
    flip = lambda a: jnp.flip(a, axis=t_axis)
    o_f, s_f = scan_fn(*args_f, init_f, with_out)
    o_b, s_b = scan_fn(*[flip(a) for a in args_b], init_b, with_out)
    o = o_f + flip(o_b) if with_out else None
    return o, s_f, s_b


def _conformer_conv(u, rows, dw_w, dw_b, ln_g, ln_b):
    bsz, t, ch = u.shape
    y = _dwconv(u.reshape(bsz * rows, t // rows, ch), dw_w, dw_b).reshape(bsz, t, ch)
    return jax.nn.silu(_layernorm(y, ln_g, ln_b))


def _cd_stream(proj, gdn_conv_w, gdn_a_log, gdn_dt_bias):
    bsz, t, _ = proj.shape
    ga, gb, q, k, v, og, a_raw, b_raw = _split_cols(proj, CD_SPLITS)
    glu = ga * jax.nn.sigmoid(gb)
    qkv = jax.nn.silu(_dwconv(jnp.concatenate([q, k, v], axis=-1), gdn_conv_w))
    q, k, v = _split_cols(qkv, (GDN_QK, GDN_QK, GDN_V))
    heads = lambda a: a.reshape(bsz, t, GDN_HEADS, -1).transpose(0, 2, 1, 3)
    q = _l2norm(heads(q)) * GDN_DK ** -0.5
    k = _l2norm(heads(k))
    v = heads(v)
    beta = jax.nn.sigmoid(b_raw.reshape(bsz, t, 2, GDN_HEADS)).transpose(2, 0, 3, 1)
    logg = (-jnp.exp(gdn_a_log)
            * jax.nn.softplus(a_raw.reshape(bsz, t, 2, GDN_HEADS) + gdn_dt_bias)).transpose(2, 0, 3, 1)
    return (q, k, v, beta[0], logg[0]), (q, k, v, beta[1], logg[1]), glu, og


def _mixer_cd(proj_l, proj_c, rows, conf_dw_w, conf_dw_b, conf_ln_g, conf_ln_b,
              gdn_conv_w, gdn_a_log, gdn_dt_bias, gdn_norm_g, need_ctx):
    c_f, c_b, c_glu, c_og = _cd_stream(proj_c, gdn_conv_w, gdn_a_log, gdn_dt_bias)
    l_f, l_b, l_glu, l_og = _cd_stream(proj_l, gdn_conv_w, gdn_a_log, gdn_dt_bias)
    bsz = proj_l.shape[0]
    z0 = jnp.zeros((bsz, GDN_HEADS, GDN_DK, GDN_DV), F32)
    od_c, s_f, s_b = _bidir_scan(_gdn_scan, c_f, c_b, 2, z0, z0, need_ctx)
    od_l, _, _ = _bidir_scan(_gdn_scan, l_f, l_b, 2, s_f, s_b, True)

    def mix(od, glu, og, n_rows):
        t = glu.shape[1]
        conv = _conformer_conv(glu, n_rows, conf_dw_w, conf_dw_b, conf_ln_g, conf_ln_b)
        o = _head_rmsnorm(jnp.swapaxes(od, 1, 2), gdn_norm_g) * jax.nn.silu(og).reshape(bsz, t, GDN_HEADS, GDN_DV)
        return jnp.concatenate([conv, o.reshape(bsz, t, GDN_V)], axis=-1)

    return mix(od_l, l_glu, l_og, rows), (mix(od_c, c_glu, c_og, 1) if need_ctx else None)


def moe_segment(i, xa, mods, p, w1, w3, w2, seg, tile0, ntiles, final_g=None):
    bsz, _, d = xa.shape
    n = ntiles * TT
    cap = n * EC_CAPACITY // N_EXPERTS
    rw = _pad_cols(p["moe_router"][i], LANE).astype(BF16)
    h, aff = moe_router(xa, p["norm2_g"][i], mods[:, :, 3], mods[:, :, 4], rw, seg, tile0, ntiles)
    slot = moe_select(aff, cap)
    cnt128 = jnp.sum((slot >= 0).reshape(bsz, N_EXPERTS, n // LANE, LANE), axis=-1, dtype=jnp.int32)
    start128 = jnp.cumsum(cnt128, axis=-1) - cnt128
    if cap >= SLOT_WIN:
        base = jnp.minimum(start128 // LANE * LANE, cap - LANE).reshape(bsz * N_EXPERTS, n // LANE)
        idx = moe_slot_index_windowed(slot, base, cap)
    else:
        idx = moe_slot_index(slot, cap)
    idx = idx.reshape(bsz, N_EXPERTS * cap)
    xe = jnp.take_along_axis(h, idx[..., None], axis=1, mode="promise_in_bounds").reshape(bsz, N_EXPERTS, cap, d)
    ye = expert_ffn(xe, w1, w3, w2, i)
    per_tile = TT // LANE
    counts = jnp.sum(cnt128.reshape(bsz, N_EXPERTS, ntiles, per_tile), axis=-1)
    starts = start128[:, :, ::per_tile]
    aligned = starts // WIN_ALIGN * WIN_ALIGN

    def run(win):
        ws = jnp.minimum(aligned, cap - win)
        return moe_combine(ws, slot, aff, ye, xa, mods[:, :, 5], seg, tile0, win, final_g)

    win_fast, win_full = min(cap, WIN_FAST), min(cap, TT + WIN_ALIGN)
    if win_fast == win_full:
        return run(win_full)
    overflow = jnp.any(starts + counts - jnp.minimum(aligned, cap - win_fast) > win_fast)
    return lax.cond(overflow, lambda: run(win_full), lambda: run(win_fast))


def layer_mixer(i, j, xa, mods, p, last):
    sh1, sc1, g1 = (mods[:, :, s] for s in range(3))
    if i % 2 == 0:
        proj, small = norm_proj(xa, p["norm1_g"][i], sh1, sc1, *_ab_in_layout(p["ab_w_in"][j], p["ab_b_in"][j]),
                                AB_WIDE // 2)
        xbc = ab_prep(proj, p["ssd_conv_w"][j], p["ssd_conv_b"][j])
        wg, bg = _gla_gate_params(p["gla_w_gate2"][j], p["gla_b_gate2"][j])
        o_gla = gla_scan(proj, small, wg, bg)
        y_ssd = ssd_scan(xbc, small, *_ssd_params(p["ssd_dt_bias"][j], p["ssd_a_log"][j]))
        return ab_out(o_gla, y_ssd, proj, xbc, p["gla_norm_g"][j], jnp.repeat(p["ssd_d"][j], SSD_HEADDIM),
                      p["ssd_norm_g"][j], p["ab_w_out"][j].astype(BF16), p["ab_b_out"][j], xa, g1)
    w_in, b_in = p["cd_w_in"][j], p["cd_b_in"][j]
    proj, small = norm_proj(xa, p["norm1_g"][i], sh1, sc1, w_in[:, :CD_WIDE].astype(BF16), b_in[:CD_WIDE],
                            _pad_cols(w_in[:, CD_WIDE:], LANE).astype(BF16), _pad_cols(b_in[CD_WIDE:], LANE),
                            CD_WIDE // 3)
    qkv = cd_prep(proj, p["gdn_conv_w"][j])
    o_gdn = gdn_scan(qkv, small, *_gdn_params(p["gdn_dt_bias"][j], p["gdn_a_log"][j]))
    return cd_out(o_gdn, proj, p["conf_dw_w"][j], p["conf_dw_b"][j], p["conf_ln_g"][j], p["conf_ln_b"][j],
                  p["gdn_norm_g"][j], p["cd_w_out"][j].astype(BF16), p["cd_b_out"][j], xa, g1)


def kernel(x, c, ctx, c_ctx, mod_w, mod_b, norm1_g, norm2_g, ab_w_in, ab_b_in, ab_w_out, ab_b_out, gla_w_gate2, gla_b_gate2, gla_norm_g, ssd_conv_w, ssd_conv_b, ssd_dt_bias, ssd_a_log, ssd_d, ssd_norm_g, cd_w_in, cd_b_in, cd_w_out, cd_b_out, conf_dw_w, conf_dw_b, conf_ln_g, conf_ln_b, gdn_conv_w, gdn_a_log, gdn_dt_bias, gdn_norm_g, moe_router, moe_w1, moe_w3, moe_w2, final_norm_g):
    p = dict(locals())
    bsz, seq, d = x.shape
    assert ctx.shape[1] == TT and seq % TT == 0
    depth = mod_w.shape[0]
    xa = jnp.concatenate([ctx, x], axis=1)
    w1, w3, w2 = moe_w1, moe_w3, moe_w2
    cond = jnp.concatenate([c, c_ctx[None]], axis=0)
    for i in range(depth):
        last = i == depth - 1
        mod = mod_proj(cond, mod_w[i].astype(BF16), mod_b[i])
        mods = jnp.stack([jnp.broadcast_to(mod[bsz], (bsz, 6 * d)), mod[:bsz]], axis=1).reshape(bsz, 2, 6, 1, d)
        xa = layer_mixer(i, i // 2, xa, mods, p, last)
        if last:
            return moe_segment(i, xa, mods, p, w1, w3, w2, 1, 1, seq // TT, final_norm_g)
        xa = moe_segment(i, xa, mods, p, w1, w3, w2, 1, 1, seq // TT)
        xa = moe_segment(i, xa, mods, p, w1, w3, w2, 0, 0, 1)
```

```python
import functools

import jax
import jax.numpy as jnp
import numpy as np
from jax import lax
from jax.experimental import pallas as pl
from jax.experimental.pallas import tpu as pltpu

F32 = jnp.float32
BF16 = jnp.bfloat16

D_MODEL = 1024
GRID_W = 64
CHUNK = 64
GLA_HEADS, GLA_DK, GLA_DV, GLA_GATE_RANK, GLA_GATE_TAU = 4, 128, 256, 16, 16.0
SSD_HEADS, SSD_HEADDIM, SSD_STATE, SSD_GROUPS = 16, 64, 128, 2
CONF_CH, CONF_KERNEL = D_MODEL, 31
GDN_HEADS, GDN_DK, GDN_DV = 8, 128, 128
N_EXPERTS, EC_CAPACITY, EXPERT_FF = 16, 2, D_MODEL

GLA_QK = GLA_HEADS * GLA_DK
GLA_V = GLA_HEADS * GLA_DV
SSD_INNER = SSD_HEADS * SSD_HEADDIM
SSD_BC = SSD_GROUPS * SSD_STATE
SSD_HPG = SSD_HEADS // SSD_GROUPS
GDN_QK = GDN_HEADS * GDN_DK
GDN_V = GDN_HEADS * GDN_DV
CD_SPLITS = (CONF_CH, CONF_CH, GDN_QK, GDN_QK, GDN_V, GDN_V, 2 * GDN_HEADS, 2 * GDN_HEADS)

LANE = 128
SUBLANES = 8
TT = 256
CPT = TT // CHUNK
VMEM_LIMIT = 48 * 1024 * 1024

AB_Q, AB_K, AB_V, AB_R, AB_Z, AB_XS, AB_BM, AB_CM, AB_WIDE = 0, 512, 1024, 2048, 3072, 4096, 5120, 5376, 5632
SM_DT = 2 * GLA_GATE_RANK
CD_WIDE = 6 * D_MODEL


def _split_cols(a, sizes):
    return jnp.split(a, np.cumsum(sizes)[:-1].tolist(), axis=-1)


def _pad_cols(a, n):
    return jnp.pad(a, [(0, 0)] * (a.ndim - 1) + [(0, n - a.shape[-1])])


def _dot(a, b):
    return jnp.dot(a, b, preferred_element_type=F32)


def _dot_nt(a, b):
    return lax.dot_general(a, b, (((1,), (1,)), ((), ())), preferred_element_type=F32)


def _split3(x):
    hi = x.astype(BF16)
    r = x - hi.astype(F32)
    mid = r.astype(BF16)
    lo = (r - mid.astype(F32)).astype(BF16)
    return hi, mid, lo


def _sel_dot(m, x):
    hi, mid, lo = _split3(x)
    return _dot(m, hi) + _dot(m, mid) + _dot(m, lo)


def _dot_sel(x, e):
    hi, mid, lo = _split3(x)
    return _dot(hi, e) + _dot(mid, e) + _dot(lo, e)


def _softplus(x):
    return jnp.maximum(x, 0.0) + jnp.log(1.0 + jnp.exp(-jnp.abs(x)))


def _silu(x):
    return x * jax.nn.sigmoid(x)


def _chunk_masks(is_fwd, n):
    r = lax.broadcasted_iota(jnp.int32, (n, n), 0)
    c = lax.broadcasted_iota(jnp.int32, (n, n), 1)
    same = lax.shift_right_logical(r, 6) == lax.shift_right_logical(c, 6)
    lo = jnp.where(is_fwd, c, r)
    hi = jnp.where(is_fwd, r, c)
    cum = jnp.logical_and(same, lo <= hi)
    return jnp.where(cum, 1.0, 0.0).astype(BF16), jnp.where(same, 1.0, 0.0).astype(BF16)


def _causal_mask(is_fwd, n):
    r = lax.broadcasted_iota(jnp.int32, (n, n), 0)
    c = lax.broadcasted_iota(jnp.int32, (n, n), 1)
    return jnp.where(is_fwd, c, r) <= jnp.where(is_fwd, r, c)


def _scan_tile(d, j, nt):
    return jnp.where(d == 0, j, jnp.where(j == 0, 0, nt - j))


def _mod_proj_kernel(c_ref, w_ref, b_ref, o_ref):
    o_ref[...] = _dot(_silu(c_ref[...]).astype(BF16), w_ref[...]) + b_ref[...]


def mod_proj(cond, w, b):
    r, d = cond.shape
    n = w.shape[1]
    tn = 6 * LANE * 2
    return pl.pallas_call(
        _mod_proj_kernel,
        grid=(n // tn,),
        in_specs=[pl.BlockSpec((r, d), lambda k: (0, 0)),
                  pl.BlockSpec((d, tn), lambda k: (0, k)),
                  pl.BlockSpec((1, tn), lambda k: (0, k))],
        out_specs=pl.BlockSpec((r, tn), lambda k: (0, k)),
        out_shape=jax.ShapeDtypeStruct((r, n), F32),
        compiler_params=pltpu.CompilerParams(dimension_semantics=("parallel",)),
        name="mod_proj",
    )(cond, w, b.reshape(1, n))


NP_TM = 3 * TT


def _norm_proj_kernel(x_ref, g_ref, sh_ref, sc_ref, w_ref, b_ref, ws_ref, bs_ref, o_ref, os_ref):
    x = x_ref[0]
    row = lax.broadcasted_iota(jnp.int32, (NP_TM, 1), 0) + pl.program_id(2) * NP_TM
    is_ctx = row < TT
    scale = jnp.where(is_ctx, sc_ref[0, 0], sc_ref[0, 1])
    shift = jnp.where(is_ctx, sh_ref[0, 0], sh_ref[0, 1])
    ms = jnp.mean(x * x, axis=-1, keepdims=True)
    h = (x * lax.rsqrt(ms + 1e-6) * g_ref[...] * (1.0 + scale) + shift).astype(BF16)
    o_ref[0] = (_dot(h, w_ref[...]) + b_ref[...]).astype(BF16)
    os_ref[0, 0] = _dot(h, ws_ref[...]) + bs_ref[...]


def norm_proj(x, g, shift, scale, w, b, w_small, b_small, tn):
    bsz, t, d = x.shape
    n = w.shape[1]
    assert t % NP_TM == 0 and n % tn == 0
    wide, narrow = pl.pallas_call(
        _norm_proj_kernel,
        grid=(n // tn, bsz, t // NP_TM),
        in_specs=[
            pl.BlockSpec((1, NP_TM, d), lambda k, i, j: (i, j, 0)),
            pl.BlockSpec((1, d), lambda k, i, j: (0, 0)),
            pl.BlockSpec((1, 2, 1, d), lambda k, i, j: (i, 0, 0, 0)),
            pl.BlockSpec((1, 2, 1, d), lambda k, i, j: (i, 0, 0, 0)),
            pl.BlockSpec((d, tn), lambda k, i, j: (0, k)),
            pl.BlockSpec((1, tn), lambda k, i, j: (0, k)),
            pl.BlockSpec((d, LANE), lambda k, i, j: (0, 0)),
            pl.BlockSpec((1, LANE), lambda k, i, j: (0, 0)),
        ],
        out_specs=[pl.BlockSpec((1, NP_TM, tn), lambda k, i, j: (i, j, k)),
                   pl.BlockSpec((1, 1, NP_TM, LANE), lambda k, i, j: (k, i, j, 0))],
        out_shape=[jax.ShapeDtypeStruct((bsz, t, n), BF16),
                   jax.ShapeDtypeStruct((n // tn, bsz, t, LANE), F32)],
        compiler_params=pltpu.CompilerParams(
            dimension_semantics=("parallel", "parallel", "parallel"), vmem_limit_bytes=VMEM_LIMIT),
        name="norm_proj",
    )(x, g.reshape(1, d), shift, scale, w, b.reshape(1, n), w_small, b_small.reshape(1, LANE))
    return wide, narrow[0]


HALO = 16


def _conv3_piece(c_ref, l_ref, r_ref, w, b, left_ok, right_ok):
    x = c_ref[0].astype(F32)
    n = x.shape[0]
    row = lax.broadcasted_iota(jnp.int32, x.shape, 0)
    prev_row = jnp.where(left_ok, l_ref[0, HALO - 1:HALO, :].astype(F32), 0.0)
    next_row = jnp.where(right_ok, r_ref[0, 0:1, :].astype(F32), 0.0)
    x_prev = jnp.where(row == 0, prev_row, pltpu.roll(x, 1, 0))
    x_next = jnp.where(row == n - 1, next_row, pltpu.roll(x, n - 1, 0))
    return _silu(w[0:1, :] * x_prev + w[1:2, :] * x + w[2:3, :] * x_next + b)


def _ab_prep_kernel(xs_ref, bm_ref, cm_ref, xsl_ref, bml_ref, cml_ref, xsr_ref, bmr_ref, cmr_ref,
                    w_ref, b_ref, o_ref):
    j = pl.program_id(1)
    nt = pl.num_programs(1)
    left_ok = j >= 2
    right_ok = jnp.logical_and(j >= 1, j < nt - 1)
    w = w_ref[...]
    b = b_ref[...]
    o_ref[0, :, 0:SSD_INNER] = _conv3_piece(xs_ref, xsl_ref, xsr_ref, w[:, 0:SSD_INNER],
                                            b[:, 0:SSD_INNER], left_ok, right_ok).astype(BF16)
    c0, c1 = SSD_INNER, SSD_INNER + SSD_BC
    o_ref[0, :, c0:c1] = _conv3_piece(bm_ref, bml_ref, bmr_ref, w[:, c0:c1], b[:, c0:c1],
                                      left_ok, right_ok).astype(BF16)
    c0, c1 = c1, c1 + SSD_BC
    o_ref[0, :, c0:c1] = _conv3_piece(cm_ref, cml_ref, cmr_ref, w[:, c0:c1], b[:, c0:c1],
                                      left_ok, right_ok).astype(BF16)


def ab_prep(proj, conv_w, conv_b):
    bsz, t, _ = proj.shape
    nt = t // TT
    rb = TT // HALO
    nrb = t // HALO
    cw = SSD_INNER + 2 * SSD_BC

    def cur(width, col):
        return pl.BlockSpec((1, TT, width), lambda i, j: (i, j, col // width))

    def left(width, col):
        return pl.BlockSpec((1, HALO, width), lambda i, j: (i, jnp.maximum(j * rb - 1, 0), col // width))

    def right(width, col):
        return pl.BlockSpec((1, HALO, width), lambda i, j: (i, jnp.minimum((j + 1) * rb, nrb - 1), col // width))

    pieces = ((SSD_INNER, AB_XS), (SSD_BC, AB_BM), (SSD_BC, AB_CM))
    return pl.pallas_call(
        _ab_prep_kernel,
        grid=(bsz, nt),
        in_specs=[cur(*p) for p in pieces] + [left(*p) for p in pieces] + [right(*p) for p in pieces] + [
            pl.BlockSpec((3, cw), lambda i, j: (0, 0)),
            pl.BlockSpec((1, cw), lambda i, j: (0, 0)),
        ],
        out_specs=pl.BlockSpec((1, TT, cw), lambda i, j: (i, j, 0)),
        out_shape=jax.ShapeDtypeStruct((bsz, t, cw), BF16),
        compiler_params=pltpu.CompilerParams(dimension_semantics=("parallel", "parallel")),
        name="ab_prep",
    )(*([proj] * 9), conv_w, conv_b.reshape(1, cw))


def _gla_kernel(q_ref, k_ref, v_ref, sm_ref, wg_ref, bg_ref, o_ref, qg_s, egl_s, oi_s, u_s, st_s):
    d = pl.program_id(1)
    j = pl.program_id(2)
    is_fwd = d == 0

    @pl.when(j == 0)
    def _():
        st_s[...] = jnp.zeros_like(st_s)

    gz = _dot(sm_ref[0].astype(BF16), wg_ref[0]) + bg_ref[0]
    logg = (jnp.minimum(gz, 0.0) - jnp.log(1.0 + jnp.exp(-jnp.abs(gz)))) * (1.0 / GLA_GATE_TAU)
    m_cum, m_all = _chunk_masks(is_fwd, TT)
    gc = _sel_dot(m_cum, logg)
    totals = [jnp.where(is_fwd, gc[(ci + 1) * CHUNK - 1:(ci + 1) * CHUNK, :], gc[ci * CHUNK:ci * CHUNK + 1, :])
              for ci in range(CPT)]
    gl = jnp.concatenate([jnp.broadcast_to(row, (CHUNK, GLA_QK)) for row in totals], axis=0)
    q = q_ref[0].astype(F32) * (GLA_DK ** -0.5)
    k = k_ref[0].astype(F32)
    qg = (q * jnp.exp(gc)).astype(BF16)
    kn = (k * jnp.exp(-gc)).astype(BF16)
    kd = (k * jnp.exp(gl - gc)).astype(BF16)
    qg_s[...] = qg
    egl_s[...] = jnp.exp(gl)
    causal = _causal_mask(is_fwd, CHUNK)

    pairs = [(h, ci) for h in range(GLA_HEADS) for ci in range(CPT)]
    rows_of = lambda ci: slice(ci * CHUNK, (ci + 1) * CHUNK)
    kcols = lambda h: slice(h * GLA_DK, (h + 1) * GLA_DK)
    vcols = lambda h: slice(h * GLA_DV, (h + 1) * GLA_DV)
    vs = [v_ref[0, rows_of(ci), vcols(h)] for h, ci in pairs]
    atts = [jnp.where(causal, _dot_nt(qg[rows_of(ci), kcols(h)], kn[rows_of(ci), kcols(h)]), 0.0).astype(BF16)
            for h, ci in pairs]
    for n, (h, ci) in enumerate(pairs):
        oi_s[rows_of(ci), vcols(h)] = _dot(atts[n], vs[n])
        u_s[n] = _dot(vs[n].astype(F32).T.astype(BF16), kd[rows_of(ci), kcols(h)])

    for ci in range(CPT):
        cidx = jnp.where(is_fwd, ci, CPT - 1 - ci)
        off = pl.multiple_of(cidx * CHUNK, CHUNK)
        rows = pl.ds(off, CHUNK)
        for h in range(GLA_HEADS):
            st = st_s[h]
            o_ref[0, 0, rows, vcols(h)] = (oi_s[rows, vcols(h)]
                                           + _dot_nt(qg_s[rows, kcols(h)], st.astype(BF16))).astype(BF16)
            st_s[h] = st * egl_s[pl.ds(off, 1), kcols(h)] + u_s[h * CPT + cidx]


def gla_scan(proj, small, wg, bg):
    bsz, t, _ = proj.shape
    nt = t // TT
    tile = lambda d, j: _scan_tile(d, j, nt)
    return pl.pallas_call(
        _gla_kernel,
        grid=(bsz, 2, nt),
        in_specs=[
            pl.BlockSpec((1, TT, GLA_QK), lambda i, d, j: (i, tile(d, j), AB_Q // GLA_QK)),
            pl.BlockSpec((1, TT, GLA_QK), lambda i, d, j: (i, tile(d, j), AB_K // GLA_QK)),
            pl.BlockSpec((1, TT, GLA_V), lambda i, d, j: (i, tile(d, j), AB_V // GLA_V)),
            pl.BlockSpec((1, TT, LANE), lambda i, d, j: (i, tile(d, j), 0)),
            pl.BlockSpec((1, LANE, GLA_QK), lambda i, d, j: (d, 0, 0)),
            pl.BlockSpec((1, 1, GLA_QK), lambda i, d, j: (d, 0, 0)),
        ],
        out_specs=pl.BlockSpec((1, 1, TT, GLA_V), lambda i, d, j: (d, i, tile(d, j), 0)),
        out_shape=jax.ShapeDtypeStruct((2, bsz, t, GLA_V), BF16),
        scratch_shapes=[pltpu.VMEM((TT, GLA_QK), BF16), pltpu.VMEM((TT, GLA_QK), F32),
                        pltpu.VMEM((TT, GLA_V), F32), pltpu.VMEM((GLA_HEADS * CPT, GLA_DV, GLA_DK), F32),
                        pltpu.VMEM((GLA_HEADS, GLA_DV, GLA_DK), F32)],
        compiler_params=pltpu.CompilerParams(
            dimension_semantics=("parallel", "parallel", "arbitrary")),
        name="gla_scan",
    )(proj, proj, proj, small, wg, bg)


def _dot_sel2(x, e):
    hi = x.astype(BF16)
    return _dot(hi, e) + _dot((x - hi.astype(F32)).astype(BF16), e)


def _ssd_kernel(xs_ref, bm_ref, cm_ref, sm_ref, dtb_ref, nega_ref, e_ref, o_ref,
                v_s, vw_s, cdec_s, dec_s, ah_s, st_s):
    d = pl.program_id(1)
    j = pl.program_id(2)
    is_fwd = d == 0
    gw = SSD_HPG * SSD_HEADDIM

    @pl.when(j == 0)
    def _():
        st_s[...] = jnp.zeros_like(st_s)

    dt = _softplus(sm_ref[0] + dtb_ref[0])
    la = dt * nega_ref[0]
    m_cum, m_all = _chunk_masks(is_fwd, TT)
    acum = _sel_dot(m_cum, la)
    atot = _sel_dot(m_all, la)
    e = e_ref[0]
    v = xs_ref[0].astype(F32) * _dot_sel2(dt, e)
    v_s[...] = v.astype(BF16)
    vw_s[...] = (v * _dot_sel2(jnp.exp(atot - acum), e)).astype(BF16)
    cdec_s[...] = _dot_sel2(jnp.exp(acum), e)
    etot = jnp.exp(atot)
    tot_rows = [etot[ci * CHUNK:ci * CHUNK + 1, :] for ci in range(CPT)]
    dec_s[...] = _dot_sel2(jnp.concatenate(tot_rows + [jnp.zeros((SUBLANES - CPT, LANE), F32)], axis=0), e)
    ah_s[...] = pltpu.roll(acum, LANE - SM_DT - d * SSD_HEADS, 1)
    causal = _causal_mask(is_fwd, CHUNK)

    for ci in range(CPT):
        cidx = jnp.where(is_fwd, ci, CPT - 1 - ci)
        off = pl.multiple_of(cidx * CHUNK, CHUNK)
        rows = pl.ds(off, CHUNK)
        ah = ah_s[rows, :]
        aht = ah.T
        for g in range(SSD_GROUPS):
            gc = slice(g * gw, (g + 1) * gw)
            nc = slice(g * SSD_STATE, (g + 1) * SSD_STATE)
            bm = bm_ref[0, rows, nc]
            cm = cm_ref[0, rows, nc]
            st = st_s[g]
            cb = _dot_nt(cm, bm)
            y_inter = _dot(cm, st.astype(BF16)) * cdec_s[rows, gc]
            v_c = v_s[rows, gc]
            ys = []
            for hh in range(SSD_HPG):
                h = g * SSD_HPG + hh
                seg = jnp.exp(jnp.where(causal, ah[:, h:h + 1] - aht[h:h + 1, :], -1e30))
                ys.append(_dot((seg * cb).astype(BF16), v_c[:, hh * SSD_HEADDIM:(hh + 1) * SSD_HEADDIM]))
            o_ref[0, 0, rows, gc] = (jnp.concatenate(ys, axis=1) + y_inter).astype(BF16)
            st_s[g] = st * dec_s[pl.ds(cidx, 1), gc] + _dot(bm.astype(F32).T.astype(BF16), vw_s[rows, gc])


def ssd_scan(xbc, small, dtb, nega, e):
    bsz, t, _ = xbc.shape
    nt = t // TT
    gw = SSD_HPG * SSD_HEADDIM
    tile = lambda d, j: _scan_tile(d, j, nt)
    return pl.pallas_call(
        _ssd_kernel,
        grid=(bsz, 2, nt),
        in_specs=[
            pl.BlockSpec((1, TT, SSD_INNER), lambda i, d, j: (i, tile(d, j), 0)),
            pl.BlockSpec((1, TT, SSD_BC), lambda i, d, j: (i, tile(d, j), SSD_INNER // SSD_BC)),
            pl.BlockSpec((1, TT, SSD_BC), lambda i, d, j: (i, tile(d, j), SSD_INNER // SSD_BC + 1)),
            pl.BlockSpec((1, TT, LANE), lambda i, d, j: (i, tile(d, j), 0)),
            pl.BlockSpec((1, 1, LANE), lambda i, d, j: (d, 0, 0)),
            pl.BlockSpec((1, 1, LANE), lambda i, d, j: (d, 0, 0)),
            pl.BlockSpec((1, LANE, SSD_INNER), lambda i, d, j: (d, 0, 0)),
        ],
        out_specs=pl.BlockSpec((1, 1, TT, SSD_INNER), lambda i, d, j: (d, i, tile(d, j), 0)),
        out_shape=jax.ShapeDtypeStruct((2, bsz, t, SSD_INNER), BF16),
        scratch_shapes=[pltpu.VMEM((TT, SSD_INNER), BF16), pltpu.VMEM((TT, SSD_INNER), BF16),
                        pltpu.VMEM((TT, SSD_INNER), F32), pltpu.VMEM((SUBLANES, SSD_INNER), F32),
                        pltpu.VMEM((TT, LANE), F32), pltpu.VMEM((SSD_GROUPS, SSD_STATE, gw), F32)],
        compiler_params=pltpu.CompilerParams(
            dimension_semantics=("parallel", "parallel", "arbitrary")),
        name="ssd_scan",
    )(xbc, xbc, xbc, small, dtb, nega, e)


def _group_rmsnorm(x, width):
    parts = []
    for s in range(x.shape[1] // width):
        seg = x[:, s * width:(s + 1) * width]
        parts.append(seg * lax.rsqrt(jnp.mean(seg * seg, axis=-1, keepdims=True) + 1e-6))
    return jnp.concatenate(parts, axis=1)


def _ab_out_kernel(of_ref, ob_ref, yf_ref, yb_ref, r_ref, z_ref, xs_ref, gg_ref, dv_ref, sg_ref,
                   w_ref, b_ref, x_ref, gate_ref, o_ref):
    f32 = lambda ref: ref[...].astype(F32)[0]
    o = _group_rmsnorm(f32(of_ref)[0] + f32(ob_ref)[0], GLA_DV) * gg_ref[...] * _silu(f32(r_ref))
    y = (f32(yf_ref)[0] + f32(yb_ref)[0] + dv_ref[...] * f32(xs_ref)) * _silu(f32(z_ref))
    y = _group_rmsnorm(y, SSD_INNER // SSD_GROUPS) * sg_ref[...]
    m = _dot(o.astype(BF16), w_ref[0:GLA_V, :]) + _dot(y.astype(BF16), w_ref[GLA_V:, :]) + b_ref[...]
    o_ref[0] = x_ref[0] + gate_ref[0, 0] * m


def ab_out(o_gla, y_ssd, proj, xbc, gla_g, d_vec, ssd_g, w, b, x, gate):
    bsz, t, d = x.shape
    seg = lambda i, j: (i, jnp.minimum(j, 1), 0, 0)
    row = lambda width: pl.BlockSpec((1, width), lambda i, j: (0, 0))
    return pl.pallas_call(
        _ab_out_kernel,
        grid=(bsz, t // TT),
        in_specs=[
            pl.BlockSpec((1, 1, TT, GLA_V), lambda i, j: (0, i, j, 0)),
            pl.BlockSpec((1, 1, TT, GLA_V), lambda i, j: (1, i, j, 0)),
            pl.BlockSpec((1, 1, TT, SSD_INNER), lambda i, j: (0, i, j, 0)),
            pl.BlockSpec((1, 1, TT, SSD_INNER), lambda i, j: (1, i, j, 0)),
            pl.BlockSpec((1, TT, GLA_V), lambda i, j: (i, j, AB_R // GLA_V)),
            pl.BlockSpec((1, TT, SSD_INNER), lambda i, j: (i, j, AB_Z // SSD_INNER)),
            pl.BlockSpec((1, TT, SSD_INNER), lambda i, j: (i, j, 0)),
            row(GLA_V), row(SSD_INNER), row(SSD_INNER),
            pl.BlockSpec((GLA_V + SSD_INNER, d), lambda i, j: (0, 0)),
            row(d),
            pl.BlockSpec((1, TT, d), lambda i, j: (i, j, 0)),
            pl.BlockSpec((1, 1, 1, d), seg),
        ],
        out_specs=pl.BlockSpec((1, TT, d), lambda i, j: (i, j, 0)),
        out_shape=jax.ShapeDtypeStruct((bsz, t, d), F32),
        compiler_params=pltpu.CompilerParams(
            dimension_semantics=("parallel", "parallel"), vmem_limit_bytes=VMEM_LIMIT),
        name="ab_out",
    )(o_gla, o_gla, y_ssd, y_ssd, proj, proj, xbc, gla_g.reshape(1, -1), d_vec.reshape(1, -1),
      ssd_g.reshape(1, -1), w, b.reshape(1, d), x, gate)


def _cd_prep_kernel(q_ref, k_ref, v_ref, ql_ref, kl_ref, vl_ref, qr_ref, kr_ref, vr_ref, w_ref, o_ref):
    j = pl.program_id(1)
    nt = pl.num_programs(1)
    left_ok = j >= 2
    right_ok = jnp.logical_and(j >= 1, j < nt - 1)
    w = w_ref[...]
    srcs = ((q_ref, ql_ref, qr_ref, GDN_DK ** -0.5), (k_ref, kl_ref, kr_ref, 1.0), (v_ref, vl_ref, vr_ref, None))
    for s, (c_ref, l_ref, r_ref, scale) in enumerate(srcs):
        c0 = s * GDN_QK
        y = _conv3_piece(c_ref, l_ref, r_ref, w[:, c0:c0 + GDN_QK], 0.0, left_ok, right_ok)
        if scale is None:
            o_ref[0, :, c0:c0 + GDN_QK] = y.astype(BF16)
            continue
        for h in range(GDN_HEADS):
            seg = y[:, h * GDN_DK:(h + 1) * GDN_DK]
            inv = lax.rsqrt(jnp.sum(seg * seg, axis=-1, keepdims=True) + 1e-6) * scale
            o_ref[0, :, c0 + h * GDN_DK:c0 + (h + 1) * GDN_DK] = (seg * inv).astype(BF16)


def cd_prep(proj, conv_w):
    bsz, t, _ = proj.shape
    nt = t // TT
    rb = TT // HALO
    nrb = t // HALO
    width = GDN_QK
    cols = (2, 3, 4)

    cur = lambda cb: pl.BlockSpec((1, TT, width), lambda i, j: (i, j, cb))
    left = lambda cb: pl.BlockSpec((1, HALO, width), lambda i, j: (i, jnp.maximum(j * rb - 1, 0), cb))
    right = lambda cb: pl.BlockSpec((1, HALO, width), lambda i, j: (i, jnp.minimum((j + 1) * rb, nrb - 1), cb))
    return pl.pallas_call(
        _cd_prep_kernel,
        grid=(bsz, nt),
        in_specs=[cur(cb) for cb in cols] + [left(cb) for cb in cols] + [right(cb) for cb in cols] + [
            pl.BlockSpec((3, 3 * width), lambda i, j: (0, 0))],
        out_specs=pl.BlockSpec((1, TT, 3 * width), lambda i, j: (i, j, 0)),
        out_shape=jax.ShapeDtypeStruct((bsz, t, 3 * width), BF16),
        compiler_params=pltpu.CompilerParams(dimension_semantics=("parallel", "parallel")),
        name="cd_prep",
    )(*([proj] * 9), conv_w)


GDN_HB = 8


def _mm2(a, b):
    return _dot(a.astype(BF16), b.astype(BF16))


def _unit_tri_inverse(mats, b16, b32, eye):
    each = lambda f, *ls: [f(*xs) for xs in zip(*ls)]
    d16 = each(lambda a: jnp.where(b16, a, 0.0), mats)
    d2 = each(lambda x: _mm2(x, x), d16)
    d4 = each(lambda x: _mm2(x, x), d2)
    d8 = each(lambda x: _mm2(x, x), d4)
    t = each(lambda x: eye - x, d16)
    for p in (d2, d4, d8):
        t = each(lambda x, y: x + _mm2(x, y), t, p)
    off32 = jnp.logical_and(b32, jnp.logical_not(b16))
    for sel in (off32, jnp.logical_not(b32)):
        a_off = each(lambda a: jnp.where(sel, a, 0.0), mats)
        inner = each(_mm2, a_off, t)
        t = each(lambda x, y: x - _mm2(x, y), t, inner)
    return t


def _gdn_kernel(q_ref, k_ref, v_ref, sm_ref, dtb_ref, nega_ref, o_ref,
                dec_s, n_s, p_s, oc_s, qp_s, st_s):
    d = pl.program_id(2)
    j = pl.program_id(3)
    is_fwd = d == 0

    @pl.when(j == 0)
    def _():
        st_s[...] = jnp.zeros_like(st_s)

    sm = sm_ref[0]
    m_cum, m_all = _chunk_masks(is_fwd, TT)
    first = d * GDN_HEADS + pl.program_id(1) * GDN_HB
    rot = jnp.where(first == 0, 0, LANE - first)
    la = pltpu.roll(_softplus(sm + dtb_ref[...]) * nega_ref[...], rot, 1)
    be_sm = pltpu.roll(jax.nn.sigmoid(sm), rot, 1)
    gc_sm = _sel_dot(m_cum, la)
    gl_sm = _sel_dot(m_all, la)
    lane_bcast = lambda a, col: jnp.broadcast_to(a[:, col:col + 1], (TT, GDN_DK))

    r = lax.broadcasted_iota(jnp.int32, (CHUNK, CHUNK), 0)
    c = lax.broadcasted_iota(jnp.int32, (CHUNK, CHUNK), 1)
    causal = _causal_mask(is_fwd, CHUNK)
    strict = jnp.logical_and(causal, r != c)
    b16 = lax.shift_right_logical(r, 4) == lax.shift_right_logical(c, 4)
    b32 = lax.shift_right_logical(r, 5) == lax.shift_right_logical(c, 5)
    eye = jnp.where(r == c, 1.0, 0.0)
    chunk_rows = [slice(ci * CHUNK, (ci + 1) * CHUNK) for ci in range(CPT)]

    amats, rhss, aqks, kdts, qgs = [], [], [], [], []
    for hh in range(GDN_HB):
        cols = slice(hh * GDN_DK, (hh + 1) * GDN_DK)
        gc = lane_bcast(gc_sm, hh)
        gl = lane_bcast(gl_sm, hh)
        beta_e = lane_bcast(be_sm, 2 * GDN_HEADS + hh)
        q = q_ref[0, :, cols].astype(F32)
        k = k_ref[0, :, cols].astype(F32)
        egc = jnp.exp(gc)
        kb = k * beta_e
        qg = q * egc
        kd = k * jnp.exp(gl - gc)
        dec_s[:, cols] = jnp.exp(gl)
        rhs = jnp.concatenate([v_ref[0, :, cols].astype(F32) * beta_e, kb * egc], axis=1)
        for rows in chunk_rows:
            gcc = gc[rows, :]
            dmat = jnp.exp(jnp.where(causal, gcc[:, 0:CHUNK] - gcc.T[0:CHUNK, :], -1e30))
            kc = k[rows].astype(BF16)
            amats.append(jnp.where(strict, _dot_nt(kb[rows].astype(BF16), kc) * dmat, 0.0))
            rhss.append(rhs[rows])
            aqks.append((_dot_nt(q[rows].astype(BF16), kc) * dmat).astype(BF16))
            kdts.append(kd[rows].T.astype(BF16))
            qgs.append(qg[rows])
    tinv = _unit_tri_inverse(amats, b16, b32, eye)
    sols = [_mm2(t, rhs_c).astype(BF16) for t, rhs_c in zip(tinv, rhss)]
    for n, sol in enumerate(sols):
        hh, ci = n // CPT, n % CPT
        rows, cols = chunk_rows[ci], slice(hh * GDN_DK, (hh + 1) * GDN_DK)
        ks = _dot(kdts[n], sol)
        qs = _dot(aqks[n], sol)
        n_s[n] = ks[:, 0:GDN_DV]
        p_s[n] = ks[:, GDN_DV:].astype(BF16)
        oc_s[rows, cols] = qs[:, 0:GDN_DV]
        qp_s[rows, cols] = (qgs[n] - qs[:, GDN_DV:]).astype(BF16)

    for ci in range(CPT):
        cidx = jnp.where(is_fwd, ci, CPT - 1 - ci)
        off = pl.multiple_of(cidx * CHUNK, CHUNK)
        rows = pl.ds(off, CHUNK)
        for hh in range(GDN_HB):
            cols = slice(hh * GDN_DK, (hh + 1) * GDN_DK)
            st = st_s[hh]
            stb = st.astype(BF16)
            o_ref[0, 0, rows, cols] = (_dot(qp_s[rows, cols], stb) + oc_s[rows, cols]).astype(BF16)
            st_s[hh] = st * dec_s[pl.ds(off, 1), cols] - _dot(p_s[hh * CPT + cidx], stb) + n_s[hh * CPT + cidx]


def gdn_scan(qkv, small, dtb, nega):
    bsz, t, _ = qkv.shape
    nt = t // TT
    tile = lambda d, j: _scan_tile(d, j, nt)
    ng = GDN_HEADS // GDN_HB
    wb = GDN_HB * GDN_DK
    return pl.pallas_call(
        _gdn_kernel,
        grid=(bsz, ng, 2, nt),
        in_specs=[
            pl.BlockSpec((1, TT, wb), lambda i, h, d, j: (i, tile(d, j), h)),
            pl.BlockSpec((1, TT, wb), lambda i, h, d, j: (i, tile(d, j), ng + h)),
            pl.BlockSpec((1, TT, wb), lambda i, h, d, j: (i, tile(d, j), 2 * ng + h)),
            pl.BlockSpec((1, TT, LANE), lambda i, h, d, j: (i, tile(d, j), 0)),
            pl.BlockSpec((1, LANE), lambda i, h, d, j: (0, 0)),
            pl.BlockSpec((1, LANE), lambda i, h, d, j: (0, 0)),
        ],
        out_specs=pl.BlockSpec((1, 1, TT, wb), lambda i, h, d, j: (d, i, tile(d, j), h)),
        out_shape=jax.ShapeDtypeStruct((2, bsz, t, GDN_V), BF16),
        scratch_shapes=[pltpu.VMEM((TT, wb), F32),
                        pltpu.VMEM((GDN_HB * CPT, GDN_DK, GDN_DV), F32),
                        pltpu.VMEM((GDN_HB * CPT, GDN_DK, GDN_DK), BF16),
                        pltpu.VMEM((TT, wb), F32), pltpu.VMEM((TT, wb), BF16),
                        pltpu.VMEM((GDN_HB, GDN_DK, GDN_DV), F32)],
        compiler_params=pltpu.CompilerParams(
            dimension_semantics=("parallel", "parallel", "parallel", "arbitrary")),
        name="gdn_scan",
    )(qkv, qkv, qkv, small, dtb, nega)


CONF_PAD = 16


def _cd_out_kernel(ga_ref, gb_ref, og_ref, of_ref, ob_ref, cw_ref, cb_ref, lg_ref, lb_ref, ng_ref,
                   w_ref, b_ref, x_ref, gate_ref, o_ref, pad_s, shift_s, conv_s):
    j = pl.program_id(1)
    half = (CONF_KERNEL - 1) // 2
    glu = ga_ref[0].astype(F32) * jax.nn.sigmoid(gb_ref[0].astype(F32))
    zeros = jnp.zeros((CONF_PAD, CONF_CH), F32)

    def conv_segments(seglen):
        stride = seglen + 2 * CONF_PAD
        for g in range(TT // seglen):
            base = g * stride
            pad_s[base:base + CONF_PAD, :] = zeros
            pad_s[base + CONF_PAD:base + CONF_PAD + seglen, :] = glu[g * seglen:(g + 1) * seglen]
            pad_s[base + CONF_PAD + seglen:base + stride, :] = zeros
        nseg = TT // seglen
        used = nseg * stride
        conv_s[...] = jnp.zeros((TT, CONF_CH), F32) + cb_ref[...]
        for phase in range(SUBLANES):
            taps = [kk for kk in range(CONF_KERNEL) if (CONF_PAD - half + kk) % SUBLANES == phase]
            if phase:
                shift_s[0:used - SUBLANES, :] = pad_s[phase:used - SUBLANES + phase, :]
            src = shift_s if phase else pad_s
            for g in range(nseg):
                acc = conv_s[g * seglen:(g + 1) * seglen, :]
                for kk in taps:
                    lo = g * stride + CONF_PAD - half + kk - phase
                    acc = acc + src[lo:lo + seglen, :] * cw_ref[kk:kk + 1, :]
                conv_s[g * seglen:(g + 1) * seglen, :] = acc

    @pl.when(j == 0)
    def _():
        conv_segments(TT)

    @pl.when(j > 0)
    def _():
        conv_segments(GRID_W)

    acc = conv_s[...]
    mu = jnp.mean(acc, axis=-1, keepdims=True)
    cen = acc - mu
    var = jnp.mean(cen * cen, axis=-1, keepdims=True)
    conv = _silu(cen * lax.rsqrt(var + 1e-5) * lg_ref[...] + lb_ref[...])
    o = (_group_rmsnorm(of_ref[0, 0].astype(F32) + ob_ref[0, 0].astype(F32), GDN_DV) * ng_ref[...]
         * _silu(og_ref[0].astype(F32)))
    m = _dot(conv.astype(BF16), w_ref[0:CONF_CH, :]) + _dot(o.astype(BF16), w_ref[CONF_CH:, :]) + b_ref[...]
    o_ref[0] = x_ref[0] + gate_ref[0, 0] * m


def cd_out(o_gdn, proj, conv_w, conv_b, ln_g, ln_b, norm_g, w, b, x, gate):
    bsz, t, d = x.shape
    seg = lambda i, j: (i, jnp.minimum(j, 1), 0, 0)
    row = lambda width: pl.BlockSpec((1, width), lambda i, j: (0, 0))
    return pl.pallas_call(
        _cd_out_kernel,
        grid=(bsz, t // TT),
        in_specs=[
            pl.BlockSpec((1, TT, CONF_CH), lambda i, j: (i, j, 0)),
            pl.BlockSpec((1, TT, CONF_CH), lambda i, j: (i, j, 1)),
            pl.BlockSpec((1, TT, GDN_V), lambda i, j: (i, j, 5)),
            pl.BlockSpec((1, 1, TT, GDN_V), lambda i, j: (0, i, j, 0)),
            pl.BlockSpec((1, 1, TT, GDN_V), lambda i, j: (1, i, j, 0)),
            pl.BlockSpec((CONF_KERNEL, CONF_CH), lambda i, j: (0, 0)),
            row(CONF_CH), row(CONF_CH), row(CONF_CH), row(GDN_V),
            pl.BlockSpec((CONF_CH + GDN_V, d), lambda i, j: (0, 0)),
            row(d),
            pl.BlockSpec((1, TT, d), lambda i, j: (i, j, 0)),
            pl.BlockSpec((1, 1, 1, d), seg),
        ],
        out_specs=pl.BlockSpec((1, TT, d), lambda i, j: (i, j, 0)),
        out_shape=jax.ShapeDtypeStruct((bsz, t, d), F32),
        scratch_shapes=[pltpu.VMEM(((TT // GRID_W) * (GRID_W + 2 * CONF_PAD), CONF_CH), F32),
                        pltpu.VMEM(((TT // GRID_W) * (GRID_W + 2 * CONF_PAD), CONF_CH), F32),
                        pltpu.VMEM((TT, CONF_CH), F32)],
        compiler_params=pltpu.CompilerParams(
            dimension_semantics=("parallel", "parallel"), vmem_limit_bytes=VMEM_LIMIT),
        name="cd_out",
    )(proj, proj, proj, o_gdn, o_gdn, conv_w, conv_b.reshape(1, -1), ln_g.reshape(1, -1), ln_b.reshape(1, -1),
      norm_g.reshape(1, -1), w, b.reshape(1, d), x, gate)


def _out_proj_kernel(m_ref, w_ref, b_ref, x_ref, gate_ref, o_ref):
    y = _dot(m_ref[0].astype(BF16), w_ref[...]) + b_ref[...]
    o_ref[0] = x_ref[0] + gate_ref[0] * y


def out_proj_residual(mixed, w, b, x, gate):
    bsz, t, k = mixed.shape
    d = w.shape[1]
    tm = min(t, 512)
    return pl.pallas_call(
        _out_proj_kernel,
        grid=(bsz, t // tm),
        in_specs=[
            pl.BlockSpec((1, tm, k), lambda i, j: (i, j, 0)),
            pl.BlockSpec((k, d), lambda i, j: (0, 0)),
            pl.BlockSpec((1, d), lambda i, j: (0, 0)),
            pl.BlockSpec((1, tm, d), lambda i, j: (i, j, 0)),
            pl.BlockSpec((1, 1, d), lambda i, j: (i, 0, 0)),
        ],
        out_specs=pl.BlockSpec((1, tm, d), lambda i, j: (i, j, 0)),
        out_shape=jax.ShapeDtypeStruct((bsz, t, d), F32),
        compiler_params=pltpu.CompilerParams(
            dimension_semantics=("parallel", "parallel"), vmem_limit_bytes=VMEM_LIMIT),
        name="out_proj",
    )(mixed, w, b.reshape(1, d), x, gate.reshape(bsz, 1, d))


def _router_kernel(x_ref, g_ref, sh_ref, sc_ref, rw_ref, h_ref, aff_ref):
    x = x_ref[0]
    ms = jnp.mean(x * x, axis=-1, keepdims=True)
    h = (x * lax.rsqrt(ms + 1e-6) * g_ref[...] * (1.0 + sc_ref[0, 0]) + sh_ref[0, 0]).astype(BF16)
    h_ref[0] = h
    logits = _dot(h, rw_ref[...])
    lane = lax.broadcasted_iota(jnp.int32, logits.shape, 1)
    logits = jnp.where(lane < N_EXPERTS, logits, -1e30)
    e = jnp.exp(logits - jnp.max(logits, axis=-1, keepdims=True))
    aff = e / jnp.sum(e, axis=-1, keepdims=True)
    aff_ref[0] = aff.T[0:N_EXPERTS, :]


def moe_router(xa, g, shift, scale, rw, seg, tile0, ntiles):
    bsz, _, d = xa.shape
    n = ntiles * TT
    return pl.pallas_call(
        _router_kernel,
        grid=(bsz, ntiles),
        in_specs=[
            pl.BlockSpec((1, TT, d), lambda i, j: (i, j + tile0, 0)),
            pl.BlockSpec((1, d), lambda i, j: (0, 0)),
            pl.BlockSpec((1, 1, 1, d), lambda i, j: (i, seg, 0, 0)),
            pl.BlockSpec((1, 1, 1, d), lambda i, j: (i, seg, 0, 0)),
            pl.BlockSpec((d, LANE), lambda i, j: (0, 0)),
        ],
        out_specs=[pl.BlockSpec((1, TT, d), lambda i, j: (i, j, 0)),
                   pl.BlockSpec((1, N_EXPERTS, TT), lambda i, j: (i, 0, j))],
        out_shape=[jax.ShapeDtypeStruct((bsz, n, d), BF16), jax.ShapeDtypeStruct((bsz, N_EXPERTS, n), F32)],
        compiler_params=pltpu.CompilerParams(dimension_semantics=("parallel", "parallel")),
        name="moe_router",
    )(xa, g.reshape(1, d), shift, scale, rw)


def _lane_block_prefix(x, u_strict):
    nblk = x.shape[1] // LANE
    run = jnp.zeros((x.shape[0], 1), F32)
    outs = []
    for cblk in range(nblk):
        xc = x[:, cblk * LANE:(cblk + 1) * LANE]
        outs.append(_dot(xc.astype(BF16), u_strict) + run)
        run = run + jnp.sum(xc, axis=-1, keepdims=True)
    return jnp.concatenate(outs, axis=1), run


def _select_kernel(aff_ref, slot_ref, *, cap):
    aff = aff_ref[0]
    bits = pltpu.bitcast(aff, jnp.int32)
    capf = jnp.float32(cap)

    def step(i, thr):
        cand = jnp.bitwise_or(thr, lax.shift_left(jnp.int32(1), 30 - i))
        cnt = jnp.sum(jnp.where(bits >= cand, 1.0, 0.0), axis=-1, keepdims=True)
        return jnp.where(cnt >= capf, cand, thr)

    thr = lax.fori_loop(0, 31, step, jnp.zeros((aff.shape[0], 1), jnp.int32))
    gt = jnp.where(bits > thr, 1.0, 0.0)
    eq = jnp.where(bits == thr, 1.0, 0.0)
    r = lax.broadcasted_iota(jnp.int32, (LANE, LANE), 0)
    c = lax.broadcasted_iota(jnp.int32, (LANE, LANE), 1)
    u_strict = jnp.where(r < c, 1.0, 0.0).astype(BF16)
    need = capf - jnp.sum(gt, axis=-1, keepdims=True)
    eq_rank, _ = _lane_block_prefix(eq, u_strict)
    sel = jnp.maximum(gt, jnp.where(eq_rank < need, eq, 0.0))
    slot, _ = _lane_block_prefix(sel, u_strict)
    slot_ref[0] = jnp.where(sel > 0.0, slot.astype(jnp.int32), -1)


def moe_select(aff, cap):
    bsz, ne, n = aff.shape
    return pl.pallas_call(
        functools.partial(_select_kernel, cap=cap),
        grid=(bsz,),
        in_specs=[pl.BlockSpec((1, ne, n), lambda i: (i, 0, 0))],
        out_specs=pl.BlockSpec((1, ne, n), lambda i: (i, 0, 0)),
        out_shape=jax.ShapeDtypeStruct((bsz, ne, n), jnp.int32),
        compiler_params=pltpu.CompilerParams(dimension_semantics=("parallel",)),
        name="moe_select",
    )(aff)


def _slot_index_kernel(slot_ref, idx_ref, *, cap):
    slot = slot_ref[0]
    n = slot.shape[1]
    srow = lax.broadcasted_iota(jnp.int32, (cap, LANE), 0)
    lane = lax.broadcasted_iota(jnp.int32, (cap, LANE), 1)
    acc = jnp.zeros((cap, LANE), jnp.int32)
    for cblk in range(n // LANE):
        s_c = slot[:, cblk * LANE:(cblk + 1) * LANE]
        acc = acc + jnp.where(srow == s_c, lane + (cblk * LANE + 1), 0)
    ones = jnp.ones((8, LANE), BF16)
    hi = _dot_nt(ones, lax.shift_right_logical(acc, 7).astype(F32).astype(BF16))
    lo = _dot_nt(ones, jnp.bitwise_and(acc, LANE - 1).astype(F32).astype(BF16))
    idx_ref[0] = (hi[0:1, :] * float(LANE) + lo[0:1, :]).astype(jnp.int32) - 1


SLOT_WIN = 2 * LANE


def _slot_index_win_kernel(base_ref, slot_ref, idx_ref, acc_s, *, cap):
    i = pl.program_id(0)
    n = slot_ref.shape[2]
    acc_s[...] = jnp.zeros(acc_s.shape, jnp.int32)
    srow = lax.broadcasted_iota(jnp.int32, (SLOT_WIN, LANE), 0)
    lane = lax.broadcasted_iota(jnp.int32, (SLOT_WIN, LANE), 1)
    for cblk in range(n // LANE):
        base = pl.multiple_of(base_ref[i, cblk], LANE)
        s_c = slot_ref[0, :, cblk * LANE:(cblk + 1) * LANE]
        rows = pl.ds(base, SLOT_WIN)
        acc_s[rows, :] = acc_s[rows, :] + jnp.where(srow + base == s_c, lane + (cblk * LANE + 1), 0)
    acc = acc_s[0:cap, :]
    ones = jnp.ones((8, LANE), BF16)
    hi = _dot_nt(ones, lax.shift_right_logical(acc, 7).astype(F32).astype(BF16))
    lo = _dot_nt(ones, jnp.bitwise_and(acc, LANE - 1).astype(F32).astype(BF16))
    idx_ref[0] = (hi[0:1, :] * float(LANE) + lo[0:1, :]).astype(jnp.int32) - 1


def moe_slot_index_windowed(slot, base, cap):
    bsz, ne, n = slot.shape
    idx = pl.pallas_call(
        functools.partial(_slot_index_win_kernel, cap=cap),
        grid_spec=pltpu.PrefetchScalarGridSpec(
            num_scalar_prefetch=1,
            grid=(bsz * ne,),
            in_specs=[pl.BlockSpec((1, 1, n), lambda i, base_ref: (i, 0, 0))],
            out_specs=pl.BlockSpec((1, 1, cap), lambda i, base_ref: (i, 0, 0)),
            scratch_shapes=[pltpu.VMEM((cap + LANE, LANE), jnp.int32)],
        ),
        out_shape=jax.ShapeDtypeStruct((bsz * ne, 1, cap), jnp.int32),
        compiler_params=pltpu.CompilerParams(dimension_semantics=("arbitrary",)),
        name="moe_slot_index",
    )(base, slot.reshape(bsz * ne, 1, n))
    return idx.reshape(bsz, ne, cap)


def moe_slot_index(slot, cap):
    bsz, ne, n = slot.shape
    idx = pl.pallas_call(
        functools.partial(_slot_index_kernel, cap=cap),
        grid=(bsz * ne,),
        in_specs=[pl.BlockSpec((1, 1, n), lambda i: (i, 0, 0))],
        out_specs=pl.BlockSpec((1, 1, cap), lambda i: (i, 0, 0)),
        out_shape=jax.ShapeDtypeStruct((bsz * ne, 1, cap), jnp.int32),
        compiler_params=pltpu.CompilerParams(dimension_semantics=("parallel",)),
        name="moe_slot_index",
    )(slot.reshape(bsz * ne, 1, n))
    return idx.reshape(bsz, ne, cap)


WIN_ALIGN = 16
WIN_FAST = 128


def _combine_kernel(ws_ref, slot_ref, aff_ref, *rest, win, final):
    ye_refs, (x_ref, gate_ref), o_ref = rest[:N_EXPERTS], rest[N_EXPERTS:N_EXPERTS + 2], rest[-1]
    b = pl.program_id(0)
    j = pl.program_id(1)
    srow = lax.broadcasted_iota(jnp.int32, (win, TT), 0)
    his, los = [], []
    for e in range(N_EXPERTS):
        sel = jnp.where(srow + ws_ref[b, e, j] == slot_ref[0, e:e + 1, :], aff_ref[0, e:e + 1, :], 0.0).T
        hi = sel.astype(BF16)
        his.append(hi)
        los.append((sel - hi.astype(F32)).astype(BF16))
    ye = jnp.concatenate([r[...] for r in ye_refs], axis=0)
    acc = _dot(jnp.concatenate(his, axis=1), ye) + _dot(jnp.concatenate(los, axis=1), ye)
    y = x_ref[0] + gate_ref[0, 0] * acc
    if final:
        g_ref = rest[N_EXPERTS + 2]
        y = y * lax.rsqrt(jnp.mean(y * y, axis=-1, keepdims=True) + 1e-6) * g_ref[...]
    o_ref[0] = y


def moe_combine(ws, slot, aff, ye, xa, gate, seg, tile0, win, final_g=None):
    bsz, ne, n = slot.shape
    nt = n // TT
    d = xa.shape[2]

    def ye_spec(e):
        return pl.BlockSpec((pl.Squeezed(), pl.Squeezed(), pl.Element(win), pl.Element(d)),
                            lambda i, j, ws_ref: (i, e, pl.multiple_of(ws_ref[i, e, j], WIN_ALIGN), 0))

    in_specs = ([pl.BlockSpec((1, ne, TT), lambda i, j, ws_ref: (i, 0, j)),
                 pl.BlockSpec((1, ne, TT), lambda i, j, ws_ref: (i, 0, j))]
                + [ye_spec(e) for e in range(ne)]
                + [pl.BlockSpec((1, TT, d), lambda i, j, ws_ref: (i, j + tile0, 0)),
                   pl.BlockSpec((1, 1, 1, d), lambda i, j, ws_ref: (i, seg, 0, 0))])
    args = (ws, slot, aff, *([ye] * ne), xa, gate)
    if final_g is None:
        out_idx, out_shape, aliases = (lambda i, j, ws_ref: (i, j + tile0, 0)), xa.shape, {3 + ne: 0}
    else:
        out_idx, out_shape, aliases = (lambda i, j, ws_ref: (i, j, 0)), (bsz, n, d), {}
        in_specs.append(pl.BlockSpec((1, d), lambda i, j, ws_ref: (0, 0)))
        args += (final_g.reshape(1, d),)
    return pl.pallas_call(
        functools.partial(_combine_kernel, win=win, final=final_g is not None),
        grid_spec=pltpu.PrefetchScalarGridSpec(
            num_scalar_prefetch=1,
            grid=(bsz, nt),
            in_specs=in_specs,
            out_specs=pl.BlockSpec((1, TT, d), out_idx),
        ),
        out_shape=jax.ShapeDtypeStruct(out_shape, F32),
        input_output_aliases=aliases,
        compiler_params=pltpu.CompilerParams(
            dimension_semantics=("parallel", "parallel"), vmem_limit_bytes=VMEM_LIMIT),
        name="moe_combine",
    )(*args)


def _expert_ffn_kernel(x_ref, w1_ref, w3_ref, w2_ref, o_ref, w1_s, w3_s, w2_s):
    @pl.when(jnp.logical_and(pl.program_id(1) == 0, pl.program_id(2) == 0))
    def _():
        w1_s[...] = w1_ref[0, 0].astype(BF16)
        w3_s[...] = w3_ref[0, 0].astype(BF16)
        w2_s[...] = w2_ref[0, 0].astype(BF16)

    x = x_ref[0, 0]
    a = _dot(x, w1_s[...])
    g = _dot(x, w3_s[...])
    o_ref[0, 0] = _dot((_silu(a) * g).astype(BF16), w2_s[...]).astype(BF16)


def expert_ffn(xe, w1, w3, w2, layer):
    bsz, ne, cap, d = xe.shape
    f = w1.shape[3]
    tm = min(cap, 512)
    return pl.pallas_call(
        _expert_ffn_kernel,
        grid=(ne, bsz, cap // tm),
        in_specs=[
            pl.BlockSpec((1, 1, tm, d), lambda e, i, j: (i, e, j, 0)),
            pl.BlockSpec((1, 1, d, f), lambda e, i, j: (layer, e, 0, 0)),
            pl.BlockSpec((1, 1, d, f), lambda e, i, j: (layer, e, 0, 0)),
            pl.BlockSpec((1, 1, f, d), lambda e, i, j: (layer, e, 0, 0)),
        ],
        out_specs=pl.BlockSpec((1, 1, tm, d), lambda e, i, j: (i, e, j, 0)),
        out_shape=jax.ShapeDtypeStruct((bsz, ne, cap, d), BF16),
        scratch_shapes=[pltpu.VMEM((d, f), BF16), pltpu.VMEM((d, f), BF16), pltpu.VMEM((f, d), BF16)],
        compiler_params=pltpu.CompilerParams(
            dimension_semantics=("arbitrary", "arbitrary", "arbitrary"), vmem_limit_bytes=VMEM_LIMIT),
        name="expert_ffn",
    )(xe, w1, w3, w2)


def _rmsnorm_kernel(x_ref, g_ref, o_ref):
    x = x_ref[0]
    ms = jnp.mean(x * x, axis=-1, keepdims=True)
    o_ref[0] = x * lax.rsqrt(ms + 1e-6) * g_ref[...]


def rmsnorm_rows(x, g, tile0):
    bsz, t, d = x.shape
    nt = t // TT - tile0
    return pl.pallas_call(
        _rmsnorm_kernel,
        grid=(bsz, nt),
        in_specs=[pl.BlockSpec((1, TT, d), lambda i, j: (i, j + tile0, 0)), pl.BlockSpec((1, d), lambda i, j: (0, 0))],
        out_specs=pl.BlockSpec((1, TT, d), lambda i, j: (i, j, 0)),
        out_shape=jax.ShapeDtypeStruct((bsz, nt * TT, d), F32),
        compiler_params=pltpu.CompilerParams(dimension_semantics=("parallel", "parallel")),
        name="final_rmsnorm",
    )(x, g.reshape(1, d))


def _ab_in_layout(w_in, b_in):
    q, k, v, r, glr, z, xs, bm, cm, dt = _split_cols(
        jnp.concatenate([w_in, b_in[None]], axis=0),
        (GLA_QK, GLA_QK, GLA_V, GLA_V, 2 * GLA_GATE_RANK, SSD_INNER, SSD_INNER, SSD_BC, SSD_BC, 2 * SSD_HEADS))
    wide = jnp.concatenate([q, k, v, r, z, xs, bm, cm], axis=1)
    narrow = _pad_cols(jnp.concatenate([glr, dt], axis=1), LANE)
    return wide[:-1].astype(BF16), wide[-1], narrow[:-1].astype(BF16), narrow[-1]


def _gla_gate_params(w_gate2, b_gate2):
    wg = jnp.zeros((2, LANE, GLA_QK), F32)
    for d in range(2):
        wg = wg.at[d, d * GLA_GATE_RANK:(d + 1) * GLA_GATE_RANK, :].set(w_gate2[d])
    return wg.astype(BF16), b_gate2.reshape(2, 1, GLA_QK)


def _ssd_params(dt_bias, a_log):
    dtb = jnp.zeros((2, 1, LANE), F32)
    nega = jnp.zeros((2, 1, LANE), F32)
    e = np.zeros((2, LANE, SSD_INNER), np.float32)
    for d in range(2):
        c0 = SM_DT + d * SSD_HEADS
        dtb = dtb.at[d, 0, c0:c0 + SSD_HEADS].set(dt_bias[d])
        nega = nega.at[d, 0, c0:c0 + SSD_HEADS].set(-jnp.exp(a_log[d]))
        for h in range(SSD_HEADS):
            e[d, c0 + h, h * SSD_HEADDIM:(h + 1) * SSD_HEADDIM] = 1.0
    return dtb, nega, jnp.asarray(e, BF16)


def _gdn_params(dt_bias, a_log):
    n = 2 * GDN_HEADS
    dtb = jnp.zeros((1, LANE), F32).at[0, 0:n].set(dt_bias.reshape(n))
    nega = jnp.zeros((1, LANE), F32).at[0, 0:n].set(-jnp.exp(a_log.reshape(n)))
    return dtb, nega


def _rmsnorm(x, g, eps=1e-6):
    return x * lax.rsqrt(jnp.mean(jnp.square(x), axis=-1, keepdims=True) + eps) * g


def _head_rmsnorm(x, g):
    return _rmsnorm(x, g.reshape(x.shape[-2:]))


def _layernorm(x, g, b, eps=1e-5):
    mu = jnp.mean(x, axis=-1, keepdims=True)
    var = jnp.mean(jnp.square(x - mu), axis=-1, keepdims=True)
    return (x - mu) * lax.rsqrt(var + eps) * g + b


def _l2norm(x, eps=1e-6):
    return x * lax.rsqrt(jnp.sum(jnp.square(x), axis=-1, keepdims=True) + eps)


def _dwconv(x, w, b=None):
    k, ch = w.shape
    pad = (k - 1) // 2
    y = lax.conv_general_dilated(x, w[:, None, :], (1,), [(pad, pad)],
                                 dimension_numbers=('NWC', 'WIO', 'NWC'), feature_group_count=ch)
    return y if b is None else y + b


def _gdn_scan(q, k, v, beta, logg, s0, with_out):
    bsz, nh, t, dk = q.shape
    dv = v.shape[-1]
    nc = t // CHUNK
    chunks = lambda z: z.reshape(bsz, nh, nc, CHUNK, *z.shape[3:])
    q, k, v, beta, logg = (chunks(z) for z in (q, k, v, beta, logg))
    gc = jnp.cumsum(logg, axis=-1)
    glast = gc[..., -1]
    tril = jnp.tril(jnp.ones((CHUNK, CHUNK), dtype=bool))
    strict = jnp.tril(jnp.ones((CHUNK, CHUNK), dtype=bool), k=-1)
    decay = jnp.exp(jnp.where(tril, gc[..., :, None] - gc[..., None, :], -jnp.inf))
    kb = k * beta[..., None]
    m = jnp.eye(CHUNK, dtype=k.dtype) + jnp.where(strict, jnp.einsum('bhcid,bhcjd->bhcij', kb, k) * decay, 0.0)
    rhs = jnp.concatenate([v * beta[..., None], kb * jnp.exp(gc)[..., None]], axis=-1)
    sol = lax.linalg.triangular_solve(m, rhs, left_side=True, lower=True, unit_diagonal=True)
    u, w = sol[..., :dv], sol[..., dv:]
    kd = k * jnp.exp(glast[..., None] - gc)[..., None]
    dec = jnp.exp(glast)
    front = lambda z: jnp.moveaxis(z, 2, 0)

    def advance(s, w_c, u_c, kd_c, dec_c):
        vn = u_c - jnp.einsum('bhid,bhde->bhie', w_c, s)
        return vn, dec_c[..., None, None] * s + jnp.einsum('bhjd,bhje->bhde', kd_c, vn)

    xs = (front(w), front(u), front(kd), front(dec))
    if not with_out:
        s_fin, _ = lax.scan(lambda s, xc: (advance(s, *xc)[1], None), s0, xs)
        return None, s_fin
    aqk = jnp.einsum('bhcid,bhcjd->bhcij', q, k) * decay
    qg = q * jnp.exp(gc)[..., None]

    def step(s, xc):
        w_c, u_c, kd_c, dec_c, aqk_c, qg_c = xc
        vn, s_new = advance(s, w_c, u_c, kd_c, dec_c)
        o = jnp.einsum('bhid,bhde->bhie', qg_c, s) + jnp.einsum('bhij,bhje->bhie', aqk_c, vn)
        return s_new, o

    s_fin, o = lax.scan(step, s0, xs + (front(aqk), front(qg)))
    return jnp.moveaxis(o, 0, 2).reshape(bsz, nh, t, dv), s_fin


def _bidir_scan(scan_fn, args_f, args_b, t_axis, init_f, init_b, with_out):
    flip = lambda a: jnp.flip(a, axis=t_axis)
    o_f, s_f = scan_fn(*args_f, init_f, with_out)
    o_b, s_b = scan_fn(*[flip(a) for a in args_b], init_b, with_out)
    o = o_f + flip(o_b) if with_out else None
    return o, s_f, s_b


def _conformer_conv(u, rows, dw_w, dw_b, ln_g, ln_b):
    bsz, t, ch = u.shape
    y = _dwconv(u.reshape(bsz * rows, t // rows, ch), dw_w, dw_b).reshape(bsz, t, ch)
    return jax.nn.silu(_layernorm(y, ln_g, ln_b))


def _cd_stream(proj, gdn_conv_w, gdn_a_log, gdn_dt_bias):
    bsz, t, _ = proj.shape
    ga, gb, q, k, v, og, a_raw, b_raw = _split_cols(proj, CD_SPLITS)
    glu = ga * jax.nn.sigmoid(gb)
    qkv = jax.nn.silu(_dwconv(jnp.concatenate([q, k, v], axis=-1), gdn_conv_w))
    q, k, v = _split_cols(qkv, (GDN_QK, GDN_QK, GDN_V))
    heads = lambda a: a.reshape(bsz, t, GDN_HEADS, -1).transpose(0, 2, 1, 3)
    q = _l2norm(heads(q)) * GDN_DK ** -0.5
    k = _l2norm(heads(k))
    v = heads(v)
    beta = jax.nn.sigmoid(b_raw.reshape(bsz, t, 2, GDN_HEADS)).transpose(2, 0, 3, 1)
    logg = (-jnp.exp(gdn_a_log)
            * jax.nn.softplus(a_raw.reshape(bsz, t, 2, GDN_HEADS) + gdn_dt_bias)).transpose(2, 0, 3, 1)
    return (q, k, v, beta[0], logg[0]), (q, k, v, beta[1], logg[1]), glu, og


def _mixer_cd(proj_l, proj_c, rows, conf_dw_w, conf_dw_b, conf_ln_g, conf_ln_b,
              gdn_conv_w, gdn_a_log, gdn_dt_bias, gdn_norm_g, need_ctx):
    c_f, c_b, c_glu, c_og = _cd_stream(proj_c, gdn_conv_w, gdn_a_log, gdn_dt_bias)
    l_f, l_b, l_glu, l_og = _cd_stream(proj_l, gdn_conv_w, gdn_a_log, gdn_dt_bias)
    bsz = proj_l.shape[0]
    z0 = jnp.zeros((bsz, GDN_HEADS, GDN_DK, GDN_DV), F32)
    od_c, s_f, s_b = _bidir_scan(_gdn_scan, c_f, c_b, 2, z0, z0, need_ctx)
    od_l, _, _ = _bidir_scan(_gdn_scan, l_f, l_b, 2, s_f, s_b, True)

    def mix(od, glu, og, n_rows):
        t = glu.shape[1]
        conv = _conformer_conv(glu, n_rows, conf_dw_w, conf_dw_b, conf_ln_g, conf_ln_b)
        o = _head_rmsnorm(jnp.swapaxes(od, 1, 2), gdn_norm_g) * jax.nn.silu(og).reshape(bsz, t, GDN_HEADS, GDN_DV)
        return jnp.concatenate([conv, o.reshape(bsz, t, GDN_V)], axis=-1)

    return mix(od_l, l_glu, l_og, rows), (mix(od_c, c_glu, c_og, 1) if need_ctx else None)


def moe_segment(i, xa, mods, p, w1, w3, w2, seg, tile0, ntiles, final_g=None):
    bsz, _, d = xa.shape
    n = ntiles * TT
    cap = n * EC_CAPACITY // N_EXPERTS
    rw = _pad_cols(p["moe_router"][i], LANE).astype(BF16)
    h, aff = moe_router(xa, p["norm2_g"][i], mods[:, :, 3], mods[:, :, 4], rw, seg, tile0, ntiles)
    slot = moe_select(aff, cap)
    cnt128 = jnp.sum((slot >= 0).reshape(bsz, N_EXPERTS, n // LANE, LANE), axis=-1, dtype=jnp.int32)
    start128 = jnp.cumsum(cnt128, axis=-1) - cnt128
    if cap >= SLOT_WIN:
        base = jnp.minimum(start128 // LANE * LANE, cap - LANE).reshape(bsz * N_EXPERTS, n // LANE)
        idx = moe_slot_index_windowed(slot, base, cap)
    else:
        idx = moe_slot_index(slot, cap)
    idx = idx.reshape(bsz, N_EXPERTS * cap)
    xe = jnp.take_along_axis(h, idx[..., None], axis=1, mode="promise_in_bounds").reshape(bsz, N_EXPERTS, cap, d)
    ye = expert_ffn(xe, w1, w3, w2, i)
    per_tile = TT // LANE
    counts = jnp.sum(cnt128.reshape(bsz, N_EXPERTS, ntiles, per_tile), axis=-1)
    starts = start128[:, :, ::per_tile]
    aligned = starts // WIN_ALIGN * WIN_ALIGN

    def run(win):
        ws = jnp.minimum(aligned, cap - win)
        return moe_combine(ws, slot, aff, ye, xa, mods[:, :, 5], seg, tile0, win, final_g)

    win_fast, win_full = min(cap, WIN_FAST), min(cap, TT + WIN_ALIGN)
    if win_fast == win_full:
        return run(win_full)
    overflow = jnp.any(starts + counts - jnp.minimum(aligned, cap - win_fast) > win_fast)
    return lax.cond(overflow, lambda: run(win_full), lambda: run(win_fast))


def layer_mixer(i, j, xa, mods, p, last):
    sh1, sc1, g1 = (mods[:, :, s] for s in range(3))
    if i % 2 == 0:
        proj, small = norm_proj(xa, p["norm1_g"][i], sh1, sc1, *_ab_in_layout(p["ab_w_in"][j], p["ab_b_in"][j]),
                                AB_WIDE // 2)
        xbc = ab_prep(proj, p["ssd_conv_w"][j], p["ssd_conv_b"][j])
        wg, bg = _gla_gate_params(p["gla_w_gate2"][j], p["gla_b_gate2"][j])
        o_gla = gla_scan(proj, small, wg, bg)
        y_ssd = ssd_scan(xbc, small, *_ssd_params(p["ssd_dt_bias"][j], p["ssd_a_log"][j]))
        return ab_out(o_gla, y_ssd, proj, xbc, p["gla_norm_g"][j], jnp.repeat(p["ssd_d"][j], SSD_HEADDIM),
                      p["ssd_norm_g"][j], p["ab_w_out"][j].astype(BF16), p["ab_b_out"][j], xa, g1)
    w_in, b_in = p["cd_w_in"][j], p["cd_b_in"][j]
    proj, small = norm_proj(xa, p["norm1_g"][i], sh1, sc1, w_in[:, :CD_WIDE].astype(BF16), b_in[:CD_WIDE],
                            _pad_cols(w_in[:, CD_WIDE:], LANE).astype(BF16), _pad_cols(b_in[CD_WIDE:], LANE),
                            CD_WIDE // 3)
    qkv = cd_prep(proj, p["gdn_conv_w"][j])
    o_gdn = gdn_scan(qkv, small, *_gdn_params(p["gdn_dt_bias"][j], p["gdn_a_log"][j]))
    return cd_out(o_gdn, proj, p["conf_dw_w"][j], p["conf_dw_b"][j], p["conf_ln_g"][j], p["conf_ln_b"][j],
                  p["gdn_norm_g"][j], p["cd_w_out"][j].astype(BF16), p["cd_b_out"][j], xa, g1)


def kernel(x, c, ctx, c_ctx, mod_w, mod_b, norm1_g, norm2_g, ab_w_in, ab_b_in, ab_w_out, ab_b_out, gla_w_gate2, gla_b_gate2, gla_norm_g, ssd_conv_w, ssd_conv_b, ssd_dt_bias, ssd_a_log, ssd_d, ssd_norm_g, cd_w_in, cd_b_in, cd_w_out, cd_b_out, conf_dw_w, conf_dw_b, conf_ln_g, conf_ln_b, gdn_conv_w, gdn_a_log, gdn_dt_bias, gdn_norm_g, moe_router, moe_w1, moe_w3, moe_w2, final_norm_g):
    p = dict(locals())
    bsz, seq, d = x.shape
    assert ctx.shape[1] == TT and seq % TT == 0
    depth = mod_w.shape[0]
    xa = jnp.concatenate([ctx, x], axis=1)
    w1, w3, w2 = moe_w1, moe_w3, moe_w2
    cond = jnp.concatenate([c, c_ctx[None]], axis=0)
    for i in range(depth):
        last = i == depth - 1
        mod = mod_proj(cond, mod_w[i].astype(BF16), mod_b[i])
        mods = jnp.stack([jnp.broadcast_to(mod[bsz], (bsz, 6 * d)), mod[:bsz]], axis=1).reshape(bsz, 2, 6, 1, d)
        xa = layer_mixer(i, i // 2, xa, mods, p, last)
        if last:
            return moe_segment(i, xa, mods, p, w1, w3, w2, 1, 1, seq // TT, final_norm_g)
        xa = moe_segment(i, xa, mods, p, w1, w3, w2, 1, 1, seq // TT)
        xa = moe_segment(i, xa, mods, p, w1, w3, w2, 0, 0, 1)
```

```python
import functools

import jax
import jax.numpy as jnp
import numpy as np
from jax import lax
from jax.experimental import pallas as pl
from jax.experimental.pallas import tpu as pltpu

F32 = jnp.float32
BF16 = jnp.bfloat16

D_MODEL = 1024
GRID_W = 64
CHUNK = 64
GLA_HEADS, GLA_DK, GLA_DV, GLA_GATE_RANK, GLA_GATE_TAU = 4, 128, 256, 16, 16.0
SSD_HEADS, SSD_HEADDIM, SSD_STATE, SSD_GROUPS = 16, 64, 128, 2
CONF_CH, CONF_KERNEL = D_MODEL, 31
GDN_HEADS, GDN_DK, GDN_DV = 8, 128, 128
N_EXPERTS, EC_CAPACITY = 16, 2

GLA_QK = GLA_HEADS * GLA_DK
GLA_V = GLA_HEADS * GLA_DV
SSD_INNER = SSD_HEADS * SSD_HEADDIM
SSD_BC = SSD_GROUPS * SSD_STATE
SSD_HPG = SSD_HEADS // SSD_GROUPS
GDN_QK = GDN_HEADS * GDN_DK
GDN_V = GDN_HEADS * GDN_DV

LANE = 128
SUBLANES = 8
TT = 256
CPT = TT // CHUNK
VMEM_LIMIT = 48 * 1024 * 1024

AB_Q, AB_K, AB_V, AB_R, AB_Z, AB_XS, AB_BM, AB_CM, AB_WIDE = 0, 512, 1024, 2048, 3072, 4096, 5120, 5376, 5632
SM_DT = 2 * GLA_GATE_RANK
CD_WIDE = 6 * D_MODEL


def _split_cols(a, sizes):
    return jnp.split(a, np.cumsum(sizes)[:-1].tolist(), axis=-1)


def _pad_cols(a, n):
    return jnp.pad(a, [(0, 0)] * (a.ndim - 1) + [(0, n - a.shape[-1])])


def _dot(a, b):
    return jnp.dot(a, b, preferred_element_type=F32)


def _dot_nt(a, b):
    return lax.dot_general(a, b, (((1,), (1,)), ((), ())), preferred_element_type=F32)


def _split3(x):
    hi = x.astype(BF16)
    r = x - hi.astype(F32)
    mid = r.astype(BF16)
    lo = (r - mid.astype(F32)).astype(BF16)
    return hi, mid, lo


def _sel_dot(m, x):
    hi, mid, lo = _split3(x)
    return _dot(m, hi) + _dot(m, mid) + _dot(m, lo)


def _softplus(x):
    return jnp.maximum(x, 0.0) + jnp.log(1.0 + jnp.exp(-jnp.abs(x)))


def _silu(x):
    return x * jax.nn.sigmoid(x)


def _chunk_masks(is_fwd, n):
    r = lax.broadcasted_iota(jnp.int32, (n, n), 0)
    c = lax.broadcasted_iota(jnp.int32, (n, n), 1)
    same = lax.shift_right_logical(r, 6) == lax.shift_right_logical(c, 6)
    lo = jnp.where(is_fwd, c, r)
    hi = jnp.where(is_fwd, r, c)
    cum = jnp.logical_and(same, lo <= hi)
    return jnp.where(cum, 1.0, 0.0).astype(BF16), jnp.where(same, 1.0, 0.0).astype(BF16)


def _causal_mask(is_fwd, n):
    r = lax.broadcasted_iota(jnp.int32, (n, n), 0)
    c = lax.broadcasted_iota(jnp.int32, (n, n), 1)
    return jnp.where(is_fwd, c, r) <= jnp.where(is_fwd, r, c)


def _scan_tile(d, j, nt):
    return jnp.where(d == 0, j, jnp.where(j == 0, 0, nt - j))


def _mod_proj_kernel(c_ref, w_ref, b_ref, o_ref):
    o_ref[...] = _dot(_silu(c_ref[...]).astype(BF16), w_ref[...]) + b_ref[...]


def mod_proj(cond, w, b):
    r, d = cond.shape
    n = w.shape[1]
    tn = 6 * LANE * 2
    return pl.pallas_call(
        _mod_proj_kernel,
        grid=(n // tn,),
        in_specs=[pl.BlockSpec((r, d), lambda k: (0, 0)),
                  pl.BlockSpec((d, tn), lambda k: (0, k)),
                  pl.BlockSpec((1, tn), lambda k: (0, k))],
        out_specs=pl.BlockSpec((r, tn), lambda k: (0, k)),
        out_shape=jax.ShapeDtypeStruct((r, n), F32),
        compiler_params=pltpu.CompilerParams(dimension_semantics=("parallel",)),
        name="mod_proj",
    )(cond, w, b.reshape(1, n))


NP_TM = 3 * TT


def _norm_proj_kernel(x_ref, g_ref, sh_ref, sc_ref, w_ref, b_ref, ws_ref, bs_ref, o_ref, os_ref):
    x = x_ref[0]
    row = lax.broadcasted_iota(jnp.int32, (NP_TM, 1), 0) + pl.program_id(2) * NP_TM
    is_ctx = row < TT
    scale = jnp.where(is_ctx, sc_ref[0, 0], sc_ref[0, 1])
    shift = jnp.where(is_ctx, sh_ref[0, 0], sh_ref[0, 1])
    ms = jnp.mean(x * x, axis=-1, keepdims=True)
    h = (x * lax.rsqrt(ms + 1e-6) * g_ref[...] * (1.0 + scale) + shift).astype(BF16)
    o_ref[0] = (_dot(h, w_ref[...]) + b_ref[...]).astype(BF16)
    os_ref[0, 0] = _dot(h, ws_ref[...]) + bs_ref[...]


def norm_proj(x, g, shift, scale, w, b, w_small, b_small, tn):
    bsz, t, d = x.shape
    n = w.shape[1]
    assert t % NP_TM == 0 and n % tn == 0
    wide, narrow = pl.pallas_call(
        _norm_proj_kernel,
        grid=(n // tn, bsz, t // NP_TM),
        in_specs=[
            pl.BlockSpec((1, NP_TM, d), lambda k, i, j: (i, j, 0)),
            pl.BlockSpec((1, d), lambda k, i, j: (0, 0)),
            pl.BlockSpec((1, 2, 1, d), lambda k, i, j: (i, 0, 0, 0)),
            pl.BlockSpec((1, 2, 1, d), lambda k, i, j: (i, 0, 0, 0)),
            pl.BlockSpec((d, tn), lambda k, i, j: (0, k)),
            pl.BlockSpec((1, tn), lambda k, i, j: (0, k)),
            pl.BlockSpec((d, LANE), lambda k, i, j: (0, 0)),
            pl.BlockSpec((1, LANE), lambda k, i, j: (0, 0)),
        ],
        out_specs=[pl.BlockSpec((1, NP_TM, tn), lambda k, i, j: (i, j, k)),
                   pl.BlockSpec((1, 1, NP_TM, LANE), lambda k, i, j: (k, i, j, 0))],
        out_shape=[jax.ShapeDtypeStruct((bsz, t, n), BF16),
                   jax.ShapeDtypeStruct((n // tn, bsz, t, LANE), F32)],
        compiler_params=pltpu.CompilerParams(
            dimension_semantics=("parallel", "parallel", "parallel"), vmem_limit_bytes=VMEM_LIMIT),
        name="norm_proj",
    )(x, g.reshape(1, d), shift, scale, w, b.reshape(1, n), w_small, b_small.reshape(1, LANE))
    return wide, narrow[0]


HALO = 16


def _conv3_piece(c_ref, l_ref, r_ref, w, b, left_ok, right_ok):
    x = c_ref[0].astype(F32)
    n = x.shape[0]
    row = lax.broadcasted_iota(jnp.int32, x.shape, 0)
    prev_row = jnp.where(left_ok, l_ref[0, HALO - 1:HALO, :].astype(F32), 0.0)
    next_row = jnp.where(right_ok, r_ref[0, 0:1, :].astype(F32), 0.0)
    x_prev = jnp.where(row == 0, prev_row, pltpu.roll(x, 1, 0))
    x_next = jnp.where(row == n - 1, next_row, pltpu.roll(x, n - 1, 0))
    return _silu(w[0:1, :] * x_prev + w[1:2, :] * x + w[2:3, :] * x_next + b)


def _ab_prep_kernel(xs_ref, bm_ref, cm_ref, xsl_ref, bml_ref, cml_ref, xsr_ref, bmr_ref, cmr_ref,
                    w_ref, b_ref, o_ref):
    j = pl.program_id(1)
    nt = pl.num_programs(1)
    left_ok = j >= 2
    right_ok = jnp.logical_and(j >= 1, j < nt - 1)
    w = w_ref[...]
    b = b_ref[...]
    o_ref[0, :, 0:SSD_INNER] = _conv3_piece(xs_ref, xsl_ref, xsr_ref, w[:, 0:SSD_INNER],
                                            b[:, 0:SSD_INNER], left_ok, right_ok).astype(BF16)
    c0, c1 = SSD_INNER, SSD_INNER + SSD_BC
    o_ref[0, :, c0:c1] = _conv3_piece(bm_ref, bml_ref, bmr_ref, w[:, c0:c1], b[:, c0:c1],
                                      left_ok, right_ok).astype(BF16)
    c0, c1 = c1, c1 + SSD_BC
    o_ref[0, :, c0:c1] = _conv3_piece(cm_ref, cml_ref, cmr_ref, w[:, c0:c1], b[:, c0:c1],
                                      left_ok, right_ok).astype(BF16)


def ab_prep(proj, conv_w, conv_b):
    bsz, t, _ = proj.shape
    nt = t // TT
    rb = TT // HALO
    nrb = t // HALO
    cw = SSD_INNER + 2 * SSD_BC

    def cur(width, col):
        return pl.BlockSpec((1, TT, width), lambda i, j: (i, j, col // width))

    def left(width, col):
        return pl.BlockSpec((1, HALO, width), lambda i, j: (i, jnp.maximum(j * rb - 1, 0), col // width))

    def right(width, col):
        return pl.BlockSpec((1, HALO, width), lambda i, j: (i, jnp.minimum((j + 1) * rb, nrb - 1), col // width))

    pieces = ((SSD_INNER, AB_XS), (SSD_BC, AB_BM), (SSD_BC, AB_CM))
    return pl.pallas_call(
        _ab_prep_kernel,
        grid=(bsz, nt),
        in_specs=[cur(*p) for p in pieces] + [left(*p) for p in pieces] + [right(*p) for p in pieces] + [
            pl.BlockSpec((3, cw), lambda i, j: (0, 0)),
            pl.BlockSpec((1, cw), lambda i, j: (0, 0)),
        ],
        out_specs=pl.BlockSpec((1, TT, cw), lambda i, j: (i, j, 0)),
        out_shape=jax.ShapeDtypeStruct((bsz, t, cw), BF16),
        compiler_params=pltpu.CompilerParams(dimension_semantics=("parallel", "parallel")),
        name="ab_prep",
    )(*([proj] * 9), conv_w, conv_b.reshape(1, cw))


def _gla_kernel(q_ref, k_ref, v_ref, sm_ref, wg_ref, bg_ref, o_ref, qg_s, egl_s, oi_s, u_s, st_s):
    d = pl.program_id(1)
    j = pl.program_id(2)
    is_fwd = d == 0

    @pl.when(j == 0)
    def _():
        st_s[...] = jnp.zeros_like(st_s)

    gz = _dot(sm_ref[0].astype(BF16), wg_ref[0]) + bg_ref[0]
    logg = (jnp.minimum(gz, 0.0) - jnp.log(1.0 + jnp.exp(-jnp.abs(gz)))) * (1.0 / GLA_GATE_TAU)
    m_cum, m_all = _chunk_masks(is_fwd, TT)
    gc = _sel_dot(m_cum, logg)
    totals = [jnp.where(is_fwd, gc[(ci + 1) * CHUNK - 1:(ci + 1) * CHUNK, :], gc[ci * CHUNK:ci * CHUNK + 1, :])
              for ci in range(CPT)]
    gl = jnp.concatenate([jnp.broadcast_to(row, (CHUNK, GLA_QK)) for row in totals], axis=0)
    q = q_ref[0].astype(F32) * (GLA_DK ** -0.5)
    k = k_ref[0].astype(F32)
    qg = (q * jnp.exp(gc)).astype(BF16)
    kn = (k * jnp.exp(-gc)).astype(BF16)
    kd = (k * jnp.exp(gl - gc)).astype(BF16)
    qg_s[...] = qg
    egl_s[...] = jnp.exp(gl)
    causal = _causal_mask(is_fwd, CHUNK)

    pairs = [(h, ci) for h in range(GLA_HEADS) for ci in range(CPT)]
    rows_of = lambda ci: slice(ci * CHUNK, (ci + 1) * CHUNK)
    kcols = lambda h: slice(h * GLA_DK, (h + 1) * GLA_DK)
    vcols = lambda h: slice(h * GLA_DV, (h + 1) * GLA_DV)
    vs = [v_ref[0, rows_of(ci), vcols(h)] for h, ci in pairs]
    atts = [jnp.where(causal, _dot_nt(qg[rows_of(ci), kcols(h)], kn[rows_of(ci), kcols(h)]), 0.0).astype(BF16)
            for h, ci in pairs]
    for n, (h, ci) in enumerate(pairs):
        oi_s[rows_of(ci), vcols(h)] = _dot(atts[n], vs[n])
        u_s[n] = _dot(vs[n].astype(F32).T.astype(BF16), kd[rows_of(ci), kcols(h)])

    for ci in range(CPT):
        cidx = jnp.where(is_fwd, ci, CPT - 1 - ci)
        off = pl.multiple_of(cidx * CHUNK, CHUNK)
        rows = pl.ds(off, CHUNK)
        for h in range(GLA_HEADS):
            st = st_s[h]
            o_ref[0, 0, rows, vcols(h)] = (oi_s[rows, vcols(h)]
                                           + _dot_nt(qg_s[rows, kcols(h)], st.astype(BF16))).astype(BF16)
            st_s[h] = st * egl_s[pl.ds(off, 1), kcols(h)] + u_s[h * CPT + cidx]


def gla_scan(proj, small, wg, bg):
    bsz, t, _ = proj.shape
    nt = t // TT
    tile = lambda d, j: _scan_tile(d, j, nt)
    return pl.pallas_call(
        _gla_kernel,
        grid=(bsz, 2, nt),
        in_specs=[
            pl.BlockSpec((1, TT, GLA_QK), lambda i, d, j: (i, tile(d, j), AB_Q // GLA_QK)),
            pl.BlockSpec((1, TT, GLA_QK), lambda i, d, j: (i, tile(d, j), AB_K // GLA_QK)),
            pl.BlockSpec((1, TT, GLA_V), lambda i, d, j: (i, tile(d, j), AB_V // GLA_V)),
            pl.BlockSpec((1, TT, LANE), lambda i, d, j: (i, tile(d, j), 0)),
            pl.BlockSpec((1, LANE, GLA_QK), lambda i, d, j: (d, 0, 0)),
            pl.BlockSpec((1, 1, GLA_QK), lambda i, d, j: (d, 0, 0)),
        ],
        out_specs=pl.BlockSpec((1, 1, TT, GLA_V), lambda i, d, j: (d, i, tile(d, j), 0)),
        out_shape=jax.ShapeDtypeStruct((2, bsz, t, GLA_V), BF16),
        scratch_shapes=[pltpu.VMEM((TT, GLA_QK), BF16), pltpu.VMEM((TT, GLA_QK), F32),
                        pltpu.VMEM((TT, GLA_V), F32), pltpu.VMEM((GLA_HEADS * CPT, GLA_DV, GLA_DK), F32),
                        pltpu.VMEM((GLA_HEADS, GLA_DV, GLA_DK), F32)],
        compiler_params=pltpu.CompilerParams(
            dimension_semantics=("parallel", "parallel", "arbitrary")),
        name="gla_scan",
    )(proj, proj, proj, small, wg, bg)


def _dot_sel2(x, e):
    hi = x.astype(BF16)
    return _dot(hi, e) + _dot((x - hi.astype(F32)).astype(BF16), e)


def _ssd_kernel(xs_ref, bm_ref, cm_ref, sm_ref, dtb_ref, nega_ref, e_ref, o_ref,
                v_s, vw_s, cdec_s, dec_s, ah_s, st_s):
    d = pl.program_id(1)
    j = pl.program_id(2)
    is_fwd = d == 0
    gw = SSD_HPG * SSD_HEADDIM

    @pl.when(j == 0)
    def _():
        st_s[...] = jnp.zeros_like(st_s)

    dt = _softplus(sm_ref[0] + dtb_ref[0])
    la = dt * nega_ref[0]
    m_cum, m_all = _chunk_masks(is_fwd, TT)
    acum = _sel_dot(m_cum, la)
    atot = _sel_dot(m_all, la)
    e = e_ref[0]
    v = xs_ref[0].astype(F32) * _dot_sel2(dt, e)
    v_s[...] = v.astype(BF16)
    vw_s[...] = (v * _dot_sel2(jnp.exp(atot - acum), e)).astype(BF16)
    cdec_s[...] = _dot_sel2(jnp.exp(acum), e)
    etot = jnp.exp(atot)
    tot_rows = [etot[ci * CHUNK:ci * CHUNK + 1, :] for ci in range(CPT)]
    dec_s[...] = _dot_sel2(jnp.concatenate(tot_rows + [jnp.zeros((SUBLANES - CPT, LANE), F32)], axis=0), e)
    ah_s[...] = pltpu.roll(acum, LANE - SM_DT - d * SSD_HEADS, 1)
    causal = _causal_mask(is_fwd, CHUNK)

    for ci in range(CPT):
        cidx = jnp.where(is_fwd, ci, CPT - 1 - ci)
        off = pl.multiple_of(cidx * CHUNK, CHUNK)
        rows = pl.ds(off, CHUNK)
        ah = ah_s[rows, :]
        aht = ah.T
        for g in range(SSD_GROUPS):
            gc = slice(g * gw, (g + 1) * gw)
            nc = slice(g * SSD_STATE, (g + 1) * SSD_STATE)
            bm = bm_ref[0, rows, nc]
            cm = cm_ref[0, rows, nc]
            st = st_s[g]
            cb = _dot_nt(cm, bm)
            y_inter = _dot(cm, st.astype(BF16)) * cdec_s[rows, gc]
            v_c = v_s[rows, gc]
            ys = []
            for hh in range(SSD_HPG):
                h = g * SSD_HPG + hh
                seg = jnp.exp(jnp.where(causal, ah[:, h:h + 1] - aht[h:h + 1, :], -1e30))
                ys.append(_dot((seg * cb).astype(BF16), v_c[:, hh * SSD_HEADDIM:(hh + 1) * SSD_HEADDIM]))
            o_ref[0, 0, rows, gc] = (jnp.concatenate(ys, axis=1) + y_inter).astype(BF16)
            st_s[g] = st * dec_s[pl.ds(cidx, 1), gc] + _dot(bm.astype(F32).T.astype(BF16), vw_s[rows, gc])


def ssd_scan(xbc, small, dtb, nega, e):
    bsz, t, _ = xbc.shape
    nt = t // TT
    gw = SSD_HPG * SSD_HEADDIM
    tile = lambda d, j: _scan_tile(d, j, nt)
    return pl.pallas_call(
        _ssd_kernel,
        grid=(bsz, 2, nt),
        in_specs=[
            pl.BlockSpec((1, TT, SSD_INNER), lambda i, d, j: (i, tile(d, j), 0)),
            pl.BlockSpec((1, TT, SSD_BC), lambda i, d, j: (i, tile(d, j), SSD_INNER // SSD_BC)),
            pl.BlockSpec((1, TT, SSD_BC), lambda i, d, j: (i, tile(d, j), SSD_INNER // SSD_BC + 1)),
            pl.BlockSpec((1, TT, LANE), lambda i, d, j: (i, tile(d, j), 0)),
            pl.BlockSpec((1, 1, LANE), lambda i, d, j: (d, 0, 0)),
            pl.BlockSpec((1, 1, LANE), lambda i, d, j: (d, 0, 0)),
            pl.BlockSpec((1, LANE, SSD_INNER), lambda i, d, j: (d, 0, 0)),
        ],
        out_specs=pl.BlockSpec((1, 1, TT, SSD_INNER), lambda i, d, j: (d, i, tile(d, j), 0)),
        out_shape=jax.ShapeDtypeStruct((2, bsz, t, SSD_INNER), BF16),
        scratch_shapes=[pltpu.VMEM((TT, SSD_INNER), BF16), pltpu.VMEM((TT, SSD_INNER), BF16),
                        pltpu.VMEM((TT, SSD_INNER), F32), pltpu.VMEM((SUBLANES, SSD_INNER), F32),
                        pltpu.VMEM((TT, LANE), F32), pltpu.VMEM((SSD_GROUPS, SSD_STATE, gw), F32)],
        compiler_params=pltpu.CompilerParams(
            dimension_semantics=("parallel", "parallel", "arbitrary")),
        name="ssd_scan",
    )(xbc, xbc, xbc, small, dtb, nega, e)


def _group_rmsnorm(x, width):
    parts = []
    for s in range(x.shape[1] // width):
        seg = x[:, s * width:(s + 1) * width]
        parts.append(seg * lax.rsqrt(jnp.mean(seg * seg, axis=-1, keepdims=True) + 1e-6))
    return jnp.concatenate(parts, axis=1)


def _ab_out_kernel(of_ref, ob_ref, yf_ref, yb_ref, r_ref, z_ref, xs_ref, gg_ref, dv_ref, sg_ref,
                   w_ref, b_ref, x_ref, gate_ref, g2_ref, sh2_ref, sc2_ref, rw_ref, o_ref, h_ref, aff_ref):
    o = (_group_rmsnorm(of_ref[0, 0].astype(F32) + ob_ref[0, 0].astype(F32), GLA_DV) * gg_ref[...]
         * _silu(r_ref[0].astype(F32)))
    y = ((yf_ref[0, 0].astype(F32) + yb_ref[0, 0].astype(F32) + dv_ref[...] * xs_ref[0].astype(F32))
         * _silu(z_ref[0].astype(F32)))
    y = _group_rmsnorm(y, SSD_INNER // SSD_GROUPS) * sg_ref[...]
    m = _dot(o.astype(BF16), w_ref[0:GLA_V, :]) + _dot(y.astype(BF16), w_ref[GLA_V:, :]) + b_ref[...]
    x_new = x_ref[0] + gate_ref[0, 0] * m
    o_ref[0] = x_new
    _route_tokens(x_new, g2_ref, sh2_ref, sc2_ref, rw_ref, h_ref, aff_ref)


def ab_out(o_gla, y_ssd, proj, xbc, gla_g, d_vec, ssd_g, w, b, x, gate, route):
    bsz, t, d = x.shape
    seg = lambda i, j: (i, jnp.minimum(j, 1), 0, 0)
    row = lambda width: pl.BlockSpec((1, width), lambda i, j: (0, 0))
    r_ins, r_outs, r_shapes = _route_specs(bsz, t, d, seg)
    g2, sh2, sc2, rw = route
    return pl.pallas_call(
        _ab_out_kernel,
        grid=(bsz, t // TT),
        in_specs=[
            pl.BlockSpec((1, 1, TT, GLA_V), lambda i, j: (0, i, j, 0)),
            pl.BlockSpec((1, 1, TT, GLA_V), lambda i, j: (1, i, j, 0)),
            pl.BlockSpec((1, 1, TT, SSD_INNER), lambda i, j: (0, i, j, 0)),
            pl.BlockSpec((1, 1, TT, SSD_INNER), lambda i, j: (1, i, j, 0)),
            pl.BlockSpec((1, TT, GLA_V), lambda i, j: (i, j, AB_R // GLA_V)),
            pl.BlockSpec((1, TT, SSD_INNER), lambda i, j: (i, j, AB_Z // SSD_INNER)),
            pl.BlockSpec((1, TT, SSD_INNER), lambda i, j: (i, j, 0)),
            row(GLA_V), row(SSD_INNER), row(SSD_INNER),
            pl.BlockSpec((GLA_V + SSD_INNER, d), lambda i, j: (0, 0)),
            row(d),
            pl.BlockSpec((1, TT, d), lambda i, j: (i, j, 0)),
            pl.BlockSpec((1, 1, 1, d), seg),
        ] + r_ins,
        out_specs=[pl.BlockSpec((1, TT, d), lambda i, j: (i, j, 0))] + r_outs,
        out_shape=[jax.ShapeDtypeStruct((bsz, t, d), F32)] + r_shapes,
        compiler_params=pltpu.CompilerParams(
            dimension_semantics=("parallel", "parallel"), vmem_limit_bytes=VMEM_LIMIT),
        name="ab_out",
    )(o_gla, o_gla, y_ssd, y_ssd, proj, proj, xbc, gla_g.reshape(1, -1), d_vec.reshape(1, -1),
      ssd_g.reshape(1, -1), w, b.reshape(1, d), x, gate, g2.reshape(1, d), sh2, sc2, rw)


def _cd_prep_kernel(q_ref, k_ref, v_ref, ql_ref, kl_ref, vl_ref, qr_ref, kr_ref, vr_ref, w_ref, o_ref):
    j = pl.program_id(1)
    nt = pl.num_programs(1)
    left_ok = j >= 2
    right_ok = jnp.logical_and(j >= 1, j < nt - 1)
    w = w_ref[...]
    srcs = ((q_ref, ql_ref, qr_ref, GDN_DK ** -0.5), (k_ref, kl_ref, kr_ref, 1.0), (v_ref, vl_ref, vr_ref, None))
    for s, (c_ref, l_ref, r_ref, scale) in enumerate(srcs):
        c0 = s * GDN_QK
        y = _conv3_piece(c_ref, l_ref, r_ref, w[:, c0:c0 + GDN_QK], 0.0, left_ok, right_ok)
        if scale is None:
            o_ref[0, :, c0:c0 + GDN_QK] = y.astype(BF16)
            continue
        for h in range(GDN_HEADS):
            seg = y[:, h * GDN_DK:(h + 1) * GDN_DK]
            inv = lax.rsqrt(jnp.sum(seg * seg, axis=-1, keepdims=True) + 1e-6) * scale
            o_ref[0, :, c0 + h * GDN_DK:c0 + (h + 1) * GDN_DK] = (seg * inv).astype(BF16)


def cd_prep(proj, conv_w):
    bsz, t, _ = proj.shape
    nt = t // TT
    rb = TT // HALO
    nrb = t // HALO
    width = GDN_QK
    cols = (2, 3, 4)

    cur = lambda cb: pl.BlockSpec((1, TT, width), lambda i, j: (i, j, cb))
    left = lambda cb: pl.BlockSpec((1, HALO, width), lambda i, j: (i, jnp.maximum(j * rb - 1, 0), cb))
    right = lambda cb: pl.BlockSpec((1, HALO, width), lambda i, j: (i, jnp.minimum((j + 1) * rb, nrb - 1), cb))
    return pl.pallas_call(
        _cd_prep_kernel,
        grid=(bsz, nt),
        in_specs=[cur(cb) for cb in cols] + [left(cb) for cb in cols] + [right(cb) for cb in cols] + [
            pl.BlockSpec((3, 3 * width), lambda i, j: (0, 0))],
        out_specs=pl.BlockSpec((1, TT, 3 * width), lambda i, j: (i, j, 0)),
        out_shape=jax.ShapeDtypeStruct((bsz, t, 3 * width), BF16),
        compiler_params=pltpu.CompilerParams(dimension_semantics=("parallel", "parallel")),
        name="cd_prep",
    )(*([proj] * 9), conv_w)


GDN_HB = 8


def _mm2(a, b):
    return _dot(a.astype(BF16), b.astype(BF16))


def _unit_tri_inverse(mats, b16, b32, eye):
    each = lambda f, *ls: [f(*xs) for xs in zip(*ls)]
    d16 = each(lambda a: jnp.where(b16, a, 0.0), mats)
    d2 = each(lambda x: _mm2(x, x), d16)
    d4 = each(lambda x: _mm2(x, x), d2)
    d8 = each(lambda x: _mm2(x, x), d4)
    t = each(lambda x: eye - x, d16)
    for p in (d2, d4, d8):
        t = each(lambda x, y: x + _mm2(x, y), t, p)
    off32 = jnp.logical_and(b32, jnp.logical_not(b16))
    for sel in (off32, jnp.logical_not(b32)):
        a_off = each(lambda a: jnp.where(sel, a, 0.0), mats)
        inner = each(_mm2, a_off, t)
        t = each(lambda x, y: x - _mm2(x, y), t, inner)
    return t


def _gdn_kernel(q_ref, k_ref, v_ref, sm_ref, dtb_ref, nega_ref, o_ref,
                dec_s, n_s, p_s, oc_s, qp_s, st_s):
    d = pl.program_id(2)
    j = pl.program_id(3)
    is_fwd = d == 0

    @pl.when(j == 0)
    def _():
        st_s[...] = jnp.zeros_like(st_s)

    sm = sm_ref[0]
    m_cum, m_all = _chunk_masks(is_fwd, TT)
    first = d * GDN_HEADS + pl.program_id(1) * GDN_HB
    rot = jnp.where(first == 0, 0, LANE - first)
    la = pltpu.roll(_softplus(sm + dtb_ref[...]) * nega_ref[...], rot, 1)
    be_sm = pltpu.roll(jax.nn.sigmoid(sm), rot, 1)
    gc_sm = _sel_dot(m_cum, la)
    gl_sm = _sel_dot(m_all, la)
    lane_bcast = lambda a, col: jnp.broadcast_to(a[:, col:col + 1], (TT, GDN_DK))

    r = lax.broadcasted_iota(jnp.int32, (CHUNK, CHUNK), 0)
    c = lax.broadcasted_iota(jnp.int32, (CHUNK, CHUNK), 1)
    causal = _causal_mask(is_fwd, CHUNK)
    strict = jnp.logical_and(causal, r != c)
    b16 = lax.shift_right_logical(r, 4) == lax.shift_right_logical(c, 4)
    b32 = lax.shift_right_logical(r, 5) == lax.shift_right_logical(c, 5)
    eye = jnp.where(r == c, 1.0, 0.0)
    chunk_rows = [slice(ci * CHUNK, (ci + 1) * CHUNK) for ci in range(CPT)]

    amats, rhss, aqks, kdts, qgs = [], [], [], [], []
    for hh in range(GDN_HB):
        cols = slice(hh * GDN_DK, (hh + 1) * GDN_DK)
        gc = lane_bcast(gc_sm, hh)
        gl = lane_bcast(gl_sm, hh)
        beta_e = lane_bcast(be_sm, 2 * GDN_HEADS + hh)
        q = q_ref[0, :, cols].astype(F32)
        k = k_ref[0, :, cols].astype(F32)
        egc = jnp.exp(gc)
        kb = k * beta_e
        qg = q * egc
        kd = k * jnp.exp(gl - gc)
        dec_s[:, cols] = jnp.exp(gl)
        rhs = jnp.concatenate([v_ref[0, :, cols].astype(F32) * beta_e, kb * egc], axis=1)
        for rows in chunk_rows:
            gcc = gc[rows, :]
            dmat = jnp.exp(jnp.where(causal, gcc[:, 0:CHUNK] - gcc.T[0:CHUNK, :], -1e30))
            kc = k[rows].astype(BF16)
            amats.append(jnp.where(strict, _dot_nt(kb[rows].astype(BF16), kc) * dmat, 0.0))
            rhss.append(rhs[rows])
            aqks.append((_dot_nt(q[rows].astype(BF16), kc) * dmat).astype(BF16))
            kdts.append(kd[rows].T.astype(BF16))
            qgs.append(qg[rows])
    tinv = _unit_tri_inverse(amats, b16, b32, eye)
    sols = [_mm2(t, rhs_c).astype(BF16) for t, rhs_c in zip(tinv, rhss)]
    for n, sol in enumerate(sols):
        hh, ci = n // CPT, n % CPT
        rows, cols = chunk_rows[ci], slice(hh * GDN_DK, (hh + 1) * GDN_DK)
        ks = _dot(kdts[n], sol)
        qs = _dot(aqks[n], sol)
        n_s[n] = ks[:, 0:GDN_DV]
        p_s[n] = ks[:, GDN_DV:].astype(BF16)
        oc_s[rows, cols] = qs[:, 0:GDN_DV]
        qp_s[rows, cols] = (qgs[n] - qs[:, GDN_DV:]).astype(BF16)

    for ci in range(CPT):
        cidx = jnp.where(is_fwd, ci, CPT - 1 - ci)
        off = pl.multiple_of(cidx * CHUNK, CHUNK)
        rows = pl.ds(off, CHUNK)
        for hh in range(GDN_HB):
            cols = slice(hh * GDN_DK, (hh + 1) * GDN_DK)
            st = st_s[hh]
            stb = st.astype(BF16)
            o_ref[0, 0, rows, cols] = (_dot(qp_s[rows, cols], stb) + oc_s[rows, cols]).astype(BF16)
            st_s[hh] = st * dec_s[pl.ds(off, 1), cols] - _dot(p_s[hh * CPT + cidx], stb) + n_s[hh * CPT + cidx]


def gdn_scan(qkv, small, dtb, nega):
    bsz, t, _ = qkv.shape
    nt = t // TT
    tile = lambda d, j: _scan_tile(d, j, nt)
    ng = GDN_HEADS // GDN_HB
    wb = GDN_HB * GDN_DK
    return pl.pallas_call(
        _gdn_kernel,
        grid=(bsz, ng, 2, nt),
        in_specs=[
            pl.BlockSpec((1, TT, wb), lambda i, h, d, j: (i, tile(d, j), h)),
            pl.BlockSpec((1, TT, wb), lambda i, h, d, j: (i, tile(d, j), ng + h)),
            pl.BlockSpec((1, TT, wb), lambda i, h, d, j: (i, tile(d, j), 2 * ng + h)),
            pl.BlockSpec((1, TT, LANE), lambda i, h, d, j: (i, tile(d, j), 0)),
            pl.BlockSpec((1, LANE), lambda i, h, d, j: (0, 0)),
            pl.BlockSpec((1, LANE), lambda i, h, d, j: (0, 0)),
        ],
        out_specs=pl.BlockSpec((1, 1, TT, wb), lambda i, h, d, j: (d, i, tile(d, j), h)),
        out_shape=jax.ShapeDtypeStruct((2, bsz, t, GDN_V), BF16),
        scratch_shapes=[pltpu.VMEM((TT, wb), F32),
                        pltpu.VMEM((GDN_HB * CPT, GDN_DK, GDN_DV), F32),
                        pltpu.VMEM((GDN_HB * CPT, GDN_DK, GDN_DK), BF16),
                        pltpu.VMEM((TT, wb), F32), pltpu.VMEM((TT, wb), BF16),
                        pltpu.VMEM((GDN_HB, GDN_DK, GDN_DV), F32)],
        compiler_params=pltpu.CompilerParams(
            dimension_semantics=("parallel", "parallel", "parallel", "arbitrary")),
        name="gdn_scan",
    )(qkv, qkv, qkv, small, dtb, nega)


CONF_PAD = 16


def _cd_out_kernel(ga_ref, gb_ref, og_ref, of_ref, ob_ref, cw_ref, cb_ref, lg_ref, lb_ref, ng_ref,
                   w_ref, b_ref, x_ref, gate_ref, g2_ref, sh2_ref, sc2_ref, rw_ref, o_ref, h_ref, aff_ref,
                   pad_s, shift_s, conv_s):
    j = pl.program_id(1)
    half = (CONF_KERNEL - 1) // 2
    glu = ga_ref[0].astype(F32) * jax.nn.sigmoid(gb_ref[0].astype(F32))
    zeros = jnp.zeros((CONF_PAD, CONF_CH), F32)

    def conv_segments(seglen):
        stride = seglen + 2 * CONF_PAD
        for g in range(TT // seglen):
            base = g * stride
            pad_s[base:base + CONF_PAD, :] = zeros
            pad_s[base + CONF_PAD:base + CONF_PAD + seglen, :] = glu[g * seglen:(g + 1) * seglen]
            pad_s[base + CONF_PAD + seglen:base + stride, :] = zeros
        nseg = TT // seglen
        used = nseg * stride
        conv_s[...] = jnp.zeros((TT, CONF_CH), F32) + cb_ref[...]
        for phase in range(SUBLANES):
            taps = [kk for kk in range(CONF_KERNEL) if (CONF_PAD - half + kk) % SUBLANES == phase]
            if phase:
                shift_s[0:used - SUBLANES, :] = pad_s[phase:used - SUBLANES + phase, :]
            src = shift_s if phase else pad_s
            for g in range(nseg):
                acc = conv_s[g * seglen:(g + 1) * seglen, :]
                for kk in taps:
                    lo = g * stride + CONF_PAD - half + kk - phase
                    acc = acc + src[lo:lo + seglen, :] * cw_ref[kk:kk + 1, :]
                conv_s[g * seglen:(g + 1) * seglen, :] = acc

    @pl.when(j == 0)
    def _():
        conv_segments(TT)

    @pl.when(j > 0)
    def _():
        conv_segments(GRID_W)

    acc = conv_s[...]
    mu = jnp.mean(acc, axis=-1, keepdims=True)
    cen = acc - mu
    var = jnp.mean(cen * cen, axis=-1, keepdims=True)
    conv = _silu(cen * lax.rsqrt(var + 1e-5) * lg_ref[...] + lb_ref[...])
    o = (_group_rmsnorm(of_ref[0, 0].astype(F32) + ob_ref[0, 0].astype(F32), GDN_DV) * ng_ref[...]
         * _silu(og_ref[0].astype(F32)))
    m = _dot(conv.astype(BF16), w_ref[0:CONF_CH, :]) + _dot(o.astype(BF16), w_ref[CONF_CH:, :]) + b_ref[...]
    x_new = x_ref[0] + gate_ref[0, 0] * m
    o_ref[0] = x_new
    _route_tokens(x_new, g2_ref, sh2_ref, sc2_ref, rw_ref, h_ref, aff_ref)


def cd_out(o_gdn, proj, conv_w, conv_b, ln_g, ln_b, norm_g, w, b, x, gate, route):
    bsz, t, d = x.shape
    seg = lambda i, j: (i, jnp.minimum(j, 1), 0, 0)
    row = lambda width: pl.BlockSpec((1, width), lambda i, j: (0, 0))
    r_ins, r_outs, r_shapes = _route_specs(bsz, t, d, seg)
    g2, sh2, sc2, rw = route
    return pl.pallas_call(
        _cd_out_kernel,
        grid=(bsz, t // TT),
        in_specs=[
            pl.BlockSpec((1, TT, CONF_CH), lambda i, j: (i, j, 0)),
            pl.BlockSpec((1, TT, CONF_CH), lambda i, j: (i, j, 1)),
            pl.BlockSpec((1, TT, GDN_V), lambda i, j: (i, j, 5)),
            pl.BlockSpec((1, 1, TT, GDN_V), lambda i, j: (0, i, j, 0)),
            pl.BlockSpec((1, 1, TT, GDN_V), lambda i, j: (1, i, j, 0)),
            pl.BlockSpec((CONF_KERNEL, CONF_CH), lambda i, j: (0, 0)),
            row(CONF_CH), row(CONF_CH), row(CONF_CH), row(GDN_V),
            pl.BlockSpec((CONF_CH + GDN_V, d), lambda i, j: (0, 0)),
            row(d),
            pl.BlockSpec((1, TT, d), lambda i, j: (i, j, 0)),
            pl.BlockSpec((1, 1, 1, d), seg),
        ] + r_ins,
        out_specs=[pl.BlockSpec((1, TT, d), lambda i, j: (i, j, 0))] + r_outs,
        out_shape=[jax.ShapeDtypeStruct((bsz, t, d), F32)] + r_shapes,
        scratch_shapes=[pltpu.VMEM(((TT // GRID_W) * (GRID_W + 2 * CONF_PAD), CONF_CH), F32),
                        pltpu.VMEM(((TT // GRID_W) * (GRID_W + 2 * CONF_PAD), CONF_CH), F32),
                        pltpu.VMEM((TT, CONF_CH), F32)],
        compiler_params=pltpu.CompilerParams(
            dimension_semantics=("parallel", "parallel"), vmem_limit_bytes=VMEM_LIMIT),
        name="cd_out",
    )(proj, proj, proj, o_gdn, o_gdn, conv_w, conv_b.reshape(1, -1), ln_g.reshape(1, -1), ln_b.reshape(1, -1),
      norm_g.reshape(1, -1), w, b.reshape(1, d), x, gate, g2.reshape(1, d), sh2, sc2, rw)


def _route_tokens(x, g_ref, sh_ref, sc_ref, rw_ref, h_ref, aff_ref):
    ms = jnp.mean(x * x, axis=-1, keepdims=True)
    h = (x * lax.rsqrt(ms + 1e-6) * g_ref[...] * (1.0 + sc_ref[0, 0]) + sh_ref[0, 0]).astype(BF16)
    h_ref[0] = h
    logits = _dot(h, rw_ref[...])
    lane = lax.broadcasted_iota(jnp.int32, logits.shape, 1)
    logits = jnp.where(lane < N_EXPERTS, logits, -1e30)
    e = jnp.exp(logits - jnp.max(logits, axis=-1, keepdims=True))
    aff = e / jnp.sum(e, axis=-1, keepdims=True)
    aff_ref[0] = aff.T[0:N_EXPERTS, :]


def _route_specs(bsz, t, d, seg_index):
    ins = [pl.BlockSpec((1, d), lambda i, j: (0, 0)),
           pl.BlockSpec((1, 1, 1, d), seg_index),
           pl.BlockSpec((1, 1, 1, d), seg_index),
           pl.BlockSpec((d, LANE), lambda i, j: (0, 0))]
    outs = [pl.BlockSpec((1, TT, d), lambda i, j: (i, j, 0)),
            pl.BlockSpec((1, N_EXPERTS, TT), lambda i, j: (i, 0, j))]
    shapes = [jax.ShapeDtypeStruct((bsz, t, d), BF16), jax.ShapeDtypeStruct((bsz, N_EXPERTS, t), F32)]
    return ins, outs, shapes


def _lane_block_prefix(x, u_strict):
    nblk = x.shape[1] // LANE
    run = jnp.zeros((x.shape[0], 1), F32)
    outs = []
    for cblk in range(nblk):
        xc = x[:, cblk * LANE:(cblk + 1) * LANE]
        outs.append(_dot(xc.astype(BF16), u_strict) + run)
        run = run + jnp.sum(xc, axis=-1, keepdims=True)
    return jnp.concatenate(outs, axis=1), run


def _select_kernel(aff_ref, slot_ref, *, cap):
    aff = aff_ref[0]
    bits = pltpu.bitcast(aff, jnp.int32)
    capf = jnp.float32(cap)

    def step(i, thr):
        cand = jnp.bitwise_or(thr, lax.shift_left(jnp.int32(1), 30 - i))
        cnt = jnp.sum(jnp.where(bits >= cand, 1.0, 0.0), axis=-1, keepdims=True)
        return jnp.where(cnt >= capf, cand, thr)

    thr = lax.fori_loop(0, 31, step, jnp.zeros((aff.shape[0], 1), jnp.int32))
    gt = jnp.where(bits > thr, 1.0, 0.0)
    eq = jnp.where(bits == thr, 1.0, 0.0)
    r = lax.broadcasted_iota(jnp.int32, (LANE, LANE), 0)
    c = lax.broadcasted_iota(jnp.int32, (LANE, LANE), 1)
    u_strict = jnp.where(r < c, 1.0, 0.0).astype(BF16)
    need = capf - jnp.sum(gt, axis=-1, keepdims=True)
    eq_rank, _ = _lane_block_prefix(eq, u_strict)
    sel = jnp.maximum(gt, jnp.where(eq_rank < need, eq, 0.0))
    slot, _ = _lane_block_prefix(sel, u_strict)
    slot_ref[0] = jnp.where(sel > 0.0, slot.astype(jnp.int32), -1)


def moe_select(aff, cap):
    bsz, ne, n = aff.shape
    return pl.pallas_call(
        functools.partial(_select_kernel, cap=cap),
        grid=(bsz,),
        in_specs=[pl.BlockSpec((1, ne, n), lambda i: (i, 0, 0))],
        out_specs=pl.BlockSpec((1, ne, n), lambda i: (i, 0, 0)),
        out_shape=jax.ShapeDtypeStruct((bsz, ne, n), jnp.int32),
        compiler_params=pltpu.CompilerParams(dimension_semantics=("parallel",)),
        name="moe_select",
    )(aff)


def _slot_index_kernel(slot_ref, idx_ref, *, cap):
    slot = slot_ref[0]
    n = slot.shape[1]
    srow = lax.broadcasted_iota(jnp.int32, (cap, LANE), 0)
    lane = lax.broadcasted_iota(jnp.int32, (cap, LANE), 1)
    acc = jnp.zeros((cap, LANE), jnp.int32)
    for cblk in range(n // LANE):
        s_c = slot[:, cblk * LANE:(cblk + 1) * LANE]
        acc = acc + jnp.where(srow == s_c, lane + (cblk * LANE + 1), 0)
    ones = jnp.ones((8, LANE), BF16)
    hi = _dot_nt(ones, lax.shift_right_logical(acc, 7).astype(F32).astype(BF16))
    lo = _dot_nt(ones, jnp.bitwise_and(acc, LANE - 1).astype(F32).astype(BF16))
    idx_ref[0] = (hi[0:1, :] * float(LANE) + lo[0:1, :]).astype(jnp.int32) - 1


SLOT_WIN = 2 * LANE


def _slot_index_win_kernel(base_ref, slot_ref, idx_ref, acc_s, *, cap):
    i = pl.program_id(0)
    n = slot_ref.shape[2]
    acc_s[...] = jnp.zeros(acc_s.shape, jnp.int32)
    srow = lax.broadcasted_iota(jnp.int32, (SLOT_WIN, LANE), 0)
    lane = lax.broadcasted_iota(jnp.int32, (SLOT_WIN, LANE), 1)
    for cblk in range(n // LANE):
        base = pl.multiple_of(base_ref[i, cblk], LANE)
        s_c = slot_ref[0, :, cblk * LANE:(cblk + 1) * LANE]
        rows = pl.ds(base, SLOT_WIN)
        acc_s[rows, :] = acc_s[rows, :] + jnp.where(srow + base == s_c, lane + (cblk * LANE + 1), 0)
    acc = acc_s[0:cap, :]
    ones = jnp.ones((8, LANE), BF16)
    hi = _dot_nt(ones, lax.shift_right_logical(acc, 7).astype(F32).astype(BF16))
    lo = _dot_nt(ones, jnp.bitwise_and(acc, LANE - 1).astype(F32).astype(BF16))
    idx_ref[0] = (hi[0:1, :] * float(LANE) + lo[0:1, :]).astype(jnp.int32) - 1


def moe_slot_index_windowed(slot, base, cap):
    bsz, ne, n = slot.shape
    idx = pl.pallas_call(
        functools.partial(_slot_index_win_kernel, cap=cap),
        grid_spec=pltpu.PrefetchScalarGridSpec(
            num_scalar_prefetch=1,
            grid=(bsz * ne,),
            in_specs=[pl.BlockSpec((1, 1, n), lambda i, base_ref: (i, 0, 0))],
            out_specs=pl.BlockSpec((1, 1, cap), lambda i, base_ref: (i, 0, 0)),
            scratch_shapes=[pltpu.VMEM((cap + LANE, LANE), jnp.int32)],
        ),
        out_shape=jax.ShapeDtypeStruct((bsz * ne, 1, cap), jnp.int32),
        compiler_params=pltpu.CompilerParams(dimension_semantics=("arbitrary",)),
        name="moe_slot_index",
    )(base, slot.reshape(bsz * ne, 1, n))
    return idx.reshape(bsz, ne, cap)


def moe_slot_index(slot, cap):
    bsz, ne, n = slot.shape
    idx = pl.pallas_call(
        functools.partial(_slot_index_kernel, cap=cap),
        grid=(bsz * ne,),
        in_specs=[pl.BlockSpec((1, 1, n), lambda i: (i, 0, 0))],
        out_specs=pl.BlockSpec((1, 1, cap), lambda i: (i, 0, 0)),
        out_shape=jax.ShapeDtypeStruct((bsz * ne, 1, cap), jnp.int32),
        compiler_params=pltpu.CompilerParams(dimension_semantics=("parallel",)),
        name="moe_slot_index",
    )(slot.reshape(bsz * ne, 1, n))
    return idx.reshape(bsz, ne, cap)


WIN_ALIGN = 16
WIN_FAST = 128


def _combine_kernel(ws_ref, slot_ref, aff_ref, *rest, win, final):
    ye_refs, (x_ref, gate_ref), o_ref = rest[:N_EXPERTS], rest[N_EXPERTS:N_EXPERTS + 2], rest[-1]
    b = pl.program_id(0)
    j = pl.program_id(1)
    srow = lax.broadcasted_iota(jnp.int32, (win, TT), 0)
    his, los = [], []
    for e in range(N_EXPERTS):
        sel = jnp.where(srow + ws_ref[b, e, j] == slot_ref[0, e:e + 1, :], aff_ref[0, e:e + 1, :], 0.0).T
        hi = sel.astype(BF16)
        his.append(hi)
        los.append((sel - hi.astype(F32)).astype(BF16))
    ye = jnp.concatenate([r[...] for r in ye_refs], axis=0)
    acc = _dot(jnp.concatenate(his, axis=1), ye) + _dot(jnp.concatenate(los, axis=1), ye)
    y = x_ref[0] + gate_ref[0, 0] * acc
    if final:
        g_ref = rest[N_EXPERTS + 2]
        y = y * lax.rsqrt(jnp.mean(y * y, axis=-1, keepdims=True) + 1e-6) * g_ref[...]
    o_ref[0] = y


def moe_combine(ws, slot, aff, ye, xa, gate, seg, tile0, win, final_g=None):
    bsz, ne, n = slot.shape
    nt = n // TT
    d = xa.shape[2]

    def ye_spec(e):
        return pl.BlockSpec((pl.Squeezed(), pl.Squeezed(), pl.Element(win), pl.Element(d)),
                            lambda i, j, ws_ref: (i, e, pl.multiple_of(ws_ref[i, e, j], WIN_ALIGN), 0))

    in_specs = ([pl.BlockSpec((1, ne, TT), lambda i, j, ws_ref: (i, 0, j)),
                 pl.BlockSpec((1, ne, TT), lambda i, j, ws_ref: (i, 0, j))]
                + [ye_spec(e) for e in range(ne)]
                + [pl.BlockSpec((1, TT, d), lambda i, j, ws_ref: (i, j + tile0, 0)),
                   pl.BlockSpec((1, 1, 1, d), lambda i, j, ws_ref: (i, seg, 0, 0))])
    args = (ws, slot, aff, *([ye] * ne), xa, gate)
    if final_g is None:
        out_idx, out_shape, aliases = (lambda i, j, ws_ref: (i, j + tile0, 0)), xa.shape, {3 + ne: 0}
    else:
        out_idx, out_shape, aliases = (lambda i, j, ws_ref: (i, j, 0)), (bsz, n, d), {}
        in_specs.append(pl.BlockSpec((1, d), lambda i, j, ws_ref: (0, 0)))
        args += (final_g.reshape(1, d),)
    return pl.pallas_call(
        functools.partial(_combine_kernel, win=win, final=final_g is not None),
        grid_spec=pltpu.PrefetchScalarGridSpec(
            num_scalar_prefetch=1,
            grid=(bsz, nt),
            in_specs=in_specs,
            out_specs=pl.BlockSpec((1, TT, d), out_idx),
        ),
        out_shape=jax.ShapeDtypeStruct(out_shape, F32),
        input_output_aliases=aliases,
        compiler_params=pltpu.CompilerParams(
            dimension_semantics=("parallel", "parallel"), vmem_limit_bytes=VMEM_LIMIT),
        name="moe_combine",
    )(*args)


def _expert_ffn_kernel(x_ref, w1_ref, w3_ref, w2_ref, o_ref, w1_s, w3_s, w2_s):
    @pl.when(jnp.logical_and(pl.program_id(1) == 0, pl.program_id(2) == 0))
    def _():
        w1_s[...] = w1_ref[0, 0].astype(BF16)
        w3_s[...] = w3_ref[0, 0].astype(BF16)
        w2_s[...] = w2_ref[0, 0].astype(BF16)

    x = x_ref[0, 0]
    a = _dot(x, w1_s[...])
    g = _dot(x, w3_s[...])
    o_ref[0, 0] = _dot((_silu(a) * g).astype(BF16), w2_s[...]).astype(BF16)


def expert_ffn(xe, w1, w3, w2, layer):
    bsz, ne, cap, d = xe.shape
    f = w1.shape[3]
    tm = min(cap, 512)
    return pl.pallas_call(
        _expert_ffn_kernel,
        grid=(ne, bsz, cap // tm),
        in_specs=[
            pl.BlockSpec((1, 1, tm, d), lambda e, i, j: (i, e, j, 0)),
            pl.BlockSpec((1, 1, d, f), lambda e, i, j: (layer, e, 0, 0)),
            pl.BlockSpec((1, 1, d, f), lambda e, i, j: (layer, e, 0, 0)),
            pl.BlockSpec((1, 1, f, d), lambda e, i, j: (layer, e, 0, 0)),
        ],
        out_specs=pl.BlockSpec((1, 1, tm, d), lambda e, i, j: (i, e, j, 0)),
        out_shape=jax.ShapeDtypeStruct((bsz, ne, cap, d), BF16),
        scratch_shapes=[pltpu.VMEM((d, f), BF16), pltpu.VMEM((d, f), BF16), pltpu.VMEM((f, d), BF16)],
        compiler_params=pltpu.CompilerParams(
            dimension_semantics=("arbitrary", "arbitrary", "arbitrary"), vmem_limit_bytes=VMEM_LIMIT),
        name="expert_ffn",
    )(xe, w1, w3, w2)


def _ab_in_layout(w_in, b_in):
    q, k, v, r, glr, z, xs, bm, cm, dt = _split_cols(
        jnp.concatenate([w_in, b_in[None]], axis=0),
        (GLA_QK, GLA_QK, GLA_V, GLA_V, 2 * GLA_GATE_RANK, SSD_INNER, SSD_INNER, SSD_BC, SSD_BC, 2 * SSD_HEADS))
    wide = jnp.concatenate([q, k, v, r, z, xs, bm, cm], axis=1)
    narrow = _pad_cols(jnp.concatenate([glr, dt], axis=1), LANE)
    return wide[:-1].astype(BF16), wide[-1], narrow[:-1].astype(BF16), narrow[-1]


def _gla_gate_params(w_gate2, b_gate2):
    wg = jnp.zeros((2, LANE, GLA_QK), F32)
    for d in range(2):
        wg = wg.at[d, d * GLA_GATE_RANK:(d + 1) * GLA_GATE_RANK, :].set(w_gate2[d])
    return wg.astype(BF16), b_gate2.reshape(2, 1, GLA_QK)


def _ssd_params(dt_bias, a_log):
    dtb = jnp.zeros((2, 1, LANE), F32)
    nega = jnp.zeros((2, 1, LANE), F32)
    e = np.zeros((2, LANE, SSD_INNER), np.float32)
    for d in range(2):
        c0 = SM_DT + d * SSD_HEADS
        dtb = dtb.at[d, 0, c0:c0 + SSD_HEADS].set(dt_bias[d])
        nega = nega.at[d, 0, c0:c0 + SSD_HEADS].set(-jnp.exp(a_log[d]))
        for h in range(SSD_HEADS):
            e[d, c0 + h, h * SSD_HEADDIM:(h + 1) * SSD_HEADDIM] = 1.0
    return dtb, nega, jnp.asarray(e, BF16)


def _gdn_params(dt_bias, a_log):
    n = 2 * GDN_HEADS
    dtb = jnp.zeros((1, LANE), F32).at[0, 0:n].set(dt_bias.reshape(n))
    nega = jnp.zeros((1, LANE), F32).at[0, 0:n].set(-jnp.exp(a_log.reshape(n)))
    return dtb, nega


def moe_segment(i, xa, h_all, aff_all, mods, w1, w3, w2, seg, tile0, ntiles, final_g=None):
    bsz, _, d = xa.shape
    n = ntiles * TT
    cap = n * EC_CAPACITY // N_EXPERTS
    aff = aff_all[:, :, tile0 * TT:tile0 * TT + n]
    slot = moe_select(aff, cap)
    cnt128 = jnp.sum((slot >= 0).reshape(bsz, N_EXPERTS, n // LANE, LANE), axis=-1, dtype=jnp.int32)
    start128 = jnp.cumsum(cnt128, axis=-1) - cnt128
    if cap >= SLOT_WIN:
        base = jnp.minimum(start128 // LANE * LANE, cap - LANE).reshape(bsz * N_EXPERTS, n // LANE)
        idx = moe_slot_index_windowed(slot, base, cap)
    else:
        idx = moe_slot_index(slot, cap)
    idx = idx.reshape(bsz, N_EXPERTS * cap) + tile0 * TT
    xe = jnp.take_along_axis(h_all, idx[..., None], axis=1, mode="promise_in_bounds")
    xe = xe.reshape(bsz, N_EXPERTS, cap, d)
    ye = expert_ffn(xe, w1, w3, w2, i)
    per_tile = TT // LANE
    counts = jnp.sum(cnt128.reshape(bsz, N_EXPERTS, ntiles, per_tile), axis=-1)
    starts = start128[:, :, ::per_tile]
    aligned = starts // WIN_ALIGN * WIN_ALIGN

    def run(win):
        ws = jnp.minimum(aligned, cap - win)
        return moe_combine(ws, slot, aff, ye, xa, mods[:, :, 5], seg, tile0, win, final_g)

    win_fast, win_full = min(cap, WIN_FAST), min(cap, TT + WIN_ALIGN)
    if win_fast == win_full:
        return run(win_full)
    overflow = jnp.any(starts + counts - jnp.minimum(aligned, cap - win_fast) > win_fast)
    return lax.cond(overflow, lambda: run(win_full), lambda: run(win_fast))


def layer_mixer(i, j, xa, mods, p, last):
    sh1, sc1, g1 = (mods[:, :, s] for s in range(3))
    route = (p["norm2_g"][i], mods[:, :, 3], mods[:, :, 4], _pad_cols(p["moe_router"][i], LANE).astype(BF16))
    if i % 2 == 0:
        proj, small = norm_proj(xa, p["norm1_g"][i], sh1, sc1, *_ab_in_layout(p["ab_w_in"][j], p["ab_b_in"][j]),
                                AB_WIDE // 2)
        xbc = ab_prep(proj, p["ssd_conv_w"][j], p["ssd_conv_b"][j])
        wg, bg = _gla_gate_params(p["gla_w_gate2"][j], p["gla_b_gate2"][j])
        o_gla = gla_scan(proj, small, wg, bg)
        y_ssd = ssd_scan(xbc, small, *_ssd_params(p["ssd_dt_bias"][j], p["ssd_a_log"][j]))
        return ab_out(o_gla, y_ssd, proj, xbc, p["gla_norm_g"][j], jnp.repeat(p["ssd_d"][j], SSD_HEADDIM),
                      p["ssd_norm_g"][j], p["ab_w_out"][j].astype(BF16), p["ab_b_out"][j], xa, g1, route)
    w_in, b_in = p["cd_w_in"][j], p["cd_b_in"][j]
    proj, small = norm_proj(xa, p["norm1_g"][i], sh1, sc1, w_in[:, :CD_WIDE].astype(BF16), b_in[:CD_WIDE],
                            _pad_cols(w_in[:, CD_WIDE:], LANE).astype(BF16), _pad_cols(b_in[CD_WIDE:], LANE),
                            CD_WIDE // 3)
    qkv = cd_prep(proj, p["gdn_conv_w"][j])
    o_gdn = gdn_scan(qkv, small, *_gdn_params(p["gdn_dt_bias"][j], p["gdn_a_log"][j]))
    return cd_out(o_gdn, proj, p["conf_dw_w"][j], p["conf_dw_b"][j], p["conf_ln_g"][j], p["conf_ln_b"][j],
                  p["gdn_norm_g"][j], p["cd_w_out"][j].astype(BF16), p["cd_b_out"][j], xa, g1, route)


def kernel(x, c, ctx, c_ctx, mod_w, mod_b, norm1_g, norm2_g, ab_w_in, ab_b_in, ab_w_out, ab_b_out, gla_w_gate2, gla_b_gate2, gla_norm_g, ssd_conv_w, ssd_conv_b, ssd_dt_bias, ssd_a_log, ssd_d, ssd_norm_g, cd_w_in, cd_b_in, cd_w_out, cd_b_out, conf_dw_w, conf_dw_b, conf_ln_g, conf_ln_b, gdn_conv_w, gdn_a_log, gdn_dt_bias, gdn_norm_g, moe_router, moe_w1, moe_w3, moe_w2, final_norm_g):
    p = dict(locals())
    bsz, seq, d = x.shape
    assert ctx.shape[1] == TT and seq % TT == 0
    depth = mod_w.shape[0]
    xa = jnp.concatenate([ctx, x], axis=1)
    w1, w3, w2 = moe_w1, moe_w3, moe_w2
    cond = jnp.concatenate([c, c_ctx[None]], axis=0)
    for i in range(depth):
        last = i == depth - 1
        mod = mod_proj(cond, mod_w[i].astype(BF16), mod_b[i])
        mods = jnp.stack([jnp.broadcast_to(mod[bsz], (bsz, 6 * d)), mod[:bsz]], axis=1).reshape(bsz, 2, 6, 1, d)
        xa, h_all, aff_all = layer_mixer(i, i // 2, xa, mods, p, last)
        if last:
            return moe_segment(i, xa, h_all, aff_all, mods, w1, w3, w2, 1, 1, seq // TT, final_norm_g)
        xa = moe_segment(i, xa, h_all, aff_all, mods, w1, w3, w2, 1, 1, seq // TT)
        xa = moe_segment(i, xa, h_all, aff_all, mods, w1, w3, w2, 0, 0, 1)
```

```python
import functools

import jax
import jax.numpy as jnp
import numpy as np
from jax import lax
from jax.experimental import pallas as pl
from jax.experimental.pallas import tpu as pltpu

F32 = jnp.float32
BF16 = jnp.bfloat16

D_MODEL = 1024
GRID_W = 64
CHUNK = 64
GLA_HEADS, GLA_DK, GLA_DV, GLA_GATE_RANK, GLA_GATE_TAU = 4, 128, 256, 16, 16.0
SSD_HEADS, SSD_HEADDIM, SSD_STATE, SSD_GROUPS = 16, 64, 128, 2
CONF_CH, CONF_KERNEL = D_MODEL, 31
GDN_HEADS, GDN_DK, GDN_DV = 8, 128, 128
N_EXPERTS, EC_CAPACITY = 16, 2

GLA_QK = GLA_HEADS * GLA_DK
GLA_V = GLA_HEADS * GLA_DV
SSD_INNER = SSD_HEADS * SSD_HEADDIM
SSD_BC = SSD_GROUPS * SSD_STATE
SSD_HPG = SSD_HEADS // SSD_GROUPS
GDN_QK = GDN_HEADS * GDN_DK
GDN_V = GDN_HEADS * GDN_DV

LANE = 128
SUBLANES = 8
TT = 256
CPT = TT // CHUNK
VMEM_LIMIT = 48 * 1024 * 1024

AB_Q, AB_K, AB_V, AB_R, AB_Z, AB_XS, AB_BM, AB_CM, AB_WIDE = 0, 512, 1024, 2048, 3072, 4096, 5120, 5376, 5632
SM_DT = 2 * GLA_GATE_RANK
CD_WIDE = 6 * D_MODEL


def _split_cols(a, sizes):
    return jnp.split(a, np.cumsum(sizes)[:-1].tolist(), axis=-1)


def _pad_cols(a, n):
    return jnp.pad(a, [(0, 0)] * (a.ndim - 1) + [(0, n - a.shape[-1])])


def _dot(a, b):
    return jnp.dot(a, b, preferred_element_type=F32)


def _dot_nt(a, b):
    return lax.dot_general(a, b, (((1,), (1,)), ((), ())), preferred_element_type=F32)


def _split3(x):
    hi = x.astype(BF16)
    r = x - hi.astype(F32)
    mid = r.astype(BF16)
    lo = (r - mid.astype(F32)).astype(BF16)
    return hi, mid, lo


def _sel_dot(m, x):
    hi, mid, lo = _split3(x)
    return _dot(m, hi) + _dot(m, mid) + _dot(m, lo)


def _softplus(x):
    return jnp.maximum(x, 0.0) + jnp.log(1.0 + jnp.exp(-jnp.abs(x)))


def _silu(x):
    return x * jax.nn.sigmoid(x)


def _chunk_masks(is_fwd, n):
    r = lax.broadcasted_iota(jnp.int32, (n, n), 0)
    c = lax.broadcasted_iota(jnp.int32, (n, n), 1)
    same = lax.shift_right_logical(r, 6) == lax.shift_right_logical(c, 6)
    lo = jnp.where(is_fwd, c, r)
    hi = jnp.where(is_fwd, r, c)
    cum = jnp.logical_and(same, lo <= hi)
    return jnp.where(cum, 1.0, 0.0).astype(BF16), jnp.where(same, 1.0, 0.0).astype(BF16)


def _causal_mask(is_fwd, n):
    r = lax.broadcasted_iota(jnp.int32, (n, n), 0)
    c = lax.broadcasted_iota(jnp.int32, (n, n), 1)
    return jnp.where(is_fwd, c, r) <= jnp.where(is_fwd, r, c)


def _scan_tile(d, j, nt):
    return jnp.where(d == 0, j, jnp.where(j == 0, 0, nt - j))


def _mod_proj_kernel(c_ref, w_ref, b_ref, o_ref):
    o_ref[...] = _dot(_silu(c_ref[...]).astype(BF16), w_ref[...]) + b_ref[...]


def mod_proj(cond, w, b):
    r, d = cond.shape
    n = w.shape[1]
    tn = 6 * LANE * 2
    return pl.pallas_call(
        _mod_proj_kernel,
        grid=(n // tn,),
        in_specs=[pl.BlockSpec((r, d), lambda k: (0, 0)),
                  pl.BlockSpec((d, tn), lambda k: (0, k)),
                  pl.BlockSpec((1, tn), lambda k: (0, k))],
        out_specs=pl.BlockSpec((r, tn), lambda k: (0, k)),
        out_shape=jax.ShapeDtypeStruct((r, n), F32),
        compiler_params=pltpu.CompilerParams(dimension_semantics=("parallel",)),
        name="mod_proj",
    )(cond, w, b.reshape(1, n))


NP_TM = 3 * TT


def _norm_proj_kernel(x_ref, g_ref, sh_ref, sc_ref, w_ref, b_ref, ws_ref, bs_ref, o_ref, os_ref):
    x = x_ref[0]
    row = lax.broadcasted_iota(jnp.int32, (NP_TM, 1), 0) + pl.program_id(2) * NP_TM
    is_ctx = row < TT
    scale = jnp.where(is_ctx, sc_ref[0, 0], sc_ref[0, 1])
    shift = jnp.where(is_ctx, sh_ref[0, 0], sh_ref[0, 1])
    ms = jnp.mean(x * x, axis=-1, keepdims=True)
    h = (x * lax.rsqrt(ms + 1e-6) * g_ref[...] * (1.0 + scale) + shift).astype(BF16)
    o_ref[0] = (_dot(h, w_ref[...]) + b_ref[...]).astype(BF16)
    os_ref[0, 0] = _dot(h, ws_ref[...]) + bs_ref[...]


def norm_proj(x, g, shift, scale, w, b, w_small, b_small, tn):
    bsz, t, d = x.shape
    n = w.shape[1]
    assert t % NP_TM == 0 and n % tn == 0
    wide, narrow = pl.pallas_call(
        _norm_proj_kernel,
        grid=(n // tn, bsz, t // NP_TM),
        in_specs=[
            pl.BlockSpec((1, NP_TM, d), lambda k, i, j: (i, j, 0)),
            pl.BlockSpec((1, d), lambda k, i, j: (0, 0)),
            pl.BlockSpec((1, 2, 1, d), lambda k, i, j: (i, 0, 0, 0)),
            pl.BlockSpec((1, 2, 1, d), lambda k, i, j: (i, 0, 0, 0)),
            pl.BlockSpec((d, tn), lambda k, i, j: (0, k)),
            pl.BlockSpec((1, tn), lambda k, i, j: (0, k)),
            pl.BlockSpec((d, LANE), lambda k, i, j: (0, 0)),
            pl.BlockSpec((1, LANE), lambda k, i, j: (0, 0)),
        ],
        out_specs=[pl.BlockSpec((1, NP_TM, tn), lambda k, i, j: (i, j, k)),
                   pl.BlockSpec((1, 1, NP_TM, LANE), lambda k, i, j: (k, i, j, 0))],
        out_shape=[jax.ShapeDtypeStruct((bsz, t, n), BF16),
                   jax.ShapeDtypeStruct((n // tn, bsz, t, LANE), F32)],
        compiler_params=pltpu.CompilerParams(
            dimension_semantics=("parallel", "parallel", "parallel"), vmem_limit_bytes=VMEM_LIMIT),
        name="norm_proj",
    )(x, g.reshape(1, d), shift, scale, w, b.reshape(1, n), w_small, b_small.reshape(1, LANE))
    return wide, narrow[0]


HALO = 16


def _conv3_piece(c_ref, l_ref, r_ref, w, b, left_ok, right_ok):
    x = c_ref[0].astype(F32)
    n = x.shape[0]
    row = lax.broadcasted_iota(jnp.int32, x.shape, 0)
    prev_row = jnp.where(left_ok, l_ref[0, HALO - 1:HALO, :].astype(F32), 0.0)
    next_row = jnp.where(right_ok, r_ref[0, 0:1, :].astype(F32), 0.0)
    x_prev = jnp.where(row == 0, prev_row, pltpu.roll(x, 1, 0))
    x_next = jnp.where(row == n - 1, next_row, pltpu.roll(x, n - 1, 0))
    return _silu(w[0:1, :] * x_prev + w[1:2, :] * x + w[2:3, :] * x_next + b)


def _ab_prep_kernel(xs_ref, bm_ref, cm_ref, xsl_ref, bml_ref, cml_ref, xsr_ref, bmr_ref, cmr_ref,
                    w_ref, b_ref, o_ref):
    j = pl.program_id(1)
    nt = pl.num_programs(1)
    left_ok = j >= 2
    right_ok = jnp.logical_and(j >= 1, j < nt - 1)
    w = w_ref[...]
    b = b_ref[...]
    o_ref[0, :, 0:SSD_INNER] = _conv3_piece(xs_ref, xsl_ref, xsr_ref, w[:, 0:SSD_INNER],
                                            b[:, 0:SSD_INNER], left_ok, right_ok).astype(BF16)
    c0, c1 = SSD_INNER, SSD_INNER + SSD_BC
    o_ref[0, :, c0:c1] = _conv3_piece(bm_ref, bml_ref, bmr_ref, w[:, c0:c1], b[:, c0:c1],
                                      left_ok, right_ok).astype(BF16)
    c0, c1 = c1, c1 + SSD_BC
    o_ref[0, :, c0:c1] = _conv3_piece(cm_ref, cml_ref, cmr_ref, w[:, c0:c1], b[:, c0:c1],
                                      left_ok, right_ok).astype(BF16)


def ab_prep(proj, conv_w, conv_b):
    bsz, t, _ = proj.shape
    nt = t // TT
    rb = TT // HALO
    nrb = t // HALO
    cw = SSD_INNER + 2 * SSD_BC

    def cur(width, col):
        return pl.BlockSpec((1, TT, width), lambda i, j: (i, j, col // width))

    def left(width, col):
        return pl.BlockSpec((1, HALO, width), lambda i, j: (i, jnp.maximum(j * rb - 1, 0), col // width))

    def right(width, col):
        return pl.BlockSpec((1, HALO, width), lambda i, j: (i, jnp.minimum((j + 1) * rb, nrb - 1), col // width))

    pieces = ((SSD_INNER, AB_XS), (SSD_BC, AB_BM), (SSD_BC, AB_CM))
    return pl.pallas_call(
        _ab_prep_kernel,
        grid=(bsz, nt),
        in_specs=[cur(*p) for p in pieces] + [left(*p) for p in pieces] + [right(*p) for p in pieces] + [
            pl.BlockSpec((3, cw), lambda i, j: (0, 0)),
            pl.BlockSpec((1, cw), lambda i, j: (0, 0)),
        ],
        out_specs=pl.BlockSpec((1, TT, cw), lambda i, j: (i, j, 0)),
        out_shape=jax.ShapeDtypeStruct((bsz, t, cw), BF16),
        compiler_params=pltpu.CompilerParams(dimension_semantics=("parallel", "parallel")),
        name="ab_prep",
    )(*([proj] * 9), conv_w, conv_b.reshape(1, cw))


def _gla_kernel(q_ref, k_ref, v_ref, sm_ref, wg_ref, bg_ref, o_ref, qg_s, egl_s, oi_s, u_s, st_s):
    d = pl.program_id(1)
    j = pl.program_id(2)
    is_fwd = d == 0

    @pl.when(j == 0)
    def _():
        st_s[...] = jnp.zeros_like(st_s)

    gz = _dot(sm_ref[0].astype(BF16), wg_ref[0]) + bg_ref[0]
    logg = (jnp.minimum(gz, 0.0) - jnp.log(1.0 + jnp.exp(-jnp.abs(gz)))) * (1.0 / GLA_GATE_TAU)
    m_cum, m_all = _chunk_masks(is_fwd, TT)
    gc = _sel_dot(m_cum, logg)
    totals = [jnp.where(is_fwd, gc[(ci + 1) * CHUNK - 1:(ci + 1) * CHUNK, :], gc[ci * CHUNK:ci * CHUNK + 1, :])
              for ci in range(CPT)]
    gl = jnp.concatenate([jnp.broadcast_to(row, (CHUNK, GLA_QK)) for row in totals], axis=0)
    q = q_ref[0].astype(F32) * (GLA_DK ** -0.5)
    k = k_ref[0].astype(F32)
    qg = (q * jnp.exp(gc)).astype(BF16)
    kn = (k * jnp.exp(-gc)).astype(BF16)
    kd = (k * jnp.exp(gl - gc)).astype(BF16)
    qg_s[...] = qg
    egl_s[...] = jnp.exp(gl)
    causal = _causal_mask(is_fwd, CHUNK)

    pairs = [(h, ci) for h in range(GLA_HEADS) for ci in range(CPT)]
    rows_of = lambda ci: slice(ci * CHUNK, (ci + 1) * CHUNK)
    kcols = lambda h: slice(h * GLA_DK, (h + 1) * GLA_DK)
    vcols = lambda h: slice(h * GLA_DV, (h + 1) * GLA_DV)
    vs = [v_ref[0, rows_of(ci), vcols(h)] for h, ci in pairs]
    atts = [jnp.where(causal, _dot_nt(qg[rows_of(ci), kcols(h)], kn[rows_of(ci), kcols(h)]), 0.0).astype(BF16)
            for h, ci in pairs]
    for n, (h, ci) in enumerate(pairs):
        oi_s[rows_of(ci), vcols(h)] = _dot(atts[n], vs[n])
        u_s[n] = _dot(vs[n].astype(F32).T.astype(BF16), kd[rows_of(ci), kcols(h)])

    for ci in range(CPT):
        cidx = jnp.where(is_fwd, ci, CPT - 1 - ci)
        off = pl.multiple_of(cidx * CHUNK, CHUNK)
        rows = pl.ds(off, CHUNK)
        for h in range(GLA_HEADS):
            st = st_s[h]
            o_ref[0, 0, rows, vcols(h)] = (oi_s[rows, vcols(h)]
                                           + _dot_nt(qg_s[rows, kcols(h)], st.astype(BF16))).astype(BF16)
            st_s[h] = st * egl_s[pl.ds(off, 1), kcols(h)] + u_s[h * CPT + cidx]


def gla_scan(proj, small, wg, bg):
    bsz, t, _ = proj.shape
    nt = t // TT
    tile = lambda d, j: _scan_tile(d, j, nt)
    return pl.pallas_call(
        _gla_kernel,
        grid=(bsz, 2, nt),
        in_specs=[
            pl.BlockSpec((1, TT, GLA_QK), lambda i, d, j: (i, tile(d, j), AB_Q // GLA_QK)),
            pl.BlockSpec((1, TT, GLA_QK), lambda i, d, j: (i, tile(d, j), AB_K // GLA_QK)),
            pl.BlockSpec((1, TT, GLA_V), lambda i, d, j: (i, tile(d, j), AB_V // GLA_V)),
            pl.BlockSpec((1, TT, LANE), lambda i, d, j: (i, tile(d, j), 0)),
            pl.BlockSpec((1, LANE, GLA_QK), lambda i, d, j: (d, 0, 0)),
            pl.BlockSpec((1, 1, GLA_QK), lambda i, d, j: (d, 0, 0)),
        ],
        out_specs=pl.BlockSpec((1, 1, TT, GLA_V), lambda i, d, j: (d, i, tile(d, j), 0)),
        out_shape=jax.ShapeDtypeStruct((2, bsz, t, GLA_V), BF16),
        scratch_shapes=[pltpu.VMEM((TT, GLA_QK), BF16), pltpu.VMEM((TT, GLA_QK), F32),
                        pltpu.VMEM((TT, GLA_V), F32), pltpu.VMEM((GLA_HEADS * CPT, GLA_DV, GLA_DK), F32),
                        pltpu.VMEM((GLA_HEADS, GLA_DV, GLA_DK), F32)],
        compiler_params=pltpu.CompilerParams(
            dimension_semantics=("parallel", "parallel", "arbitrary")),
        name="gla_scan",
    )(proj, proj, proj, small, wg, bg)


def _dot_sel2(x, e):
    hi = x.astype(BF16)
    return _dot(hi, e) + _dot((x - hi.astype(F32)).astype(BF16), e)


def _ssd_kernel(xs_ref, bm_ref, cm_ref, sm_ref, dtb_ref, nega_ref, e_ref, o_ref,
                v_s, vw_s, cdec_s, dec_s, ah_s, st_s):
    d = pl.program_id(1)
    j = pl.program_id(2)
    is_fwd = d == 0
    gw = SSD_HPG * SSD_HEADDIM

    @pl.when(j == 0)
    def _():
        st_s[...] = jnp.zeros_like(st_s)

    dt = _softplus(sm_ref[0] + dtb_ref[0])
    la = dt * nega_ref[0]
    m_cum, m_all = _chunk_masks(is_fwd, TT)
    acum = _sel_dot(m_cum, la)
    atot = _sel_dot(m_all, la)
    e = e_ref[0]
    v = xs_ref[0].astype(F32) * _dot_sel2(dt, e)
    v_s[...] = v.astype(BF16)
    vw_s[...] = (v * _dot_sel2(jnp.exp(atot - acum), e)).astype(BF16)
    cdec_s[...] = _dot_sel2(jnp.exp(acum), e)
    etot = jnp.exp(atot)
    tot_rows = [etot[ci * CHUNK:ci * CHUNK + 1, :] for ci in range(CPT)]
    dec_s[...] = _dot_sel2(jnp.concatenate(tot_rows + [jnp.zeros((SUBLANES - CPT, LANE), F32)], axis=0), e)
    ah_s[...] = pltpu.roll(acum, LANE - SM_DT - d * SSD_HEADS, 1)
    causal = _causal_mask(is_fwd, CHUNK)

    for ci in range(CPT):
        cidx = jnp.where(is_fwd, ci, CPT - 1 - ci)
        off = pl.multiple_of(cidx * CHUNK, CHUNK)
        rows = pl.ds(off, CHUNK)
        ah = ah_s[rows, :]
        aht = ah.T
        for g in range(SSD_GROUPS):
            gc = slice(g * gw, (g + 1) * gw)
            nc = slice(g * SSD_STATE, (g + 1) * SSD_STATE)
            bm = bm_ref[0, rows, nc]
            cm = cm_ref[0, rows, nc]
            st = st_s[g]
            cb = _dot_nt(cm, bm)
            y_inter = _dot(cm, st.astype(BF16)) * cdec_s[rows, gc]
            v_c = v_s[rows, gc]
            ys = []
            for hh in range(SSD_HPG):
                h = g * SSD_HPG + hh
                seg = jnp.exp(jnp.where(causal, ah[:, h:h + 1] - aht[h:h + 1, :], -1e30))
                ys.append(_dot((seg * cb).astype(BF16), v_c[:, hh * SSD_HEADDIM:(hh + 1) * SSD_HEADDIM]))
            o_ref[0, 0, rows, gc] = (jnp.concatenate(ys, axis=1) + y_inter).astype(BF16)
            st_s[g] = st * dec_s[pl.ds(cidx, 1), gc] + _dot(bm.astype(F32).T.astype(BF16), vw_s[rows, gc])


def ssd_scan(xbc, small, dtb, nega, e):
    bsz, t, _ = xbc.shape
    nt = t // TT
    gw = SSD_HPG * SSD_HEADDIM
    tile = lambda d, j: _scan_tile(d, j, nt)
    return pl.pallas_call(
        _ssd_kernel,
        grid=(bsz, 2, nt),
        in_specs=[
            pl.BlockSpec((1, TT, SSD_INNER), lambda i, d, j: (i, tile(d, j), 0)),
            pl.BlockSpec((1, TT, SSD_BC), lambda i, d, j: (i, tile(d, j), SSD_INNER // SSD_BC)),
            pl.BlockSpec((1, TT, SSD_BC), lambda i, d, j: (i, tile(d, j), SSD_INNER // SSD_BC + 1)),
            pl.BlockSpec((1, TT, LANE), lambda i, d, j: (i, tile(d, j), 0)),
            pl.BlockSpec((1, 1, LANE), lambda i, d, j: (d, 0, 0)),
            pl.BlockSpec((1, 1, LANE), lambda i, d, j: (d, 0, 0)),
            pl.BlockSpec((1, LANE, SSD_INNER), lambda i, d, j: (d, 0, 0)),
        ],
        out_specs=pl.BlockSpec((1, 1, TT, SSD_INNER), lambda i, d, j: (d, i, tile(d, j), 0)),
        out_shape=jax.ShapeDtypeStruct((2, bsz, t, SSD_INNER), BF16),
        scratch_shapes=[pltpu.VMEM((TT, SSD_INNER), BF16), pltpu.VMEM((TT, SSD_INNER), BF16),
                        pltpu.VMEM((TT, SSD_INNER), F32), pltpu.VMEM((SUBLANES, SSD_INNER), F32),
                        pltpu.VMEM((TT, LANE), F32), pltpu.VMEM((SSD_GROUPS, SSD_STATE, gw), F32)],
        compiler_params=pltpu.CompilerParams(
            dimension_semantics=("parallel", "parallel", "arbitrary")),
        name="ssd_scan",
    )(xbc, xbc, xbc, small, dtb, nega, e)


def _group_rmsnorm(x, width):
    parts = []
    for s in range(x.shape[1] // width):
        seg = x[:, s * width:(s + 1) * width]
        parts.append(seg * lax.rsqrt(jnp.mean(seg * seg, axis=-1, keepdims=True) + 1e-6))
    return jnp.concatenate(parts, axis=1)


def _ab_out_kernel(of_ref, ob_ref, yf_ref, yb_ref, r_ref, z_ref, xs_ref, gg_ref, dv_ref, sg_ref,
                   w_ref, b_ref, x_ref, gate_ref, g2_ref, sh2_ref, sc2_ref, rw_ref, o_ref, h_ref, aff_ref):
    o = (_group_rmsnorm(of_ref[0, 0].astype(F32) + ob_ref[0, 0].astype(F32), GLA_DV) * gg_ref[...]
         * _silu(r_ref[0].astype(F32)))
    y = ((yf_ref[0, 0].astype(F32) + yb_ref[0, 0].astype(F32) + dv_ref[...] * xs_ref[0].astype(F32))
         * _silu(z_ref[0].astype(F32)))
    y = _group_rmsnorm(y, SSD_INNER // SSD_GROUPS) * sg_ref[...]
    m = _dot(o.astype(BF16), w_ref[0:GLA_V, :]) + _dot(y.astype(BF16), w_ref[GLA_V:, :]) + b_ref[...]
    x_new = x_ref[0] + gate_ref[0, 0] * m
    o_ref[0] = x_new
    _route_tokens(x_new, g2_ref, sh2_ref, sc2_ref, rw_ref, h_ref, aff_ref)


def ab_out(o_gla, y_ssd, proj, xbc, gla_g, d_vec, ssd_g, w, b, x, gate, route):
    bsz, t, d = x.shape
    seg = lambda i, j: (i, jnp.minimum(j, 1), 0, 0)
    row = lambda width: pl.BlockSpec((1, width), lambda i, j: (0, 0))
    r_ins, r_outs, r_shapes = _route_specs(bsz, t, d, seg)
    g2, sh2, sc2, rw = route
    return pl.pallas_call(
        _ab_out_kernel,
        grid=(bsz, t // TT),
        in_specs=[
            pl.BlockSpec((1, 1, TT, GLA_V), lambda i, j: (0, i, j, 0)),
            pl.BlockSpec((1, 1, TT, GLA_V), lambda i, j: (1, i, j, 0)),
            pl.BlockSpec((1, 1, TT, SSD_INNER), lambda i, j: (0, i, j, 0)),
            pl.BlockSpec((1, 1, TT, SSD_INNER), lambda i, j: (1, i, j, 0)),
            pl.BlockSpec((1, TT, GLA_V), lambda i, j: (i, j, AB_R // GLA_V)),
            pl.BlockSpec((1, TT, SSD_INNER), lambda i, j: (i, j, AB_Z // SSD_INNER)),
            pl.BlockSpec((1, TT, SSD_INNER), lambda i, j: (i, j, 0)),
            row(GLA_V), row(SSD_INNER), row(SSD_INNER),
            pl.BlockSpec((GLA_V + SSD_INNER, d), lambda i, j: (0, 0)),
            row(d),
            pl.BlockSpec((1, TT, d), lambda i, j: (i, j, 0)),
            pl.BlockSpec((1, 1, 1, d), seg),
        ] + r_ins,
        out_specs=[pl.BlockSpec((1, TT, d), lambda i, j: (i, j, 0))] + r_outs,
        out_shape=[jax.ShapeDtypeStruct((bsz, t, d), F32)] + r_shapes,
        compiler_params=pltpu.CompilerParams(
            dimension_semantics=("parallel", "parallel"), vmem_limit_bytes=VMEM_LIMIT),
        name="ab_out",
    )(o_gla, o_gla, y_ssd, y_ssd, proj, proj, xbc, gla_g.reshape(1, -1), d_vec.reshape(1, -1),
      ssd_g.reshape(1, -1), w, b.reshape(1, d), x, gate, g2.reshape(1, d), sh2, sc2, rw)


def _cd_prep_kernel(q_ref, k_ref, v_ref, ql_ref, kl_ref, vl_ref, qr_ref, kr_ref, vr_ref, w_ref, o_ref):
    j = pl.program_id(1)
    nt = pl.num_programs(1)
    left_ok = j >= 2
    right_ok = jnp.logical_and(j >= 1, j < nt - 1)
    w = w_ref[...]
    srcs = ((q_ref, ql_ref, qr_ref, GDN_DK ** -0.5), (k_ref, kl_ref, kr_ref, 1.0), (v_ref, vl_ref, vr_ref, None))
    for s, (c_ref, l_ref, r_ref, scale) in enumerate(srcs):
        c0 = s * GDN_QK
        y = _conv3_piece(c_ref, l_ref, r_ref, w[:, c0:c0 + GDN_QK], 0.0, left_ok, right_ok)
        if scale is None:
            o_ref[0, :, c0:c0 + GDN_QK] = y.astype(BF16)
            continue
        for h in range(GDN_HEADS):
            seg = y[:, h * GDN_DK:(h + 1) * GDN_DK]
            inv = lax.rsqrt(jnp.sum(seg * seg, axis=-1, keepdims=True) + 1e-6) * scale
            o_ref[0, :, c0 + h * GDN_DK:c0 + (h + 1) * GDN_DK] = (seg * inv).astype(BF16)


def cd_prep(proj, conv_w):
    bsz, t, _ = proj.shape
    nt = t // TT
    rb = TT // HALO
    nrb = t // HALO
    width = GDN_QK
    cols = (2, 3, 4)

    cur = lambda cb: pl.BlockSpec((1, TT, width), lambda i, j: (i, j, cb))
    left = lambda cb: pl.BlockSpec((1, HALO, width), lambda i, j: (i, jnp.maximum(j * rb - 1, 0), cb))
    right = lambda cb: pl.BlockSpec((1, HALO, width), lambda i, j: (i, jnp.minimum((j + 1) * rb, nrb - 1), cb))
    return pl.pallas_call(
        _cd_prep_kernel,
        grid=(bsz, nt),
        in_specs=[cur(cb) for cb in cols] + [left(cb) for cb in cols] + [right(cb) for cb in cols] + [
            pl.BlockSpec((3, 3 * width), lambda i, j: (0, 0))],
        out_specs=pl.BlockSpec((1, TT, 3 * width), lambda i, j: (i, j, 0)),
        out_shape=jax.ShapeDtypeStruct((bsz, t, 3 * width), BF16),
        compiler_params=pltpu.CompilerParams(dimension_semantics=("parallel", "parallel")),
        name="cd_prep",
    )(*([proj] * 9), conv_w)


GDN_HB = 8


def _mm2(a, b):
    return _dot(a.astype(BF16), b.astype(BF16))


def _unit_tri_inverse(mats, b16, b32, eye):
    each = lambda f, *ls: [f(*xs) for xs in zip(*ls)]
    d16 = each(lambda a: jnp.where(b16, a, 0.0), mats)
    d2 = each(lambda x: _mm2(x, x), d16)
    d4 = each(lambda x: _mm2(x, x), d2)
    d8 = each(lambda x: _mm2(x, x), d4)
    t = each(lambda x: eye - x, d16)
    for p in (d2, d4, d8):
        t = each(lambda x, y: x + _mm2(x, y), t, p)
    off32 = jnp.logical_and(b32, jnp.logical_not(b16))
    for sel in (off32, jnp.logical_not(b32)):
        a_off = each(lambda a: jnp.where(sel, a, 0.0), mats)
        inner = each(_mm2, a_off, t)
        t = each(lambda x, y: x - _mm2(x, y), t, inner)
    return t


def _gdn_kernel(q_ref, k_ref, v_ref, sm_ref, dtb_ref, nega_ref, o_ref,
                dec_s, n_s, p_s, oc_s, qp_s, st_s):
    d = pl.program_id(2)
    j = pl.program_id(3)
    is_fwd = d == 0

    @pl.when(j == 0)
    def _():
        st_s[...] = jnp.zeros_like(st_s)

    sm = sm_ref[0]
    m_cum, m_all = _chunk_masks(is_fwd, TT)
    first = d * GDN_HEADS + pl.program_id(1) * GDN_HB
    rot = jnp.where(first == 0, 0, LANE - first)
    la = pltpu.roll(_softplus(sm + dtb_ref[...]) * nega_ref[...], rot, 1)
    be_sm = pltpu.roll(jax.nn.sigmoid(sm), rot, 1)
    gc_sm = _sel_dot(m_cum, la)
    gl_sm = _sel_dot(m_all, la)
    lane_bcast = lambda a, col: jnp.broadcast_to(a[:, col:col + 1], (TT, GDN_DK))

    r = lax.broadcasted_iota(jnp.int32, (CHUNK, CHUNK), 0)
    c = lax.broadcasted_iota(jnp.int32, (CHUNK, CHUNK), 1)
    causal = _causal_mask(is_fwd, CHUNK)
    strict = jnp.logical_and(causal, r != c)
    b16 = lax.shift_right_logical(r, 4) == lax.shift_right_logical(c, 4)
    b32 = lax.shift_right_logical(r, 5) == lax.shift_right_logical(c, 5)
    eye = jnp.where(r == c, 1.0, 0.0)
    chunk_rows = [slice(ci * CHUNK, (ci + 1) * CHUNK) for ci in range(CPT)]

    amats, rhss, aqks, kdts, qgs = [], [], [], [], []
    for hh in range(GDN_HB):
        cols = slice(hh * GDN_DK, (hh + 1) * GDN_DK)
        gc = lane_bcast(gc_sm, hh)
        gl = lane_bcast(gl_sm, hh)
        beta_e = lane_bcast(be_sm, 2 * GDN_HEADS + hh)
        q = q_ref[0, :, cols].astype(F32)
        k = k_ref[0, :, cols].astype(F32)
        egc = jnp.exp(gc)
        kb = k * beta_e
        qg = q * egc
        kd = k * jnp.exp(gl - gc)
        dec_s[:, cols] = jnp.exp(gl)
        rhs = jnp.concatenate([v_ref[0, :, cols].astype(F32) * beta_e, kb * egc], axis=1)
        for rows in chunk_rows:
            gcc = gc[rows, :]
            dmat = jnp.exp(jnp.where(causal, gcc[:, 0:CHUNK] - gcc.T[0:CHUNK, :], -1e30))
            kc = k[rows].astype(BF16)
            amats.append(jnp.where(strict, _dot_nt(kb[rows].astype(BF16), kc) * dmat, 0.0))
            rhss.append(rhs[rows])
            aqks.append((_dot_nt(q[rows].astype(BF16), kc) * dmat).astype(BF16))
            kdts.append(kd[rows].T.astype(BF16))
            qgs.append(qg[rows])
    tinv = _unit_tri_inverse(amats, b16, b32, eye)
    sols = [_mm2(t, rhs_c).astype(BF16) for t, rhs_c in zip(tinv, rhss)]
    for n, sol in enumerate(sols):
        hh, ci = n // CPT, n % CPT
        rows, cols = chunk_rows[ci], slice(hh * GDN_DK, (hh + 1) * GDN_DK)
        ks = _dot(kdts[n], sol)
        qs = _dot(aqks[n], sol)
        n_s[n] = ks[:, 0:GDN_DV]
        p_s[n] = ks[:, GDN_DV:].astype(BF16)
        oc_s[rows, cols] = qs[:, 0:GDN_DV]
        qp_s[rows, cols] = (qgs[n] - qs[:, GDN_DV:]).astype(BF16)

    for ci in range(CPT):
        cidx = jnp.where(is_fwd, ci, CPT - 1 - ci)
        off = pl.multiple_of(cidx * CHUNK, CHUNK)
        rows = pl.ds(off, CHUNK)
        for hh in range(GDN_HB):
            cols = slice(hh * GDN_DK, (hh + 1) * GDN_DK)
            st = st_s[hh]
            stb = st.astype(BF16)
            o_ref[0, 0, rows, cols] = (_dot(qp_s[rows, cols], stb) + oc_s[rows, cols]).astype(BF16)
            st_s[hh] = st * dec_s[pl.ds(off, 1), cols] - _dot(p_s[hh * CPT + cidx], stb) + n_s[hh * CPT + cidx]


def gdn_scan(qkv, small, dtb, nega):
    bsz, t, _ = qkv.shape
    nt = t // TT
    tile = lambda d, j: _scan_tile(d, j, nt)
    ng = GDN_HEADS // GDN_HB
    wb = GDN_HB * GDN_DK
    return pl.pallas_call(
        _gdn_kernel,
        grid=(bsz, ng, 2, nt),
        in_specs=[
            pl.BlockSpec((1, TT, wb), lambda i, h, d, j: (i, tile(d, j), h)),
            pl.BlockSpec((1, TT, wb), lambda i, h, d, j: (i, tile(d, j), ng + h)),
            pl.BlockSpec((1, TT, wb), lambda i, h, d, j: (i, tile(d, j), 2 * ng + h)),
            pl.BlockSpec((1, TT, LANE), lambda i, h, d, j: (i, tile(d, j), 0)),
            pl.BlockSpec((1, LANE), lambda i, h, d, j: (0, 0)),
            pl.BlockSpec((1, LANE), lambda i, h, d, j: (0, 0)),
        ],
        out_specs=pl.BlockSpec((1, 1, TT, wb), lambda i, h, d, j: (d, i, tile(d, j), h)),
        out_shape=jax.ShapeDtypeStruct((2, bsz, t, GDN_V), BF16),
        scratch_shapes=[pltpu.VMEM((TT, wb), F32),
                        pltpu.VMEM((GDN_HB * CPT, GDN_DK, GDN_DV), F32),
                        pltpu.VMEM((GDN_HB * CPT, GDN_DK, GDN_DK), BF16),
                        pltpu.VMEM((TT, wb), F32), pltpu.VMEM((TT, wb), BF16),
                        pltpu.VMEM((GDN_HB, GDN_DK, GDN_DV), F32)],
        compiler_params=pltpu.CompilerParams(
            dimension_semantics=("parallel", "parallel", "parallel", "arbitrary")),
        name="gdn_scan",
    )(qkv, qkv, qkv, small, dtb, nega)


CONF_PAD = 16


def _cd_out_kernel(ga_ref, gb_ref, og_ref, of_ref, ob_ref, cw_ref, cb_ref, lg_ref, lb_ref, ng_ref,
                   w_ref, b_ref, x_ref, gate_ref, g2_ref, sh2_ref, sc2_ref, rw_ref, o_ref, h_ref, aff_ref,
                   pad_s, shift_s, conv_s):
    j = pl.program_id(1)
    half = (CONF_KERNEL - 1) // 2
    glu = ga_ref[0].astype(F32) * jax.nn.sigmoid(gb_ref[0].astype(F32))
    zeros = jnp.zeros((CONF_PAD, CONF_CH), F32)

    def conv_segments(seglen):
        stride = seglen + 2 * CONF_PAD
        for g in range(TT // seglen):
            base = g * stride
            pad_s[base:base + CONF_PAD, :] = zeros
            pad_s[base + CONF_PAD:base + CONF_PAD + seglen, :] = glu[g * seglen:(g + 1) * seglen]
            pad_s[base + CONF_PAD + seglen:base + stride, :] = zeros
        nseg = TT // seglen
        used = nseg * stride
        conv_s[...] = jnp.zeros((TT, CONF_CH), F32) + cb_ref[...]
        for phase in range(SUBLANES):
            taps = [kk for kk in range(CONF_KERNEL) if (CONF_PAD - half + kk) % SUBLANES == phase]
            if phase:
                shift_s[0:used - SUBLANES, :] = pad_s[phase:used - SUBLANES + phase, :]
            src = shift_s if phase else pad_s
            for g in range(nseg):
                acc = conv_s[g * seglen:(g + 1) * seglen, :]
                for kk in taps:
                    lo = g * stride + CONF_PAD - half + kk - phase
                    acc = acc + src[lo:lo + seglen, :] * cw_ref[kk:kk + 1, :]
                conv_s[g * seglen:(g + 1) * seglen, :] = acc

    @pl.when(j == 0)
    def _():
        conv_segments(TT)

    @pl.when(j > 0)
    def _():
        conv_segments(GRID_W)

    acc = conv_s[...]
    mu = jnp.mean(acc, axis=-1, keepdims=True)
    cen = acc - mu
    var = jnp.mean(cen * cen, axis=-1, keepdims=True)
    conv = _silu(cen * lax.rsqrt(var + 1e-5) * lg_ref[...] + lb_ref[...])
    o = (_group_rmsnorm(of_ref[0, 0].astype(F32) + ob_ref[0, 0].astype(F32), GDN_DV) * ng_ref[...]
         * _silu(og_ref[0].astype(F32)))
    m = _dot(conv.astype(BF16), w_ref[0:CONF_CH, :]) + _dot(o.astype(BF16), w_ref[CONF_CH:, :]) + b_ref[...]
    x_new = x_ref[0] + gate_ref[0, 0] * m
    o_ref[0] = x_new
    _route_tokens(x_new, g2_ref, sh2_ref, sc2_ref, rw_ref, h_ref, aff_ref)


def cd_out(o_gdn, proj, conv_w, conv_b, ln_g, ln_b, norm_g, w, b, x, gate, route):
    bsz, t, d = x.shape
    seg = lambda i, j: (i, jnp.minimum(j, 1), 0, 0)
    row = lambda width: pl.BlockSpec((1, width), lambda i, j: (0, 0))
    r_ins, r_outs, r_shapes = _route_specs(bsz, t, d, seg)
    g2, sh2, sc2, rw = route
    return pl.pallas_call(
        _cd_out_kernel,
        grid=(bsz, t // TT),
        in_specs=[
            pl.BlockSpec((1, TT, CONF_CH), lambda i, j: (i, j, 0)),
            pl.BlockSpec((1, TT, CONF_CH), lambda i, j: (i, j, 1)),
            pl.BlockSpec((1, TT, GDN_V), lambda i, j: (i, j, 5)),
            pl.BlockSpec((1, 1, TT, GDN_V), lambda i, j: (0, i, j, 0)),
            pl.BlockSpec((1, 1, TT, GDN_V), lambda i, j: (1, i, j, 0)),
            pl.BlockSpec((CONF_KERNEL, CONF_CH), lambda i, j: (0, 0)),
            row(CONF_CH), row(CONF_CH), row(CONF_CH), row(GDN_V),
            pl.BlockSpec((CONF_CH + GDN_V, d), lambda i, j: (0, 0)),
            row(d),
            pl.BlockSpec((1, TT, d), lambda i, j: (i, j, 0)),
            pl.BlockSpec((1, 1, 1, d), seg),
        ] + r_ins,
        out_specs=[pl.BlockSpec((1, TT, d), lambda i, j: (i, j, 0))] + r_outs,
        out_shape=[jax.ShapeDtypeStruct((bsz, t, d), F32)] + r_shapes,
        scratch_shapes=[pltpu.VMEM(((TT // GRID_W) * (GRID_W + 2 * CONF_PAD), CONF_CH), F32),
                        pltpu.VMEM(((TT // GRID_W) * (GRID_W + 2 * CONF_PAD), CONF_CH), F32),
                        pltpu.VMEM((TT, CONF_CH), F32)],
        compiler_params=pltpu.CompilerParams(
            dimension_semantics=("parallel", "parallel"), vmem_limit_bytes=VMEM_LIMIT),
        name="cd_out",
    )(proj, proj, proj, o_gdn, o_gdn, conv_w, conv_b.reshape(1, -1), ln_g.reshape(1, -1), ln_b.reshape(1, -1),
      norm_g.reshape(1, -1), w, b.reshape(1, d), x, gate, g2.reshape(1, d), sh2, sc2, rw)


def _route_tokens(x, g_ref, sh_ref, sc_ref, rw_ref, h_ref, aff_ref):
    ms = jnp.mean(x * x, axis=-1, keepdims=True)
    h = (x * lax.rsqrt(ms + 1e-6) * g_ref[...] * (1.0 + sc_ref[0, 0]) + sh_ref[0, 0]).astype(BF16)
    h_ref[0] = h
    logits = _dot(h, rw_ref[...])
    lane = lax.broadcasted_iota(jnp.int32, logits.shape, 1)
    logits = jnp.where(lane < N_EXPERTS, logits, -1e30)
    e = jnp.exp(logits - jnp.max(logits, axis=-1, keepdims=True))
    aff = e / jnp.sum(e, axis=-1, keepdims=True)
    aff_ref[0] = aff.T[0:N_EXPERTS, :]


def _route_specs(bsz, t, d, seg_index):
    ins = [pl.BlockSpec((1, d), lambda i, j: (0, 0)),
           pl.BlockSpec((1, 1, 1, d), seg_index),
           pl.BlockSpec((1, 1, 1, d), seg_index),
           pl.BlockSpec((d, LANE), lambda i, j: (0, 0))]
    outs = [pl.BlockSpec((1, TT, d), lambda i, j: (i, j, 0)),
            pl.BlockSpec((1, N_EXPERTS, TT), lambda i, j: (i, 0, j))]
    shapes = [jax.ShapeDtypeStruct((bsz, t, d), BF16), jax.ShapeDtypeStruct((bsz, N_EXPERTS, t), F32)]
    return ins, outs, shapes


def _lane_block_prefix(x, u_strict):
    nblk = x.shape[1] // LANE
    run = jnp.zeros((x.shape[0], 1), F32)
    outs = []
    for cblk in range(nblk):
        xc = x[:, cblk * LANE:(cblk + 1) * LANE]
        outs.append(_dot(xc.astype(BF16), u_strict) + run)
        run = run + jnp.sum(xc, axis=-1, keepdims=True)
    return jnp.concatenate(outs, axis=1), run


def _select_kernel(aff_ref, slot_ref, *, cap):
    aff = aff_ref[0]
    bits = pltpu.bitcast(aff, jnp.int32)
    capf = jnp.float32(cap)

    def step(i, thr):
        cand = jnp.bitwise_or(thr, lax.shift_left(jnp.int32(1), 30 - i))
        cnt = jnp.sum(jnp.where(bits >= cand, 1.0, 0.0), axis=-1, keepdims=True)
        return jnp.where(cnt >= capf, cand, thr)

    thr = lax.fori_loop(0, 31, step, jnp.zeros((aff.shape[0], 1), jnp.int32))
    gt = jnp.where(bits > thr, 1.0, 0.0)
    eq = jnp.where(bits == thr, 1.0, 0.0)
    r = lax.broadcasted_iota(jnp.int32, (LANE, LANE), 0)
    c = lax.broadcasted_iota(jnp.int32, (LANE, LANE), 1)
    u_strict = jnp.where(r < c, 1.0, 0.0).astype(BF16)
    need = capf - jnp.sum(gt, axis=-1, keepdims=True)
    eq_rank, _ = _lane_block_prefix(eq, u_strict)
    sel = jnp.maximum(gt, jnp.where(eq_rank < need, eq, 0.0))
    slot, _ = _lane_block_prefix(sel, u_strict)
    slot_ref[0] = jnp.where(sel > 0.0, slot.astype(jnp.int32), -1)


def moe_select(aff, cap):
    bsz, ne, n = aff.shape
    return pl.pallas_call(
        functools.partial(_select_kernel, cap=cap),
        grid=(bsz,),
        in_specs=[pl.BlockSpec((1, ne, n), lambda i: (i, 0, 0))],
        out_specs=pl.BlockSpec((1, ne, n), lambda i: (i, 0, 0)),
        out_shape=jax.ShapeDtypeStruct((bsz, ne, n), jnp.int32),
        compiler_params=pltpu.CompilerParams(dimension_semantics=("parallel",)),
        name="moe_select",
    )(aff)


def _slot_index_kernel(slot_ref, idx_ref, *, cap):
    slot = slot_ref[0]
    n = slot.shape[1]
    srow = lax.broadcasted_iota(jnp.int32, (cap, LANE), 0)
    lane = lax.broadcasted_iota(jnp.int32, (cap, LANE), 1)
    acc = jnp.zeros((cap, LANE), jnp.int32)
    for cblk in range(n // LANE):
        s_c = slot[:, cblk * LANE:(cblk + 1) * LANE]
        acc = acc + jnp.where(srow == s_c, lane + (cblk * LANE + 1), 0)
    ones = jnp.ones((8, LANE), BF16)
    hi = _dot_nt(ones, lax.shift_right_logical(acc, 7).astype(F32).astype(BF16))
    lo = _dot_nt(ones, jnp.bitwise_and(acc, LANE - 1).astype(F32).astype(BF16))
    idx_ref[0] = (hi[0:1, :] * float(LANE) + lo[0:1, :]).astype(jnp.int32) - 1


SLOT_WIN = 2 * LANE


def _slot_index_win_kernel(base_ref, slot_ref, idx_ref, acc_s, *, cap):
    i = pl.program_id(0)
    n = slot_ref.shape[2]
    acc_s[...] = jnp.zeros(acc_s.shape, jnp.int32)
    srow = lax.broadcasted_iota(jnp.int32, (SLOT_WIN, LANE), 0)
    lane = lax.broadcasted_iota(jnp.int32, (SLOT_WIN, LANE), 1)
    for cblk in range(n // LANE):
        base = pl.multiple_of(base_ref[i, cblk], LANE)
        s_c = slot_ref[0, :, cblk * LANE:(cblk + 1) * LANE]
        rows = pl.ds(base, SLOT_WIN)
        acc_s[rows, :] = acc_s[rows, :] + jnp.where(srow + base == s_c, lane + (cblk * LANE + 1), 0)
    acc = acc_s[0:cap, :]
    ones = jnp.ones((8, LANE), BF16)
    hi = _dot_nt(ones, lax.shift_right_logical(acc, 7).astype(F32).astype(BF16))
    lo = _dot_nt(ones, jnp.bitwise_and(acc, LANE - 1).astype(F32).astype(BF16))
    idx_ref[0] = (hi[0:1, :] * float(LANE) + lo[0:1, :]).astype(jnp.int32) - 1


def moe_slot_index_windowed(slot, base, cap):
    bsz, ne, n = slot.shape
    idx = pl.pallas_call(
        functools.partial(_slot_index_win_kernel, cap=cap),
        grid_spec=pltpu.PrefetchScalarGridSpec(
            num_scalar_prefetch=1,
            grid=(bsz * ne,),
            in_specs=[pl.BlockSpec((1, 1, n), lambda i, base_ref: (i, 0, 0))],
            out_specs=pl.BlockSpec((1, 1, cap), lambda i, base_ref: (i, 0, 0)),
            scratch_shapes=[pltpu.VMEM((cap + LANE, LANE), jnp.int32)],
        ),
        out_shape=jax.ShapeDtypeStruct((bsz * ne, 1, cap), jnp.int32),
        compiler_params=pltpu.CompilerParams(dimension_semantics=("arbitrary",)),
        name="moe_slot_index",
    )(base, slot.reshape(bsz * ne, 1, n))
    return idx.reshape(bsz, ne, cap)


def moe_slot_index(slot, cap):
    bsz, ne, n = slot.shape
    idx = pl.pallas_call(
        functools.partial(_slot_index_kernel, cap=cap),
        grid=(bsz * ne,),
        in_specs=[pl.BlockSpec((1, 1, n), lambda i: (i, 0, 0))],
        out_specs=pl.BlockSpec((1, 1, cap), lambda i: (i, 0, 0)),
        out_shape=jax.ShapeDtypeStruct((bsz * ne, 1, cap), jnp.int32),
        compiler_params=pltpu.CompilerParams(dimension_semantics=("parallel",)),
        name="moe_slot_index",
    )(slot.reshape(bsz * ne, 1, n))
    return idx.reshape(bsz, ne, cap)


WIN_ALIGN = 16
WIN_FAST = 128


def _combine_kernel(ws_ref, slot_ref, aff_ref, *rest, win, final):
    ye_refs, (x_ref, gate_ref), o_ref = rest[:N_EXPERTS], rest[N_EXPERTS:N_EXPERTS + 2], rest[-1]
    b = pl.program_id(0)
    j = pl.program_id(1)
    srow = lax.broadcasted_iota(jnp.int32, (win, TT), 0)
    his, los = [], []
    for e in range(N_EXPERTS):
        sel = jnp.where(srow + ws_ref[b, e, j] == slot_ref[0, e:e + 1, :], aff_ref[0, e:e + 1, :], 0.0).T
        hi = sel.astype(BF16)
        his.append(hi)
        los.append((sel - hi.astype(F32)).astype(BF16))
    ye = jnp.concatenate([r[...] for r in ye_refs], axis=0)
    acc = _dot(jnp.concatenate(his, axis=1), ye) + _dot(jnp.concatenate(los, axis=1), ye)
    y = x_ref[0] + gate_ref[0, 0] * acc
    if final:
        g_ref = rest[N_EXPERTS + 2]
        y = y * lax.rsqrt(jnp.mean(y * y, axis=-1, keepdims=True) + 1e-6) * g_ref[...]
    o_ref[0] = y


def moe_combine(ws, slot, aff, ye, xa, gate, seg, tile0, win, final_g=None):
    bsz, ne, n = slot.shape
    nt = n // TT
    d = xa.shape[2]

    def ye_spec(e):
        return pl.BlockSpec((pl.Squeezed(), pl.Squeezed(), pl.Element(win), pl.Element(d)),
                            lambda i, j, ws_ref: (i, e, pl.multiple_of(ws_ref[i, e, j], WIN_ALIGN), 0))

    in_specs = ([pl.BlockSpec((1, ne, TT), lambda i, j, ws_ref: (i, 0, j)),
                 pl.BlockSpec((1, ne, TT), lambda i, j, ws_ref: (i, 0, j))]
                + [ye_spec(e) for e in range(ne)]
                + [pl.BlockSpec((1, TT, d), lambda i, j, ws_ref: (i, j + tile0, 0)),
                   pl.BlockSpec((1, 1, 1, d), lambda i, j, ws_ref: (i, seg, 0, 0))])
    args = (ws, slot, aff, *([ye] * ne), xa, gate)
    if final_g is None:
        out_idx, out_shape, aliases = (lambda i, j, ws_ref: (i, j + tile0, 0)), xa.shape, {3 + ne: 0}
    else:
        out_idx, out_shape, aliases = (lambda i, j, ws_ref: (i, j, 0)), (bsz, n, d), {}
        in_specs.append(pl.BlockSpec((1, d), lambda i, j, ws_ref: (0, 0)))
        args += (final_g.reshape(1, d),)
    return pl.pallas_call(
        functools.partial(_combine_kernel, win=win, final=final_g is not None),
        grid_spec=pltpu.PrefetchScalarGridSpec(
            num_scalar_prefetch=1,
            grid=(bsz, nt),
            in_specs=in_specs,
            out_specs=pl.BlockSpec((1, TT, d), out_idx),
        ),
        out_shape=jax.ShapeDtypeStruct(out_shape, F32),
        input_output_aliases=aliases,
        compiler_params=pltpu.CompilerParams(
            dimension_semantics=("parallel", "parallel"), vmem_limit_bytes=VMEM_LIMIT),
        name="moe_combine",
    )(*args)


def _expert_ffn_kernel(x_ref, w1_ref, w3_ref, w2_ref, o_ref, w1_s, w3_s, w2_s):
    @pl.when(jnp.logical_and(pl.program_id(1) == 0, pl.program_id(2) == 0))
    def _():
        w1_s[...] = w1_ref[0, 0].astype(BF16)
        w3_s[...] = w3_ref[0, 0].astype(BF16)
        w2_s[...] = w2_ref[0, 0].astype(BF16)

    x = x_ref[0, 0]
    a = _dot(x, w1_s[...])
    g = _dot(x, w3_s[...])
    o_ref[0, 0] = _dot((_silu(a) * g).astype(BF16), w2_s[...]).astype(BF16)


def expert_ffn(xe, w1, w3, w2, layer):
    bsz, ne, cap, d = xe.shape
    f = w1.shape[3]
    tm = cap // max(1, cap // 512)
    assert cap % tm == 0 and tm % WIN_ALIGN == 0
    return pl.pallas_call(
        _expert_ffn_kernel,
        grid=(ne, bsz, cap // tm),
        in_specs=[
            pl.BlockSpec((1, 1, tm, d), lambda e, i, j: (i, e, j, 0)),
            pl.BlockSpec((1, 1, d, f), lambda e, i, j: (layer, e, 0, 0)),
            pl.BlockSpec((1, 1, d, f), lambda e, i, j: (layer, e, 0, 0)),
            pl.BlockSpec((1, 1, f, d), lambda e, i, j: (layer, e, 0, 0)),
        ],
        out_specs=pl.BlockSpec((1, 1, tm, d), lambda e, i, j: (i, e, j, 0)),
        out_shape=jax.ShapeDtypeStruct((bsz, ne, cap, d), BF16),
        scratch_shapes=[pltpu.VMEM((d, f), BF16), pltpu.VMEM((d, f), BF16), pltpu.VMEM((f, d), BF16)],
        compiler_params=pltpu.CompilerParams(
            dimension_semantics=("arbitrary", "arbitrary", "arbitrary"), vmem_limit_bytes=VMEM_LIMIT),
        name="expert_ffn",
    )(xe, w1, w3, w2)


def _ab_in_layout(w_in, b_in):
    q, k, v, r, glr, z, xs, bm, cm, dt = _split_cols(
        jnp.concatenate([w_in, b_in[None]], axis=0),
        (GLA_QK, GLA_QK, GLA_V, GLA_V, 2 * GLA_GATE_RANK, SSD_INNER, SSD_INNER, SSD_BC, SSD_BC, 2 * SSD_HEADS))
    wide = jnp.concatenate([q, k, v, r, z, xs, bm, cm], axis=1)
    narrow = _pad_cols(jnp.concatenate([glr, dt], axis=1), LANE)
    return wide[:-1].astype(BF16), wide[-1], narrow[:-1].astype(BF16), narrow[-1]


def _gla_gate_params(w_gate2, b_gate2):
    wg = jnp.zeros((2, LANE, GLA_QK), F32)
    for d in range(2):
        wg = wg.at[d, d * GLA_GATE_RANK:(d + 1) * GLA_GATE_RANK, :].set(w_gate2[d])
    return wg.astype(BF16), b_gate2.reshape(2, 1, GLA_QK)


def _ssd_params(dt_bias, a_log):
    dtb = jnp.zeros((2, 1, LANE), F32)
    nega = jnp.zeros((2, 1, LANE), F32)
    e = np.zeros((2, LANE, SSD_INNER), np.float32)
    for d in range(2):
        c0 = SM_DT + d * SSD_HEADS
        dtb = dtb.at[d, 0, c0:c0 + SSD_HEADS].set(dt_bias[d])
        nega = nega.at[d, 0, c0:c0 + SSD_HEADS].set(-jnp.exp(a_log[d]))
        for h in range(SSD_HEADS):
            e[d, c0 + h, h * SSD_HEADDIM:(h + 1) * SSD_HEADDIM] = 1.0
    return dtb, nega, jnp.asarray(e, BF16)


def _gdn_params(dt_bias, a_log):
    n = 2 * GDN_HEADS
    dtb = jnp.zeros((1, LANE), F32).at[0, 0:n].set(dt_bias.reshape(n))
    nega = jnp.zeros((1, LANE), F32).at[0, 0:n].set(-jnp.exp(a_log.reshape(n)))
    return dtb, nega


def _moe_route(aff_all, tile0, ntiles):
    bsz = aff_all.shape[0]
    n = ntiles * TT
    cap = n * EC_CAPACITY // N_EXPERTS
    aff = aff_all[:, :, tile0 * TT:tile0 * TT + n]
    slot = moe_select(aff, cap)
    cnt128 = jnp.sum((slot >= 0).reshape(bsz, N_EXPERTS, n // LANE, LANE), axis=-1, dtype=jnp.int32)
    start128 = jnp.cumsum(cnt128, axis=-1) - cnt128
    if cap >= SLOT_WIN:
        base = jnp.minimum(start128 // LANE * LANE, cap - LANE).reshape(bsz * N_EXPERTS, n // LANE)
        idx = moe_slot_index_windowed(slot, base, cap)
    else:
        idx = moe_slot_index(slot, cap)
    per_tile = TT // LANE
    counts = jnp.sum(cnt128.reshape(bsz, N_EXPERTS, ntiles, per_tile), axis=-1)
    return dict(aff=aff, slot=slot, idx=idx + tile0 * TT, cap=cap, counts=counts, starts=start128[:, :, ::per_tile])


def moe_layer(i, xa, h_all, aff_all, mods, w1, w3, w2, segments, final_g=None):
    bsz, _, d = xa.shape
    routes = [_moe_route(aff_all, tile0, ntiles) for _, tile0, ntiles in segments]
    cap_all = sum(r["cap"] for r in routes)
    idx = jnp.concatenate([r["idx"] for r in routes], axis=2).reshape(bsz, N_EXPERTS * cap_all)
    xe = jnp.take_along_axis(h_all, idx[..., None], axis=1, mode="promise_in_bounds")
    ye = expert_ffn(xe.reshape(bsz, N_EXPERTS, cap_all, d), w1, w3, w2, i)
    row0 = 0
    for (seg, tile0, _), r in zip(segments, routes):
        cap = r["cap"]
        aligned = r["starts"] // WIN_ALIGN * WIN_ALIGN

        slot = r["slot"] if row0 == 0 else jnp.where(r["slot"] >= 0, r["slot"] + row0, -1)

        def run(win, r=r, cap=cap, aligned=aligned, row0=row0, seg=seg, tile0=tile0, xa=xa, slot=slot):
            ws = jnp.minimum(aligned, cap - win) + row0
            return moe_combine(ws, slot, r["aff"], ye, xa, mods[:, :, 5], seg, tile0, win, final_g)

        win_fast, win_full = min(cap, WIN_FAST), min(cap, TT + WIN_ALIGN)
        if win_fast == win_full:
            xa = run(win_full)
        else:
            overflow = jnp.any(r["starts"] + r["counts"] - jnp.minimum(aligned, cap - win_fast) > win_fast)
            xa = lax.cond(overflow, functools.partial(run, win_full), functools.partial(run, win_fast))
        row0 += cap
    return xa


def layer_mixer(i, j, xa, mods, p, last):
    sh1, sc1, g1 = (mods[:, :, s] for s in range(3))
    route = (p["norm2_g"][i], mods[:, :, 3], mods[:, :, 4], _pad_cols(p["moe_router"][i], LANE).astype(BF16))
    if i % 2 == 0:
        proj, small = norm_proj(xa, p["norm1_g"][i], sh1, sc1, *_ab_in_layout(p["ab_w_in"][j], p["ab_b_in"][j]),
                                AB_WIDE // 2)
        xbc = ab_prep(proj, p["ssd_conv_w"][j], p["ssd_conv_b"][j])
        wg, bg = _gla_gate_params(p["gla_w_gate2"][j], p["gla_b_gate2"][j])
        o_gla = gla_scan(proj, small, wg, bg)
        y_ssd = ssd_scan(xbc, small, *_ssd_params(p["ssd_dt_bias"][j], p["ssd_a_log"][j]))
        return ab_out(o_gla, y_ssd, proj, xbc, p["gla_norm_g"][j], jnp.repeat(p["ssd_d"][j], SSD_HEADDIM),
                      p["ssd_norm_g"][j], p["ab_w_out"][j].astype(BF16), p["ab_b_out"][j], xa, g1, route)
    w_in, b_in = p["cd_w_in"][j], p["cd_b_in"][j]
    proj, small = norm_proj(xa, p["norm1_g"][i], sh1, sc1, w_in[:, :CD_WIDE].astype(BF16), b_in[:CD_WIDE],
                            _pad_cols(w_in[:, CD_WIDE:], LANE).astype(BF16), _pad_cols(b_in[CD_WIDE:], LANE),
                            CD_WIDE // 3)
    qkv = cd_prep(proj, p["gdn_conv_w"][j])
    o_gdn = gdn_scan(qkv, small, *_gdn_params(p["gdn_dt_bias"][j], p["gdn_a_log"][j]))
    return cd_out(o_gdn, proj, p["conf_dw_w"][j], p["conf_dw_b"][j], p["conf_ln_g"][j], p["conf_ln_b"][j],
                  p["gdn_norm_g"][j], p["cd_w_out"][j].astype(BF16), p["cd_b_out"][j], xa, g1, route)


def kernel(x, c, ctx, c_ctx, mod_w, mod_b, norm1_g, norm2_g, ab_w_in, ab_b_in, ab_w_out, ab_b_out, gla_w_gate2, gla_b_gate2, gla_norm_g, ssd_conv_w, ssd_conv_b, ssd_dt_bias, ssd_a_log, ssd_d, ssd_norm_g, cd_w_in, cd_b_in, cd_w_out, cd_b_out, conf_dw_w, conf_dw_b, conf_ln_g, conf_ln_b, gdn_conv_w, gdn_a_log, gdn_dt_bias, gdn_norm_g, moe_router, moe_w1, moe_w3, moe_w2, final_norm_g):
    p = dict(locals())
    bsz, seq, d = x.shape
    assert ctx.shape[1] == TT and seq % TT == 0
    depth = mod_w.shape[0]
    xa = jnp.concatenate([ctx, x], axis=1)
    w1, w3, w2 = moe_w1, moe_w3, moe_w2
    cond = jnp.concatenate([c, c_ctx[None]], axis=0)
    for i in range(depth):
        last = i == depth - 1
        mod = mod_proj(cond, mod_w[i].astype(BF16), mod_b[i])
        mods = jnp.stack([jnp.broadcast_to(mod[bsz], (bsz, 6 * d)), mod[:bsz]], axis=1).reshape(bsz, 2, 6, 1, d)
        xa, h_all, aff_all = layer_mixer(i, i // 2, xa, mods, p, last)
        latent, context = (1, 1, seq // TT), (0, 0, 1)
        if last:
            return moe_layer(i, xa, h_all, aff_all, mods, w1, w3, w2, [latent], final_norm_g)
        xa = moe_layer(i, xa, h_all, aff_all, mods, w1, w3, w2, [latent, context])
```

```python
import functools

import jax
import jax.numpy as jnp
import numpy as np
from jax import lax
from jax.experimental import pallas as pl
from jax.experimental.pallas import tpu as pltpu

F32 = jnp.float32
BF16 = jnp.bfloat16

D_MODEL = 1024
GRID_W = 64
CHUNK = 64
GLA_HEADS, GLA_DK, GLA_DV, GLA_GATE_RANK, GLA_GATE_TAU = 4, 128, 256, 16, 16.0
SSD_HEADS, SSD_HEADDIM, SSD_STATE, SSD_GROUPS = 16, 64, 128, 2
CONF_CH, CONF_KERNEL = D_MODEL, 31
GDN_HEADS, GDN_DK, GDN_DV = 8, 128, 128
N_EXPERTS, EC_CAPACITY = 16, 2

GLA_QK = GLA_HEADS * GLA_DK
GLA_V = GLA_HEADS * GLA_DV
SSD_INNER = SSD_HEADS * SSD_HEADDIM
SSD_BC = SSD_GROUPS * SSD_STATE
SSD_HPG = SSD_HEADS // SSD_GROUPS
GDN_QK = GDN_HEADS * GDN_DK
GDN_V = GDN_HEADS * GDN_DV

LANE = 128
SUBLANES = 8
TT = 256
CPT = TT // CHUNK
VMEM_LIMIT = 48 * 1024 * 1024

AB_Q, AB_K, AB_V, AB_R, AB_Z, AB_XS, AB_BM, AB_CM, AB_WIDE = 0, 512, 1024, 2048, 3072, 4096, 5120, 5376, 5632
SM_DT = 2 * GLA_GATE_RANK
CD_WIDE = 6 * D_MODEL


def _split_cols(a, sizes):
    return jnp.split(a, np.cumsum(sizes)[:-1].tolist(), axis=-1)


def _pad_cols(a, n):
    return jnp.pad(a, [(0, 0)] * (a.ndim - 1) + [(0, n - a.shape[-1])])


def _dot(a, b):
    return jnp.dot(a, b, preferred_element_type=F32)


def _dot_nt(a, b):
    return lax.dot_general(a, b, (((1,), (1,)), ((), ())), preferred_element_type=F32)


def _split3(x):
    hi = x.astype(BF16)
    r = x - hi.astype(F32)
    mid = r.astype(BF16)
    lo = (r - mid.astype(F32)).astype(BF16)
    return hi, mid, lo


def _sel_dot(m, x):
    hi, mid, lo = _split3(x)
    return _dot(m, hi) + _dot(m, mid) + _dot(m, lo)


def _softplus(x):
    return jnp.maximum(x, 0.0) + jnp.log(1.0 + jnp.exp(-jnp.abs(x)))


def _silu(x):
    return x * jax.nn.sigmoid(x)


def _chunk_cum_mask(is_fwd, n):
    r = lax.broadcasted_iota(jnp.int32, (n, n), 0)
    c = lax.broadcasted_iota(jnp.int32, (n, n), 1)
    same = lax.shift_right_logical(r, 6) == lax.shift_right_logical(c, 6)
    lo = jnp.where(is_fwd, c, r)
    hi = jnp.where(is_fwd, r, c)
    return jnp.where(jnp.logical_and(same, lo <= hi), 1.0, 0.0).astype(BF16)


def _chunk_totals(cum, is_fwd):
    rows = [jnp.where(is_fwd, cum[(ci + 1) * CHUNK - 1:(ci + 1) * CHUNK, :], cum[ci * CHUNK:ci * CHUNK + 1, :])
            for ci in range(CPT)]
    return jnp.concatenate([jnp.broadcast_to(row, (CHUNK, cum.shape[1])) for row in rows], axis=0)


def _causal_mask(is_fwd, n):
    r = lax.broadcasted_iota(jnp.int32, (n, n), 0)
    c = lax.broadcasted_iota(jnp.int32, (n, n), 1)
    return jnp.where(is_fwd, c, r) <= jnp.where(is_fwd, r, c)


def _scan_tile(d, j, nt):
    return jnp.where(d == 0, j, jnp.where(j == 0, 0, nt - j))


def _mod_proj_kernel(c_ref, w_ref, b_ref, o_ref):
    o_ref[...] = _dot(_silu(c_ref[...]).astype(BF16), w_ref[...]) + b_ref[...]


def mod_proj(cond, w, b):
    r, d = cond.shape
    n = w.shape[1]
    tn = 6 * LANE * 2
    return pl.pallas_call(
        _mod_proj_kernel,
        grid=(n // tn,),
        in_specs=[pl.BlockSpec((r, d), lambda k: (0, 0)),
                  pl.BlockSpec((d, tn), lambda k: (0, k)),
                  pl.BlockSpec((1, tn), lambda k: (0, k))],
        out_specs=pl.BlockSpec((r, tn), lambda k: (0, k)),
        out_shape=jax.ShapeDtypeStruct((r, n), F32),
        compiler_params=pltpu.CompilerParams(dimension_semantics=("parallel",)),
        name="mod_proj",
    )(cond, w, b.reshape(1, n))


NP_TM = 3 * TT


def _norm_proj_kernel(x_ref, g_ref, sh_ref, sc_ref, w_ref, b_ref, ws_ref, bs_ref, o_ref, os_ref):
    x = x_ref[0]
    row = lax.broadcasted_iota(jnp.int32, (NP_TM, 1), 0) + pl.program_id(2) * NP_TM
    is_ctx = row < TT
    scale = jnp.where(is_ctx, sc_ref[0, 0], sc_ref[0, 1])
    shift = jnp.where(is_ctx, sh_ref[0, 0], sh_ref[0, 1])
    ms = jnp.mean(x * x, axis=-1, keepdims=True)
    h = (x * lax.rsqrt(ms + 1e-6) * g_ref[...] * (1.0 + scale) + shift).astype(BF16)
    o_ref[0] = (_dot(h, w_ref[...]) + b_ref[...]).astype(BF16)
    os_ref[0, 0] = _dot(h, ws_ref[...]) + bs_ref[...]


def norm_proj(x, g, shift, scale, w, b, w_small, b_small, tn):
    bsz, t, d = x.shape
    n = w.shape[1]
    assert t % NP_TM == 0 and n % tn == 0
    wide, narrow = pl.pallas_call(
        _norm_proj_kernel,
        grid=(n // tn, bsz, t // NP_TM),
        in_specs=[
            pl.BlockSpec((1, NP_TM, d), lambda k, i, j: (i, j, 0)),
            pl.BlockSpec((1, d), lambda k, i, j: (0, 0)),
            pl.BlockSpec((1, 2, 1, d), lambda k, i, j: (i, 0, 0, 0)),
            pl.BlockSpec((1, 2, 1, d), lambda k, i, j: (i, 0, 0, 0)),
            pl.BlockSpec((d, tn), lambda k, i, j: (0, k)),
            pl.BlockSpec((1, tn), lambda k, i, j: (0, k)),
            pl.BlockSpec((d, LANE), lambda k, i, j: (0, 0)),
            pl.BlockSpec((1, LANE), lambda k, i, j: (0, 0)),
        ],
        out_specs=[pl.BlockSpec((1, NP_TM, tn), lambda k, i, j: (i, j, k)),
                   pl.BlockSpec((1, 1, NP_TM, LANE), lambda k, i, j: (k, i, j, 0))],
        out_shape=[jax.ShapeDtypeStruct((bsz, t, n), BF16),
                   jax.ShapeDtypeStruct((n // tn, bsz, t, LANE), F32)],
        compiler_params=pltpu.CompilerParams(
            dimension_semantics=("parallel", "parallel", "parallel"), vmem_limit_bytes=VMEM_LIMIT),
        name="norm_proj",
    )(x, g.reshape(1, d), shift, scale, w, b.reshape(1, n), w_small, b_small.reshape(1, LANE))
    return wide, narrow[0]


HALO = 16


def _conv3_piece(c_ref, l_ref, r_ref, w, b, left_ok, right_ok):
    x = c_ref[0].astype(F32)
    n = x.shape[0]
    row = lax.broadcasted_iota(jnp.int32, x.shape, 0)
    prev_row = jnp.where(left_ok, l_ref[0, HALO - 1:HALO, :].astype(F32), 0.0)
    next_row = jnp.where(right_ok, r_ref[0, 0:1, :].astype(F32), 0.0)
    x_prev = jnp.where(row == 0, prev_row, pltpu.roll(x, 1, 0))
    x_next = jnp.where(row == n - 1, next_row, pltpu.roll(x, n - 1, 0))
    return _silu(w[0:1, :] * x_prev + w[1:2, :] * x + w[2:3, :] * x_next + b)


def _ab_prep_kernel(xs_ref, bm_ref, cm_ref, xsl_ref, bml_ref, cml_ref, xsr_ref, bmr_ref, cmr_ref,
                    w_ref, b_ref, o_ref):
    j = pl.program_id(1)
    nt = pl.num_programs(1)
    left_ok = j >= 2
    right_ok = jnp.logical_and(j >= 1, j < nt - 1)
    w = w_ref[...]
    b = b_ref[...]
    o_ref[0, :, 0:SSD_INNER] = _conv3_piece(xs_ref, xsl_ref, xsr_ref, w[:, 0:SSD_INNER],
                                            b[:, 0:SSD_INNER], left_ok, right_ok).astype(BF16)
    c0, c1 = SSD_INNER, SSD_INNER + SSD_BC
    o_ref[0, :, c0:c1] = _conv3_piece(bm_ref, bml_ref, bmr_ref, w[:, c0:c1], b[:, c0:c1],
                                      left_ok, right_ok).astype(BF16)
    c0, c1 = c1, c1 + SSD_BC
    o_ref[0, :, c0:c1] = _conv3_piece(cm_ref, cml_ref, cmr_ref, w[:, c0:c1], b[:, c0:c1],
                                      left_ok, right_ok).astype(BF16)


def ab_prep(proj, conv_w, conv_b):
    bsz, t, _ = proj.shape
    nt = t // TT
    rb = TT // HALO
    nrb = t // HALO
    cw = SSD_INNER + 2 * SSD_BC

    def cur(width, col):
        return pl.BlockSpec((1, TT, width), lambda i, j: (i, j, col // width))

    def left(width, col):
        return pl.BlockSpec((1, HALO, width), lambda i, j: (i, jnp.maximum(j * rb - 1, 0), col // width))

    def right(width, col):
        return pl.BlockSpec((1, HALO, width), lambda i, j: (i, jnp.minimum((j + 1) * rb, nrb - 1), col // width))

    pieces = ((SSD_INNER, AB_XS), (SSD_BC, AB_BM), (SSD_BC, AB_CM))
    return pl.pallas_call(
        _ab_prep_kernel,
        grid=(bsz, nt),
        in_specs=[cur(*p) for p in pieces] + [left(*p) for p in pieces] + [right(*p) for p in pieces] + [
            pl.BlockSpec((3, cw), lambda i, j: (0, 0)),
            pl.BlockSpec((1, cw), lambda i, j: (0, 0)),
        ],
        out_specs=pl.BlockSpec((1, TT, cw), lambda i, j: (i, j, 0)),
        out_shape=jax.ShapeDtypeStruct((bsz, t, cw), BF16),
        compiler_params=pltpu.CompilerParams(dimension_semantics=("parallel", "parallel")),
        name="ab_prep",
    )(*([proj] * 9), conv_w, conv_b.reshape(1, cw))


def _gla_kernel(q_ref, k_ref, v_ref, sm_ref, wg_ref, bg_ref, o_ref, qg_s, egl_s, oi_s, u_s, st_s):
    d = pl.program_id(1)
    j = pl.program_id(2)
    is_fwd = d == 0

    @pl.when(j == 0)
    def _():
        st_s[...] = jnp.zeros_like(st_s)

    gz = _dot(sm_ref[0].astype(BF16), wg_ref[0]) + bg_ref[0]
    logg = (jnp.minimum(gz, 0.0) - jnp.log(1.0 + jnp.exp(-jnp.abs(gz)))) * (1.0 / GLA_GATE_TAU)
    gc = _sel_dot(_chunk_cum_mask(is_fwd, TT), logg)
    gl = _chunk_totals(gc, is_fwd)
    q = q_ref[0].astype(F32) * (GLA_DK ** -0.5)
    k = k_ref[0].astype(F32)
    qg = (q * jnp.exp(gc)).astype(BF16)
    kn = (k * jnp.exp(-gc)).astype(BF16)
    kd = (k * jnp.exp(gl - gc)).astype(BF16)
    qg_s[...] = qg
    egl_s[...] = jnp.exp(gl)
    causal = _causal_mask(is_fwd, CHUNK)

    pairs = [(h, ci) for h in range(GLA_HEADS) for ci in range(CPT)]
    rows_of = lambda ci: slice(ci * CHUNK, (ci + 1) * CHUNK)
    kcols = lambda h: slice(h * GLA_DK, (h + 1) * GLA_DK)
    vcols = lambda h: slice(h * GLA_DV, (h + 1) * GLA_DV)
    vs = [v_ref[0, rows_of(ci), vcols(h)] for h, ci in pairs]
    atts = [jnp.where(causal, _dot_nt(qg[rows_of(ci), kcols(h)], kn[rows_of(ci), kcols(h)]), 0.0).astype(BF16)
            for h, ci in pairs]
    for n, (h, ci) in enumerate(pairs):
        oi_s[rows_of(ci), vcols(h)] = _dot(atts[n], vs[n])
        u_s[n] = _dot(vs[n].astype(F32).T.astype(BF16), kd[rows_of(ci), kcols(h)])

    for ci in range(CPT):
        cidx = jnp.where(is_fwd, ci, CPT - 1 - ci)
        off = pl.multiple_of(cidx * CHUNK, CHUNK)
        rows = pl.ds(off, CHUNK)
        for h in range(GLA_HEADS):
            st = st_s[h]
            o_ref[0, 0, rows, vcols(h)] = (oi_s[rows, vcols(h)]
                                           + _dot_nt(qg_s[rows, kcols(h)], st.astype(BF16))).astype(BF16)
            st_s[h] = st * egl_s[pl.ds(off, 1), kcols(h)] + u_s[h * CPT + cidx]


def gla_scan(proj, small, wg, bg):
    bsz, t, _ = proj.shape
    nt = t // TT
    tile = lambda d, j: _scan_tile(d, j, nt)
    return pl.pallas_call(
        _gla_kernel,
        grid=(bsz, 2, nt),
        in_specs=[
            pl.BlockSpec((1, TT, GLA_QK), lambda i, d, j: (i, tile(d, j), AB_Q // GLA_QK)),
            pl.BlockSpec((1, TT, GLA_QK), lambda i, d, j: (i, tile(d, j), AB_K // GLA_QK)),
            pl.BlockSpec((1, TT, GLA_V), lambda i, d, j: (i, tile(d, j), AB_V // GLA_V)),
            pl.BlockSpec((1, TT, LANE), lambda i, d, j: (i, tile(d, j), 0)),
            pl.BlockSpec((1, LANE, GLA_QK), lambda i, d, j: (d, 0, 0)),
            pl.BlockSpec((1, 1, GLA_QK), lambda i, d, j: (d, 0, 0)),
        ],
        out_specs=pl.BlockSpec((1, 1, TT, GLA_V), lambda i, d, j: (d, i, tile(d, j), 0)),
        out_shape=jax.ShapeDtypeStruct((2, bsz, t, GLA_V), BF16),
        scratch_shapes=[pltpu.VMEM((TT, GLA_QK), BF16), pltpu.VMEM((TT, GLA_QK), F32),
                        pltpu.VMEM((TT, GLA_V), F32), pltpu.VMEM((GLA_HEADS * CPT, GLA_DV, GLA_DK), F32),
                        pltpu.VMEM((GLA_HEADS, GLA_DV, GLA_DK), F32)],
        compiler_params=pltpu.CompilerParams(
            dimension_semantics=("parallel", "parallel", "arbitrary")),
        name="gla_scan",
    )(proj, proj, proj, small, wg, bg)


def _dot_sel2(x, e):
    hi = x.astype(BF16)
    return _dot(hi, e) + _dot((x - hi.astype(F32)).astype(BF16), e)


def _ssd_kernel(xs_ref, bm_ref, cm_ref, sm_ref, dtb_ref, nega_ref, e_ref, o_ref,
                v_s, vw_s, cdec_s, dec_s, ah_s, st_s):
    d = pl.program_id(1)
    j = pl.program_id(2)
    is_fwd = d == 0
    gw = SSD_HPG * SSD_HEADDIM

    @pl.when(j == 0)
    def _():
        st_s[...] = jnp.zeros_like(st_s)

    dt = _softplus(sm_ref[0] + dtb_ref[0])
    la = dt * nega_ref[0]
    acum = _sel_dot(_chunk_cum_mask(is_fwd, TT), la)
    atot = _chunk_totals(acum, is_fwd)
    e = e_ref[0]
    v = xs_ref[0].astype(F32) * _dot_sel2(dt, e)
    v_s[...] = v.astype(BF16)
    vw_s[...] = (v * _dot_sel2(jnp.exp(atot - acum), e)).astype(BF16)
    cdec_s[...] = _dot_sel2(jnp.exp(acum), e)
    etot = jnp.exp(atot)
    tot_rows = [etot[ci * CHUNK:ci * CHUNK + 1, :] for ci in range(CPT)]
    dec_s[...] = _dot_sel2(jnp.concatenate(tot_rows + [jnp.zeros((SUBLANES - CPT, LANE), F32)], axis=0), e)
    ah_s[...] = pltpu.roll(acum, LANE - SM_DT - d * SSD_HEADS, 1)
    causal = _causal_mask(is_fwd, CHUNK)

    for ci in range(CPT):
        cidx = jnp.where(is_fwd, ci, CPT - 1 - ci)
        off = pl.multiple_of(cidx * CHUNK, CHUNK)
        rows = pl.ds(off, CHUNK)
        ah = ah_s[rows, :]
        aht = ah.T
        for g in range(SSD_GROUPS):
            gc = slice(g * gw, (g + 1) * gw)
            nc = slice(g * SSD_STATE, (g + 1) * SSD_STATE)
            bm = bm_ref[0, rows, nc]
            cm = cm_ref[0, rows, nc]
            st = st_s[g]
            cb = _dot_nt(cm, bm)
            y_inter = _dot(cm, st.astype(BF16)) * cdec_s[rows, gc]
            v_c = v_s[rows, gc]
            ys = []
            for hh in range(SSD_HPG):
                h = g * SSD_HPG + hh
                seg = jnp.exp(jnp.where(causal, ah[:, h:h + 1] - aht[h:h + 1, :], -1e30))
                ys.append(_dot((seg * cb).astype(BF16), v_c[:, hh * SSD_HEADDIM:(hh + 1) * SSD_HEADDIM]))
            o_ref[0, 0, rows, gc] = (jnp.concatenate(ys, axis=1) + y_inter).astype(BF16)
            st_s[g] = st * dec_s[pl.ds(cidx, 1), gc] + _dot(bm.astype(F32).T.astype(BF16), vw_s[rows, gc])


def ssd_scan(xbc, small, dtb, nega, e):
    bsz, t, _ = xbc.shape
    nt = t // TT
    gw = SSD_HPG * SSD_HEADDIM
    tile = lambda d, j: _scan_tile(d, j, nt)
    return pl.pallas_call(
        _ssd_kernel,
        grid=(bsz, 2, nt),
        in_specs=[
            pl.BlockSpec((1, TT, SSD_INNER), lambda i, d, j: (i, tile(d, j), 0)),
            pl.BlockSpec((1, TT, SSD_BC), lambda i, d, j: (i, tile(d, j), SSD_INNER // SSD_BC)),
            pl.BlockSpec((1, TT, SSD_BC), lambda i, d, j: (i, tile(d, j), SSD_INNER // SSD_BC + 1)),
            pl.BlockSpec((1, TT, LANE), lambda i, d, j: (i, tile(d, j), 0)),
            pl.BlockSpec((1, 1, LANE), lambda i, d, j: (d, 0, 0)),
            pl.BlockSpec((1, 1, LANE), lambda i, d, j: (d, 0, 0)),
            pl.BlockSpec((1, LANE, SSD_INNER), lambda i, d, j: (d, 0, 0)),
        ],
        out_specs=pl.BlockSpec((1, 1, TT, SSD_INNER), lambda i, d, j: (d, i, tile(d, j), 0)),
        out_shape=jax.ShapeDtypeStruct((2, bsz, t, SSD_INNER), BF16),
        scratch_shapes=[pltpu.VMEM((TT, SSD_INNER), BF16), pltpu.VMEM((TT, SSD_INNER), BF16),
                        pltpu.VMEM((TT, SSD_INNER), F32), pltpu.VMEM((SUBLANES, SSD_INNER), F32),
                        pltpu.VMEM((TT, LANE), F32), pltpu.VMEM((SSD_GROUPS, SSD_STATE, gw), F32)],
        compiler_params=pltpu.CompilerParams(
            dimension_semantics=("parallel", "parallel", "arbitrary")),
        name="ssd_scan",
    )(xbc, xbc, xbc, small, dtb, nega, e)


def _group_rmsnorm(x, width):
    parts = []
    for s in range(x.shape[1] // width):
        seg = x[:, s * width:(s + 1) * width]
        parts.append(seg * lax.rsqrt(jnp.mean(seg * seg, axis=-1, keepdims=True) + 1e-6))
    return jnp.concatenate(parts, axis=1)


def _ab_out_kernel(of_ref, ob_ref, yf_ref, yb_ref, r_ref, z_ref, xs_ref, gg_ref, dv_ref, sg_ref,
                   w_ref, b_ref, x_ref, gate_ref, g2_ref, sh2_ref, sc2_ref, rw_ref, o_ref, h_ref, aff_ref):
    o = (_group_rmsnorm(of_ref[0, 0].astype(F32) + ob_ref[0, 0].astype(F32), GLA_DV) * gg_ref[...]
         * _silu(r_ref[0].astype(F32)))
    y = ((yf_ref[0, 0].astype(F32) + yb_ref[0, 0].astype(F32) + dv_ref[...] * xs_ref[0].astype(F32))
         * _silu(z_ref[0].astype(F32)))
    y = _group_rmsnorm(y, SSD_INNER // SSD_GROUPS) * sg_ref[...]
    m = _dot(o.astype(BF16), w_ref[0:GLA_V, :]) + _dot(y.astype(BF16), w_ref[GLA_V:, :]) + b_ref[...]
    x_new = x_ref[0] + gate_ref[0, 0] * m
    o_ref[0] = x_new
    _route_tokens(x_new, g2_ref, sh2_ref, sc2_ref, rw_ref, h_ref, aff_ref)


def ab_out(o_gla, y_ssd, proj, xbc, gla_g, d_vec, ssd_g, w, b, x, gate, route):
    bsz, t, d = x.shape
    seg = lambda i, j: (i, jnp.minimum(j, 1), 0, 0)
    row = lambda width: pl.BlockSpec((1, width), lambda i, j: (0, 0))
    r_ins, r_outs, r_shapes = _route_specs(bsz, t, d, seg)
    g2, sh2, sc2, rw = route
    return pl.pallas_call(
        _ab_out_kernel,
        grid=(bsz, t // TT),
        in_specs=[
            pl.BlockSpec((1, 1, TT, GLA_V), lambda i, j: (0, i, j, 0)),
            pl.BlockSpec((1, 1, TT, GLA_V), lambda i, j: (1, i, j, 0)),
            pl.BlockSpec((1, 1, TT, SSD_INNER), lambda i, j: (0, i, j, 0)),
            pl.BlockSpec((1, 1, TT, SSD_INNER), lambda i, j: (1, i, j, 0)),
            pl.BlockSpec((1, TT, GLA_V), lambda i, j: (i, j, AB_R // GLA_V)),
            pl.BlockSpec((1, TT, SSD_INNER), lambda i, j: (i, j, AB_Z // SSD_INNER)),
            pl.BlockSpec((1, TT, SSD_INNER), lambda i, j: (i, j, 0)),
            row(GLA_V), row(SSD_INNER), row(SSD_INNER),
            pl.BlockSpec((GLA_V + SSD_INNER, d), lambda i, j: (0, 0)),
            row(d),
            pl.BlockSpec((1, TT, d), lambda i, j: (i, j, 0)),
            pl.BlockSpec((1, 1, 1, d), seg),
        ] + r_ins,
        out_specs=[pl.BlockSpec((1, TT, d), lambda i, j: (i, j, 0))] + r_outs,
        out_shape=[jax.ShapeDtypeStruct((bsz, t, d), F32)] + r_shapes,
        compiler_params=pltpu.CompilerParams(
            dimension_semantics=("parallel", "parallel"), vmem_limit_bytes=VMEM_LIMIT),
        name="ab_out",
    )(o_gla, o_gla, y_ssd, y_ssd, proj, proj, xbc, gla_g.reshape(1, -1), d_vec.reshape(1, -1),
      ssd_g.reshape(1, -1), w, b.reshape(1, d), x, gate, g2.reshape(1, d), sh2, sc2, rw)


def _cd_prep_kernel(q_ref, k_ref, v_ref, ql_ref, kl_ref, vl_ref, qr_ref, kr_ref, vr_ref, w_ref, o_ref):
    j = pl.program_id(1)
    nt = pl.num_programs(1)
    left_ok = j >= 2
    right_ok = jnp.logical_and(j >= 1, j < nt - 1)
    w = w_ref[...]
    srcs = ((q_ref, ql_ref, qr_ref, GDN_DK ** -0.5), (k_ref, kl_ref, kr_ref, 1.0), (v_ref, vl_ref, vr_ref, None))
    for s, (c_ref, l_ref, r_ref, scale) in enumerate(srcs):
        c0 = s * GDN_QK
        y = _conv3_piece(c_ref, l_ref, r_ref, w[:, c0:c0 + GDN_QK], 0.0, left_ok, right_ok)
        if scale is None:
            o_ref[0, :, c0:c0 + GDN_QK] = y.astype(BF16)
            continue
        for h in range(GDN_HEADS):
            seg = y[:, h * GDN_DK:(h + 1) * GDN_DK]
            inv = lax.rsqrt(jnp.sum(seg * seg, axis=-1, keepdims=True) + 1e-6) * scale
            o_ref[0, :, c0 + h * GDN_DK:c0 + (h + 1) * GDN_DK] = (seg * inv).astype(BF16)


def cd_prep(proj, conv_w):
    bsz, t, _ = proj.shape
    nt = t // TT
    rb = TT // HALO
    nrb = t // HALO
    width = GDN_QK
    cols = (2, 3, 4)

    cur = lambda cb: pl.BlockSpec((1, TT, width), lambda i, j: (i, j, cb))
    left = lambda cb: pl.BlockSpec((1, HALO, width), lambda i, j: (i, jnp.maximum(j * rb - 1, 0), cb))
    right = lambda cb: pl.BlockSpec((1, HALO, width), lambda i, j: (i, jnp.minimum((j + 1) * rb, nrb - 1), cb))
    return pl.pallas_call(
        _cd_prep_kernel,
        grid=(bsz, nt),
        in_specs=[cur(cb) for cb in cols] + [left(cb) for cb in cols] + [right(cb) for cb in cols] + [
            pl.BlockSpec((3, 3 * width), lambda i, j: (0, 0))],
        out_specs=pl.BlockSpec((1, TT, 3 * width), lambda i, j: (i, j, 0)),
        out_shape=jax.ShapeDtypeStruct((bsz, t, 3 * width), BF16),
        compiler_params=pltpu.CompilerParams(dimension_semantics=("parallel", "parallel")),
        name="cd_prep",
    )(*([proj] * 9), conv_w)


GDN_HB = 8


def _mm2(a, b):
    return _dot(a.astype(BF16), b.astype(BF16))


def _unit_tri_inverse(mats, b16, b32, eye):
    each = lambda f, *ls: [f(*xs) for xs in zip(*ls)]
    d16 = each(lambda a: jnp.where(b16, a, 0.0), mats)
    d2 = each(lambda x: _mm2(x, x), d16)
    d4 = each(lambda x: _mm2(x, x), d2)
    d8 = each(lambda x: _mm2(x, x), d4)
    t = each(lambda x: eye - x, d16)
    for p in (d2, d4, d8):
        t = each(lambda x, y: x + _mm2(x, y), t, p)
    off32 = jnp.logical_and(b32, jnp.logical_not(b16))
    for sel in (off32, jnp.logical_not(b32)):
        a_off = each(lambda a: jnp.where(sel, a, 0.0), mats)
        inner = each(_mm2, a_off, t)
        t = each(lambda x, y: x - _mm2(x, y), t, inner)
    return t


def _gdn_kernel(q_ref, k_ref, v_ref, sm_ref, dtb_ref, nega_ref, o_ref,
                dec_s, n_s, p_s, oc_s, qp_s, st_s):
    d = pl.program_id(2)
    j = pl.program_id(3)
    is_fwd = d == 0

    @pl.when(j == 0)
    def _():
        st_s[...] = jnp.zeros_like(st_s)

    sm = sm_ref[0]
    first = d * GDN_HEADS + pl.program_id(1) * GDN_HB
    rot = jnp.where(first == 0, 0, LANE - first)
    la = pltpu.roll(_softplus(sm + dtb_ref[...]) * nega_ref[...], rot, 1)
    be_sm = pltpu.roll(jax.nn.sigmoid(sm), rot, 1)
    gc_sm = _sel_dot(_chunk_cum_mask(is_fwd, TT), la)
    gl_sm = _chunk_totals(gc_sm, is_fwd)
    lane_bcast = lambda a, col: jnp.broadcast_to(a[:, col:col + 1], (TT, GDN_DK))

    r = lax.broadcasted_iota(jnp.int32, (CHUNK, CHUNK), 0)
    c = lax.broadcasted_iota(jnp.int32, (CHUNK, CHUNK), 1)
    causal = _causal_mask(is_fwd, CHUNK)
    strict = jnp.logical_and(causal, r != c)
    b16 = lax.shift_right_logical(r, 4) == lax.shift_right_logical(c, 4)
    b32 = lax.shift_right_logical(r, 5) == lax.shift_right_logical(c, 5)
    eye = jnp.where(r == c, 1.0, 0.0)
    chunk_rows = [slice(ci * CHUNK, (ci + 1) * CHUNK) for ci in range(CPT)]

    amats, rhss, aqks, kdts, qgs = [], [], [], [], []
    for hh in range(GDN_HB):
        cols = slice(hh * GDN_DK, (hh + 1) * GDN_DK)
        gc = lane_bcast(gc_sm, hh)
        gl = lane_bcast(gl_sm, hh)
        beta_e = lane_bcast(be_sm, 2 * GDN_HEADS + hh)
        q = q_ref[0, :, cols].astype(F32)
        k = k_ref[0, :, cols].astype(F32)
        egc = jnp.exp(gc)
        kb = k * beta_e
        qg = q * egc
        kd = k * jnp.exp(gl - gc)
        dec_s[:, cols] = jnp.exp(gl)
        rhs = jnp.concatenate([v_ref[0, :, cols].astype(F32) * beta_e, kb * egc], axis=1)
        for rows in chunk_rows:
            gcc = gc[rows, :]
            dmat = jnp.exp(jnp.where(causal, gcc[:, 0:CHUNK] - gcc.T[0:CHUNK, :], -1e30))
            kc = k[rows].astype(BF16)
            amats.append(jnp.where(strict, _dot_nt(kb[rows].astype(BF16), kc) * dmat, 0.0))
            rhss.append(rhs[rows])
            aqks.append((_dot_nt(q[rows].astype(BF16), kc) * dmat).astype(BF16))
            kdts.append(kd[rows].T.astype(BF16))
            qgs.append(qg[rows])
    tinv = _unit_tri_inverse(amats, b16, b32, eye)
    sols = [_mm2(t, rhs_c).astype(BF16) for t, rhs_c in zip(tinv, rhss)]
    for n, sol in enumerate(sols):
        hh, ci = n // CPT, n % CPT
        rows, cols = chunk_rows[ci], slice(hh * GDN_DK, (hh + 1) * GDN_DK)
        ks = _dot(kdts[n], sol)
        qs = _dot(aqks[n], sol)
        n_s[n] = ks[:, 0:GDN_DV]
        p_s[n] = ks[:, GDN_DV:].astype(BF16)
        oc_s[rows, cols] = qs[:, 0:GDN_DV]
        qp_s[rows, cols] = (qgs[n] - qs[:, GDN_DV:]).astype(BF16)

    for ci in range(CPT):
        cidx = jnp.where(is_fwd, ci, CPT - 1 - ci)
        off = pl.multiple_of(cidx * CHUNK, CHUNK)
        rows = pl.ds(off, CHUNK)
        for hh in range(GDN_HB):
            cols = slice(hh * GDN_DK, (hh + 1) * GDN_DK)
            st = st_s[hh]
            stb = st.astype(BF16)
            o_ref[0, 0, rows, cols] = (_dot(qp_s[rows, cols], stb) + oc_s[rows, cols]).astype(BF16)
            st_s[hh] = st * dec_s[pl.ds(off, 1), cols] - _dot(p_s[hh * CPT + cidx], stb) + n_s[hh * CPT + cidx]


def gdn_scan(qkv, small, dtb, nega):
    bsz, t, _ = qkv.shape
    nt = t // TT
    tile = lambda d, j: _scan_tile(d, j, nt)
    ng = GDN_HEADS // GDN_HB
    wb = GDN_HB * GDN_DK
    return pl.pallas_call(
        _gdn_kernel,
        grid=(bsz, ng, 2, nt),
        in_specs=[
            pl.BlockSpec((1, TT, wb), lambda i, h, d, j: (i, tile(d, j), h)),
            pl.BlockSpec((1, TT, wb), lambda i, h, d, j: (i, tile(d, j), ng + h)),
            pl.BlockSpec((1, TT, wb), lambda i, h, d, j: (i, tile(d, j), 2 * ng + h)),
            pl.BlockSpec((1, TT, LANE), lambda i, h, d, j: (i, tile(d, j), 0)),
            pl.BlockSpec((1, LANE), lambda i, h, d, j: (0, 0)),
            pl.BlockSpec((1, LANE), lambda i, h, d, j: (0, 0)),
        ],
        out_specs=pl.BlockSpec((1, 1, TT, wb), lambda i, h, d, j: (d, i, tile(d, j), h)),
        out_shape=jax.ShapeDtypeStruct((2, bsz, t, GDN_V), BF16),
        scratch_shapes=[pltpu.VMEM((TT, wb), F32),
                        pltpu.VMEM((GDN_HB * CPT, GDN_DK, GDN_DV), F32),
                        pltpu.VMEM((GDN_HB * CPT, GDN_DK, GDN_DK), BF16),
                        pltpu.VMEM((TT, wb), F32), pltpu.VMEM((TT, wb), BF16),
                        pltpu.VMEM((GDN_HB, GDN_DK, GDN_DV), F32)],
        compiler_params=pltpu.CompilerParams(
            dimension_semantics=("parallel", "parallel", "parallel", "arbitrary")),
        name="gdn_scan",
    )(qkv, qkv, qkv, small, dtb, nega)


CONF_PAD = 16


def _cd_out_kernel(ga_ref, gb_ref, og_ref, of_ref, ob_ref, cw_ref, cb_ref, lg_ref, lb_ref, ng_ref,
                   w_ref, b_ref, x_ref, gate_ref, g2_ref, sh2_ref, sc2_ref, rw_ref, o_ref, h_ref, aff_ref,
                   pad_s, shift_s, conv_s):
    j = pl.program_id(1)
    half = (CONF_KERNEL - 1) // 2
    glu = ga_ref[0].astype(F32) * jax.nn.sigmoid(gb_ref[0].astype(F32))
    zeros = jnp.zeros((CONF_PAD, CONF_CH), F32)

    def conv_segments(seglen):
        stride = seglen + 2 * CONF_PAD
        for g in range(TT // seglen):
            base = g * stride
            pad_s[base:base + CONF_PAD, :] = zeros
            pad_s[base + CONF_PAD:base + CONF_PAD + seglen, :] = glu[g * seglen:(g + 1) * seglen]
            pad_s[base + CONF_PAD + seglen:base + stride, :] = zeros
        nseg = TT // seglen
        used = nseg * stride
        conv_s[...] = jnp.zeros((TT, CONF_CH), F32) + cb_ref[...]
        for phase in range(SUBLANES):
            taps = [kk for kk in range(CONF_KERNEL) if (CONF_PAD - half + kk) % SUBLANES == phase]
            if phase:
                shift_s[0:used - SUBLANES, :] = pad_s[phase:used - SUBLANES + phase, :]
            src = shift_s if phase else pad_s
            for g in range(nseg):
                acc = conv_s[g * seglen:(g + 1) * seglen, :]
                for kk in taps:
                    lo = g * stride + CONF_PAD - half + kk - phase
                    acc = acc + src[lo:lo + seglen, :] * cw_ref[kk:kk + 1, :]
                conv_s[g * seglen:(g + 1) * seglen, :] = acc

    @pl.when(j == 0)
    def _():
        conv_segments(TT)

    @pl.when(j > 0)
    def _():
        conv_segments(GRID_W)

    acc = conv_s[...]
    mu = jnp.mean(acc, axis=-1, keepdims=True)
    cen = acc - mu
    var = jnp.mean(cen * cen, axis=-1, keepdims=True)
    conv = _silu(cen * lax.rsqrt(var + 1e-5) * lg_ref[...] + lb_ref[...])
    o = (_group_rmsnorm(of_ref[0, 0].astype(F32) + ob_ref[0, 0].astype(F32), GDN_DV) * ng_ref[...]
         * _silu(og_ref[0].astype(F32)))
    m = _dot(conv.astype(BF16), w_ref[0:CONF_CH, :]) + _dot(o.astype(BF16), w_ref[CONF_CH:, :]) + b_ref[...]
    x_new = x_ref[0] + gate_ref[0, 0] * m
    o_ref[0] = x_new
    _route_tokens(x_new, g2_ref, sh2_ref, sc2_ref, rw_ref, h_ref, aff_ref)


def cd_out(o_gdn, proj, conv_w, conv_b, ln_g, ln_b, norm_g, w, b, x, gate, route):
    bsz, t, d = x.shape
    seg = lambda i, j: (i, jnp.minimum(j, 1), 0, 0)
    row = lambda width: pl.BlockSpec((1, width), lambda i, j: (0, 0))
    r_ins, r_outs, r_shapes = _route_specs(bsz, t, d, seg)
    g2, sh2, sc2, rw = route
    return pl.pallas_call(
        _cd_out_kernel,
        grid=(bsz, t // TT),
        in_specs=[
            pl.BlockSpec((1, TT, CONF_CH), lambda i, j: (i, j, 0)),
            pl.BlockSpec((1, TT, CONF_CH), lambda i, j: (i, j, 1)),
            pl.BlockSpec((1, TT, GDN_V), lambda i, j: (i, j, 5)),
            pl.BlockSpec((1, 1, TT, GDN_V), lambda i, j: (0, i, j, 0)),
            pl.BlockSpec((1, 1, TT, GDN_V), lambda i, j: (1, i, j, 0)),
            pl.BlockSpec((CONF_KERNEL, CONF_CH), lambda i, j: (0, 0)),
            row(CONF_CH), row(CONF_CH), row(CONF_CH), row(GDN_V),
            pl.BlockSpec((CONF_CH + GDN_V, d), lambda i, j: (0, 0)),
            row(d),
            pl.BlockSpec((1, TT, d), lambda i, j: (i, j, 0)),
            pl.BlockSpec((1, 1, 1, d), seg),
        ] + r_ins,
        out_specs=[pl.BlockSpec((1, TT, d), lambda i, j: (i, j, 0))] + r_outs,
        out_shape=[jax.ShapeDtypeStruct((bsz, t, d), F32)] + r_shapes,
        scratch_shapes=[pltpu.VMEM(((TT // GRID_W) * (GRID_W + 2 * CONF_PAD), CONF_CH), F32),
                        pltpu.VMEM(((TT // GRID_W) * (GRID_W + 2 * CONF_PAD), CONF_CH), F32),
                        pltpu.VMEM((TT, CONF_CH), F32)],
        compiler_params=pltpu.CompilerParams(
            dimension_semantics=("parallel", "parallel"), vmem_limit_bytes=VMEM_LIMIT),
        name="cd_out",
    )(proj, proj, proj, o_gdn, o_gdn, conv_w, conv_b.reshape(1, -1), ln_g.reshape(1, -1), ln_b.reshape(1, -1),
      norm_g.reshape(1, -1), w, b.reshape(1, d), x, gate, g2.reshape(1, d), sh2, sc2, rw)


def _route_tokens(x, g_ref, sh_ref, sc_ref, rw_ref, h_ref, aff_ref):
    ms = jnp.mean(x * x, axis=-1, keepdims=True)
    h = (x * lax.rsqrt(ms + 1e-6) * g_ref[...] * (1.0 + sc_ref[0, 0]) + sh_ref[0, 0]).astype(BF16)
    h_ref[0] = h
    logits = _dot(h, rw_ref[...])
    lane = lax.broadcasted_iota(jnp.int32, logits.shape, 1)
    logits = jnp.where(lane < N_EXPERTS, logits, -1e30)
    e = jnp.exp(logits - jnp.max(logits, axis=-1, keepdims=True))
    aff = e / jnp.sum(e, axis=-1, keepdims=True)
    aff_ref[0] = aff.T[0:N_EXPERTS, :]


def _route_specs(bsz, t, d, seg_index):
    ins = [pl.BlockSpec((1, d), lambda i, j: (0, 0)),
           pl.BlockSpec((1, 1, 1, d), seg_index),
           pl.BlockSpec((1, 1, 1, d), seg_index),
           pl.BlockSpec((d, LANE), lambda i, j: (0, 0))]
    outs = [pl.BlockSpec((1, TT, d), lambda i, j: (i, j, 0)),
            pl.BlockSpec((1, N_EXPERTS, TT), lambda i, j: (i, 0, j))]
    shapes = [jax.ShapeDtypeStruct((bsz, t, d), BF16), jax.ShapeDtypeStruct((bsz, N_EXPERTS, t), F32)]
    return ins, outs, shapes


def _lane_block_prefix(x, u_strict):
    nblk = x.shape[1] // LANE
    run = jnp.zeros((x.shape[0], 1), F32)
    outs = []
    for cblk in range(nblk):
        xc = x[:, cblk * LANE:(cblk + 1) * LANE]
        outs.append(_dot(xc.astype(BF16), u_strict) + run)
        run = run + jnp.sum(xc, axis=-1, keepdims=True)
    return jnp.concatenate(outs, axis=1), run


def _select_kernel(aff_ref, slot_ref, *, cap):
    aff = aff_ref[0]
    bits = pltpu.bitcast(aff, jnp.int32)
    capf = jnp.float32(cap)

    def step(i, thr):
        cand = jnp.bitwise_or(thr, lax.shift_left(jnp.int32(1), 30 - i))
        cnt = jnp.sum(jnp.where(bits >= cand, 1.0, 0.0), axis=-1, keepdims=True)
        return jnp.where(cnt >= capf, cand, thr)

    thr = lax.fori_loop(0, 31, step, jnp.zeros((aff.shape[0], 1), jnp.int32))
    gt = jnp.where(bits > thr, 1.0, 0.0)
    eq = jnp.where(bits == thr, 1.0, 0.0)
    r = lax.broadcasted_iota(jnp.int32, (LANE, LANE), 0)
    c = lax.broadcasted_iota(jnp.int32, (LANE, LANE), 1)
    u_strict = jnp.where(r < c, 1.0, 0.0).astype(BF16)
    need = capf - jnp.sum(gt, axis=-1, keepdims=True)
    eq_rank, _ = _lane_block_prefix(eq, u_strict)
    sel = jnp.maximum(gt, jnp.where(eq_rank < need, eq, 0.0))
    slot, _ = _lane_block_prefix(sel, u_strict)
    slot_ref[0] = jnp.where(sel > 0.0, slot.astype(jnp.int32), -1)


def moe_select(aff, cap):
    bsz, ne, n = aff.shape
    return pl.pallas_call(
        functools.partial(_select_kernel, cap=cap),
        grid=(bsz,),
        in_specs=[pl.BlockSpec((1, ne, n), lambda i: (i, 0, 0))],
        out_specs=pl.BlockSpec((1, ne, n), lambda i: (i, 0, 0)),
        out_shape=jax.ShapeDtypeStruct((bsz, ne, n), jnp.int32),
        compiler_params=pltpu.CompilerParams(dimension_semantics=("parallel",)),
        name="moe_select",
    )(aff)


def _slot_index_kernel(slot_ref, idx_ref, *, cap):
    slot = slot_ref[0]
    n = slot.shape[1]
    srow = lax.broadcasted_iota(jnp.int32, (cap, LANE), 0)
    lane = lax.broadcasted_iota(jnp.int32, (cap, LANE), 1)
    acc = jnp.zeros((cap, LANE), jnp.int32)
    for cblk in range(n // LANE):
        s_c = slot[:, cblk * LANE:(cblk + 1) * LANE]
        acc = acc + jnp.where(srow == s_c, lane + (cblk * LANE + 1), 0)
    ones = jnp.ones((8, LANE), BF16)
    hi = _dot_nt(ones, lax.shift_right_logical(acc, 7).astype(F32).astype(BF16))
    lo = _dot_nt(ones, jnp.bitwise_and(acc, LANE - 1).astype(F32).astype(BF16))
    idx_ref[0] = (hi[0:1, :] * float(LANE) + lo[0:1, :]).astype(jnp.int32) - 1


SLOT_WIN = 2 * LANE


def _slot_index_win_kernel(base_ref, slot_ref, idx_ref, acc_s, *, cap):
    i = pl.program_id(0)
    n = slot_ref.shape[2]
    acc_s[...] = jnp.zeros(acc_s.shape, jnp.int32)
    srow = lax.broadcasted_iota(jnp.int32, (SLOT_WIN, LANE), 0)
    lane = lax.broadcasted_iota(jnp.int32, (SLOT_WIN, LANE), 1)
    for cblk in range(n // LANE):
        base = pl.multiple_of(base_ref[i, cblk], LANE)
        s_c = slot_ref[0, :, cblk * LANE:(cblk + 1) * LANE]
        rows = pl.ds(base, SLOT_WIN)
        acc_s[rows, :] = acc_s[rows, :] + jnp.where(srow + base == s_c, lane + (cblk * LANE + 1), 0)
    acc = acc_s[0:cap, :]
    ones = jnp.ones((8, LANE), BF16)
    hi = _dot_nt(ones, lax.shift_right_logical(acc, 7).astype(F32).astype(BF16))
    lo = _dot_nt(ones, jnp.bitwise_and(acc, LANE - 1).astype(F32).astype(BF16))
    idx_ref[0] = (hi[0:1, :] * float(LANE) + lo[0:1, :]).astype(jnp.int32) - 1


def moe_slot_index_windowed(slot, base, cap):
    bsz, ne, n = slot.shape
    idx = pl.pallas_call(
        functools.partial(_slot_index_win_kernel, cap=cap),
        grid_spec=pltpu.PrefetchScalarGridSpec(
            num_scalar_prefetch=1,
            grid=(bsz * ne,),
            in_specs=[pl.BlockSpec((1, 1, n), lambda i, base_ref: (i, 0, 0))],
            out_specs=pl.BlockSpec((1, 1, cap), lambda i, base_ref: (i, 0, 0)),
            scratch_shapes=[pltpu.VMEM((cap + LANE, LANE), jnp.int32)],
        ),
        out_shape=jax.ShapeDtypeStruct((bsz * ne, 1, cap), jnp.int32),
        compiler_params=pltpu.CompilerParams(dimension_semantics=("arbitrary",)),
        name="moe_slot_index",
    )(base, slot.reshape(bsz * ne, 1, n))
    return idx.reshape(bsz, ne, cap)


def moe_slot_index(slot, cap):
    bsz, ne, n = slot.shape
    idx = pl.pallas_call(
        functools.partial(_slot_index_kernel, cap=cap),
        grid=(bsz * ne,),
        in_specs=[pl.BlockSpec((1, 1, n), lambda i: (i, 0, 0))],
        out_specs=pl.BlockSpec((1, 1, cap), lambda i: (i, 0, 0)),
        out_shape=jax.ShapeDtypeStruct((bsz * ne, 1, cap), jnp.int32),
        compiler_params=pltpu.CompilerParams(dimension_semantics=("parallel",)),
        name="moe_slot_index",
    )(slot.reshape(bsz * ne, 1, n))
    return idx.reshape(bsz, ne, cap)


WIN_ALIGN = 16
WIN_FAST = 128


def _combine_kernel(ws_ref, slot_ref, aff_ref, *rest, win, final):
    ye_refs, (x_ref, gate_ref), o_ref = rest[:N_EXPERTS], rest[N_EXPERTS:N_EXPERTS + 2], rest[-1]
    b = pl.program_id(0)
    j = pl.program_id(1)
    srow = lax.broadcasted_iota(jnp.int32, (win, TT), 0)
    his, los = [], []
    for e in range(N_EXPERTS):
        sel = jnp.where(srow + ws_ref[b, e, j] == slot_ref[0, e:e + 1, :], aff_ref[0, e:e + 1, :], 0.0).T
        hi = sel.astype(BF16)
        his.append(hi)
        los.append((sel - hi.astype(F32)).astype(BF16))
    ye = jnp.concatenate([r[...] for r in ye_refs], axis=0)
    acc = _dot(jnp.concatenate(his, axis=1), ye) + _dot(jnp.concatenate(los, axis=1), ye)
    y = x_ref[0] + gate_ref[0, 0] * acc
    if final:
        g_ref = rest[N_EXPERTS + 2]
        y = y * lax.rsqrt(jnp.mean(y * y, axis=-1, keepdims=True) + 1e-6) * g_ref[...]
    o_ref[0] = y


def moe_combine(ws, slot, aff, ye, xa, gate, seg, tile0, win, final_g=None):
    bsz, ne, n = slot.shape
    nt = n // TT
    d = xa.shape[2]

    def ye_spec(e):
        return pl.BlockSpec((pl.Squeezed(), pl.Squeezed(), pl.Element(win), pl.Element(d)),
                            lambda i, j, ws_ref: (i, e, pl.multiple_of(ws_ref[i, e, j], WIN_ALIGN), 0))

    in_specs = ([pl.BlockSpec((1, ne, TT), lambda i, j, ws_ref: (i, 0, j)),
                 pl.BlockSpec((1, ne, TT), lambda i, j, ws_ref: (i, 0, j))]
                + [ye_spec(e) for e in range(ne)]
                + [pl.BlockSpec((1, TT, d), lambda i, j, ws_ref: (i, j + tile0, 0)),
                   pl.BlockSpec((1, 1, 1, d), lambda i, j, ws_ref: (i, seg, 0, 0))])
    args = (ws, slot, aff, *([ye] * ne), xa, gate)
    if final_g is None:
        out_idx, out_shape, aliases = (lambda i, j, ws_ref: (i, j + tile0, 0)), xa.shape, {3 + ne: 0}
    else:
        out_idx, out_shape, aliases = (lambda i, j, ws_ref: (i, j, 0)), (bsz, n, d), {}
        in_specs.append(pl.BlockSpec((1, d), lambda i, j, ws_ref: (0, 0)))
        args += (final_g.reshape(1, d),)
    return pl.pallas_call(
        functools.partial(_combine_kernel, win=win, final=final_g is not None),
        grid_spec=pltpu.PrefetchScalarGridSpec(
            num_scalar_prefetch=1,
            grid=(bsz, nt),
            in_specs=in_specs,
            out_specs=pl.BlockSpec((1, TT, d), out_idx),
        ),
        out_shape=jax.ShapeDtypeStruct(out_shape, F32),
        input_output_aliases=aliases,
        compiler_params=pltpu.CompilerParams(
            dimension_semantics=("parallel", "parallel"), vmem_limit_bytes=VMEM_LIMIT),
        name="moe_combine",
    )(*args)


def _expert_ffn_kernel(x_ref, w1_ref, w3_ref, w2_ref, o_ref, w1_s, w3_s, w2_s):
    @pl.when(jnp.logical_and(pl.program_id(1) == 0, pl.program_id(2) == 0))
    def _():
        w1_s[...] = w1_ref[0, 0].astype(BF16)
        w3_s[...] = w3_ref[0, 0].astype(BF16)
        w2_s[...] = w2_ref[0, 0].astype(BF16)

    x = x_ref[0, 0]
    a = _dot(x, w1_s[...])
    g = _dot(x, w3_s[...])
    o_ref[0, 0] = _dot((_silu(a) * g).astype(BF16), w2_s[...]).astype(BF16)


def expert_ffn(xe, w1, w3, w2, layer):
    bsz, ne, cap, d = xe.shape
    f = w1.shape[3]
    tm = cap // max(1, cap // 512)
    assert cap % tm == 0 and tm % WIN_ALIGN == 0
    return pl.pallas_call(
        _expert_ffn_kernel,
        grid=(ne, bsz, cap // tm),
        in_specs=[
            pl.BlockSpec((1, 1, tm, d), lambda e, i, j: (i, e, j, 0)),
            pl.BlockSpec((1, 1, d, f), lambda e, i, j: (layer, e, 0, 0)),
            pl.BlockSpec((1, 1, d, f), lambda e, i, j: (layer, e, 0, 0)),
            pl.BlockSpec((1, 1, f, d), lambda e, i, j: (layer, e, 0, 0)),
        ],
        out_specs=pl.BlockSpec((1, 1, tm, d), lambda e, i, j: (i, e, j, 0)),
        out_shape=jax.ShapeDtypeStruct((bsz, ne, cap, d), BF16),
        scratch_shapes=[pltpu.VMEM((d, f), BF16), pltpu.VMEM((d, f), BF16), pltpu.VMEM((f, d), BF16)],
        compiler_params=pltpu.CompilerParams(
            dimension_semantics=("arbitrary", "arbitrary", "arbitrary"), vmem_limit_bytes=VMEM_LIMIT),
        name="expert_ffn",
    )(xe, w1, w3, w2)


def _ab_in_layout(w_in, b_in):
    q, k, v, r, glr, z, xs, bm, cm, dt = _split_cols(
        jnp.concatenate([w_in, b_in[None]], axis=0),
        (GLA_QK, GLA_QK, GLA_V, GLA_V, 2 * GLA_GATE_RANK, SSD_INNER, SSD_INNER, SSD_BC, SSD_BC, 2 * SSD_HEADS))
    wide = jnp.concatenate([q, k, v, r, z, xs, bm, cm], axis=1)
    narrow = _pad_cols(jnp.concatenate([glr, dt], axis=1), LANE)
    return wide[:-1].astype(BF16), wide[-1], narrow[:-1].astype(BF16), narrow[-1]


def _gla_gate_params(w_gate2, b_gate2):
    wg = jnp.zeros((2, LANE, GLA_QK), F32)
    for d in range(2):
        wg = wg.at[d, d * GLA_GATE_RANK:(d + 1) * GLA_GATE_RANK, :].set(w_gate2[d])
    return wg.astype(BF16), b_gate2.reshape(2, 1, GLA_QK)


def _ssd_params(dt_bias, a_log):
    dtb = jnp.zeros((2, 1, LANE), F32)
    nega = jnp.zeros((2, 1, LANE), F32)
    e = np.zeros((2, LANE, SSD_INNER), np.float32)
    for d in range(2):
        c0 = SM_DT + d * SSD_HEADS
        dtb = dtb.at[d, 0, c0:c0 + SSD_HEADS].set(dt_bias[d])
        nega = nega.at[d, 0, c0:c0 + SSD_HEADS].set(-jnp.exp(a_log[d]))
        for h in range(SSD_HEADS):
            e[d, c0 + h, h * SSD_HEADDIM:(h + 1) * SSD_HEADDIM] = 1.0
    return dtb, nega, jnp.asarray(e, BF16)


def _gdn_params(dt_bias, a_log):
    n = 2 * GDN_HEADS
    dtb = jnp.zeros((1, LANE), F32).at[0, 0:n].set(dt_bias.reshape(n))
    nega = jnp.zeros((1, LANE), F32).at[0, 0:n].set(-jnp.exp(a_log.reshape(n)))
    return dtb, nega


def _moe_route(aff_all, tile0, ntiles):
    bsz = aff_all.shape[0]
    n = ntiles * TT
    cap = n * EC_CAPACITY // N_EXPERTS
    aff = aff_all[:, :, tile0 * TT:tile0 * TT + n]
    slot = moe_select(aff, cap)
    cnt128 = jnp.sum((slot >= 0).reshape(bsz, N_EXPERTS, n // LANE, LANE), axis=-1, dtype=jnp.int32)
    start128 = jnp.cumsum(cnt128, axis=-1) - cnt128
    if cap >= SLOT_WIN:
        base = jnp.minimum(start128 // LANE * LANE, cap - LANE).reshape(bsz * N_EXPERTS, n // LANE)
        idx = moe_slot_index_windowed(slot, base, cap)
    else:
        idx = moe_slot_index(slot, cap)
    per_tile = TT // LANE
    counts = jnp.sum(cnt128.reshape(bsz, N_EXPERTS, ntiles, per_tile), axis=-1)
    return dict(aff=aff, slot=slot, idx=idx + tile0 * TT, cap=cap, counts=counts, starts=start128[:, :, ::per_tile])


def moe_layer(i, xa, h_all, aff_all, mods, w1, w3, w2, segments, final_g=None):
    bsz, _, d = xa.shape
    routes = [_moe_route(aff_all, tile0, ntiles) for _, tile0, ntiles in segments]
    cap_all = sum(r["cap"] for r in routes)
    idx = jnp.concatenate([r["idx"] for r in routes], axis=2).reshape(bsz, N_EXPERTS * cap_all)
    xe = jnp.take_along_axis(h_all, idx[..., None], axis=1, mode="promise_in_bounds")
    ye = expert_ffn(xe.reshape(bsz, N_EXPERTS, cap_all, d), w1, w3, w2, i)
    row0 = 0
    for (seg, tile0, _), r in zip(segments, routes):
        cap = r["cap"]
        aligned = r["starts"] // WIN_ALIGN * WIN_ALIGN

        slot = r["slot"] if row0 == 0 else jnp.where(r["slot"] >= 0, r["slot"] + row0, -1)

        def run(win, r=r, cap=cap, aligned=aligned, row0=row0, seg=seg, tile0=tile0, xa=xa, slot=slot):
            ws = jnp.minimum(aligned, cap - win) + row0
            return moe_combine(ws, slot, r["aff"], ye, xa, mods[:, :, 5], seg, tile0, win, final_g)

        win_fast, win_full = min(cap, WIN_FAST), min(cap, TT + WIN_ALIGN)
        if win_fast == win_full:
            xa = run(win_full)
        else:
            overflow = jnp.any(r["starts"] + r["counts"] - jnp.minimum(aligned, cap - win_fast) > win_fast)
            xa = lax.cond(overflow, functools.partial(run, win_full), functools.partial(run, win_fast))
        row0 += cap
    return xa


def layer_mixer(i, j, xa, mods, p, last):
    sh1, sc1, g1 = (mods[:, :, s] for s in range(3))
    route = (p["norm2_g"][i], mods[:, :, 3], mods[:, :, 4], _pad_cols(p["moe_router"][i], LANE).astype(BF16))
    if i % 2 == 0:
        proj, small = norm_proj(xa, p["norm1_g"][i], sh1, sc1, *_ab_in_layout(p["ab_w_in"][j], p["ab_b_in"][j]),
                                AB_WIDE // 2)
        xbc = ab_prep(proj, p["ssd_conv_w"][j], p["ssd_conv_b"][j])
        wg, bg = _gla_gate_params(p["gla_w_gate2"][j], p["gla_b_gate2"][j])
        o_gla = gla_scan(proj, small, wg, bg)
        y_ssd = ssd_scan(xbc, small, *_ssd_params(p["ssd_dt_bias"][j], p["ssd_a_log"][j]))
        return ab_out(o_gla, y_ssd, proj, xbc, p["gla_norm_g"][j], jnp.repeat(p["ssd_d"][j], SSD_HEADDIM),
                      p["ssd_norm_g"][j], p["ab_w_out"][j].astype(BF16), p["ab_b_out"][j], xa, g1, route)
    w_in, b_in = p["cd_w_in"][j], p["cd_b_in"][j]
    proj, small = norm_proj(xa, p["norm1_g"][i], sh1, sc1, w_in[:, :CD_WIDE].astype(BF16), b_in[:CD_WIDE],
                            _pad_cols(w_in[:, CD_WIDE:], LANE).astype(BF16), _pad_cols(b_in[CD_WIDE:], LANE),
                            CD_WIDE // 3)
    qkv = cd_prep(proj, p["gdn_conv_w"][j])
    o_gdn = gdn_scan(qkv, small, *_gdn_params(p["gdn_dt_bias"][j], p["gdn_a_log"][j]))
    return cd_out(o_gdn, proj, p["conf_dw_w"][j], p["conf_dw_b"][j], p["conf_ln_g"][j], p["conf_ln_b"][j],
                  p["gdn_norm_g"][j], p["cd_w_out"][j].astype(BF16), p["cd_b_out"][j], xa, g1, route)


def kernel(x, c, ctx, c_ctx, mod_w, mod_b, norm1_g, norm2_g, ab_w_in, ab_b_in, ab_w_out, ab_b_out, gla_w_gate2, gla_b_gate2, gla_norm_g, ssd_conv_w, ssd_conv_b, ssd_dt_bias, ssd_a_log, ssd_d, ssd_norm_g, cd_w_in, cd_b_in, cd_w_out, cd_b_out, conf_dw_w, conf_dw_b, conf_ln_g, conf_ln_b, gdn_conv_w, gdn_a_log, gdn_dt_bias, gdn_norm_g, moe_router, moe_w1, moe_w3, moe_w2, final_norm_g):
    p = dict(locals())
    bsz, seq, d = x.shape
    assert ctx.shape[1] == TT and seq % TT == 0
    depth = mod_w.shape[0]
    xa = jnp.concatenate([ctx, x], axis=1)
    w1, w3, w2 = moe_w1, moe_w3, moe_w2
    cond = jnp.concatenate([c, c_ctx[None]], axis=0)
    for i in range(depth):
        last = i == depth - 1
        mod = mod_proj(cond, mod_w[i].astype(BF16), mod_b[i])
        mods = jnp.stack([jnp.broadcast_to(mod[bsz], (bsz, 6 * d)), mod[:bsz]], axis=1).reshape(bsz, 2, 6, 1, d)
        xa, h_all, aff_all = layer_mixer(i, i // 2, xa, mods, p, last)
        latent, context = (1, 1, seq // TT), (0, 0, 1)
        if last:
            return moe_layer(i, xa, h_all, aff_all, mods, w1, w3, w2, [latent], final_norm_g)
        xa = moe_layer(i, xa, h_all, aff_all, mods, w1, w3, w2, [latent, context])
```

```python
import functools

import jax
import jax.numpy as jnp
import numpy as np
from jax import lax
from jax.experimental import pallas as pl
from jax.experimental.pallas import tpu as pltpu

F32 = jnp.float32
BF16 = jnp.bfloat16

D_MODEL = 1024
GRID_W = 64
CHUNK = 64
GLA_HEADS, GLA_DK, GLA_DV, GLA_GATE_RANK, GLA_GATE_TAU = 4, 128, 256, 16, 16.0
SSD_HEADS, SSD_HEADDIM, SSD_STATE, SSD_GROUPS = 16, 64, 128, 2
CONF_CH, CONF_KERNEL = D_MODEL, 31
GDN_HEADS, GDN_DK, GDN_DV = 8, 128, 128
N_EXPERTS, EC_CAPACITY = 16, 2

GLA_QK = GLA_HEADS * GLA_DK
GLA_V = GLA_HEADS * GLA_DV
SSD_INNER = SSD_HEADS * SSD_HEADDIM
SSD_BC = SSD_GROUPS * SSD_STATE
SSD_HPG = SSD_HEADS // SSD_GROUPS
GDN_QK = GDN_HEADS * GDN_DK
GDN_V = GDN_HEADS * GDN_DV

LANE = 128
SUBLANES = 8
TT = 256
CPT = TT // CHUNK
VMEM_LIMIT = 48 * 1024 * 1024

AB_Q, AB_K, AB_V, AB_R, AB_Z, AB_XS, AB_BM, AB_CM, AB_WIDE = 0, 512, 1024, 2048, 3072, 4096, 5120, 5376, 5632
SM_DT = 2 * GLA_GATE_RANK
CD_WIDE = 6 * D_MODEL


def _split_cols(a, sizes):
    return jnp.split(a, np.cumsum(sizes)[:-1].tolist(), axis=-1)


def _pad_cols(a, n):
    return jnp.pad(a, [(0, 0)] * (a.ndim - 1) + [(0, n - a.shape[-1])])


def _dot(a, b):
    return jnp.dot(a, b, preferred_element_type=F32)


def _dot_nt(a, b):
    return lax.dot_general(a, b, (((1,), (1,)), ((), ())), preferred_element_type=F32)


def _split3(x):
    hi = x.astype(BF16)
    r = x - hi.astype(F32)
    mid = r.astype(BF16)
    lo = (r - mid.astype(F32)).astype(BF16)
    return hi, mid, lo


def _sel_dot(m, x):
    hi, mid, lo = _split3(x)
    return _dot(m, hi) + _dot(m, mid) + _dot(m, lo)


def _softplus(x):
    return jnp.maximum(x, 0.0) + jnp.log(1.0 + jnp.exp(-jnp.abs(x)))


def _silu(x):
    return x * jax.nn.sigmoid(x)


def _chunk_cum_mask(is_fwd, n):
    r = lax.broadcasted_iota(jnp.int32, (n, n), 0)
    c = lax.broadcasted_iota(jnp.int32, (n, n), 1)
    same = lax.shift_right_logical(r, 6) == lax.shift_right_logical(c, 6)
    lo = jnp.where(is_fwd, c, r)
    hi = jnp.where(is_fwd, r, c)
    return jnp.where(jnp.logical_and(same, lo <= hi), 1.0, 0.0).astype(BF16)


def _chunk_totals(cum, is_fwd):
    rows = [jnp.where(is_fwd, cum[(ci + 1) * CHUNK - 1:(ci + 1) * CHUNK, :], cum[ci * CHUNK:ci * CHUNK + 1, :])
            for ci in range(CPT)]
    return jnp.concatenate([jnp.broadcast_to(row, (CHUNK, cum.shape[1])) for row in rows], axis=0)


def _causal_mask(is_fwd, n):
    r = lax.broadcasted_iota(jnp.int32, (n, n), 0)
    c = lax.broadcasted_iota(jnp.int32, (n, n), 1)
    return jnp.where(is_fwd, c, r) <= jnp.where(is_fwd, r, c)


def _scan_tile(d, j, nt):
    return jnp.where(d == 0, j, jnp.where(j == 0, 0, nt - j))


def _mod_proj_kernel(c_ref, w_ref, b_ref, o_ref):
    o_ref[...] = _dot(_silu(c_ref[...]).astype(BF16), w_ref[...]) + b_ref[...]


def mod_proj(cond, w, b):
    r, d = cond.shape
    n = w.shape[1]
    tn = 6 * LANE * 2
    return pl.pallas_call(
        _mod_proj_kernel,
        grid=(n // tn,),
        in_specs=[pl.BlockSpec((r, d), lambda k: (0, 0)),
                  pl.BlockSpec((d, tn), lambda k: (0, k)),
                  pl.BlockSpec((1, tn), lambda k: (0, k))],
        out_specs=pl.BlockSpec((r, tn), lambda k: (0, k)),
        out_shape=jax.ShapeDtypeStruct((r, n), F32),
        compiler_params=pltpu.CompilerParams(dimension_semantics=("parallel",)),
        name="mod_proj",
    )(cond, w, b.reshape(1, n))


NP_TM = 3 * TT


def _norm_proj_kernel(x_ref, g_ref, sh_ref, sc_ref, w_ref, b_ref, ws_ref, bs_ref, o_ref, os_ref):
    x = x_ref[0]
    row = lax.broadcasted_iota(jnp.int32, (NP_TM, 1), 0) + pl.program_id(2) * NP_TM
    is_ctx = row < TT
    scale = jnp.where(is_ctx, sc_ref[0, 0], sc_ref[0, 1])
    shift = jnp.where(is_ctx, sh_ref[0, 0], sh_ref[0, 1])
    ms = jnp.mean(x * x, axis=-1, keepdims=True)
    h = (x * lax.rsqrt(ms + 1e-6) * g_ref[...] * (1.0 + scale) + shift).astype(BF16)
    o_ref[0] = (_dot(h, w_ref[...]) + b_ref[...]).astype(BF16)
    os_ref[0, 0] = _dot(h, ws_ref[...]) + bs_ref[...]


def norm_proj(x, g, shift, scale, w, b, w_small, b_small, tn):
    bsz, t, d = x.shape
    n = w.shape[1]
    assert t % NP_TM == 0 and n % tn == 0
    wide, narrow = pl.pallas_call(
        _norm_proj_kernel,
        grid=(n // tn, bsz, t // NP_TM),
        in_specs=[
            pl.BlockSpec((1, NP_TM, d), lambda k, i, j: (i, j, 0)),
            pl.BlockSpec((1, d), lambda k, i, j: (0, 0)),
            pl.BlockSpec((1, 2, 1, d), lambda k, i, j: (i, 0, 0, 0)),
            pl.BlockSpec((1, 2, 1, d), lambda k, i, j: (i, 0, 0, 0)),
            pl.BlockSpec((d, tn), lambda k, i, j: (0, k)),
            pl.BlockSpec((1, tn), lambda k, i, j: (0, k)),
            pl.BlockSpec((d, LANE), lambda k, i, j: (0, 0)),
            pl.BlockSpec((1, LANE), lambda k, i, j: (0, 0)),
        ],
        out_specs=[pl.BlockSpec((1, NP_TM, tn), lambda k, i, j: (i, j, k)),
                   pl.BlockSpec((1, 1, NP_TM, LANE), lambda k, i, j: (k, i, j, 0))],
        out_shape=[jax.ShapeDtypeStruct((bsz, t, n), BF16),
                   jax.ShapeDtypeStruct((n // tn, bsz, t, LANE), F32)],
        compiler_params=pltpu.CompilerParams(
            dimension_semantics=("parallel", "parallel", "parallel"), vmem_limit_bytes=VMEM_LIMIT),
        name="norm_proj",
    )(x, g.reshape(1, d), shift, scale, w, b.reshape(1, n), w_small, b_small.reshape(1, LANE))
    return wide, narrow[0]


HALO = 16


def _conv3_piece(c_ref, l_ref, r_ref, w, b, left_ok, right_ok):
    x = c_ref[0].astype(F32)
    n = x.shape[0]
    row = lax.broadcasted_iota(jnp.int32, x.shape, 0)
    prev_row = jnp.where(left_ok, l_ref[0, HALO - 1:HALO, :].astype(F32), 0.0)
    next_row = jnp.where(right_ok, r_ref[0, 0:1, :].astype(F32), 0.0)
    x_prev = jnp.where(row == 0, prev_row, pltpu.roll(x, 1, 0))
    x_next = jnp.where(row == n - 1, next_row, pltpu.roll(x, n - 1, 0))
    return _silu(w[0:1, :] * x_prev + w[1:2, :] * x + w[2:3, :] * x_next + b)


def _ab_prep_kernel(xs_ref, bm_ref, cm_ref, xsl_ref, bml_ref, cml_ref, xsr_ref, bmr_ref, cmr_ref,
                    w_ref, b_ref, o_ref):
    j = pl.program_id(1)
    nt = pl.num_programs(1)
    left_ok = j >= 2
    right_ok = jnp.logical_and(j >= 1, j < nt - 1)
    w = w_ref[...]
    b = b_ref[...]
    o_ref[0, :, 0:SSD_INNER] = _conv3_piece(xs_ref, xsl_ref, xsr_ref, w[:, 0:SSD_INNER],
                                            b[:, 0:SSD_INNER], left_ok, right_ok).astype(BF16)
    c0, c1 = SSD_INNER, SSD_INNER + SSD_BC
    o_ref[0, :, c0:c1] = _conv3_piece(bm_ref, bml_ref, bmr_ref, w[:, c0:c1], b[:, c0:c1],
                                      left_ok, right_ok).astype(BF16)
    c0, c1 = c1, c1 + SSD_BC
    o_ref[0, :, c0:c1] = _conv3_piece(cm_ref, cml_ref, cmr_ref, w[:, c0:c1], b[:, c0:c1],
                                      left_ok, right_ok).astype(BF16)


def ab_prep(proj, conv_w, conv_b):
    bsz, t, _ = proj.shape
    nt = t // TT
    rb = TT // HALO
    nrb = t // HALO
    cw = SSD_INNER + 2 * SSD_BC

    def cur(width, col):
        return pl.BlockSpec((1, TT, width), lambda i, j: (i, j, col // width))

    def left(width, col):
        return pl.BlockSpec((1, HALO, width), lambda i, j: (i, jnp.maximum(j * rb - 1, 0), col // width))

    def right(width, col):
        return pl.BlockSpec((1, HALO, width), lambda i, j: (i, jnp.minimum((j + 1) * rb, nrb - 1), col // width))

    pieces = ((SSD_INNER, AB_XS), (SSD_BC, AB_BM), (SSD_BC, AB_CM))
    return pl.pallas_call(
        _ab_prep_kernel,
        grid=(bsz, nt),
        in_specs=[cur(*p) for p in pieces] + [left(*p) for p in pieces] + [right(*p) for p in pieces] + [
            pl.BlockSpec((3, cw), lambda i, j: (0, 0)),
            pl.BlockSpec((1, cw), lambda i, j: (0, 0)),
        ],
        out_specs=pl.BlockSpec((1, TT, cw), lambda i, j: (i, j, 0)),
        out_shape=jax.ShapeDtypeStruct((bsz, t, cw), BF16),
        compiler_params=pltpu.CompilerParams(dimension_semantics=("parallel", "parallel")),
        name="ab_prep",
    )(*([proj] * 9), conv_w, conv_b.reshape(1, cw))


def _gla_kernel(q_ref, k_ref, v_ref, sm_ref, wg_ref, bg_ref, o_ref, qg_s, egl_s, oi_s, u_s, st_s):
    d = pl.program_id(1)
    j = pl.program_id(2)
    is_fwd = d == 0

    @pl.when(j == 0)
    def _():
        st_s[...] = jnp.zeros_like(st_s)

    gz = _dot(sm_ref[0].astype(BF16), wg_ref[0]) + bg_ref[0]
    logg = (jnp.minimum(gz, 0.0) - jnp.log(1.0 + jnp.exp(-jnp.abs(gz)))) * (1.0 / GLA_GATE_TAU)
    gc = _sel_dot(_chunk_cum_mask(is_fwd, TT), logg)
    gl = _chunk_totals(gc, is_fwd)
    q = q_ref[0].astype(F32) * (GLA_DK ** -0.5)
    k = k_ref[0].astype(F32)
    qg = (q * jnp.exp(gc)).astype(BF16)
    kn = (k * jnp.exp(-gc)).astype(BF16)
    kd = (k * jnp.exp(gl - gc)).astype(BF16)
    qg_s[...] = qg
    egl_s[...] = jnp.exp(gl)
    causal = _causal_mask(is_fwd, CHUNK)

    pairs = [(h, ci) for h in range(GLA_HEADS) for ci in range(CPT)]
    rows_of = lambda ci: slice(ci * CHUNK, (ci + 1) * CHUNK)
    kcols = lambda h: slice(h * GLA_DK, (h + 1) * GLA_DK)
    vcols = lambda h: slice(h * GLA_DV, (h + 1) * GLA_DV)
    vs = [v_ref[0, rows_of(ci), vcols(h)] for h, ci in pairs]
    atts = [jnp.where(causal, _dot_nt(qg[rows_of(ci), kcols(h)], kn[rows_of(ci), kcols(h)]), 0.0).astype(BF16)
            for h, ci in pairs]
    for n, (h, ci) in enumerate(pairs):
        oi_s[rows_of(ci), vcols(h)] = _dot(atts[n], vs[n])
        u_s[n] = _dot(vs[n].astype(F32).T.astype(BF16), kd[rows_of(ci), kcols(h)])

    for ci in range(CPT):
        cidx = jnp.where(is_fwd, ci, CPT - 1 - ci)
        off = pl.multiple_of(cidx * CHUNK, CHUNK)
        rows = pl.ds(off, CHUNK)
        for h in range(GLA_HEADS):
            st = st_s[h]
            o_ref[0, 0, rows, vcols(h)] = (oi_s[rows, vcols(h)]
                                           + _dot_nt(qg_s[rows, kcols(h)], st.astype(BF16))).astype(BF16)
            st_s[h] = st * egl_s[pl.ds(off, 1), kcols(h)] + u_s[h * CPT + cidx]


def gla_scan(proj, small, wg, bg):
    bsz, t, _ = proj.shape
    nt = t // TT
    tile = lambda d, j: _scan_tile(d, j, nt)
    return pl.pallas_call(
        _gla_kernel,
        grid=(bsz, 2, nt),
        in_specs=[
            pl.BlockSpec((1, TT, GLA_QK), lambda i, d, j: (i, tile(d, j), AB_Q // GLA_QK)),
            pl.BlockSpec((1, TT, GLA_QK), lambda i, d, j: (i, tile(d, j), AB_K // GLA_QK)),
            pl.BlockSpec((1, TT, GLA_V), lambda i, d, j: (i, tile(d, j), AB_V // GLA_V)),
            pl.BlockSpec((1, TT, LANE), lambda i, d, j: (i, tile(d, j), 0)),
            pl.BlockSpec((1, LANE, GLA_QK), lambda i, d, j: (d, 0, 0)),
            pl.BlockSpec((1, 1, GLA_QK), lambda i, d, j: (d, 0, 0)),
        ],
        out_specs=pl.BlockSpec((1, 1, TT, GLA_V), lambda i, d, j: (d, i, tile(d, j), 0)),
        out_shape=jax.ShapeDtypeStruct((2, bsz, t, GLA_V), BF16),
        scratch_shapes=[pltpu.VMEM((TT, GLA_QK), BF16), pltpu.VMEM((TT, GLA_QK), F32),
                        pltpu.VMEM((TT, GLA_V), F32), pltpu.VMEM((GLA_HEADS * CPT, GLA_DV, GLA_DK), F32),
                        pltpu.VMEM((GLA_HEADS, GLA_DV, GLA_DK), F32)],
        compiler_params=pltpu.CompilerParams(
            dimension_semantics=("parallel", "parallel", "arbitrary")),
        name="gla_scan",
    )(proj, proj, proj, small, wg, bg)


def _dot_sel2(x, e):
    hi = x.astype(BF16)
    return _dot(hi, e) + _dot((x - hi.astype(F32)).astype(BF16), e)


def _ssd_kernel(xs_ref, bm_ref, cm_ref, sm_ref, dtb_ref, nega_ref, e_ref, o_ref,
                v_s, vw_s, cdec_s, dec_s, ah_s, st_s):
    d = pl.program_id(1)
    j = pl.program_id(2)
    is_fwd = d == 0
    gw = SSD_HPG * SSD_HEADDIM

    @pl.when(j == 0)
    def _():
        st_s[...] = jnp.zeros_like(st_s)

    dt = _softplus(sm_ref[0] + dtb_ref[0])
    la = dt * nega_ref[0]
    acum = _sel_dot(_chunk_cum_mask(is_fwd, TT), la)
    atot = _chunk_totals(acum, is_fwd)
    e = e_ref[0]
    v = xs_ref[0].astype(F32) * _dot_sel2(dt, e)
    v_s[...] = v.astype(BF16)
    vw_s[...] = (v * _dot_sel2(jnp.exp(atot - acum), e)).astype(BF16)
    cdec_s[...] = _dot_sel2(jnp.exp(acum), e)
    etot = jnp.exp(atot)
    tot_rows = [etot[ci * CHUNK:ci * CHUNK + 1, :] for ci in range(CPT)]
    dec_s[...] = _dot_sel2(jnp.concatenate(tot_rows + [jnp.zeros((SUBLANES - CPT, LANE), F32)], axis=0), e)
    ah_s[...] = pltpu.roll(acum, LANE - SM_DT - d * SSD_HEADS, 1)
    causal = _causal_mask(is_fwd, CHUNK)

    for ci in range(CPT):
        cidx = jnp.where(is_fwd, ci, CPT - 1 - ci)
        off = pl.multiple_of(cidx * CHUNK, CHUNK)
        rows = pl.ds(off, CHUNK)
        ah = ah_s[rows, :]
        aht = ah.T
        for g in range(SSD_GROUPS):
            gc = slice(g * gw, (g + 1) * gw)
            nc = slice(g * SSD_STATE, (g + 1) * SSD_STATE)
            bm = bm_ref[0, rows, nc]
            cm = cm_ref[0, rows, nc]
            st = st_s[g]
            cb = _dot_nt(cm, bm)
            y_inter = _dot(cm, st.astype(BF16)) * cdec_s[rows, gc]
            v_c = v_s[rows, gc]
            ys = []
            for hh in range(SSD_HPG):
                h = g * SSD_HPG + hh
                seg = jnp.exp(jnp.where(causal, ah[:, h:h + 1] - aht[h:h + 1, :], -1e30))
                ys.append(_dot((seg * cb).astype(BF16), v_c[:, hh * SSD_HEADDIM:(hh + 1) * SSD_HEADDIM]))
            o_ref[0, 0, rows, gc] = (jnp.concatenate(ys, axis=1) + y_inter).astype(BF16)
            st_s[g] = st * dec_s[pl.ds(cidx, 1), gc] + _dot(bm.astype(F32).T.astype(BF16), vw_s[rows, gc])


def ssd_scan(xbc, small, dtb, nega, e):
    bsz, t, _ = xbc.shape
    nt = t // TT
    gw = SSD_HPG * SSD_HEADDIM
    tile = lambda d, j: _scan_tile(d, j, nt)
    return pl.pallas_call(
        _ssd_kernel,
        grid=(bsz, 2, nt),
        in_specs=[
            pl.BlockSpec((1, TT, SSD_INNER), lambda i, d, j: (i, tile(d, j), 0)),
            pl.BlockSpec((1, TT, SSD_BC), lambda i, d, j: (i, tile(d, j), SSD_INNER // SSD_BC)),
            pl.BlockSpec((1, TT, SSD_BC), lambda i, d, j: (i, tile(d, j), SSD_INNER // SSD_BC + 1)),
            pl.BlockSpec((1, TT, LANE), lambda i, d, j: (i, tile(d, j), 0)),
            pl.BlockSpec((1, 1, LANE), lambda i, d, j: (d, 0, 0)),
            pl.BlockSpec((1, 1, LANE), lambda i, d, j: (d, 0, 0)),
            pl.BlockSpec((1, LANE, SSD_INNER), lambda i, d, j: (d, 0, 0)),
        ],
        out_specs=pl.BlockSpec((1, 1, TT, SSD_INNER), lambda i, d, j: (d, i, tile(d, j), 0)),
        out_shape=jax.ShapeDtypeStruct((2, bsz, t, SSD_INNER), BF16),
        scratch_shapes=[pltpu.VMEM((TT, SSD_INNER), BF16), pltpu.VMEM((TT, SSD_INNER), BF16),
                        pltpu.VMEM((TT, SSD_INNER), F32), pltpu.VMEM((SUBLANES, SSD_INNER), F32),
                        pltpu.VMEM((TT, LANE), F32), pltpu.VMEM((SSD_GROUPS, SSD_STATE, gw), F32)],
        compiler_params=pltpu.CompilerParams(
            dimension_semantics=("parallel", "parallel", "arbitrary")),
        name="ssd_scan",
    )(xbc, xbc, xbc, small, dtb, nega, e)


def _group_rmsnorm(x, width):
    parts = []
    for s in range(x.shape[1] // width):
        seg = x[:, s * width:(s + 1) * width]
        parts.append(seg * lax.rsqrt(jnp.mean(seg * seg, axis=-1, keepdims=True) + 1e-6))
    return jnp.concatenate(parts, axis=1)


def _ab_out_kernel(of_ref, ob_ref, yf_ref, yb_ref, r_ref, z_ref, xs_ref, gg_ref, dv_ref, sg_ref,
                   w_ref, b_ref, x_ref, gate_ref, g2_ref, sh2_ref, sc2_ref, rw_ref, o_ref, h_ref, aff_ref):
    o = (_group_rmsnorm(of_ref[0, 0].astype(F32) + ob_ref[0, 0].astype(F32), GLA_DV) * gg_ref[...]
         * _silu(r_ref[0].astype(F32)))
    y = ((yf_ref[0, 0].astype(F32) + yb_ref[0, 0].astype(F32) + dv_ref[...] * xs_ref[0].astype(F32))
         * _silu(z_ref[0].astype(F32)))
    y = _group_rmsnorm(y, SSD_INNER // SSD_GROUPS) * sg_ref[...]
    m = _dot(o.astype(BF16), w_ref[0:GLA_V, :]) + _dot(y.astype(BF16), w_ref[GLA_V:, :]) + b_ref[...]
    x_new = x_ref[0] + gate_ref[0, 0] * m
    o_ref[0] = x_new
    _route_tokens(x_new, g2_ref, sh2_ref, sc2_ref, rw_ref, h_ref, aff_ref)


def ab_out(o_gla, y_ssd, proj, xbc, gla_g, d_vec, ssd_g, w, b, x, gate, route):
    bsz, t, d = x.shape
    seg = lambda i, j: (i, jnp.minimum(j, 1), 0, 0)
    row = lambda width: pl.BlockSpec((1, width), lambda i, j: (0, 0))
    r_ins, r_outs, r_shapes = _route_specs(bsz, t, d, seg)
    g2, sh2, sc2, rw = route
    return pl.pallas_call(
        _ab_out_kernel,
        grid=(bsz, t // TT),
        in_specs=[
            pl.BlockSpec((1, 1, TT, GLA_V), lambda i, j: (0, i, j, 0)),
            pl.BlockSpec((1, 1, TT, GLA_V), lambda i, j: (1, i, j, 0)),
            pl.BlockSpec((1, 1, TT, SSD_INNER), lambda i, j: (0, i, j, 0)),
            pl.BlockSpec((1, 1, TT, SSD_INNER), lambda i, j: (1, i, j, 0)),
            pl.BlockSpec((1, TT, GLA_V), lambda i, j: (i, j, AB_R // GLA_V)),
            pl.BlockSpec((1, TT, SSD_INNER), lambda i, j: (i, j, AB_Z // SSD_INNER)),
            pl.BlockSpec((1, TT, SSD_INNER), lambda i, j: (i, j, 0)),
            row(GLA_V), row(SSD_INNER), row(SSD_INNER),
            pl.BlockSpec((GLA_V + SSD_INNER, d), lambda i, j: (0, 0)),
            row(d),
            pl.BlockSpec((1, TT, d), lambda i, j: (i, j, 0)),
            pl.BlockSpec((1, 1, 1, d), seg),
        ] + r_ins,
        out_specs=[pl.BlockSpec((1, TT, d), lambda i, j: (i, j, 0))] + r_outs,
        out_shape=[jax.ShapeDtypeStruct((bsz, t, d), F32)] + r_shapes,
        compiler_params=pltpu.CompilerParams(
            dimension_semantics=("parallel", "parallel"), vmem_limit_bytes=VMEM_LIMIT),
        name="ab_out",
    )(o_gla, o_gla, y_ssd, y_ssd, proj, proj, xbc, gla_g.reshape(1, -1), d_vec.reshape(1, -1),
      ssd_g.reshape(1, -1), w, b.reshape(1, d), x, gate, g2.reshape(1, d), sh2, sc2, rw)


def _cd_prep_kernel(q_ref, k_ref, v_ref, ql_ref, kl_ref, vl_ref, qr_ref, kr_ref, vr_ref, w_ref, o_ref):
    j = pl.program_id(1)
    nt = pl.num_programs(1)
    left_ok = j >= 2
    right_ok = jnp.logical_and(j >= 1, j < nt - 1)
    w = w_ref[...]
    srcs = ((q_ref, ql_ref, qr_ref, GDN_DK ** -0.5), (k_ref, kl_ref, kr_ref, 1.0), (v_ref, vl_ref, vr_ref, None))
    for s, (c_ref, l_ref, r_ref, scale) in enumerate(srcs):
        c0 = s * GDN_QK
        y = _conv3_piece(c_ref, l_ref, r_ref, w[:, c0:c0 + GDN_QK], 0.0, left_ok, right_ok)
        if scale is None:
            o_ref[0, :, c0:c0 + GDN_QK] = y.astype(BF16)
            continue
        for h in range(GDN_HEADS):
            seg = y[:, h * GDN_DK:(h + 1) * GDN_DK]
            inv = lax.rsqrt(jnp.sum(seg * seg, axis=-1, keepdims=True) + 1e-6) * scale
            o_ref[0, :, c0 + h * GDN_DK:c0 + (h + 1) * GDN_DK] = (seg * inv).astype(BF16)


def cd_prep(proj, conv_w):
    bsz, t, _ = proj.shape
    nt = t // TT
    rb = TT // HALO
    nrb = t // HALO
    width = GDN_QK
    cols = (2, 3, 4)

    cur = lambda cb: pl.BlockSpec((1, TT, width), lambda i, j: (i, j, cb))
    left = lambda cb: pl.BlockSpec((1, HALO, width), lambda i, j: (i, jnp.maximum(j * rb - 1, 0), cb))
    right = lambda cb: pl.BlockSpec((1, HALO, width), lambda i, j: (i, jnp.minimum((j + 1) * rb, nrb - 1), cb))
    return pl.pallas_call(
        _cd_prep_kernel,
        grid=(bsz, nt),
        in_specs=[cur(cb) for cb in cols] + [left(cb) for cb in cols] + [right(cb) for cb in cols] + [
            pl.BlockSpec((3, 3 * width), lambda i, j: (0, 0))],
        out_specs=pl.BlockSpec((1, TT, 3 * width), lambda i, j: (i, j, 0)),
        out_shape=jax.ShapeDtypeStruct((bsz, t, 3 * width), BF16),
        compiler_params=pltpu.CompilerParams(dimension_semantics=("parallel", "parallel")),
        name="cd_prep",
    )(*([proj] * 9), conv_w)


GDN_HB = 8


def _mm2(a, b):
    return _dot(a.astype(BF16), b.astype(BF16))


def _unit_tri_inverse(mats, b16, b32, eye):
    each = lambda f, *ls: [f(*xs) for xs in zip(*ls)]
    d16 = each(lambda a: jnp.where(b16, a, 0.0), mats)
    d2 = each(lambda x: _mm2(x, x), d16)
    d4 = each(lambda x: _mm2(x, x), d2)
    d8 = each(lambda x: _mm2(x, x), d4)
    t = each(lambda x: eye - x, d16)
    for p in (d2, d4, d8):
        t = each(lambda x, y: x + _mm2(x, y), t, p)
    off32 = jnp.logical_and(b32, jnp.logical_not(b16))
    for sel in (off32, jnp.logical_not(b32)):
        a_off = each(lambda a: jnp.where(sel, a, 0.0), mats)
        inner = each(_mm2, a_off, t)
        t = each(lambda x, y: x - _mm2(x, y), t, inner)
    return t


def _gdn_kernel(q_ref, k_ref, v_ref, sm_ref, dtb_ref, nega_ref, o_ref,
                dec_s, n_s, p_s, oc_s, qp_s, st_s):
    d = pl.program_id(2)
    j = pl.program_id(3)
    is_fwd = d == 0

    @pl.when(j == 0)
    def _():
        st_s[...] = jnp.zeros_like(st_s)

    sm = sm_ref[0]
    first = d * GDN_HEADS + pl.program_id(1) * GDN_HB
    rot = jnp.where(first == 0, 0, LANE - first)
    la = pltpu.roll(_softplus(sm + dtb_ref[...]) * nega_ref[...], rot, 1)
    be_sm = pltpu.roll(jax.nn.sigmoid(sm), rot, 1)
    gc_sm = _sel_dot(_chunk_cum_mask(is_fwd, TT), la)
    gl_sm = _chunk_totals(gc_sm, is_fwd)
    lane_bcast = lambda a, col: jnp.broadcast_to(a[:, col:col + 1], (TT, GDN_DK))

    r = lax.broadcasted_iota(jnp.int32, (CHUNK, CHUNK), 0)
    c = lax.broadcasted_iota(jnp.int32, (CHUNK, CHUNK), 1)
    causal = _causal_mask(is_fwd, CHUNK)
    strict = jnp.logical_and(causal, r != c)
    b16 = lax.shift_right_logical(r, 4) == lax.shift_right_logical(c, 4)
    b32 = lax.shift_right_logical(r, 5) == lax.shift_right_logical(c, 5)
    eye = jnp.where(r == c, 1.0, 0.0)
    chunk_rows = [slice(ci * CHUNK, (ci + 1) * CHUNK) for ci in range(CPT)]

    amats, rhss, aqks, kdts, qgs = [], [], [], [], []
    for hh in range(GDN_HB):
        cols = slice(hh * GDN_DK, (hh + 1) * GDN_DK)
        gc = lane_bcast(gc_sm, hh)
        gl = lane_bcast(gl_sm, hh)
        beta_e = lane_bcast(be_sm, 2 * GDN_HEADS + hh)
        q = q_ref[0, :, cols].astype(F32)
        k = k_ref[0, :, cols].astype(F32)
        egc = jnp.exp(gc)
        kb = k * beta_e
        qg = q * egc
        kd = k * jnp.exp(gl - gc)
        dec_s[:, cols] = jnp.exp(gl)
        rhs = jnp.concatenate([v_ref[0, :, cols].astype(F32) * beta_e, kb * egc], axis=1)
        for rows in chunk_rows:
            gcc = gc[rows, :]
            dmat = jnp.exp(jnp.where(causal, gcc[:, 0:CHUNK] - gcc.T[0:CHUNK, :], -1e30))
            kc = k[rows].astype(BF16)
            amats.append(jnp.where(strict, _dot_nt(kb[rows].astype(BF16), kc) * dmat, 0.0))
            rhss.append(rhs[rows])
            aqks.append((_dot_nt(q[rows].astype(BF16), kc) * dmat).astype(BF16))
            kdts.append(kd[rows].T.astype(BF16))
            qgs.append(qg[rows])
    tinv = _unit_tri_inverse(amats, b16, b32, eye)
    sols = [_mm2(t, rhs_c).astype(BF16) for t, rhs_c in zip(tinv, rhss)]
    for n, sol in enumerate(sols):
        hh, ci = n // CPT, n % CPT
        rows, cols = chunk_rows[ci], slice(hh * GDN_DK, (hh + 1) * GDN_DK)
        ks = _dot(kdts[n], sol)
        qs = _dot(aqks[n], sol)
        n_s[n] = ks[:, 0:GDN_DV]
        p_s[n] = ks[:, GDN_DV:].astype(BF16)
        oc_s[rows, cols] = qs[:, 0:GDN_DV]
        qp_s[rows, cols] = (qgs[n] - qs[:, GDN_DV:]).astype(BF16)

    for ci in range(CPT):
        cidx = jnp.where(is_fwd, ci, CPT - 1 - ci)
        off = pl.multiple_of(cidx * CHUNK, CHUNK)
        rows = pl.ds(off, CHUNK)
        for hh in range(GDN_HB):
            cols = slice(hh * GDN_DK, (hh + 1) * GDN_DK)
            st = st_s[hh]
            stb = st.astype(BF16)
            o_ref[0, 0, rows, cols] = (_dot(qp_s[rows, cols], stb) + oc_s[rows, cols]).astype(BF16)
            st_s[hh] = st * dec_s[pl.ds(off, 1), cols] - _dot(p_s[hh * CPT + cidx], stb) + n_s[hh * CPT + cidx]


def gdn_scan(qkv, small, dtb, nega):
    bsz, t, _ = qkv.shape
    nt = t // TT
    tile = lambda d, j: _scan_tile(d, j, nt)
    ng = GDN_HEADS // GDN_HB
    wb = GDN_HB * GDN_DK
    return pl.pallas_call(
        _gdn_kernel,
        grid=(bsz, ng, 2, nt),
        in_specs=[
            pl.BlockSpec((1, TT, wb), lambda i, h, d, j: (i, tile(d, j), h)),
            pl.BlockSpec((1, TT, wb), lambda i, h, d, j: (i, tile(d, j), ng + h)),
            pl.BlockSpec((1, TT, wb), lambda i, h, d, j: (i, tile(d, j), 2 * ng + h)),
            pl.BlockSpec((1, TT, LANE), lambda i, h, d, j: (i, tile(d, j), 0)),
            pl.BlockSpec((1, LANE), lambda i, h, d, j: (0, 0)),
            pl.BlockSpec((1, LANE), lambda i, h, d, j: (0, 0)),
        ],
        out_specs=pl.BlockSpec((1, 1, TT, wb), lambda i, h, d, j: (d, i, tile(d, j), h)),
        out_shape=jax.ShapeDtypeStruct((2, bsz, t, GDN_V), BF16),
        scratch_shapes=[pltpu.VMEM((TT, wb), F32),
                        pltpu.VMEM((GDN_HB * CPT, GDN_DK, GDN_DV), F32),
                        pltpu.VMEM((GDN_HB * CPT, GDN_DK, GDN_DK), BF16),
                        pltpu.VMEM((TT, wb), F32), pltpu.VMEM((TT, wb), BF16),
                        pltpu.VMEM((GDN_HB, GDN_DK, GDN_DV), F32)],
        compiler_params=pltpu.CompilerParams(
            dimension_semantics=("parallel", "parallel", "parallel", "arbitrary")),
        name="gdn_scan",
    )(qkv, qkv, qkv, small, dtb, nega)


CONF_PAD = 16


def _cd_out_kernel(ga_ref, gb_ref, og_ref, of_ref, ob_ref, cw_ref, cb_ref, lg_ref, lb_ref, ng_ref,
                   w_ref, b_ref, x_ref, gate_ref, g2_ref, sh2_ref, sc2_ref, rw_ref, o_ref, h_ref, aff_ref,
                   pad_s, shift_s, conv_s):
    j = pl.program_id(1)
    half = (CONF_KERNEL - 1) // 2
    glu = ga_ref[0].astype(F32) * jax.nn.sigmoid(gb_ref[0].astype(F32))
    zeros = jnp.zeros((CONF_PAD, CONF_CH), F32)

    def conv_segments(seglen):
        stride = seglen + 2 * CONF_PAD
        for g in range(TT // seglen):
            base = g * stride
            pad_s[base:base + CONF_PAD, :] = zeros
            pad_s[base + CONF_PAD:base + CONF_PAD + seglen, :] = glu[g * seglen:(g + 1) * seglen]
            pad_s[base + CONF_PAD + seglen:base + stride, :] = zeros
        nseg = TT // seglen
        used = nseg * stride
        conv_s[...] = jnp.zeros((TT, CONF_CH), F32) + cb_ref[...]
        for phase in range(SUBLANES):
            taps = [kk for kk in range(CONF_KERNEL) if (CONF_PAD - half + kk) % SUBLANES == phase]
            if phase:
                shift_s[0:used - SUBLANES, :] = pad_s[phase:used - SUBLANES + phase, :]
            src = shift_s if phase else pad_s
            for g in range(nseg):
                acc = conv_s[g * seglen:(g + 1) * seglen, :]
                for kk in taps:
                    lo = g * stride + CONF_PAD - half + kk - phase
                    acc = acc + src[lo:lo + seglen, :] * cw_ref[kk:kk + 1, :]
                conv_s[g * seglen:(g + 1) * seglen, :] = acc

    @pl.when(j == 0)
    def _():
        conv_segments(TT)

    @pl.when(j > 0)
    def _():
        conv_segments(GRID_W)

    acc = conv_s[...]
    mu = jnp.mean(acc, axis=-1, keepdims=True)
    cen = acc - mu
    var = jnp.mean(cen * cen, axis=-1, keepdims=True)
    conv = _silu(cen * lax.rsqrt(var + 1e-5) * lg_ref[...] + lb_ref[...])
    o = (_group_rmsnorm(of_ref[0, 0].astype(F32) + ob_ref[0, 0].astype(F32), GDN_DV) * ng_ref[...]
         * _silu(og_ref[0].astype(F32)))
    m = _dot(conv.astype(BF16), w_ref[0:CONF_CH, :]) + _dot(o.astype(BF16), w_ref[CONF_CH:, :]) + b_ref[...]
    x_new = x_ref[0] + gate_ref[0, 0] * m
    o_ref[0] = x_new
    _route_tokens(x_new, g2_ref, sh2_ref, sc2_ref, rw_ref, h_ref, aff_ref)


def cd_out(o_gdn, proj, conv_w, conv_b, ln_g, ln_b, norm_g, w, b, x, gate, route):
    bsz, t, d = x.shape
    seg = lambda i, j: (i, jnp.minimum(j, 1), 0, 0)
    row = lambda width: pl.BlockSpec((1, width), lambda i, j: (0, 0))
    r_ins, r_outs, r_shapes = _route_specs(bsz, t, d, seg)
    g2, sh2, sc2, rw = route
    return pl.pallas_call(
        _cd_out_kernel,
        grid=(bsz, t // TT),
        in_specs=[
            pl.BlockSpec((1, TT, CONF_CH), lambda i, j: (i, j, 0)),
            pl.BlockSpec((1, TT, CONF_CH), lambda i, j: (i, j, 1)),
            pl.BlockSpec((1, TT, GDN_V), lambda i, j: (i, j, 5)),
            pl.BlockSpec((1, 1, TT, GDN_V), lambda i, j: (0, i, j, 0)),
            pl.BlockSpec((1, 1, TT, GDN_V), lambda i, j: (1, i, j, 0)),
            pl.BlockSpec((CONF_KERNEL, CONF_CH), lambda i, j: (0, 0)),
            row(CONF_CH), row(CONF_CH), row(CONF_CH), row(GDN_V),
            pl.BlockSpec((CONF_CH + GDN_V, d), lambda i, j: (0, 0)),
            row(d),
            pl.BlockSpec((1, TT, d), lambda i, j: (i, j, 0)),
            pl.BlockSpec((1, 1, 1, d), seg),
        ] + r_ins,
        out_specs=[pl.BlockSpec((1, TT, d), lambda i, j: (i, j, 0))] + r_outs,
        out_shape=[jax.ShapeDtypeStruct((bsz, t, d), F32)] + r_shapes,
        scratch_shapes=[pltpu.VMEM(((TT // GRID_W) * (GRID_W + 2 * CONF_PAD), CONF_CH), F32),
                        pltpu.VMEM(((TT // GRID_W) * (GRID_W + 2 * CONF_PAD), CONF_CH), F32),
                        pltpu.VMEM((TT, CONF_CH), F32)],
        compiler_params=pltpu.CompilerParams(
            dimension_semantics=("parallel", "parallel"), vmem_limit_bytes=VMEM_LIMIT),
        name="cd_out",
    )(proj, proj, proj, o_gdn, o_gdn, conv_w, conv_b.reshape(1, -1), ln_g.reshape(1, -1), ln_b.reshape(1, -1),
      norm_g.reshape(1, -1), w, b.reshape(1, d), x, gate, g2.reshape(1, d), sh2, sc2, rw)


def _route_tokens(x, g_ref, sh_ref, sc_ref, rw_ref, h_ref, aff_ref):
    ms = jnp.mean(x * x, axis=-1, keepdims=True)
    h = (x * lax.rsqrt(ms + 1e-6) * g_ref[...] * (1.0 + sc_ref[0, 0]) + sh_ref[0, 0]).astype(BF16)
    h_ref[0] = h
    logits = _dot(h, rw_ref[...])
    lane = lax.broadcasted_iota(jnp.int32, logits.shape, 1)
    logits = jnp.where(lane < N_EXPERTS, logits, -1e30)
    e = jnp.exp(logits - jnp.max(logits, axis=-1, keepdims=True))
    aff = e / jnp.sum(e, axis=-1, keepdims=True)
    aff_ref[0] = aff.T[0:N_EXPERTS, :]


def _route_specs(bsz, t, d, seg_index):
    ins = [pl.BlockSpec((1, d), lambda i, j: (0, 0)),
           pl.BlockSpec((1, 1, 1, d), seg_index),
           pl.BlockSpec((1, 1, 1, d), seg_index),
           pl.BlockSpec((d, LANE), lambda i, j: (0, 0))]
    outs = [pl.BlockSpec((1, TT, d), lambda i, j: (i, j, 0)),
            pl.BlockSpec((1, N_EXPERTS, TT), lambda i, j: (i, 0, j))]
    shapes = [jax.ShapeDtypeStruct((bsz, t, d), BF16), jax.ShapeDtypeStruct((bsz, N_EXPERTS, t), F32)]
    return ins, outs, shapes


def _lane_block_prefix(x, u_strict):
    nblk = x.shape[1] // LANE
    run = jnp.zeros((x.shape[0], 1), F32)
    outs = []
    for cblk in range(nblk):
        xc = x[:, cblk * LANE:(cblk + 1) * LANE]
        outs.append(_dot(xc.astype(BF16), u_strict) + run)
        run = run + jnp.sum(xc, axis=-1, keepdims=True)
    return jnp.concatenate(outs, axis=1), run


def _select_kernel(aff_ref, slot_ref, *, cap):
    aff = aff_ref[0]
    bits = pltpu.bitcast(aff, jnp.int32)
    capf = jnp.float32(cap)

    def step(i, thr):
        cand = jnp.bitwise_or(thr, lax.shift_left(jnp.int32(1), 30 - i))
        cnt = jnp.sum(jnp.where(bits >= cand, 1.0, 0.0), axis=-1, keepdims=True)
        return jnp.where(cnt >= capf, cand, thr)

    thr = lax.fori_loop(0, 31, step, jnp.zeros((aff.shape[0], 1), jnp.int32))
    gt = jnp.where(bits > thr, 1.0, 0.0)
    eq = jnp.where(bits == thr, 1.0, 0.0)
    r = lax.broadcasted_iota(jnp.int32, (LANE, LANE), 0)
    c = lax.broadcasted_iota(jnp.int32, (LANE, LANE), 1)
    u_strict = jnp.where(r < c, 1.0, 0.0).astype(BF16)
    need = capf - jnp.sum(gt, axis=-1, keepdims=True)
    eq_rank, _ = _lane_block_prefix(eq, u_strict)
    sel = jnp.maximum(gt, jnp.where(eq_rank < need, eq, 0.0))
    slot, _ = _lane_block_prefix(sel, u_strict)
    slot_ref[0] = jnp.where(sel > 0.0, slot.astype(jnp.int32), -1)


def moe_select(aff, cap):
    bsz, ne, n = aff.shape
    return pl.pallas_call(
        functools.partial(_select_kernel, cap=cap),
        grid=(bsz,),
        in_specs=[pl.BlockSpec((1, ne, n), lambda i: (i, 0, 0))],
        out_specs=pl.BlockSpec((1, ne, n), lambda i: (i, 0, 0)),
        out_shape=jax.ShapeDtypeStruct((bsz, ne, n), jnp.int32),
        compiler_params=pltpu.CompilerParams(dimension_semantics=("parallel",)),
        name="moe_select",
    )(aff)


def _slot_index_kernel(slot_ref, idx_ref, *, cap):
    slot = slot_ref[0]
    n = slot.shape[1]
    srow = lax.broadcasted_iota(jnp.int32, (cap, LANE), 0)
    lane = lax.broadcasted_iota(jnp.int32, (cap, LANE), 1)
    acc = jnp.zeros((cap, LANE), jnp.int32)
    for cblk in range(n // LANE):
        s_c = slot[:, cblk * LANE:(cblk + 1) * LANE]
        acc = acc + jnp.where(srow == s_c, lane + (cblk * LANE + 1), 0)
    ones = jnp.ones((8, LANE), BF16)
    hi = _dot_nt(ones, lax.shift_right_logical(acc, 7).astype(F32).astype(BF16))
    lo = _dot_nt(ones, jnp.bitwise_and(acc, LANE - 1).astype(F32).astype(BF16))
    idx_ref[0] = (hi[0:1, :] * float(LANE) + lo[0:1, :]).astype(jnp.int32) - 1


SLOT_WIN = 2 * LANE


def _slot_index_win_kernel(base_ref, slot_ref, idx_ref, acc_s, *, cap):
    i = pl.program_id(0)
    n = slot_ref.shape[2]
    acc_s[...] = jnp.zeros(acc_s.shape, jnp.int32)
    srow = lax.broadcasted_iota(jnp.int32, (SLOT_WIN, LANE), 0)
    lane = lax.broadcasted_iota(jnp.int32, (SLOT_WIN, LANE), 1)
    for cblk in range(n // LANE):
        base = pl.multiple_of(base_ref[i, cblk], LANE)
        s_c = slot_ref[0, :, cblk * LANE:(cblk + 1) * LANE]
        rows = pl.ds(base, SLOT_WIN)
        acc_s[rows, :] = acc_s[rows, :] + jnp.where(srow + base == s_c, lane + (cblk * LANE + 1), 0)
    acc = acc_s[0:cap, :]
    ones = jnp.ones((8, LANE), BF16)
    hi = _dot_nt(ones, lax.shift_right_logical(acc, 7).astype(F32).astype(BF16))
    lo = _dot_nt(ones, jnp.bitwise_and(acc, LANE - 1).astype(F32).astype(BF16))
    idx_ref[0] = (hi[0:1, :] * float(LANE) + lo[0:1, :]).astype(jnp.int32) - 1


def moe_slot_index_windowed(slot, base, cap):
    bsz, ne, n = slot.shape
    idx = pl.pallas_call(
        functools.partial(_slot_index_win_kernel, cap=cap),
        grid_spec=pltpu.PrefetchScalarGridSpec(
            num_scalar_prefetch=1,
            grid=(bsz * ne,),
            in_specs=[pl.BlockSpec((1, 1, n), lambda i, base_ref: (i, 0, 0))],
            out_specs=pl.BlockSpec((1, 1, cap), lambda i, base_ref: (i, 0, 0)),
            scratch_shapes=[pltpu.VMEM((cap + LANE, LANE), jnp.int32)],
        ),
        out_shape=jax.ShapeDtypeStruct((bsz * ne, 1, cap), jnp.int32),
        compiler_params=pltpu.CompilerParams(dimension_semantics=("arbitrary",)),
        name="moe_slot_index",
    )(base, slot.reshape(bsz * ne, 1, n))
    return idx.reshape(bsz, ne, cap)


def moe_slot_index(slot, cap):
    bsz, ne, n = slot.shape
    idx = pl.pallas_call(
        functools.partial(_slot_index_kernel, cap=cap),
        grid=(bsz * ne,),
        in_specs=[pl.BlockSpec((1, 1, n), lambda i: (i, 0, 0))],
        out_specs=pl.BlockSpec((1, 1, cap), lambda i: (i, 0, 0)),
        out_shape=jax.ShapeDtypeStruct((bsz * ne, 1, cap), jnp.int32),
        compiler_params=pltpu.CompilerParams(dimension_semantics=("parallel",)),
        name="moe_slot_index",
    )(slot.reshape(bsz * ne, 1, n))
    return idx.reshape(bsz, ne, cap)


WIN_ALIGN = 16
WIN_FAST = 128


def _combine_kernel(ws_ref, slot_ref, aff_ref, *rest, win, final):
    ye_refs, (x_ref, gate_ref), o_ref = rest[:N_EXPERTS], rest[N_EXPERTS:N_EXPERTS + 2], rest[-1]
    b = pl.program_id(0)
    j = pl.program_id(1)
    srow = lax.broadcasted_iota(jnp.int32, (win, TT), 0)
    his, los = [], []
    for e in range(N_EXPERTS):
        sel = jnp.where(srow + ws_ref[b, e, j] == slot_ref[0, e:e + 1, :], aff_ref[0, e:e + 1, :], 0.0).T
        hi = sel.astype(BF16)
        his.append(hi)
        los.append((sel - hi.astype(F32)).astype(BF16))
    ye = jnp.concatenate([r[...] for r in ye_refs], axis=0)
    acc = _dot(jnp.concatenate(his, axis=1), ye) + _dot(jnp.concatenate(los, axis=1), ye)
    y = x_ref[0] + gate_ref[0, 0] * acc
    if final:
        g_ref = rest[N_EXPERTS + 2]
        y = y * lax.rsqrt(jnp.mean(y * y, axis=-1, keepdims=True) + 1e-6) * g_ref[...]
    o_ref[0] = y


def moe_combine(ws, slot, aff, ye, xa, gate, seg, tile0, win, final_g=None):
    bsz, ne, n = slot.shape
    nt = n // TT
    d = xa.shape[2]

    def ye_spec(e):
        return pl.BlockSpec((pl.Squeezed(), pl.Squeezed(), pl.Element(win), pl.Element(d)),
                            lambda i, j, ws_ref: (i, e, pl.multiple_of(ws_ref[i, e, j], WIN_ALIGN), 0))

    in_specs = ([pl.BlockSpec((1, ne, TT), lambda i, j, ws_ref: (i, 0, j)),
                 pl.BlockSpec((1, ne, TT), lambda i, j, ws_ref: (i, 0, j))]
                + [ye_spec(e) for e in range(ne)]
                + [pl.BlockSpec((1, TT, d), lambda i, j, ws_ref: (i, j + tile0, 0)),
                   pl.BlockSpec((1, 1, 1, d), lambda i, j, ws_ref: (i, seg, 0, 0))])
    args = (ws, slot, aff, *([ye] * ne), xa, gate)
    if final_g is None:
        out_idx, out_shape, aliases = (lambda i, j, ws_ref: (i, j + tile0, 0)), xa.shape, {3 + ne: 0}
    else:
        out_idx, out_shape, aliases = (lambda i, j, ws_ref: (i, j, 0)), (bsz, n, d), {}
        in_specs.append(pl.BlockSpec((1, d), lambda i, j, ws_ref: (0, 0)))
        args += (final_g.reshape(1, d),)
    return pl.pallas_call(
        functools.partial(_combine_kernel, win=win, final=final_g is not None),
        grid_spec=pltpu.PrefetchScalarGridSpec(
            num_scalar_prefetch=1,
            grid=(bsz, nt),
            in_specs=in_specs,
            out_specs=pl.BlockSpec((1, TT, d), out_idx),
        ),
        out_shape=jax.ShapeDtypeStruct(out_shape, F32),
        input_output_aliases=aliases,
        compiler_params=pltpu.CompilerParams(
            dimension_semantics=("parallel", "parallel"), vmem_limit_bytes=VMEM_LIMIT),
        name="moe_combine",
    )(*args)


def _expert_ffn_kernel(x_ref, w1_ref, w3_ref, w2_ref, o_ref, w1_s, w3_s, w2_s):
    @pl.when(jnp.logical_and(pl.program_id(1) == 0, pl.program_id(2) == 0))
    def _():
        w1_s[...] = w1_ref[0, 0].astype(BF16)
        w3_s[...] = w3_ref[0, 0].astype(BF16)
        w2_s[...] = w2_ref[0, 0].astype(BF16)

    x = x_ref[0, 0]
    a = _dot(x, w1_s[...])
    g = _dot(x, w3_s[...])
    o_ref[0, 0] = _dot((_silu(a) * g).astype(BF16), w2_s[...]).astype(BF16)


def expert_ffn(xe, w1, w3, w2, layer):
    bsz, ne, cap, d = xe.shape
    f = w1.shape[3]
    tm = cap // max(1, cap // 512)
    assert cap % tm == 0 and tm % WIN_ALIGN == 0
    return pl.pallas_call(
        _expert_ffn_kernel,
        grid=(ne, bsz, cap // tm),
        in_specs=[
            pl.BlockSpec((1, 1, tm, d), lambda e, i, j: (i, e, j, 0)),
            pl.BlockSpec((1, 1, d, f), lambda e, i, j: (layer, e, 0, 0)),
            pl.BlockSpec((1, 1, d, f), lambda e, i, j: (layer, e, 0, 0)),
            pl.BlockSpec((1, 1, f, d), lambda e, i, j: (layer, e, 0, 0)),
        ],
        out_specs=pl.BlockSpec((1, 1, tm, d), lambda e, i, j: (i, e, j, 0)),
        out_shape=jax.ShapeDtypeStruct((bsz, ne, cap, d), BF16),
        scratch_shapes=[pltpu.VMEM((d, f), BF16), pltpu.VMEM((d, f), BF16), pltpu.VMEM((f, d), BF16)],
        compiler_params=pltpu.CompilerParams(
            dimension_semantics=("arbitrary", "arbitrary", "arbitrary"), vmem_limit_bytes=VMEM_LIMIT),
        name="expert_ffn",
    )(xe, w1, w3, w2)


def _ab_in_layout(w_in, b_in):
    q, k, v, r, glr, z, xs, bm, cm, dt = _split_cols(
        jnp.concatenate([w_in, b_in[None]], axis=0),
        (GLA_QK, GLA_QK, GLA_V, GLA_V, 2 * GLA_GATE_RANK, SSD_INNER, SSD_INNER, SSD_BC, SSD_BC, 2 * SSD_HEADS))
    wide = jnp.concatenate([q, k, v, r, z, xs, bm, cm], axis=1)
    narrow = _pad_cols(jnp.concatenate([glr, dt], axis=1), LANE)
    return wide[:-1].astype(BF16), wide[-1], narrow[:-1].astype(BF16), narrow[-1]


def _gla_gate_params(w_gate2, b_gate2):
    wg = jnp.zeros((2, LANE, GLA_QK), F32)
    for d in range(2):
        wg = wg.at[d, d * GLA_GATE_RANK:(d + 1) * GLA_GATE_RANK, :].set(w_gate2[d])
    return wg.astype(BF16), b_gate2.reshape(2, 1, GLA_QK)


def _ssd_params(dt_bias, a_log):
    dtb = jnp.zeros((2, 1, LANE), F32)
    nega = jnp.zeros((2, 1, LANE), F32)
    e = np.zeros((2, LANE, SSD_INNER), np.float32)
    for d in range(2):
        c0 = SM_DT + d * SSD_HEADS
        dtb = dtb.at[d, 0, c0:c0 + SSD_HEADS].set(dt_bias[d])
        nega = nega.at[d, 0, c0:c0 + SSD_HEADS].set(-jnp.exp(a_log[d]))
        for h in range(SSD_HEADS):
            e[d, c0 + h, h * SSD_HEADDIM:(h + 1) * SSD_HEADDIM] = 1.0
    return dtb, nega, jnp.asarray(e, BF16)


def _gdn_params(dt_bias, a_log):
    n = 2 * GDN_HEADS
    dtb = jnp.zeros((1, LANE), F32).at[0, 0:n].set(dt_bias.reshape(n))
    nega = jnp.zeros((1, LANE), F32).at[0, 0:n].set(-jnp.exp(a_log.reshape(n)))
    return dtb, nega


def _moe_route(aff_all, tile0, ntiles):
    bsz = aff_all.shape[0]
    n = ntiles * TT
    cap = n * EC_CAPACITY // N_EXPERTS
    aff = aff_all[:, :, tile0 * TT:tile0 * TT + n]
    slot = moe_select(aff, cap)
    cnt128 = jnp.sum((slot >= 0).reshape(bsz, N_EXPERTS, n // LANE, LANE), axis=-1, dtype=jnp.int32)
    start128 = jnp.cumsum(cnt128, axis=-1) - cnt128
    if cap >= SLOT_WIN:
        base = jnp.minimum(start128 // LANE * LANE, cap - LANE).reshape(bsz * N_EXPERTS, n // LANE)
        idx = moe_slot_index_windowed(slot, base, cap)
    else:
        idx = moe_slot_index(slot, cap)
    per_tile = TT // LANE
    counts = jnp.sum(cnt128.reshape(bsz, N_EXPERTS, ntiles, per_tile), axis=-1)
    return dict(aff=aff, slot=slot, idx=idx + tile0 * TT, cap=cap, counts=counts, starts=start128[:, :, ::per_tile])


def moe_layer(i, xa, h_all, aff_all, mods, w1, w3, w2, segments, final_g=None):
    bsz, _, d = xa.shape
    routes = [_moe_route(aff_all, tile0, ntiles) for _, tile0, ntiles in segments]
    cap_all = sum(r["cap"] for r in routes)
    idx = jnp.concatenate([r["idx"] for r in routes], axis=2).reshape(bsz, N_EXPERTS * cap_all)
    xe = jnp.take_along_axis(h_all, idx[..., None], axis=1, mode="promise_in_bounds")
    ye = expert_ffn(xe.reshape(bsz, N_EXPERTS, cap_all, d), w1, w3, w2, i)
    row0 = 0
    for (seg, tile0, _), r in zip(segments, routes):
        cap = r["cap"]
        aligned = r["starts"] // WIN_ALIGN * WIN_ALIGN

        slot = r["slot"] if row0 == 0 else jnp.where(r["slot"] >= 0, r["slot"] + row0, -1)

        def run(win, r=r, cap=cap, aligned=aligned, row0=row0, seg=seg, tile0=tile0, xa=xa, slot=slot):
            ws = jnp.minimum(aligned, cap - win) + row0
            return moe_combine(ws, slot, r["aff"], ye, xa, mods[:, :, 5], seg, tile0, win, final_g)

        win_fast, win_full = min(cap, WIN_FAST), min(cap, TT + WIN_ALIGN)
        if win_fast == win_full:
            xa = run(win_full)
        else:
            overflow = jnp.any(r["starts"] + r["counts"] - jnp.minimum(aligned, cap - win_fast) > win_fast)
            xa = lax.cond(overflow, functools.partial(run, win_full), functools.partial(run, win_fast))
        row0 += cap
    return xa


def layer_mixer(i, j, xa, mods, p, last):
    sh1, sc1, g1 = (mods[:, :, s] for s in range(3))
    route = (p["norm2_g"][i], mods[:, :, 3], mods[:, :, 4], _pad_cols(p["moe_router"][i], LANE).astype(BF16))
    if i % 2 == 0:
        proj, small = norm_proj(xa, p["norm1_g"][i], sh1, sc1, *_ab_in_layout(p["ab_w_in"][j], p["ab_b_in"][j]),
                                AB_WIDE // 2)
        xbc = ab_prep(proj, p["ssd_conv_w"][j], p["ssd_conv_b"][j])
        wg, bg = _gla_gate_params(p["gla_w_gate2"][j], p["gla_b_gate2"][j])
        o_gla = gla_scan(proj, small, wg, bg)
        y_ssd = ssd_scan(xbc, small, *_ssd_params(p["ssd_dt_bias"][j], p["ssd_a_log"][j]))
        return ab_out(o_gla, y_ssd, proj, xbc, p["gla_norm_g"][j], jnp.repeat(p["ssd_d"][j], SSD_HEADDIM),
                      p["ssd_norm_g"][j], p["ab_w_out"][j].astype(BF16), p["ab_b_out"][j], xa, g1, route)
    w_in, b_in = p["cd_w_in"][j], p["cd_b_in"][j]
    proj, small = norm_proj(xa, p["norm1_g"][i], sh1, sc1, w_in[:, :CD_WIDE].astype(BF16), b_in[:CD_WIDE],
                            _pad_cols(w_in[:, CD_WIDE:], LANE).astype(BF16), _pad_cols(b_in[CD_WIDE:], LANE),
                            CD_WIDE // 2)
    qkv = cd_prep(proj, p["gdn_conv_w"][j])
    o_gdn = gdn_scan(qkv, small, *_gdn_params(p["gdn_dt_bias"][j], p["gdn_a_log"][j]))
    return cd_out(o_gdn, proj, p["conf_dw_w"][j], p["conf_dw_b"][j], p["conf_ln_g"][j], p["conf_ln_b"][j],
                  p["gdn_norm_g"][j], p["cd_w_out"][j].astype(BF16), p["cd_b_out"][j], xa, g1, route)


def kernel(x, c, ctx, c_ctx, mod_w, mod_b, norm1_g, norm2_g, ab_w_in, ab_b_in, ab_w_out, ab_b_out, gla_w_gate2, gla_b_gate2, gla_norm_g, ssd_conv_w, ssd_conv_b, ssd_dt_bias, ssd_a_log, ssd_d, ssd_norm_g, cd_w_in, cd_b_in, cd_w_out, cd_b_out, conf_dw_w, conf_dw_b, conf_ln_g, conf_ln_b, gdn_conv_w, gdn_a_log, gdn_dt_bias, gdn_norm_g, moe_router, moe_w1, moe_w3, moe_w2, final_norm_g):
    p = dict(locals())
    bsz, seq, d = x.shape
    assert ctx.shape[1] == TT and seq % TT == 0
    depth = mod_w.shape[0]
    xa = jnp.concatenate([ctx, x], axis=1)
    w1, w3, w2 = moe_w1, moe_w3, moe_w2
    cond = jnp.concatenate([c, c_ctx[None]], axis=0)
    for i in range(depth):
        last = i == depth - 1
        mod = mod_proj(cond, mod_w[i].astype(BF16), mod_b[i])
        mods = jnp.stack([jnp.broadcast_to(mod[bsz], (bsz, 6 * d)), mod[:bsz]], axis=1).reshape(bsz, 2, 6, 1, d)
        xa, h_all, aff_all = layer_mixer(i, i // 2, xa, mods, p, last)
        latent, context = (1, 1, seq // TT), (0, 0, 1)
        if last:
            return moe_layer(i, xa, h_all, aff_all, mods, w1, w3, w2, [latent], final_norm_g)
        xa = moe_layer(i, xa, h_all, aff_all, mods, w1, w3, w2, [latent, context])
```
